```python
import jax, jax.numpy as jnp
from jax import lax
import numpy as np

D_MODEL = 1024
BATCH = 8
SEQ = 8192
DEPTH = 2

GRID_W = 64
CTX_LEN = 256
EPS = 1e-6
D_FF = 2816
LRU_WIDTH = 512
LRU_HEADS = 8
LRU_HEAD_DIM = LRU_WIDTH // LRU_HEADS
CONV_WIDTH = 4
CONV_LEFT = 2
RG_C = 8.0
MLP_GROUPS = 4
MLP_GROUP_DIM = 128
MLP_WIDTH = MLP_GROUPS * MLP_GROUP_DIM
CHUNK = 128
ROWS_PER_CHUNK = CHUNK // GRID_W
MIX_WIDTH = LRU_WIDTH + MLP_WIDTH
IN_PROJ_WIDTH = 2 * LRU_WIDTH + 2 * MLP_WIDTH
N_MOD = 9

kernel_name = "hybrid_rglru_chunkmlp_macaron_dit"


def rmsnorm(x, g):
    x32 = x.astype(jnp.float32)
    y = x32 * lax.rsqrt(jnp.mean(x32 * x32, axis=-1, keepdims=True) + EPS)
    return (y * g.astype(jnp.float32)).astype(x.dtype)


def layernorm(x, g):
    x32 = x.astype(jnp.float32)
    mu = jnp.mean(x32, axis=-1, keepdims=True)
    xc = x32 - mu
    y = xc * lax.rsqrt(jnp.mean(xc * xc, axis=-1, keepdims=True) + EPS)
    return (y * g.astype(jnp.float32)).astype(x.dtype)


def modulate(z, shift, scale):
    return z * (1.0 + scale) + shift


def swiglu(z, w_in, w_out):
    gate, up = jnp.split(z @ w_in, 2, axis=-1)
    return (jax.nn.silu(gate) * up) @ w_out


def short_conv(x, w, b):
    n = x.shape[1]
    xp = jnp.pad(x, ((0, 0), (CONV_LEFT, CONV_WIDTH - 1 - CONV_LEFT), (0, 0)))
    y = b
    for k in range(CONV_WIDTH):
        y = y + xp[:, k:k + n] * w[k]
    return y


def rglru_coeffs(xc, w_r, b_r, w_i, b_i, lam):
    bsz, n, _ = xc.shape
    xh = xc.reshape(bsz, n, LRU_HEADS, LRU_HEAD_DIM)
    r = jax.nn.sigmoid(jnp.einsum("bnhd,hde->bnhe", xh, w_r).reshape(bsz, n, LRU_WIDTH) + b_r)
    i = jax.nn.sigmoid(jnp.einsum("bnhd,hde->bnhe", xh, w_i).reshape(bsz, n, LRU_WIDTH) + b_i)
    log_a = (-RG_C * r * jax.nn.softplus(-lam)).astype(jnp.float32)
    a = jnp.exp(log_a)
    b = jnp.sqrt(-jnp.expm1(2.0 * log_a)) * (i * xc).astype(jnp.float32)
    return a, b


def _combine(left, right):
    a_l, b_l = left
    a_r, b_r = right
    return a_l * a_r, a_r * b_l + b_r


def linear_scan(a, b, h0, reverse):
    a_cum, h = lax.associative_scan(_combine, (a, b), reverse=reverse, axis=1)
    return h + a_cum * h0[:, None, :]


def lru_scans(xl, conv_w, conv_b, w_r, b_r, w_i, b_i, lam, h0_f, h0_b):
    xc = short_conv(xl, conv_w, conv_b)
    a_f, b_f = rglru_coeffs(xc, w_r[0], b_r[0], w_i[0], b_i[0], lam[0])
    h_f = linear_scan(a_f, b_f, h0_f, reverse=False)
    a_b, b_b = rglru_coeffs(xc, w_r[1], b_r[1], w_i[1], b_i[1], lam[1])
    h_b = linear_scan(a_b, b_b, h0_b, reverse=True)
    return h_f, h_b


def spatial_gating(u, v, norm_g, w_s, b_s, n_chunks):
    bsz = v.shape[0]
    v = layernorm(v, norm_g)
    vc = v.reshape(bsz, n_chunks, CHUNK, MLP_GROUPS, MLP_GROUP_DIM)
    z = jnp.einsum("gpq,bnqgc->bnpgc", w_s, vc) + b_s.T[None, None, :, :, None]
    return u * z.reshape(u.shape)


def mix_out(p, h_f, h_b, sgu_norm_g, sgu_w, sgu_b, w_out, n_chunks):
    gl = p[..., LRU_WIDTH:2 * LRU_WIDTH]
    u = p[..., 2 * LRU_WIDTH:2 * LRU_WIDTH + MLP_WIDTH]
    v = p[..., 2 * LRU_WIDTH + MLP_WIDTH:]
    y_lru = (h_f + h_b).astype(p.dtype) * jax.nn.gelu(gl)
    y_sgu = spatial_gating(u, v, sgu_norm_g, sgu_w, sgu_b, n_chunks)
    return jnp.concatenate([y_lru, y_sgu], axis=-1) @ w_out


def _fwd_setup_inputs(seed: int = 0) -> dict:
    key = jax.random.key(seed)
    ks = jax.random.split(key, 32)
    f32 = jnp.float32

    def nrm(k, shape, s):
        return jax.random.normal(k, shape, f32) * s

    def gain(k, shape):
        return 1.0 + 0.02 * jax.random.normal(k, shape, f32)

    u = jax.random.uniform(ks[20], (DEPTH, 2, LRU_WIDTH), f32, minval=0.9, maxval=0.999)
    a0 = u ** (1.0 / RG_C)
    lru_lambda = jnp.log(a0) - jnp.log1p(-a0)

    return {
        "x": nrm(ks[0], (BATCH, SEQ, D_MODEL), 1.0),
        "c": nrm(ks[1], (BATCH, D_MODEL), 1.0),
        "ctx": nrm(ks[2], (BATCH, CTX_LEN, D_MODEL), 1.0),
        "c_ctx": nrm(ks[3], (D_MODEL,), 1.0),
        "w_ada": nrm(ks[4], (DEPTH, D_MODEL, N_MOD * D_MODEL), 0.5 * D_MODEL ** -0.5),
        "b_ada": nrm(ks[5], (DEPTH, N_MOD * D_MODEL), 0.02),
        "ffn1_norm_g": gain(ks[6], (DEPTH, D_MODEL)),
        "ffn1_w_in": nrm(ks[7], (DEPTH, D_MODEL, 2 * D_FF), D_MODEL ** -0.5),
        "ffn1_w_out": nrm(ks[8], (DEPTH, D_FF, D_MODEL), D_FF ** -0.5),
        "mix_norm_g": gain(ks[9], (DEPTH, D_MODEL)),
        "w_in_mix": nrm(ks[10], (DEPTH, D_MODEL, IN_PROJ_WIDTH), D_MODEL ** -0.5),
        "lru_conv_w": nrm(ks[11], (DEPTH, CONV_WIDTH, LRU_WIDTH), CONV_WIDTH ** -0.5),
        "lru_conv_b": nrm(ks[12], (DEPTH, LRU_WIDTH), 0.02),
        "lru_w_r": nrm(ks[13], (DEPTH, 2, LRU_HEADS, LRU_HEAD_DIM, LRU_HEAD_DIM), LRU_HEAD_DIM ** -0.5),
        "lru_b_r": nrm(ks[14], (DEPTH, 2, LRU_WIDTH), 0.02),
        "lru_w_i": nrm(ks[15], (DEPTH, 2, LRU_HEADS, LRU_HEAD_DIM, LRU_HEAD_DIM), LRU_HEAD_DIM ** -0.5),
        "lru_b_i": nrm(ks[16], (DEPTH, 2, LRU_WIDTH), 0.02),
        "lru_lambda": lru_lambda,
        "sgu_norm_g": gain(ks[17], (DEPTH, MLP_WIDTH)),
        "sgu_w": nrm(ks[18], (DEPTH, MLP_GROUPS, CHUNK, CHUNK), CHUNK ** -0.5),
        "sgu_b": nrm(ks[19], (DEPTH, MLP_GROUPS, CHUNK), 0.02),
        "w_out_mix": nrm(ks[21], (DEPTH, MIX_WIDTH, D_MODEL), MIX_WIDTH ** -0.5),
        "ffn2_norm_g": gain(ks[22], (DEPTH, D_MODEL)),
        "ffn2_w_in": nrm(ks[23], (DEPTH, D_MODEL, 2 * D_FF), D_MODEL ** -0.5),
        "ffn2_w_out": nrm(ks[24], (DEPTH, D_FF, D_MODEL), D_FF ** -0.5),
        "final_norm_g": gain(ks[25], (D_MODEL,)),
    }


def _fwd_reference(x, c, ctx, c_ctx, w_ada, b_ada, ffn1_norm_g, ffn1_w_in, ffn1_w_out,
              mix_norm_g, w_in_mix, lru_conv_w, lru_conv_b, lru_w_r, lru_b_r, lru_w_i,
              lru_b_i, lru_lambda, sgu_norm_g, sgu_w, sgu_b, w_out_mix,
              ffn2_norm_g, ffn2_w_in, ffn2_w_out, final_norm_g):
    bsz, n_lat, _ = x.shape
    rows = n_lat // GRID_W
    n_chunks_lat = rows // ROWS_PER_CHUNK
    n_chunks_ctx = CTX_LEN // CHUNK
    zeros_state = jnp.zeros((bsz, LRU_WIDTH), jnp.float32)

    sc = jax.nn.silu(c)
    scc = jax.nn.silu(c_ctx)[None, :]
    h, hc = x, ctx
    for l in range(DEPTH):
        last = l == DEPTH - 1
        m = jnp.split((sc @ w_ada[l] + b_ada[l])[:, None, :], N_MOD, axis=-1)
        mc = jnp.split((scc @ w_ada[l] + b_ada[l])[:, None, :], N_MOD, axis=-1)

        h = h + 0.5 * m[2] * swiglu(modulate(rmsnorm(h, ffn1_norm_g[l]), m[0], m[1]),
                                    ffn1_w_in[l], ffn1_w_out[l])
        hc = hc + 0.5 * mc[2] * swiglu(modulate(rmsnorm(hc, ffn1_norm_g[l]), mc[0], mc[1]),
                                       ffn1_w_in[l], ffn1_w_out[l])

        lru_p = (lru_conv_w[l], lru_conv_b[l], lru_w_r[l], lru_b_r[l],
                 lru_w_i[l], lru_b_i[l], lru_lambda[l])
        zc = modulate(rmsnorm(hc, mix_norm_g[l]), mc[3], mc[4])
        if not last:
            pc = zc @ w_in_mix[l]
            hf_c, hb_c = lru_scans(pc[..., :LRU_WIDTH], *lru_p, zeros_state, zeros_state)
            hc = hc + mc[5] * mix_out(pc, hf_c, hb_c, sgu_norm_g[l], sgu_w[l], sgu_b[l],
                                      w_out_mix[l], n_chunks_ctx)
        else:
            hf_c, hb_c = lru_scans(zc @ w_in_mix[l][:, :LRU_WIDTH], *lru_p,
                                   zeros_state, zeros_state)
        z = modulate(rmsnorm(h, mix_norm_g[l]), m[3], m[4])
        p = z @ w_in_mix[l]
        hf, hb = lru_scans(p[..., :LRU_WIDTH], *lru_p, hf_c[:, -1], hb_c[:, 0])
        h = h + m[5] * mix_out(p, hf, hb, sgu_norm_g[l], sgu_w[l], sgu_b[l],
                               w_out_mix[l], n_chunks_lat)

        h = h + 0.5 * m[8] * swiglu(modulate(rmsnorm(h, ffn2_norm_g[l]), m[6], m[7]),
                                    ffn2_w_in[l], ffn2_w_out[l])
        if not last:
            hc = hc + 0.5 * mc[8] * swiglu(modulate(rmsnorm(hc, ffn2_norm_g[l]), mc[6], mc[7]),
                                           ffn2_w_in[l], ffn2_w_out[l])

    return rmsnorm(h, final_norm_g)


import jax as _jax
import jax.numpy as _jnp

TWIN_FORMAT = 'train_step'
FWD_PARAMS = ['x', 'c', 'ctx', 'c_ctx', 'w_ada', 'b_ada', 'ffn1_norm_g', 'ffn1_w_in', 'ffn1_w_out', 'mix_norm_g', 'w_in_mix', 'lru_conv_w', 'lru_conv_b', 'lru_w_r', 'lru_b_r', 'lru_w_i', 'lru_b_i', 'lru_lambda', 'sgu_norm_g', 'sgu_w', 'sgu_b', 'w_out_mix', 'ffn2_norm_g', 'ffn2_w_in', 'ffn2_w_out', 'final_norm_g']
TWIN_WEIGHTS = ['c_ctx', 'w_ada', 'b_ada', 'ffn1_norm_g', 'ffn1_w_in', 'ffn1_w_out', 'mix_norm_g', 'w_in_mix', 'lru_conv_w', 'lru_conv_b', 'lru_w_r', 'lru_b_r', 'lru_w_i', 'lru_b_i', 'lru_lambda', 'sgu_norm_g', 'sgu_w', 'sgu_b', 'w_out_mix', 'ffn2_norm_g', 'ffn2_w_in', 'ffn2_w_out', 'final_norm_g']
TWIN_DIFF_INPUT = 'x'
TWIN_INPUTS = ['x', 'c', 'ctx', 'c_ctx', 'w_ada', 'b_ada', 'ffn1_norm_g', 'ffn1_w_in', 'ffn1_w_out', 'mix_norm_g', 'w_in_mix', 'lru_conv_w', 'lru_conv_b', 'lru_w_r', 'lru_b_r', 'lru_w_i', 'lru_b_i', 'lru_lambda', 'sgu_norm_g', 'sgu_w', 'sgu_b', 'w_out_mix', 'ffn2_norm_g', 'ffn2_w_in', 'ffn2_w_out', 'final_norm_g', 'loss_target', 'm_c_ctx', 'm_w_ada', 'm_b_ada', 'm_ffn1_norm_g', 'm_ffn1_w_in', 'm_ffn1_w_out', 'm_mix_norm_g', 'm_w_in_mix', 'm_lru_conv_w', 'm_lru_conv_b', 'm_lru_w_r', 'm_lru_b_r', 'm_lru_w_i', 'm_lru_b_i', 'm_lru_lambda', 'm_sgu_norm_g', 'm_sgu_w', 'm_sgu_b', 'm_w_out_mix', 'm_ffn2_norm_g', 'm_ffn2_w_in', 'm_ffn2_w_out', 'm_final_norm_g', 'v_c_ctx', 'v_w_ada', 'v_b_ada', 'v_ffn1_norm_g', 'v_ffn1_w_in', 'v_ffn1_w_out', 'v_mix_norm_g', 'v_w_in_mix', 'v_lru_conv_w', 'v_lru_conv_b', 'v_lru_w_r', 'v_lru_b_r', 'v_lru_w_i', 'v_lru_b_i', 'v_lru_lambda', 'v_sgu_norm_g', 'v_sgu_w', 'v_sgu_b', 'v_w_out_mix', 'v_ffn2_norm_g', 'v_ffn2_w_in', 'v_ffn2_w_out', 'v_final_norm_g']
TWIN_OUTPUTS = ['loss', 'grad_x', 'grad_c_ctx', 'grad_w_ada', 'grad_b_ada', 'grad_ffn1_norm_g', 'grad_ffn1_w_in', 'grad_ffn1_w_out', 'grad_mix_norm_g', 'grad_w_in_mix', 'grad_lru_conv_w', 'grad_lru_conv_b', 'grad_lru_w_r', 'grad_lru_b_r', 'grad_lru_w_i', 'grad_lru_b_i', 'grad_lru_lambda', 'grad_sgu_norm_g', 'grad_sgu_w', 'grad_sgu_b', 'grad_w_out_mix', 'grad_ffn2_norm_g', 'grad_ffn2_w_in', 'grad_ffn2_w_out', 'grad_final_norm_g', 'delta_c_ctx', 'delta_w_ada', 'delta_b_ada', 'delta_ffn1_norm_g', 'delta_ffn1_w_in', 'delta_ffn1_w_out', 'delta_mix_norm_g', 'delta_w_in_mix', 'delta_lru_conv_w', 'delta_lru_conv_b', 'delta_lru_w_r', 'delta_lru_b_r', 'delta_lru_w_i', 'delta_lru_b_i', 'delta_lru_lambda', 'delta_sgu_norm_g', 'delta_sgu_w', 'delta_sgu_b', 'delta_w_out_mix', 'delta_ffn2_norm_g', 'delta_ffn2_w_in', 'delta_ffn2_w_out', 'delta_final_norm_g', 'new_m_c_ctx', 'new_m_w_ada', 'new_m_b_ada', 'new_m_ffn1_norm_g', 'new_m_ffn1_w_in', 'new_m_ffn1_w_out', 'new_m_mix_norm_g', 'new_m_w_in_mix', 'new_m_lru_conv_w', 'new_m_lru_conv_b', 'new_m_lru_w_r', 'new_m_lru_b_r', 'new_m_lru_w_i', 'new_m_lru_b_i', 'new_m_lru_lambda', 'new_m_sgu_norm_g', 'new_m_sgu_w', 'new_m_sgu_b', 'new_m_w_out_mix', 'new_m_ffn2_norm_g', 'new_m_ffn2_w_in', 'new_m_ffn2_w_out', 'new_m_final_norm_g', 'new_v_c_ctx', 'new_v_w_ada', 'new_v_b_ada', 'new_v_ffn1_norm_g', 'new_v_ffn1_w_in', 'new_v_ffn1_w_out', 'new_v_mix_norm_g', 'new_v_w_in_mix', 'new_v_lru_conv_w', 'new_v_lru_conv_b', 'new_v_lru_w_r', 'new_v_lru_b_r', 'new_v_lru_w_i', 'new_v_lru_b_i', 'new_v_lru_lambda', 'new_v_sgu_norm_g', 'new_v_sgu_w', 'new_v_sgu_b', 'new_v_w_out_mix', 'new_v_ffn2_norm_g', 'new_v_ffn2_w_in', 'new_v_ffn2_w_out', 'new_v_final_norm_g']
TWIN_LEAF_KINDS = {'loss': 'loss', 'grad_x': 'grad_x', 'grad_c_ctx': 'grad_w', 'grad_w_ada': 'grad_w', 'grad_b_ada': 'grad_w', 'grad_ffn1_norm_g': 'grad_w', 'grad_ffn1_w_in': 'grad_w', 'grad_ffn1_w_out': 'grad_w', 'grad_mix_norm_g': 'grad_w', 'grad_w_in_mix': 'grad_w', 'grad_lru_conv_w': 'grad_w', 'grad_lru_conv_b': 'grad_w', 'grad_lru_w_r': 'grad_w', 'grad_lru_b_r': 'grad_w', 'grad_lru_w_i': 'grad_w', 'grad_lru_b_i': 'grad_w', 'grad_lru_lambda': 'grad_w', 'grad_sgu_norm_g': 'grad_w', 'grad_sgu_w': 'grad_w', 'grad_sgu_b': 'grad_w', 'grad_w_out_mix': 'grad_w', 'grad_ffn2_norm_g': 'grad_w', 'grad_ffn2_w_in': 'grad_w', 'grad_ffn2_w_out': 'grad_w', 'grad_final_norm_g': 'grad_w', 'delta_c_ctx': 'delta_w', 'delta_w_ada': 'delta_w', 'delta_b_ada': 'delta_w', 'delta_ffn1_norm_g': 'delta_w', 'delta_ffn1_w_in': 'delta_w', 'delta_ffn1_w_out': 'delta_w', 'delta_mix_norm_g': 'delta_w', 'delta_w_in_mix': 'delta_w', 'delta_lru_conv_w': 'delta_w', 'delta_lru_conv_b': 'delta_w', 'delta_lru_w_r': 'delta_w', 'delta_lru_b_r': 'delta_w', 'delta_lru_w_i': 'delta_w', 'delta_lru_b_i': 'delta_w', 'delta_lru_lambda': 'delta_w', 'delta_sgu_norm_g': 'delta_w', 'delta_sgu_w': 'delta_w', 'delta_sgu_b': 'delta_w', 'delta_w_out_mix': 'delta_w', 'delta_ffn2_norm_g': 'delta_w', 'delta_ffn2_w_in': 'delta_w', 'delta_ffn2_w_out': 'delta_w', 'delta_final_norm_g': 'delta_w', 'new_m_c_ctx': 'new_m', 'new_m_w_ada': 'new_m', 'new_m_b_ada': 'new_m', 'new_m_ffn1_norm_g': 'new_m', 'new_m_ffn1_w_in': 'new_m', 'new_m_ffn1_w_out': 'new_m', 'new_m_mix_norm_g': 'new_m', 'new_m_w_in_mix': 'new_m', 'new_m_lru_conv_w': 'new_m', 'new_m_lru_conv_b': 'new_m', 'new_m_lru_w_r': 'new_m', 'new_m_lru_b_r': 'new_m', 'new_m_lru_w_i': 'new_m', 'new_m_lru_b_i': 'new_m', 'new_m_lru_lambda': 'new_m', 'new_m_sgu_norm_g': 'new_m', 'new_m_sgu_w': 'new_m', 'new_m_sgu_b': 'new_m', 'new_m_w_out_mix': 'new_m', 'new_m_ffn2_norm_g': 'new_m', 'new_m_ffn2_w_in': 'new_m', 'new_m_ffn2_w_out': 'new_m', 'new_m_final_norm_g': 'new_m', 'new_v_c_ctx': 'new_v', 'new_v_w_ada': 'new_v', 'new_v_b_ada': 'new_v', 'new_v_ffn1_norm_g': 'new_v', 'new_v_ffn1_w_in': 'new_v', 'new_v_ffn1_w_out': 'new_v', 'new_v_mix_norm_g': 'new_v', 'new_v_w_in_mix': 'new_v', 'new_v_lru_conv_w': 'new_v', 'new_v_lru_conv_b': 'new_v', 'new_v_lru_w_r': 'new_v', 'new_v_lru_b_r': 'new_v', 'new_v_lru_w_i': 'new_v', 'new_v_lru_b_i': 'new_v', 'new_v_lru_lambda': 'new_v', 'new_v_sgu_norm_g': 'new_v', 'new_v_sgu_w': 'new_v', 'new_v_sgu_b': 'new_v', 'new_v_w_out_mix': 'new_v', 'new_v_ffn2_norm_g': 'new_v', 'new_v_ffn2_w_in': 'new_v', 'new_v_ffn2_w_out': 'new_v', 'new_v_final_norm_g': 'new_v'}


def _forward(args):
    return _fwd_reference(*[args[k] for k in FWD_PARAMS])


def _output_shape():
    def fwd():
        inp = _fwd_setup_inputs(0)
        return _fwd_reference(*[inp[k] for k in FWD_PARAMS])
    out = _jax.eval_shape(fwd)
    return out.shape, out.dtype

N_MICROBATCH = 1
ADAM_LR = 0.001
ADAM_B1 = 0.9
ADAM_B2 = 0.999
ADAM_EPS = 1e-08
ADAM_WD = 0.01
ADAM_STEP = 10
PER_EXAMPLE_BATCH_AXIS = {'x': 0, 'c': 0, 'ctx': 0, 'loss_target': 0}
SHARED_INPUTS = []
_WEIGHT_DTYPES = {'c_ctx': _jnp.float32, 'w_ada': _jnp.float32, 'b_ada': _jnp.float32, 'ffn1_norm_g': _jnp.float32, 'ffn1_w_in': _jnp.float32, 'ffn1_w_out': _jnp.float32, 'mix_norm_g': _jnp.float32, 'w_in_mix': _jnp.float32, 'lru_conv_w': _jnp.float32, 'lru_conv_b': _jnp.float32, 'lru_w_r': _jnp.float32, 'lru_b_r': _jnp.float32, 'lru_w_i': _jnp.float32, 'lru_b_i': _jnp.float32, 'lru_lambda': _jnp.float32, 'sgu_norm_g': _jnp.float32, 'sgu_w': _jnp.float32, 'sgu_b': _jnp.float32, 'w_out_mix': _jnp.float32, 'ffn2_norm_g': _jnp.float32, 'ffn2_w_in': _jnp.float32, 'ffn2_w_out': _jnp.float32, 'final_norm_g': _jnp.float32}
MOMENT_SCALE = {'c_ctx': 5.825571e-02, 'w_ada': 1.886995e-01, 'b_ada': 3.507608e-01, 'ffn1_norm_g': 3.708340e-02, 'ffn1_w_in': 1.672974e-02, 'ffn1_w_out': 2.743380e-02, 'mix_norm_g': 1.672277e-01, 'w_in_mix': 1.657344e-01, 'lru_conv_w': 2.587855e-01, 'lru_conv_b': 7.605778e-01, 'lru_w_r': 1.879252e-02, 'lru_b_r': 1.843971e-02, 'lru_w_i': 3.893136e-02, 'lru_b_i': 5.021477e-02, 'lru_lambda': 4.507007e-02, 'sgu_norm_g': 7.268929e-02, 'sgu_w': 7.221414e-02, 'sgu_b': 7.627440e-02, 'w_out_mix': 1.749949e-01, 'ffn2_norm_g': 3.302033e-02, 'ffn2_w_in': 1.447298e-02, 'ffn2_w_out': 2.365040e-02, 'final_norm_g': 6.452007e+01}


def _to_microbatches(a, axis):
    t = _jnp.moveaxis(a, axis, 0)
    t = t.reshape((N_MICROBATCH, t.shape[0] // N_MICROBATCH) + t.shape[1:])
    return _jnp.moveaxis(t, 1, axis + 1)


def setup_inputs(seed: int = 0) -> dict:
    inp = _fwd_setup_inputs(seed)
    key = _jax.random.fold_in(_jax.random.key(seed), 7919)
    shape, _ = _output_shape()
    out = dict(inp)
    out["loss_target"] = _jax.random.normal(_jax.random.fold_in(key, 0), shape, _jnp.float32)
    for i, name in enumerate(TWIN_WEIGHTS):
        w = inp[name].astype(_jnp.float32)
        if MOMENT_SCALE is None:
            s = _jnp.sqrt(_jnp.mean(_jnp.square(w)) + 1e-30)
        else:
            s = MOMENT_SCALE[name]
        km, kv = _jax.random.split(_jax.random.fold_in(key, i + 1))
        out[name] = w
        out["m_" + name] = s * _jax.random.normal(km, w.shape, _jnp.float32)
        out["v_" + name] = (s * s) * _jax.random.uniform(kv, w.shape, _jnp.float32, 0.5, 1.5)
    if N_MICROBATCH > 1:
        for name, axis in PER_EXAMPLE_BATCH_AXIS.items():
            out[name] = _to_microbatches(out[name], axis)
    return {'x': out['x'], 'c': out['c'], 'ctx': out['ctx'], 'c_ctx': out['c_ctx'], 'w_ada': out['w_ada'], 'b_ada': out['b_ada'], 'ffn1_norm_g': out['ffn1_norm_g'], 'ffn1_w_in': out['ffn1_w_in'], 'ffn1_w_out': out['ffn1_w_out'], 'mix_norm_g': out['mix_norm_g'], 'w_in_mix': out['w_in_mix'], 'lru_conv_w': out['lru_conv_w'], 'lru_conv_b': out['lru_conv_b'], 'lru_w_r': out['lru_w_r'], 'lru_b_r': out['lru_b_r'], 'lru_w_i': out['lru_w_i'], 'lru_b_i': out['lru_b_i'], 'lru_lambda': out['lru_lambda'], 'sgu_norm_g': out['sgu_norm_g'], 'sgu_w': out['sgu_w'], 'sgu_b': out['sgu_b'], 'w_out_mix': out['w_out_mix'], 'ffn2_norm_g': out['ffn2_norm_g'], 'ffn2_w_in': out['ffn2_w_in'], 'ffn2_w_out': out['ffn2_w_out'], 'final_norm_g': out['final_norm_g'], 'loss_target': out['loss_target'], 'm_c_ctx': out['m_c_ctx'], 'm_w_ada': out['m_w_ada'], 'm_b_ada': out['m_b_ada'], 'm_ffn1_norm_g': out['m_ffn1_norm_g'], 'm_ffn1_w_in': out['m_ffn1_w_in'], 'm_ffn1_w_out': out['m_ffn1_w_out'], 'm_mix_norm_g': out['m_mix_norm_g'], 'm_w_in_mix': out['m_w_in_mix'], 'm_lru_conv_w': out['m_lru_conv_w'], 'm_lru_conv_b': out['m_lru_conv_b'], 'm_lru_w_r': out['m_lru_w_r'], 'm_lru_b_r': out['m_lru_b_r'], 'm_lru_w_i': out['m_lru_w_i'], 'm_lru_b_i': out['m_lru_b_i'], 'm_lru_lambda': out['m_lru_lambda'], 'm_sgu_norm_g': out['m_sgu_norm_g'], 'm_sgu_w': out['m_sgu_w'], 'm_sgu_b': out['m_sgu_b'], 'm_w_out_mix': out['m_w_out_mix'], 'm_ffn2_norm_g': out['m_ffn2_norm_g'], 'm_ffn2_w_in': out['m_ffn2_w_in'], 'm_ffn2_w_out': out['m_ffn2_w_out'], 'm_final_norm_g': out['m_final_norm_g'], 'v_c_ctx': out['v_c_ctx'], 'v_w_ada': out['v_w_ada'], 'v_b_ada': out['v_b_ada'], 'v_ffn1_norm_g': out['v_ffn1_norm_g'], 'v_ffn1_w_in': out['v_ffn1_w_in'], 'v_ffn1_w_out': out['v_ffn1_w_out'], 'v_mix_norm_g': out['v_mix_norm_g'], 'v_w_in_mix': out['v_w_in_mix'], 'v_lru_conv_w': out['v_lru_conv_w'], 'v_lru_conv_b': out['v_lru_conv_b'], 'v_lru_w_r': out['v_lru_w_r'], 'v_lru_b_r': out['v_lru_b_r'], 'v_lru_w_i': out['v_lru_w_i'], 'v_lru_b_i': out['v_lru_b_i'], 'v_lru_lambda': out['v_lru_lambda'], 'v_sgu_norm_g': out['v_sgu_norm_g'], 'v_sgu_w': out['v_sgu_w'], 'v_sgu_b': out['v_sgu_b'], 'v_w_out_mix': out['v_w_out_mix'], 'v_ffn2_norm_g': out['v_ffn2_norm_g'], 'v_ffn2_w_in': out['v_ffn2_w_in'], 'v_ffn2_w_out': out['v_ffn2_w_out'], 'v_final_norm_g': out['v_final_norm_g']}


def _loss(weights, diff, rest, loss_target):
    with _jax.named_scope("forward"):
        args = {**rest, TWIN_DIFF_INPUT: diff, **{k: w.astype(_WEIGHT_DTYPES[k]) for k, w in weights.items()}}
        y = _forward(args)
    with _jax.named_scope("loss_head"):
        err = _jnp.square(y.astype(_jnp.float32) - loss_target)
        return 0.5 * _jnp.sum(_jnp.mean(err, axis=-1)) if err.ndim else 0.5 * err


def _adamw(w, g, m, v):
    m = ADAM_B1 * m + (1.0 - ADAM_B1) * g
    v = ADAM_B2 * v + (1.0 - ADAM_B2) * _jnp.square(g)
    m_hat = m / (1.0 - ADAM_B1 ** ADAM_STEP)
    v_hat = v / (1.0 - ADAM_B2 ** ADAM_STEP)
    delta = -ADAM_LR * (m_hat / (_jnp.sqrt(v_hat) + ADAM_EPS) + ADAM_WD * w)
    return delta, m, v


def reference(x, c, ctx, c_ctx, w_ada, b_ada, ffn1_norm_g, ffn1_w_in, ffn1_w_out, mix_norm_g, w_in_mix, lru_conv_w, lru_conv_b, lru_w_r, lru_b_r, lru_w_i, lru_b_i, lru_lambda, sgu_norm_g, sgu_w, sgu_b, w_out_mix, ffn2_norm_g, ffn2_w_in, ffn2_w_out, final_norm_g, loss_target, m_c_ctx, m_w_ada, m_b_ada, m_ffn1_norm_g, m_ffn1_w_in, m_ffn1_w_out, m_mix_norm_g, m_w_in_mix, m_lru_conv_w, m_lru_conv_b, m_lru_w_r, m_lru_b_r, m_lru_w_i, m_lru_b_i, m_lru_lambda, m_sgu_norm_g, m_sgu_w, m_sgu_b, m_w_out_mix, m_ffn2_norm_g, m_ffn2_w_in, m_ffn2_w_out, m_final_norm_g, v_c_ctx, v_w_ada, v_b_ada, v_ffn1_norm_g, v_ffn1_w_in, v_ffn1_w_out, v_mix_norm_g, v_w_in_mix, v_lru_conv_w, v_lru_conv_b, v_lru_w_r, v_lru_b_r, v_lru_w_i, v_lru_b_i, v_lru_lambda, v_sgu_norm_g, v_sgu_w, v_sgu_b, v_w_out_mix, v_ffn2_norm_g, v_ffn2_w_in, v_ffn2_w_out, v_final_norm_g):
    given = dict(x=x, c=c, ctx=ctx, c_ctx=c_ctx, w_ada=w_ada, b_ada=b_ada, ffn1_norm_g=ffn1_norm_g, ffn1_w_in=ffn1_w_in, ffn1_w_out=ffn1_w_out, mix_norm_g=mix_norm_g, w_in_mix=w_in_mix, lru_conv_w=lru_conv_w, lru_conv_b=lru_conv_b, lru_w_r=lru_w_r, lru_b_r=lru_b_r, lru_w_i=lru_w_i, lru_b_i=lru_b_i, lru_lambda=lru_lambda, sgu_norm_g=sgu_norm_g, sgu_w=sgu_w, sgu_b=sgu_b, w_out_mix=w_out_mix, ffn2_norm_g=ffn2_norm_g, ffn2_w_in=ffn2_w_in, ffn2_w_out=ffn2_w_out, final_norm_g=final_norm_g, loss_target=loss_target, m_c_ctx=m_c_ctx, m_w_ada=m_w_ada, m_b_ada=m_b_ada, m_ffn1_norm_g=m_ffn1_norm_g, m_ffn1_w_in=m_ffn1_w_in, m_ffn1_w_out=m_ffn1_w_out, m_mix_norm_g=m_mix_norm_g, m_w_in_mix=m_w_in_mix, m_lru_conv_w=m_lru_conv_w, m_lru_conv_b=m_lru_conv_b, m_lru_w_r=m_lru_w_r, m_lru_b_r=m_lru_b_r, m_lru_w_i=m_lru_w_i, m_lru_b_i=m_lru_b_i, m_lru_lambda=m_lru_lambda, m_sgu_norm_g=m_sgu_norm_g, m_sgu_w=m_sgu_w, m_sgu_b=m_sgu_b, m_w_out_mix=m_w_out_mix, m_ffn2_norm_g=m_ffn2_norm_g, m_ffn2_w_in=m_ffn2_w_in, m_ffn2_w_out=m_ffn2_w_out, m_final_norm_g=m_final_norm_g, v_c_ctx=v_c_ctx, v_w_ada=v_w_ada, v_b_ada=v_b_ada, v_ffn1_norm_g=v_ffn1_norm_g, v_ffn1_w_in=v_ffn1_w_in, v_ffn1_w_out=v_ffn1_w_out, v_mix_norm_g=v_mix_norm_g, v_w_in_mix=v_w_in_mix, v_lru_conv_w=v_lru_conv_w, v_lru_conv_b=v_lru_conv_b, v_lru_w_r=v_lru_w_r, v_lru_b_r=v_lru_b_r, v_lru_w_i=v_lru_w_i, v_lru_b_i=v_lru_b_i, v_lru_lambda=v_lru_lambda, v_sgu_norm_g=v_sgu_norm_g, v_sgu_w=v_sgu_w, v_sgu_b=v_sgu_b, v_w_out_mix=v_w_out_mix, v_ffn2_norm_g=v_ffn2_norm_g, v_ffn2_w_in=v_ffn2_w_in, v_ffn2_w_out=v_ffn2_w_out, v_final_norm_g=v_final_norm_g)
    weights = {n: given[n] for n in TWIN_WEIGHTS}
    shared = {n: given[n] for n in SHARED_INPUTS}
    per_example = {n: given[n] for n in ['x', 'c', 'ctx']}
    grad_fn = _jax.value_and_grad(_loss, argnums=(0, 1))

    def one_microbatch(ex, loss_target):
        ex = dict(ex)
        diff = ex.pop(TWIN_DIFF_INPUT)
        return grad_fn(weights, diff, {**shared, **ex}, loss_target)

    if N_MICROBATCH == 1:
        loss, (grad_w, grad_x) = one_microbatch(per_example, given["loss_target"])
    else:
        def body(carry, xs):
            loss_sum, grad_sum = carry
            l_k, (gw_k, gx_k) = one_microbatch(xs[0], xs[1])
            with _jax.named_scope("update"):
                return (loss_sum + l_k, _jax.tree.map(_jnp.add, grad_sum, gw_k)), gx_k

        init = (_jnp.zeros((), _jnp.float32), _jax.tree.map(_jnp.zeros_like, weights))
        (loss, grad_w), grad_x = _jax.lax.scan(body, init, (per_example, given["loss_target"]))
    with _jax.named_scope("update"):
        delta_w, new_m, new_v = {}, {}, {}
        for n in TWIN_WEIGHTS:
            delta_w[n], new_m[n], new_v[n] = _adamw(weights[n], grad_w[n], given["m_" + n], given["v_" + n])
    return (loss, grad_x, *[grad_w[n] for n in TWIN_WEIGHTS], *[delta_w[n] for n in TWIN_WEIGHTS],
            *[new_m[n] for n in TWIN_WEIGHTS], *[new_v[n] for n in TWIN_WEIGHTS])
```

```python
import functools

import jax
import jax.numpy as jnp
from jax import lax
from jax.experimental import pallas as pl
from jax.experimental.pallas import tpu as pltpu

F32 = jnp.float32
BF16 = jnp.bfloat16

D = 1024
CTX = 256
DEPTH = 2
EPS = 1e-6
D_FF = 2816
LRU_W = 512
HEADS = 8
HEAD_DIM = 64
CONV_W = 4
RG_C = 8.0
GROUPS = 4
GROUP_DIM = 128
CHUNK = 128
MLP_W = 512
IN_PROJ = 2048
N_MOD = 9
N_DEV = 8

LR = 0.001
B1 = 0.9
B2 = 0.999
ADAM_EPS = 1e-08
WD = 0.01
STEP = 10

SHARD_FF_IN = 704
HALF = 352
HALF_PAD = 384
SHARD_PAD = 2 * HALF_PAD
HID_PAD = 4 * SHARD_PAD
N_MIX_SHARD = IN_PROJ // N_DEV
OMIX_SHARD = D // N_DEV
ADA_SHARD = N_MOD * D // N_DEV

TM = 256
HALO = 8
VMEM_LIMIT = 60 * 1024 * 1024

MESH = pl.DeviceIdType.MESH
ANY = pl.BlockSpec(memory_space=pl.ANY)


def _cp(n_axes=1):
    return pltpu.CompilerParams(dimension_semantics=("arbitrary",) * n_axes, vmem_limit_bytes=VMEM_LIMIT)


def _dot(a, b):
    return jnp.dot(a, b, preferred_element_type=F32)


def _dot_nt(a, b):
    return lax.dot_general(a, b, (((1,), (1,)), ((), ())), preferred_element_type=F32)


def _dot_tn(a, b):
    return lax.dot_general(a, b, (((0,), (0,)), ((), ())), preferred_element_type=F32)


def _sigmoid(x):
    return 1.0 / (1.0 + jnp.exp(-x))


def _is_ctx(i, tm):
    row = i * tm + lax.broadcasted_iota(jnp.int32, (tm, 1), 0)
    return row < CTX


def _sel(is_ctx, mod_ref, k):
    return jnp.where(is_ctx, mod_ref[0, k:k + 1, :], mod_ref[1, k:k + 1, :])


def _acc2(ref, k, val, is_ctx):
    vc = jnp.sum(jnp.where(is_ctx, val, 0.0), axis=0, keepdims=True)
    va = jnp.sum(val, axis=0, keepdims=True)
    ref[0, k:k + 1, :] += vc
    ref[1, k:k + 1, :] += va - vc


def _norm_mod(h, g, shift, scale):
    r = lax.rsqrt(jnp.mean(h * h, axis=-1, keepdims=True) + EPS)
    n = h * r
    return (n * g) * (1.0 + scale) + shift, n, r


def _norm_mod_bwd(dz, n, r, g, scale):
    dn = dz * (g * (1.0 + scale))
    return r * (dn - n * jnp.mean(dn * n, axis=-1, keepdims=True))


def ffn_fwd(h, mod, g, win, wout, name):
    t = h.shape[0]

    def body(h_ref, mod_ref, g_ref, win_hbm, wout_hbm, out_ref, gu_ref, acc_ref, win_v, wout_v):
        i = pl.program_id(0)

        @pl.when(i == 0)
        def _():
            pltpu.sync_copy(win_hbm, win_v)
            pltpu.sync_copy(wout_hbm, wout_v)

        hh = h_ref[...]
        ic = _is_ctx(i, TM)
        z, _, _ = _norm_mod(hh, g_ref[...], _sel(ic, mod_ref, 0), _sel(ic, mod_ref, 1))
        zb = z.astype(BF16)
        acc = jnp.zeros((TM, D), F32)
        for dd in range(4):
            gg = _dot(zb, win_v[dd])
            uu = _dot(zb, win_v[dd + 4])
            gu_ref[:, dd * SHARD_PAD:(dd + 1) * SHARD_PAD] = gg.astype(BF16)
            gu_ref[:, (dd + 4) * SHARD_PAD:(dd + 5) * SHARD_PAD] = uu.astype(BF16)
            a = (gg * _sigmoid(gg)) * uu
            acc = acc + _dot(a.astype(BF16), wout_v[dd * SHARD_PAD:(dd + 1) * SHARD_PAD, :])
        acc_ref[...] = acc
        out_ref[...] = hh + (0.5 * _sel(ic, mod_ref, 2)) * acc

    row = pl.BlockSpec((TM, D), lambda i: (i, 0))
    return pl.pallas_call(
        body, name=name, grid=(t // TM,),
        in_specs=[row, pl.BlockSpec((2, 3, D), lambda i: (0, 0, 0)), pl.BlockSpec((1, D), lambda i: (0, 0)), ANY, ANY],
        out_specs=[row, pl.BlockSpec((TM, 2 * HID_PAD), lambda i: (i, 0)), row],
        out_shape=[jax.ShapeDtypeStruct((t, D), F32), jax.ShapeDtypeStruct((t, 2 * HID_PAD), BF16),
                   jax.ShapeDtypeStruct((t, D), F32)],
        scratch_shapes=[pltpu.VMEM((N_DEV, D, SHARD_PAD), BF16), pltpu.VMEM((HID_PAD, D), BF16)],
        compiler_params=_cp(),
    )(h, mod, g, win, wout)


def ffn_bwd_a(dy, acc, gu, mod, wout, name):
    t = dy.shape[0]
    nt = t // TM

    def body(dy_ref, acc_ref, gu_ref, mod_ref, wout_hbm, dp_ref, dwout_hbm, dgate_ref, wout_v, dwout_v):
        i = pl.program_id(0)

        @pl.when(i == 0)
        def _():
            pltpu.sync_copy(wout_hbm, wout_v)
            dwout_v[...] = jnp.zeros_like(dwout_v)
            dgate_ref[...] = jnp.zeros_like(dgate_ref)

        dy_ = dy_ref[...]
        ic = _is_ctx(i, TM)
        _acc2(dgate_ref, 0, 0.5 * dy_ * acc_ref[...], ic)
        daccb = ((0.5 * _sel(ic, mod_ref, 2)) * dy_).astype(BF16)
        for dd in range(4):
            blk = slice(dd * SHARD_PAD, (dd + 1) * SHARD_PAD)
            ublk = slice((dd + 4) * SHARD_PAD, (dd + 5) * SHARD_PAD)
            da = _dot_nt(daccb, wout_v[blk, :])
            gg = gu_ref[:, blk].astype(F32)
            uu = gu_ref[:, ublk].astype(F32)
            s = _sigmoid(gg)
            sl = gg * s
            dwout_v[blk, :] += _dot_tn((sl * uu).astype(BF16), daccb)
            dp_ref[:, blk] = (da * uu * (s + sl * (1.0 - s))).astype(BF16)
            dp_ref[:, ublk] = (da * sl).astype(BF16)

        @pl.when(i == nt - 1)
        def _():
            pltpu.sync_copy(dwout_v, dwout_hbm)

    row = pl.BlockSpec((TM, D), lambda i: (i, 0))
    wide = pl.BlockSpec((TM, 2 * HID_PAD), lambda i: (i, 0))
    return pl.pallas_call(
        body, name=name, grid=(nt,),
        in_specs=[row, row, wide, pl.BlockSpec((2, 3, D), lambda i: (0, 0, 0)), ANY],
        out_specs=[wide, ANY, pl.BlockSpec((2, 1, D), lambda i: (0, 0, 0))],
        out_shape=[jax.ShapeDtypeStruct((t, 2 * HID_PAD), BF16), jax.ShapeDtypeStruct((HID_PAD, D), F32),
                   jax.ShapeDtypeStruct((2, 1, D), F32)],
        scratch_shapes=[pltpu.VMEM((HID_PAD, D), BF16), pltpu.VMEM((HID_PAD, D), F32)],
        compiler_params=_cp(),
    )(dy, acc, gu, mod, wout)


def ffn_bwd_b(dy, h, dp, mod, g, win, name):
    t = dy.shape[0]
    nt = t // TM

    def body(dy_ref, h_ref, dp_ref, mod_ref, g_ref, win_hbm, dh_ref, dwin_hbm, dmod_ref, dg_ref, win_v, dwin_v):
        i = pl.program_id(0)

        @pl.when(i == 0)
        def _():
            pltpu.sync_copy(win_hbm, win_v)
            dwin_v[...] = jnp.zeros_like(dwin_v)
            dmod_ref[...] = jnp.zeros_like(dmod_ref)
            dg_ref[...] = jnp.zeros_like(dg_ref)

        ic = _is_ctx(i, TM)
        gain = g_ref[...]
        scale = _sel(ic, mod_ref, 1)
        z, n, r = _norm_mod(h_ref[...], gain, _sel(ic, mod_ref, 0), scale)
        zb = z.astype(BF16)
        dz = jnp.zeros((TM, D), F32)
        for dd in range(N_DEV):
            dpd = dp_ref[:, dd * SHARD_PAD:(dd + 1) * SHARD_PAD]
            dz = dz + _dot_nt(dpd, win_v[dd])
            dwin_v[dd] += _dot_tn(zb, dpd)
        _acc2(dmod_ref, 0, dz, ic)
        _acc2(dmod_ref, 1, dz * (n * gain), ic)
        dg_ref[...] += jnp.sum(dz * (1.0 + scale) * n, axis=0, keepdims=True)
        dh_ref[...] = dy_ref[...] + _norm_mod_bwd(dz, n, r, gain, scale)

        @pl.when(i == nt - 1)
        def _():
            pltpu.sync_copy(dwin_v, dwin_hbm)

    row = pl.BlockSpec((TM, D), lambda i: (i, 0))
    return pl.pallas_call(
        body, name=name, grid=(nt,),
        in_specs=[row, row, pl.BlockSpec((TM, 2 * HID_PAD), lambda i: (i, 0)),
                  pl.BlockSpec((2, 3, D), lambda i: (0, 0, 0)), pl.BlockSpec((1, D), lambda i: (0, 0)), ANY],
        out_specs=[row, ANY, pl.BlockSpec((2, 2, D), lambda i: (0, 0, 0)), pl.BlockSpec((1, D), lambda i: (0, 0))],
        out_shape=[jax.ShapeDtypeStruct((t, D), F32), jax.ShapeDtypeStruct((N_DEV, D, SHARD_PAD), F32),
                   jax.ShapeDtypeStruct((2, 2, D), F32), jax.ShapeDtypeStruct((1, D), F32)],
        scratch_shapes=[pltpu.VMEM((N_DEV, D, SHARD_PAD), BF16), pltpu.VMEM((N_DEV, D, SHARD_PAD), F32)],
        compiler_params=_cp(),
    )(dy, h, dp, mod, g, win)


def mix_in_fwd(h, mod, g, wmix, name):
    t = h.shape[0]

    def body(h_ref, mod_ref, g_ref, w_hbm, p_ref, w_v):
        i = pl.program_id(0)

        @pl.when(i == 0)
        def _():
            pltpu.sync_copy(w_hbm, w_v)

        ic = _is_ctx(i, TM)
        z, _, _ = _norm_mod(h_ref[...], g_ref[...], _sel(ic, mod_ref, 0), _sel(ic, mod_ref, 1))
        zb = z.astype(BF16)
        for dd in range(N_DEV):
            p_ref[:, dd * N_MIX_SHARD:(dd + 1) * N_MIX_SHARD] = _dot(zb, w_v[dd])

    return pl.pallas_call(
        body, name=name, grid=(t // TM,),
        in_specs=[pl.BlockSpec((TM, D), lambda i: (i, 0)), pl.BlockSpec((2, 3, D), lambda i: (0, 0, 0)),
                  pl.BlockSpec((1, D), lambda i: (0, 0)), ANY],
        out_specs=pl.BlockSpec((TM, IN_PROJ), lambda i: (i, 0)),
        out_shape=jax.ShapeDtypeStruct((t, IN_PROJ), F32),
        scratch_shapes=[pltpu.VMEM((N_DEV, D, N_MIX_SHARD), BF16)],
        compiler_params=_cp(),
    )(h, mod, g, wmix)


def _halo_specs(nt, tile_of):
    nb = nt * (TM // HALO)
    main = pl.BlockSpec((TM, LRU_W), lambda s: (tile_of(s), 0))
    prev = pl.BlockSpec((HALO, LRU_W), lambda s: (jnp.maximum(tile_of(s) * (TM // HALO) - 1, 0), 0))
    nxt = pl.BlockSpec((HALO, LRU_W), lambda s: (jnp.minimum((tile_of(s) + 1) * (TM // HALO), nb - 1), 0))
    return main, prev, nxt


def _ext(tile, nt, main, prev, nxt):
    has_prev = jnp.logical_and(tile != 0, tile != 1)
    has_next = jnp.logical_and(tile != 0, tile != nt - 1)
    return jnp.concatenate([jnp.where(has_prev, prev, 0.0), main, jnp.where(has_next, nxt, 0.0)], axis=0)


def _shifted(ext, off):
    n = ext.shape[0]
    return pltpu.roll(ext, (-off) % n, 0)[HALO:HALO + TM]


def _conv(ext, cw_ref, cb_ref):
    xc = cb_ref[...] + cw_ref[0:1, :] * _shifted(ext, -2)
    for k in range(1, CONV_W):
        xc = xc + cw_ref[k:k + 1, :] * _shifted(ext, k - 2)
    return xc


def _log1p(y):
    return jnp.where(y < 1e-2, y * (1.0 - y * (0.5 - y * (1.0 / 3.0 - 0.25 * y))), jnp.log(1.0 + y))


def _softplus_neg(lam):
    return jnp.maximum(-lam, 0.0) + _log1p(jnp.exp(-jnp.abs(lam)))


def _expm1(x):
    p = x * (1.0 + x * (1 / 2 + x * (1 / 6 + x * (1 / 24 + x * (1 / 120 + x * (1 / 720 + x * (1 / 5040)))))))
    return jnp.where(jnp.abs(x) < 0.25, p, jnp.exp(x) - 1.0)


def _gates(xc, wr, br, wi, bi, lam):
    xb = xc.astype(BF16)
    r = _sigmoid(_dot(xb, wr) + br)
    ig = _sigmoid(_dot(xb, wi) + bi)
    sp = _softplus_neg(lam)
    log_a = -RG_C * r * sp
    a = jnp.exp(log_a)
    mult = jnp.sqrt(-_expm1(2.0 * log_a))
    return r, ig, sp, a, mult


def _scan(a, b, reverse):
    n = a.shape[0]
    row = lax.broadcasted_iota(jnp.int32, a.shape, 0)
    s = 1
    while s < n:
        if reverse:
            keep = row < n - s
            a_s = jnp.where(keep, pltpu.roll(a, n - s, 0), 1.0)
            b_s = jnp.where(keep, pltpu.roll(b, n - s, 0), 0.0)
        else:
            keep = row >= s
            a_s = jnp.where(keep, pltpu.roll(a, s, 0), 1.0)
            b_s = jnp.where(keep, pltpu.roll(b, s, 0), 0.0)
        b = a * b_s + b
        a = a * a_s
        s *= 2
    return a, b


def lru_fwd(p, conv_w, conv_b, wr, br, wi, bi, lam, reverse, name):
    t = p.shape[0]
    nt = t // TM

    def tile_of(s):
        return jnp.where(s == 0, 0, nt - s) if reverse else s

    def body(x_ref, xp_ref, xn_ref, cw_ref, cb_ref, wr_ref, br_ref, wi_ref, bi_ref, lam_ref, h_ref, carry):
        s = pl.program_id(0)
        tile = tile_of(s)

        @pl.when(s == 0)
        def _():
            carry[...] = jnp.zeros_like(carry)

        ext = _ext(tile, nt, x_ref[...], xp_ref[...], xn_ref[...])
        xc = _conv(ext, cw_ref, cb_ref)
        _, ig, _, a, mult = _gates(xc, wr_ref[...], br_ref[...], wi_ref[...], bi_ref[...], lam_ref[...])
        a_cum, hl = _scan(a, mult * (ig * xc), reverse)
        hh = hl + a_cum * carry[...]
        h_ref[...] = hh
        carry[...] = hh[0:1, :] if reverse else hh[TM - 1:TM, :]

    main, prev, nxt = _halo_specs(nt, tile_of)
    vec = pl.BlockSpec((1, LRU_W), lambda s: (0, 0))
    mat = pl.BlockSpec((LRU_W, LRU_W), lambda s: (0, 0))
    return pl.pallas_call(
        body, name=name, grid=(nt,),
        in_specs=[main, prev, nxt, pl.BlockSpec((CONV_W, LRU_W), lambda s: (0, 0)), vec, mat, vec, mat, vec, vec],
        out_specs=main,
        out_shape=jax.ShapeDtypeStruct((t, LRU_W), F32),
        scratch_shapes=[pltpu.VMEM((1, LRU_W), F32)],
        compiler_params=_cp(),
    )(p, p, p, conv_w, conv_b, wr, br, wi, bi, lam)


def lru_bwd(p, hs, dhs, conv_w, conv_b, wr, br, wi, bi, lam, reverse, name):
    t = p.shape[0]
    nt = t // TM
    bpt = TM // HALO

    def tile_of(s):
        return jnp.where(s == nt - 1, 0, s + 1) if reverse else nt - 1 - s

    def hprev_block(s):
        tile = tile_of(s)
        if reverse:
            return (jnp.where(tile == nt - 1, 0, jnp.minimum((tile + 1) * bpt, nt * bpt - 1)), 0)
        return (jnp.maximum(tile * bpt - 1, 0), 0)

    def body(x_ref, xp_ref, xn_ref, h_ref, hp_ref, dh_ref, cw_ref, cb_ref, wr_ref, br_ref, wi_ref, bi_ref, lam_ref,
             dxc_ref, dwr_ref, dwi_ref, dbr_ref, dbi_ref, dlam_ref, carry):
        s = pl.program_id(0)
        tile = tile_of(s)

        @pl.when(s == 0)
        def _():
            carry[...] = jnp.zeros_like(carry)
            for ref in (dwr_ref, dwi_ref, dbr_ref, dbi_ref, dlam_ref):
                ref[...] = jnp.zeros_like(ref)

        ext = _ext(tile, nt, x_ref[...], xp_ref[...], xn_ref[...])
        xc = _conv(ext, cw_ref, cb_ref)
        wr_, wi_ = wr_ref[...], wi_ref[...]
        r, ig, sp, a, mult = _gates(xc, wr_, br_ref[...], wi_, bi_ref[...], lam_ref[...])
        gated = ig * xc
        row = lax.broadcasted_iota(jnp.int32, (TM, LRU_W), 0)
        hh = h_ref[...]
        start = jnp.where(tile != 0, hp_ref[0:1, :] if reverse else hp_ref[HALO - 1:HALO, :], 0.0)
        if reverse:
            edge = row == TM - 1
            hprev = jnp.where(edge, start, pltpu.roll(hh, TM - 1, 0))
            coef = jnp.where(row == 0, 0.0, pltpu.roll(a, 1, 0))
            bb = dh_ref[...] + jnp.where(row == 0, carry[...], 0.0)
        else:
            edge = row == 0
            hprev = jnp.where(edge, start, pltpu.roll(hh, 1, 0))
            coef = jnp.where(row == TM - 1, 0.0, pltpu.roll(a, TM - 1, 0))
            bb = dh_ref[...] + jnp.where(row == TM - 1, carry[...], 0.0)
        _, lmb = _scan(coef, bb, not reverse)
        al = a * lmb
        carry[...] = al[TM - 1:TM, :] if reverse else al[0:1, :]

        dgated = lmb * mult
        dloga = (lmb * hprev) * a - (lmb * gated) * (a * a) / mult
        dpre_r = (dloga * (-RG_C * sp)) * r * (1.0 - r)
        dpre_i = (dgated * xc) * ig * (1.0 - ig)
        drb, dib = dpre_r.astype(BF16), dpre_i.astype(BF16)
        xb = xc.astype(BF16)
        dxc_ref[...] = dgated * ig + _dot_nt(drb, wr_) + _dot_nt(dib, wi_)
        dwr_ref[...] += _dot_tn(xb, drb)
        dwi_ref[...] += _dot_tn(xb, dib)
        dbr_ref[...] += jnp.sum(dpre_r, axis=0, keepdims=True)
        dbi_ref[...] += jnp.sum(dpre_i, axis=0, keepdims=True)
        dlam_ref[...] += jnp.sum(dloga * (-RG_C * r), axis=0, keepdims=True)

        @pl.when(s == nt - 1)
        def _():
            dlam_ref[...] = dlam_ref[...] * (-_sigmoid(-lam_ref[...]))

    main, prev, nxt = _halo_specs(nt, tile_of)
    vec = pl.BlockSpec((1, LRU_W), lambda s: (0, 0))
    mat = pl.BlockSpec((LRU_W, LRU_W), lambda s: (0, 0))
    vshape = jax.ShapeDtypeStruct((1, LRU_W), F32)
    mshape = jax.ShapeDtypeStruct((LRU_W, LRU_W), F32)
    return pl.pallas_call(
        body, name=name, grid=(nt,),
        in_specs=[main, prev, nxt, main, pl.BlockSpec((HALO, LRU_W), hprev_block), main,
                  pl.BlockSpec((CONV_W, LRU_W), lambda s: (0, 0)), vec, mat, vec, mat, vec, vec],
        out_specs=[main, mat, mat, vec, vec, vec],
        out_shape=[jax.ShapeDtypeStruct((t, LRU_W), F32), mshape, mshape, vshape, vshape, vshape],
        scratch_shapes=[pltpu.VMEM((1, LRU_W), F32)],
        compiler_params=_cp(),
    )(p, p, p, hs, hs, dhs, conv_w, conv_b, wr, br, wi, bi, lam)


GELU_C = 0.7978845608028654
GELU_A = 0.044715


def _gelu(x):
    th = jnp.tanh(GELU_C * (x + GELU_A * x * x * x))
    return 0.5 * x * (1.0 + th), th


def _sgu(v, gain, w_ref, bt_ref):
    mu = jnp.mean(v, axis=-1, keepdims=True)
    xc = v - mu
    rs = lax.rsqrt(jnp.mean(xc * xc, axis=-1, keepdims=True) + EPS)
    vhat = xc * rs
    vnb = (vhat * gain).astype(BF16)
    chunks = []
    for ch in range(TM // CHUNK):
        zs = []
        for gi in range(GROUPS):
            vb = vnb[ch * CHUNK:(ch + 1) * CHUNK, gi * GROUP_DIM:(gi + 1) * GROUP_DIM]
            zs.append(_dot(w_ref[gi].astype(BF16), vb) + bt_ref[:, gi:gi + 1])
        chunks.append(jnp.concatenate(zs, axis=1))
    return jnp.concatenate(chunks, axis=0), vhat, rs, vnb


def _pcols(k):
    return pl.BlockSpec((TM, LRU_W), lambda i: (i, k))


def mix_out_fwd(h, p, hf, hb, mod, sgu_g, sgu_w, sgu_bt, womix, name):
    t = h.shape[0]

    def body(h_ref, gl_ref, u_ref, v_ref, hf_ref, hb_ref, mod_ref, sg_ref, sw_ref, sb_ref, w_hbm, out_ref, o_ref, w_v):
        i = pl.program_id(0)

        @pl.when(i == 0)
        def _():
            pltpu.sync_copy(w_hbm, w_v)

        ic = _is_ctx(i, TM)
        ge, _ = _gelu(gl_ref[...])
        y_lru = (hf_ref[...] + hb_ref[...]) * ge
        z, _, _, _ = _sgu(v_ref[...], sg_ref[...], sw_ref, sb_ref)
        y = jnp.concatenate([y_lru, u_ref[...] * z], axis=1).astype(BF16)
        o = _dot(y, w_v[...])
        o_ref[...] = o
        out_ref[...] = h_ref[...] + _sel(ic, mod_ref, 2) * o

    row = pl.BlockSpec((TM, D), lambda i: (i, 0))
    half = pl.BlockSpec((TM, LRU_W), lambda i: (i, 0))
    return pl.pallas_call(
        body, name=name, grid=(t // TM,),
        in_specs=[row, _pcols(1), _pcols(2), _pcols(3), half, half, pl.BlockSpec((2, 3, D), lambda i: (0, 0, 0)),
                  pl.BlockSpec((1, MLP_W), lambda i: (0, 0)), pl.BlockSpec((GROUPS, CHUNK, CHUNK), lambda i: (0, 0, 0)),
                  pl.BlockSpec((CHUNK, GROUPS), lambda i: (0, 0)), ANY],
        out_specs=[row, row],
        out_shape=[jax.ShapeDtypeStruct((t, D), F32), jax.ShapeDtypeStruct((t, D), F32)],
        scratch_shapes=[pltpu.VMEM((D, D), BF16)],
        compiler_params=_cp(),
    )(h, p, p, p, hf, hb, mod, sgu_g, sgu_w, sgu_bt, womix)


def mix_out_bwd(dy, p, hf, hb, o, mod, sgu_g, sgu_w, sgu_bt, womix, name):
    t = dy.shape[0]

    def body(dy_ref, gl_ref, u_ref, v_ref, hf_ref, hb_ref, o_ref, mod_ref, sg_ref, sw_ref, sb_ref, w_hbm,
             dhs_ref, dp_ref, dw_ref, dgate_ref, dsg_ref, dsw_ref, dsb_ref, w_v):
        i = pl.program_id(0)

        @pl.when(i == 0)
        def _():
            pltpu.sync_copy(w_hbm, w_v)
            for ref in (dw_ref, dgate_ref, dsg_ref, dsw_ref, dsb_ref):
                ref[...] = jnp.zeros_like(ref)

        ic = _is_ctx(i, TM)
        dy_ = dy_ref[...]
        _acc2(dgate_ref, 0, dy_ * o_ref[...], ic)
        dob = (_sel(ic, mod_ref, 2) * dy_).astype(BF16)

        gl = gl_ref[...]
        ge, th = _gelu(gl)
        hsum = hf_ref[...] + hb_ref[...]
        gain = sg_ref[...]
        uu = u_ref[...]
        z, vhat, rs, vnb = _sgu(v_ref[...], gain, sw_ref, sb_ref)
        y = jnp.concatenate([hsum * ge, uu * z], axis=1).astype(BF16)
        dw_ref[...] += _dot_tn(y, dob)
        dyy = _dot_nt(dob, w_v[...])
        dyl, dys = dyy[:, :LRU_W], dyy[:, LRU_W:]

        dhs_ref[...] = dyl * ge
        dge = 0.5 * (1.0 + th) + 0.5 * gl * (1.0 - th * th) * (GELU_C * (1.0 + 3.0 * GELU_A * gl * gl))
        dp_ref[:, 0:LRU_W] = dyl * hsum * dge
        dp_ref[:, LRU_W:2 * LRU_W] = dys * z

        dz = dys * uu
        dzb = dz.astype(BF16)
        dvn_chunks, dsb_cols = [], [jnp.zeros((CHUNK, 1), F32)] * GROUPS
        for ch in range(TM // CHUNK):
            cols = []
            for gi in range(GROUPS):
                rs_, cs_ = slice(ch * CHUNK, (ch + 1) * CHUNK), slice(gi * GROUP_DIM, (gi + 1) * GROUP_DIM)
                dzg = dzb[rs_, cs_]
                dsb_cols[gi] = dsb_cols[gi] + jnp.sum(dz[rs_, cs_], axis=1, keepdims=True)
                dsw_ref[gi] += _dot_nt(dzg, vnb[rs_, cs_])
                cols.append(_dot_tn(sw_ref[gi].astype(BF16), dzg))
            dvn_chunks.append(jnp.concatenate(cols, axis=1))
        dsb_ref[...] += jnp.concatenate(dsb_cols, axis=1)
        dvn = jnp.concatenate(dvn_chunks, axis=0)
        dsg_ref[...] += jnp.sum(dvn * vhat, axis=0, keepdims=True)
        dvh = dvn * gain
        dp_ref[:, 2 * LRU_W:3 * LRU_W] = rs * (dvh - jnp.mean(dvh, axis=-1, keepdims=True)
                                               - vhat * jnp.mean(dvh * vhat, axis=-1, keepdims=True))

    row = pl.BlockSpec((TM, D), lambda i: (i, 0))
    half = pl.BlockSpec((TM, LRU_W), lambda i: (i, 0))
    const2 = lambda i: (0, 0)
    const3 = lambda i: (0, 0, 0)
    return pl.pallas_call(
        body, name=name, grid=(t // TM,),
        in_specs=[row, _pcols(1), _pcols(2), _pcols(3), half, half, row, pl.BlockSpec((2, 3, D), const3),
                  pl.BlockSpec((1, MLP_W), const2), pl.BlockSpec((GROUPS, CHUNK, CHUNK), const3),
                  pl.BlockSpec((CHUNK, GROUPS), const2), ANY],
        out_specs=[half, pl.BlockSpec((TM, 3 * LRU_W), lambda i: (i, 0)), pl.BlockSpec((D, D), const2),
                   pl.BlockSpec((2, 1, D), const3), pl.BlockSpec((1, MLP_W), const2),
                   pl.BlockSpec((GROUPS, CHUNK, CHUNK), const3), pl.BlockSpec((CHUNK, GROUPS), const2)],
        out_shape=[jax.ShapeDtypeStruct((t, LRU_W), F32), jax.ShapeDtypeStruct((t, 3 * LRU_W), F32),
                   jax.ShapeDtypeStruct((D, D), F32), jax.ShapeDtypeStruct((2, 1, D), F32),
                   jax.ShapeDtypeStruct((1, MLP_W), F32), jax.ShapeDtypeStruct((GROUPS, CHUNK, CHUNK), F32),
                   jax.ShapeDtypeStruct((CHUNK, GROUPS), F32)],
        scratch_shapes=[pltpu.VMEM((D, D), BF16)],
        compiler_params=_cp(),
    )(dy, p, p, p, hf, hb, o, mod, sgu_g, sgu_w, sgu_bt, womix)


def mix_in_bwd(dy, h, p, dxf, dxb, dprest, mod, g, conv_w, wmix, name):
    t = dy.shape[0]
    nt = t // TM

    def body(dy_ref, h_ref, x_ref, xp_ref, xn_ref, f_ref, fp_ref, fn_ref, b_ref, bp_ref, bn_ref, dpr_ref, mod_ref,
             g_ref, cw_ref, w_hbm, dh_ref, dw_hbm, dmod_ref, dg_ref, dcw_ref, dcb_ref, w_v, dw_v):
        i = pl.program_id(0)

        @pl.when(i == 0)
        def _():
            pltpu.sync_copy(w_hbm, w_v)
            dw_v[...] = jnp.zeros_like(dw_v)
            for ref in (dmod_ref, dg_ref, dcw_ref, dcb_ref):
                ref[...] = jnp.zeros_like(ref)

        dmain = f_ref[...] + b_ref[...]
        dext = _ext(i, nt, dmain, fp_ref[...] + bp_ref[...], fn_ref[...] + bn_ref[...])
        xext = _ext(i, nt, x_ref[...], xp_ref[...], xn_ref[...])
        dxl = cw_ref[0:1, :] * _shifted(dext, 2)
        for k in range(1, CONV_W):
            dxl = dxl + cw_ref[k:k + 1, :] * _shifted(dext, 2 - k)
        dcw_ref[...] += jnp.concatenate(
            [jnp.sum(dmain * _shifted(xext, k - 2), axis=0, keepdims=True) for k in range(CONV_W)], axis=0)
        dcb_ref[...] += jnp.sum(dmain, axis=0, keepdims=True)

        ic = _is_ctx(i, TM)
        gain = g_ref[...]
        scale = _sel(ic, mod_ref, 1)
        z, n, r = _norm_mod(h_ref[...], gain, _sel(ic, mod_ref, 0), scale)
        zb = z.astype(BF16)
        dpb = jnp.concatenate([dxl, dpr_ref[...]], axis=1).astype(BF16)
        dz = jnp.zeros((TM, D), F32)
        for dd in range(N_DEV):
            dpd = dpb[:, dd * N_MIX_SHARD:(dd + 1) * N_MIX_SHARD]
            dz = dz + _dot_nt(dpd, w_v[dd])
            dw_v[dd] += _dot_tn(zb, dpd)
        _acc2(dmod_ref, 0, dz, ic)
        _acc2(dmod_ref, 1, dz * (n * gain), ic)
        dg_ref[...] += jnp.sum(dz * (1.0 + scale) * n, axis=0, keepdims=True)
        dh_ref[...] = dy_ref[...] + _norm_mod_bwd(dz, n, r, gain, scale)

        @pl.when(i == nt - 1)
        def _():
            pltpu.sync_copy(dw_v, dw_hbm)

    main, prev, nxt = _halo_specs(nt, lambda s: s)
    row = pl.BlockSpec((TM, D), lambda i: (i, 0))
    const2 = lambda i: (0, 0)
    return pl.pallas_call(
        body, name=name, grid=(nt,),
        in_specs=[row, row, main, prev, nxt, main, prev, nxt, main, prev, nxt,
                  pl.BlockSpec((TM, 3 * LRU_W), lambda i: (i, 0)), pl.BlockSpec((2, 3, D), lambda i: (0, 0, 0)),
                  pl.BlockSpec((1, D), const2), pl.BlockSpec((CONV_W, LRU_W), const2), ANY],
        out_specs=[row, ANY, pl.BlockSpec((2, 2, D), lambda i: (0, 0, 0)), pl.BlockSpec((1, D), const2),
                   pl.BlockSpec((CONV_W, LRU_W), const2), pl.BlockSpec((1, LRU_W), const2)],
        out_shape=[jax.ShapeDtypeStruct((t, D), F32), jax.ShapeDtypeStruct((N_DEV, D, N_MIX_SHARD), F32),
                   jax.ShapeDtypeStruct((2, 2, D), F32), jax.ShapeDtypeStruct((1, D), F32),
                   jax.ShapeDtypeStruct((CONV_W, LRU_W), F32), jax.ShapeDtypeStruct((1, LRU_W), F32)],
        scratch_shapes=[pltpu.VMEM((N_DEV, D, N_MIX_SHARD), BF16), pltpu.VMEM((N_DEV, D, N_MIX_SHARD), F32)],
        compiler_params=_cp(),
    )(dy, h, p, p, p, dxf, dxf, dxf, dxb, dxb, dxb, dprest, mod, g, conv_w, wmix)


def loss_head(h, g, target, name):
    t = h.shape[0]
    nc = CTX // TM

    def body(h_ref, g_ref, t_ref, dh_ref, loss_ref, dg_ref):
        i = pl.program_id(0)

        @pl.when(i == 0)
        def _():
            loss_ref[...] = jnp.zeros_like(loss_ref)
            dg_ref[...] = jnp.zeros_like(dg_ref)

        @pl.when(i < nc)
        def _():
            dh_ref[...] = jnp.zeros_like(dh_ref)

        @pl.when(i >= nc)
        def _():
            hh = h_ref[...]
            gain = g_ref[...]
            r = lax.rsqrt(jnp.mean(hh * hh, axis=-1, keepdims=True) + EPS)
            n = hh * r
            err = n * gain - t_ref[...]
            loss_ref[...] += 0.5 * jnp.sum(jnp.mean(err * err, axis=-1, keepdims=True))
            dy = err * (1.0 / D)
            dg_ref[...] += jnp.sum(dy * n, axis=0, keepdims=True)
            dn = dy * gain
            dh_ref[...] = r * (dn - n * jnp.mean(dn * n, axis=-1, keepdims=True))

    row = pl.BlockSpec((TM, D), lambda i: (i, 0))
    return pl.pallas_call(
        body, name=name, grid=(t // TM,),
        in_specs=[row, pl.BlockSpec((1, D), lambda i: (0, 0)),
                  pl.BlockSpec((TM, D), lambda i: (jnp.maximum(i - nc, 0), 0))],
        out_specs=[row, pl.BlockSpec((8, 128), lambda i: (0, 0)), pl.BlockSpec((1, D), lambda i: (0, 0))],
        out_shape=[jax.ShapeDtypeStruct((t, D), F32), jax.ShapeDtypeStruct((8, 128), F32),
                   jax.ShapeDtypeStruct((1, D), F32)],
        compiler_params=_cp(),
    )(h, g, target)


def pad_in_shard(w):
    z = jnp.zeros(w.shape[:-1] + (HALF_PAD - HALF,), w.dtype)
    return jnp.concatenate([w[..., :HALF], z, w[..., HALF:], z], axis=-1)


def unpad_in_shard(g):
    return jnp.concatenate([g[..., :HALF], g[..., HALF_PAD:HALF_PAD + HALF]], axis=-1)


def pad_out_shard(w):
    return jnp.concatenate([w, jnp.zeros(w.shape[:-2] + (HALF_PAD - HALF, w.shape[-1]), w.dtype)], axis=-2)


def unpad_out_shard(g):
    return g[..., :HALF, :]


def _block_diag(w):
    eye = jnp.eye(HEADS, dtype=w.dtype)
    return jnp.einsum("dhij,hk->dhikj", w, eye).reshape(2, LRU_W, LRU_W)


def _block_diag_inv(full):
    f = full.reshape(2, HEADS, HEAD_DIM, HEADS, HEAD_DIM)
    return jnp.stack([f[:, hd, :, hd, :] for hd in range(HEADS)], axis=1)


def small_layer(g1, gm, g2, conv_w, conv_b, w_r, b_r, w_i, b_i, lam, sgu_g, sgu_w, sgu_b):
    return dict(g1=g1[None, :], gm=gm[None, :], g2=g2[None, :], conv_w=conv_w, conv_b=conv_b[None, :],
                wr=_block_diag(w_r).astype(BF16), br=b_r[:, None, :], wi=_block_diag(w_i).astype(BF16),
                bi=b_i[:, None, :], lam=lam[:, None, :], sgu_g=sgu_g[None, :], sgu_w=sgu_w, sgu_bt=sgu_b.T)


def small_grads(g):
    return dict(ffn1_norm_g=g["g1"][0], mix_norm_g=g["gm"][0], ffn2_norm_g=g["g2"][0], lru_conv_w=g["conv_w"],
                lru_conv_b=g["conv_b"][0], lru_w_r=_block_diag_inv(g["wr"]), lru_b_r=g["br"][:, 0, :],
                lru_w_i=_block_diag_inv(g["wi"]), lru_b_i=g["bi"][:, 0, :], lru_lambda=g["lam"][:, 0, :],
                sgu_norm_g=g["sgu_g"][0], sgu_w=g["sgu_w"], sgu_b=g["sgu_bt"].T)


def local_fwd_bwd(h0, target, mods, layers, final_g):
    assert CTX == TM
    saved = []
    h = h0
    for l, w in enumerate(layers):
        m1, mm, m2 = mods[l][:, 0:3], mods[l][:, 3:6], mods[l][:, 6:9]
        h1, gu1, acc1 = ffn_fwd(h, m1, w["g1"], w["win1"], w["wout1"], f"ffn1_fwd_{l}")
        p = mix_in_fwd(h1, mm, w["gm"], w["wmix"], f"mix_in_fwd_{l}")
        hs = [lru_fwd(p, w["conv_w"], w["conv_b"], w["wr"][d], w["br"][d], w["wi"][d], w["bi"][d], w["lam"][d],
                      bool(d), f"lru_fwd_{l}_{d}") for d in range(2)]
        h2, o = mix_out_fwd(h1, p, hs[0], hs[1], mm, w["sgu_g"], w["sgu_w"], w["sgu_bt"], w["womix"], f"mix_out_fwd_{l}")
        h3, gu2, acc2 = ffn_fwd(h2, m2, w["g2"], w["win2"], w["wout2"], f"ffn2_fwd_{l}")
        saved.append((h, h1, h2, gu1, acc1, p, hs, o, gu2, acc2))
        h = h3
    dh, loss, dgf = loss_head(h, final_g, target, "loss_head")

    grads = [None] * len(layers)
    dmods = [None] * len(layers)
    for l in reversed(range(len(layers))):
        w = layers[l]
        m1, mm, m2 = mods[l][:, 0:3], mods[l][:, 3:6], mods[l][:, 6:9]
        hin, h1, h2, gu1, acc1, p, hs, o, gu2, acc2 = saved[l]
        g = {}
        dp2, g["wout2"], dgate2 = ffn_bwd_a(dh, acc2, gu2, m2, w["wout2"], f"ffn2_bwd_a_{l}")
        dh, g["win2"], dmod2, g["g2"] = ffn_bwd_b(dh, h2, dp2, m2, w["g2"], w["win2"], f"ffn2_bwd_b_{l}")
        dhs, dprest, g["womix"], dgatem, g["sgu_g"], g["sgu_w"], g["sgu_bt"] = mix_out_bwd(
            dh, p, hs[0], hs[1], o, mm, w["sgu_g"], w["sgu_w"], w["sgu_bt"], w["womix"], f"mix_out_bwd_{l}")
        dx, per_dir = [], []
        for d in range(2):
            out = lru_bwd(p, hs[d], dhs, w["conv_w"], w["conv_b"], w["wr"][d], w["br"][d], w["wi"][d], w["bi"][d],
                          w["lam"][d], bool(d), f"lru_bwd_{l}_{d}")
            dx.append(out[0])
            per_dir.append(out[1:])
        for k, nm in enumerate(("wr", "wi", "br", "bi", "lam")):
            g[nm] = jnp.stack([per_dir[0][k], per_dir[1][k]])
        dh, g["wmix"], dmodm, g["gm"], g["conv_w"], g["conv_b"] = mix_in_bwd(
            dh, h1, p, dx[0], dx[1], dprest, mm, w["gm"], w["conv_w"], w["wmix"], f"mix_in_bwd_{l}")
        dp1, g["wout1"], dgate1 = ffn_bwd_a(dh, acc1, gu1, m1, w["wout1"], f"ffn1_bwd_a_{l}")
        dh, g["win1"], dmod1, g["g1"] = ffn_bwd_b(dh, hin, dp1, m1, w["g1"], w["win1"], f"ffn1_bwd_b_{l}")
        dmods[l] = jnp.concatenate([dmod1, dgate1, dmodm, dgatem, dmod2, dgate2], axis=1)
        grads[l] = g
    return loss, dh, jnp.stack(dmods), grads, dgf


def _position():
    return lax.axis_index("x"), lax.axis_index("y"), lax.axis_index("c")


def all_gather(shards, space, name):
    n = len(shards)

    def body(*refs):
        ins, outs = refs[:n], refs[n:2 * n]
        send_sems, recv_sems, local_sems = refs[2 * n:]
        x, y, c = _position()
        me, sibling = (x, y, c), (x, y, 1 - c)
        chips = [(1 - x, y), (x, 1 - y), (1 - x, 1 - y)]

        def copy(t, k, block, to, src=None):
            dst = outs[t].at[4 * block[0] + 2 * block[1] + block[2]]
            return pltpu.make_async_remote_copy(
                src_ref=dst if src is None else src, dst_ref=dst, send_sem=send_sems.at[t, k],
                recv_sem=recv_sems.at[t, k], device_id=to, device_id_type=MESH)

        local, remote = [], []
        for t in range(n):
            mine = pltpu.make_async_copy(ins[t], outs[t].at[4 * x + 2 * y + c], local_sems.at[t])
            mine.start()
            local.append(mine)
            first = [copy(t, 0, me, sibling, src=ins[t])]
            first += [copy(t, 1 + j, me, (*chip, c), src=ins[t]) for j, chip in enumerate(chips)]
            for cp in first:
                cp.start()
            remote += first
        for j, chip in enumerate(chips):
            for t in range(n):
                copy(t, 1 + j, (*chip, c), me).wait_recv()
                fwd = copy(t, 4 + j, (*chip, c), sibling)
                fwd.start()
                remote.append(fwd)
        for t in range(n):
            copy(t, 0, sibling, me).wait_recv()
            for j, chip in enumerate(chips):
                copy(t, 4 + j, (*chip, 1 - c), me).wait_recv()
        for cp in remote:
            cp.wait_send()
        for cp in local:
            cp.wait()

    spec = pl.BlockSpec(memory_space=space)
    return pl.pallas_call(
        body, name=name,
        in_specs=[spec] * n, out_specs=[spec] * n,
        out_shape=[jax.ShapeDtypeStruct((N_DEV,) + s.shape, s.dtype) for s in shards],
        scratch_shapes=[pltpu.SemaphoreType.DMA((n, 7)), pltpu.SemaphoreType.DMA((n, 7)), pltpu.SemaphoreType.DMA((n,))],
        compiler_params=pltpu.CompilerParams(vmem_limit_bytes=VMEM_LIMIT),
    )(*shards)


def _exchange(tensors, plan, n_slots, name):
    n = len(tensors)

    def body(*refs):
        ins, outs = refs[:n], refs[n:2 * n]
        send_sems, recv_sems = refs[2 * n:]
        copies = []
        for t in range(n):
            for k, (block, to) in enumerate(plan(*_position())):
                cp = pltpu.make_async_remote_copy(
                    src_ref=ins[t].at[block], dst_ref=outs[t].at[k], send_sem=send_sems.at[t, k],
                    recv_sem=recv_sems.at[t, k], device_id=to, device_id_type=MESH)
                cp.start()
                copies.append(cp)
        for cp in copies:
            cp.wait()

    return pl.pallas_call(
        body, name=name,
        in_specs=[ANY] * n, out_specs=[ANY] * n,
        out_shape=[jax.ShapeDtypeStruct((n_slots,) + s.shape[1:], s.dtype) for s in tensors],
        scratch_shapes=[pltpu.SemaphoreType.DMA((n, n_slots)), pltpu.SemaphoreType.DMA((n, n_slots))],
    )(*tensors)


def exchange_pair(grads, name):
    def plan(x, y, c):
        return [(4 * cx + 2 * cy + (1 - c), (x, y, 1 - c)) for cx in range(2) for cy in range(2)]
    return _exchange(grads, plan, 4, name)


def exchange_chips(parts, name):
    def plan(x, y, c):
        return [(2 * cx + cy, (cx, cy, c)) for cx, cy in [(1 - x, y), (x, 1 - y), (1 - x, 1 - y)]]
    return _exchange(parts, plan, 3, name)


def _row_block(r, c, limit=262144):
    best = 8
    for rb in range(8, r + 1, 8):
        if r % rb == 0 and rb * c <= limit:
            best = rb
    return best


def pair_sum(grads, recv, c_idx, name):
    _, r, c = grads.shape
    rb = _row_block(r, c)

    def body(c_ref, g_ref, r_ref, o_ref):
        o_ref[...] = (g_ref[...] + r_ref[...]).astype(BF16)

    return pl.pallas_call(
        body, name=name,
        grid_spec=pltpu.PrefetchScalarGridSpec(
            num_scalar_prefetch=1, grid=(4, r // rb),
            in_specs=[pl.BlockSpec((1, rb, c), lambda j, i, c_ref: (2 * j + c_ref[0], i, 0)),
                      pl.BlockSpec((1, rb, c), lambda j, i, c_ref: (j, i, 0))],
            out_specs=pl.BlockSpec((1, rb, c), lambda j, i, c_ref: (j, i, 0))),
        out_shape=jax.ShapeDtypeStruct((4, r, c), BF16),
        compiler_params=_cp(2),
    )(c_idx, grads, recv)


def final_sum(grads, recv1, recv2, where, name):
    _, r, c = grads.shape
    rb = _row_block(r, c)

    def body(w_ref, g_ref, r1_ref, r2_ref, o_ref):
        far = (r2_ref[0].astype(F32) + r2_ref[1].astype(F32)) + r2_ref[2].astype(F32)
        o_ref[...] = (g_ref[0] + r1_ref[0]) + far

    return pl.pallas_call(
        body, name=name,
        grid_spec=pltpu.PrefetchScalarGridSpec(
            num_scalar_prefetch=1, grid=(r // rb,),
            in_specs=[pl.BlockSpec((1, rb, c), lambda i, w_ref: (w_ref[0], i, 0)),
                      pl.BlockSpec((1, rb, c), lambda i, w_ref: (w_ref[1], i, 0)),
                      pl.BlockSpec((3, rb, c), lambda i, w_ref: (0, i, 0))],
            out_specs=pl.BlockSpec((rb, c), lambda i, w_ref: (i, 0))),
        out_shape=jax.ShapeDtypeStruct((r, c), F32),
        compiler_params=_cp(1),
    )(where, grads, recv1, recv2)


ADA_ROWS = 16


def _silu(v):
    return v * _sigmoid(v)


def ada_fwd(cond, w_ada, b_slab, name):
    def body(c_ref, w_ref, b_ref, o_ref):
        s = _silu(c_ref[...]).astype(BF16)
        o_ref[0] = _dot(s, w_ref[0].astype(BF16)) + b_ref[0]

    return pl.pallas_call(
        body, name=name, grid=(DEPTH,),
        in_specs=[pl.BlockSpec((ADA_ROWS, D), lambda l: (0, 0)), pl.BlockSpec((1, D, ADA_SHARD), lambda l: (l, 0, 0)),
                  pl.BlockSpec((1, 1, ADA_SHARD), lambda l: (l, 0, 0))],
        out_specs=pl.BlockSpec((1, ADA_ROWS, ADA_SHARD), lambda l: (l, 0, 0)),
        out_shape=jax.ShapeDtypeStruct((DEPTH, ADA_ROWS, ADA_SHARD), F32),
        compiler_params=_cp(),
    )(cond, w_ada, b_slab)


def ada_bwd(cond, dm_sample, dm_ctx, w_ada, name):
    def body(c_ref, ds_ref, dc_ref, w_ref, gw_ref, dsc_ref):
        @pl.when(pl.program_id(0) == 0)
        def _():
            dsc_ref[...] = jnp.zeros_like(dsc_ref)

        s = _silu(c_ref[...]).astype(BF16)
        dcs = dc_ref[0]
        tot = dcs[0:1]
        for j in range(1, N_DEV):
            tot = tot + dcs[j:j + 1]
        tot8 = jnp.where(lax.broadcasted_iota(jnp.int32, (N_DEV, ADA_SHARD), 0) == 0, tot, 0.0)
        dm = jnp.concatenate([ds_ref[0], tot8], axis=0).astype(BF16)
        gw_ref[0] = _dot_tn(s, dm)
        dsc_ref[...] += _dot_nt(dm, w_ref[0].astype(BF16))[N_DEV:N_DEV + 1]

    slab = pl.BlockSpec((1, N_DEV, ADA_SHARD), lambda l: (l, 0, 0))
    wspec = pl.BlockSpec((1, D, ADA_SHARD), lambda l: (l, 0, 0))
    return pl.pallas_call(
        body, name=name, grid=(DEPTH,),
        in_specs=[pl.BlockSpec((ADA_ROWS, D), lambda l: (0, 0)), slab, slab, wspec],
        out_specs=[wspec, pl.BlockSpec((1, D), lambda l: (0, 0))],
        out_shape=[jax.ShapeDtypeStruct((DEPTH, D, ADA_SHARD), F32), jax.ShapeDtypeStruct((1, D), F32)],
        compiler_params=_cp(),
    )(cond, dm_sample, dm_ctx, w_ada)


def sum_over_devices(parts, name, silu_rows=0, w=None):
    _, r, c = parts.shape

    def body(*refs):
        p_ref, o_ref = refs[0], refs[-1]
        tot = p_ref[0]
        for j in range(1, N_DEV):
            tot = tot + p_ref[j]
        o_ref[...] = tot
        if silu_rows:
            wv = refs[1][...]
            s = _sigmoid(wv)
            o_ref[0:silu_rows, :] = tot[0:silu_rows, :] * (s * (1.0 + wv * (1.0 - s)))

    vm = pl.BlockSpec(memory_space=pltpu.VMEM)
    args = (parts,) if w is None else (parts, w)
    return pl.pallas_call(
        body, name=name, in_specs=[vm] * len(args), out_specs=vm,
        out_shape=jax.ShapeDtypeStruct((r, c), F32),
        compiler_params=pltpu.CompilerParams(vmem_limit_bytes=VMEM_LIMIT),
    )(*args)


def sum_dmods(dm_all, name):
    def body(d_ref, o_ref):
        for l in range(DEPTH):
            tot = d_ref[0, l]
            for j in range(1, N_DEV):
                tot = tot + d_ref[j, l]
            o_ref[l:l + 1, :] = tot[0:1] + tot[1:2]

    vm = pl.BlockSpec(memory_space=pltpu.VMEM)
    return pl.pallas_call(
        body, name=name, in_specs=[vm], out_specs=vm,
        out_shape=jax.ShapeDtypeStruct((DEPTH, N_MOD * D), F32),
    )(dm_all)


def adamw(w, g, m, v, name):
    r, c = w.shape
    rb = _row_block(r, c, limit=131072)

    def body(w_ref, g_ref, m_ref, v_ref, d_ref, nm_ref, nv_ref):
        g_ = g_ref[...]
        nm = B1 * m_ref[...] + (1.0 - B1) * g_
        nv = B2 * v_ref[...] + (1.0 - B2) * (g_ * g_)
        nm_ref[...] = nm
        nv_ref[...] = nv
        m_hat = nm / (1.0 - B1 ** STEP)
        v_hat = nv / (1.0 - B2 ** STEP)
        d_ref[...] = -LR * (m_hat / (jnp.sqrt(v_hat) + ADAM_EPS) + WD * w_ref[...])

    blk = pl.BlockSpec((rb, c), lambda i: (i, 0))
    shp = jax.ShapeDtypeStruct((r, c), F32)
    return pl.pallas_call(
        body, name=name, grid=(r // rb,), in_specs=[blk] * 4, out_specs=[blk] * 3, out_shape=[shp] * 3,
        compiler_params=_cp(),
    )(w, g, m, v)


def _adamw_nd(w, g, m, v, name):
    shape = w.shape
    flat = lambda a: a.reshape(-1, shape[-1])
    return tuple(o.reshape(shape) for o in adamw(flat(w), flat(g), flat(m), flat(v), name))


LANES = 128


PACK_UNIT = 8 * LANES


def _pack(arrays):
    rows = []
    for a in arrays:
        f = a.reshape(-1).astype(F32)
        pad = (-f.shape[0]) % PACK_UNIT
        if pad:
            f = jnp.concatenate([f, jnp.zeros((pad,), F32)])
        rows.append(f.reshape(-1, LANES))
    return jnp.concatenate(rows, axis=0)


def _unpack(packed, shapes):
    out, r0 = [], 0
    lead = packed.shape[:-2]
    for shp in shapes:
        size = 1
        for s in shp:
            size *= s
        nr = 8 * -(-size // PACK_UNIT)
        blk = packed[..., r0:r0 + nr, :].reshape(lead + (nr * LANES,))[..., :size]
        out.append(blk.reshape(lead + tuple(shp)))
        r0 += nr
    return out


WEIGHTS = ["c_ctx", "w_ada", "b_ada", "ffn1_norm_g", "ffn1_w_in", "ffn1_w_out", "mix_norm_g", "w_in_mix", "lru_conv_w",
           "lru_conv_b", "lru_w_r", "lru_b_r", "lru_w_i", "lru_b_i", "lru_lambda", "sgu_norm_g", "sgu_w", "sgu_b",
           "w_out_mix", "ffn2_norm_g", "ffn2_w_in", "ffn2_w_out", "final_norm_g"]
BIG = ["w_ada", "ffn1_w_in", "ffn1_w_out", "w_in_mix", "w_out_mix", "ffn2_w_in", "ffn2_w_out"]
SHARDED_SMALL = ["lru_conv_w", "lru_b_r", "lru_b_i", "lru_lambda"]
LAYER_SMALL = ["ffn1_norm_g", "mix_norm_g", "ffn2_norm_g", "lru_conv_w", "lru_conv_b", "lru_w_r", "lru_b_r", "lru_w_i",
               "lru_b_i", "lru_lambda", "sgu_norm_g", "sgu_w", "sgu_b"]
LRU_SHARD = LRU_W // N_DEV


def _widen(a):
    return jnp.moveaxis(a, 0, -2).reshape(a.shape[1:-1] + (LRU_W,))


def kernel(x, c, ctx, c_ctx, w_ada, b_ada, ffn1_norm_g, ffn1_w_in, ffn1_w_out, mix_norm_g, w_in_mix, lru_conv_w, lru_conv_b, lru_w_r, lru_b_r, lru_w_i, lru_b_i, lru_lambda, sgu_norm_g, sgu_w, sgu_b, w_out_mix, ffn2_norm_g, ffn2_w_in, ffn2_w_out, final_norm_g, loss_target, m_c_ctx, m_w_ada, m_b_ada, m_ffn1_norm_g, m_ffn1_w_in, m_ffn1_w_out, m_mix_norm_g, m_w_in_mix, m_lru_conv_w, m_lru_conv_b, m_lru_w_r, m_lru_b_r, m_lru_w_i, m_lru_b_i, m_lru_lambda, m_sgu_norm_g, m_sgu_w, m_sgu_b, m_w_out_mix, m_ffn2_norm_g, m_ffn2_w_in, m_ffn2_w_out, m_final_norm_g, v_c_ctx, v_w_ada, v_b_ada, v_ffn1_norm_g, v_ffn1_w_in, v_ffn1_w_out, v_mix_norm_g, v_w_in_mix, v_lru_conv_w, v_lru_conv_b, v_lru_w_r, v_lru_b_r, v_lru_w_i, v_lru_b_i, v_lru_lambda, v_sgu_norm_g, v_sgu_w, v_sgu_b, v_w_out_mix, v_ffn2_norm_g, v_ffn2_w_in, v_ffn2_w_out, v_final_norm_g):
    given = dict(locals())
    W = {n: given[n] for n in WEIGHTS}
    M = {n: given["m_" + n] for n in WEIGHTS}
    V = {n: given["v_" + n] for n in WEIGHTS}
    xi, yi, ci = _position()
    me = 4 * xi + 2 * yi + ci
    chip = 2 * xi + yi

    sharded_shapes = [W[n].shape for n in SHARDED_SMALL]
    got = all_gather([_pack([c[0]] + [W[n] for n in SHARDED_SMALL])], pltpu.VMEM, "gather_small")[0]
    parts = _unpack(got, [(D,)] + sharded_shapes)
    c_all = parts[0]
    wide = {n: _widen(a) for n, a in zip(SHARDED_SMALL, parts[1:])}
    cond = jnp.concatenate([c_all, c_ctx[None, :], jnp.zeros((ADA_ROWS - N_DEV - 1, D), F32)], axis=0)
    b_slab = lax.dynamic_slice_in_dim(b_ada, me * ADA_SHARD, ADA_SHARD, axis=1)[:, None, :]
    slabs = ada_fwd(cond, w_ada, b_slab, "ada_fwd")
    mall = all_gather([slabs.reshape(DEPTH * ADA_ROWS, ADA_SHARD)], pltpu.VMEM, "gather_mod")[0]
    mall = mall.reshape(N_DEV, DEPTH, ADA_ROWS, ADA_SHARD)
    m_sample = lax.dynamic_index_in_dim(mall, me, axis=2, keepdims=False)
    m_ctx = mall[:, :, N_DEV, :]
    mods = jnp.stack([jnp.transpose(m, (1, 0, 2)).reshape(DEPTH, N_MOD, D) for m in (m_ctx, m_sample)], axis=1)

    shards = []
    for l in range(DEPTH):
        shards += [pad_in_shard(ffn1_w_in[l]), pad_out_shard(ffn1_w_out[l]), w_in_mix[l], w_out_mix[l],
                   pad_in_shard(ffn2_w_in[l]), pad_out_shard(ffn2_w_out[l])]
    full = all_gather([s.astype(BF16) for s in shards], pl.ANY, "gather_weights")
    layers = []
    for l in range(DEPTH):
        f = full[6 * l:6 * l + 6]
        layer = dict(win1=f[0], wout1=f[1].reshape(HID_PAD, D), wmix=f[2], womix=f[3].reshape(D, D), win2=f[4],
                     wout2=f[5].reshape(HID_PAD, D))
        layer.update(small_layer(ffn1_norm_g[l], mix_norm_g[l], ffn2_norm_g[l], wide["lru_conv_w"][l], lru_conv_b[l],
                                 lru_w_r[l], wide["lru_b_r"][l], lru_w_i[l], wide["lru_b_i"][l], wide["lru_lambda"][l],
                                 sgu_norm_g[l], sgu_w[l], sgu_b[l]))
        layers.append(layer)

    h0 = jnp.concatenate([ctx[0], x[0]], axis=0)
    loss_blk, dh0, dmods, grads, dgf = local_fwd_bwd(h0, loss_target[0], mods, layers, final_norm_g[None, :])

    big = []
    for l in range(DEPTH):
        g = grads[l]
        big += [g["win1"], g["wout1"].reshape(N_DEV, HALF_PAD, D), g["wmix"], g["womix"].reshape(N_DEV, OMIX_SHARD, D),
                g["win2"], g["wout2"].reshape(N_DEV, HALF_PAD, D)]
    recv1 = exchange_pair(big, "rs_pair")
    c_idx = ci.reshape(1).astype(jnp.int32)
    partial = [pair_sum(big[t], recv1[t], c_idx, f"pair_sum_{t}") for t in range(len(big))]
    recv2 = exchange_chips(partial, "rs_chips")
    where = jnp.stack([me, chip]).astype(jnp.int32)
    gsum = [final_sum(big[t], recv1[t], recv2[t], where, f"final_sum_{t}") for t in range(len(big))]
    G = {
        "ffn1_w_in": jnp.stack([unpad_in_shard(gsum[6 * l]) for l in range(DEPTH)]),
        "ffn1_w_out": jnp.stack([unpad_out_shard(gsum[6 * l + 1]) for l in range(DEPTH)]),
        "w_in_mix": jnp.stack([gsum[6 * l + 2] for l in range(DEPTH)]),
        "w_out_mix": jnp.stack([gsum[6 * l + 3] for l in range(DEPTH)]),
        "ffn2_w_in": jnp.stack([unpad_in_shard(gsum[6 * l + 4]) for l in range(DEPTH)]),
        "ffn2_w_out": jnp.stack([unpad_out_shard(gsum[6 * l + 5]) for l in range(DEPTH)]),
    }

    n_rows = DEPTH * 2 * N_MOD
    dm_rows = jnp.concatenate([dmods.reshape(n_rows, D), jnp.zeros((-n_rows % 8, D), F32)], axis=0)
    dm_all = all_gather([dm_rows], pltpu.VMEM, "gather_dmod")[0][:, :n_rows]
    dm_all = dm_all.reshape(N_DEV, DEPTH, 2, N_MOD * D)
    mine = lax.dynamic_slice_in_dim(dm_all, me * ADA_SHARD, ADA_SHARD, axis=3)
    G["w_ada"], dsc = ada_bwd(cond, jnp.transpose(mine[:, :, 1, :], (1, 0, 2)), jnp.transpose(mine[:, :, 0, :], (1, 0, 2)),
                              w_ada, "ada_bwd")
    G["b_ada"] = sum_dmods(dm_all, "sum_dmods")

    per_layer = [small_grads(grads[l]) for l in range(DEPTH)]
    small = [jnp.stack([per_layer[l][n] for l in range(DEPTH)]) for n in LAYER_SMALL]
    small_shapes = [(D,), (D,)] + [s.shape for s in small]
    got = all_gather([_pack([dsc[0], dgf[0]] + small)], pltpu.VMEM, "gather_small_grads")[0]
    summed = _unpack(sum_over_devices(got, "sum_small_grads", silu_rows=D // LANES, w=c_ctx.reshape(D // LANES, LANES)),
                     small_shapes)
    G["c_ctx"], G["final_norm_g"] = summed[0], summed[1]
    for n, a in zip(LAYER_SMALL, summed[2:]):
        G[n] = lax.dynamic_slice_in_dim(a, me * LRU_SHARD, LRU_SHARD, axis=a.ndim - 1) if n in SHARDED_SMALL else a

    delta, new_m, new_v = {}, {}, {}
    for n in BIG:
        delta[n], new_m[n], new_v[n] = _adamw_nd(W[n], G[n], M[n], V[n], f"adamw_{n}")
    rest = [n for n in WEIGHTS if n not in BIG]
    shapes = [W[n].shape for n in rest]
    outs = adamw(*[_pack([src[n] for n in rest]) for src in (W, G, M, V)], "adamw_small")
    for dst, packed in zip((delta, new_m, new_v), outs):
        for n, a in zip(rest, _unpack(packed, shapes)):
            dst[n] = a

    loss = lax.psum(loss_blk[0, 0], ("x", "y", "c"))
    grad_x = dh0[CTX:][None]
    return (loss, grad_x, *[G[n] for n in WEIGHTS], *[delta[n] for n in WEIGHTS], *[new_m[n] for n in WEIGHTS],
            *[new_v[n] for n in WEIGHTS])
```

```python
import functools

import jax
import jax.numpy as jnp
from jax import lax
from jax.experimental import pallas as pl
from jax.experimental.pallas import tpu as pltpu

F32 = jnp.float32
BF16 = jnp.bfloat16

D = 1024
CTX = 256
DEPTH = 2
EPS = 1e-6
D_FF = 2816
LRU_W = 512
HEADS = 8
HEAD_DIM = 64
CONV_W = 4
RG_C = 8.0
GROUPS = 4
GROUP_DIM = 128
CHUNK = 128
MLP_W = 512
IN_PROJ = 2048
N_MOD = 9
N_DEV = 8

LR = 0.001
B1 = 0.9
B2 = 0.999
ADAM_EPS = 1e-08
WD = 0.01
STEP = 10

SHARD_FF_IN = 704
HALF = 352
HALF_PAD = 384
SHARD_PAD = 2 * HALF_PAD
HID_PAD = 4 * SHARD_PAD
N_MIX_SHARD = IN_PROJ // N_DEV
OMIX_SHARD = D // N_DEV
ADA_SHARD = N_MOD * D // N_DEV

TM = 256
HALO = 8
VMEM_LIMIT = 60 * 1024 * 1024

MESH = pl.DeviceIdType.MESH
ANY = pl.BlockSpec(memory_space=pl.ANY)


def _cp(n_axes=1):
    return pltpu.CompilerParams(dimension_semantics=("arbitrary",) * n_axes, vmem_limit_bytes=VMEM_LIMIT)


def _position():
    return lax.axis_index("x"), lax.axis_index("y"), lax.axis_index("c")


class GatherRider:
    def __init__(self, shards):
        n = len(shards)
        self.n = n
        self.ins = list(shards)
        self.out_shape = [jax.ShapeDtypeStruct((N_DEV,) + s.shape, s.dtype) for s in shards]
        self.sems = [pltpu.SemaphoreType.DMA((n, 7)), pltpu.SemaphoreType.DMA((n, 7)), pltpu.SemaphoreType.DMA((n,))]

    def _ctx(self, outs, sems):
        x, y, c = _position()
        chips = [(1 - x, y), (x, 1 - y), (1 - x, 1 - y)]

        def copy(t, k, block, to, src=None):
            dst = outs[t].at[4 * block[0] + 2 * block[1] + block[2]]
            return pltpu.make_async_remote_copy(
                src_ref=dst if src is None else src, dst_ref=dst, send_sem=sems[0].at[t, k],
                recv_sem=sems[1].at[t, k], device_id=to, device_id_type=MESH)

        return (x, y, c), (x, y, 1 - c), chips, copy

    def _local(self, ins, outs, sems, t):
        x, y, c = _position()
        return pltpu.make_async_copy(ins[t], outs[t].at[4 * x + 2 * y + c], sems[2].at[t])

    def _first(self, ins, outs, sems, t):
        me, sibling, chips, copy = self._ctx(outs, sems)
        return [copy(t, 0, me, sibling, src=ins[t])] + [copy(t, 1 + j, me, (*chip, me[2]), src=ins[t])
                                                         for j, chip in enumerate(chips)]

    def start(self, ins, outs, sems):
        for t in range(self.n):
            self._local(ins, outs, sems, t).start()
            for cp in self._first(ins, outs, sems, t):
                cp.start()

    def mid(self, ins, outs, sems):
        me, sibling, chips, copy = self._ctx(outs, sems)
        for j, chip in enumerate(chips):
            for t in range(self.n):
                copy(t, 1 + j, (*chip, me[2]), me).wait_recv()
                copy(t, 4 + j, (*chip, me[2]), sibling).start()

    def finish(self, ins, outs, sems):
        me, sibling, chips, copy = self._ctx(outs, sems)
        for t in range(self.n):
            copy(t, 0, sibling, me).wait_recv()
            for j, chip in enumerate(chips):
                copy(t, 4 + j, (*chip, 1 - me[2]), me).wait_recv()
        for t in range(self.n):
            for cp in self._first(ins, outs, sems, t):
                cp.wait_send()
            for j, chip in enumerate(chips):
                copy(t, 4 + j, (*chip, me[2]), sibling).wait_send()
            self._local(ins, outs, sems, t).wait()


class ExchangeRider:
    def __init__(self, tensors, plan, n_slots):
        n = len(tensors)
        self.n, self.plan = n, plan
        self.ins = list(tensors)
        self.out_shape = [jax.ShapeDtypeStruct((n_slots,) + s.shape[1:], s.dtype) for s in tensors]
        self.sems = [pltpu.SemaphoreType.DMA((n, n_slots)), pltpu.SemaphoreType.DMA((n, n_slots))]

    def _copies(self, ins, outs, sems):
        return [pltpu.make_async_remote_copy(
            src_ref=ins[t].at[block], dst_ref=outs[t].at[k], send_sem=sems[0].at[t, k], recv_sem=sems[1].at[t, k],
            device_id=to, device_id_type=MESH)
            for t in range(self.n) for k, (block, to) in enumerate(self.plan(*_position()))]

    def start(self, ins, outs, sems):
        for cp in self._copies(ins, outs, sems):
            cp.start()

    def mid(self, ins, outs, sems):
        pass

    def finish(self, ins, outs, sems):
        for cp in self._copies(ins, outs, sems):
            cp.wait()


def pair_rider(grads):
    def plan(x, y, c):
        return [(4 * cx + 2 * cy + (1 - c), (x, y, 1 - c)) for cx in range(2) for cy in range(2)]
    return ExchangeRider(grads, plan, 4)


def chips_rider(parts):
    def plan(x, y, c):
        return [(2 * cx + cy, (cx, cy, c)) for cx, cy in [(1 - x, y), (x, 1 - y), (1 - x, 1 - y)]]
    return ExchangeRider(parts, plan, 3)


def run_alone(rider, space, name):
    ni = len(rider.ins)
    no = len(rider.out_shape)

    def body(*refs):
        ins, outs, sems = refs[:ni], refs[ni:ni + no], refs[ni + no:]
        rider.start(ins, outs, sems)
        rider.mid(ins, outs, sems)
        rider.finish(ins, outs, sems)

    spec = pl.BlockSpec(memory_space=space)
    return pl.pallas_call(
        body, name=name, in_specs=[spec] * ni, out_specs=[spec] * no, out_shape=rider.out_shape,
        scratch_shapes=rider.sems, compiler_params=pltpu.CompilerParams(vmem_limit_bytes=VMEM_LIMIT),
    )(*rider.ins)


def _grid_call(body, *, name, nsteps, in_specs, out_specs, out_shape, scratch_shapes, args, rider=None):
    if rider is None:
        outs = pl.pallas_call(body, name=name, grid=(nsteps,), in_specs=in_specs, out_specs=out_specs,
                              out_shape=out_shape, scratch_shapes=scratch_shapes, compiler_params=_cp())(*args)
        return outs, []
    ni, no, ns = len(in_specs), len(out_specs), len(scratch_shapes)
    ri, ro = len(rider.ins), len(rider.out_shape)

    def wrapped(*refs):
        ins, refs = refs[:ni], refs[ni:]
        r_ins, refs = refs[:ri], refs[ri:]
        outs, refs = refs[:no], refs[no:]
        r_outs, refs = refs[:ro], refs[ro:]
        scratch, r_sems = refs[:ns], refs[ns:]
        s = pl.program_id(0)

        @pl.when(s == 0)
        def _():
            rider.start(r_ins, r_outs, r_sems)

        body(*ins, *outs, *scratch)

        @pl.when(s == nsteps // 2)
        def _():
            rider.mid(r_ins, r_outs, r_sems)

        @pl.when(s == nsteps - 1)
        def _():
            rider.finish(r_ins, r_outs, r_sems)

    outs = pl.pallas_call(
        wrapped, name=name, grid=(nsteps,), in_specs=list(in_specs) + [ANY] * ri, out_specs=list(out_specs) + [ANY] * ro,
        out_shape=list(out_shape) + rider.out_shape, scratch_shapes=list(scratch_shapes) + rider.sems,
        compiler_params=_cp())(*args, *rider.ins)
    return outs[:no], outs[no:]


def _dot(a, b):
    return jnp.dot(a, b, preferred_element_type=F32)


def _dot_nt(a, b):
    return lax.dot_general(a, b, (((1,), (1,)), ((), ())), preferred_element_type=F32)


def _dot_tn(a, b):
    return lax.dot_general(a, b, (((0,), (0,)), ((), ())), preferred_element_type=F32)


def _sigmoid(x):
    return 1.0 / (1.0 + jnp.exp(-x))


def _is_ctx(i, tm):
    row = i * tm + lax.broadcasted_iota(jnp.int32, (tm, 1), 0)
    return row < CTX


def _sel(is_ctx, mod_ref, k):
    return jnp.where(is_ctx, mod_ref[0, k:k + 1, :], mod_ref[1, k:k + 1, :])


def _acc2(ref, k, val, is_ctx):
    vc = jnp.sum(jnp.where(is_ctx, val, 0.0), axis=0, keepdims=True)
    va = jnp.sum(val, axis=0, keepdims=True)
    ref[0, k:k + 1, :] += vc
    ref[1, k:k + 1, :] += va - vc


def _norm_mod(h, g, shift, scale):
    r = lax.rsqrt(jnp.mean(h * h, axis=-1, keepdims=True) + EPS)
    n = h * r
    return (n * g) * (1.0 + scale) + shift, n, r


def _norm_mod_bwd(dz, n, r, g, scale):
    dn = dz * (g * (1.0 + scale))
    return r * (dn - n * jnp.mean(dn * n, axis=-1, keepdims=True))


def ffn_fwd(h, mod, g, win, wout, name, rider=None):
    split = isinstance(h, tuple)
    nc = CTX // TM
    t = h[0].shape[0] + h[1].shape[0] if split else h.shape[0]

    def body(*refs):
        if split:
            c_ref, x_ref, mod_ref, g_ref, win_hbm, wout_hbm, out_ref, gu_ref, acc_ref, h0_ref, win_v, wout_v = refs
        else:
            h_ref, mod_ref, g_ref, win_hbm, wout_hbm, out_ref, gu_ref, acc_ref, win_v, wout_v = refs
        i = pl.program_id(0)

        @pl.when(i == 0)
        def _():
            pltpu.sync_copy(win_hbm, win_v)
            pltpu.sync_copy(wout_hbm, wout_v)

        if split:
            hh = jnp.where(i < nc, c_ref[...], x_ref[...])
            h0_ref[...] = hh
        else:
            hh = h_ref[...]
        ic = _is_ctx(i, TM)
        z, _, _ = _norm_mod(hh, g_ref[...], _sel(ic, mod_ref, 0), _sel(ic, mod_ref, 1))
        zb = z.astype(BF16)
        acc = jnp.zeros((TM, D), F32)
        for dd in range(4):
            gg = _dot(zb, win_v[dd])
            uu = _dot(zb, win_v[dd + 4])
            gu_ref[:, dd * SHARD_PAD:(dd + 1) * SHARD_PAD] = gg.astype(BF16)
            gu_ref[:, (dd + 4) * SHARD_PAD:(dd + 5) * SHARD_PAD] = uu.astype(BF16)
            a = (gg * _sigmoid(gg)) * uu
            acc = acc + _dot(a.astype(BF16), wout_v[dd * SHARD_PAD:(dd + 1) * SHARD_PAD, :])
        acc_ref[...] = acc
        out_ref[...] = hh + (0.5 * _sel(ic, mod_ref, 2)) * acc

    row = pl.BlockSpec((TM, D), lambda i: (i, 0))
    rshape = jax.ShapeDtypeStruct((t, D), F32)
    if split:
        rows_in = [pl.BlockSpec((TM, D), lambda i: (jnp.minimum(i, nc - 1), 0)),
                   pl.BlockSpec((TM, D), lambda i: (jnp.maximum(i - nc, 0), 0))]
    else:
        rows_in = [row]
    return _grid_call(
        body, name=name, nsteps=t // TM,
        in_specs=rows_in + [pl.BlockSpec((2, 3, D), lambda i: (0, 0, 0)), pl.BlockSpec((1, D), lambda i: (0, 0)), ANY, ANY],
        out_specs=[row, pl.BlockSpec((TM, 2 * HID_PAD), lambda i: (i, 0)), row] + ([row] if split else []),
        out_shape=[rshape, jax.ShapeDtypeStruct((t, 2 * HID_PAD), BF16), rshape] + ([rshape] if split else []),
        scratch_shapes=[pltpu.VMEM((N_DEV, D, SHARD_PAD), BF16), pltpu.VMEM((HID_PAD, D), BF16)],
        args=(*(h if split else (h,)), mod, g, win, wout), rider=rider)


def ffn_bwd_a(dy, acc, gu, mod, wout, name, rider=None):
    t = dy.shape[0]
    nt = t // TM

    def body(dy_ref, acc_ref, gu_ref, mod_ref, wout_hbm, dp_ref, dwout_hbm, dgate_ref, wout_v, dwout_v):
        i = pl.program_id(0)

        @pl.when(i == 0)
        def _():
            pltpu.sync_copy(wout_hbm, wout_v)
            dwout_v[...] = jnp.zeros_like(dwout_v)
            dgate_ref[...] = jnp.zeros_like(dgate_ref)

        dy_ = dy_ref[...]
        ic = _is_ctx(i, TM)
        _acc2(dgate_ref, 0, 0.5 * dy_ * acc_ref[...], ic)
        daccb = ((0.5 * _sel(ic, mod_ref, 2)) * dy_).astype(BF16)
        for dd in range(4):
            blk = slice(dd * SHARD_PAD, (dd + 1) * SHARD_PAD)
            ublk = slice((dd + 4) * SHARD_PAD, (dd + 5) * SHARD_PAD)
            da = _dot_nt(daccb, wout_v[blk, :])
            gg = gu_ref[:, blk].astype(F32)
            uu = gu_ref[:, ublk].astype(F32)
            s = _sigmoid(gg)
            sl = gg * s
            dwout_v[blk, :] += _dot_tn((sl * uu).astype(BF16), daccb)
            dp_ref[:, blk] = (da * uu * (s + sl * (1.0 - s))).astype(BF16)
            dp_ref[:, ublk] = (da * sl).astype(BF16)

        @pl.when(i == nt - 1)
        def _():
            pltpu.sync_copy(dwout_v, dwout_hbm)

    row = pl.BlockSpec((TM, D), lambda i: (i, 0))
    wide = pl.BlockSpec((TM, 2 * HID_PAD), lambda i: (i, 0))
    return _grid_call(
        body, name=name, nsteps=nt,
        in_specs=[row, row, wide, pl.BlockSpec((2, 3, D), lambda i: (0, 0, 0)), ANY],
        out_specs=[wide, ANY, pl.BlockSpec((2, 1, D), lambda i: (0, 0, 0))],
        out_shape=[jax.ShapeDtypeStruct((t, 2 * HID_PAD), BF16), jax.ShapeDtypeStruct((HID_PAD, D), F32),
                   jax.ShapeDtypeStruct((2, 1, D), F32)],
        scratch_shapes=[pltpu.VMEM((HID_PAD, D), BF16), pltpu.VMEM((HID_PAD, D), F32)],
        args=(dy, acc, gu, mod, wout), rider=rider)


def ffn_bwd_b(dy, h, dp, mod, g, win, name, rider=None, latent_only=False):
    t = dy.shape[0]
    nt = t // TM
    nc = CTX // TM

    def body(dy_ref, h_ref, dp_ref, mod_ref, g_ref, win_hbm, dh_ref, dwin_hbm, dmod_ref, dg_ref, win_v, dwin_v):
        i = pl.program_id(0)

        @pl.when(i == 0)
        def _():
            pltpu.sync_copy(win_hbm, win_v)
            dwin_v[...] = jnp.zeros_like(dwin_v)
            dmod_ref[...] = jnp.zeros_like(dmod_ref)
            dg_ref[...] = jnp.zeros_like(dg_ref)

        ic = _is_ctx(i, TM)
        gain = g_ref[...]
        scale = _sel(ic, mod_ref, 1)
        z, n, r = _norm_mod(h_ref[...], gain, _sel(ic, mod_ref, 0), scale)
        zb = z.astype(BF16)
        dz = jnp.zeros((TM, D), F32)
        for dd in range(N_DEV):
            dpd = dp_ref[:, dd * SHARD_PAD:(dd + 1) * SHARD_PAD]
            dz = dz + _dot_nt(dpd, win_v[dd])
            dwin_v[dd] += _dot_tn(zb, dpd)
        _acc2(dmod_ref, 0, dz, ic)
        _acc2(dmod_ref, 1, dz * (n * gain), ic)
        dg_ref[...] += jnp.sum(dz * (1.0 + scale) * n, axis=0, keepdims=True)
        dh_ref[...] = dy_ref[...] + _norm_mod_bwd(dz, n, r, gain, scale)

        @pl.when(i == nt - 1)
        def _():
            pltpu.sync_copy(dwin_v, dwin_hbm)

    row = pl.BlockSpec((TM, D), lambda i: (i, 0))
    if latent_only:
        dh_spec = pl.BlockSpec((TM, D), lambda i: (jnp.maximum(i - nc, 0), 0))
        dh_shape = jax.ShapeDtypeStruct((t - CTX, D), F32)
    else:
        dh_spec, dh_shape = row, jax.ShapeDtypeStruct((t, D), F32)
    return _grid_call(
        body, name=name, nsteps=nt,
        in_specs=[row, row, pl.BlockSpec((TM, 2 * HID_PAD), lambda i: (i, 0)),
                  pl.BlockSpec((2, 3, D), lambda i: (0, 0, 0)), pl.BlockSpec((1, D), lambda i: (0, 0)), ANY],
        out_specs=[dh_spec, ANY, pl.BlockSpec((2, 2, D), lambda i: (0, 0, 0)), pl.BlockSpec((1, D), lambda i: (0, 0))],
        out_shape=[dh_shape, jax.ShapeDtypeStruct((N_DEV, D, SHARD_PAD), F32),
                   jax.ShapeDtypeStruct((2, 2, D), F32), jax.ShapeDtypeStruct((1, D), F32)],
        scratch_shapes=[pltpu.VMEM((N_DEV, D, SHARD_PAD), BF16), pltpu.VMEM((N_DEV, D, SHARD_PAD), F32)],
        args=(dy, h, dp, mod, g, win), rider=rider)


def mix_in_fwd(h, mod, g, wmix, name):
    t = h.shape[0]

    def body(h_ref, mod_ref, g_ref, w_hbm, p_ref, w_v):
        i = pl.program_id(0)

        @pl.when(i == 0)
        def _():
            pltpu.sync_copy(w_hbm, w_v)

        ic = _is_ctx(i, TM)
        z, _, _ = _norm_mod(h_ref[...], g_ref[...], _sel(ic, mod_ref, 0), _sel(ic, mod_ref, 1))
        zb = z.astype(BF16)
        for dd in range(N_DEV):
            p_ref[:, dd * N_MIX_SHARD:(dd + 1) * N_MIX_SHARD] = _dot(zb, w_v[dd])

    return pl.pallas_call(
        body, name=name, grid=(t // TM,),
        in_specs=[pl.BlockSpec((TM, D), lambda i: (i, 0)), pl.BlockSpec((2, 3, D), lambda i: (0, 0, 0)),
                  pl.BlockSpec((1, D), lambda i: (0, 0)), ANY],
        out_specs=pl.BlockSpec((TM, IN_PROJ), lambda i: (i, 0)),
        out_shape=jax.ShapeDtypeStruct((t, IN_PROJ), F32),
        scratch_shapes=[pltpu.VMEM((N_DEV, D, N_MIX_SHARD), BF16)],
        compiler_params=_cp(),
    )(h, mod, g, wmix)


def _halo_specs(nt, tile_of):
    nb = nt * (TM // HALO)
    main = pl.BlockSpec((TM, LRU_W), lambda s: (tile_of(s), 0))
    prev = pl.BlockSpec((HALO, LRU_W), lambda s: (jnp.maximum(tile_of(s) * (TM // HALO) - 1, 0), 0))
    nxt = pl.BlockSpec((HALO, LRU_W), lambda s: (jnp.minimum((tile_of(s) + 1) * (TM // HALO), nb - 1), 0))
    return main, prev, nxt


def _ext(tile, nt, main, prev, nxt):
    has_prev = jnp.logical_and(tile != 0, tile != 1)
    has_next = jnp.logical_and(tile != 0, tile != nt - 1)
    return jnp.concatenate([jnp.where(has_prev, prev, 0.0), main, jnp.where(has_next, nxt, 0.0)], axis=0)


def _shifted(ext, off):
    n = ext.shape[0]
    return pltpu.roll(ext, (-off) % n, 0)[HALO:HALO + TM]


def _conv(ext, cw_ref, cb_ref):
    xc = cb_ref[...] + cw_ref[0:1, :] * _shifted(ext, -2)
    for k in range(1, CONV_W):
        xc = xc + cw_ref[k:k + 1, :] * _shifted(ext, k - 2)
    return xc


def _log1p(y):
    return jnp.where(y < 1e-2, y * (1.0 - y * (0.5 - y * (1.0 / 3.0 - 0.25 * y))), jnp.log(1.0 + y))


def _softplus_neg(lam):
    return jnp.maximum(-lam, 0.0) + _log1p(jnp.exp(-jnp.abs(lam)))


def _expm1(x):
    p = x * (1.0 + x * (1 / 2 + x * (1 / 6 + x * (1 / 24 + x * (1 / 120 + x * (1 / 720 + x * (1 / 5040)))))))
    return jnp.where(jnp.abs(x) < 0.25, p, jnp.exp(x) - 1.0)


def _gates(xc, wr, br, wi, bi, lam):
    xb = xc.astype(BF16)
    r = _sigmoid(_dot(xb, wr) + br)
    ig = _sigmoid(_dot(xb, wi) + bi)
    sp = _softplus_neg(lam)
    log_a = -RG_C * r * sp
    a = jnp.exp(log_a)
    mult = jnp.sqrt(-_expm1(2.0 * log_a))
    return r, ig, sp, a, mult


def _scan(a, b, reverse):
    n = a.shape[0]
    row = lax.broadcasted_iota(jnp.int32, a.shape, 0)
    s = 1
    while s < n:
        if reverse:
            keep = row < n - s
            a_s = jnp.where(keep, pltpu.roll(a, n - s, 0), 1.0)
            b_s = jnp.where(keep, pltpu.roll(b, n - s, 0), 0.0)
        else:
            keep = row >= s
            a_s = jnp.where(keep, pltpu.roll(a, s, 0), 1.0)
            b_s = jnp.where(keep, pltpu.roll(b, s, 0), 0.0)
        b = a * b_s + b
        a = a * a_s
        s *= 2
    return a, b


def lru_fwd(p, conv_w, conv_b, wr, br, wi, bi, lam, reverse, name):
    t = p.shape[0]
    nt = t // TM

    def tile_of(s):
        return jnp.where(s == 0, 0, nt - s) if reverse else s

    def body(x_ref, xp_ref, xn_ref, cw_ref, cb_ref, wr_ref, br_ref, wi_ref, bi_ref, lam_ref, h_ref, carry):
        s = pl.program_id(0)
        tile = tile_of(s)

        @pl.when(s == 0)
        def _():
            carry[...] = jnp.zeros_like(carry)

        ext = _ext(tile, nt, x_ref[...], xp_ref[...], xn_ref[...])
        xc = _conv(ext, cw_ref, cb_ref)
        _, ig, _, a, mult = _gates(xc, wr_ref[...], br_ref[...], wi_ref[...], bi_ref[...], lam_ref[...])
        a_cum, hl = _scan(a, mult * (ig * xc), reverse)
        hh = hl + a_cum * carry[...]
        h_ref[...] = hh
        carry[...] = hh[0:1, :] if reverse else hh[TM - 1:TM, :]

    main, prev, nxt = _halo_specs(nt, tile_of)
    vec = pl.BlockSpec((1, LRU_W), lambda s: (0, 0))
    mat = pl.BlockSpec((LRU_W, LRU_W), lambda s: (0, 0))
    return pl.pallas_call(
        body, name=name, grid=(nt,),
        in_specs=[main, prev, nxt, pl.BlockSpec((CONV_W, LRU_W), lambda s: (0, 0)), vec, mat, vec, mat, vec, vec],
        out_specs=main,
        out_shape=jax.ShapeDtypeStruct((t, LRU_W), F32),
        scratch_shapes=[pltpu.VMEM((1, LRU_W), F32)],
        compiler_params=_cp(),
    )(p, p, p, conv_w, conv_b, wr, br, wi, bi, lam)


def lru_bwd(p, hs, dhs, conv_w, conv_b, wr, br, wi, bi, lam, reverse, name):
    t = p.shape[0]
    nt = t // TM
    bpt = TM // HALO

    def tile_of(s):
        return jnp.where(s == nt - 1, 0, s + 1) if reverse else nt - 1 - s

    def hprev_block(s):
        tile = tile_of(s)
        if reverse:
            return (jnp.where(tile == nt - 1, 0, jnp.minimum((tile + 1) * bpt, nt * bpt - 1)), 0)
        return (jnp.maximum(tile * bpt - 1, 0), 0)

    def body(x_ref, xp_ref, xn_ref, h_ref, hp_ref, dh_ref, cw_ref, cb_ref, wr_ref, br_ref, wi_ref, bi_ref, lam_ref,
             dxc_ref, dwr_ref, dwi_ref, dbr_ref, dbi_ref, dlam_ref, carry):
        s = pl.program_id(0)
        tile = tile_of(s)

        @pl.when(s == 0)
        def _():
            carry[...] = jnp.zeros_like(carry)
            for ref in (dwr_ref, dwi_ref, dbr_ref, dbi_ref, dlam_ref):
                ref[...] = jnp.zeros_like(ref)

        ext = _ext(tile, nt, x_ref[...], xp_ref[...], xn_ref[...])
        xc = _conv(ext, cw_ref, cb_ref)
        wr_, wi_ = wr_ref[...], wi_ref[...]
        r, ig, sp, a, mult = _gates(xc, wr_, br_ref[...], wi_, bi_ref[...], lam_ref[...])
        gated = ig * xc
        row = lax.broadcasted_iota(jnp.int32, (TM, LRU_W), 0)
        hh = h_ref[...]
        start = jnp.where(tile != 0, hp_ref[0:1, :] if reverse else hp_ref[HALO - 1:HALO, :], 0.0)
        if reverse:
            edge = row == TM - 1
            hprev = jnp.where(edge, start, pltpu.roll(hh, TM - 1, 0))
            coef = jnp.where(row == 0, 0.0, pltpu.roll(a, 1, 0))
            bb = dh_ref[...] + jnp.where(row == 0, carry[...], 0.0)
        else:
            edge = row == 0
            hprev = jnp.where(edge, start, pltpu.roll(hh, 1, 0))
            coef = jnp.where(row == TM - 1, 0.0, pltpu.roll(a, TM - 1, 0))
            bb = dh_ref[...] + jnp.where(row == TM - 1, carry[...], 0.0)
        _, lmb = _scan(coef, bb, not reverse)
        al = a * lmb
        carry[...] = al[TM - 1:TM, :] if reverse else al[0:1, :]

        dgated = lmb * mult
        dloga = (lmb * hprev) * a - (lmb * gated) * (a * a) / mult
        dpre_r = (dloga * (-RG_C * sp)) * r * (1.0 - r)
        dpre_i = (dgated * xc) * ig * (1.0 - ig)
        drb, dib = dpre_r.astype(BF16), dpre_i.astype(BF16)
        xb = xc.astype(BF16)
        dxc_ref[...] = dgated * ig + _dot_nt(drb, wr_) + _dot_nt(dib, wi_)
        dwr_ref[...] += _dot_tn(xb, drb)
        dwi_ref[...] += _dot_tn(xb, dib)
        dbr_ref[...] += jnp.sum(dpre_r, axis=0, keepdims=True)
        dbi_ref[...] += jnp.sum(dpre_i, axis=0, keepdims=True)
        dlam_ref[...] += jnp.sum(dloga * (-RG_C * r), axis=0, keepdims=True)

        @pl.when(s == nt - 1)
        def _():
            dlam_ref[...] = dlam_ref[...] * (-_sigmoid(-lam_ref[...]))

    main, prev, nxt = _halo_specs(nt, tile_of)
    vec = pl.BlockSpec((1, LRU_W), lambda s: (0, 0))
    mat = pl.BlockSpec((LRU_W, LRU_W), lambda s: (0, 0))
    vshape = jax.ShapeDtypeStruct((1, LRU_W), F32)
    mshape = jax.ShapeDtypeStruct((LRU_W, LRU_W), F32)
    return pl.pallas_call(
        body, name=name, grid=(nt,),
        in_specs=[main, prev, nxt, main, pl.BlockSpec((HALO, LRU_W), hprev_block), main,
                  pl.BlockSpec((CONV_W, LRU_W), lambda s: (0, 0)), vec, mat, vec, mat, vec, vec],
        out_specs=[main, mat, mat, vec, vec, vec],
        out_shape=[jax.ShapeDtypeStruct((t, LRU_W), F32), mshape, mshape, vshape, vshape, vshape],
        scratch_shapes=[pltpu.VMEM((1, LRU_W), F32)],
        compiler_params=_cp(),
    )(p, p, p, hs, hs, dhs, conv_w, conv_b, wr, br, wi, bi, lam)


GELU_C = 0.7978845608028654
GELU_A = 0.044715


def _gelu(x):
    th = jnp.tanh(GELU_C * (x + GELU_A * x * x * x))
    return 0.5 * x * (1.0 + th), th


def _sgu(v, gain, w_ref, bt_ref):
    mu = jnp.mean(v, axis=-1, keepdims=True)
    xc = v - mu
    rs = lax.rsqrt(jnp.mean(xc * xc, axis=-1, keepdims=True) + EPS)
    vhat = xc * rs
    vnb = (vhat * gain).astype(BF16)
    chunks = []
    for ch in range(TM // CHUNK):
        zs = []
        for gi in range(GROUPS):
            vb = vnb[ch * CHUNK:(ch + 1) * CHUNK, gi * GROUP_DIM:(gi + 1) * GROUP_DIM]
            zs.append(_dot(w_ref[gi].astype(BF16), vb) + bt_ref[:, gi:gi + 1])
        chunks.append(jnp.concatenate(zs, axis=1))
    return jnp.concatenate(chunks, axis=0), vhat, rs, vnb


def _pcols(k):
    return pl.BlockSpec((TM, LRU_W), lambda i: (i, k))


def mix_out_fwd(h, p, hf, hb, mod, sgu_g, sgu_w, sgu_bt, womix, name):
    t = h.shape[0]

    def body(h_ref, gl_ref, u_ref, v_ref, hf_ref, hb_ref, mod_ref, sg_ref, sw_ref, sb_ref, w_hbm, out_ref, o_ref, w_v):
        i = pl.program_id(0)

        @pl.when(i == 0)
        def _():
            pltpu.sync_copy(w_hbm, w_v)

        ic = _is_ctx(i, TM)
        ge, _ = _gelu(gl_ref[...])
        y_lru = (hf_ref[...] + hb_ref[...]) * ge
        z, _, _, _ = _sgu(v_ref[...], sg_ref[...], sw_ref, sb_ref)
        y = jnp.concatenate([y_lru, u_ref[...] * z], axis=1).astype(BF16)
        o = _dot(y, w_v[...])
        o_ref[...] = o
        out_ref[...] = h_ref[...] + _sel(ic, mod_ref, 2) * o

    row = pl.BlockSpec((TM, D), lambda i: (i, 0))
    half = pl.BlockSpec((TM, LRU_W), lambda i: (i, 0))
    return pl.pallas_call(
        body, name=name, grid=(t // TM,),
        in_specs=[row, _pcols(1), _pcols(2), _pcols(3), half, half, pl.BlockSpec((2, 3, D), lambda i: (0, 0, 0)),
                  pl.BlockSpec((1, MLP_W), lambda i: (0, 0)), pl.BlockSpec((GROUPS, CHUNK, CHUNK), lambda i: (0, 0, 0)),
                  pl.BlockSpec((CHUNK, GROUPS), lambda i: (0, 0)), ANY],
        out_specs=[row, row],
        out_shape=[jax.ShapeDtypeStruct((t, D), F32), jax.ShapeDtypeStruct((t, D), F32)],
        scratch_shapes=[pltpu.VMEM((D, D), BF16)],
        compiler_params=_cp(),
    )(h, p, p, p, hf, hb, mod, sgu_g, sgu_w, sgu_bt, womix)


def mix_out_bwd(dy, p, hf, hb, o, mod, sgu_g, sgu_w, sgu_bt, womix, name, rider=None):
    t = dy.shape[0]

    def body(dy_ref, gl_ref, u_ref, v_ref, hf_ref, hb_ref, o_ref, mod_ref, sg_ref, sw_ref, sb_ref, w_hbm,
             dhs_ref, dp_ref, dw_ref, dgate_ref, dsg_ref, dsw_ref, dsb_ref, w_v):
        i = pl.program_id(0)

        @pl.when(i == 0)
        def _():
            pltpu.sync_copy(w_hbm, w_v)
            for ref in (dw_ref, dgate_ref, dsg_ref, dsw_ref, dsb_ref):
                ref[...] = jnp.zeros_like(ref)

        ic = _is_ctx(i, TM)
        dy_ = dy_ref[...]
        _acc2(dgate_ref, 0, dy_ * o_ref[...], ic)
        dob = (_sel(ic, mod_ref, 2) * dy_).astype(BF16)

        gl = gl_ref[...]
        ge, th = _gelu(gl)
        hsum = hf_ref[...] + hb_ref[...]
        gain = sg_ref[...]
        uu = u_ref[...]
        z, vhat, rs, vnb = _sgu(v_ref[...], gain, sw_ref, sb_ref)
        y = jnp.concatenate([hsum * ge, uu * z], axis=1).astype(BF16)
        dw_ref[...] += _dot_tn(y, dob)
        dyy = _dot_nt(dob, w_v[...])
        dyl, dys = dyy[:, :LRU_W], dyy[:, LRU_W:]

        dhs_ref[...] = dyl * ge
        dge = 0.5 * (1.0 + th) + 0.5 * gl * (1.0 - th * th) * (GELU_C * (1.0 + 3.0 * GELU_A * gl * gl))
        dp_ref[:, 0:LRU_W] = dyl * hsum * dge
        dp_ref[:, LRU_W:2 * LRU_W] = dys * z

        dz = dys * uu
        dzb = dz.astype(BF16)
        dvn_chunks, dsb_cols = [], [jnp.zeros((CHUNK, 1), F32)] * GROUPS
        for ch in range(TM // CHUNK):
            cols = []
            for gi in range(GROUPS):
                rs_, cs_ = slice(ch * CHUNK, (ch + 1) * CHUNK), slice(gi * GROUP_DIM, (gi + 1) * GROUP_DIM)
                dzg = dzb[rs_, cs_]
                dsb_cols[gi] = dsb_cols[gi] + jnp.sum(dz[rs_, cs_], axis=1, keepdims=True)
                dsw_ref[gi] += _dot_nt(dzg, vnb[rs_, cs_])
                cols.append(_dot_tn(sw_ref[gi].astype(BF16), dzg))
            dvn_chunks.append(jnp.concatenate(cols, axis=1))
        dsb_ref[...] += jnp.concatenate(dsb_cols, axis=1)
        dvn = jnp.concatenate(dvn_chunks, axis=0)
        dsg_ref[...] += jnp.sum(dvn * vhat, axis=0, keepdims=True)
        dvh = dvn * gain
        dp_ref[:, 2 * LRU_W:3 * LRU_W] = rs * (dvh - jnp.mean(dvh, axis=-1, keepdims=True)
                                               - vhat * jnp.mean(dvh * vhat, axis=-1, keepdims=True))

    row = pl.BlockSpec((TM, D), lambda i: (i, 0))
    half = pl.BlockSpec((TM, LRU_W), lambda i: (i, 0))
    const2 = lambda i: (0, 0)
    const3 = lambda i: (0, 0, 0)
    return _grid_call(
        body, name=name, nsteps=t // TM,
        in_specs=[row, _pcols(1), _pcols(2), _pcols(3), half, half, row, pl.BlockSpec((2, 3, D), const3),
                  pl.BlockSpec((1, MLP_W), const2), pl.BlockSpec((GROUPS, CHUNK, CHUNK), const3),
                  pl.BlockSpec((CHUNK, GROUPS), const2), ANY],
        out_specs=[half, pl.BlockSpec((TM, 3 * LRU_W), lambda i: (i, 0)), pl.BlockSpec((D, D), const2),
                   pl.BlockSpec((2, 1, D), const3), pl.BlockSpec((1, MLP_W), const2),
                   pl.BlockSpec((GROUPS, CHUNK, CHUNK), const3), pl.BlockSpec((CHUNK, GROUPS), const2)],
        out_shape=[jax.ShapeDtypeStruct((t, LRU_W), F32), jax.ShapeDtypeStruct((t, 3 * LRU_W), F32),
                   jax.ShapeDtypeStruct((D, D), F32), jax.ShapeDtypeStruct((2, 1, D), F32),
                   jax.ShapeDtypeStruct((1, MLP_W), F32), jax.ShapeDtypeStruct((GROUPS, CHUNK, CHUNK), F32),
                   jax.ShapeDtypeStruct((CHUNK, GROUPS), F32)],
        scratch_shapes=[pltpu.VMEM((D, D), BF16)],
        args=(dy, p, p, p, hf, hb, o, mod, sgu_g, sgu_w, sgu_bt, womix), rider=rider)


def mix_in_bwd(dy, h, p, dxf, dxb, dprest, mod, g, conv_w, wmix, name, rider=None):
    t = dy.shape[0]
    nt = t // TM

    def body(dy_ref, h_ref, x_ref, xp_ref, xn_ref, f_ref, fp_ref, fn_ref, b_ref, bp_ref, bn_ref, dpr_ref, mod_ref,
             g_ref, cw_ref, w_hbm, dh_ref, dw_hbm, dmod_ref, dg_ref, dcw_ref, dcb_ref, w_v, dw_v):
        i = pl.program_id(0)

        @pl.when(i == 0)
        def _():
            pltpu.sync_copy(w_hbm, w_v)
            dw_v[...] = jnp.zeros_like(dw_v)
            for ref in (dmod_ref, dg_ref, dcw_ref, dcb_ref):
                ref[...] = jnp.zeros_like(ref)

        dmain = f_ref[...] + b_ref[...]
        dext = _ext(i, nt, dmain, fp_ref[...] + bp_ref[...], fn_ref[...] + bn_ref[...])
        xext = _ext(i, nt, x_ref[...], xp_ref[...], xn_ref[...])
        dxl = cw_ref[0:1, :] * _shifted(dext, 2)
        for k in range(1, CONV_W):
            dxl = dxl + cw_ref[k:k + 1, :] * _shifted(dext, 2 - k)
        dcw_ref[...] += jnp.concatenate(
            [jnp.sum(dmain * _shifted(xext, k - 2), axis=0, keepdims=True) for k in range(CONV_W)], axis=0)
        dcb_ref[...] += jnp.sum(dmain, axis=0, keepdims=True)

        ic = _is_ctx(i, TM)
        gain = g_ref[...]
        scale = _sel(ic, mod_ref, 1)
        z, n, r = _norm_mod(h_ref[...], gain, _sel(ic, mod_ref, 0), scale)
        zb = z.astype(BF16)
        dpb = jnp.concatenate([dxl, dpr_ref[...]], axis=1).astype(BF16)
        dz = jnp.zeros((TM, D), F32)
        for dd in range(N_DEV):
            dpd = dpb[:, dd * N_MIX_SHARD:(dd + 1) * N_MIX_SHARD]
            dz = dz + _dot_nt(dpd, w_v[dd])
            dw_v[dd] += _dot_tn(zb, dpd)
        _acc2(dmod_ref, 0, dz, ic)
        _acc2(dmod_ref, 1, dz * (n * gain), ic)
        dg_ref[...] += jnp.sum(dz * (1.0 + scale) * n, axis=0, keepdims=True)
        dh_ref[...] = dy_ref[...] + _norm_mod_bwd(dz, n, r, gain, scale)

        @pl.when(i == nt - 1)
        def _():
            pltpu.sync_copy(dw_v, dw_hbm)

    main, prev, nxt = _halo_specs(nt, lambda s: s)
    row = pl.BlockSpec((TM, D), lambda i: (i, 0))
    const2 = lambda i: (0, 0)
    return _grid_call(
        body, name=name, nsteps=nt,
        in_specs=[row, row, main, prev, nxt, main, prev, nxt, main, prev, nxt,
                  pl.BlockSpec((TM, 3 * LRU_W), lambda i: (i, 0)), pl.BlockSpec((2, 3, D), lambda i: (0, 0, 0)),
                  pl.BlockSpec((1, D), const2), pl.BlockSpec((CONV_W, LRU_W), const2), ANY],
        out_specs=[row, ANY, pl.BlockSpec((2, 2, D), lambda i: (0, 0, 0)), pl.BlockSpec((1, D), const2),
                   pl.BlockSpec((CONV_W, LRU_W), const2), pl.BlockSpec((1, LRU_W), const2)],
        out_shape=[jax.ShapeDtypeStruct((t, D), F32), jax.ShapeDtypeStruct((N_DEV, D, N_MIX_SHARD), F32),
                   jax.ShapeDtypeStruct((2, 2, D), F32), jax.ShapeDtypeStruct((1, D), F32),
                   jax.ShapeDtypeStruct((CONV_W, LRU_W), F32), jax.ShapeDtypeStruct((1, LRU_W), F32)],
        scratch_shapes=[pltpu.VMEM((N_DEV, D, N_MIX_SHARD), BF16), pltpu.VMEM((N_DEV, D, N_MIX_SHARD), F32)],
        args=(dy, h, p, p, p, dxf, dxf, dxf, dxb, dxb, dxb, dprest, mod, g, conv_w, wmix), rider=rider)


def loss_head(h, g, target, name):
    t = h.shape[0]
    nc = CTX // TM

    def body(h_ref, g_ref, t_ref, dh_ref, loss_ref, dg_ref):
        i = pl.program_id(0)

        @pl.when(i == 0)
        def _():
            loss_ref[...] = jnp.zeros_like(loss_ref)
            dg_ref[...] = jnp.zeros_like(dg_ref)

        @pl.when(i < nc)
        def _():
            dh_ref[...] = jnp.zeros_like(dh_ref)

        @pl.when(i >= nc)
        def _():
            hh = h_ref[...]
            gain = g_ref[...]
            r = lax.rsqrt(jnp.mean(hh * hh, axis=-1, keepdims=True) + EPS)
            n = hh * r
            err = n * gain - t_ref[...]
            loss_ref[...] += 0.5 * jnp.sum(jnp.mean(err * err, axis=-1, keepdims=True))
            dy = err * (1.0 / D)
            dg_ref[...] += jnp.sum(dy * n, axis=0, keepdims=True)
            dn = dy * gain
            dh_ref[...] = r * (dn - n * jnp.mean(dn * n, axis=-1, keepdims=True))

    row = pl.BlockSpec((TM, D), lambda i: (i, 0))
    return pl.pallas_call(
        body, name=name, grid=(t // TM,),
        in_specs=[row, pl.BlockSpec((1, D), lambda i: (0, 0)),
                  pl.BlockSpec((TM, D), lambda i: (jnp.maximum(i - nc, 0), 0))],
        out_specs=[row, pl.BlockSpec((8, 128), lambda i: (0, 0)), pl.BlockSpec((1, D), lambda i: (0, 0))],
        out_shape=[jax.ShapeDtypeStruct((t, D), F32), jax.ShapeDtypeStruct((8, 128), F32),
                   jax.ShapeDtypeStruct((1, D), F32)],
        compiler_params=_cp(),
    )(h, g, target)


def pad_in_shard(w):
    z = jnp.zeros(w.shape[:-1] + (HALF_PAD - HALF,), w.dtype)
    return jnp.concatenate([w[..., :HALF], z, w[..., HALF:], z], axis=-1)


def unpad_in_shard(g):
    return jnp.concatenate([g[..., :HALF], g[..., HALF_PAD:HALF_PAD + HALF]], axis=-1)


def pad_out_shard(w):
    return jnp.concatenate([w, jnp.zeros(w.shape[:-2] + (HALF_PAD - HALF, w.shape[-1]), w.dtype)], axis=-2)


def unpad_out_shard(g):
    return g[..., :HALF, :]


def _block_diag(w):
    eye = jnp.eye(HEADS, dtype=w.dtype)
    return jnp.einsum("dhij,hk->dhikj", w, eye).reshape(2, LRU_W, LRU_W)


def _block_diag_inv(full):
    f = full.reshape(2, HEADS, HEAD_DIM, HEADS, HEAD_DIM)
    return jnp.stack([f[:, hd, :, hd, :] for hd in range(HEADS)], axis=1)


def small_layer(g1, gm, g2, conv_w, conv_b, w_r, b_r, w_i, b_i, lam, sgu_g, sgu_w, sgu_b):
    return dict(g1=g1[None, :], gm=gm[None, :], g2=g2[None, :], conv_w=conv_w, conv_b=conv_b[None, :],
                wr=_block_diag(w_r).astype(BF16), br=b_r[:, None, :], wi=_block_diag(w_i).astype(BF16),
                bi=b_i[:, None, :], lam=lam[:, None, :], sgu_g=sgu_g[None, :], sgu_w=sgu_w, sgu_bt=sgu_b.T)


def small_grads(g):
    return dict(ffn1_norm_g=g["g1"][0], mix_norm_g=g["gm"][0], ffn2_norm_g=g["g2"][0], lru_conv_w=g["conv_w"],
                lru_conv_b=g["conv_b"][0], lru_w_r=_block_diag_inv(g["wr"]), lru_b_r=g["br"][:, 0, :],
                lru_w_i=_block_diag_inv(g["wi"]), lru_b_i=g["bi"][:, 0, :], lru_lambda=g["lam"][:, 0, :],
                sgu_norm_g=g["sgu_g"][0], sgu_w=g["sgu_w"], sgu_b=g["sgu_bt"].T)


BIG_KEYS = ("win1", "wout1", "wmix", "womix", "win2", "wout2")


def _as_blocks(key, g):
    if key in ("wout1", "wout2"):
        return g.reshape(N_DEV, HALF_PAD, D)
    return g.reshape(N_DEV, OMIX_SHARD, D) if key == "womix" else g


def _gathered(key, a):
    if key in ("wout1", "wout2"):
        return a.reshape(HID_PAD, D)
    return a.reshape(D, D) if key == "womix" else a


class _ReduceScatter:
    def __init__(self, c_idx, where):
        self.c_idx, self.where = c_idx, where
        self.out = {}

    def pair(self, group):
        return pair_rider([g for _, g in group])

    def after_pair(self, group, recv1, tag):
        parts = [pair_sum(g, r, self.c_idx, f"pair_sum_{tag}_{i}") for i, ((_, g), r) in enumerate(zip(group, recv1))]
        return chips_rider(parts)

    def after_chips(self, group, recv1, recv2, tag):
        for i, ((key, g), r1, r2) in enumerate(zip(group, recv1, recv2)):
            self.out[key] = final_sum(g, r1, r2, self.where, f"final_sum_{tag}_{i}")


def fwd_bwd(ctx_rows, x_rows, target, mods, shards, smalls, final_g, c_idx, where):
    assert CTX == TM and len(shards) == 2

    def gather(keys_by_layer):
        return GatherRider([shards[l][k] for l, k in keys_by_layer])

    def put(full, keys_by_layer, got):
        for (l, k), a in zip(keys_by_layer, got):
            full[l][k] = _gathered(k, a)

    full = [dict(s) for s in smalls]
    first = [(0, "win1"), (0, "wout1")]
    put(full, first, run_alone(gather(first), pl.ANY, "gather_w0"))
    riders = {
        "ffn1_fwd_0": [(0, "wmix"), (0, "womix"), (0, "win2"), (0, "wout2")],
        "ffn2_fwd_0": [(1, "win1"), (1, "wout1"), (1, "wmix"), (1, "womix")],
        "ffn1_fwd_1": [(1, "win2"), (1, "wout2")],
    }

    def ffn(which, l, h):
        name = f"ffn{which}_fwd_{l}"
        w = full[l]
        keys = riders.get(name)
        m = mods[l][:, 0:3] if which == 1 else mods[l][:, 6:9]
        outs, got = ffn_fwd(h, m, w[f"g{which}"], w[f"win{which}"], w[f"wout{which}"], name,
                            rider=gather(keys) if keys else None)
        if keys:
            put(full, keys, got)
        return outs

    saved = []
    h = (ctx_rows, x_rows)
    for l in range(2):
        mm = mods[l][:, 3:6]
        outs = ffn(1, l, h)
        h1, gu1, acc1 = outs[:3]
        hin = outs[3] if l == 0 else h
        w = full[l]
        p = mix_in_fwd(h1, mm, w["gm"], w["wmix"], f"mix_in_fwd_{l}")
        hs = [lru_fwd(p, w["conv_w"], w["conv_b"], w["wr"][d], w["br"][d], w["wi"][d], w["bi"][d], w["lam"][d],
                      bool(d), f"lru_fwd_{l}_{d}") for d in range(2)]
        h2, o = mix_out_fwd(h1, p, hs[0], hs[1], mm, w["sgu_g"], w["sgu_w"], w["sgu_bt"], w["womix"], f"mix_out_fwd_{l}")
        h3, gu2, acc2 = ffn(2, l, h2)
        saved.append((hin, h1, h2, gu1, acc1, p, hs, o, gu2, acc2))
        h = h3
    dh, loss, dgf = loss_head(h, final_g, target, "loss_head")

    rs = _ReduceScatter(c_idx, where)
    grads, dmods, sums = [None, None], [None, None], [None, None]
    pending = None
    for l in (1, 0):
        w = full[l]
        m1, mm, m2 = mods[l][:, 0:3], mods[l][:, 3:6], mods[l][:, 6:9]
        hin, h1, h2, gu1, acc1, p, hs, o, gu2, acc2 = saved[l]
        g = {}
        rs.out = {}
        (dp2, g["wout2"], dgate2), r1 = ffn_bwd_a(dh, acc2, gu2, m2, w["wout2"], f"ffn2_bwd_a_{l}",
                                                  rider=rs.pair(pending[0]) if pending else None)
        chips = rs.after_pair(pending[0], r1, pending[1]) if pending else None
        (dh, g["win2"], dmod2, g["g2"]), r2 = ffn_bwd_b(dh, h2, dp2, m2, w["g2"], w["win2"], f"ffn2_bwd_b_{l}", rider=chips)
        if pending:
            rs.after_chips(pending[0], r1, r2, pending[1])
            sums[l + 1].update(rs.out)
            rs.out = {}

        grp = [(k, _as_blocks(k, g[k])) for k in ("win2", "wout2")]
        (dhs, dprest, g["womix"], dgatem, g["sgu_g"], g["sgu_w"], g["sgu_bt"]), r1 = mix_out_bwd(
            dh, p, hs[0], hs[1], o, mm, w["sgu_g"], w["sgu_w"], w["sgu_bt"], w["womix"], f"mix_out_bwd_{l}",
            rider=rs.pair(grp))
        chips = rs.after_pair(grp, r1, f"a{l}")
        dx, per_dir = [], []
        for d in range(2):
            out = lru_bwd(p, hs[d], dhs, w["conv_w"], w["conv_b"], w["wr"][d], w["br"][d], w["wi"][d], w["bi"][d],
                          w["lam"][d], bool(d), f"lru_bwd_{l}_{d}")
            dx.append(out[0])
            per_dir.append(out[1:])
        for k, nm in enumerate(("wr", "wi", "br", "bi", "lam")):
            g[nm] = jnp.stack([per_dir[0][k], per_dir[1][k]])
        (dh, g["wmix"], dmodm, g["gm"], g["conv_w"], g["conv_b"]), r2 = mix_in_bwd(
            dh, h1, p, dx[0], dx[1], dprest, mm, w["gm"], w["conv_w"], w["wmix"], f"mix_in_bwd_{l}", rider=chips)
        rs.after_chips(grp, r1, r2, f"a{l}")
        sums[l] = dict(rs.out)
        rs.out = {}

        if l == 1:
            (dp1, g["wout1"], dgate1), _ = ffn_bwd_a(dh, acc1, gu1, m1, w["wout1"], f"ffn1_bwd_a_{l}")
            (dh, g["win1"], dmod1, g["g1"]), _ = ffn_bwd_b(dh, hin, dp1, m1, w["g1"], w["win1"], f"ffn1_bwd_b_{l}")
            pending = ([(k, _as_blocks(k, g[k])) for k in ("womix", "wmix", "wout1", "win1")], f"b{l}")
        else:
            grp = [(k, _as_blocks(k, g[k])) for k in ("womix", "wmix")]
            (dp1, g["wout1"], dgate1), r1 = ffn_bwd_a(dh, acc1, gu1, m1, w["wout1"], f"ffn1_bwd_a_{l}", rider=rs.pair(grp))
            chips = rs.after_pair(grp, r1, f"b{l}")
            (dh, g["win1"], dmod1, g["g1"]), r2 = ffn_bwd_b(dh, hin, dp1, m1, w["g1"], w["win1"], f"ffn1_bwd_b_{l}",
                                                            rider=chips, latent_only=True)
            rs.after_chips(grp, r1, r2, f"b{l}")
            grp = [(k, _as_blocks(k, g[k])) for k in ("wout1", "win1")]
            r1 = run_alone(rs.pair(grp), pl.ANY, "rs_pair_last")
            r2 = run_alone(rs.after_pair(grp, r1, f"c{l}"), pl.ANY, "rs_chips_last")
            rs.after_chips(grp, r1, r2, f"c{l}")
            sums[l].update(rs.out)
        dmods[l] = jnp.concatenate([dmod1, dgate1, dmodm, dgatem, dmod2, dgate2], axis=1)
        grads[l] = g
    return loss, dh, jnp.stack(dmods), grads, sums, dgf


def _row_block(r, c, limit=262144):
    best = 8
    for rb in range(8, r + 1, 8):
        if r % rb == 0 and rb * c <= limit:
            best = rb
    return best


def pair_sum(grads, recv, c_idx, name):
    _, r, c = grads.shape
    rb = _row_block(r, c)

    def body(c_ref, g_ref, r_ref, o_ref):
        o_ref[...] = (g_ref[...] + r_ref[...]).astype(BF16)

    return pl.pallas_call(
        body, name=name,
        grid_spec=pltpu.PrefetchScalarGridSpec(
            num_scalar_prefetch=1, grid=(4, r // rb),
            in_specs=[pl.BlockSpec((1, rb, c), lambda j, i, c_ref: (2 * j + c_ref[0], i, 0)),
                      pl.BlockSpec((1, rb, c), lambda j, i, c_ref: (j, i, 0))],
            out_specs=pl.BlockSpec((1, rb, c), lambda j, i, c_ref: (j, i, 0))),
        out_shape=jax.ShapeDtypeStruct((4, r, c), BF16),
        compiler_params=_cp(2),
    )(c_idx, grads, recv)


def final_sum(grads, recv1, recv2, where, name):
    _, r, c = grads.shape
    rb = _row_block(r, c)

    def body(w_ref, g_ref, r1_ref, r2_ref, o_ref):
        far = (r2_ref[0].astype(F32) + r2_ref[1].astype(F32)) + r2_ref[2].astype(F32)
        o_ref[...] = (g_ref[0] + r1_ref[0]) + far

    return pl.pallas_call(
        body, name=name,
        grid_spec=pltpu.PrefetchScalarGridSpec(
            num_scalar_prefetch=1, grid=(r // rb,),
            in_specs=[pl.BlockSpec((1, rb, c), lambda i, w_ref: (w_ref[0], i, 0)),
                      pl.BlockSpec((1, rb, c), lambda i, w_ref: (w_ref[1], i, 0)),
                      pl.BlockSpec((3, rb, c), lambda i, w_ref: (0, i, 0))],
            out_specs=pl.BlockSpec((rb, c), lambda i, w_ref: (i, 0))),
        out_shape=jax.ShapeDtypeStruct((r, c), F32),
        compiler_params=_cp(1),
    )(where, grads, recv1, recv2)


ADA_ROWS = 16


def _silu(v):
    return v * _sigmoid(v)


def ada_fwd(cond, w_ada, b_slab, name):
    def body(c_ref, w_ref, b_ref, o_ref):
        s = _silu(c_ref[...]).astype(BF16)
        o_ref[0] = _dot(s, w_ref[0].astype(BF16)) + b_ref[0]

    return pl.pallas_call(
        body, name=name, grid=(DEPTH,),
        in_specs=[pl.BlockSpec((ADA_ROWS, D), lambda l: (0, 0)), pl.BlockSpec((1, D, ADA_SHARD), lambda l: (l, 0, 0)),
                  pl.BlockSpec((1, 1, ADA_SHARD), lambda l: (l, 0, 0))],
        out_specs=pl.BlockSpec((1, ADA_ROWS, ADA_SHARD), lambda l: (l, 0, 0)),
        out_shape=jax.ShapeDtypeStruct((DEPTH, ADA_ROWS, ADA_SHARD), F32),
        compiler_params=_cp(),
    )(cond, w_ada, b_slab)


def ada_bwd(cond, dm_sample, dm_ctx, w_ada, name):
    def body(c_ref, ds_ref, dc_ref, w_ref, gw_ref, dsc_ref):
        @pl.when(pl.program_id(0) == 0)
        def _():
            dsc_ref[...] = jnp.zeros_like(dsc_ref)

        s = _silu(c_ref[...]).astype(BF16)
        dcs = dc_ref[0]
        tot = dcs[0:1]
        for j in range(1, N_DEV):
            tot = tot + dcs[j:j + 1]
        tot8 = jnp.where(lax.broadcasted_iota(jnp.int32, (N_DEV, ADA_SHARD), 0) == 0, tot, 0.0)
        dm = jnp.concatenate([ds_ref[0], tot8], axis=0).astype(BF16)
        gw_ref[0] = _dot_tn(s, dm)
        dsc_ref[...] += _dot_nt(dm, w_ref[0].astype(BF16))[N_DEV:N_DEV + 1]

    slab = pl.BlockSpec((1, N_DEV, ADA_SHARD), lambda l: (l, 0, 0))
    wspec = pl.BlockSpec((1, D, ADA_SHARD), lambda l: (l, 0, 0))
    return pl.pallas_call(
        body, name=name, grid=(DEPTH,),
        in_specs=[pl.BlockSpec((ADA_ROWS, D), lambda l: (0, 0)), slab, slab, wspec],
        out_specs=[wspec, pl.BlockSpec((1, D), lambda l: (0, 0))],
        out_shape=[jax.ShapeDtypeStruct((DEPTH, D, ADA_SHARD), F32), jax.ShapeDtypeStruct((1, D), F32)],
        compiler_params=_cp(),
    )(cond, dm_sample, dm_ctx, w_ada)


def sum_over_devices(parts, name, silu_rows=0, w=None):
    _, r, c = parts.shape

    def body(*refs):
        p_ref, o_ref = refs[0], refs[-1]
        tot = p_ref[0]
        for j in range(1, N_DEV):
            tot = tot + p_ref[j]
        o_ref[...] = tot
        if silu_rows:
            wv = refs[1][...]
            s = _sigmoid(wv)
            o_ref[0:silu_rows, :] = tot[0:silu_rows, :] * (s * (1.0 + wv * (1.0 - s)))

    vm = pl.BlockSpec(memory_space=pltpu.VMEM)
    args = (parts,) if w is None else (parts, w)
    return pl.pallas_call(
        body, name=name, in_specs=[vm] * len(args), out_specs=vm,
        out_shape=jax.ShapeDtypeStruct((r, c), F32),
        compiler_params=pltpu.CompilerParams(vmem_limit_bytes=VMEM_LIMIT),
    )(*args)


def sum_dmods(dm_all, name):
    def body(d_ref, o_ref):
        for l in range(DEPTH):
            tot = d_ref[0, l]
            for j in range(1, N_DEV):
                tot = tot + d_ref[j, l]
            o_ref[l:l + 1, :] = tot[0:1] + tot[1:2]

    vm = pl.BlockSpec(memory_space=pltpu.VMEM)
    return pl.pallas_call(
        body, name=name, in_specs=[vm], out_specs=vm,
        out_shape=jax.ShapeDtypeStruct((DEPTH, N_MOD * D), F32),
    )(dm_all)


def adamw(w, g, m, v, name):
    r, c = w.shape
    rb = _row_block(r, c, limit=131072)

    def body(w_ref, g_ref, m_ref, v_ref, d_ref, nm_ref, nv_ref):
        g_ = g_ref[...]
        nm = B1 * m_ref[...] + (1.0 - B1) * g_
        nv = B2 * v_ref[...] + (1.0 - B2) * (g_ * g_)
        nm_ref[...] = nm
        nv_ref[...] = nv
        m_hat = nm / (1.0 - B1 ** STEP)
        v_hat = nv / (1.0 - B2 ** STEP)
        d_ref[...] = -LR * (m_hat / (jnp.sqrt(v_hat) + ADAM_EPS) + WD * w_ref[...])

    blk = pl.BlockSpec((rb, c), lambda i: (i, 0))
    shp = jax.ShapeDtypeStruct((r, c), F32)
    return pl.pallas_call(
        body, name=name, grid=(r // rb,), in_specs=[blk] * 4, out_specs=[blk] * 3, out_shape=[shp] * 3,
        compiler_params=_cp(),
    )(w, g, m, v)


def _adamw_nd(w, g, m, v, name):
    shape = w.shape
    flat = lambda a: a.reshape(-1, shape[-1])
    return tuple(o.reshape(shape) for o in adamw(flat(w), flat(g), flat(m), flat(v), name))


LANES = 128


PACK_UNIT = 8 * LANES


ADAMW_SMALL_ROWS = 512


def _pack(arrays, row_multiple=8):
    rows = []
    for a in arrays:
        f = a.reshape(-1).astype(F32)
        pad = (-f.shape[0]) % PACK_UNIT
        if pad:
            f = jnp.concatenate([f, jnp.zeros((pad,), F32)])
        rows.append(f.reshape(-1, LANES))
    n = sum(r.shape[0] for r in rows)
    if n % row_multiple:
        rows.append(jnp.zeros((-n % row_multiple, LANES), F32))
    return jnp.concatenate(rows, axis=0)


def _unpack(packed, shapes):
    out, r0 = [], 0
    lead = packed.shape[:-2]
    for shp in shapes:
        size = 1
        for s in shp:
            size *= s
        nr = 8 * -(-size // PACK_UNIT)
        blk = packed[..., r0:r0 + nr, :].reshape(lead + (nr * LANES,))[..., :size]
        out.append(blk.reshape(lead + tuple(shp)))
        r0 += nr
    return out


WEIGHTS = ["c_ctx", "w_ada", "b_ada", "ffn1_norm_g", "ffn1_w_in", "ffn1_w_out", "mix_norm_g", "w_in_mix", "lru_conv_w",
           "lru_conv_b", "lru_w_r", "lru_b_r", "lru_w_i", "lru_b_i", "lru_lambda", "sgu_norm_g", "sgu_w", "sgu_b",
           "w_out_mix", "ffn2_norm_g", "ffn2_w_in", "ffn2_w_out", "final_norm_g"]
BIG = ["w_ada", "ffn1_w_in", "ffn1_w_out", "w_in_mix", "w_out_mix", "ffn2_w_in", "ffn2_w_out"]
SHARDED_SMALL = ["lru_conv_w", "lru_b_r", "lru_b_i", "lru_lambda"]
LAYER_SMALL = ["ffn1_norm_g", "mix_norm_g", "ffn2_norm_g", "lru_conv_w", "lru_conv_b", "lru_w_r", "lru_b_r", "lru_w_i",
               "lru_b_i", "lru_lambda", "sgu_norm_g", "sgu_w", "sgu_b"]
LRU_SHARD = LRU_W // N_DEV


def _widen(a):
    return jnp.moveaxis(a, 0, -2).reshape(a.shape[1:-1] + (LRU_W,))


def kernel(x, c, ctx, c_ctx, w_ada, b_ada, ffn1_norm_g, ffn1_w_in, ffn1_w_out, mix_norm_g, w_in_mix, lru_conv_w, lru_conv_b, lru_w_r, lru_b_r, lru_w_i, lru_b_i, lru_lambda, sgu_norm_g, sgu_w, sgu_b, w_out_mix, ffn2_norm_g, ffn2_w_in, ffn2_w_out, final_norm_g, loss_target, m_c_ctx, m_w_ada, m_b_ada, m_ffn1_norm_g, m_ffn1_w_in, m_ffn1_w_out, m_mix_norm_g, m_w_in_mix, m_lru_conv_w, m_lru_conv_b, m_lru_w_r, m_lru_b_r, m_lru_w_i, m_lru_b_i, m_lru_lambda, m_sgu_norm_g, m_sgu_w, m_sgu_b, m_w_out_mix, m_ffn2_norm_g, m_ffn2_w_in, m_ffn2_w_out, m_final_norm_g, v_c_ctx, v_w_ada, v_b_ada, v_ffn1_norm_g, v_ffn1_w_in, v_ffn1_w_out, v_mix_norm_g, v_w_in_mix, v_lru_conv_w, v_lru_conv_b, v_lru_w_r, v_lru_b_r, v_lru_w_i, v_lru_b_i, v_lru_lambda, v_sgu_norm_g, v_sgu_w, v_sgu_b, v_w_out_mix, v_ffn2_norm_g, v_ffn2_w_in, v_ffn2_w_out, v_final_norm_g):
    given = dict(locals())
    W = {n: given[n] for n in WEIGHTS}
    M = {n: given["m_" + n] for n in WEIGHTS}
    V = {n: given["v_" + n] for n in WEIGHTS}
    xi, yi, ci = _position()
    me = 4 * xi + 2 * yi + ci
    chip = 2 * xi + yi

    sharded_shapes = [W[n].shape for n in SHARDED_SMALL]
    got = run_alone(GatherRider([_pack([c[0]] + [W[n] for n in SHARDED_SMALL])]), pltpu.VMEM, "gather_small")[0]
    parts = _unpack(got, [(D,)] + sharded_shapes)
    c_all = parts[0]
    wide = {n: _widen(a) for n, a in zip(SHARDED_SMALL, parts[1:])}
    cond = jnp.concatenate([c_all, c_ctx[None, :], jnp.zeros((ADA_ROWS - N_DEV - 1, D), F32)], axis=0)
    b_slab = lax.dynamic_slice_in_dim(b_ada, me * ADA_SHARD, ADA_SHARD, axis=1)[:, None, :]
    slabs = ada_fwd(cond, w_ada, b_slab, "ada_fwd")
    mall = run_alone(GatherRider([slabs.reshape(DEPTH * ADA_ROWS, ADA_SHARD)]), pltpu.VMEM, "gather_mod")[0]
    mall = mall.reshape(N_DEV, DEPTH, ADA_ROWS, ADA_SHARD)
    m_sample = lax.dynamic_index_in_dim(mall, me, axis=2, keepdims=False)
    m_ctx = mall[:, :, N_DEV, :]
    mods = jnp.stack([jnp.transpose(m, (1, 0, 2)).reshape(DEPTH, N_MOD, D) for m in (m_ctx, m_sample)], axis=1)

    shards, smalls = [], []
    for l in range(DEPTH):
        sh = dict(win1=pad_in_shard(ffn1_w_in[l]), wout1=pad_out_shard(ffn1_w_out[l]), wmix=w_in_mix[l],
                  womix=w_out_mix[l], win2=pad_in_shard(ffn2_w_in[l]), wout2=pad_out_shard(ffn2_w_out[l]))
        shards.append({k: a.astype(BF16) for k, a in sh.items()})
        smalls.append(small_layer(ffn1_norm_g[l], mix_norm_g[l], ffn2_norm_g[l], wide["lru_conv_w"][l], lru_conv_b[l],
                                  lru_w_r[l], wide["lru_b_r"][l], lru_w_i[l], wide["lru_b_i"][l], wide["lru_lambda"][l],
                                  sgu_norm_g[l], sgu_w[l], sgu_b[l]))

    c_idx = ci.reshape(1).astype(jnp.int32)
    where = jnp.stack([me, chip]).astype(jnp.int32)
    loss_blk, dx, dmods, grads, gsum, dgf = fwd_bwd(ctx[0], x[0], loss_target[0], mods, shards, smalls,
                                                    final_norm_g[None, :], c_idx, where)
    G = {
        "ffn1_w_in": jnp.stack([unpad_in_shard(gsum[l]["win1"]) for l in range(DEPTH)]),
        "ffn1_w_out": jnp.stack([unpad_out_shard(gsum[l]["wout1"]) for l in range(DEPTH)]),
        "w_in_mix": jnp.stack([gsum[l]["wmix"] for l in range(DEPTH)]),
        "w_out_mix": jnp.stack([gsum[l]["womix"] for l in range(DEPTH)]),
        "ffn2_w_in": jnp.stack([unpad_in_shard(gsum[l]["win2"]) for l in range(DEPTH)]),
        "ffn2_w_out": jnp.stack([unpad_out_shard(gsum[l]["wout2"]) for l in range(DEPTH)]),
    }

    n_rows = DEPTH * 2 * N_MOD
    dm_rows = jnp.concatenate([dmods.reshape(n_rows, D), jnp.zeros((-n_rows % 8, D), F32)], axis=0)
    dm_all = run_alone(GatherRider([dm_rows]), pltpu.VMEM, "gather_dmod")[0][:, :n_rows]
    dm_all = dm_all.reshape(N_DEV, DEPTH, 2, N_MOD * D)
    mine = lax.dynamic_slice_in_dim(dm_all, me * ADA_SHARD, ADA_SHARD, axis=3)
    G["w_ada"], dsc = ada_bwd(cond, jnp.transpose(mine[:, :, 1, :], (1, 0, 2)), jnp.transpose(mine[:, :, 0, :], (1, 0, 2)),
                              w_ada, "ada_bwd")
    G["b_ada"] = sum_dmods(dm_all, "sum_dmods")

    per_layer = [small_grads(grads[l]) for l in range(DEPTH)]
    small = [jnp.stack([per_layer[l][n] for l in range(DEPTH)]) for n in LAYER_SMALL]
    small_shapes = [(D,), (D,)] + [s.shape for s in small]
    got = run_alone(GatherRider([_pack([dsc[0], dgf[0]] + small)]), pltpu.VMEM, "gather_small_grads")[0]
    summed = _unpack(sum_over_devices(got, "sum_small_grads", silu_rows=D // LANES, w=c_ctx.reshape(D // LANES, LANES)),
                     small_shapes)
    G["c_ctx"], G["final_norm_g"] = summed[0], summed[1]
    for n, a in zip(LAYER_SMALL, summed[2:]):
        G[n] = lax.dynamic_slice_in_dim(a, me * LRU_SHARD, LRU_SHARD, axis=a.ndim - 1) if n in SHARDED_SMALL else a

    delta, new_m, new_v = {}, {}, {}
    for n in BIG:
        delta[n], new_m[n], new_v[n] = _adamw_nd(W[n], G[n], M[n], V[n], f"adamw_{n}")
    rest = [n for n in WEIGHTS if n not in BIG]
    shapes = [W[n].shape for n in rest]
    outs = adamw(*[_pack([src[n] for n in rest], row_multiple=ADAMW_SMALL_ROWS) for src in (W, G, M, V)], "adamw_small")
    for dst, packed in zip((delta, new_m, new_v), outs):
        for n, a in zip(rest, _unpack(packed, shapes)):
            dst[n] = a

    loss = lax.psum(loss_blk[0, 0], ("x", "y", "c"))
    grad_x = dx[None]
    return (loss, grad_x, *[G[n] for n in WEIGHTS], *[delta[n] for n in WEIGHTS], *[new_m[n] for n in WEIGHTS],
            *[new_v[n] for n in WEIGHTS])
```

```python
import functools

import jax
import jax.numpy as jnp
from jax import lax
from jax.experimental import pallas as pl
from jax.experimental.pallas import tpu as pltpu

F32 = jnp.float32
BF16 = jnp.bfloat16

D = 1024
CTX = 256
DEPTH = 2
EPS = 1e-6
D_FF = 2816
LRU_W = 512
HEADS = 8
HEAD_DIM = 64
CONV_W = 4
RG_C = 8.0
GROUPS = 4
GROUP_DIM = 128
CHUNK = 128
MLP_W = 512
IN_PROJ = 2048
N_MOD = 9
N_DEV = 8

LR = 0.001
B1 = 0.9
B2 = 0.999
ADAM_EPS = 1e-08
WD = 0.01
STEP = 10

SHARD_FF_IN = 704
HALF = 352
HALF_PAD = 384
SHARD_PAD = 2 * HALF_PAD
HID_PAD = 4 * SHARD_PAD
N_MIX_SHARD = IN_PROJ // N_DEV
OMIX_SHARD = D // N_DEV
ADA_SHARD = N_MOD * D // N_DEV

TM = 256
HALO = 8
VMEM_LIMIT = 60 * 1024 * 1024

MESH = pl.DeviceIdType.MESH
ANY = pl.BlockSpec(memory_space=pl.ANY)


def _cp(n_axes=1):
    return pltpu.CompilerParams(dimension_semantics=("arbitrary",) * n_axes, vmem_limit_bytes=VMEM_LIMIT)


def _position():
    return lax.axis_index("x"), lax.axis_index("y"), lax.axis_index("c")


class GatherRider:
    def __init__(self, shards):
        n = len(shards)
        self.n = n
        self.ins = list(shards)
        self.out_shape = [jax.ShapeDtypeStruct((N_DEV,) + s.shape, s.dtype) for s in shards]
        self.sems = [pltpu.SemaphoreType.DMA((n, 7)), pltpu.SemaphoreType.DMA((n, 7)), pltpu.SemaphoreType.DMA((n,))]

    def _ctx(self, outs, sems):
        x, y, c = _position()
        chips = [(1 - x, y), (x, 1 - y), (1 - x, 1 - y)]

        def copy(t, k, block, to, src=None):
            dst = outs[t].at[4 * block[0] + 2 * block[1] + block[2]]
            return pltpu.make_async_remote_copy(
                src_ref=dst if src is None else src, dst_ref=dst, send_sem=sems[0].at[t, k],
                recv_sem=sems[1].at[t, k], device_id=to, device_id_type=MESH)

        return (x, y, c), (x, y, 1 - c), chips, copy

    def _local(self, ins, outs, sems, t):
        x, y, c = _position()
        return pltpu.make_async_copy(ins[t], outs[t].at[4 * x + 2 * y + c], sems[2].at[t])

    def _first(self, ins, outs, sems, t):
        me, sibling, chips, copy = self._ctx(outs, sems)
        return [copy(t, 0, me, sibling, src=ins[t])] + [copy(t, 1 + j, me, (*chip, me[2]), src=ins[t])
                                                         for j, chip in enumerate(chips)]

    def start(self, ins, outs, sems):
        for t in range(self.n):
            self._local(ins, outs, sems, t).start()
            for cp in self._first(ins, outs, sems, t):
                cp.start()

    def mid(self, ins, outs, sems):
        me, sibling, chips, copy = self._ctx(outs, sems)
        for j, chip in enumerate(chips):
            for t in range(self.n):
                copy(t, 1 + j, (*chip, me[2]), me).wait_recv()
                copy(t, 4 + j, (*chip, me[2]), sibling).start()

    def finish(self, ins, outs, sems):
        me, sibling, chips, copy = self._ctx(outs, sems)
        for t in range(self.n):
            copy(t, 0, sibling, me).wait_recv()
            for j, chip in enumerate(chips):
                copy(t, 4 + j, (*chip, 1 - me[2]), me).wait_recv()
        for t in range(self.n):
            for cp in self._first(ins, outs, sems, t):
                cp.wait_send()
            for j, chip in enumerate(chips):
                copy(t, 4 + j, (*chip, me[2]), sibling).wait_send()
            self._local(ins, outs, sems, t).wait()


class ExchangeRider:
    def __init__(self, tensors, plan, n_slots):
        n = len(tensors)
        self.n, self.plan = n, plan
        self.ins = list(tensors)
        self.out_shape = [jax.ShapeDtypeStruct((n_slots,) + s.shape[1:], s.dtype) for s in tensors]
        self.sems = [pltpu.SemaphoreType.DMA((n, n_slots)), pltpu.SemaphoreType.DMA((n, n_slots))]

    def _copies(self, ins, outs, sems):
        return [pltpu.make_async_remote_copy(
            src_ref=ins[t].at[block], dst_ref=outs[t].at[k], send_sem=sems[0].at[t, k], recv_sem=sems[1].at[t, k],
            device_id=to, device_id_type=MESH)
            for t in range(self.n) for k, (block, to) in enumerate(self.plan(*_position()))]

    def start(self, ins, outs, sems):
        for cp in self._copies(ins, outs, sems):
            cp.start()

    def mid(self, ins, outs, sems):
        pass

    def finish(self, ins, outs, sems):
        for cp in self._copies(ins, outs, sems):
            cp.wait()


class Riders:
    def __init__(self, riders):
        self.riders = list(riders)
        self.ins = [a for r in self.riders for a in r.ins]
        self.out_shape = [s for r in self.riders for s in r.out_shape]
        self.sems = [s for r in self.riders for s in r.sems]

    def _each(self, ins, outs, sems):
        i = o = s = 0
        for r in self.riders:
            ni, no, ns = len(r.ins), len(r.out_shape), len(r.sems)
            yield r, ins[i:i + ni], outs[o:o + no], sems[s:s + ns]
            i, o, s = i + ni, o + no, s + ns

    def start(self, ins, outs, sems):
        for r, a, b, c in self._each(ins, outs, sems):
            r.start(a, b, c)

    def mid(self, ins, outs, sems):
        for r, a, b, c in self._each(ins, outs, sems):
            r.mid(a, b, c)

    def finish(self, ins, outs, sems):
        for r, a, b, c in self._each(ins, outs, sems):
            r.finish(a, b, c)

    def split(self, outs):
        res, o = [], 0
        for r in self.riders:
            res.append(list(outs[o:o + len(r.out_shape)]))
            o += len(r.out_shape)
        return res


def pair_rider(grads):
    def plan(x, y, c):
        return [(4 * cx + 2 * cy + (1 - c), (x, y, 1 - c)) for cx in range(2) for cy in range(2)]
    return ExchangeRider(grads, plan, 4)


def chips_rider(parts):
    def plan(x, y, c):
        return [(2 * cx + cy, (cx, cy, c)) for cx, cy in [(1 - x, y), (x, 1 - y), (1 - x, 1 - y)]]
    return ExchangeRider(parts, plan, 3)


def run_alone(rider, space, name):
    ni = len(rider.ins)
    no = len(rider.out_shape)

    def body(*refs):
        ins, outs, sems = refs[:ni], refs[ni:ni + no], refs[ni + no:]
        rider.start(ins, outs, sems)
        rider.mid(ins, outs, sems)
        rider.finish(ins, outs, sems)

    spec = pl.BlockSpec(memory_space=space)
    return pl.pallas_call(
        body, name=name, in_specs=[spec] * ni, out_specs=[spec] * no, out_shape=rider.out_shape,
        scratch_shapes=rider.sems, compiler_params=pltpu.CompilerParams(vmem_limit_bytes=VMEM_LIMIT),
    )(*rider.ins)


def _grid_call(body, *, name, nsteps, in_specs, out_specs, out_shape, scratch_shapes, args, rider=None):
    if rider is None:
        outs = pl.pallas_call(body, name=name, grid=(nsteps,), in_specs=in_specs, out_specs=out_specs,
                              out_shape=out_shape, scratch_shapes=scratch_shapes, compiler_params=_cp())(*args)
        return outs, []
    ni, no, ns = len(in_specs), len(out_specs), len(scratch_shapes)
    ri, ro = len(rider.ins), len(rider.out_shape)

    def wrapped(*refs):
        ins, refs = refs[:ni], refs[ni:]
        r_ins, refs = refs[:ri], refs[ri:]
        outs, refs = refs[:no], refs[no:]
        r_outs, refs = refs[:ro], refs[ro:]
        scratch, r_sems = refs[:ns], refs[ns:]
        s = pl.program_id(0)

        @pl.when(s == 0)
        def _():
            rider.start(r_ins, r_outs, r_sems)

        body(*ins, *outs, *scratch)

        @pl.when(s == (3 * nsteps) // 4)
        def _():
            rider.mid(r_ins, r_outs, r_sems)

        @pl.when(s == nsteps - 1)
        def _():
            rider.finish(r_ins, r_outs, r_sems)

    outs = pl.pallas_call(
        wrapped, name=name, grid=(nsteps,), in_specs=list(in_specs) + [ANY] * ri, out_specs=list(out_specs) + [ANY] * ro,
        out_shape=list(out_shape) + rider.out_shape, scratch_shapes=list(scratch_shapes) + rider.sems,
        compiler_params=_cp())(*args, *rider.ins)
    return outs[:no], outs[no:]


def _dot(a, b):
    return jnp.dot(a, b, preferred_element_type=F32)


def _dot_nt(a, b):
    return lax.dot_general(a, b, (((1,), (1,)), ((), ())), preferred_element_type=F32)


def _dot_tn(a, b):
    return lax.dot_general(a, b, (((0,), (0,)), ((), ())), preferred_element_type=F32)


def _sigmoid(x):
    return 1.0 / (1.0 + jnp.exp(-x))


def _kind(i):
    return jnp.where(i < CTX // TM, 0, 1)


def _sel(kind, mod_ref, k):
    return mod_ref[kind, k:k + 1, :]


def _acc2(ref, k, val, kind):
    ref[kind, k:k + 1, :] += jnp.sum(val, axis=0, keepdims=True)


def _norm_mod(h, g, shift, scale):
    r = lax.rsqrt(jnp.mean(h * h, axis=-1, keepdims=True) + EPS)
    n = h * r
    return (n * g) * (1.0 + scale) + shift, n, r


def _norm_mod_bwd(dz, n, r, g, scale):
    dn = dz * (g * (1.0 + scale))
    return r * (dn - n * jnp.mean(dn * n, axis=-1, keepdims=True))


def ffn_fwd(h, mod, g, win, wout, name, rider=None):
    split = isinstance(h, tuple)
    nc = CTX // TM
    t = h[0].shape[0] + h[1].shape[0] if split else h.shape[0]

    def body(*refs):
        if split:
            c_ref, x_ref, mod_ref, g_ref, win_hbm, wout_hbm, out_ref, gu_ref, acc_ref, h0_ref, win_v, wout_v = refs
        else:
            h_ref, mod_ref, g_ref, win_hbm, wout_hbm, out_ref, gu_ref, acc_ref, win_v, wout_v = refs
        i = pl.program_id(0)

        @pl.when(i == 0)
        def _():
            pltpu.sync_copy(win_hbm, win_v)
            pltpu.sync_copy(wout_hbm, wout_v)

        if split:
            hh = jnp.where(i < nc, c_ref[...], x_ref[...])
            h0_ref[...] = hh
        else:
            hh = h_ref[...]
        ic = _kind(i)
        z, _, _ = _norm_mod(hh, g_ref[...], _sel(ic, mod_ref, 0), _sel(ic, mod_ref, 1))
        zb = z.astype(BF16)
        acc = jnp.zeros((TM, D), F32)
        for dd in range(4):
            gg = _dot(zb, win_v[dd])
            uu = _dot(zb, win_v[dd + 4])
            gu_ref[:, dd * SHARD_PAD:(dd + 1) * SHARD_PAD] = gg.astype(BF16)
            gu_ref[:, (dd + 4) * SHARD_PAD:(dd + 5) * SHARD_PAD] = uu.astype(BF16)
            a = (gg * _sigmoid(gg)) * uu
            acc = acc + _dot(a.astype(BF16), wout_v[dd * SHARD_PAD:(dd + 1) * SHARD_PAD, :])
        acc_ref[...] = acc
        out_ref[...] = hh + (0.5 * _sel(ic, mod_ref, 2)) * acc

    row = pl.BlockSpec((TM, D), lambda i: (i, 0))
    rshape = jax.ShapeDtypeStruct((t, D), F32)
    if split:
        rows_in = [pl.BlockSpec((TM, D), lambda i: (jnp.minimum(i, nc - 1), 0)),
                   pl.BlockSpec((TM, D), lambda i: (jnp.maximum(i - nc, 0), 0))]
    else:
        rows_in = [row]
    return _grid_call(
        body, name=name, nsteps=t // TM,
        in_specs=rows_in + [pl.BlockSpec((2, 3, D), lambda i: (0, 0, 0)), pl.BlockSpec((1, D), lambda i: (0, 0)), ANY, ANY],
        out_specs=[row, pl.BlockSpec((TM, 2 * HID_PAD), lambda i: (i, 0)), row] + ([row] if split else []),
        out_shape=[rshape, jax.ShapeDtypeStruct((t, 2 * HID_PAD), BF16), rshape] + ([rshape] if split else []),
        scratch_shapes=[pltpu.VMEM((N_DEV, D, SHARD_PAD), BF16), pltpu.VMEM((HID_PAD, D), BF16)],
        args=(*(h if split else (h,)), mod, g, win, wout), rider=rider)


def ffn_bwd_a(dy, acc, gu, mod, wout, name, rider=None):
    t = dy.shape[0]
    nt = t // TM

    def body(dy_ref, acc_ref, gu_ref, mod_ref, wout_hbm, dp_ref, dwout_hbm, dgate_ref, wout_v, dwout_v):
        i = pl.program_id(0)

        @pl.when(i == 0)
        def _():
            pltpu.sync_copy(wout_hbm, wout_v)
            dwout_v[...] = jnp.zeros_like(dwout_v)
            dgate_ref[...] = jnp.zeros_like(dgate_ref)

        dy_ = dy_ref[...]
        ic = _kind(i)
        _acc2(dgate_ref, 0, 0.5 * dy_ * acc_ref[...], ic)
        daccb = ((0.5 * _sel(ic, mod_ref, 2)) * dy_).astype(BF16)
        for dd in range(4):
            blk = slice(dd * SHARD_PAD, (dd + 1) * SHARD_PAD)
            ublk = slice((dd + 4) * SHARD_PAD, (dd + 5) * SHARD_PAD)
            da = _dot_nt(daccb, wout_v[blk, :])
            gg = gu_ref[:, blk].astype(F32)
            uu = gu_ref[:, ublk].astype(F32)
            s = _sigmoid(gg)
            sl = gg * s
            dwout_v[blk, :] += _dot_tn((sl * uu).astype(BF16), daccb)
            dp_ref[:, blk] = (da * uu * (s + sl * (1.0 - s))).astype(BF16)
            dp_ref[:, ublk] = (da * sl).astype(BF16)

        @pl.when(i == nt - 1)
        def _():
            pltpu.sync_copy(dwout_v, dwout_hbm)

    row = pl.BlockSpec((TM, D), lambda i: (i, 0))
    wide = pl.BlockSpec((TM, 2 * HID_PAD), lambda i: (i, 0))
    return _grid_call(
        body, name=name, nsteps=nt,
        in_specs=[row, row, wide, pl.BlockSpec((2, 3, D), lambda i: (0, 0, 0)), ANY],
        out_specs=[wide, ANY, pl.BlockSpec((2, 1, D), lambda i: (0, 0, 0))],
        out_shape=[jax.ShapeDtypeStruct((t, 2 * HID_PAD), BF16), jax.ShapeDtypeStruct((HID_PAD, D), F32),
                   jax.ShapeDtypeStruct((2, 1, D), F32)],
        scratch_shapes=[pltpu.VMEM((HID_PAD, D), BF16), pltpu.VMEM((HID_PAD, D), F32)],
        args=(dy, acc, gu, mod, wout), rider=rider)


def ffn_bwd_b(dy, h, dp, mod, g, win, name, rider=None, latent_only=False):
    t = dy.shape[0]
    nt = t // TM
    nc = CTX // TM

    def body(dy_ref, h_ref, dp_ref, mod_ref, g_ref, win_hbm, dh_ref, dwin_hbm, dmod_ref, dg_ref, win_v, dwin_v):
        i = pl.program_id(0)

        @pl.when(i == 0)
        def _():
            pltpu.sync_copy(win_hbm, win_v)
            dwin_v[...] = jnp.zeros_like(dwin_v)
            dmod_ref[...] = jnp.zeros_like(dmod_ref)
            dg_ref[...] = jnp.zeros_like(dg_ref)

        ic = _kind(i)
        gain = g_ref[...]
        scale = _sel(ic, mod_ref, 1)
        z, n, r = _norm_mod(h_ref[...], gain, _sel(ic, mod_ref, 0), scale)
        zb = z.astype(BF16)
        dz = jnp.zeros((TM, D), F32)
        for dd in range(N_DEV):
            dpd = dp_ref[:, dd * SHARD_PAD:(dd + 1) * SHARD_PAD]
            dz = dz + _dot_nt(dpd, win_v[dd])
            dwin_v[dd] += _dot_tn(zb, dpd)
        _acc2(dmod_ref, 0, dz, ic)
        _acc2(dmod_ref, 1, dz * (n * gain), ic)
        dg_ref[...] += jnp.sum(dz * (1.0 + scale) * n, axis=0, keepdims=True)
        dh_ref[...] = dy_ref[...] + _norm_mod_bwd(dz, n, r, gain, scale)

        @pl.when(i == nt - 1)
        def _():
            pltpu.sync_copy(dwin_v, dwin_hbm)

    row = pl.BlockSpec((TM, D), lambda i: (i, 0))
    if latent_only:
        dh_spec = pl.BlockSpec((TM, D), lambda i: (jnp.maximum(i - nc, 0), 0))
        dh_shape = jax.ShapeDtypeStruct((t - CTX, D), F32)
    else:
        dh_spec, dh_shape = row, jax.ShapeDtypeStruct((t, D), F32)
    return _grid_call(
        body, name=name, nsteps=nt,
        in_specs=[row, row, pl.BlockSpec((TM, 2 * HID_PAD), lambda i: (i, 0)),
                  pl.BlockSpec((2, 3, D), lambda i: (0, 0, 0)), pl.BlockSpec((1, D), lambda i: (0, 0)), ANY],
        out_specs=[dh_spec, ANY, pl.BlockSpec((2, 2, D), lambda i: (0, 0, 0)), pl.BlockSpec((1, D), lambda i: (0, 0))],
        out_shape=[dh_shape, jax.ShapeDtypeStruct((N_DEV, D, SHARD_PAD), F32),
                   jax.ShapeDtypeStruct((2, 2, D), F32), jax.ShapeDtypeStruct((1, D), F32)],
        scratch_shapes=[pltpu.VMEM((N_DEV, D, SHARD_PAD), BF16), pltpu.VMEM((N_DEV, D, SHARD_PAD), F32)],
        args=(dy, h, dp, mod, g, win), rider=rider)


def mix_in_fwd(h, mod, g, wmix, name):
    t = h.shape[0]

    def body(h_ref, mod_ref, g_ref, w_hbm, p_ref, w_v):
        i = pl.program_id(0)

        @pl.when(i == 0)
        def _():
            pltpu.sync_copy(w_hbm, w_v)

        ic = _kind(i)
        z, _, _ = _norm_mod(h_ref[...], g_ref[...], _sel(ic, mod_ref, 0), _sel(ic, mod_ref, 1))
        zb = z.astype(BF16)
        for dd in range(N_DEV):
            p_ref[:, dd * N_MIX_SHARD:(dd + 1) * N_MIX_SHARD] = _dot(zb, w_v[dd])

    return pl.pallas_call(
        body, name=name, grid=(t // TM,),
        in_specs=[pl.BlockSpec((TM, D), lambda i: (i, 0)), pl.BlockSpec((2, 3, D), lambda i: (0, 0, 0)),
                  pl.BlockSpec((1, D), lambda i: (0, 0)), ANY],
        out_specs=pl.BlockSpec((TM, IN_PROJ), lambda i: (i, 0)),
        out_shape=jax.ShapeDtypeStruct((t, IN_PROJ), F32),
        scratch_shapes=[pltpu.VMEM((N_DEV, D, N_MIX_SHARD), BF16)],
        compiler_params=_cp(),
    )(h, mod, g, wmix)


def _halo_specs(nt, tile_of):
    nb = nt * (TM // HALO)
    main = pl.BlockSpec((TM, LRU_W), lambda s: (tile_of(s), 0))
    prev = pl.BlockSpec((HALO, LRU_W), lambda s: (jnp.maximum(tile_of(s) * (TM // HALO) - 1, 0), 0))
    nxt = pl.BlockSpec((HALO, LRU_W), lambda s: (jnp.minimum((tile_of(s) + 1) * (TM // HALO), nb - 1), 0))
    return main, prev, nxt


def _ext(tile, nt, main, prev, nxt):
    has_prev = jnp.logical_and(tile != 0, tile != 1)
    has_next = jnp.logical_and(tile != 0, tile != nt - 1)
    return jnp.concatenate([jnp.where(has_prev, prev, 0.0), main, jnp.where(has_next, nxt, 0.0)], axis=0)


def _shifted(ext, off):
    n = ext.shape[0]
    return pltpu.roll(ext, (-off) % n, 0)[HALO:HALO + TM]


def _conv(ext, cw_ref, cb_ref):
    xc = cb_ref[...] + cw_ref[0:1, :] * _shifted(ext, -2)
    for k in range(1, CONV_W):
        xc = xc + cw_ref[k:k + 1, :] * _shifted(ext, k - 2)
    return xc


def _log1p(y):
    return jnp.where(y < 1e-2, y * (1.0 - y * (0.5 - y * (1.0 / 3.0 - 0.25 * y))), jnp.log(1.0 + y))


def _softplus_neg(lam):
    return jnp.maximum(-lam, 0.0) + _log1p(jnp.exp(-jnp.abs(lam)))


def _one_minus_exp(x, exp_half):
    p = x * (1.0 + x * (1 / 2 + x * (1 / 6 + x * (1 / 24))))
    return jnp.where(x > -0.1, -p, 1.0 - exp_half * exp_half)


def _gates(xc, wr, br, wi, bi, lam):
    xb = xc.astype(BF16)
    r = _sigmoid(_dot(xb, wr) + br)
    ig = _sigmoid(_dot(xb, wi) + bi)
    sp = _softplus_neg(lam)
    log_a = -RG_C * r * sp
    a = jnp.exp(log_a)
    mult = jnp.sqrt(_one_minus_exp(2.0 * log_a, a))
    return r, ig, sp, a, mult


def _scan(a, b, reverse):
    n = a.shape[0]
    row = lax.broadcasted_iota(jnp.int32, a.shape, 0)
    s = 1
    while s < n:
        if s < HALO:
            if reverse:
                keep = row < n - s
                a_s = jnp.where(keep, pltpu.roll(a, n - s, 0), 1.0)
                b_s = jnp.where(keep, pltpu.roll(b, n - s, 0), 0.0)
            else:
                keep = row >= s
                a_s = jnp.where(keep, pltpu.roll(a, s, 0), 1.0)
                b_s = jnp.where(keep, pltpu.roll(b, s, 0), 0.0)
            b = a * b_s + b
            a = a * a_s
        elif reverse:
            b = jnp.concatenate([a[:n - s] * b[s:] + b[:n - s], b[n - s:]], axis=0)
            a = jnp.concatenate([a[:n - s] * a[s:], a[n - s:]], axis=0)
        else:
            b = jnp.concatenate([b[:s], a[s:] * b[:n - s] + b[s:]], axis=0)
            a = jnp.concatenate([a[:s], a[s:] * a[:n - s]], axis=0)
        s *= 2
    return a, b


def lru_fwd(p, conv_w, conv_b, wr, br, wi, bi, lam, reverse, name, rider=None):
    t = p.shape[0]
    nt = t // TM

    def tile_of(s):
        return jnp.where(s == 0, 0, nt - s) if reverse else s

    def body(x_ref, xp_ref, xn_ref, cw_ref, cb_ref, wr_ref, br_ref, wi_ref, bi_ref, lam_ref, h_ref, carry):
        s = pl.program_id(0)
        tile = tile_of(s)

        @pl.when(s == 0)
        def _():
            carry[...] = jnp.zeros_like(carry)

        ext = _ext(tile, nt, x_ref[...], xp_ref[...], xn_ref[...])
        xc = _conv(ext, cw_ref, cb_ref)
        _, ig, _, a, mult = _gates(xc, wr_ref[...], br_ref[...], wi_ref[...], bi_ref[...], lam_ref[...])
        a_cum, hl = _scan(a, mult * (ig * xc), reverse)
        hh = hl + a_cum * carry[...]
        h_ref[...] = hh
        carry[...] = hh[0:1, :] if reverse else hh[TM - 1:TM, :]

    main, prev, nxt = _halo_specs(nt, tile_of)
    vec = pl.BlockSpec((1, LRU_W), lambda s: (0, 0))
    mat = pl.BlockSpec((LRU_W, LRU_W), lambda s: (0, 0))
    outs, got = _grid_call(
        body, name=name, nsteps=nt,
        in_specs=[main, prev, nxt, pl.BlockSpec((CONV_W, LRU_W), lambda s: (0, 0)), vec, mat, vec, mat, vec, vec],
        out_specs=[main],
        out_shape=[jax.ShapeDtypeStruct((t, LRU_W), F32)],
        scratch_shapes=[pltpu.VMEM((1, LRU_W), F32)],
        args=(p, p, p, conv_w, conv_b, wr, br, wi, bi, lam), rider=rider)
    return outs[0], got


def lru_bwd(p, hs, dhs, conv_w, conv_b, wr, br, wi, bi, lam, reverse, name):
    t = p.shape[0]
    nt = t // TM
    bpt = TM // HALO

    def tile_of(s):
        return jnp.where(s == nt - 1, 0, s + 1) if reverse else nt - 1 - s

    def hprev_block(s):
        tile = tile_of(s)
        if reverse:
            return (jnp.where(tile == nt - 1, 0, jnp.minimum((tile + 1) * bpt, nt * bpt - 1)), 0)
        return (jnp.maximum(tile * bpt - 1, 0), 0)

    def body(x_ref, xp_ref, xn_ref, h_ref, hp_ref, dh_ref, cw_ref, cb_ref, wr_ref, br_ref, wi_ref, bi_ref, lam_ref,
             dxc_ref, dwr_ref, dwi_ref, dbr_ref, dbi_ref, dlam_ref, carry):
        s = pl.program_id(0)
        tile = tile_of(s)

        @pl.when(s == 0)
        def _():
            carry[...] = jnp.zeros_like(carry)
            for ref in (dwr_ref, dwi_ref, dbr_ref, dbi_ref, dlam_ref):
                ref[...] = jnp.zeros_like(ref)

        ext = _ext(tile, nt, x_ref[...], xp_ref[...], xn_ref[...])
        xc = _conv(ext, cw_ref, cb_ref)
        wr_, wi_ = wr_ref[...], wi_ref[...]
        r, ig, sp, a, mult = _gates(xc, wr_, br_ref[...], wi_, bi_ref[...], lam_ref[...])
        gated = ig * xc
        row = lax.broadcasted_iota(jnp.int32, (TM, LRU_W), 0)
        hh = h_ref[...]
        start = jnp.where(tile != 0, hp_ref[0:1, :] if reverse else hp_ref[HALO - 1:HALO, :], 0.0)
        if reverse:
            edge = row == TM - 1
            hprev = jnp.where(edge, start, pltpu.roll(hh, TM - 1, 0))
            coef = jnp.where(row == 0, 0.0, pltpu.roll(a, 1, 0))
            bb = dh_ref[...] + jnp.where(row == 0, carry[...], 0.0)
        else:
            edge = row == 0
            hprev = jnp.where(edge, start, pltpu.roll(hh, 1, 0))
            coef = jnp.where(row == TM - 1, 0.0, pltpu.roll(a, TM - 1, 0))
            bb = dh_ref[...] + jnp.where(row == TM - 1, carry[...], 0.0)
        _, lmb = _scan(coef, bb, not reverse)
        al = a * lmb
        carry[...] = al[TM - 1:TM, :] if reverse else al[0:1, :]

        dgated = lmb * mult
        dloga = (lmb * hprev) * a - (lmb * gated) * (a * a) / mult
        dpre_r = (dloga * (-RG_C * sp)) * r * (1.0 - r)
        dpre_i = (dgated * xc) * ig * (1.0 - ig)
        drb, dib = dpre_r.astype(BF16), dpre_i.astype(BF16)
        xb = xc.astype(BF16)
        dxc_ref[...] = dgated * ig + _dot_nt(drb, wr_) + _dot_nt(dib, wi_)
        dwr_ref[...] += _dot_tn(xb, drb)
        dwi_ref[...] += _dot_tn(xb, dib)
        dbr_ref[...] += jnp.sum(dpre_r, axis=0, keepdims=True)
        dbi_ref[...] += jnp.sum(dpre_i, axis=0, keepdims=True)
        dlam_ref[...] += jnp.sum(dloga * (-RG_C * r), axis=0, keepdims=True)

        @pl.when(s == nt - 1)
        def _():
            dlam_ref[...] = dlam_ref[...] * (-_sigmoid(-lam_ref[...]))

    main, prev, nxt = _halo_specs(nt, tile_of)
    vec = pl.BlockSpec((1, LRU_W), lambda s: (0, 0))
    mat = pl.BlockSpec((LRU_W, LRU_W), lambda s: (0, 0))
    vshape = jax.ShapeDtypeStruct((1, LRU_W), F32)
    mshape = jax.ShapeDtypeStruct((LRU_W, LRU_W), F32)
    return pl.pallas_call(
        body, name=name, grid=(nt,),
        in_specs=[main, prev, nxt, main, pl.BlockSpec((HALO, LRU_W), hprev_block), main,
                  pl.BlockSpec((CONV_W, LRU_W), lambda s: (0, 0)), vec, mat, vec, mat, vec, vec],
        out_specs=[main, mat, mat, vec, vec, vec],
        out_shape=[jax.ShapeDtypeStruct((t, LRU_W), F32), mshape, mshape, vshape, vshape, vshape],
        scratch_shapes=[pltpu.VMEM((1, LRU_W), F32)],
        compiler_params=_cp(),
    )(p, p, p, hs, hs, dhs, conv_w, conv_b, wr, br, wi, bi, lam)


GELU_C = 0.7978845608028654
GELU_A = 0.044715


def _gelu(x):
    th = jnp.tanh(GELU_C * (x + GELU_A * x * x * x))
    return 0.5 * x * (1.0 + th), th


def _sgu(v, gain, w_ref, bt_ref):
    mu = jnp.mean(v, axis=-1, keepdims=True)
    xc = v - mu
    rs = lax.rsqrt(jnp.mean(xc * xc, axis=-1, keepdims=True) + EPS)
    vhat = xc * rs
    vnb = (vhat * gain).astype(BF16)
    chunks = []
    for ch in range(TM // CHUNK):
        zs = []
        for gi in range(GROUPS):
            vb = vnb[ch * CHUNK:(ch + 1) * CHUNK, gi * GROUP_DIM:(gi + 1) * GROUP_DIM]
            zs.append(_dot(w_ref[gi].astype(BF16), vb) + bt_ref[:, gi:gi + 1])
        chunks.append(jnp.concatenate(zs, axis=1))
    return jnp.concatenate(chunks, axis=0), vhat, rs, vnb


def _pcols(k):
    return pl.BlockSpec((TM, LRU_W), lambda i: (i, k))


def mix_out_fwd(h, p, hf, hb, mod, sgu_g, sgu_w, sgu_bt, womix, name):
    t = h.shape[0]

    def body(h_ref, gl_ref, u_ref, v_ref, hf_ref, hb_ref, mod_ref, sg_ref, sw_ref, sb_ref, w_hbm, out_ref, o_ref, w_v):
        i = pl.program_id(0)

        @pl.when(i == 0)
        def _():
            pltpu.sync_copy(w_hbm, w_v)

        ic = _kind(i)
        ge, _ = _gelu(gl_ref[...])
        y_lru = (hf_ref[...] + hb_ref[...]) * ge
        z, _, _, _ = _sgu(v_ref[...], sg_ref[...], sw_ref, sb_ref)
        y = jnp.concatenate([y_lru, u_ref[...] * z], axis=1).astype(BF16)
        o = _dot(y, w_v[...])
        o_ref[...] = o
        out_ref[...] = h_ref[...] + _sel(ic, mod_ref, 2) * o

    row = pl.BlockSpec((TM, D), lambda i: (i, 0))
    half = pl.BlockSpec((TM, LRU_W), lambda i: (i, 0))
    return pl.pallas_call(
        body, name=name, grid=(t // TM,),
        in_specs=[row, _pcols(1), _pcols(2), _pcols(3), half, half, pl.BlockSpec((2, 3, D), lambda i: (0, 0, 0)),
                  pl.BlockSpec((1, MLP_W), lambda i: (0, 0)), pl.BlockSpec((GROUPS, CHUNK, CHUNK), lambda i: (0, 0, 0)),
                  pl.BlockSpec((CHUNK, GROUPS), lambda i: (0, 0)), ANY],
        out_specs=[row, row],
        out_shape=[jax.ShapeDtypeStruct((t, D), F32), jax.ShapeDtypeStruct((t, D), F32)],
        scratch_shapes=[pltpu.VMEM((D, D), BF16)],
        compiler_params=_cp(),
    )(h, p, p, p, hf, hb, mod, sgu_g, sgu_w, sgu_bt, womix)


def mix_out_bwd(dy, p, hf, hb, o, mod, sgu_g, sgu_w, sgu_bt, womix, name, rider=None):
    t = dy.shape[0]

    def body(dy_ref, gl_ref, u_ref, v_ref, hf_ref, hb_ref, o_ref, mod_ref, sg_ref, sw_ref, sb_ref, w_hbm,
             dhs_ref, dp_ref, dw_ref, dgate_ref, dsg_ref, dsw_ref, dsb_ref, w_v):
        i = pl.program_id(0)

        @pl.when(i == 0)
        def _():
            pltpu.sync_copy(w_hbm, w_v)
            for ref in (dw_ref, dgate_ref, dsg_ref, dsw_ref, dsb_ref):
                ref[...] = jnp.zeros_like(ref)

        ic = _kind(i)
        dy_ = dy_ref[...]
        _acc2(dgate_ref, 0, dy_ * o_ref[...], ic)
        dob = (_sel(ic, mod_ref, 2) * dy_).astype(BF16)

        gl = gl_ref[...]
        ge, th = _gelu(gl)
        hsum = hf_ref[...] + hb_ref[...]
        gain = sg_ref[...]
        uu = u_ref[...]
        z, vhat, rs, vnb = _sgu(v_ref[...], gain, sw_ref, sb_ref)
        y = jnp.concatenate([hsum * ge, uu * z], axis=1).astype(BF16)
        dw_ref[...] += _dot_tn(y, dob)
        dyy = _dot_nt(dob, w_v[...])
        dyl, dys = dyy[:, :LRU_W], dyy[:, LRU_W:]

        dhs_ref[...] = dyl * ge
        dge = 0.5 * (1.0 + th) + 0.5 * gl * (1.0 - th * th) * (GELU_C * (1.0 + 3.0 * GELU_A * gl * gl))
        dp_ref[:, 0:LRU_W] = dyl * hsum * dge
        dp_ref[:, LRU_W:2 * LRU_W] = dys * z

        dz = dys * uu
        dzb = dz.astype(BF16)
        dvn_chunks, dsb_cols = [], [jnp.zeros((CHUNK, 1), F32)] * GROUPS
        for ch in range(TM // CHUNK):
            cols = []
            for gi in range(GROUPS):
                rs_, cs_ = slice(ch * CHUNK, (ch + 1) * CHUNK), slice(gi * GROUP_DIM, (gi + 1) * GROUP_DIM)
                dzg = dzb[rs_, cs_]
                dsb_cols[gi] = dsb_cols[gi] + jnp.sum(dz[rs_, cs_], axis=1, keepdims=True)
                dsw_ref[gi] += _dot_nt(dzg, vnb[rs_, cs_])
                cols.append(_dot_tn(sw_ref[gi].astype(BF16), dzg))
            dvn_chunks.append(jnp.concatenate(cols, axis=1))
        dsb_ref[...] += jnp.concatenate(dsb_cols, axis=1)
        dvn = jnp.concatenate(dvn_chunks, axis=0)
        dsg_ref[...] += jnp.sum(dvn * vhat, axis=0, keepdims=True)
        dvh = dvn * gain
        dp_ref[:, 2 * LRU_W:3 * LRU_W] = rs * (dvh - jnp.mean(dvh, axis=-1, keepdims=True)
                                               - vhat * jnp.mean(dvh * vhat, axis=-1, keepdims=True))

    row = pl.BlockSpec((TM, D), lambda i: (i, 0))
    half = pl.BlockSpec((TM, LRU_W), lambda i: (i, 0))
    const2 = lambda i: (0, 0)
    const3 = lambda i: (0, 0, 0)
    return _grid_call(
        body, name=name, nsteps=t // TM,
        in_specs=[row, _pcols(1), _pcols(2), _pcols(3), half, half, row, pl.BlockSpec((2, 3, D), const3),
                  pl.BlockSpec((1, MLP_W), const2), pl.BlockSpec((GROUPS, CHUNK, CHUNK), const3),
                  pl.BlockSpec((CHUNK, GROUPS), const2), ANY],
        out_specs=[half, pl.BlockSpec((TM, 3 * LRU_W), lambda i: (i, 0)), pl.BlockSpec((D, D), const2),
                   pl.BlockSpec((2, 1, D), const3), pl.BlockSpec((1, MLP_W), const2),
                   pl.BlockSpec((GROUPS, CHUNK, CHUNK), const3), pl.BlockSpec((CHUNK, GROUPS), const2)],
        out_shape=[jax.ShapeDtypeStruct((t, LRU_W), F32), jax.ShapeDtypeStruct((t, 3 * LRU_W), F32),
                   jax.ShapeDtypeStruct((D, D), F32), jax.ShapeDtypeStruct((2, 1, D), F32),
                   jax.ShapeDtypeStruct((1, MLP_W), F32), jax.ShapeDtypeStruct((GROUPS, CHUNK, CHUNK), F32),
                   jax.ShapeDtypeStruct((CHUNK, GROUPS), F32)],
        scratch_shapes=[pltpu.VMEM((D, D), BF16)],
        args=(dy, p, p, p, hf, hb, o, mod, sgu_g, sgu_w, sgu_bt, womix), rider=rider)


def mix_in_bwd(dy, h, p, dxf, dxb, dprest, mod, g, conv_w, wmix, name, rider=None):
    t = dy.shape[0]
    nt = t // TM

    def body(dy_ref, h_ref, x_ref, xp_ref, xn_ref, f_ref, fp_ref, fn_ref, b_ref, bp_ref, bn_ref, dpr_ref, mod_ref,
             g_ref, cw_ref, w_hbm, dh_ref, dw_hbm, dmod_ref, dg_ref, dcw_ref, dcb_ref, w_v, dw_v):
        i = pl.program_id(0)

        @pl.when(i == 0)
        def _():
            pltpu.sync_copy(w_hbm, w_v)
            dw_v[...] = jnp.zeros_like(dw_v)
            for ref in (dmod_ref, dg_ref, dcw_ref, dcb_ref):
                ref[...] = jnp.zeros_like(ref)

        dmain = f_ref[...] + b_ref[...]
        dext = _ext(i, nt, dmain, fp_ref[...] + bp_ref[...], fn_ref[...] + bn_ref[...])
        xext = _ext(i, nt, x_ref[...], xp_ref[...], xn_ref[...])
        dxl = cw_ref[0:1, :] * _shifted(dext, 2)
        for k in range(1, CONV_W):
            dxl = dxl + cw_ref[k:k + 1, :] * _shifted(dext, 2 - k)
        dcw_ref[...] += jnp.concatenate(
            [jnp.sum(dmain * _shifted(xext, k - 2), axis=0, keepdims=True) for k in range(CONV_W)], axis=0)
        dcb_ref[...] += jnp.sum(dmain, axis=0, keepdims=True)

        ic = _kind(i)
        gain = g_ref[...]
        scale = _sel(ic, mod_ref, 1)
        z, n, r = _norm_mod(h_ref[...], gain, _sel(ic, mod_ref, 0), scale)
        zb = z.astype(BF16)
        dpb = jnp.concatenate([dxl, dpr_ref[...]], axis=1).astype(BF16)
        dz = jnp.zeros((TM, D), F32)
        for dd in range(N_DEV):
            dpd = dpb[:, dd * N_MIX_SHARD:(dd + 1) * N_MIX_SHARD]
            dz = dz + _dot_nt(dpd, w_v[dd])
            dw_v[dd] += _dot_tn(zb, dpd)
        _acc2(dmod_ref, 0, dz, ic)
        _acc2(dmod_ref, 1, dz * (n * gain), ic)
        dg_ref[...] += jnp.sum(dz * (1.0 + scale) * n, axis=0, keepdims=True)
        dh_ref[...] = dy_ref[...] + _norm_mod_bwd(dz, n, r, gain, scale)

        @pl.when(i == nt - 1)
        def _():
            pltpu.sync_copy(dw_v, dw_hbm)

    main, prev, nxt = _halo_specs(nt, lambda s: s)
    row = pl.BlockSpec((TM, D), lambda i: (i, 0))
    const2 = lambda i: (0, 0)
    return _grid_call(
        body, name=name, nsteps=nt,
        in_specs=[row, row, main, prev, nxt, main, prev, nxt, main, prev, nxt,
                  pl.BlockSpec((TM, 3 * LRU_W), lambda i: (i, 0)), pl.BlockSpec((2, 3, D), lambda i: (0, 0, 0)),
                  pl.BlockSpec((1, D), const2), pl.BlockSpec((CONV_W, LRU_W), const2), ANY],
        out_specs=[row, ANY, pl.BlockSpec((2, 2, D), lambda i: (0, 0, 0)), pl.BlockSpec((1, D), const2),
                   pl.BlockSpec((CONV_W, LRU_W), const2), pl.BlockSpec((1, LRU_W), const2)],
        out_shape=[jax.ShapeDtypeStruct((t, D), F32), jax.ShapeDtypeStruct((N_DEV, D, N_MIX_SHARD), F32),
                   jax.ShapeDtypeStruct((2, 2, D), F32), jax.ShapeDtypeStruct((1, D), F32),
                   jax.ShapeDtypeStruct((CONV_W, LRU_W), F32), jax.ShapeDtypeStruct((1, LRU_W), F32)],
        scratch_shapes=[pltpu.VMEM((N_DEV, D, N_MIX_SHARD), BF16), pltpu.VMEM((N_DEV, D, N_MIX_SHARD), F32)],
        args=(dy, h, p, p, p, dxf, dxf, dxf, dxb, dxb, dxb, dprest, mod, g, conv_w, wmix), rider=rider)


def loss_head(h, g, target, name):
    t = h.shape[0]
    nc = CTX // TM

    def body(h_ref, g_ref, t_ref, dh_ref, loss_ref, dg_ref):
        i = pl.program_id(0)

        @pl.when(i == 0)
        def _():
            loss_ref[...] = jnp.zeros_like(loss_ref)
            dg_ref[...] = jnp.zeros_like(dg_ref)

        @pl.when(i < nc)
        def _():
            dh_ref[...] = jnp.zeros_like(dh_ref)

        @pl.when(i >= nc)
        def _():
            hh = h_ref[...]
            gain = g_ref[...]
            r = lax.rsqrt(jnp.mean(hh * hh, axis=-1, keepdims=True) + EPS)
            n = hh * r
            err = n * gain - t_ref[...]
            loss_ref[...] += 0.5 * jnp.sum(jnp.mean(err * err, axis=-1, keepdims=True))
            dy = err * (1.0 / D)
            dg_ref[...] += jnp.sum(dy * n, axis=0, keepdims=True)
            dn = dy * gain
            dh_ref[...] = r * (dn - n * jnp.mean(dn * n, axis=-1, keepdims=True))

    row = pl.BlockSpec((TM, D), lambda i: (i, 0))
    return pl.pallas_call(
        body, name=name, grid=(t // TM,),
        in_specs=[row, pl.BlockSpec((1, D), lambda i: (0, 0)),
                  pl.BlockSpec((TM, D), lambda i: (jnp.maximum(i - nc, 0), 0))],
        out_specs=[row, pl.BlockSpec((8, 128), lambda i: (0, 0)), pl.BlockSpec((1, D), lambda i: (0, 0))],
        out_shape=[jax.ShapeDtypeStruct((t, D), F32), jax.ShapeDtypeStruct((8, 128), F32),
                   jax.ShapeDtypeStruct((1, D), F32)],
        compiler_params=_cp(),
    )(h, g, target)


def pad_in_shard(w):
    z = jnp.zeros(w.shape[:-1] + (HALF_PAD - HALF,), w.dtype)
    return jnp.concatenate([w[..., :HALF], z, w[..., HALF:], z], axis=-1)


def unpad_in_shard(g):
    return jnp.concatenate([g[..., :HALF], g[..., HALF_PAD:HALF_PAD + HALF]], axis=-1)


def pad_out_shard(w):
    return jnp.concatenate([w, jnp.zeros(w.shape[:-2] + (HALF_PAD - HALF, w.shape[-1]), w.dtype)], axis=-2)


def unpad_out_shard(g):
    return g[..., :HALF, :]


def _block_diag(w):
    eye = jnp.eye(HEADS, dtype=w.dtype)
    return jnp.einsum("dhij,hk->dhikj", w, eye).reshape(2, LRU_W, LRU_W)


def _block_diag_inv(full):
    f = full.reshape(2, HEADS, HEAD_DIM, HEADS, HEAD_DIM)
    return jnp.stack([f[:, hd, :, hd, :] for hd in range(HEADS)], axis=1)


def small_layer(g1, gm, g2, conv_w, conv_b, w_r, b_r, w_i, b_i, lam, sgu_g, sgu_w, sgu_b):
    return dict(g1=g1[None, :], gm=gm[None, :], g2=g2[None, :], conv_w=conv_w, conv_b=conv_b[None, :],
                wr=_block_diag(w_r).astype(BF16), br=b_r[:, None, :], wi=_block_diag(w_i).astype(BF16),
                bi=b_i[:, None, :], lam=lam[:, None, :], sgu_g=sgu_g[None, :], sgu_w=sgu_w, sgu_bt=sgu_b.T)


def small_grads(g):
    return dict(ffn1_norm_g=g["g1"][0], mix_norm_g=g["gm"][0], ffn2_norm_g=g["g2"][0], lru_conv_w=g["conv_w"],
                lru_conv_b=g["conv_b"][0], lru_w_r=_block_diag_inv(g["wr"]), lru_b_r=g["br"][:, 0, :],
                lru_w_i=_block_diag_inv(g["wi"]), lru_b_i=g["bi"][:, 0, :], lru_lambda=g["lam"][:, 0, :],
                sgu_norm_g=g["sgu_g"][0], sgu_w=g["sgu_w"], sgu_b=g["sgu_bt"].T)


BIG_KEYS = ("win1", "wout1", "wmix", "womix", "win2", "wout2")


def _as_blocks(key, g):
    if key in ("wout1", "wout2"):
        return g.reshape(N_DEV, HALF_PAD, D)
    return g.reshape(N_DEV, OMIX_SHARD, D) if key == "womix" else g


def _gathered(key, a):
    if key in ("wout1", "wout2"):
        return a.reshape(HID_PAD, D)
    return a.reshape(D, D) if key == "womix" else a


class _ReduceScatter:
    def __init__(self, c_idx, where):
        self.c_idx, self.where = c_idx, where
        self.out = {}

    def pair(self, group):
        return pair_rider([g for _, g in group])

    def after_pair(self, group, recv1, tag):
        parts = [pair_sum(g, r, self.c_idx, f"pair_sum_{tag}_{i}") for i, ((_, g), r) in enumerate(zip(group, recv1))]
        return chips_rider(parts)

    def after_chips(self, group, recv1, recv2, tag):
        for i, ((key, g), r1, r2) in enumerate(zip(group, recv1, recv2)):
            self.out[key] = final_sum(g, r1, r2, self.where, f"final_sum_{tag}_{i}")


def fwd_bwd(ctx_rows, x_rows, target, mods, shards, smalls, final_g, c_idx, where):
    assert CTX == TM and len(shards) == 2

    def gather(keys_by_layer):
        return GatherRider([shards[l][k] for l, k in keys_by_layer])

    def put(full, keys_by_layer, got):
        for (l, k), a in zip(keys_by_layer, got):
            full[l][k] = _gathered(k, a)

    full = [dict(s) for s in smalls]
    first = [(0, "win1"), (0, "wout1")]
    put(full, first, run_alone(gather(first), pl.ANY, "gather_w0"))
    riders = {
        "ffn1_fwd_0": [(0, "wmix"), (0, "womix"), (0, "win2")],
        "lru_fwd_0_0": [(0, "wout2")],
        "ffn2_fwd_0": [(1, "win1"), (1, "wout1")],
        "ffn1_fwd_1": [(1, "wmix"), (1, "womix"), (1, "win2")],
        "lru_fwd_1_0": [(1, "wout2")],
    }

    def ffn(which, l, h):
        name = f"ffn{which}_fwd_{l}"
        w = full[l]
        keys = riders.get(name)
        m = mods[l][:, 0:3] if which == 1 else mods[l][:, 6:9]
        outs, got = ffn_fwd(h, m, w[f"g{which}"], w[f"win{which}"], w[f"wout{which}"], name,
                            rider=gather(keys) if keys else None)
        if keys:
            put(full, keys, got)
        return outs

    saved = []
    h = (ctx_rows, x_rows)
    for l in range(2):
        mm = mods[l][:, 3:6]
        outs = ffn(1, l, h)
        h1, gu1, acc1 = outs[:3]
        hin = outs[3] if l == 0 else h
        w = full[l]
        p = mix_in_fwd(h1, mm, w["gm"], w["wmix"], f"mix_in_fwd_{l}")
        hs = []
        for d in range(2):
            keys = riders.get(f"lru_fwd_{l}_{d}")
            hd, got = lru_fwd(p, w["conv_w"], w["conv_b"], w["wr"][d], w["br"][d], w["wi"][d], w["bi"][d], w["lam"][d],
                              bool(d), f"lru_fwd_{l}_{d}", rider=gather(keys) if keys else None)
            if keys:
                put(full, keys, got)
            hs.append(hd)
        h2, o = mix_out_fwd(h1, p, hs[0], hs[1], mm, w["sgu_g"], w["sgu_w"], w["sgu_bt"], w["womix"], f"mix_out_fwd_{l}")
        h3, gu2, acc2 = ffn(2, l, h2)
        saved.append((hin, h1, h2, gu1, acc1, p, hs, o, gu2, acc2))
        h = h3
    dh, loss, dgf = loss_head(h, final_g, target, "loss_head")

    rs = _ReduceScatter(c_idx, where)
    grads, dmods, sums = [None, None], [None, None], [None, None]
    pending = None
    for l in (1, 0):
        w = full[l]
        m1, mm, m2 = mods[l][:, 0:3], mods[l][:, 3:6], mods[l][:, 6:9]
        hin, h1, h2, gu1, acc1, p, hs, o, gu2, acc2 = saved[l]
        g = {}
        rs.out = {}
        both = Riders([rs.pair(pending[0]), GatherRider([small_pack])]) if pending else None
        (dp2, g["wout2"], dgate2), got = ffn_bwd_a(dh, acc2, gu2, m2, w["wout2"], f"ffn2_bwd_a_{l}", rider=both)
        if pending:
            r1, (small_all,) = both.split(got)
        chips = rs.after_pair(pending[0], r1, pending[1]) if pending else None
        (dh, g["win2"], dmod2, g["g2"]), r2 = ffn_bwd_b(dh, h2, dp2, m2, w["g2"], w["win2"], f"ffn2_bwd_b_{l}", rider=chips)
        if pending:
            rs.after_chips(pending[0], r1, r2, pending[1])
            sums[l + 1].update(rs.out)
            rs.out = {}

        grp = [(k, _as_blocks(k, g[k])) for k in ("win2", "wout2")]
        (dhs, dprest, g["womix"], dgatem, g["sgu_g"], g["sgu_w"], g["sgu_bt"]), r1 = mix_out_bwd(
            dh, p, hs[0], hs[1], o, mm, w["sgu_g"], w["sgu_w"], w["sgu_bt"], w["womix"], f"mix_out_bwd_{l}",
            rider=rs.pair(grp))
        chips = rs.after_pair(grp, r1, f"a{l}")
        dx, per_dir = [], []
        for d in range(2):
            out = lru_bwd(p, hs[d], dhs, w["conv_w"], w["conv_b"], w["wr"][d], w["br"][d], w["wi"][d], w["bi"][d],
                          w["lam"][d], bool(d), f"lru_bwd_{l}_{d}")
            dx.append(out[0])
            per_dir.append(out[1:])
        for k, nm in enumerate(("wr", "wi", "br", "bi", "lam")):
            g[nm] = jnp.stack([per_dir[0][k], per_dir[1][k]])
        (dh, g["wmix"], dmodm, g["gm"], g["conv_w"], g["conv_b"]), r2 = mix_in_bwd(
            dh, h1, p, dx[0], dx[1], dprest, mm, w["gm"], w["conv_w"], w["wmix"], f"mix_in_bwd_{l}", rider=chips)
        rs.after_chips(grp, r1, r2, f"a{l}")
        sums[l] = dict(rs.out)
        rs.out = {}

        if l == 1:
            (dp1, g["wout1"], dgate1), _ = ffn_bwd_a(dh, acc1, gu1, m1, w["wout1"], f"ffn1_bwd_a_{l}")
            (dh, g["win1"], dmod1, g["g1"]), _ = ffn_bwd_b(dh, hin, dp1, m1, w["g1"], w["win1"], f"ffn1_bwd_b_{l}")
            pending = ([(k, _as_blocks(k, g[k])) for k in ("womix", "wmix", "wout1", "win1")], f"b{l}")
            per = small_grads(g)
            small_pack = _pack([per[n] for n in LAYER_SMALL])
        else:
            g_mix = [(k, _as_blocks(k, g[k])) for k in ("womix", "wmix")]
            (dp1, g["wout1"], dgate1), r1_mix = ffn_bwd_a(dh, acc1, gu1, m1, w["wout1"], f"ffn1_bwd_a_{l}",
                                                          rider=rs.pair(g_mix))
            g_out = [("wout1", _as_blocks("wout1", g["wout1"]))]
            both = Riders([rs.after_pair(g_mix, r1_mix, f"b{l}"), rs.pair(g_out)])
            (dh, g["win1"], dmod1, g["g1"]), got = ffn_bwd_b(dh, hin, dp1, m1, w["g1"], w["win1"], f"ffn1_bwd_b_{l}",
                                                             rider=both, latent_only=True)
            r2_mix, r1_out = both.split(got)
            rs.after_chips(g_mix, r1_mix, r2_mix, f"b{l}")
            g_in = [("win1", g["win1"])]
            both = Riders([rs.after_pair(g_out, r1_out, f"c{l}"), rs.pair(g_in)])
            r2_out, r1_in = both.split(run_alone(both, pl.ANY, "rs_tail_0"))
            rs.after_chips(g_out, r1_out, r2_out, f"c{l}")
            r2_in = run_alone(rs.after_pair(g_in, r1_in, f"d{l}"), pl.ANY, "rs_tail_1")
            rs.after_chips(g_in, r1_in, r2_in, f"d{l}")
            sums[l].update(rs.out)
        dmods[l] = jnp.concatenate([dmod1, dgate1, dmodm, dgatem, dmod2, dgate2], axis=1)
        grads[l] = g
    return loss, dh, jnp.stack(dmods), grads, small_all, sums, dgf


def _row_block(r, c, limit=262144):
    best = 8
    for rb in range(8, r + 1, 8):
        if r % rb == 0 and rb * c <= limit:
            best = rb
    return best


def pair_sum(grads, recv, c_idx, name):
    _, r, c = grads.shape
    rb = _row_block(r, c)

    def body(c_ref, g_ref, r_ref, o_ref):
        o_ref[...] = (g_ref[...] + r_ref[...]).astype(BF16)

    return pl.pallas_call(
        body, name=name,
        grid_spec=pltpu.PrefetchScalarGridSpec(
            num_scalar_prefetch=1, grid=(4, r // rb),
            in_specs=[pl.BlockSpec((1, rb, c), lambda j, i, c_ref: (2 * j + c_ref[0], i, 0)),
                      pl.BlockSpec((1, rb, c), lambda j, i, c_ref: (j, i, 0))],
            out_specs=pl.BlockSpec((1, rb, c), lambda j, i, c_ref: (j, i, 0))),
        out_shape=jax.ShapeDtypeStruct((4, r, c), BF16),
        compiler_params=_cp(2),
    )(c_idx, grads, recv)


def final_sum(grads, recv1, recv2, where, name):
    _, r, c = grads.shape
    rb = _row_block(r, c)

    def body(w_ref, g_ref, r1_ref, r2_ref, o_ref):
        far = (r2_ref[0].astype(F32) + r2_ref[1].astype(F32)) + r2_ref[2].astype(F32)
        o_ref[...] = (g_ref[0] + r1_ref[0]) + far

    return pl.pallas_call(
        body, name=name,
        grid_spec=pltpu.PrefetchScalarGridSpec(
            num_scalar_prefetch=1, grid=(r // rb,),
            in_specs=[pl.BlockSpec((1, rb, c), lambda i, w_ref: (w_ref[0], i, 0)),
                      pl.BlockSpec((1, rb, c), lambda i, w_ref: (w_ref[1], i, 0)),
                      pl.BlockSpec((3, rb, c), lambda i, w_ref: (0, i, 0))],
            out_specs=pl.BlockSpec((rb, c), lambda i, w_ref: (i, 0))),
        out_shape=jax.ShapeDtypeStruct((r, c), F32),
        compiler_params=_cp(1),
    )(where, grads, recv1, recv2)


ADA_ROWS = 16


def _silu(v):
    return v * _sigmoid(v)


def ada_fwd(cond, w_ada, b_slab, name):
    def body(c_ref, w_ref, b_ref, o_ref):
        s = _silu(c_ref[...]).astype(BF16)
        o_ref[0] = _dot(s, w_ref[0].astype(BF16)) + b_ref[0]

    return pl.pallas_call(
        body, name=name, grid=(DEPTH,),
        in_specs=[pl.BlockSpec((ADA_ROWS, D), lambda l: (0, 0)), pl.BlockSpec((1, D, ADA_SHARD), lambda l: (l, 0, 0)),
                  pl.BlockSpec((1, 1, ADA_SHARD), lambda l: (l, 0, 0))],
        out_specs=pl.BlockSpec((1, ADA_ROWS, ADA_SHARD), lambda l: (l, 0, 0)),
        out_shape=jax.ShapeDtypeStruct((DEPTH, ADA_ROWS, ADA_SHARD), F32),
        compiler_params=_cp(),
    )(cond, w_ada, b_slab)


def ada_bwd(cond, dm_sample, dm_ctx, w_ada, name):
    def body(c_ref, ds_ref, dc_ref, w_ref, gw_ref, dsc_ref):
        @pl.when(pl.program_id(0) == 0)
        def _():
            dsc_ref[...] = jnp.zeros_like(dsc_ref)

        s = _silu(c_ref[...]).astype(BF16)
        dcs = dc_ref[0]
        tot = dcs[0:1]
        for j in range(1, N_DEV):
            tot = tot + dcs[j:j + 1]
        tot8 = jnp.where(lax.broadcasted_iota(jnp.int32, (N_DEV, ADA_SHARD), 0) == 0, tot, 0.0)
        dm = jnp.concatenate([ds_ref[0], tot8], axis=0).astype(BF16)
        gw_ref[0] = _dot_tn(s, dm)
        dsc_ref[...] += _dot_nt(dm, w_ref[0].astype(BF16))[N_DEV:N_DEV + 1]

    slab = pl.BlockSpec((1, N_DEV, ADA_SHARD), lambda l: (l, 0, 0))
    wspec = pl.BlockSpec((1, D, ADA_SHARD), lambda l: (l, 0, 0))
    return pl.pallas_call(
        body, name=name, grid=(DEPTH,),
        in_specs=[pl.BlockSpec((ADA_ROWS, D), lambda l: (0, 0)), slab, slab, wspec],
        out_specs=[wspec, pl.BlockSpec((1, D), lambda l: (0, 0))],
        out_shape=[jax.ShapeDtypeStruct((DEPTH, D, ADA_SHARD), F32), jax.ShapeDtypeStruct((1, D), F32)],
        compiler_params=_cp(),
    )(cond, dm_sample, dm_ctx, w_ada)


def sum_over_devices(parts, name, silu_rows=0, w=None):
    _, r, c = parts.shape

    def body(*refs):
        p_ref, o_ref = refs[0], refs[-1]
        tot = p_ref[0]
        for j in range(1, N_DEV):
            tot = tot + p_ref[j]
        o_ref[...] = tot
        if silu_rows:
            wv = refs[1][...]
            s = _sigmoid(wv)
            o_ref[0:silu_rows, :] = tot[0:silu_rows, :] * (s * (1.0 + wv * (1.0 - s)))

    vm = pl.BlockSpec(memory_space=pltpu.VMEM)
    args = (parts,) if w is None else (parts, w)
    return pl.pallas_call(
        body, name=name, in_specs=[vm] * len(args), out_specs=vm,
        out_shape=jax.ShapeDtypeStruct((r, c), F32),
        compiler_params=pltpu.CompilerParams(vmem_limit_bytes=VMEM_LIMIT),
    )(*args)


def sum_dmods(dm_all, name):
    def body(d_ref, o_ref):
        for l in range(DEPTH):
            tot = d_ref[0, l]
            for j in range(1, N_DEV):
                tot = tot + d_ref[j, l]
            o_ref[l:l + 1, :] = tot[0:1] + tot[1:2]

    vm = pl.BlockSpec(memory_space=pltpu.VMEM)
    return pl.pallas_call(
        body, name=name, in_specs=[vm], out_specs=vm,
        out_shape=jax.ShapeDtypeStruct((DEPTH, N_MOD * D), F32),
    )(dm_all)


def adamw(w, g, m, v, name):
    r, c = w.shape
    rb = _row_block(r, c, limit=131072)

    def body(w_ref, g_ref, m_ref, v_ref, d_ref, nm_ref, nv_ref):
        g_ = g_ref[...]
        nm = B1 * m_ref[...] + (1.0 - B1) * g_
        nv = B2 * v_ref[...] + (1.0 - B2) * (g_ * g_)
        nm_ref[...] = nm
        nv_ref[...] = nv
        m_hat = nm / (1.0 - B1 ** STEP)
        v_hat = nv / (1.0 - B2 ** STEP)
        d_ref[...] = -LR * (m_hat / (jnp.sqrt(v_hat) + ADAM_EPS) + WD * w_ref[...])

    blk = pl.BlockSpec((rb, c), lambda i: (i, 0))
    shp = jax.ShapeDtypeStruct((r, c), F32)
    return pl.pallas_call(
        body, name=name, grid=(r // rb,), in_specs=[blk] * 4, out_specs=[blk] * 3, out_shape=[shp] * 3,
        compiler_params=_cp(),
    )(w, g, m, v)


def _adamw_nd(w, g, m, v, name):
    shape = w.shape
    flat = lambda a: a.reshape(-1, shape[-1])
    return tuple(o.reshape(shape) for o in adamw(flat(w), flat(g), flat(m), flat(v), name))


LANES = 128


PACK_UNIT = 8 * LANES


ADAMW_SMALL_ROWS = 512


def _pack(arrays, row_multiple=8):
    pieces, n = [], 0
    for a in arrays:
        pieces.append(a.reshape(-1).astype(F32))
        pad = (-a.size) % PACK_UNIT
        if pad:
            pieces.append(jnp.zeros((pad,), F32))
        n += a.size + pad
    tail = (-n) % (row_multiple * LANES)
    if tail:
        pieces.append(jnp.zeros((tail,), F32))
    return jnp.concatenate(pieces).reshape(-1, LANES)


def _unpack(packed, shapes):
    out, r0 = [], 0
    lead = packed.shape[:-2]
    for shp in shapes:
        size = 1
        for s in shp:
            size *= s
        nr = 8 * -(-size // PACK_UNIT)
        blk = packed[..., r0:r0 + nr, :].reshape(lead + (nr * LANES,))[..., :size]
        out.append(blk.reshape(lead + tuple(shp)))
        r0 += nr
    return out


WEIGHTS = ["c_ctx", "w_ada", "b_ada", "ffn1_norm_g", "ffn1_w_in", "ffn1_w_out", "mix_norm_g", "w_in_mix", "lru_conv_w",
           "lru_conv_b", "lru_w_r", "lru_b_r", "lru_w_i", "lru_b_i", "lru_lambda", "sgu_norm_g", "sgu_w", "sgu_b",
           "w_out_mix", "ffn2_norm_g", "ffn2_w_in", "ffn2_w_out", "final_norm_g"]
BIG = ["w_ada", "ffn1_w_in", "ffn1_w_out", "w_in_mix", "w_out_mix", "ffn2_w_in", "ffn2_w_out"]
SHARDED_SMALL = ["lru_conv_w", "lru_b_r", "lru_b_i", "lru_lambda"]
LAYER_SMALL = ["ffn1_norm_g", "mix_norm_g", "ffn2_norm_g", "lru_conv_w", "lru_conv_b", "lru_w_r", "lru_b_r", "lru_w_i",
               "lru_b_i", "lru_lambda", "sgu_norm_g", "sgu_w", "sgu_b"]
LRU_SHARD = LRU_W // N_DEV


def _widen(a):
    return jnp.moveaxis(a, 0, -2).reshape(a.shape[1:-1] + (LRU_W,))


def kernel(x, c, ctx, c_ctx, w_ada, b_ada, ffn1_norm_g, ffn1_w_in, ffn1_w_out, mix_norm_g, w_in_mix, lru_conv_w, lru_conv_b, lru_w_r, lru_b_r, lru_w_i, lru_b_i, lru_lambda, sgu_norm_g, sgu_w, sgu_b, w_out_mix, ffn2_norm_g, ffn2_w_in, ffn2_w_out, final_norm_g, loss_target, m_c_ctx, m_w_ada, m_b_ada, m_ffn1_norm_g, m_ffn1_w_in, m_ffn1_w_out, m_mix_norm_g, m_w_in_mix, m_lru_conv_w, m_lru_conv_b, m_lru_w_r, m_lru_b_r, m_lru_w_i, m_lru_b_i, m_lru_lambda, m_sgu_norm_g, m_sgu_w, m_sgu_b, m_w_out_mix, m_ffn2_norm_g, m_ffn2_w_in, m_ffn2_w_out, m_final_norm_g, v_c_ctx, v_w_ada, v_b_ada, v_ffn1_norm_g, v_ffn1_w_in, v_ffn1_w_out, v_mix_norm_g, v_w_in_mix, v_lru_conv_w, v_lru_conv_b, v_lru_w_r, v_lru_b_r, v_lru_w_i, v_lru_b_i, v_lru_lambda, v_sgu_norm_g, v_sgu_w, v_sgu_b, v_w_out_mix, v_ffn2_norm_g, v_ffn2_w_in, v_ffn2_w_out, v_final_norm_g):
    given = dict(locals())
    W = {n: given[n] for n in WEIGHTS}
    M = {n: given["m_" + n] for n in WEIGHTS}
    V = {n: given["v_" + n] for n in WEIGHTS}
    xi, yi, ci = _position()
    me = 4 * xi + 2 * yi + ci
    chip = 2 * xi + yi

    sharded_shapes = [W[n].shape for n in SHARDED_SMALL]
    got = run_alone(GatherRider([_pack([c[0]] + [W[n] for n in SHARDED_SMALL])]), pltpu.VMEM, "gather_small")[0]
    parts = _unpack(got, [(D,)] + sharded_shapes)
    c_all = parts[0]
    wide = {n: _widen(a) for n, a in zip(SHARDED_SMALL, parts[1:])}
    cond = jnp.concatenate([c_all, c_ctx[None, :], jnp.zeros((ADA_ROWS - N_DEV - 1, D), F32)], axis=0)
    b_slab = lax.dynamic_slice_in_dim(b_ada, me * ADA_SHARD, ADA_SHARD, axis=1)[:, None, :]
    slabs = ada_fwd(cond, w_ada, b_slab, "ada_fwd")
    mall = run_alone(GatherRider([slabs.reshape(DEPTH * ADA_ROWS, ADA_SHARD)]), pltpu.VMEM, "gather_mod")[0]
    mall = mall.reshape(N_DEV, DEPTH, ADA_ROWS, ADA_SHARD)
    m_sample = lax.dynamic_index_in_dim(mall, me, axis=2, keepdims=False)
    m_ctx = mall[:, :, N_DEV, :]
    mods = jnp.stack([jnp.transpose(m, (1, 0, 2)).reshape(DEPTH, N_MOD, D) for m in (m_ctx, m_sample)], axis=1)

    shards, smalls = [], []
    for l in range(DEPTH):
        sh = dict(win1=pad_in_shard(ffn1_w_in[l]), wout1=pad_out_shard(ffn1_w_out[l]), wmix=w_in_mix[l],
                  womix=w_out_mix[l], win2=pad_in_shard(ffn2_w_in[l]), wout2=pad_out_shard(ffn2_w_out[l]))
        shards.append({k: a.astype(BF16) for k, a in sh.items()})
        smalls.append(small_layer(ffn1_norm_g[l], mix_norm_g[l], ffn2_norm_g[l], wide["lru_conv_w"][l], lru_conv_b[l],
                                  lru_w_r[l], wide["lru_b_r"][l], lru_w_i[l], wide["lru_b_i"][l], wide["lru_lambda"][l],
                                  sgu_norm_g[l], sgu_w[l], sgu_b[l]))

    c_idx = ci.reshape(1).astype(jnp.int32)
    where = jnp.stack([me, chip]).astype(jnp.int32)
    loss_blk, dx, dmods, grads, small1_all, gsum, dgf = fwd_bwd(ctx[0], x[0], loss_target[0], mods, shards, smalls,
                                                                final_norm_g[None, :], c_idx, where)
    G = {
        "ffn1_w_in": jnp.stack([unpad_in_shard(gsum[l]["win1"]) for l in range(DEPTH)]),
        "ffn1_w_out": jnp.stack([unpad_out_shard(gsum[l]["wout1"]) for l in range(DEPTH)]),
        "w_in_mix": jnp.stack([gsum[l]["wmix"] for l in range(DEPTH)]),
        "w_out_mix": jnp.stack([gsum[l]["womix"] for l in range(DEPTH)]),
        "ffn2_w_in": jnp.stack([unpad_in_shard(gsum[l]["win2"]) for l in range(DEPTH)]),
        "ffn2_w_out": jnp.stack([unpad_out_shard(gsum[l]["wout2"]) for l in range(DEPTH)]),
    }

    n_rows = DEPTH * 2 * N_MOD
    dm_rows = jnp.concatenate([dmods.reshape(n_rows, D), jnp.zeros((-n_rows % 8, D), F32)], axis=0)
    dm_all = run_alone(GatherRider([dm_rows]), pltpu.VMEM, "gather_dmod")[0][:, :n_rows]
    dm_all = dm_all.reshape(N_DEV, DEPTH, 2, N_MOD * D)
    mine = lax.dynamic_slice_in_dim(dm_all, me * ADA_SHARD, ADA_SHARD, axis=3)
    G["w_ada"], dsc = ada_bwd(cond, jnp.transpose(mine[:, :, 1, :], (1, 0, 2)), jnp.transpose(mine[:, :, 0, :], (1, 0, 2)),
                              w_ada, "ada_bwd")
    G["b_ada"] = sum_dmods(dm_all, "sum_dmods")

    per0 = small_grads(grads[0])
    shapes0 = [per0[n].shape for n in LAYER_SMALL]
    got = run_alone(GatherRider([_pack([dsc[0], dgf[0]] + [per0[n] for n in LAYER_SMALL])]), pltpu.VMEM,
                    "gather_small_grads")[0]
    sum0 = _unpack(sum_over_devices(got, "sum_small_grads_0", silu_rows=D // LANES, w=c_ctx.reshape(D // LANES, LANES)),
                   [(D,), (D,)] + shapes0)
    sum1 = _unpack(sum_over_devices(small1_all, "sum_small_grads_1"), shapes0)
    G["c_ctx"], G["final_norm_g"] = sum0[0], sum0[1]
    for n, a0, a1 in zip(LAYER_SMALL, sum0[2:], sum1):
        a = jnp.stack([a0, a1])
        G[n] = lax.dynamic_slice_in_dim(a, me * LRU_SHARD, LRU_SHARD, axis=a.ndim - 1) if n in SHARDED_SMALL else a

    delta, new_m, new_v = {}, {}, {}
    for n in BIG:
        delta[n], new_m[n], new_v[n] = _adamw_nd(W[n], G[n], M[n], V[n], f"adamw_{n}")
    rest = [n for n in WEIGHTS if n not in BIG]
    shapes = [W[n].shape for n in rest]
    outs = adamw(*[_pack([src[n] for n in rest], row_multiple=ADAMW_SMALL_ROWS) for src in (W, G, M, V)], "adamw_small")
    for dst, packed in zip((delta, new_m, new_v), outs):
        for n, a in zip(rest, _unpack(packed, shapes)):
            dst[n] = a

    loss = lax.psum(loss_blk[0, 0], ("x", "y", "c"))
    grad_x = dx[None]
    return (loss, grad_x, *[G[n] for n in WEIGHTS], *[delta[n] for n in WEIGHTS], *[new_m[n] for n in WEIGHTS],
            *[new_v[n] for n in WEIGHTS])
```

```python
import functools

import jax
import jax.numpy as jnp
from jax import lax
from jax.experimental import pallas as pl
from jax.experimental.pallas import tpu as pltpu

F32 = jnp.float32
BF16 = jnp.bfloat16

D = 1024
CTX = 256
DEPTH = 2
EPS = 1e-6
D_FF = 2816
LRU_W = 512
HEADS = 8
HEAD_DIM = 64
CONV_W = 4
RG_C = 8.0
GROUPS = 4
GROUP_DIM = 128
CHUNK = 128
MLP_W = 512
IN_PROJ = 2048
N_MOD = 9
N_DEV = 8

LR = 0.001
B1 = 0.9
B2 = 0.999
ADAM_EPS = 1e-08
WD = 0.01
STEP = 10

SHARD_FF_IN = 704
HALF = 352
HALF_PAD = 384
SHARD_PAD = 2 * HALF_PAD
HID_PAD = 4 * SHARD_PAD
N_MIX_SHARD = IN_PROJ // N_DEV
OMIX_SHARD = D // N_DEV
ADA_SHARD = N_MOD * D // N_DEV

TM = 256
HALO = 8
VMEM_LIMIT = 60 * 1024 * 1024

MESH = pl.DeviceIdType.MESH
ANY = pl.BlockSpec(memory_space=pl.ANY)


def _cp(n_axes=1):
    return pltpu.CompilerParams(dimension_semantics=("arbitrary",) * n_axes, vmem_limit_bytes=VMEM_LIMIT)


def _position():
    return lax.axis_index("x"), lax.axis_index("y"), lax.axis_index("c")


class GatherRider:
    def __init__(self, shards):
        n = len(shards)
        self.n = n
        self.ins = list(shards)
        self.out_shape = [jax.ShapeDtypeStruct((N_DEV,) + s.shape, s.dtype) for s in shards]
        self.sems = [pltpu.SemaphoreType.DMA((n, 7)), pltpu.SemaphoreType.DMA((n, 7)), pltpu.SemaphoreType.DMA((n,))]

    def _ctx(self, outs, sems):
        x, y, c = _position()
        chips = [(1 - x, y), (x, 1 - y), (1 - x, 1 - y)]

        def copy(t, k, block, to, src=None):
            dst = outs[t].at[4 * block[0] + 2 * block[1] + block[2]]
            return pltpu.make_async_remote_copy(
                src_ref=dst if src is None else src, dst_ref=dst, send_sem=sems[0].at[t, k],
                recv_sem=sems[1].at[t, k], device_id=to, device_id_type=MESH)

        return (x, y, c), (x, y, 1 - c), chips, copy

    def _local(self, ins, outs, sems, t):
        x, y, c = _position()
        return pltpu.make_async_copy(ins[t], outs[t].at[4 * x + 2 * y + c], sems[2].at[t])

    def _first(self, ins, outs, sems, t):
        me, sibling, chips, copy = self._ctx(outs, sems)
        return [copy(t, 0, me, sibling, src=ins[t])] + [copy(t, 1 + j, me, (*chip, me[2]), src=ins[t])
                                                         for j, chip in enumerate(chips)]

    def start(self, ins, outs, sems):
        for t in range(self.n):
            self._local(ins, outs, sems, t).start()
            for cp in self._first(ins, outs, sems, t):
                cp.start()

    def mid(self, ins, outs, sems):
        me, sibling, chips, copy = self._ctx(outs, sems)
        for j, chip in enumerate(chips):
            for t in range(self.n):
                copy(t, 1 + j, (*chip, me[2]), me).wait_recv()
                copy(t, 4 + j, (*chip, me[2]), sibling).start()

    def finish(self, ins, outs, sems):
        me, sibling, chips, copy = self._ctx(outs, sems)
        for t in range(self.n):
            copy(t, 0, sibling, me).wait_recv()
            for j, chip in enumerate(chips):
                copy(t, 4 + j, (*chip, 1 - me[2]), me).wait_recv()
        for t in range(self.n):
            for cp in self._first(ins, outs, sems, t):
                cp.wait_send()
            for j, chip in enumerate(chips):
                copy(t, 4 + j, (*chip, me[2]), sibling).wait_send()
            self._local(ins, outs, sems, t).wait()


class ExchangeRider:
    def __init__(self, tensors, plan, n_slots):
        n = len(tensors)
        self.n, self.plan = n, plan
        self.ins = list(tensors)
        self.out_shape = [jax.ShapeDtypeStruct((n_slots,) + s.shape[1:], s.dtype) for s in tensors]
        self.sems = [pltpu.SemaphoreType.DMA((n, n_slots)), pltpu.SemaphoreType.DMA((n, n_slots))]

    def _copies(self, ins, outs, sems):
        return [pltpu.make_async_remote_copy(
            src_ref=ins[t].at[block], dst_ref=outs[t].at[k], send_sem=sems[0].at[t, k], recv_sem=sems[1].at[t, k],
            device_id=to, device_id_type=MESH)
            for t in range(self.n) for k, (block, to) in enumerate(self.plan(*_position()))]

    def start(self, ins, outs, sems):
        for cp in self._copies(ins, outs, sems):
            cp.start()

    def mid(self, ins, outs, sems):
        pass

    def finish(self, ins, outs, sems):
        for cp in self._copies(ins, outs, sems):
            cp.wait()


class Riders:
    def __init__(self, riders):
        self.riders = list(riders)
        self.ins = [a for r in self.riders for a in r.ins]
        self.out_shape = [s for r in self.riders for s in r.out_shape]
        self.sems = [s for r in self.riders for s in r.sems]

    def _each(self, ins, outs, sems):
        i = o = s = 0
        for r in self.riders:
            ni, no, ns = len(r.ins), len(r.out_shape), len(r.sems)
            yield r, ins[i:i + ni], outs[o:o + no], sems[s:s + ns]
            i, o, s = i + ni, o + no, s + ns

    def start(self, ins, outs, sems):
        for r, a, b, c in self._each(ins, outs, sems):
            r.start(a, b, c)

    def mid(self, ins, outs, sems):
        for r, a, b, c in self._each(ins, outs, sems):
            r.mid(a, b, c)

    def finish(self, ins, outs, sems):
        for r, a, b, c in self._each(ins, outs, sems):
            r.finish(a, b, c)

    def split(self, outs):
        res, o = [], 0
        for r in self.riders:
            res.append(list(outs[o:o + len(r.out_shape)]))
            o += len(r.out_shape)
        return res


def pair_rider(grads):
    def plan(x, y, c):
        return [(4 * cx + 2 * cy + (1 - c), (x, y, 1 - c)) for cx in range(2) for cy in range(2)]
    return ExchangeRider(grads, plan, 4)


def chips_rider(parts):
    def plan(x, y, c):
        return [(2 * cx + cy, (cx, cy, c)) for cx, cy in [(1 - x, y), (x, 1 - y), (1 - x, 1 - y)]]
    return ExchangeRider(parts, plan, 3)


def run_alone(rider, space, name):
    ni = len(rider.ins)
    no = len(rider.out_shape)

    def body(*refs):
        ins, outs, sems = refs[:ni], refs[ni:ni + no], refs[ni + no:]
        rider.start(ins, outs, sems)
        rider.mid(ins, outs, sems)
        rider.finish(ins, outs, sems)

    spec = pl.BlockSpec(memory_space=space)
    return pl.pallas_call(
        body, name=name, in_specs=[spec] * ni, out_specs=[spec] * no, out_shape=rider.out_shape,
        scratch_shapes=rider.sems, compiler_params=pltpu.CompilerParams(vmem_limit_bytes=VMEM_LIMIT),
    )(*rider.ins)


def _grid_call(body, *, name, nsteps, in_specs, out_specs, out_shape, scratch_shapes, args, rider=None, aliases=None):
    aliases = aliases or {}
    if rider is None:
        outs = pl.pallas_call(body, name=name, grid=(nsteps,), in_specs=in_specs, out_specs=out_specs,
                              out_shape=out_shape, scratch_shapes=scratch_shapes, input_output_aliases=aliases,
                              compiler_params=_cp())(*args)
        return outs, []
    ni, no, ns = len(in_specs), len(out_specs), len(scratch_shapes)
    ri, ro = len(rider.ins), len(rider.out_shape)

    def wrapped(*refs):
        ins, refs = refs[:ni], refs[ni:]
        r_ins, refs = refs[:ri], refs[ri:]
        outs, refs = refs[:no], refs[no:]
        r_outs, refs = refs[:ro], refs[ro:]
        scratch, r_sems = refs[:ns], refs[ns:]
        s = pl.program_id(0)

        @pl.when(s == 0)
        def _():
            rider.start(r_ins, r_outs, r_sems)

        body(*ins, *outs, *scratch)

        @pl.when(s == (3 * nsteps) // 4)
        def _():
            rider.mid(r_ins, r_outs, r_sems)

        @pl.when(s == nsteps - 1)
        def _():
            rider.finish(r_ins, r_outs, r_sems)

    outs = pl.pallas_call(
        wrapped, name=name, grid=(nsteps,), in_specs=list(in_specs) + [ANY] * ri, out_specs=list(out_specs) + [ANY] * ro,
        out_shape=list(out_shape) + rider.out_shape, scratch_shapes=list(scratch_shapes) + rider.sems,
        input_output_aliases=aliases, compiler_params=_cp())(*args, *rider.ins)
    return outs[:no], outs[no:]


def _dot(a, b):
    return jnp.dot(a, b, preferred_element_type=F32)


def _dot_nt(a, b):
    return lax.dot_general(a, b, (((1,), (1,)), ((), ())), preferred_element_type=F32)


def _dot_tn(a, b):
    return lax.dot_general(a, b, (((0,), (0,)), ((), ())), preferred_element_type=F32)


def _sigmoid(x):
    return 1.0 / (1.0 + jnp.exp(-x))


def _kind(i):
    return jnp.where(i < CTX // TM, 0, 1)


def _sel(kind, mod_ref, k):
    return mod_ref[kind, k:k + 1, :]


def _acc2(ref, k, val, kind):
    ref[kind, k:k + 1, :] += jnp.sum(val, axis=0, keepdims=True)


def _norm_mod(h, g, shift, scale):
    r = lax.rsqrt(jnp.mean(h * h, axis=-1, keepdims=True) + EPS)
    n = h * r
    return (n * g) * (1.0 + scale) + shift, n, r


def _norm_mod_bwd(dz, n, r, g, scale):
    dn = dz * (g * (1.0 + scale))
    return r * (dn - n * jnp.mean(dn * n, axis=-1, keepdims=True))


def ffn_fwd(h, mod, g, win, wout, name, rider=None):
    split = isinstance(h, tuple)
    nc = CTX // TM
    t = h[0].shape[0] + h[1].shape[0] if split else h.shape[0]

    def body(*refs):
        if split:
            c_ref, x_ref, mod_ref, g_ref, win_hbm, wout_hbm, out_ref, gu_ref, acc_ref, h0_ref, win_v, wout_v = refs
        else:
            h_ref, mod_ref, g_ref, win_hbm, wout_hbm, out_ref, gu_ref, acc_ref, win_v, wout_v = refs
        i = pl.program_id(0)

        @pl.when(i == 0)
        def _():
            pltpu.sync_copy(win_hbm, win_v)
            pltpu.sync_copy(wout_hbm, wout_v)

        if split:
            hh = jnp.where(i < nc, c_ref[...], x_ref[...])
            h0_ref[...] = hh
        else:
            hh = h_ref[...]
        ic = _kind(i)
        z, _, _ = _norm_mod(hh, g_ref[...], _sel(ic, mod_ref, 0), _sel(ic, mod_ref, 1))
        zb = z.astype(BF16)
        acc = jnp.zeros((TM, D), F32)
        for dd in range(4):
            gg = _dot(zb, win_v[dd])
            uu = _dot(zb, win_v[dd + 4])
            gu_ref[:, dd * SHARD_PAD:(dd + 1) * SHARD_PAD] = gg.astype(BF16)
            gu_ref[:, (dd + 4) * SHARD_PAD:(dd + 5) * SHARD_PAD] = uu.astype(BF16)
            a = (gg * _sigmoid(gg)) * uu
            acc = acc + _dot(a.astype(BF16), wout_v[dd * SHARD_PAD:(dd + 1) * SHARD_PAD, :])
        acc_ref[...] = acc
        out_ref[...] = hh + (0.5 * _sel(ic, mod_ref, 2)) * acc

    row = pl.BlockSpec((TM, D), lambda i: (i, 0))
    rshape = jax.ShapeDtypeStruct((t, D), F32)
    if split:
        rows_in = [pl.BlockSpec((TM, D), lambda i: (jnp.minimum(i, nc - 1), 0)),
                   pl.BlockSpec((TM, D), lambda i: (jnp.maximum(i - nc, 0), 0))]
    else:
        rows_in = [row]
    return _grid_call(
        body, name=name, nsteps=t // TM,
        in_specs=rows_in + [pl.BlockSpec((2, 3, D), lambda i: (0, 0, 0)), pl.BlockSpec((1, D), lambda i: (0, 0)), ANY, ANY],
        out_specs=[row, pl.BlockSpec((TM, 2 * HID_PAD), lambda i: (i, 0)), row] + ([row] if split else []),
        out_shape=[rshape, jax.ShapeDtypeStruct((t, 2 * HID_PAD), BF16), rshape] + ([rshape] if split else []),
        scratch_shapes=[pltpu.VMEM((N_DEV, D, SHARD_PAD), BF16), pltpu.VMEM((HID_PAD, D), BF16)],
        args=(*(h if split else (h,)), mod, g, win, wout), rider=rider)


def ffn_bwd_a(dy, acc, gu, mod, wout, name, rider=None):
    t = dy.shape[0]
    nt = t // TM

    def body(dy_ref, acc_ref, gu_ref, mod_ref, wout_hbm, dp_ref, dwout_hbm, dgate_ref, wout_v, dwout_v):
        i = pl.program_id(0)

        @pl.when(i == 0)
        def _():
            pltpu.sync_copy(wout_hbm, wout_v)
            dwout_v[...] = jnp.zeros_like(dwout_v)
            dgate_ref[...] = jnp.zeros_like(dgate_ref)

        dy_ = dy_ref[...]
        ic = _kind(i)
        _acc2(dgate_ref, 0, 0.5 * dy_ * acc_ref[...], ic)
        daccb = ((0.5 * _sel(ic, mod_ref, 2)) * dy_).astype(BF16)
        for dd in range(4):
            blk = slice(dd * SHARD_PAD, (dd + 1) * SHARD_PAD)
            ublk = slice((dd + 4) * SHARD_PAD, (dd + 5) * SHARD_PAD)
            da = _dot_nt(daccb, wout_v[blk, :])
            gg = gu_ref[:, blk].astype(F32)
            uu = gu_ref[:, ublk].astype(F32)
            s = _sigmoid(gg)
            sl = gg * s
            dwout_v[blk, :] += _dot_tn((sl * uu).astype(BF16), daccb)
            dp_ref[:, blk] = (da * uu * (s + sl * (1.0 - s))).astype(BF16)
            dp_ref[:, ublk] = (da * sl).astype(BF16)

        @pl.when(i == nt - 1)
        def _():
            pltpu.sync_copy(dwout_v, dwout_hbm)

    row = pl.BlockSpec((TM, D), lambda i: (i, 0))
    wide = pl.BlockSpec((TM, 2 * HID_PAD), lambda i: (i, 0))
    return _grid_call(
        body, name=name, nsteps=nt,
        in_specs=[row, row, wide, pl.BlockSpec((2, 3, D), lambda i: (0, 0, 0)), ANY],
        out_specs=[wide, ANY, pl.BlockSpec((2, 1, D), lambda i: (0, 0, 0))],
        out_shape=[jax.ShapeDtypeStruct((t, 2 * HID_PAD), BF16), jax.ShapeDtypeStruct((HID_PAD, D), F32),
                   jax.ShapeDtypeStruct((2, 1, D), F32)],
        scratch_shapes=[pltpu.VMEM((HID_PAD, D), BF16), pltpu.VMEM((HID_PAD, D), F32)],
        args=(dy, acc, gu, mod, wout), rider=rider)


def ffn_bwd_b(dy, h, dp, mod, g, win, name, rider=None, latent_only=False):
    t = dy.shape[0]
    nt = t // TM
    nc = CTX // TM

    def body(dy_ref, h_ref, dp_ref, mod_ref, g_ref, win_hbm, dh_ref, dwin_hbm, dmod_ref, dg_ref, win_v, dwin_v):
        i = pl.program_id(0)

        @pl.when(i == 0)
        def _():
            pltpu.sync_copy(win_hbm, win_v)
            dwin_v[...] = jnp.zeros_like(dwin_v)
            dmod_ref[...] = jnp.zeros_like(dmod_ref)
            dg_ref[...] = jnp.zeros_like(dg_ref)

        ic = _kind(i)
        gain = g_ref[...]
        scale = _sel(ic, mod_ref, 1)
        z, n, r = _norm_mod(h_ref[...], gain, _sel(ic, mod_ref, 0), scale)
        zb = z.astype(BF16)
        dz = jnp.zeros((TM, D), F32)
        for dd in range(N_DEV):
            dpd = dp_ref[:, dd * SHARD_PAD:(dd + 1) * SHARD_PAD]
            dz = dz + _dot_nt(dpd, win_v[dd])
            dwin_v[dd] += _dot_tn(zb, dpd)
        _acc2(dmod_ref, 0, dz, ic)
        _acc2(dmod_ref, 1, dz * (n * gain), ic)
        dg_ref[...] += jnp.sum(dz * (1.0 + scale) * n, axis=0, keepdims=True)
        dh_ref[...] = dy_ref[...] + _norm_mod_bwd(dz, n, r, gain, scale)

        @pl.when(i == nt - 1)
        def _():
            pltpu.sync_copy(dwin_v, dwin_hbm)

    row = pl.BlockSpec((TM, D), lambda i: (i, 0))
    if latent_only:
        dh_spec = pl.BlockSpec((TM, D), lambda i: (jnp.maximum(i - nc, 0), 0))
        dh_shape = jax.ShapeDtypeStruct((t - CTX, D), F32)
    else:
        dh_spec, dh_shape = row, jax.ShapeDtypeStruct((t, D), F32)
    return _grid_call(
        body, name=name, nsteps=nt,
        in_specs=[row, row, pl.BlockSpec((TM, 2 * HID_PAD), lambda i: (i, 0)),
                  pl.BlockSpec((2, 3, D), lambda i: (0, 0, 0)), pl.BlockSpec((1, D), lambda i: (0, 0)), ANY],
        out_specs=[dh_spec, ANY, pl.BlockSpec((2, 2, D), lambda i: (0, 0, 0)), pl.BlockSpec((1, D), lambda i: (0, 0))],
        out_shape=[dh_shape, jax.ShapeDtypeStruct((N_DEV, D, SHARD_PAD), F32),
                   jax.ShapeDtypeStruct((2, 2, D), F32), jax.ShapeDtypeStruct((1, D), F32)],
        scratch_shapes=[pltpu.VMEM((N_DEV, D, SHARD_PAD), BF16), pltpu.VMEM((N_DEV, D, SHARD_PAD), F32)],
        args=(dy, h, dp, mod, g, win), rider=rider)


def ffn_bwd_dw(h, dp, mod, g, name, rider=None):
    t = h.shape[0]
    nt = t // TM

    def body(h_ref, dp_ref, mod_ref, g_ref, dwin_hbm, dwin_v):
        i = pl.program_id(0)

        @pl.when(i == 0)
        def _():
            dwin_v[...] = jnp.zeros_like(dwin_v)

        ic = _kind(i)
        z, _, _ = _norm_mod(h_ref[...], g_ref[...], _sel(ic, mod_ref, 0), _sel(ic, mod_ref, 1))
        zb = z.astype(BF16)
        for dd in range(N_DEV):
            dwin_v[dd] += _dot_tn(zb, dp_ref[:, dd * SHARD_PAD:(dd + 1) * SHARD_PAD])

        @pl.when(i == nt - 1)
        def _():
            pltpu.sync_copy(dwin_v, dwin_hbm)

    return _grid_call(
        body, name=name, nsteps=nt,
        in_specs=[pl.BlockSpec((TM, D), lambda i: (i, 0)), pl.BlockSpec((TM, 2 * HID_PAD), lambda i: (i, 0)),
                  pl.BlockSpec((2, 3, D), lambda i: (0, 0, 0)), pl.BlockSpec((1, D), lambda i: (0, 0))],
        out_specs=[ANY], out_shape=[jax.ShapeDtypeStruct((N_DEV, D, SHARD_PAD), F32)],
        scratch_shapes=[pltpu.VMEM((N_DEV, D, SHARD_PAD), F32)],
        args=(h, dp, mod, g), rider=rider)


def ffn_bwd_dh(dy, h, dp, mod, g, win, name, tiles, carry=None, rider=None):
    t = dy.shape[0]
    nc = CTX // TM
    t0, t1 = tiles

    def body(*refs):
        if carry is None:
            dy_ref, h_ref, dp_ref, mod_ref, g_ref, win_hbm, dh_ref, dmod_ref, dg_ref, win_v = refs
        else:
            dy_ref, h_ref, dp_ref, mod_ref, g_ref, win_hbm, _, dmod0_ref, dg0_ref, dh_ref, dmod_ref, dg_ref, win_v = refs
        i = pl.program_id(0)

        @pl.when(i == 0)
        def _():
            pltpu.sync_copy(win_hbm, win_v)
            dmod_ref[...] = jnp.zeros_like(dmod_ref) if carry is None else dmod0_ref[...]
            dg_ref[...] = jnp.zeros_like(dg_ref) if carry is None else dg0_ref[...]

        ic = _kind(i + t0)
        gain = g_ref[...]
        scale = _sel(ic, mod_ref, 1)
        _, n, r = _norm_mod(h_ref[...], gain, _sel(ic, mod_ref, 0), scale)
        dz = jnp.zeros((TM, D), F32)
        for dd in range(N_DEV):
            dz = dz + _dot_nt(dp_ref[:, dd * SHARD_PAD:(dd + 1) * SHARD_PAD], win_v[dd])
        _acc2(dmod_ref, 0, dz, ic)
        _acc2(dmod_ref, 1, dz * (n * gain), ic)
        dg_ref[...] += jnp.sum(dz * (1.0 + scale) * n, axis=0, keepdims=True)
        dh_ref[...] = dy_ref[...] + _norm_mod_bwd(dz, n, r, gain, scale)

    row = pl.BlockSpec((TM, D), lambda i: (i + t0, 0))
    small = [pl.BlockSpec((2, 2, D), lambda i: (0, 0, 0)), pl.BlockSpec((1, D), lambda i: (0, 0))]
    in_specs = [row, row, pl.BlockSpec((TM, 2 * HID_PAD), lambda i: (i + t0, 0)),
                pl.BlockSpec((2, 3, D), lambda i: (0, 0, 0)), pl.BlockSpec((1, D), lambda i: (0, 0)), ANY]
    args = (dy, h, dp, mod, g, win)
    if carry is not None:
        in_specs += [ANY] + small
        args += tuple(carry)
    return _grid_call(
        body, name=name, nsteps=t1 - t0, in_specs=in_specs,
        out_specs=[pl.BlockSpec((TM, D), lambda i: (jnp.maximum(i + t0 - nc, 0), 0))] + small,
        out_shape=[jax.ShapeDtypeStruct((t - CTX, D), F32), jax.ShapeDtypeStruct((2, 2, D), F32),
                   jax.ShapeDtypeStruct((1, D), F32)],
        scratch_shapes=[pltpu.VMEM((N_DEV, D, SHARD_PAD), BF16)],
        args=args, rider=rider, aliases=None if carry is None else {6: 0})


def mix_in_fwd(h, mod, g, wmix, name):
    t = h.shape[0]

    def body(h_ref, mod_ref, g_ref, w_hbm, p_ref, w_v):
        i = pl.program_id(0)

        @pl.when(i == 0)
        def _():
            pltpu.sync_copy(w_hbm, w_v)

        ic = _kind(i)
        z, _, _ = _norm_mod(h_ref[...], g_ref[...], _sel(ic, mod_ref, 0), _sel(ic, mod_ref, 1))
        zb = z.astype(BF16)
        for dd in range(N_DEV):
            p_ref[:, dd * N_MIX_SHARD:(dd + 1) * N_MIX_SHARD] = _dot(zb, w_v[dd])

    return pl.pallas_call(
        body, name=name, grid=(t // TM,),
        in_specs=[pl.BlockSpec((TM, D), lambda i: (i, 0)), pl.BlockSpec((2, 3, D), lambda i: (0, 0, 0)),
                  pl.BlockSpec((1, D), lambda i: (0, 0)), ANY],
        out_specs=pl.BlockSpec((TM, IN_PROJ), lambda i: (i, 0)),
        out_shape=jax.ShapeDtypeStruct((t, IN_PROJ), F32),
        scratch_shapes=[pltpu.VMEM((N_DEV, D, N_MIX_SHARD), BF16)],
        compiler_params=_cp(),
    )(h, mod, g, wmix)


def _halo_specs(nt, tile_of):
    nb = nt * (TM // HALO)
    main = pl.BlockSpec((TM, LRU_W), lambda s: (tile_of(s), 0))
    prev = pl.BlockSpec((HALO, LRU_W), lambda s: (jnp.maximum(tile_of(s) * (TM // HALO) - 1, 0), 0))
    nxt = pl.BlockSpec((HALO, LRU_W), lambda s: (jnp.minimum((tile_of(s) + 1) * (TM // HALO), nb - 1), 0))
    return main, prev, nxt


def _ext(tile, nt, main, prev, nxt):
    has_prev = jnp.logical_and(tile != 0, tile != 1)
    has_next = jnp.logical_and(tile != 0, tile != nt - 1)
    return jnp.concatenate([jnp.where(has_prev, prev, 0.0), main, jnp.where(has_next, nxt, 0.0)], axis=0)


def _shifted(ext, off):
    n = ext.shape[0]
    return pltpu.roll(ext, (-off) % n, 0)[HALO:HALO + TM]


def _conv(ext, cw_ref, cb_ref):
    xc = cb_ref[...] + cw_ref[0:1, :] * _shifted(ext, -2)
    for k in range(1, CONV_W):
        xc = xc + cw_ref[k:k + 1, :] * _shifted(ext, k - 2)
    return xc


def _log1p(y):
    return jnp.where(y < 1e-2, y * (1.0 - y * (0.5 - y * (1.0 / 3.0 - 0.25 * y))), jnp.log(1.0 + y))


def _softplus_neg(lam):
    return jnp.maximum(-lam, 0.0) + _log1p(jnp.exp(-jnp.abs(lam)))


def _one_minus_exp(x, exp_half):
    p = x * (1.0 + x * (1 / 2 + x * (1 / 6 + x * (1 / 24))))
    return jnp.where(x > -0.1, -p, 1.0 - exp_half * exp_half)


def _gates(xc, wr, br, wi, bi, lam):
    xb = xc.astype(BF16)
    r = _sigmoid(_dot(xb, wr) + br)
    ig = _sigmoid(_dot(xb, wi) + bi)
    sp = _softplus_neg(lam)
    log_a = -RG_C * r * sp
    a = jnp.exp(log_a)
    mult = jnp.sqrt(_one_minus_exp(2.0 * log_a, a))
    return r, ig, sp, a, mult


def _scan(a, b, reverse):
    n = a.shape[0]
    row = lax.broadcasted_iota(jnp.int32, a.shape, 0)
    s = 1
    while s < n:
        if s < HALO:
            if reverse:
                keep = row < n - s
                a_s = jnp.where(keep, pltpu.roll(a, n - s, 0), 1.0)
                b_s = jnp.where(keep, pltpu.roll(b, n - s, 0), 0.0)
            else:
                keep = row >= s
                a_s = jnp.where(keep, pltpu.roll(a, s, 0), 1.0)
                b_s = jnp.where(keep, pltpu.roll(b, s, 0), 0.0)
            b = a * b_s + b
            a = a * a_s
        elif reverse:
            b = jnp.concatenate([a[:n - s] * b[s:] + b[:n - s], b[n - s:]], axis=0)
            a = jnp.concatenate([a[:n - s] * a[s:], a[n - s:]], axis=0)
        else:
            b = jnp.concatenate([b[:s], a[s:] * b[:n - s] + b[s:]], axis=0)
            a = jnp.concatenate([a[:s], a[s:] * a[:n - s]], axis=0)
        s *= 2
    return a, b


def lru_fwd(p, conv_w, conv_b, wr, br, wi, bi, lam, reverse, name, rider=None):
    t = p.shape[0]
    nt = t // TM

    def tile_of(s):
        return jnp.where(s == 0, 0, nt - s) if reverse else s

    def body(x_ref, xp_ref, xn_ref, cw_ref, cb_ref, wr_ref, br_ref, wi_ref, bi_ref, lam_ref, h_ref, carry):
        s = pl.program_id(0)
        tile = tile_of(s)

        @pl.when(s == 0)
        def _():
            carry[...] = jnp.zeros_like(carry)

        ext = _ext(tile, nt, x_ref[...], xp_ref[...], xn_ref[...])
        xc = _conv(ext, cw_ref, cb_ref)
        _, ig, _, a, mult = _gates(xc, wr_ref[...], br_ref[...], wi_ref[...], bi_ref[...], lam_ref[...])
        a_cum, hl = _scan(a, mult * (ig * xc), reverse)
        hh = hl + a_cum * carry[...]
        h_ref[...] = hh
        carry[...] = hh[0:1, :] if reverse else hh[TM - 1:TM, :]

    main, prev, nxt = _halo_specs(nt, tile_of)
    vec = pl.BlockSpec((1, LRU_W), lambda s: (0, 0))
    mat = pl.BlockSpec((LRU_W, LRU_W), lambda s: (0, 0))
    outs, got = _grid_call(
        body, name=name, nsteps=nt,
        in_specs=[main, prev, nxt, pl.BlockSpec((CONV_W, LRU_W), lambda s: (0, 0)), vec, mat, vec, mat, vec, vec],
        out_specs=[main],
        out_shape=[jax.ShapeDtypeStruct((t, LRU_W), F32)],
        scratch_shapes=[pltpu.VMEM((1, LRU_W), F32)],
        args=(p, p, p, conv_w, conv_b, wr, br, wi, bi, lam), rider=rider)
    return outs[0], got


def lru_bwd(p, hs, dhs, conv_w, conv_b, wr, br, wi, bi, lam, reverse, name):
    t = p.shape[0]
    nt = t // TM
    bpt = TM // HALO

    def tile_of(s):
        return jnp.where(s == nt - 1, 0, s + 1) if reverse else nt - 1 - s

    def hprev_block(s):
        tile = tile_of(s)
        if reverse:
            return (jnp.where(tile == nt - 1, 0, jnp.minimum((tile + 1) * bpt, nt * bpt - 1)), 0)
        return (jnp.maximum(tile * bpt - 1, 0), 0)

    def body(x_ref, xp_ref, xn_ref, h_ref, hp_ref, dh_ref, cw_ref, cb_ref, wr_ref, br_ref, wi_ref, bi_ref, lam_ref,
             dxc_ref, dwr_ref, dwi_ref, dbr_ref, dbi_ref, dlam_ref, carry):
        s = pl.program_id(0)
        tile = tile_of(s)

        @pl.when(s == 0)
        def _():
            carry[...] = jnp.zeros_like(carry)
            for ref in (dwr_ref, dwi_ref, dbr_ref, dbi_ref, dlam_ref):
                ref[...] = jnp.zeros_like(ref)

        ext = _ext(tile, nt, x_ref[...], xp_ref[...], xn_ref[...])
        xc = _conv(ext, cw_ref, cb_ref)
        wr_, wi_ = wr_ref[...], wi_ref[...]
        r, ig, sp, a, mult = _gates(xc, wr_, br_ref[...], wi_, bi_ref[...], lam_ref[...])
        gated = ig * xc
        row = lax.broadcasted_iota(jnp.int32, (TM, LRU_W), 0)
        hh = h_ref[...]
        start = jnp.where(tile != 0, hp_ref[0:1, :] if reverse else hp_ref[HALO - 1:HALO, :], 0.0)
        if reverse:
            edge = row == TM - 1
            hprev = jnp.where(edge, start, pltpu.roll(hh, TM - 1, 0))
            coef = jnp.where(row == 0, 0.0, pltpu.roll(a, 1, 0))
            bb = dh_ref[...] + jnp.where(row == 0, carry[...], 0.0)
        else:
            edge = row == 0
            hprev = jnp.where(edge, start, pltpu.roll(hh, 1, 0))
            coef = jnp.where(row == TM - 1, 0.0, pltpu.roll(a, TM - 1, 0))
            bb = dh_ref[...] + jnp.where(row == TM - 1, carry[...], 0.0)
        _, lmb = _scan(coef, bb, not reverse)
        al = a * lmb
        carry[...] = al[TM - 1:TM, :] if reverse else al[0:1, :]

        dgated = lmb * mult
        dloga = (lmb * hprev) * a - (lmb * gated) * (a * a) / mult
        dpre_r = (dloga * (-RG_C * sp)) * r * (1.0 - r)
        dpre_i = (dgated * xc) * ig * (1.0 - ig)
        drb, dib = dpre_r.astype(BF16), dpre_i.astype(BF16)
        xb = xc.astype(BF16)
        dxc_ref[...] = dgated * ig + _dot_nt(drb, wr_) + _dot_nt(dib, wi_)
        dwr_ref[...] += _dot_tn(xb, drb)
        dwi_ref[...] += _dot_tn(xb, dib)
        dbr_ref[...] += jnp.sum(dpre_r, axis=0, keepdims=True)
        dbi_ref[...] += jnp.sum(dpre_i, axis=0, keepdims=True)
        dlam_ref[...] += jnp.sum(dloga * (-RG_C * r), axis=0, keepdims=True)

        @pl.when(s == nt - 1)
        def _():
            dlam_ref[...] = dlam_ref[...] * (-_sigmoid(-lam_ref[...]))

    main, prev, nxt = _halo_specs(nt, tile_of)
    vec = pl.BlockSpec((1, LRU_W), lambda s: (0, 0))
    mat = pl.BlockSpec((LRU_W, LRU_W), lambda s: (0, 0))
    vshape = jax.ShapeDtypeStruct((1, LRU_W), F32)
    mshape = jax.ShapeDtypeStruct((LRU_W, LRU_W), F32)
    return pl.pallas_call(
        body, name=name, grid=(nt,),
        in_specs=[main, prev, nxt, main, pl.BlockSpec((HALO, LRU_W), hprev_block), main,
                  pl.BlockSpec((CONV_W, LRU_W), lambda s: (0, 0)), vec, mat, vec, mat, vec, vec],
        out_specs=[main, mat, mat, vec, vec, vec],
        out_shape=[jax.ShapeDtypeStruct((t, LRU_W), F32), mshape, mshape, vshape, vshape, vshape],
        scratch_shapes=[pltpu.VMEM((1, LRU_W), F32)],
        compiler_params=_cp(),
    )(p, p, p, hs, hs, dhs, conv_w, conv_b, wr, br, wi, bi, lam)


GELU_C = 0.7978845608028654
GELU_A = 0.044715


def _gelu(x):
    th = jnp.tanh(GELU_C * (x + GELU_A * x * x * x))
    return 0.5 * x * (1.0 + th), th


def _sgu(v, gain, w_ref, bt_ref):
    mu = jnp.mean(v, axis=-1, keepdims=True)
    xc = v - mu
    rs = lax.rsqrt(jnp.mean(xc * xc, axis=-1, keepdims=True) + EPS)
    vhat = xc * rs
    vnb = (vhat * gain).astype(BF16)
    chunks = []
    for ch in range(TM // CHUNK):
        zs = []
        for gi in range(GROUPS):
            vb = vnb[ch * CHUNK:(ch + 1) * CHUNK, gi * GROUP_DIM:(gi + 1) * GROUP_DIM]
            zs.append(_dot(w_ref[gi].astype(BF16), vb) + bt_ref[:, gi:gi + 1])
        chunks.append(jnp.concatenate(zs, axis=1))
    return jnp.concatenate(chunks, axis=0), vhat, rs, vnb


def _pcols(k):
    return pl.BlockSpec((TM, LRU_W), lambda i: (i, k))


def mix_out_fwd(h, p, hf, hb, mod, sgu_g, sgu_w, sgu_bt, womix, name):
    t = h.shape[0]

    def body(h_ref, gl_ref, u_ref, v_ref, hf_ref, hb_ref, mod_ref, sg_ref, sw_ref, sb_ref, w_hbm, out_ref, o_ref, w_v):
        i = pl.program_id(0)

        @pl.when(i == 0)
        def _():
            pltpu.sync_copy(w_hbm, w_v)

        ic = _kind(i)
        ge, _ = _gelu(gl_ref[...])
        y_lru = (hf_ref[...] + hb_ref[...]) * ge
        z, _, _, _ = _sgu(v_ref[...], sg_ref[...], sw_ref, sb_ref)
        y = jnp.concatenate([y_lru, u_ref[...] * z], axis=1).astype(BF16)
        o = _dot(y, w_v[...])
        o_ref[...] = o
        out_ref[...] = h_ref[...] + _sel(ic, mod_ref, 2) * o

    row = pl.BlockSpec((TM, D), lambda i: (i, 0))
    half = pl.BlockSpec((TM, LRU_W), lambda i: (i, 0))
    return pl.pallas_call(
        body, name=name, grid=(t // TM,),
        in_specs=[row, _pcols(1), _pcols(2), _pcols(3), half, half, pl.BlockSpec((2, 3, D), lambda i: (0, 0, 0)),
                  pl.BlockSpec((1, MLP_W), lambda i: (0, 0)), pl.BlockSpec((GROUPS, CHUNK, CHUNK), lambda i: (0, 0, 0)),
                  pl.BlockSpec((CHUNK, GROUPS), lambda i: (0, 0)), ANY],
        out_specs=[row, row],
        out_shape=[jax.ShapeDtypeStruct((t, D), F32), jax.ShapeDtypeStruct((t, D), F32)],
        scratch_shapes=[pltpu.VMEM((D, D), BF16)],
        compiler_params=_cp(),
    )(h, p, p, p, hf, hb, mod, sgu_g, sgu_w, sgu_bt, womix)


def mix_out_bwd(dy, p, hf, hb, o, mod, sgu_g, sgu_w, sgu_bt, womix, name, rider=None):
    t = dy.shape[0]

    def body(dy_ref, gl_ref, u_ref, v_ref, hf_ref, hb_ref, o_ref, mod_ref, sg_ref, sw_ref, sb_ref, w_hbm,
             dhs_ref, dp_ref, dw_ref, dgate_ref, dsg_ref, dsw_ref, dsb_ref, w_v):
        i = pl.program_id(0)

        @pl.when(i == 0)
        def _():
            pltpu.sync_copy(w_hbm, w_v)
            for ref in (dw_ref, dgate_ref, dsg_ref, dsw_ref, dsb_ref):
                ref[...] = jnp.zeros_like(ref)

        ic = _kind(i)
        dy_ = dy_ref[...]
        _acc2(dgate_ref, 0, dy_ * o_ref[...], ic)
        dob = (_sel(ic, mod_ref, 2) * dy_).astype(BF16)

        gl = gl_ref[...]
        ge, th = _gelu(gl)
        hsum = hf_ref[...] + hb_ref[...]
        gain = sg_ref[...]
        uu = u_ref[...]
        z, vhat, rs, vnb = _sgu(v_ref[...], gain, sw_ref, sb_ref)
        y = jnp.concatenate([hsum * ge, uu * z], axis=1).astype(BF16)
        dw_ref[...] += _dot_tn(y, dob)
        dyy = _dot_nt(dob, w_v[...])
        dyl, dys = dyy[:, :LRU_W], dyy[:, LRU_W:]

        dhs_ref[...] = dyl * ge
        dge = 0.5 * (1.0 + th) + 0.5 * gl * (1.0 - th * th) * (GELU_C * (1.0 + 3.0 * GELU_A * gl * gl))
        dp_ref[:, 0:LRU_W] = dyl * hsum * dge
        dp_ref[:, LRU_W:2 * LRU_W] = dys * z

        dz = dys * uu
        dzb = dz.astype(BF16)
        dvn_chunks, dsb_cols = [], [jnp.zeros((CHUNK, 1), F32)] * GROUPS
        for ch in range(TM // CHUNK):
            cols = []
            for gi in range(GROUPS):
                rs_, cs_ = slice(ch * CHUNK, (ch + 1) * CHUNK), slice(gi * GROUP_DIM, (gi + 1) * GROUP_DIM)
                dzg = dzb[rs_, cs_]
                dsb_cols[gi] = dsb_cols[gi] + jnp.sum(dz[rs_, cs_], axis=1, keepdims=True)
                dsw_ref[gi] += _dot_nt(dzg, vnb[rs_, cs_])
                cols.append(_dot_tn(sw_ref[gi].astype(BF16), dzg))
            dvn_chunks.append(jnp.concatenate(cols, axis=1))
        dsb_ref[...] += jnp.concatenate(dsb_cols, axis=1)
        dvn = jnp.concatenate(dvn_chunks, axis=0)
        dsg_ref[...] += jnp.sum(dvn * vhat, axis=0, keepdims=True)
        dvh = dvn * gain
        dp_ref[:, 2 * LRU_W:3 * LRU_W] = rs * (dvh - jnp.mean(dvh, axis=-1, keepdims=True)
                                               - vhat * jnp.mean(dvh * vhat, axis=-1, keepdims=True))

    row = pl.BlockSpec((TM, D), lambda i: (i, 0))
    half = pl.BlockSpec((TM, LRU_W), lambda i: (i, 0))
    const2 = lambda i: (0, 0)
    const3 = lambda i: (0, 0, 0)
    return _grid_call(
        body, name=name, nsteps=t // TM,
        in_specs=[row, _pcols(1), _pcols(2), _pcols(3), half, half, row, pl.BlockSpec((2, 3, D), const3),
                  pl.BlockSpec((1, MLP_W), const2), pl.BlockSpec((GROUPS, CHUNK, CHUNK), const3),
                  pl.BlockSpec((CHUNK, GROUPS), const2), ANY],
        out_specs=[half, pl.BlockSpec((TM, 3 * LRU_W), lambda i: (i, 0)), pl.BlockSpec((D, D), const2),
                   pl.BlockSpec((2, 1, D), const3), pl.BlockSpec((1, MLP_W), const2),
                   pl.BlockSpec((GROUPS, CHUNK, CHUNK), const3), pl.BlockSpec((CHUNK, GROUPS), const2)],
        out_shape=[jax.ShapeDtypeStruct((t, LRU_W), F32), jax.ShapeDtypeStruct((t, 3 * LRU_W), F32),
                   jax.ShapeDtypeStruct((D, D), F32), jax.ShapeDtypeStruct((2, 1, D), F32),
                   jax.ShapeDtypeStruct((1, MLP_W), F32), jax.ShapeDtypeStruct((GROUPS, CHUNK, CHUNK), F32),
                   jax.ShapeDtypeStruct((CHUNK, GROUPS), F32)],
        scratch_shapes=[pltpu.VMEM((D, D), BF16)],
        args=(dy, p, p, p, hf, hb, o, mod, sgu_g, sgu_w, sgu_bt, womix), rider=rider)


def mix_in_bwd(dy, h, p, dxf, dxb, dprest, mod, g, conv_w, wmix, name, rider=None):
    t = dy.shape[0]
    nt = t // TM

    def body(dy_ref, h_ref, x_ref, xp_ref, xn_ref, f_ref, fp_ref, fn_ref, b_ref, bp_ref, bn_ref, dpr_ref, mod_ref,
             g_ref, cw_ref, w_hbm, dh_ref, dw_hbm, dmod_ref, dg_ref, dcw_ref, dcb_ref, w_v, dw_v):
        i = pl.program_id(0)

        @pl.when(i == 0)
        def _():
            pltpu.sync_copy(w_hbm, w_v)
            dw_v[...] = jnp.zeros_like(dw_v)
            for ref in (dmod_ref, dg_ref, dcw_ref, dcb_ref):
                ref[...] = jnp.zeros_like(ref)

        dmain = f_ref[...] + b_ref[...]
        dext = _ext(i, nt, dmain, fp_ref[...] + bp_ref[...], fn_ref[...] + bn_ref[...])
        xext = _ext(i, nt, x_ref[...], xp_ref[...], xn_ref[...])
        dxl = cw_ref[0:1, :] * _shifted(dext, 2)
        for k in range(1, CONV_W):
            dxl = dxl + cw_ref[k:k + 1, :] * _shifted(dext, 2 - k)
        dcw_ref[...] += jnp.concatenate(
            [jnp.sum(dmain * _shifted(xext, k - 2), axis=0, keepdims=True) for k in range(CONV_W)], axis=0)
        dcb_ref[...] += jnp.sum(dmain, axis=0, keepdims=True)

        ic = _kind(i)
        gain = g_ref[...]
        scale = _sel(ic, mod_ref, 1)
        z, n, r = _norm_mod(h_ref[...], gain, _sel(ic, mod_ref, 0), scale)
        zb = z.astype(BF16)
        dpb = jnp.concatenate([dxl, dpr_ref[...]], axis=1).astype(BF16)
        dz = jnp.zeros((TM, D), F32)
        for dd in range(N_DEV):
            dpd = dpb[:, dd * N_MIX_SHARD:(dd + 1) * N_MIX_SHARD]
            dz = dz + _dot_nt(dpd, w_v[dd])
            dw_v[dd] += _dot_tn(zb, dpd)
        _acc2(dmod_ref, 0, dz, ic)
        _acc2(dmod_ref, 1, dz * (n * gain), ic)
        dg_ref[...] += jnp.sum(dz * (1.0 + scale) * n, axis=0, keepdims=True)
        dh_ref[...] = dy_ref[...] + _norm_mod_bwd(dz, n, r, gain, scale)

        @pl.when(i == nt - 1)
        def _():
            pltpu.sync_copy(dw_v, dw_hbm)

    main, prev, nxt = _halo_specs(nt, lambda s: s)
    row = pl.BlockSpec((TM, D), lambda i: (i, 0))
    const2 = lambda i: (0, 0)
    return _grid_call(
        body, name=name, nsteps=nt,
        in_specs=[row, row, main, prev, nxt, main, prev, nxt, main, prev, nxt,
                  pl.BlockSpec((TM, 3 * LRU_W), lambda i: (i, 0)), pl.BlockSpec((2, 3, D), lambda i: (0, 0, 0)),
                  pl.BlockSpec((1, D), const2), pl.BlockSpec((CONV_W, LRU_W), const2), ANY],
        out_specs=[row, ANY, pl.BlockSpec((2, 2, D), lambda i: (0, 0, 0)), pl.BlockSpec((1, D), const2),
                   pl.BlockSpec((CONV_W, LRU_W), const2), pl.BlockSpec((1, LRU_W), const2)],
        out_shape=[jax.ShapeDtypeStruct((t, D), F32), jax.ShapeDtypeStruct((N_DEV, D, N_MIX_SHARD), F32),
                   jax.ShapeDtypeStruct((2, 2, D), F32), jax.ShapeDtypeStruct((1, D), F32),
                   jax.ShapeDtypeStruct((CONV_W, LRU_W), F32), jax.ShapeDtypeStruct((1, LRU_W), F32)],
        scratch_shapes=[pltpu.VMEM((N_DEV, D, N_MIX_SHARD), BF16), pltpu.VMEM((N_DEV, D, N_MIX_SHARD), F32)],
        args=(dy, h, p, p, p, dxf, dxf, dxf, dxb, dxb, dxb, dprest, mod, g, conv_w, wmix), rider=rider)


def loss_head(h, g, target, name):
    t = h.shape[0]
    nc = CTX // TM

    def body(h_ref, g_ref, t_ref, dh_ref, loss_ref, dg_ref):
        i = pl.program_id(0)

        @pl.when(i == 0)
        def _():
            loss_ref[...] = jnp.zeros_like(loss_ref)
            dg_ref[...] = jnp.zeros_like(dg_ref)

        @pl.when(i < nc)
        def _():
            dh_ref[...] = jnp.zeros_like(dh_ref)

        @pl.when(i >= nc)
        def _():
            hh = h_ref[...]
            gain = g_ref[...]
            r = lax.rsqrt(jnp.mean(hh * hh, axis=-1, keepdims=True) + EPS)
            n = hh * r
            err = n * gain - t_ref[...]
            loss_ref[...] += 0.5 * jnp.sum(jnp.mean(err * err, axis=-1, keepdims=True))
            dy = err * (1.0 / D)
            dg_ref[...] += jnp.sum(dy * n, axis=0, keepdims=True)
            dn = dy * gain
            dh_ref[...] = r * (dn - n * jnp.mean(dn * n, axis=-1, keepdims=True))

    row = pl.BlockSpec((TM, D), lambda i: (i, 0))
    return pl.pallas_call(
        body, name=name, grid=(t // TM,),
        in_specs=[row, pl.BlockSpec((1, D), lambda i: (0, 0)),
                  pl.BlockSpec((TM, D), lambda i: (jnp.maximum(i - nc, 0), 0))],
        out_specs=[row, pl.BlockSpec((8, 128), lambda i: (0, 0)), pl.BlockSpec((1, D), lambda i: (0, 0))],
        out_shape=[jax.ShapeDtypeStruct((t, D), F32), jax.ShapeDtypeStruct((8, 128), F32),
                   jax.ShapeDtypeStruct((1, D), F32)],
        compiler_params=_cp(),
    )(h, g, target)


def pad_in_shard(w):
    z = jnp.zeros(w.shape[:-1] + (HALF_PAD - HALF,), w.dtype)
    return jnp.concatenate([w[..., :HALF], z, w[..., HALF:], z], axis=-1)


def unpad_in_shard(g):
    return jnp.concatenate([g[..., :HALF], g[..., HALF_PAD:HALF_PAD + HALF]], axis=-1)


def pad_out_shard(w):
    return jnp.concatenate([w, jnp.zeros(w.shape[:-2] + (HALF_PAD - HALF, w.shape[-1]), w.dtype)], axis=-2)


def unpad_out_shard(g):
    return g[..., :HALF, :]


def _block_diag(w):
    eye = jnp.eye(HEADS, dtype=w.dtype)
    return jnp.einsum("dhij,hk->dhikj", w, eye).reshape(2, LRU_W, LRU_W)


def _block_diag_inv(full):
    f = full.reshape(2, HEADS, HEAD_DIM, HEADS, HEAD_DIM)
    return jnp.stack([f[:, hd, :, hd, :] for hd in range(HEADS)], axis=1)


def small_layer(g1, gm, g2, conv_w, conv_b, w_r, b_r, w_i, b_i, lam, sgu_g, sgu_w, sgu_b):
    return dict(g1=g1[None, :], gm=gm[None, :], g2=g2[None, :], conv_w=conv_w, conv_b=conv_b[None, :],
                wr=_block_diag(w_r).astype(BF16), br=b_r[:, None, :], wi=_block_diag(w_i).astype(BF16),
                bi=b_i[:, None, :], lam=lam[:, None, :], sgu_g=sgu_g[None, :], sgu_w=sgu_w, sgu_bt=sgu_b.T)


def small_grads(g):
    return dict(ffn1_norm_g=g["g1"][0], mix_norm_g=g["gm"][0], ffn2_norm_g=g["g2"][0], lru_conv_w=g["conv_w"],
                lru_conv_b=g["conv_b"][0], lru_w_r=_block_diag_inv(g["wr"]), lru_b_r=g["br"][:, 0, :],
                lru_w_i=_block_diag_inv(g["wi"]), lru_b_i=g["bi"][:, 0, :], lru_lambda=g["lam"][:, 0, :],
                sgu_norm_g=g["sgu_g"][0], sgu_w=g["sgu_w"], sgu_b=g["sgu_bt"].T)


BIG_KEYS = ("win1", "wout1", "wmix", "womix", "win2", "wout2")


def _as_blocks(key, g):
    if key in ("wout1", "wout2"):
        return g.reshape(N_DEV, HALF_PAD, D)
    return g.reshape(N_DEV, OMIX_SHARD, D) if key == "womix" else g


def _gathered(key, a):
    if key in ("wout1", "wout2"):
        return a.reshape(HID_PAD, D)
    return a.reshape(D, D) if key == "womix" else a


class _ReduceScatter:
    def __init__(self, c_idx, where):
        self.c_idx, self.where = c_idx, where
        self.out = {}

    def pair(self, group):
        return pair_rider([g for _, g in group])

    def after_pair(self, group, recv1, tag):
        parts = [pair_sum(g, r, self.c_idx, f"pair_sum_{tag}_{i}") for i, ((_, g), r) in enumerate(zip(group, recv1))]
        return chips_rider(parts)

    def after_chips(self, group, recv1, recv2, tag):
        for i, ((key, g), r1, r2) in enumerate(zip(group, recv1, recv2)):
            self.out[key] = final_sum(g, r1, r2, self.where, f"final_sum_{tag}_{i}")


def fwd_bwd(ctx_rows, x_rows, target, mods, shards, smalls, final_g, c_idx, where):
    assert CTX == TM and len(shards) == 2

    def gather(keys_by_layer):
        return GatherRider([shards[l][k] for l, k in keys_by_layer])

    def put(full, keys_by_layer, got):
        for (l, k), a in zip(keys_by_layer, got):
            full[l][k] = _gathered(k, a)

    full = [dict(s) for s in smalls]
    first = [(0, "win1"), (0, "wout1")]
    put(full, first, run_alone(gather(first), pl.ANY, "gather_w0"))
    riders = {
        "ffn1_fwd_0": [(0, "wmix"), (0, "womix"), (0, "win2")],
        "lru_fwd_0_0": [(0, "wout2")],
        "ffn2_fwd_0": [(1, "win1"), (1, "wout1")],
        "ffn1_fwd_1": [(1, "wmix"), (1, "womix"), (1, "win2")],
        "lru_fwd_1_0": [(1, "wout2")],
    }

    def ffn(which, l, h):
        name = f"ffn{which}_fwd_{l}"
        w = full[l]
        keys = riders.get(name)
        m = mods[l][:, 0:3] if which == 1 else mods[l][:, 6:9]
        outs, got = ffn_fwd(h, m, w[f"g{which}"], w[f"win{which}"], w[f"wout{which}"], name,
                            rider=gather(keys) if keys else None)
        if keys:
            put(full, keys, got)
        return outs

    saved = []
    h = (ctx_rows, x_rows)
    for l in range(2):
        mm = mods[l][:, 3:6]
        outs = ffn(1, l, h)
        h1, gu1, acc1 = outs[:3]
        hin = outs[3] if l == 0 else h
        w = full[l]
        p = mix_in_fwd(h1, mm, w["gm"], w["wmix"], f"mix_in_fwd_{l}")
        hs = []
        for d in range(2):
            keys = riders.get(f"lru_fwd_{l}_{d}")
            hd, got = lru_fwd(p, w["conv_w"], w["conv_b"], w["wr"][d], w["br"][d], w["wi"][d], w["bi"][d], w["lam"][d],
                              bool(d), f"lru_fwd_{l}_{d}", rider=gather(keys) if keys else None)
            if keys:
                put(full, keys, got)
            hs.append(hd)
        h2, o = mix_out_fwd(h1, p, hs[0], hs[1], mm, w["sgu_g"], w["sgu_w"], w["sgu_bt"], w["womix"], f"mix_out_fwd_{l}")
        h3, gu2, acc2 = ffn(2, l, h2)
        saved.append((hin, h1, h2, gu1, acc1, p, hs, o, gu2, acc2))
        h = h3
    dh, loss, dgf = loss_head(h, final_g, target, "loss_head")

    rs = _ReduceScatter(c_idx, where)
    grads, dmods, sums = [None, None], [None, None], [None, None]
    pending = None
    for l in (1, 0):
        w = full[l]
        m1, mm, m2 = mods[l][:, 0:3], mods[l][:, 3:6], mods[l][:, 6:9]
        hin, h1, h2, gu1, acc1, p, hs, o, gu2, acc2 = saved[l]
        g = {}
        rs.out = {}
        both = Riders([rs.pair(pending[0]), GatherRider([small_pack])]) if pending else None
        (dp2, g["wout2"], dgate2), got = ffn_bwd_a(dh, acc2, gu2, m2, w["wout2"], f"ffn2_bwd_a_{l}", rider=both)
        if pending:
            r1, (small_all,) = both.split(got)
        chips = rs.after_pair(pending[0], r1, pending[1]) if pending else None
        (dh, g["win2"], dmod2, g["g2"]), r2 = ffn_bwd_b(dh, h2, dp2, m2, w["g2"], w["win2"], f"ffn2_bwd_b_{l}", rider=chips)
        if pending:
            rs.after_chips(pending[0], r1, r2, pending[1])
            sums[l + 1].update(rs.out)
            rs.out = {}

        grp = [(k, _as_blocks(k, g[k])) for k in ("win2", "wout2")]
        (dhs, dprest, g["womix"], dgatem, g["sgu_g"], g["sgu_w"], g["sgu_bt"]), r1 = mix_out_bwd(
            dh, p, hs[0], hs[1], o, mm, w["sgu_g"], w["sgu_w"], w["sgu_bt"], w["womix"], f"mix_out_bwd_{l}",
            rider=rs.pair(grp))
        chips = rs.after_pair(grp, r1, f"a{l}")
        dx, per_dir = [], []
        for d in range(2):
            out = lru_bwd(p, hs[d], dhs, w["conv_w"], w["conv_b"], w["wr"][d], w["br"][d], w["wi"][d], w["bi"][d],
                          w["lam"][d], bool(d), f"lru_bwd_{l}_{d}")
            dx.append(out[0])
            per_dir.append(out[1:])
        for k, nm in enumerate(("wr", "wi", "br", "bi", "lam")):
            g[nm] = jnp.stack([per_dir[0][k], per_dir[1][k]])
        (dh, g["wmix"], dmodm, g["gm"], g["conv_w"], g["conv_b"]), r2 = mix_in_bwd(
            dh, h1, p, dx[0], dx[1], dprest, mm, w["gm"], w["conv_w"], w["wmix"], f"mix_in_bwd_{l}", rider=chips)
        rs.after_chips(grp, r1, r2, f"a{l}")
        sums[l] = dict(rs.out)
        rs.out = {}

        if l == 1:
            (dp1, g["wout1"], dgate1), _ = ffn_bwd_a(dh, acc1, gu1, m1, w["wout1"], f"ffn1_bwd_a_{l}")
            (dh, g["win1"], dmod1, g["g1"]), _ = ffn_bwd_b(dh, hin, dp1, m1, w["g1"], w["win1"], f"ffn1_bwd_b_{l}")
            pending = ([(k, _as_blocks(k, g[k])) for k in ("womix", "wmix", "wout1", "win1")], f"b{l}")
            per = small_grads(g)
            small_pack = _pack([per[n] for n in LAYER_SMALL])
        else:
            g_mix = [(k, _as_blocks(k, g[k])) for k in ("womix", "wmix")]
            (dp1, g["wout1"], dgate1), r1_mix = ffn_bwd_a(dh, acc1, gu1, m1, w["wout1"], f"ffn1_bwd_a_{l}",
                                                          rider=rs.pair(g_mix))
            g_out = [("wout1", _as_blocks("wout1", g["wout1"]))]
            both = Riders([rs.after_pair(g_mix, r1_mix, f"b{l}"), rs.pair(g_out)])
            (g["win1"],), got = ffn_bwd_dw(hin, dp1, m1, w["g1"], f"ffn1_bwd_dw_{l}", rider=both)
            r2_mix, r1_out = both.split(got)
            rs.after_chips(g_mix, r1_mix, r2_mix, f"b{l}")
            g_in = [("win1", g["win1"])]
            both = Riders([rs.after_pair(g_out, r1_out, f"c{l}"), rs.pair(g_in)])
            nt = dh.shape[0] // TM
            cut = (3 * nt) // 8
            part, got = ffn_bwd_dh(dh, hin, dp1, m1, w["g1"], w["win1"], f"ffn1_bwd_dh0_{l}", (0, cut), rider=both)
            r2_out, r1_in = both.split(got)
            rs.after_chips(g_out, r1_out, r2_out, f"c{l}")
            (dh, dmod1, g["g1"]), r2_in = ffn_bwd_dh(dh, hin, dp1, m1, w["g1"], w["win1"], f"ffn1_bwd_dh1_{l}", (cut, nt),
                                                    carry=part, rider=rs.after_pair(g_in, r1_in, f"d{l}"))
            rs.after_chips(g_in, r1_in, r2_in, f"d{l}")
            sums[l].update(rs.out)
        dmods[l] = jnp.concatenate([dmod1, dgate1, dmodm, dgatem, dmod2, dgate2], axis=1)
        grads[l] = g
    return loss, dh, jnp.stack(dmods), grads, small_all, sums, dgf


def _row_block(r, c, limit=262144):
    best = 8
    for rb in range(8, r + 1, 8):
        if r % rb == 0 and rb * c <= limit:
            best = rb
    return best


def pair_sum(grads, recv, c_idx, name):
    _, r, c = grads.shape
    rb = _row_block(r, c)

    def body(c_ref, g_ref, r_ref, o_ref):
        o_ref[...] = (g_ref[...] + r_ref[...]).astype(BF16)

    return pl.pallas_call(
        body, name=name,
        grid_spec=pltpu.PrefetchScalarGridSpec(
            num_scalar_prefetch=1, grid=(4, r // rb),
            in_specs=[pl.BlockSpec((1, rb, c), lambda j, i, c_ref: (2 * j + c_ref[0], i, 0)),
                      pl.BlockSpec((1, rb, c), lambda j, i, c_ref: (j, i, 0))],
            out_specs=pl.BlockSpec((1, rb, c), lambda j, i, c_ref: (j, i, 0))),
        out_shape=jax.ShapeDtypeStruct((4, r, c), BF16),
        compiler_params=_cp(2),
    )(c_idx, grads, recv)


def final_sum(grads, recv1, recv2, where, name):
    _, r, c = grads.shape
    rb = _row_block(r, c)

    def body(w_ref, g_ref, r1_ref, r2_ref, o_ref):
        far = (r2_ref[0].astype(F32) + r2_ref[1].astype(F32)) + r2_ref[2].astype(F32)
        o_ref[...] = (g_ref[0] + r1_ref[0]) + far

    return pl.pallas_call(
        body, name=name,
        grid_spec=pltpu.PrefetchScalarGridSpec(
            num_scalar_prefetch=1, grid=(r // rb,),
            in_specs=[pl.BlockSpec((1, rb, c), lambda i, w_ref: (w_ref[0], i, 0)),
                      pl.BlockSpec((1, rb, c), lambda i, w_ref: (w_ref[1], i, 0)),
                      pl.BlockSpec((3, rb, c), lambda i, w_ref: (0, i, 0))],
            out_specs=pl.BlockSpec((rb, c), lambda i, w_ref: (i, 0))),
        out_shape=jax.ShapeDtypeStruct((r, c), F32),
        compiler_params=_cp(1),
    )(where, grads, recv1, recv2)


ADA_ROWS = 16


def _silu(v):
    return v * _sigmoid(v)


def ada_fwd(cond, w_ada, b_slab, name):
    def body(c_ref, w_ref, b_ref, o_ref):
        s = _silu(c_ref[...]).astype(BF16)
        o_ref[0] = _dot(s, w_ref[0].astype(BF16)) + b_ref[0]

    return pl.pallas_call(
        body, name=name, grid=(DEPTH,),
        in_specs=[pl.BlockSpec((ADA_ROWS, D), lambda l: (0, 0)), pl.BlockSpec((1, D, ADA_SHARD), lambda l: (l, 0, 0)),
                  pl.BlockSpec((1, 1, ADA_SHARD), lambda l: (l, 0, 0))],
        out_specs=pl.BlockSpec((1, ADA_ROWS, ADA_SHARD), lambda l: (l, 0, 0)),
        out_shape=jax.ShapeDtypeStruct((DEPTH, ADA_ROWS, ADA_SHARD), F32),
        compiler_params=_cp(),
    )(cond, w_ada, b_slab)


def ada_bwd(cond, dm_sample, dm_ctx, w_ada, name):
    def body(c_ref, ds_ref, dc_ref, w_ref, gw_ref, dsc_ref):
        @pl.when(pl.program_id(0) == 0)
        def _():
            dsc_ref[...] = jnp.zeros_like(dsc_ref)

        s = _silu(c_ref[...]).astype(BF16)
        dcs = dc_ref[0]
        tot = dcs[0:1]
        for j in range(1, N_DEV):
            tot = tot + dcs[j:j + 1]
        tot8 = jnp.where(lax.broadcasted_iota(jnp.int32, (N_DEV, ADA_SHARD), 0) == 0, tot, 0.0)
        dm = jnp.concatenate([ds_ref[0], tot8], axis=0).astype(BF16)
        gw_ref[0] = _dot_tn(s, dm)
        dsc_ref[...] += _dot_nt(dm, w_ref[0].astype(BF16))[N_DEV:N_DEV + 1]

    slab = pl.BlockSpec((1, N_DEV, ADA_SHARD), lambda l: (l, 0, 0))
    wspec = pl.BlockSpec((1, D, ADA_SHARD), lambda l: (l, 0, 0))
    return pl.pallas_call(
        body, name=name, grid=(DEPTH,),
        in_specs=[pl.BlockSpec((ADA_ROWS, D), lambda l: (0, 0)), slab, slab, wspec],
        out_specs=[wspec, pl.BlockSpec((1, D), lambda l: (0, 0))],
        out_shape=[jax.ShapeDtypeStruct((DEPTH, D, ADA_SHARD), F32), jax.ShapeDtypeStruct((1, D), F32)],
        compiler_params=_cp(),
    )(cond, dm_sample, dm_ctx, w_ada)


def sum_over_devices(parts, name, silu_rows=0, w=None):
    _, r, c = parts.shape

    def body(*refs):
        p_ref, o_ref = refs[0], refs[-1]
        tot = p_ref[0]
        for j in range(1, N_DEV):
            tot = tot + p_ref[j]
        o_ref[...] = tot
        if silu_rows:
            wv = refs[1][...]
            s = _sigmoid(wv)
            o_ref[0:silu_rows, :] = tot[0:silu_rows, :] * (s * (1.0 + wv * (1.0 - s)))

    vm = pl.BlockSpec(memory_space=pltpu.VMEM)
    args = (parts,) if w is None else (parts, w)
    return pl.pallas_call(
        body, name=name, in_specs=[vm] * len(args), out_specs=vm,
        out_shape=jax.ShapeDtypeStruct((r, c), F32),
        compiler_params=pltpu.CompilerParams(vmem_limit_bytes=VMEM_LIMIT),
    )(*args)


def sum_dmods(dm_all, name):
    def body(d_ref, o_ref):
        for l in range(DEPTH):
            tot = d_ref[0, l]
            for j in range(1, N_DEV):
                tot = tot + d_ref[j, l]
            o_ref[l:l + 1, :] = tot[0:1] + tot[1:2]

    vm = pl.BlockSpec(memory_space=pltpu.VMEM)
    return pl.pallas_call(
        body, name=name, in_specs=[vm], out_specs=vm,
        out_shape=jax.ShapeDtypeStruct((DEPTH, N_MOD * D), F32),
    )(dm_all)


def adamw(w, g, m, v, name, rider=None):
    r, c = w.shape
    rb = _row_block(r, c, limit=131072)

    def body(w_ref, g_ref, m_ref, v_ref, d_ref, nm_ref, nv_ref):
        g_ = g_ref[...]
        nm = B1 * m_ref[...] + (1.0 - B1) * g_
        nv = B2 * v_ref[...] + (1.0 - B2) * (g_ * g_)
        nm_ref[...] = nm
        nv_ref[...] = nv
        m_hat = nm / (1.0 - B1 ** STEP)
        v_hat = nv / (1.0 - B2 ** STEP)
        d_ref[...] = -LR * (m_hat / (jnp.sqrt(v_hat) + ADAM_EPS) + WD * w_ref[...])

    blk = pl.BlockSpec((rb, c), lambda i: (i, 0))
    shp = jax.ShapeDtypeStruct((r, c), F32)
    return _grid_call(body, name=name, nsteps=r // rb, in_specs=[blk] * 4, out_specs=[blk] * 3, out_shape=[shp] * 3,
                      scratch_shapes=[], args=(w, g, m, v), rider=rider)


def _adamw_nd(w, g, m, v, name, rider=None):
    shape = w.shape
    flat = lambda a: a.reshape(-1, shape[-1])
    outs, got = adamw(flat(w), flat(g), flat(m), flat(v), name, rider=rider)
    return tuple(o.reshape(shape) for o in outs), got


LANES = 128


PACK_UNIT = 8 * LANES


ADAMW_SMALL_ROWS = 512


def _pack(arrays, row_multiple=8):
    pieces, n = [], 0
    for a in arrays:
        pieces.append(a.reshape(-1).astype(F32))
        pad = (-a.size) % PACK_UNIT
        if pad:
            pieces.append(jnp.zeros((pad,), F32))
        n += a.size + pad
    tail = (-n) % (row_multiple * LANES)
    if tail:
        pieces.append(jnp.zeros((tail,), F32))
    return jnp.concatenate(pieces).reshape(-1, LANES)


def _unpack(packed, shapes):
    out, r0 = [], 0
    lead = packed.shape[:-2]
    for shp in shapes:
        size = 1
        for s in shp:
            size *= s
        nr = 8 * -(-size // PACK_UNIT)
        blk = packed[..., r0:r0 + nr, :].reshape(lead + (nr * LANES,))[..., :size]
        out.append(blk.reshape(lead + tuple(shp)))
        r0 += nr
    return out


WEIGHTS = ["c_ctx", "w_ada", "b_ada", "ffn1_norm_g", "ffn1_w_in", "ffn1_w_out", "mix_norm_g", "w_in_mix", "lru_conv_w",
           "lru_conv_b", "lru_w_r", "lru_b_r", "lru_w_i", "lru_b_i", "lru_lambda", "sgu_norm_g", "sgu_w", "sgu_b",
           "w_out_mix", "ffn2_norm_g", "ffn2_w_in", "ffn2_w_out", "final_norm_g"]
BIG = ["w_ada", "ffn1_w_in", "ffn1_w_out", "w_in_mix", "w_out_mix", "ffn2_w_in", "ffn2_w_out"]
SHARDED_SMALL = ["lru_conv_w", "lru_b_r", "lru_b_i", "lru_lambda"]
LAYER_SMALL = ["ffn1_norm_g", "mix_norm_g", "ffn2_norm_g", "lru_conv_w", "lru_conv_b", "lru_w_r", "lru_b_r", "lru_w_i",
               "lru_b_i", "lru_lambda", "sgu_norm_g", "sgu_w", "sgu_b"]
LRU_SHARD = LRU_W // N_DEV


def _widen(a):
    return jnp.moveaxis(a, 0, -2).reshape(a.shape[1:-1] + (LRU_W,))


def kernel(x, c, ctx, c_ctx, w_ada, b_ada, ffn1_norm_g, ffn1_w_in, ffn1_w_out, mix_norm_g, w_in_mix, lru_conv_w, lru_conv_b, lru_w_r, lru_b_r, lru_w_i, lru_b_i, lru_lambda, sgu_norm_g, sgu_w, sgu_b, w_out_mix, ffn2_norm_g, ffn2_w_in, ffn2_w_out, final_norm_g, loss_target, m_c_ctx, m_w_ada, m_b_ada, m_ffn1_norm_g, m_ffn1_w_in, m_ffn1_w_out, m_mix_norm_g, m_w_in_mix, m_lru_conv_w, m_lru_conv_b, m_lru_w_r, m_lru_b_r, m_lru_w_i, m_lru_b_i, m_lru_lambda, m_sgu_norm_g, m_sgu_w, m_sgu_b, m_w_out_mix, m_ffn2_norm_g, m_ffn2_w_in, m_ffn2_w_out, m_final_norm_g, v_c_ctx, v_w_ada, v_b_ada, v_ffn1_norm_g, v_ffn1_w_in, v_ffn1_w_out, v_mix_norm_g, v_w_in_mix, v_lru_conv_w, v_lru_conv_b, v_lru_w_r, v_lru_b_r, v_lru_w_i, v_lru_b_i, v_lru_lambda, v_sgu_norm_g, v_sgu_w, v_sgu_b, v_w_out_mix, v_ffn2_norm_g, v_ffn2_w_in, v_ffn2_w_out, v_final_norm_g):
    given = dict(locals())
    W = {n: given[n] for n in WEIGHTS}
    M = {n: given["m_" + n] for n in WEIGHTS}
    V = {n: given["v_" + n] for n in WEIGHTS}
    xi, yi, ci = _position()
    me = 4 * xi + 2 * yi + ci
    chip = 2 * xi + yi

    sharded_shapes = [W[n].shape for n in SHARDED_SMALL]
    got = run_alone(GatherRider([_pack([c[0]] + [W[n] for n in SHARDED_SMALL])]), pltpu.VMEM, "gather_small")[0]
    parts = _unpack(got, [(D,)] + sharded_shapes)
    c_all = parts[0]
    wide = {n: _widen(a) for n, a in zip(SHARDED_SMALL, parts[1:])}
    cond = jnp.concatenate([c_all, c_ctx[None, :], jnp.zeros((ADA_ROWS - N_DEV - 1, D), F32)], axis=0)
    b_slab = lax.dynamic_slice_in_dim(b_ada, me * ADA_SHARD, ADA_SHARD, axis=1)[:, None, :]
    slabs = ada_fwd(cond, w_ada, b_slab, "ada_fwd")
    mall = run_alone(GatherRider([slabs.reshape(DEPTH * ADA_ROWS, ADA_SHARD)]), pltpu.VMEM, "gather_mod")[0]
    mall = mall.reshape(N_DEV, DEPTH, ADA_ROWS, ADA_SHARD)
    m_sample = lax.dynamic_index_in_dim(mall, me, axis=2, keepdims=False)
    m_ctx = mall[:, :, N_DEV, :]
    mods = jnp.stack([jnp.transpose(m, (1, 0, 2)).reshape(DEPTH, N_MOD, D) for m in (m_ctx, m_sample)], axis=1)

    shards, smalls = [], []
    for l in range(DEPTH):
        sh = dict(win1=pad_in_shard(ffn1_w_in[l]), wout1=pad_out_shard(ffn1_w_out[l]), wmix=w_in_mix[l],
                  womix=w_out_mix[l], win2=pad_in_shard(ffn2_w_in[l]), wout2=pad_out_shard(ffn2_w_out[l]))
        shards.append({k: a.astype(BF16) for k, a in sh.items()})
        smalls.append(small_layer(ffn1_norm_g[l], mix_norm_g[l], ffn2_norm_g[l], wide["lru_conv_w"][l], lru_conv_b[l],
                                  lru_w_r[l], wide["lru_b_r"][l], lru_w_i[l], wide["lru_b_i"][l], wide["lru_lambda"][l],
                                  sgu_norm_g[l], sgu_w[l], sgu_b[l]))

    c_idx = ci.reshape(1).astype(jnp.int32)
    where = jnp.stack([me, chip]).astype(jnp.int32)
    loss_blk, dx, dmods, grads, small1_all, gsum, dgf = fwd_bwd(ctx[0], x[0], loss_target[0], mods, shards, smalls,
                                                                final_norm_g[None, :], c_idx, where)
    G = {
        "ffn1_w_in": jnp.stack([unpad_in_shard(gsum[l]["win1"]) for l in range(DEPTH)]),
        "ffn1_w_out": jnp.stack([unpad_out_shard(gsum[l]["wout1"]) for l in range(DEPTH)]),
        "w_in_mix": jnp.stack([gsum[l]["wmix"] for l in range(DEPTH)]),
        "w_out_mix": jnp.stack([gsum[l]["womix"] for l in range(DEPTH)]),
        "ffn2_w_in": jnp.stack([unpad_in_shard(gsum[l]["win2"]) for l in range(DEPTH)]),
        "ffn2_w_out": jnp.stack([unpad_out_shard(gsum[l]["wout2"]) for l in range(DEPTH)]),
    }

    delta, new_m, new_v = {}, {}, {}

    def update(n, rider=None):
        (delta[n], new_m[n], new_v[n]), got = _adamw_nd(W[n], G[n], M[n], V[n], f"adamw_{n}", rider=rider)
        return got

    n_rows = DEPTH * 2 * N_MOD
    dm_rows = jnp.concatenate([dmods.reshape(n_rows, D), jnp.zeros((-n_rows % 8, D), F32)], axis=0)
    per0 = small_grads(grads[0])
    shapes0 = [per0[n].shape for n in LAYER_SMALL]
    dm_all, small0_all = update("ffn2_w_in", Riders([GatherRider([dm_rows]), GatherRider([_pack([per0[n] for n in LAYER_SMALL])])]))
    dm_all = dm_all[:, :n_rows].reshape(N_DEV, DEPTH, 2, N_MOD * D)
    mine = lax.dynamic_slice_in_dim(dm_all, me * ADA_SHARD, ADA_SHARD, axis=3)
    G["w_ada"], dsc = ada_bwd(cond, jnp.transpose(mine[:, :, 1, :], (1, 0, 2)), jnp.transpose(mine[:, :, 0, :], (1, 0, 2)),
                              w_ada, "ada_bwd")
    G["b_ada"] = sum_dmods(dm_all, "sum_dmods")

    (head_all,) = update("w_ada", GatherRider([_pack([dsc[0], dgf[0]])]))
    head = _unpack(sum_over_devices(head_all, "sum_head_grads", silu_rows=D // LANES, w=c_ctx.reshape(D // LANES, LANES)),
                   [(D,), (D,)])
    sum0 = _unpack(sum_over_devices(small0_all, "sum_small_grads_0"), shapes0)
    sum1 = _unpack(sum_over_devices(small1_all, "sum_small_grads_1"), shapes0)
    G["c_ctx"], G["final_norm_g"] = head
    for n, a0, a1 in zip(LAYER_SMALL, sum0, sum1):
        a = jnp.stack([a0, a1])
        G[n] = lax.dynamic_slice_in_dim(a, me * LRU_SHARD, LRU_SHARD, axis=a.ndim - 1) if n in SHARDED_SMALL else a

    for n in BIG:
        if n not in delta:
            update(n)
    rest = [n for n in WEIGHTS if n not in BIG]
    shapes = [W[n].shape for n in rest]
    outs, _ = adamw(*[_pack([src[n] for n in rest], row_multiple=ADAMW_SMALL_ROWS) for src in (W, G, M, V)], "adamw_small")
    for dst, packed in zip((delta, new_m, new_v), outs):
        for n, a in zip(rest, _unpack(packed, shapes)):
            dst[n] = a

    loss = lax.psum(loss_blk[0, 0], ("x", "y", "c"))
    grad_x = dx[None]
    return (loss, grad_x, *[G[n] for n in WEIGHTS], *[delta[n] for n in WEIGHTS], *[new_m[n] for n in WEIGHTS],
            *[new_v[n] for n in WEIGHTS])
```

```python
import functools

import jax
import jax.numpy as jnp
from jax import lax
from jax.experimental import pallas as pl
from jax.experimental.pallas import tpu as pltpu

F32 = jnp.float32
BF16 = jnp.bfloat16

D = 1024
CTX = 256
DEPTH = 2
EPS = 1e-6
D_FF = 2816
LRU_W = 512
HEADS = 8
HEAD_DIM = 64
CONV_W = 4
RG_C = 8.0
GROUPS = 4
GROUP_DIM = 128
CHUNK = 128
MLP_W = 512
IN_PROJ = 2048
N_MOD = 9
N_DEV = 8

LR = 0.001
B1 = 0.9
B2 = 0.999
ADAM_EPS = 1e-08
WD = 0.01
STEP = 10

SHARD_FF_IN = 704
HALF = 352
HALF_PAD = 384
SHARD_PAD = 2 * HALF_PAD
HID_PAD = 4 * SHARD_PAD
N_MIX_SHARD = IN_PROJ // N_DEV
OMIX_SHARD = D // N_DEV
ADA_SHARD = N_MOD * D // N_DEV

TM = 256
HALO = 8
VMEM_LIMIT = 60 * 1024 * 1024

MESH = pl.DeviceIdType.MESH
ANY = pl.BlockSpec(memory_space=pl.ANY)


def _cp(n_axes=1):
    return pltpu.CompilerParams(dimension_semantics=("arbitrary",) * n_axes, vmem_limit_bytes=VMEM_LIMIT)


def _position():
    return lax.axis_index("x"), lax.axis_index("y"), lax.axis_index("c")


class GatherRider:
    def __init__(self, shards):
        n = len(shards)
        self.n = n
        self.ins = list(shards)
        self.out_shape = [jax.ShapeDtypeStruct((N_DEV,) + s.shape, s.dtype) for s in shards]
        self.sems = [pltpu.SemaphoreType.DMA((n, 7)), pltpu.SemaphoreType.DMA((n, 7)), pltpu.SemaphoreType.DMA((n,))]

    def _ctx(self, outs, sems):
        x, y, c = _position()
        chips = [(1 - x, y), (x, 1 - y), (1 - x, 1 - y)]

        def copy(t, k, block, to, src=None):
            dst = outs[t].at[4 * block[0] + 2 * block[1] + block[2]]
            return pltpu.make_async_remote_copy(
                src_ref=dst if src is None else src, dst_ref=dst, send_sem=sems[0].at[t, k],
                recv_sem=sems[1].at[t, k], device_id=to, device_id_type=MESH)

        return (x, y, c), (x, y, 1 - c), chips, copy

    def _local(self, ins, outs, sems, t):
        x, y, c = _position()
        return pltpu.make_async_copy(ins[t], outs[t].at[4 * x + 2 * y + c], sems[2].at[t])

    def _first(self, ins, outs, sems, t):
        me, sibling, chips, copy = self._ctx(outs, sems)
        return [copy(t, 0, me, sibling, src=ins[t])] + [copy(t, 1 + j, me, (*chip, me[2]), src=ins[t])
                                                         for j, chip in enumerate(chips)]

    def start(self, ins, outs, sems):
        for t in range(self.n):
            self._local(ins, outs, sems, t).start()
            for cp in self._first(ins, outs, sems, t):
                cp.start()

    def mid(self, ins, outs, sems):
        me, sibling, chips, copy = self._ctx(outs, sems)
        for j, chip in enumerate(chips):
            for t in range(self.n):
                copy(t, 1 + j, (*chip, me[2]), me).wait_recv()
                copy(t, 4 + j, (*chip, me[2]), sibling).start()

    def finish(self, ins, outs, sems):
        me, sibling, chips, copy = self._ctx(outs, sems)
        for t in range(self.n):
            copy(t, 0, sibling, me).wait_recv()
            for j, chip in enumerate(chips):
                copy(t, 4 + j, (*chip, 1 - me[2]), me).wait_recv()
        for t in range(self.n):
            for cp in self._first(ins, outs, sems, t):
                cp.wait_send()
            for j, chip in enumerate(chips):
                copy(t, 4 + j, (*chip, me[2]), sibling).wait_send()
            self._local(ins, outs, sems, t).wait()


class ExchangeRider:
    def __init__(self, tensors, plan, n_slots):
        n = len(tensors)
        self.n, self.plan = n, plan
        self.ins = list(tensors)
        self.out_shape = [jax.ShapeDtypeStruct((n_slots,) + s.shape[1:], s.dtype) for s in tensors]
        self.sems = [pltpu.SemaphoreType.DMA((n, n_slots)), pltpu.SemaphoreType.DMA((n, n_slots))]

    def _copies(self, ins, outs, sems):
        return [pltpu.make_async_remote_copy(
            src_ref=ins[t].at[block], dst_ref=outs[t].at[k], send_sem=sems[0].at[t, k], recv_sem=sems[1].at[t, k],
            device_id=to, device_id_type=MESH)
            for t in range(self.n) for k, (block, to) in enumerate(self.plan(*_position()))]

    def start(self, ins, outs, sems):
        for cp in self._copies(ins, outs, sems):
            cp.start()

    def mid(self, ins, outs, sems):
        pass

    def finish(self, ins, outs, sems):
        for cp in self._copies(ins, outs, sems):
            cp.wait()


class Riders:
    def __init__(self, riders):
        self.riders = list(riders)
        self.ins = [a for r in self.riders for a in r.ins]
        self.out_shape = [s for r in self.riders for s in r.out_shape]
        self.sems = [s for r in self.riders for s in r.sems]

    def _each(self, ins, outs, sems):
        i = o = s = 0
        for r in self.riders:
            ni, no, ns = len(r.ins), len(r.out_shape), len(r.sems)
            yield r, ins[i:i + ni], outs[o:o + no], sems[s:s + ns]
            i, o, s = i + ni, o + no, s + ns

    def start(self, ins, outs, sems):
        for r, a, b, c in self._each(ins, outs, sems):
            r.start(a, b, c)

    def mid(self, ins, outs, sems):
        for r, a, b, c in self._each(ins, outs, sems):
            r.mid(a, b, c)

    def finish(self, ins, outs, sems):
        for r, a, b, c in self._each(ins, outs, sems):
            r.finish(a, b, c)

    def split(self, outs):
        res, o = [], 0
        for r in self.riders:
            res.append(list(outs[o:o + len(r.out_shape)]))
            o += len(r.out_shape)
        return res


def pair_rider(grads):
    def plan(x, y, c):
        return [(4 * cx + 2 * cy + (1 - c), (x, y, 1 - c)) for cx in range(2) for cy in range(2)]
    return ExchangeRider(grads, plan, 4)


def chips_rider(parts):
    def plan(x, y, c):
        return [(2 * cx + cy, (cx, cy, c)) for cx, cy in [(1 - x, y), (x, 1 - y), (1 - x, 1 - y)]]
    return ExchangeRider(parts, plan, 3)


def run_alone(rider, space, name):
    ni = len(rider.ins)
    no = len(rider.out_shape)

    def body(*refs):
        ins, outs, sems = refs[:ni], refs[ni:ni + no], refs[ni + no:]
        rider.start(ins, outs, sems)
        rider.mid(ins, outs, sems)
        rider.finish(ins, outs, sems)

    spec = pl.BlockSpec(memory_space=space)
    return pl.pallas_call(
        body, name=name, in_specs=[spec] * ni, out_specs=[spec] * no, out_shape=rider.out_shape,
        scratch_shapes=rider.sems, compiler_params=pltpu.CompilerParams(vmem_limit_bytes=VMEM_LIMIT),
    )(*rider.ins)


def _grid_call(body, *, name, nsteps, in_specs, out_specs, out_shape, scratch_shapes, args, rider=None, aliases=None):
    aliases = aliases or {}
    if rider is None:
        outs = pl.pallas_call(body, name=name, grid=(nsteps,), in_specs=in_specs, out_specs=out_specs,
                              out_shape=out_shape, scratch_shapes=scratch_shapes, input_output_aliases=aliases,
                              compiler_params=_cp())(*args)
        return outs, []
    ni, no, ns = len(in_specs), len(out_specs), len(scratch_shapes)
    ri, ro = len(rider.ins), len(rider.out_shape)

    def wrapped(*refs):
        ins, refs = refs[:ni], refs[ni:]
        r_ins, refs = refs[:ri], refs[ri:]
        outs, refs = refs[:no], refs[no:]
        r_outs, refs = refs[:ro], refs[ro:]
        scratch, r_sems = refs[:ns], refs[ns:]
        s = pl.program_id(0)

        @pl.when(s == 0)
        def _():
            rider.start(r_ins, r_outs, r_sems)

        body(*ins, *outs, *scratch)

        @pl.when(s == (3 * nsteps) // 4)
        def _():
            rider.mid(r_ins, r_outs, r_sems)

        @pl.when(s == nsteps - 1)
        def _():
            rider.finish(r_ins, r_outs, r_sems)

    outs = pl.pallas_call(
        wrapped, name=name, grid=(nsteps,), in_specs=list(in_specs) + [ANY] * ri, out_specs=list(out_specs) + [ANY] * ro,
        out_shape=list(out_shape) + rider.out_shape, scratch_shapes=list(scratch_shapes) + rider.sems,
        input_output_aliases=aliases, compiler_params=_cp())(*args, *rider.ins)
    return outs[:no], outs[no:]


def _dot(a, b):
    return jnp.dot(a, b, preferred_element_type=F32)


def _dot_nt(a, b):
    return lax.dot_general(a, b, (((1,), (1,)), ((), ())), preferred_element_type=F32)


def _dot_tn(a, b):
    return lax.dot_general(a, b, (((0,), (0,)), ((), ())), preferred_element_type=F32)


def _sigmoid(x):
    return 1.0 / (1.0 + jnp.exp(-x))


def _kind(i):
    return jnp.where(i < CTX // TM, 0, 1)


def _sel(kind, mod_ref, k):
    return mod_ref[kind, k:k + 1, :]


def _acc2(ref, k, val, kind):
    ref[kind, k:k + 1, :] += jnp.sum(val, axis=0, keepdims=True)


def _norm_mod(h, g, shift, scale):
    r = lax.rsqrt(jnp.mean(h * h, axis=-1, keepdims=True) + EPS)
    n = h * r
    return (n * g) * (1.0 + scale) + shift, n, r


def _norm_mod_bwd(dz, n, r, g, scale):
    dn = dz * (g * (1.0 + scale))
    return r * (dn - n * jnp.mean(dn * n, axis=-1, keepdims=True))


def ffn_fwd(h, mod, g, win, wout, name, rider=None):
    split = isinstance(h, tuple)
    nc = CTX // TM
    t = h[0].shape[0] + h[1].shape[0] if split else h.shape[0]

    def body(*refs):
        if split:
            c_ref, x_ref, mod_ref, g_ref, win_hbm, wout_hbm, out_ref, gu_ref, acc_ref, h0_ref, win_v, wout_v = refs
        else:
            h_ref, mod_ref, g_ref, win_hbm, wout_hbm, out_ref, gu_ref, acc_ref, win_v, wout_v = refs
        i = pl.program_id(0)

        @pl.when(i == 0)
        def _():
            pltpu.sync_copy(win_hbm, win_v)
            pltpu.sync_copy(wout_hbm, wout_v)

        if split:
            hh = jnp.where(i < nc, c_ref[...], x_ref[...])
            h0_ref[...] = hh
        else:
            hh = h_ref[...]
        ic = _kind(i)
        z, _, _ = _norm_mod(hh, g_ref[...], _sel(ic, mod_ref, 0), _sel(ic, mod_ref, 1))
        zb = z.astype(BF16)
        acc = jnp.zeros((TM, D), F32)
        for dd in range(4):
            gg = _dot(zb, win_v[dd])
            uu = _dot(zb, win_v[dd + 4])
            gu_ref[:, dd * SHARD_PAD:(dd + 1) * SHARD_PAD] = gg.astype(BF16)
            gu_ref[:, (dd + 4) * SHARD_PAD:(dd + 5) * SHARD_PAD] = uu.astype(BF16)
            a = (gg * _sigmoid(gg)) * uu
            acc = acc + _dot(a.astype(BF16), wout_v[dd * SHARD_PAD:(dd + 1) * SHARD_PAD, :])
        acc_ref[...] = acc
        out_ref[...] = hh + (0.5 * _sel(ic, mod_ref, 2)) * acc

    row = pl.BlockSpec((TM, D), lambda i: (i, 0))
    rshape = jax.ShapeDtypeStruct((t, D), F32)
    if split:
        rows_in = [pl.BlockSpec((TM, D), lambda i: (jnp.minimum(i, nc - 1), 0)),
                   pl.BlockSpec((TM, D), lambda i: (jnp.maximum(i - nc, 0), 0))]
    else:
        rows_in = [row]
    return _grid_call(
        body, name=name, nsteps=t // TM,
        in_specs=rows_in + [pl.BlockSpec((2, 3, D), lambda i: (0, 0, 0)), pl.BlockSpec((1, D), lambda i: (0, 0)), ANY, ANY],
        out_specs=[row, pl.BlockSpec((TM, 2 * HID_PAD), lambda i: (i, 0)), row] + ([row] if split else []),
        out_shape=[rshape, jax.ShapeDtypeStruct((t, 2 * HID_PAD), BF16), rshape] + ([rshape] if split else []),
        scratch_shapes=[pltpu.VMEM((N_DEV, D, SHARD_PAD), BF16), pltpu.VMEM((HID_PAD, D), BF16)],
        args=(*(h if split else (h,)), mod, g, win, wout), rider=rider)


def ffn_bwd_a(dy, acc, gu, mod, wout, name, rider=None):
    t = dy.shape[0]
    nt = t // TM

    def body(dy_ref, acc_ref, gu_ref, mod_ref, wout_hbm, dp_ref, dwout_hbm, dgate_ref, wout_v, dwout_v):
        i = pl.program_id(0)

        @pl.when(i == 0)
        def _():
            pltpu.sync_copy(wout_hbm, wout_v)
            dwout_v[...] = jnp.zeros_like(dwout_v)
            dgate_ref[...] = jnp.zeros_like(dgate_ref)

        dy_ = dy_ref[...]
        ic = _kind(i)
        _acc2(dgate_ref, 0, 0.5 * dy_ * acc_ref[...], ic)
        daccb = ((0.5 * _sel(ic, mod_ref, 2)) * dy_).astype(BF16)
        for dd in range(4):
            blk = slice(dd * SHARD_PAD, (dd + 1) * SHARD_PAD)
            ublk = slice((dd + 4) * SHARD_PAD, (dd + 5) * SHARD_PAD)
            da = _dot_nt(daccb, wout_v[blk, :])
            gg = gu_ref[:, blk].astype(F32)
            uu = gu_ref[:, ublk].astype(F32)
            s = _sigmoid(gg)
            sl = gg * s
            dwout_v[blk, :] += _dot_tn((sl * uu).astype(BF16), daccb)
            dp_ref[:, blk] = (da * uu * (s + sl * (1.0 - s))).astype(BF16)
            dp_ref[:, ublk] = (da * sl).astype(BF16)

        @pl.when(i == nt - 1)
        def _():
            pltpu.sync_copy(dwout_v, dwout_hbm)

    row = pl.BlockSpec((TM, D), lambda i: (i, 0))
    wide = pl.BlockSpec((TM, 2 * HID_PAD), lambda i: (i, 0))
    return _grid_call(
        body, name=name, nsteps=nt,
        in_specs=[row, row, wide, pl.BlockSpec((2, 3, D), lambda i: (0, 0, 0)), ANY],
        out_specs=[wide, ANY, pl.BlockSpec((2, 1, D), lambda i: (0, 0, 0))],
        out_shape=[jax.ShapeDtypeStruct((t, 2 * HID_PAD), BF16), jax.ShapeDtypeStruct((HID_PAD, D), F32),
                   jax.ShapeDtypeStruct((2, 1, D), F32)],
        scratch_shapes=[pltpu.VMEM((HID_PAD, D), BF16), pltpu.VMEM((HID_PAD, D), F32)],
        args=(dy, acc, gu, mod, wout), rider=rider)


def ffn_bwd_b(dy, h, dp, mod, g, win, name, rider=None, latent_only=False):
    t = dy.shape[0]
    nt = t // TM
    nc = CTX // TM

    def body(dy_ref, h_ref, dp_ref, mod_ref, g_ref, win_hbm, dh_ref, dwin_hbm, dmod_ref, dg_ref, win_v, dwin_v):
        i = pl.program_id(0)

        @pl.when(i == 0)
        def _():
            pltpu.sync_copy(win_hbm, win_v)
            dwin_v[...] = jnp.zeros_like(dwin_v)
            dmod_ref[...] = jnp.zeros_like(dmod_ref)
            dg_ref[...] = jnp.zeros_like(dg_ref)

        ic = _kind(i)
        gain = g_ref[...]
        scale = _sel(ic, mod_ref, 1)
        z, n, r = _norm_mod(h_ref[...], gain, _sel(ic, mod_ref, 0), scale)
        zb = z.astype(BF16)
        dz = jnp.zeros((TM, D), F32)
        for dd in range(N_DEV):
            dpd = dp_ref[:, dd * SHARD_PAD:(dd + 1) * SHARD_PAD]
            dz = dz + _dot_nt(dpd, win_v[dd])
            dwin_v[dd] += _dot_tn(zb, dpd)
        _acc2(dmod_ref, 0, dz, ic)
        _acc2(dmod_ref, 1, dz * (n * gain), ic)
        dg_ref[...] += jnp.sum(dz * (1.0 + scale) * n, axis=0, keepdims=True)
        dh_ref[...] = dy_ref[...] + _norm_mod_bwd(dz, n, r, gain, scale)

        @pl.when(i == nt - 1)
        def _():
            pltpu.sync_copy(dwin_v, dwin_hbm)

    row = pl.BlockSpec((TM, D), lambda i: (i, 0))
    if latent_only:
        dh_spec = pl.BlockSpec((TM, D), lambda i: (jnp.maximum(i - nc, 0), 0))
        dh_shape = jax.ShapeDtypeStruct((t - CTX, D), F32)
    else:
        dh_spec, dh_shape = row, jax.ShapeDtypeStruct((t, D), F32)
    return _grid_call(
        body, name=name, nsteps=nt,
        in_specs=[row, row, pl.BlockSpec((TM, 2 * HID_PAD), lambda i: (i, 0)),
                  pl.BlockSpec((2, 3, D), lambda i: (0, 0, 0)), pl.BlockSpec((1, D), lambda i: (0, 0)), ANY],
        out_specs=[dh_spec, ANY, pl.BlockSpec((2, 2, D), lambda i: (0, 0, 0)), pl.BlockSpec((1, D), lambda i: (0, 0))],
        out_shape=[dh_shape, jax.ShapeDtypeStruct((N_DEV, D, SHARD_PAD), F32),
                   jax.ShapeDtypeStruct((2, 2, D), F32), jax.ShapeDtypeStruct((1, D), F32)],
        scratch_shapes=[pltpu.VMEM((N_DEV, D, SHARD_PAD), BF16), pltpu.VMEM((N_DEV, D, SHARD_PAD), F32)],
        args=(dy, h, dp, mod, g, win), rider=rider)


def ffn_bwd_dw(h, dp, mod, g, name, rider=None):
    t = h.shape[0]
    nt = t // TM

    def body(h_ref, dp_ref, mod_ref, g_ref, dwin_hbm, dwin_v):
        i = pl.program_id(0)

        @pl.when(i == 0)
        def _():
            dwin_v[...] = jnp.zeros_like(dwin_v)

        ic = _kind(i)
        z, _, _ = _norm_mod(h_ref[...], g_ref[...], _sel(ic, mod_ref, 0), _sel(ic, mod_ref, 1))
        zb = z.astype(BF16)
        for dd in range(N_DEV):
            dwin_v[dd] += _dot_tn(zb, dp_ref[:, dd * SHARD_PAD:(dd + 1) * SHARD_PAD])

        @pl.when(i == nt - 1)
        def _():
            pltpu.sync_copy(dwin_v, dwin_hbm)

    return _grid_call(
        body, name=name, nsteps=nt,
        in_specs=[pl.BlockSpec((TM, D), lambda i: (i, 0)), pl.BlockSpec((TM, 2 * HID_PAD), lambda i: (i, 0)),
                  pl.BlockSpec((2, 3, D), lambda i: (0, 0, 0)), pl.BlockSpec((1, D), lambda i: (0, 0))],
        out_specs=[ANY], out_shape=[jax.ShapeDtypeStruct((N_DEV, D, SHARD_PAD), F32)],
        scratch_shapes=[pltpu.VMEM((N_DEV, D, SHARD_PAD), F32)],
        args=(h, dp, mod, g), rider=rider)


def ffn_bwd_dh(dy, h, dp, mod, g, win, name, tiles, carry=None, rider=None):
    t = dy.shape[0]
    nc = CTX // TM
    t0, t1 = tiles

    def body(*refs):
        if carry is None:
            dy_ref, h_ref, dp_ref, mod_ref, g_ref, win_hbm, dh_ref, dmod_ref, dg_ref, win_v = refs
        else:
            dy_ref, h_ref, dp_ref, mod_ref, g_ref, win_hbm, _, dmod0_ref, dg0_ref, dh_ref, dmod_ref, dg_ref, win_v = refs
        i = pl.program_id(0)

        @pl.when(i == 0)
        def _():
            pltpu.sync_copy(win_hbm, win_v)
            dmod_ref[...] = jnp.zeros_like(dmod_ref) if carry is None else dmod0_ref[...]
            dg_ref[...] = jnp.zeros_like(dg_ref) if carry is None else dg0_ref[...]

        ic = _kind(i + t0)
        gain = g_ref[...]
        scale = _sel(ic, mod_ref, 1)
        _, n, r = _norm_mod(h_ref[...], gain, _sel(ic, mod_ref, 0), scale)
        dz = jnp.zeros((TM, D), F32)
        for dd in range(N_DEV):
            dz = dz + _dot_nt(dp_ref[:, dd * SHARD_PAD:(dd + 1) * SHARD_PAD], win_v[dd])
        _acc2(dmod_ref, 0, dz, ic)
        _acc2(dmod_ref, 1, dz * (n * gain), ic)
        dg_ref[...] += jnp.sum(dz * (1.0 + scale) * n, axis=0, keepdims=True)
        dh_ref[...] = dy_ref[...] + _norm_mod_bwd(dz, n, r, gain, scale)

    row = pl.BlockSpec((TM, D), lambda i: (i + t0, 0))
    small = [pl.BlockSpec((2, 2, D), lambda i: (0, 0, 0)), pl.BlockSpec((1, D), lambda i: (0, 0))]
    in_specs = [row, row, pl.BlockSpec((TM, 2 * HID_PAD), lambda i: (i + t0, 0)),
                pl.BlockSpec((2, 3, D), lambda i: (0, 0, 0)), pl.BlockSpec((1, D), lambda i: (0, 0)), ANY]
    args = (dy, h, dp, mod, g, win)
    if carry is not None:
        in_specs += [ANY] + small
        args += tuple(carry)
    return _grid_call(
        body, name=name, nsteps=t1 - t0, in_specs=in_specs,
        out_specs=[pl.BlockSpec((TM, D), lambda i: (jnp.maximum(i + t0 - nc, 0), 0))] + small,
        out_shape=[jax.ShapeDtypeStruct((t - CTX, D), F32), jax.ShapeDtypeStruct((2, 2, D), F32),
                   jax.ShapeDtypeStruct((1, D), F32)],
        scratch_shapes=[pltpu.VMEM((N_DEV, D, SHARD_PAD), BF16)],
        args=args, rider=rider, aliases=None if carry is None else {6: 0})


def mix_in_fwd(h, mod, g, wmix, name):
    t = h.shape[0]

    def body(h_ref, mod_ref, g_ref, w_hbm, p_ref, w_v):
        i = pl.program_id(0)

        @pl.when(i == 0)
        def _():
            pltpu.sync_copy(w_hbm, w_v)

        ic = _kind(i)
        z, _, _ = _norm_mod(h_ref[...], g_ref[...], _sel(ic, mod_ref, 0), _sel(ic, mod_ref, 1))
        zb = z.astype(BF16)
        for dd in range(N_DEV):
            p_ref[:, dd * N_MIX_SHARD:(dd + 1) * N_MIX_SHARD] = _dot(zb, w_v[dd])

    return pl.pallas_call(
        body, name=name, grid=(t // TM,),
        in_specs=[pl.BlockSpec((TM, D), lambda i: (i, 0)), pl.BlockSpec((2, 3, D), lambda i: (0, 0, 0)),
                  pl.BlockSpec((1, D), lambda i: (0, 0)), ANY],
        out_specs=pl.BlockSpec((TM, IN_PROJ), lambda i: (i, 0)),
        out_shape=jax.ShapeDtypeStruct((t, IN_PROJ), F32),
        scratch_shapes=[pltpu.VMEM((N_DEV, D, N_MIX_SHARD), BF16)],
        compiler_params=_cp(),
    )(h, mod, g, wmix)


def _halo_specs(nt, tile_of):
    nb = nt * (TM // HALO)
    main = pl.BlockSpec((TM, LRU_W), lambda s: (tile_of(s), 0))
    prev = pl.BlockSpec((HALO, LRU_W), lambda s: (jnp.maximum(tile_of(s) * (TM // HALO) - 1, 0), 0))
    nxt = pl.BlockSpec((HALO, LRU_W), lambda s: (jnp.minimum((tile_of(s) + 1) * (TM // HALO), nb - 1), 0))
    return main, prev, nxt


def _ext(tile, nt, main, prev, nxt):
    has_prev = jnp.logical_and(tile != 0, tile != 1)
    has_next = jnp.logical_and(tile != 0, tile != nt - 1)
    return jnp.concatenate([jnp.where(has_prev, prev, 0.0), main, jnp.where(has_next, nxt, 0.0)], axis=0)


def _shifted(ext, off):
    n = ext.shape[0]
    return pltpu.roll(ext, (-off) % n, 0)[HALO:HALO + TM]


def _conv(ext, cw_ref, cb_ref):
    xc = cb_ref[...] + cw_ref[0:1, :] * _shifted(ext, -2)
    for k in range(1, CONV_W):
        xc = xc + cw_ref[k:k + 1, :] * _shifted(ext, k - 2)
    return xc


def _log1p(y):
    return jnp.where(y < 1e-2, y * (1.0 - y * (0.5 - y * (1.0 / 3.0 - 0.25 * y))), jnp.log(1.0 + y))


def _softplus_neg(lam):
    return jnp.maximum(-lam, 0.0) + _log1p(jnp.exp(-jnp.abs(lam)))


def _one_minus_exp(x, exp_half):
    p = x * (1.0 + x * (1 / 2 + x * (1 / 6 + x * (1 / 24))))
    return jnp.where(x > -0.1, -p, 1.0 - exp_half * exp_half)


def _gates(xc, wr, br, wi, bi, lam):
    xb = xc.astype(BF16)
    r = _sigmoid(_dot(xb, wr) + br)
    ig = _sigmoid(_dot(xb, wi) + bi)
    sp = _softplus_neg(lam)
    log_a = -RG_C * r * sp
    a = jnp.exp(log_a)
    mult = jnp.sqrt(_one_minus_exp(2.0 * log_a, a))
    return r, ig, sp, a, mult


def _scan(a, b, reverse):
    n = a.shape[0]
    row = lax.broadcasted_iota(jnp.int32, a.shape, 0)
    s = 1
    while s < n:
        if s < HALO:
            if reverse:
                keep = row < n - s
                a_s = jnp.where(keep, pltpu.roll(a, n - s, 0), 1.0)
                b_s = jnp.where(keep, pltpu.roll(b, n - s, 0), 0.0)
            else:
                keep = row >= s
                a_s = jnp.where(keep, pltpu.roll(a, s, 0), 1.0)
                b_s = jnp.where(keep, pltpu.roll(b, s, 0), 0.0)
            b = a * b_s + b
            a = a * a_s
        elif reverse:
            b = jnp.concatenate([a[:n - s] * b[s:] + b[:n - s], b[n - s:]], axis=0)
            a = jnp.concatenate([a[:n - s] * a[s:], a[n - s:]], axis=0)
        else:
            b = jnp.concatenate([b[:s], a[s:] * b[:n - s] + b[s:]], axis=0)
            a = jnp.concatenate([a[:s], a[s:] * a[:n - s]], axis=0)
        s *= 2
    return a, b


def lru_fwd(p, conv_w, conv_b, wr, br, wi, bi, lam, reverse, name, rider=None):
    t = p.shape[0]
    nt = t // TM

    def tile_of(s):
        return jnp.where(s == 0, 0, nt - s) if reverse else s

    def body(x_ref, xp_ref, xn_ref, cw_ref, cb_ref, wr_ref, br_ref, wi_ref, bi_ref, lam_ref, h_ref, carry):
        s = pl.program_id(0)
        tile = tile_of(s)

        @pl.when(s == 0)
        def _():
            carry[...] = jnp.zeros_like(carry)

        ext = _ext(tile, nt, x_ref[...], xp_ref[...], xn_ref[...])
        xc = _conv(ext, cw_ref, cb_ref)
        _, ig, _, a, mult = _gates(xc, wr_ref[...], br_ref[...], wi_ref[...], bi_ref[...], lam_ref[...])
        a_cum, hl = _scan(a, mult * (ig * xc), reverse)
        hh = hl + a_cum * carry[...]
        h_ref[...] = hh
        carry[...] = hh[0:1, :] if reverse else hh[TM - 1:TM, :]

    main, prev, nxt = _halo_specs(nt, tile_of)
    vec = pl.BlockSpec((1, LRU_W), lambda s: (0, 0))
    mat = pl.BlockSpec((LRU_W, LRU_W), lambda s: (0, 0))
    outs, got = _grid_call(
        body, name=name, nsteps=nt,
        in_specs=[main, prev, nxt, pl.BlockSpec((CONV_W, LRU_W), lambda s: (0, 0)), vec, mat, vec, mat, vec, vec],
        out_specs=[main],
        out_shape=[jax.ShapeDtypeStruct((t, LRU_W), F32)],
        scratch_shapes=[pltpu.VMEM((1, LRU_W), F32)],
        args=(p, p, p, conv_w, conv_b, wr, br, wi, bi, lam), rider=rider)
    return outs[0], got


def lru_bwd(p, hs, dhs, conv_w, conv_b, wr, br, wi, bi, lam, reverse, name):
    t = p.shape[0]
    nt = t // TM
    bpt = TM // HALO

    def tile_of(s):
        return jnp.where(s == nt - 1, 0, s + 1) if reverse else nt - 1 - s

    def hprev_block(s):
        tile = tile_of(s)
        if reverse:
            return (jnp.where(tile == nt - 1, 0, jnp.minimum((tile + 1) * bpt, nt * bpt - 1)), 0)
        return (jnp.maximum(tile * bpt - 1, 0), 0)

    def body(x_ref, xp_ref, xn_ref, h_ref, hp_ref, dh_ref, cw_ref, cb_ref, wr_ref, br_ref, wi_ref, bi_ref, lam_ref,
             dxc_ref, dwr_ref, dwi_ref, dbr_ref, dbi_ref, dlam_ref, carry):
        s = pl.program_id(0)
        tile = tile_of(s)

        @pl.when(s == 0)
        def _():
            carry[...] = jnp.zeros_like(carry)
            for ref in (dwr_ref, dwi_ref, dbr_ref, dbi_ref, dlam_ref):
                ref[...] = jnp.zeros_like(ref)

        ext = _ext(tile, nt, x_ref[...], xp_ref[...], xn_ref[...])
        xc = _conv(ext, cw_ref, cb_ref)
        wr_, wi_ = wr_ref[...], wi_ref[...]
        r, ig, sp, a, mult = _gates(xc, wr_, br_ref[...], wi_, bi_ref[...], lam_ref[...])
        gated = ig * xc
        row = lax.broadcasted_iota(jnp.int32, (TM, LRU_W), 0)
        hh = h_ref[...]
        start = jnp.where(tile != 0, hp_ref[0:1, :] if reverse else hp_ref[HALO - 1:HALO, :], 0.0)
        if reverse:
            edge = row == TM - 1
            hprev = jnp.where(edge, start, pltpu.roll(hh, TM - 1, 0))
            coef = jnp.where(row == 0, 0.0, pltpu.roll(a, 1, 0))
            bb = dh_ref[...] + jnp.where(row == 0, carry[...], 0.0)
        else:
            edge = row == 0
            hprev = jnp.where(edge, start, pltpu.roll(hh, 1, 0))
            coef = jnp.where(row == TM - 1, 0.0, pltpu.roll(a, TM - 1, 0))
            bb = dh_ref[...] + jnp.where(row == TM - 1, carry[...], 0.0)
        _, lmb = _scan(coef, bb, not reverse)
        al = a * lmb
        carry[...] = al[TM - 1:TM, :] if reverse else al[0:1, :]

        dgated = lmb * mult
        dloga = (lmb * hprev) * a - (lmb * gated) * (a * a) / mult
        dpre_r = (dloga * (-RG_C * sp)) * r * (1.0 - r)
        dpre_i = (dgated * xc) * ig * (1.0 - ig)
        drb, dib = dpre_r.astype(BF16), dpre_i.astype(BF16)
        xb = xc.astype(BF16)
        dxc_ref[...] = dgated * ig + _dot_nt(drb, wr_) + _dot_nt(dib, wi_)
        dwr_ref[...] += _dot_tn(xb, drb)
        dwi_ref[...] += _dot_tn(xb, dib)
        dbr_ref[...] += jnp.sum(dpre_r, axis=0, keepdims=True)
        dbi_ref[...] += jnp.sum(dpre_i, axis=0, keepdims=True)
        dlam_ref[...] += jnp.sum(dloga * (-RG_C * r), axis=0, keepdims=True)

        @pl.when(s == nt - 1)
        def _():
            dlam_ref[...] = dlam_ref[...] * (-_sigmoid(-lam_ref[...]))

    main, prev, nxt = _halo_specs(nt, tile_of)
    vec = pl.BlockSpec((1, LRU_W), lambda s: (0, 0))
    mat = pl.BlockSpec((LRU_W, LRU_W), lambda s: (0, 0))
    vshape = jax.ShapeDtypeStruct((1, LRU_W), F32)
    mshape = jax.ShapeDtypeStruct((LRU_W, LRU_W), F32)
    return pl.pallas_call(
        body, name=name, grid=(nt,),
        in_specs=[main, prev, nxt, main, pl.BlockSpec((HALO, LRU_W), hprev_block), main,
                  pl.BlockSpec((CONV_W, LRU_W), lambda s: (0, 0)), vec, mat, vec, mat, vec, vec],
        out_specs=[main, mat, mat, vec, vec, vec],
        out_shape=[jax.ShapeDtypeStruct((t, LRU_W), F32), mshape, mshape, vshape, vshape, vshape],
        scratch_shapes=[pltpu.VMEM((1, LRU_W), F32)],
        compiler_params=_cp(),
    )(p, p, p, hs, hs, dhs, conv_w, conv_b, wr, br, wi, bi, lam)


GELU_C = 0.7978845608028654
GELU_A = 0.044715


def _gelu(x):
    th = jnp.tanh(GELU_C * (x + GELU_A * x * x * x))
    return 0.5 * x * (1.0 + th), th


def _sgu(v, gain, w_ref, bt_ref):
    mu = jnp.mean(v, axis=-1, keepdims=True)
    xc = v - mu
    rs = lax.rsqrt(jnp.mean(xc * xc, axis=-1, keepdims=True) + EPS)
    vhat = xc * rs
    vnb = (vhat * gain).astype(BF16)
    chunks = []
    for ch in range(TM // CHUNK):
        zs = []
        for gi in range(GROUPS):
            vb = vnb[ch * CHUNK:(ch + 1) * CHUNK, gi * GROUP_DIM:(gi + 1) * GROUP_DIM]
            zs.append(_dot(w_ref[gi].astype(BF16), vb) + bt_ref[:, gi:gi + 1])
        chunks.append(jnp.concatenate(zs, axis=1))
    return jnp.concatenate(chunks, axis=0), vhat, rs, vnb


def _pcols(k):
    return pl.BlockSpec((TM, LRU_W), lambda i: (i, k))


def mix_out_fwd(h, p, hf, hb, mod, sgu_g, sgu_w, sgu_bt, womix, name):
    t = h.shape[0]

    def body(h_ref, gl_ref, u_ref, v_ref, hf_ref, hb_ref, mod_ref, sg_ref, sw_ref, sb_ref, w_hbm, out_ref, o_ref, w_v):
        i = pl.program_id(0)

        @pl.when(i == 0)
        def _():
            pltpu.sync_copy(w_hbm, w_v)

        ic = _kind(i)
        ge, _ = _gelu(gl_ref[...])
        y_lru = (hf_ref[...] + hb_ref[...]) * ge
        z, _, _, _ = _sgu(v_ref[...], sg_ref[...], sw_ref, sb_ref)
        y = jnp.concatenate([y_lru, u_ref[...] * z], axis=1).astype(BF16)
        o = _dot(y, w_v[...])
        o_ref[...] = o
        out_ref[...] = h_ref[...] + _sel(ic, mod_ref, 2) * o

    row = pl.BlockSpec((TM, D), lambda i: (i, 0))
    half = pl.BlockSpec((TM, LRU_W), lambda i: (i, 0))
    return pl.pallas_call(
        body, name=name, grid=(t // TM,),
        in_specs=[row, _pcols(1), _pcols(2), _pcols(3), half, half, pl.BlockSpec((2, 3, D), lambda i: (0, 0, 0)),
                  pl.BlockSpec((1, MLP_W), lambda i: (0, 0)), pl.BlockSpec((GROUPS, CHUNK, CHUNK), lambda i: (0, 0, 0)),
                  pl.BlockSpec((CHUNK, GROUPS), lambda i: (0, 0)), ANY],
        out_specs=[row, row],
        out_shape=[jax.ShapeDtypeStruct((t, D), F32), jax.ShapeDtypeStruct((t, D), F32)],
        scratch_shapes=[pltpu.VMEM((D, D), BF16)],
        compiler_params=_cp(),
    )(h, p, p, p, hf, hb, mod, sgu_g, sgu_w, sgu_bt, womix)


def mix_out_bwd(dy, p, hf, hb, o, mod, sgu_g, sgu_w, sgu_bt, womix, name, rider=None):
    t = dy.shape[0]

    def body(dy_ref, gl_ref, u_ref, v_ref, hf_ref, hb_ref, o_ref, mod_ref, sg_ref, sw_ref, sb_ref, w_hbm,
             dhs_ref, dp_ref, dw_ref, dgate_ref, dsg_ref, dsw_ref, dsb_ref, w_v):
        i = pl.program_id(0)

        @pl.when(i == 0)
        def _():
            pltpu.sync_copy(w_hbm, w_v)
            for ref in (dw_ref, dgate_ref, dsg_ref, dsw_ref, dsb_ref):
                ref[...] = jnp.zeros_like(ref)

        ic = _kind(i)
        dy_ = dy_ref[...]
        _acc2(dgate_ref, 0, dy_ * o_ref[...], ic)
        dob = (_sel(ic, mod_ref, 2) * dy_).astype(BF16)

        gl = gl_ref[...]
        ge, th = _gelu(gl)
        hsum = hf_ref[...] + hb_ref[...]
        gain = sg_ref[...]
        uu = u_ref[...]
        z, vhat, rs, vnb = _sgu(v_ref[...], gain, sw_ref, sb_ref)
        y = jnp.concatenate([hsum * ge, uu * z], axis=1).astype(BF16)
        dw_ref[...] += _dot_tn(y, dob)
        dyy = _dot_nt(dob, w_v[...])
        dyl, dys = dyy[:, :LRU_W], dyy[:, LRU_W:]

        dhs_ref[...] = dyl * ge
        dge = 0.5 * (1.0 + th) + 0.5 * gl * (1.0 - th * th) * (GELU_C * (1.0 + 3.0 * GELU_A * gl * gl))
        dp_ref[:, 0:LRU_W] = dyl * hsum * dge
        dp_ref[:, LRU_W:2 * LRU_W] = dys * z

        dz = dys * uu
        dzb = dz.astype(BF16)
        dvn_chunks, dsb_cols = [], [jnp.zeros((CHUNK, 1), F32)] * GROUPS
        for ch in range(TM // CHUNK):
            cols = []
            for gi in range(GROUPS):
                rs_, cs_ = slice(ch * CHUNK, (ch + 1) * CHUNK), slice(gi * GROUP_DIM, (gi + 1) * GROUP_DIM)
                dzg = dzb[rs_, cs_]
                dsb_cols[gi] = dsb_cols[gi] + jnp.sum(dz[rs_, cs_], axis=1, keepdims=True)
                dsw_ref[gi] += _dot_nt(dzg, vnb[rs_, cs_])
                cols.append(_dot_tn(sw_ref[gi].astype(BF16), dzg))
            dvn_chunks.append(jnp.concatenate(cols, axis=1))
        dsb_ref[...] += jnp.concatenate(dsb_cols, axis=1)
        dvn = jnp.concatenate(dvn_chunks, axis=0)
        dsg_ref[...] += jnp.sum(dvn * vhat, axis=0, keepdims=True)
        dvh = dvn * gain
        dp_ref[:, 2 * LRU_W:3 * LRU_W] = rs * (dvh - jnp.mean(dvh, axis=-1, keepdims=True)
                                               - vhat * jnp.mean(dvh * vhat, axis=-1, keepdims=True))

    row = pl.BlockSpec((TM, D), lambda i: (i, 0))
    half = pl.BlockSpec((TM, LRU_W), lambda i: (i, 0))
    const2 = lambda i: (0, 0)
    const3 = lambda i: (0, 0, 0)
    return _grid_call(
        body, name=name, nsteps=t // TM,
        in_specs=[row, _pcols(1), _pcols(2), _pcols(3), half, half, row, pl.BlockSpec((2, 3, D), const3),
                  pl.BlockSpec((1, MLP_W), const2), pl.BlockSpec((GROUPS, CHUNK, CHUNK), const3),
                  pl.BlockSpec((CHUNK, GROUPS), const2), ANY],
        out_specs=[half, pl.BlockSpec((TM, 3 * LRU_W), lambda i: (i, 0)), pl.BlockSpec((D, D), const2),
                   pl.BlockSpec((2, 1, D), const3), pl.BlockSpec((1, MLP_W), const2),
                   pl.BlockSpec((GROUPS, CHUNK, CHUNK), const3), pl.BlockSpec((CHUNK, GROUPS), const2)],
        out_shape=[jax.ShapeDtypeStruct((t, LRU_W), F32), jax.ShapeDtypeStruct((t, 3 * LRU_W), F32),
                   jax.ShapeDtypeStruct((D, D), F32), jax.ShapeDtypeStruct((2, 1, D), F32),
                   jax.ShapeDtypeStruct((1, MLP_W), F32), jax.ShapeDtypeStruct((GROUPS, CHUNK, CHUNK), F32),
                   jax.ShapeDtypeStruct((CHUNK, GROUPS), F32)],
        scratch_shapes=[pltpu.VMEM((D, D), BF16)],
        args=(dy, p, p, p, hf, hb, o, mod, sgu_g, sgu_w, sgu_bt, womix), rider=rider)


def mix_in_bwd(dy, h, p, dxf, dxb, dprest, mod, g, conv_w, wmix, name, rider=None):
    t = dy.shape[0]
    nt = t // TM

    def body(dy_ref, h_ref, x_ref, xp_ref, xn_ref, f_ref, fp_ref, fn_ref, b_ref, bp_ref, bn_ref, dpr_ref, mod_ref,
             g_ref, cw_ref, w_hbm, dh_ref, dw_hbm, dmod_ref, dg_ref, dcw_ref, dcb_ref, w_v, dw_v):
        i = pl.program_id(0)

        @pl.when(i == 0)
        def _():
            pltpu.sync_copy(w_hbm, w_v)
            dw_v[...] = jnp.zeros_like(dw_v)
            for ref in (dmod_ref, dg_ref, dcw_ref, dcb_ref):
                ref[...] = jnp.zeros_like(ref)

        dmain = f_ref[...] + b_ref[...]
        dext = _ext(i, nt, dmain, fp_ref[...] + bp_ref[...], fn_ref[...] + bn_ref[...])
        xext = _ext(i, nt, x_ref[...], xp_ref[...], xn_ref[...])
        dxl = cw_ref[0:1, :] * _shifted(dext, 2)
        for k in range(1, CONV_W):
            dxl = dxl + cw_ref[k:k + 1, :] * _shifted(dext, 2 - k)
        dcw_ref[...] += jnp.concatenate(
            [jnp.sum(dmain * _shifted(xext, k - 2), axis=0, keepdims=True) for k in range(CONV_W)], axis=0)
        dcb_ref[...] += jnp.sum(dmain, axis=0, keepdims=True)

        ic = _kind(i)
        gain = g_ref[...]
        scale = _sel(ic, mod_ref, 1)
        z, n, r = _norm_mod(h_ref[...], gain, _sel(ic, mod_ref, 0), scale)
        zb = z.astype(BF16)
        dpb = jnp.concatenate([dxl, dpr_ref[...]], axis=1).astype(BF16)
        dz = jnp.zeros((TM, D), F32)
        for dd in range(N_DEV):
            dpd = dpb[:, dd * N_MIX_SHARD:(dd + 1) * N_MIX_SHARD]
            dz = dz + _dot_nt(dpd, w_v[dd])
            dw_v[dd] += _dot_tn(zb, dpd)
        _acc2(dmod_ref, 0, dz, ic)
        _acc2(dmod_ref, 1, dz * (n * gain), ic)
        dg_ref[...] += jnp.sum(dz * (1.0 + scale) * n, axis=0, keepdims=True)
        dh_ref[...] = dy_ref[...] + _norm_mod_bwd(dz, n, r, gain, scale)

        @pl.when(i == nt - 1)
        def _():
            pltpu.sync_copy(dw_v, dw_hbm)

    main, prev, nxt = _halo_specs(nt, lambda s: s)
    row = pl.BlockSpec((TM, D), lambda i: (i, 0))
    const2 = lambda i: (0, 0)
    return _grid_call(
        body, name=name, nsteps=nt,
        in_specs=[row, row, main, prev, nxt, main, prev, nxt, main, prev, nxt,
                  pl.BlockSpec((TM, 3 * LRU_W), lambda i: (i, 0)), pl.BlockSpec((2, 3, D), lambda i: (0, 0, 0)),
                  pl.BlockSpec((1, D), const2), pl.BlockSpec((CONV_W, LRU_W), const2), ANY],
        out_specs=[row, ANY, pl.BlockSpec((2, 2, D), lambda i: (0, 0, 0)), pl.BlockSpec((1, D), const2),
                   pl.BlockSpec((CONV_W, LRU_W), const2), pl.BlockSpec((1, LRU_W), const2)],
        out_shape=[jax.ShapeDtypeStruct((t, D), F32), jax.ShapeDtypeStruct((N_DEV, D, N_MIX_SHARD), F32),
                   jax.ShapeDtypeStruct((2, 2, D), F32), jax.ShapeDtypeStruct((1, D), F32),
                   jax.ShapeDtypeStruct((CONV_W, LRU_W), F32), jax.ShapeDtypeStruct((1, LRU_W), F32)],
        scratch_shapes=[pltpu.VMEM((N_DEV, D, N_MIX_SHARD), BF16), pltpu.VMEM((N_DEV, D, N_MIX_SHARD), F32)],
        args=(dy, h, p, p, p, dxf, dxf, dxf, dxb, dxb, dxb, dprest, mod, g, conv_w, wmix), rider=rider)


def loss_head(h, g, target, name):
    t = h.shape[0]
    nc = CTX // TM

    def body(h_ref, g_ref, t_ref, dh_ref, loss_ref, dg_ref):
        i = pl.program_id(0)

        @pl.when(i == 0)
        def _():
            loss_ref[...] = jnp.zeros_like(loss_ref)
            dg_ref[...] = jnp.zeros_like(dg_ref)

        @pl.when(i < nc)
        def _():
            dh_ref[...] = jnp.zeros_like(dh_ref)

        @pl.when(i >= nc)
        def _():
            hh = h_ref[...]
            gain = g_ref[...]
            r = lax.rsqrt(jnp.mean(hh * hh, axis=-1, keepdims=True) + EPS)
            n = hh * r
            err = n * gain - t_ref[...]
            loss_ref[...] += 0.5 * jnp.sum(jnp.mean(err * err, axis=-1, keepdims=True))
            dy = err * (1.0 / D)
            dg_ref[...] += jnp.sum(dy * n, axis=0, keepdims=True)
            dn = dy * gain
            dh_ref[...] = r * (dn - n * jnp.mean(dn * n, axis=-1, keepdims=True))

    row = pl.BlockSpec((TM, D), lambda i: (i, 0))
    return pl.pallas_call(
        body, name=name, grid=(t // TM,),
        in_specs=[row, pl.BlockSpec((1, D), lambda i: (0, 0)),
                  pl.BlockSpec((TM, D), lambda i: (jnp.maximum(i - nc, 0), 0))],
        out_specs=[row, pl.BlockSpec((8, 128), lambda i: (0, 0)), pl.BlockSpec((1, D), lambda i: (0, 0))],
        out_shape=[jax.ShapeDtypeStruct((t, D), F32), jax.ShapeDtypeStruct((8, 128), F32),
                   jax.ShapeDtypeStruct((1, D), F32)],
        compiler_params=_cp(),
    )(h, g, target)


def pad_out_shard(w):
    return jnp.concatenate([w, jnp.zeros(w.shape[:-2] + (HALF_PAD - HALF, w.shape[-1]), w.dtype)], axis=-2)


def _block_diag(w):
    eye = jnp.eye(HEADS, dtype=w.dtype)
    return jnp.einsum("dhij,hk->dhikj", w, eye).reshape(2, LRU_W, LRU_W)


def _block_diag_inv(full):
    f = full.reshape(2, HEADS, HEAD_DIM, HEADS, HEAD_DIM)
    return jnp.stack([f[:, hd, :, hd, :] for hd in range(HEADS)], axis=1)


def small_layer(g1, gm, g2, conv_w, conv_b, w_r, b_r, w_i, b_i, lam, sgu_g, sgu_w, sgu_b):
    return dict(g1=g1[None, :], gm=gm[None, :], g2=g2[None, :], conv_w=conv_w, conv_b=conv_b[None, :],
                wr=_block_diag(w_r).astype(BF16), br=b_r[:, None, :], wi=_block_diag(w_i).astype(BF16),
                bi=b_i[:, None, :], lam=lam[:, None, :], sgu_g=sgu_g[None, :], sgu_w=sgu_w, sgu_bt=sgu_b.T)


def small_grads(g):
    out = dict(mix_norm_g=g["gm"][0], ffn2_norm_g=g["g2"][0], lru_conv_w=g["conv_w"],
               lru_conv_b=g["conv_b"][0], lru_w_r=_block_diag_inv(g["wr"]), lru_b_r=g["br"][:, 0, :],
               lru_w_i=_block_diag_inv(g["wi"]), lru_b_i=g["bi"][:, 0, :], lru_lambda=g["lam"][:, 0, :],
               sgu_norm_g=g["sgu_g"][0], sgu_w=g["sgu_w"], sgu_b=g["sgu_bt"].T)
    if "g1" in g:
        out["ffn1_norm_g"] = g["g1"][0]
    return out


BIG_KEYS = ("win1", "wout1", "wmix", "womix", "win2", "wout2")


def _as_blocks(key, g):
    if key in ("wout1", "wout2"):
        return g.reshape(N_DEV, HALF_PAD, D)
    return g.reshape(N_DEV, OMIX_SHARD, D) if key == "womix" else g


def _gathered(key, a):
    if key in ("wout1", "wout2"):
        return a.reshape(HID_PAD, D)
    return a.reshape(D, D) if key == "womix" else a


class _ReduceScatter:
    def __init__(self, c_idx, where):
        self.c_idx, self.where = c_idx, where
        self.out = {}

    def pair(self, group):
        return pair_rider([g for _, g in group])

    def after_pair(self, group, recv1, tag):
        parts = [pair_sum(g, r, self.c_idx, f"pair_sum_{tag}_{i}") for i, ((_, g), r) in enumerate(zip(group, recv1))]
        return chips_rider(parts)

    def after_chips(self, group, recv1, recv2, tag):
        for i, ((key, g), r1, r2) in enumerate(zip(group, recv1, recv2)):
            self.out[key] = final_sum(g, r1, r2, self.where, f"final_sum_{tag}_{i}")


def fwd_bwd(ctx_rows, x_rows, target, mods, shards, smalls, final_g, c_idx, where):
    assert CTX == TM and len(shards) == 2

    def gather(keys_by_layer):
        return GatherRider([shards[l][k] for l, k in keys_by_layer])

    def put(full, keys_by_layer, got):
        for (l, k), a in zip(keys_by_layer, got):
            full[l][k] = _gathered(k, a)

    full = [dict(s) for s in smalls]
    first = [(0, "win1"), (0, "wout1")]
    put(full, first, run_alone(gather(first), pl.ANY, "gather_w0"))
    riders = {
        "ffn1_fwd_0": [(0, "wmix"), (0, "womix"), (0, "win2")],
        "lru_fwd_0_0": [(0, "wout2")],
        "ffn2_fwd_0": [(1, "win1"), (1, "wout1")],
        "ffn1_fwd_1": [(1, "wmix"), (1, "womix"), (1, "win2")],
        "lru_fwd_1_0": [(1, "wout2")],
    }

    def ffn(which, l, h):
        name = f"ffn{which}_fwd_{l}"
        w = full[l]
        keys = riders.get(name)
        m = mods[l][:, 0:3] if which == 1 else mods[l][:, 6:9]
        outs, got = ffn_fwd(h, m, w[f"g{which}"], w[f"win{which}"], w[f"wout{which}"], name,
                            rider=gather(keys) if keys else None)
        if keys:
            put(full, keys, got)
        return outs

    saved = []
    h = (ctx_rows, x_rows)
    for l in range(2):
        mm = mods[l][:, 3:6]
        outs = ffn(1, l, h)
        h1, gu1, acc1 = outs[:3]
        hin = outs[3] if l == 0 else h
        w = full[l]
        p = mix_in_fwd(h1, mm, w["gm"], w["wmix"], f"mix_in_fwd_{l}")
        hs = []
        for d in range(2):
            keys = riders.get(f"lru_fwd_{l}_{d}")
            hd, got = lru_fwd(p, w["conv_w"], w["conv_b"], w["wr"][d], w["br"][d], w["wi"][d], w["bi"][d], w["lam"][d],
                              bool(d), f"lru_fwd_{l}_{d}", rider=gather(keys) if keys else None)
            if keys:
                put(full, keys, got)
            hs.append(hd)
        h2, o = mix_out_fwd(h1, p, hs[0], hs[1], mm, w["sgu_g"], w["sgu_w"], w["sgu_bt"], w["womix"], f"mix_out_fwd_{l}")
        h3, gu2, acc2 = ffn(2, l, h2)
        saved.append((hin, h1, h2, gu1, acc1, p, hs, o, gu2, acc2))
        h = h3
    dh, loss, dgf = loss_head(h, final_g, target, "loss_head")

    rs = _ReduceScatter(c_idx, where)
    grads, dmods, sums = [None, None], [None, None], [None, None]
    pending = None
    for l in (1, 0):
        w = full[l]
        m1, mm, m2 = mods[l][:, 0:3], mods[l][:, 3:6], mods[l][:, 6:9]
        hin, h1, h2, gu1, acc1, p, hs, o, gu2, acc2 = saved[l]
        g = {}
        rs.out = {}
        both = Riders([rs.pair(pending[0]), GatherRider([small_pack])]) if pending else None
        (dp2, g["wout2"], dgate2), got = ffn_bwd_a(dh, acc2, gu2, m2, w["wout2"], f"ffn2_bwd_a_{l}", rider=both)
        if pending:
            r1, (small_all,) = both.split(got)
        chips = rs.after_pair(pending[0], r1, pending[1]) if pending else None
        (dh, g["win2"], dmod2, g["g2"]), r2 = ffn_bwd_b(dh, h2, dp2, m2, w["g2"], w["win2"], f"ffn2_bwd_b_{l}", rider=chips)
        if pending:
            rs.after_chips(pending[0], r1, r2, pending[1])
            sums[l + 1].update(rs.out)
            rs.out = {}

        grp = [(k, _as_blocks(k, g[k])) for k in ("win2", "wout2")]
        (dhs, dprest, g["womix"], dgatem, g["sgu_g"], g["sgu_w"], g["sgu_bt"]), r1 = mix_out_bwd(
            dh, p, hs[0], hs[1], o, mm, w["sgu_g"], w["sgu_w"], w["sgu_bt"], w["womix"], f"mix_out_bwd_{l}",
            rider=rs.pair(grp))
        chips = rs.after_pair(grp, r1, f"a{l}")
        dx, per_dir = [], []
        for d in range(2):
            out = lru_bwd(p, hs[d], dhs, w["conv_w"], w["conv_b"], w["wr"][d], w["br"][d], w["wi"][d], w["bi"][d],
                          w["lam"][d], bool(d), f"lru_bwd_{l}_{d}")
            dx.append(out[0])
            per_dir.append(out[1:])
        for k, nm in enumerate(("wr", "wi", "br", "bi", "lam")):
            g[nm] = jnp.stack([per_dir[0][k], per_dir[1][k]])
        (dh, g["wmix"], dmodm, g["gm"], g["conv_w"], g["conv_b"]), r2 = mix_in_bwd(
            dh, h1, p, dx[0], dx[1], dprest, mm, w["gm"], w["conv_w"], w["wmix"], f"mix_in_bwd_{l}", rider=chips)
        rs.after_chips(grp, r1, r2, f"a{l}")
        sums[l] = dict(rs.out)
        rs.out = {}

        if l == 1:
            (dp1, g["wout1"], dgate1), _ = ffn_bwd_a(dh, acc1, gu1, m1, w["wout1"], f"ffn1_bwd_a_{l}")
            (dh, g["win1"], dmod1, g["g1"]), _ = ffn_bwd_b(dh, hin, dp1, m1, w["g1"], w["win1"], f"ffn1_bwd_b_{l}")
            pending = ([(k, _as_blocks(k, g[k])) for k in ("womix", "wmix", "wout1", "win1")], f"b{l}")
            per = small_grads(g)
            small_pack = _pack([per[n] for n in LAYER_SMALL])
        else:
            g_mix = [(k, _as_blocks(k, g[k])) for k in ("womix", "wmix")]
            (dp1, g["wout1"], dgate1), r1_mix = ffn_bwd_a(dh, acc1, gu1, m1, w["wout1"], f"ffn1_bwd_a_{l}",
                                                          rider=rs.pair(g_mix))
            g_out = [("wout1", _as_blocks("wout1", g["wout1"]))]
            per = small_grads(g)
            three = Riders([rs.after_pair(g_mix, r1_mix, f"b{l}"), rs.pair(g_out),
                            GatherRider([_pack([per[n] for n in LAYER_SMALL[1:]])])])
            (g["win1"],), got = ffn_bwd_dw(hin, dp1, m1, w["g1"], f"ffn1_bwd_dw_{l}", rider=three)
            r2_mix, r1_out, (small0_all,) = three.split(got)
            rs.after_chips(g_mix, r1_mix, r2_mix, f"b{l}")
            g_in = [("win1", g["win1"])]
            both = Riders([rs.after_pair(g_out, r1_out, f"c{l}"), rs.pair(g_in)])
            nt = dh.shape[0] // TM
            cut0, cut1 = (3 * nt) // 8, nt - max(nt // 16, 1)
            part, got = ffn_bwd_dh(dh, hin, dp1, m1, w["g1"], w["win1"], f"ffn1_bwd_dh0_{l}", (0, cut0), rider=both)
            r2_out, r1_in = both.split(got)
            rs.after_chips(g_out, r1_out, r2_out, f"c{l}")
            part, r2_in = ffn_bwd_dh(dh, hin, dp1, m1, w["g1"], w["win1"], f"ffn1_bwd_dh1_{l}", (cut0, cut1), carry=part,
                                     rider=rs.after_pair(g_in, r1_in, f"d{l}"))
            rs.after_chips(g_in, r1_in, r2_in, f"d{l}")
            (dh, dmod1, g["g1"]), _ = ffn_bwd_dh(dh, hin, dp1, m1, w["g1"], w["win1"], f"ffn1_bwd_dh2_{l}", (cut1, nt),
                                                 carry=part)
            sums[l].update(rs.out)
        dmods[l] = jnp.concatenate([dmod1, dgate1, dmodm, dgatem, dmod2, dgate2], axis=1)
        grads[l] = g
    return loss, dh, jnp.stack(dmods), grads, (small0_all, small_all), sums, dgf


def _row_block(r, c, limit=262144):
    best = 8
    for rb in range(8, r + 1, 8):
        if r % rb == 0 and rb * c <= limit:
            best = rb
    return best


def pair_sum(grads, recv, c_idx, name):
    _, r, c = grads.shape
    rb = _row_block(r, c)

    def body(c_ref, g_ref, r_ref, o_ref):
        o_ref[...] = (g_ref[...] + r_ref[...]).astype(BF16)

    return pl.pallas_call(
        body, name=name,
        grid_spec=pltpu.PrefetchScalarGridSpec(
            num_scalar_prefetch=1, grid=(4, r // rb),
            in_specs=[pl.BlockSpec((1, rb, c), lambda j, i, c_ref: (2 * j + c_ref[0], i, 0)),
                      pl.BlockSpec((1, rb, c), lambda j, i, c_ref: (j, i, 0))],
            out_specs=pl.BlockSpec((1, rb, c), lambda j, i, c_ref: (j, i, 0))),
        out_shape=jax.ShapeDtypeStruct((4, r, c), BF16),
        compiler_params=_cp(2),
    )(c_idx, grads, recv)


def final_sum(grads, recv1, recv2, where, name):
    _, r, c = grads.shape
    rb = _row_block(r, c)

    def body(w_ref, g_ref, r1_ref, r2_ref, o_ref):
        far = (r2_ref[0].astype(F32) + r2_ref[1].astype(F32)) + r2_ref[2].astype(F32)
        o_ref[...] = (g_ref[0] + r1_ref[0]) + far

    return pl.pallas_call(
        body, name=name,
        grid_spec=pltpu.PrefetchScalarGridSpec(
            num_scalar_prefetch=1, grid=(r // rb,),
            in_specs=[pl.BlockSpec((1, rb, c), lambda i, w_ref: (w_ref[0], i, 0)),
                      pl.BlockSpec((1, rb, c), lambda i, w_ref: (w_ref[1], i, 0)),
                      pl.BlockSpec((3, rb, c), lambda i, w_ref: (0, i, 0))],
            out_specs=pl.BlockSpec((rb, c), lambda i, w_ref: (i, 0))),
        out_shape=jax.ShapeDtypeStruct((r, c), F32),
        compiler_params=_cp(1),
    )(where, grads, recv1, recv2)


ADA_ROWS = 16


def _silu(v):
    return v * _sigmoid(v)


def ada_fwd(cond, w_ada, b_slab, name):
    def body(c_ref, w_ref, b_ref, o_ref):
        s = _silu(c_ref[...]).astype(BF16)
        o_ref[0] = _dot(s, w_ref[0].astype(BF16)) + b_ref[0]

    return pl.pallas_call(
        body, name=name, grid=(DEPTH,),
        in_specs=[pl.BlockSpec((ADA_ROWS, D), lambda l: (0, 0)), pl.BlockSpec((1, D, ADA_SHARD), lambda l: (l, 0, 0)),
                  pl.BlockSpec((1, 1, ADA_SHARD), lambda l: (l, 0, 0))],
        out_specs=pl.BlockSpec((1, ADA_ROWS, ADA_SHARD), lambda l: (l, 0, 0)),
        out_shape=jax.ShapeDtypeStruct((DEPTH, ADA_ROWS, ADA_SHARD), F32),
        compiler_params=_cp(),
    )(cond, w_ada, b_slab)


def ada_bwd(cond, dm_sample, dm_ctx, w_ada, name):
    def body(c_ref, ds_ref, dc_ref, w_ref, gw_ref, dsc_ref):
        @pl.when(pl.program_id(0) == 0)
        def _():
            dsc_ref[...] = jnp.zeros_like(dsc_ref)

        s = _silu(c_ref[...]).astype(BF16)
        dcs = dc_ref[0]
        tot = dcs[0:1]
        for j in range(1, N_DEV):
            tot = tot + dcs[j:j + 1]
        tot8 = jnp.where(lax.broadcasted_iota(jnp.int32, (N_DEV, ADA_SHARD), 0) == 0, tot, 0.0)
        dm = jnp.concatenate([ds_ref[0], tot8], axis=0).astype(BF16)
        gw_ref[0] = _dot_tn(s, dm)
        dsc_ref[...] += _dot_nt(dm, w_ref[0].astype(BF16))[N_DEV:N_DEV + 1]

    slab = pl.BlockSpec((1, N_DEV, ADA_SHARD), lambda l: (l, 0, 0))
    wspec = pl.BlockSpec((1, D, ADA_SHARD), lambda l: (l, 0, 0))
    return pl.pallas_call(
        body, name=name, grid=(DEPTH,),
        in_specs=[pl.BlockSpec((ADA_ROWS, D), lambda l: (0, 0)), slab, slab, wspec],
        out_specs=[wspec, pl.BlockSpec((1, D), lambda l: (0, 0))],
        out_shape=[jax.ShapeDtypeStruct((DEPTH, D, ADA_SHARD), F32), jax.ShapeDtypeStruct((1, D), F32)],
        compiler_params=_cp(),
    )(cond, dm_sample, dm_ctx, w_ada)


def sum_over_devices(parts, name, silu_rows=0, w=None):
    _, r, c = parts.shape

    def body(*refs):
        p_ref, o_ref = refs[0], refs[-1]
        tot = p_ref[0]
        for j in range(1, N_DEV):
            tot = tot + p_ref[j]
        o_ref[...] = tot
        if silu_rows:
            wv = refs[1][...]
            s = _sigmoid(wv)
            o_ref[0:silu_rows, :] = tot[0:silu_rows, :] * (s * (1.0 + wv * (1.0 - s)))

    vm = pl.BlockSpec(memory_space=pltpu.VMEM)
    args = (parts,) if w is None else (parts, w)
    return pl.pallas_call(
        body, name=name, in_specs=[vm] * len(args), out_specs=vm,
        out_shape=jax.ShapeDtypeStruct((r, c), F32),
        compiler_params=pltpu.CompilerParams(vmem_limit_bytes=VMEM_LIMIT),
    )(*args)


def sum_dmods(dm_all, name):
    def body(d_ref, o_ref):
        for l in range(DEPTH):
            tot = d_ref[0, l]
            for j in range(1, N_DEV):
                tot = tot + d_ref[j, l]
            o_ref[l:l + 1, :] = tot[0:1] + tot[1:2]

    vm = pl.BlockSpec(memory_space=pltpu.VMEM)
    return pl.pallas_call(
        body, name=name, in_specs=[vm], out_specs=vm,
        out_shape=jax.ShapeDtypeStruct((DEPTH, N_MOD * D), F32),
    )(dm_all)


def adamw(w, g, m, v, name, rider=None):
    r, c = w.shape
    rb = _row_block(r, c, limit=131072)

    def body(w_ref, g_ref, m_ref, v_ref, d_ref, nm_ref, nv_ref):
        g_ = g_ref[...]
        nm = B1 * m_ref[...] + (1.0 - B1) * g_
        nv = B2 * v_ref[...] + (1.0 - B2) * (g_ * g_)
        nm_ref[...] = nm
        nv_ref[...] = nv
        m_hat = nm / (1.0 - B1 ** STEP)
        v_hat = nv / (1.0 - B2 ** STEP)
        d_ref[...] = -LR * (m_hat / (jnp.sqrt(v_hat) + ADAM_EPS) + WD * w_ref[...])

    blk = pl.BlockSpec((rb, c), lambda i: (i, 0))
    shp = jax.ShapeDtypeStruct((r, c), F32)
    return _grid_call(body, name=name, nsteps=r // rb, in_specs=[blk] * 4, out_specs=[blk] * 3, out_shape=[shp] * 3,
                      scratch_shapes=[], args=(w, g, m, v), rider=rider)


def prep_in_shards(w, name):
    rb = 256

    def body(w_ref, o_ref):
        pad = jnp.zeros((rb, HALF_PAD - HALF), BF16)
        o_ref[0, :, 0:HALF] = w_ref[0, :, 0:HALF].astype(BF16)
        o_ref[0, :, HALF:HALF_PAD] = pad
        o_ref[0, :, HALF_PAD:HALF_PAD + HALF] = w_ref[0, :, HALF:2 * HALF].astype(BF16)
        o_ref[0, :, HALF_PAD + HALF:SHARD_PAD] = pad

    return pl.pallas_call(
        body, name=name, grid=(DEPTH, D // rb),
        in_specs=[pl.BlockSpec((1, rb, SHARD_FF_IN), lambda l, i: (l, i, 0))],
        out_specs=pl.BlockSpec((1, rb, SHARD_PAD), lambda l, i: (l, i, 0)),
        out_shape=jax.ShapeDtypeStruct((DEPTH, D, SHARD_PAD), BF16),
        compiler_params=_cp(2),
    )(w)


def _adamw_math(w, g, m, v):
    nm = B1 * m + (1.0 - B1) * g
    nv = B2 * v + (1.0 - B2) * (g * g)
    m_hat = nm / (1.0 - B1 ** STEP)
    v_hat = nv / (1.0 - B2 ** STEP)
    return -LR * (m_hat / (jnp.sqrt(v_hat) + ADAM_EPS) + WD * w), nm, nv


def adamw_layers(w, g_layers, m, v, name):
    _, r, c = w.shape
    cg = g_layers[0].shape[1]
    halves = cg != c
    assert (c, cg) == (SHARD_FF_IN, SHARD_PAD) if halves else cg == c
    rb = _row_block(r, cg, limit=131072)
    nb = r // rb

    def body(w_ref, g0_ref, g1_ref, m_ref, v_ref, go_ref, d_ref, nm_ref, nv_ref):
        first = pl.program_id(0) == 0
        spans = [(slice(0, HALF), slice(0, HALF)), (slice(HALF, 2 * HALF), slice(HALF_PAD, HALF_PAD + HALF))] if halves \
            else [(slice(None), slice(None))]
        for dst, src in spans:
            g = jnp.where(first, g0_ref[:, src], g1_ref[:, src])
            d, nm, nv = _adamw_math(w_ref[0, :, dst], g, m_ref[0, :, dst], v_ref[0, :, dst])
            go_ref[0, :, dst] = g
            d_ref[0, :, dst] = d
            nm_ref[0, :, dst] = nm
            nv_ref[0, :, dst] = nv

    blk = pl.BlockSpec((1, rb, c), lambda l, i: (l, i, 0))
    g0_spec = pl.BlockSpec((rb, cg), lambda l, i: (jnp.where(l == 0, i, nb - 1), 0))
    g1_spec = pl.BlockSpec((rb, cg), lambda l, i: (jnp.where(l == 0, 0, i), 0))
    shp = jax.ShapeDtypeStruct(w.shape, F32)
    return pl.pallas_call(
        body, name=name, grid=(DEPTH, nb), in_specs=[blk, g0_spec, g1_spec, blk, blk], out_specs=[blk] * 4,
        out_shape=[shp] * 4, compiler_params=_cp(2),
    )(w, g_layers[0], g_layers[1], m, v)


def _adamw_nd(w, g, m, v, name, rider=None):
    shape = w.shape
    flat = lambda a: a.reshape(-1, shape[-1])
    outs, got = adamw(flat(w), flat(g), flat(m), flat(v), name, rider=rider)
    return tuple(o.reshape(shape) for o in outs), got


LANES = 128


PACK_UNIT = 8 * LANES


ADAMW_SMALL_ROWS = 512


def _pack(arrays, row_multiple=8):
    pieces, n = [], 0
    for a in arrays:
        pieces.append(a.reshape(-1).astype(F32))
        pad = (-a.size) % PACK_UNIT
        if pad:
            pieces.append(jnp.zeros((pad,), F32))
        n += a.size + pad
    tail = (-n) % (row_multiple * LANES)
    if tail:
        pieces.append(jnp.zeros((tail,), F32))
    return jnp.concatenate(pieces).reshape(-1, LANES)


def _unpack(packed, shapes):
    out, r0 = [], 0
    lead = packed.shape[:-2]
    for shp in shapes:
        size = 1
        for s in shp:
            size *= s
        nr = 8 * -(-size // PACK_UNIT)
        blk = packed[..., r0:r0 + nr, :].reshape(lead + (nr * LANES,))[..., :size]
        out.append(blk.reshape(lead + tuple(shp)))
        r0 += nr
    return out


WEIGHTS = ["c_ctx", "w_ada", "b_ada", "ffn1_norm_g", "ffn1_w_in", "ffn1_w_out", "mix_norm_g", "w_in_mix", "lru_conv_w",
           "lru_conv_b", "lru_w_r", "lru_b_r", "lru_w_i", "lru_b_i", "lru_lambda", "sgu_norm_g", "sgu_w", "sgu_b",
           "w_out_mix", "ffn2_norm_g", "ffn2_w_in", "ffn2_w_out", "final_norm_g"]
BIG = ["w_ada", "ffn1_w_in", "ffn1_w_out", "w_in_mix", "w_out_mix", "ffn2_w_in", "ffn2_w_out"]
SHARDED_SMALL = ["lru_conv_w", "lru_b_r", "lru_b_i", "lru_lambda"]
LAYER_SMALL = ["ffn1_norm_g", "mix_norm_g", "ffn2_norm_g", "lru_conv_w", "lru_conv_b", "lru_w_r", "lru_b_r", "lru_w_i",
               "lru_b_i", "lru_lambda", "sgu_norm_g", "sgu_w", "sgu_b"]
LRU_SHARD = LRU_W // N_DEV


def _widen(a):
    return jnp.moveaxis(a, 0, -2).reshape(a.shape[1:-1] + (LRU_W,))


def kernel(x, c, ctx, c_ctx, w_ada, b_ada, ffn1_norm_g, ffn1_w_in, ffn1_w_out, mix_norm_g, w_in_mix, lru_conv_w, lru_conv_b, lru_w_r, lru_b_r, lru_w_i, lru_b_i, lru_lambda, sgu_norm_g, sgu_w, sgu_b, w_out_mix, ffn2_norm_g, ffn2_w_in, ffn2_w_out, final_norm_g, loss_target, m_c_ctx, m_w_ada, m_b_ada, m_ffn1_norm_g, m_ffn1_w_in, m_ffn1_w_out, m_mix_norm_g, m_w_in_mix, m_lru_conv_w, m_lru_conv_b, m_lru_w_r, m_lru_b_r, m_lru_w_i, m_lru_b_i, m_lru_lambda, m_sgu_norm_g, m_sgu_w, m_sgu_b, m_w_out_mix, m_ffn2_norm_g, m_ffn2_w_in, m_ffn2_w_out, m_final_norm_g, v_c_ctx, v_w_ada, v_b_ada, v_ffn1_norm_g, v_ffn1_w_in, v_ffn1_w_out, v_mix_norm_g, v_w_in_mix, v_lru_conv_w, v_lru_conv_b, v_lru_w_r, v_lru_b_r, v_lru_w_i, v_lru_b_i, v_lru_lambda, v_sgu_norm_g, v_sgu_w, v_sgu_b, v_w_out_mix, v_ffn2_norm_g, v_ffn2_w_in, v_ffn2_w_out, v_final_norm_g):
    given = dict(locals())
    W = {n: given[n] for n in WEIGHTS}
    M = {n: given["m_" + n] for n in WEIGHTS}
    V = {n: given["v_" + n] for n in WEIGHTS}
    xi, yi, ci = _position()
    me = 4 * xi + 2 * yi + ci
    chip = 2 * xi + yi

    sharded_shapes = [W[n].shape for n in SHARDED_SMALL]
    got = run_alone(GatherRider([_pack([c[0]] + [W[n] for n in SHARDED_SMALL])]), pltpu.VMEM, "gather_small")[0]
    parts = _unpack(got, [(D,)] + sharded_shapes)
    c_all = parts[0]
    wide = {n: _widen(a) for n, a in zip(SHARDED_SMALL, parts[1:])}
    cond = jnp.concatenate([c_all, c_ctx[None, :], jnp.zeros((ADA_ROWS - N_DEV - 1, D), F32)], axis=0)
    b_slab = lax.dynamic_slice_in_dim(b_ada, me * ADA_SHARD, ADA_SHARD, axis=1)[:, None, :]
    slabs = ada_fwd(cond, w_ada, b_slab, "ada_fwd")
    mall = run_alone(GatherRider([slabs.reshape(DEPTH * ADA_ROWS, ADA_SHARD)]), pltpu.VMEM, "gather_mod")[0]
    mall = mall.reshape(N_DEV, DEPTH, ADA_ROWS, ADA_SHARD)
    m_sample = lax.dynamic_index_in_dim(mall, me, axis=2, keepdims=False)
    m_ctx = mall[:, :, N_DEV, :]
    mods = jnp.stack([jnp.transpose(m, (1, 0, 2)).reshape(DEPTH, N_MOD, D) for m in (m_ctx, m_sample)], axis=1)

    shards, smalls = [], []
    win1_b, win2_b = prep_in_shards(ffn1_w_in, "prep_ffn1_w_in"), prep_in_shards(ffn2_w_in, "prep_ffn2_w_in")
    for l in range(DEPTH):
        sh = dict(wout1=pad_out_shard(ffn1_w_out[l]), wmix=w_in_mix[l], womix=w_out_mix[l], wout2=pad_out_shard(ffn2_w_out[l]))
        shards.append(dict({k: a.astype(BF16) for k, a in sh.items()}, win1=win1_b[l], win2=win2_b[l]))
        smalls.append(small_layer(ffn1_norm_g[l], mix_norm_g[l], ffn2_norm_g[l], wide["lru_conv_w"][l], lru_conv_b[l],
                                  lru_w_r[l], wide["lru_b_r"][l], lru_w_i[l], wide["lru_b_i"][l], wide["lru_lambda"][l],
                                  sgu_norm_g[l], sgu_w[l], sgu_b[l]))

    c_idx = ci.reshape(1).astype(jnp.int32)
    where = jnp.stack([me, chip]).astype(jnp.int32)
    loss_blk, dx, dmods, grads, (small0_all, small1_all), gsum, dgf = fwd_bwd(
        ctx[0], x[0], loss_target[0], mods, shards, smalls, final_norm_g[None, :], c_idx, where)
    smalls_shape = {n: (W[n].shape[1:-1] + (LRU_W,)) if n in SHARDED_SMALL else W[n].shape[1:] for n in LAYER_SMALL}
    G, delta, new_m, new_v = {}, {}, {}, {}
    for key, n in (("win1", "ffn1_w_in"), ("wout1", "ffn1_w_out"), ("wmix", "w_in_mix"), ("womix", "w_out_mix"),
                   ("win2", "ffn2_w_in"), ("wout2", "ffn2_w_out")):
        G[n], delta[n], new_m[n], new_v[n] = adamw_layers(W[n], [gsum[l][key] for l in range(DEPTH)], M[n], V[n],
                                                          f"adamw_{n}")

    n_rows = DEPTH * 2 * N_MOD
    dm_rows = jnp.concatenate([dmods.reshape(n_rows, D), jnp.zeros((-n_rows % 8, D), F32)], axis=0)
    dm_all = run_alone(GatherRider([dm_rows]), pltpu.VMEM, "gather_dmod")[0][:, :n_rows]
    dm_all = dm_all.reshape(N_DEV, DEPTH, 2, N_MOD * D)
    mine = lax.dynamic_slice_in_dim(dm_all, me * ADA_SHARD, ADA_SHARD, axis=3)
    G["w_ada"], dsc = ada_bwd(cond, jnp.transpose(mine[:, :, 1, :], (1, 0, 2)), jnp.transpose(mine[:, :, 0, :], (1, 0, 2)),
                              w_ada, "ada_bwd")
    G["b_ada"] = sum_dmods(dm_all, "sum_dmods")
    (delta["w_ada"], new_m["w_ada"], new_v["w_ada"]), _ = _adamw_nd(w_ada, G["w_ada"], m_w_ada, v_w_ada, "adamw_w_ada")

    head_all = run_alone(GatherRider([_pack([dsc[0], dgf[0], grads[0]["g1"][0]])]), pltpu.VMEM, "gather_head_grads")[0]
    head = _unpack(sum_over_devices(head_all, "sum_head_grads", silu_rows=D // LANES, w=c_ctx.reshape(D // LANES, LANES)),
                   [(D,), (D,), (D,)])
    shapes = [smalls_shape[n] for n in LAYER_SMALL]
    sum0 = [head[2]] + _unpack(sum_over_devices(small0_all, "sum_small_grads_0"), shapes[1:])
    sum1 = _unpack(sum_over_devices(small1_all, "sum_small_grads_1"), shapes)
    G["c_ctx"], G["final_norm_g"] = head[0], head[1]
    for n, a0, a1 in zip(LAYER_SMALL, sum0, sum1):
        a = jnp.stack([a0, a1])
        G[n] = lax.dynamic_slice_in_dim(a, me * LRU_SHARD, LRU_SHARD, axis=a.ndim - 1) if n in SHARDED_SMALL else a

    rest = [n for n in WEIGHTS if n not in BIG]
    shapes = [W[n].shape for n in rest]
    outs, _ = adamw(*[_pack([src[n] for n in rest], row_multiple=ADAMW_SMALL_ROWS) for src in (W, G, M, V)], "adamw_small")
    for dst, packed in zip((delta, new_m, new_v), outs):
        for n, a in zip(rest, _unpack(packed, shapes)):
            dst[n] = a

    loss = lax.psum(loss_blk[0, 0], ("x", "y", "c"))
    grad_x = dx[None]
    return (loss, grad_x, *[G[n] for n in WEIGHTS], *[delta[n] for n in WEIGHTS], *[new_m[n] for n in WEIGHTS],
            *[new_v[n] for n in WEIGHTS])
```

```python
import functools

import jax
import jax.numpy as jnp
from jax import lax
from jax.experimental import pallas as pl
from jax.experimental.pallas import tpu as pltpu

F32 = jnp.float32
BF16 = jnp.bfloat16

D = 1024
CTX = 256
DEPTH = 2
EPS = 1e-6
D_FF = 2816
LRU_W = 512
HEADS = 8
HEAD_DIM = 64
CONV_W = 4
RG_C = 8.0
GROUPS = 4
GROUP_DIM = 128
CHUNK = 128
MLP_W = 512
IN_PROJ = 2048
N_MOD = 9
N_DEV = 8

LR = 0.001
B1 = 0.9
B2 = 0.999
ADAM_EPS = 1e-08
WD = 0.01
STEP = 10

FF_IN_SHARD = 2 * D_FF // N_DEV
FF_OUT_SHARD = D_FF // N_DEV
HT = 256
WT = 512
N_MIX_SHARD = IN_PROJ // N_DEV
OMIX_SHARD = D // N_DEV
ADA_SHARD = N_MOD * D // N_DEV

TM = 256
HALO = 8
VMEM_LIMIT = 60 * 1024 * 1024

MESH = pl.DeviceIdType.MESH
ANY = pl.BlockSpec(memory_space=pl.ANY)


def _cp(n_axes=1):
    return pltpu.CompilerParams(dimension_semantics=("arbitrary",) * n_axes, vmem_limit_bytes=VMEM_LIMIT)


def _position():
    return lax.axis_index("x"), lax.axis_index("y"), lax.axis_index("c")


class GatherRider:
    def __init__(self, shards):
        n = len(shards)
        self.n = n
        self.ins = list(shards)
        self.out_shape = [jax.ShapeDtypeStruct((N_DEV,) + s.shape, s.dtype) for s in shards]
        self.sems = [pltpu.SemaphoreType.DMA((n, 7)), pltpu.SemaphoreType.DMA((n, 7)), pltpu.SemaphoreType.DMA((n,))]

    def _ctx(self, outs, sems):
        x, y, c = _position()
        chips = [(1 - x, y), (x, 1 - y), (1 - x, 1 - y)]

        def copy(t, k, block, to, src=None):
            dst = outs[t].at[4 * block[0] + 2 * block[1] + block[2]]
            return pltpu.make_async_remote_copy(
                src_ref=dst if src is None else src, dst_ref=dst, send_sem=sems[0].at[t, k],
                recv_sem=sems[1].at[t, k], device_id=to, device_id_type=MESH)

        return (x, y, c), (x, y, 1 - c), chips, copy

    def _local(self, ins, outs, sems, t):
        x, y, c = _position()
        return pltpu.make_async_copy(ins[t], outs[t].at[4 * x + 2 * y + c], sems[2].at[t])

    def _first(self, ins, outs, sems, t):
        me, sibling, chips, copy = self._ctx(outs, sems)
        return [copy(t, 0, me, sibling, src=ins[t])] + [copy(t, 1 + j, me, (*chip, me[2]), src=ins[t])
                                                         for j, chip in enumerate(chips)]

    def start(self, ins, outs, sems):
        for t in range(self.n):
            self._local(ins, outs, sems, t).start()
            for cp in self._first(ins, outs, sems, t):
                cp.start()

    def mid(self, ins, outs, sems):
        me, sibling, chips, copy = self._ctx(outs, sems)
        for j, chip in enumerate(chips):
            for t in range(self.n):
                copy(t, 1 + j, (*chip, me[2]), me).wait_recv()
                copy(t, 4 + j, (*chip, me[2]), sibling).start()

    def finish(self, ins, outs, sems):
        me, sibling, chips, copy = self._ctx(outs, sems)
        for t in range(self.n):
            copy(t, 0, sibling, me).wait_recv()
            for j, chip in enumerate(chips):
                copy(t, 4 + j, (*chip, 1 - me[2]), me).wait_recv()
        for t in range(self.n):
            for cp in self._first(ins, outs, sems, t):
                cp.wait_send()
            for j, chip in enumerate(chips):
                copy(t, 4 + j, (*chip, me[2]), sibling).wait_send()
            self._local(ins, outs, sems, t).wait()


class ExchangeRider:
    def __init__(self, tensors, plan, n_slots):
        n = len(tensors)
        self.n, self.plan = n, plan
        self.ins = list(tensors)
        self.out_shape = [jax.ShapeDtypeStruct((n_slots,) + s.shape[1:], s.dtype) for s in tensors]
        self.sems = [pltpu.SemaphoreType.DMA((n, n_slots)), pltpu.SemaphoreType.DMA((n, n_slots))]

    def _copies(self, ins, outs, sems):
        return [pltpu.make_async_remote_copy(
            src_ref=ins[t].at[block], dst_ref=outs[t].at[k], send_sem=sems[0].at[t, k], recv_sem=sems[1].at[t, k],
            device_id=to, device_id_type=MESH)
            for t in range(self.n) for k, (block, to) in enumerate(self.plan(*_position()))]

    def start(self, ins, outs, sems):
        for cp in self._copies(ins, outs, sems):
            cp.start()

    def mid(self, ins, outs, sems):
        pass

    def finish(self, ins, outs, sems):
        for cp in self._copies(ins, outs, sems):
            cp.wait()


class Riders:
    def __init__(self, riders):
        self.riders = list(riders)
        self.ins = [a for r in self.riders for a in r.ins]
        self.out_shape = [s for r in self.riders for s in r.out_shape]
        self.sems = [s for r in self.riders for s in r.sems]

    def _each(self, ins, outs, sems):
        i = o = s = 0
        for r in self.riders:
            ni, no, ns = len(r.ins), len(r.out_shape), len(r.sems)
            yield r, ins[i:i + ni], outs[o:o + no], sems[s:s + ns]
            i, o, s = i + ni, o + no, s + ns

    def start(self, ins, outs, sems):
        for r, a, b, c in self._each(ins, outs, sems):
            r.start(a, b, c)

    def mid(self, ins, outs, sems):
        for r, a, b, c in self._each(ins, outs, sems):
            r.mid(a, b, c)

    def finish(self, ins, outs, sems):
        for r, a, b, c in self._each(ins, outs, sems):
            r.finish(a, b, c)

    def split(self, outs):
        res, o = [], 0
        for r in self.riders:
            res.append(list(outs[o:o + len(r.out_shape)]))
            o += len(r.out_shape)
        return res


def pair_rider(grads):
    def plan(x, y, c):
        return [(4 * cx + 2 * cy + (1 - c), (x, y, 1 - c)) for cx in range(2) for cy in range(2)]
    return ExchangeRider(grads, plan, 4)


def chips_rider(parts):
    def plan(x, y, c):
        return [(2 * cx + cy, (cx, cy, c)) for cx, cy in [(1 - x, y), (x, 1 - y), (1 - x, 1 - y)]]
    return ExchangeRider(parts, plan, 3)


def run_alone(rider, space, name):
    ni = len(rider.ins)
    no = len(rider.out_shape)

    def body(*refs):
        ins, outs, sems = refs[:ni], refs[ni:ni + no], refs[ni + no:]
        rider.start(ins, outs, sems)
        rider.mid(ins, outs, sems)
        rider.finish(ins, outs, sems)

    spec = pl.BlockSpec(memory_space=space)
    return pl.pallas_call(
        body, name=name, in_specs=[spec] * ni, out_specs=[spec] * no, out_shape=rider.out_shape,
        scratch_shapes=rider.sems, compiler_params=pltpu.CompilerParams(vmem_limit_bytes=VMEM_LIMIT),
    )(*rider.ins)


def _grid_call(body, *, name, nsteps, in_specs, out_specs, out_shape, scratch_shapes, args, rider=None, aliases=None):
    aliases = aliases or {}
    if rider is None:
        outs = pl.pallas_call(body, name=name, grid=(nsteps,), in_specs=in_specs, out_specs=out_specs,
                              out_shape=out_shape, scratch_shapes=scratch_shapes, input_output_aliases=aliases,
                              compiler_params=_cp())(*args)
        return outs, []
    ni, no, ns = len(in_specs), len(out_specs), len(scratch_shapes)
    ri, ro = len(rider.ins), len(rider.out_shape)

    def wrapped(*refs):
        ins, refs = refs[:ni], refs[ni:]
        r_ins, refs = refs[:ri], refs[ri:]
        outs, refs = refs[:no], refs[no:]
        r_outs, refs = refs[:ro], refs[ro:]
        scratch, r_sems = refs[:ns], refs[ns:]
        s = pl.program_id(0)

        @pl.when(s == 0)
        def _():
            rider.start(r_ins, r_outs, r_sems)

        body(*ins, *outs, *scratch)

        @pl.when(s == (3 * nsteps) // 4)
        def _():
            rider.mid(r_ins, r_outs, r_sems)

        @pl.when(s == nsteps - 1)
        def _():
            rider.finish(r_ins, r_outs, r_sems)

    outs = pl.pallas_call(
        wrapped, name=name, grid=(nsteps,), in_specs=list(in_specs) + [ANY] * ri, out_specs=list(out_specs) + [ANY] * ro,
        out_shape=list(out_shape) + rider.out_shape, scratch_shapes=list(scratch_shapes) + rider.sems,
        input_output_aliases=aliases, compiler_params=_cp())(*args, *rider.ins)
    return outs[:no], outs[no:]


def _dot(a, b):
    return jnp.dot(a, b, preferred_element_type=F32)


def _dot_nt(a, b):
    return lax.dot_general(a, b, (((1,), (1,)), ((), ())), preferred_element_type=F32)


def _dot_tn(a, b):
    return lax.dot_general(a, b, (((0,), (0,)), ((), ())), preferred_element_type=F32)


def _sigmoid(x):
    return 1.0 / (1.0 + jnp.exp(-x))


def _kind(i):
    return jnp.where(i < CTX // TM, 0, 1)


def _sel(kind, mod_ref, k):
    return mod_ref[kind, k:k + 1, :]


def _acc2(ref, k, val, kind):
    ref[kind, k:k + 1, :] += jnp.sum(val, axis=0, keepdims=True)


def _norm_mod(h, g, shift, scale):
    r = lax.rsqrt(jnp.mean(h * h, axis=-1, keepdims=True) + EPS)
    n = h * r
    return (n * g) * (1.0 + scale) + shift, n, r


def _norm_mod_bwd(dz, n, r, g, scale):
    dn = dz * (g * (1.0 + scale))
    return r * (dn - n * jnp.mean(dn * n, axis=-1, keepdims=True))


def ffn_fwd(h, mod, g, win, wout, name, rider=None):
    split = isinstance(h, tuple)
    nc = CTX // TM
    t = h[0].shape[0] + h[1].shape[0] if split else h.shape[0]

    def body(*refs):
        if split:
            c_ref, x_ref, mod_ref, g_ref, win_hbm, wout_hbm, out_ref, gu_ref, acc_ref, h0_ref, win_v, wout_v, a_v = refs
        else:
            h_ref, mod_ref, g_ref, win_hbm, wout_hbm, out_ref, gu_ref, acc_ref, win_v, wout_v, a_v = refs
        i = pl.program_id(0)

        @pl.when(i == 0)
        def _():
            pltpu.sync_copy(win_hbm, win_v)
            pltpu.sync_copy(wout_hbm, wout_v)

        if split:
            hh = jnp.where(i < nc, c_ref[...], x_ref[...])
            h0_ref[...] = hh
        else:
            hh = h_ref[...]
        ic = _kind(i)
        z, _, _ = _norm_mod(hh, g_ref[...], _sel(ic, mod_ref, 0), _sel(ic, mod_ref, 1))
        zb = z.astype(BF16)
        for j in range(D_FF // HT):
            gb, ub = slice(j * HT, (j + 1) * HT), slice(D_FF + j * HT, D_FF + (j + 1) * HT)
            gg = _dot_nt(zb, win_v[gb, :])
            uu = _dot_nt(zb, win_v[ub, :])
            gu_ref[:, gb] = gg.astype(BF16)
            gu_ref[:, ub] = uu.astype(BF16)
            a_v[:, gb] = ((gg * _sigmoid(gg)) * uu).astype(BF16)
        acc = _dot(a_v[...], wout_v[...])
        acc_ref[...] = acc
        out_ref[...] = hh + (0.5 * _sel(ic, mod_ref, 2)) * acc

    row = pl.BlockSpec((TM, D), lambda i: (i, 0))
    rshape = jax.ShapeDtypeStruct((t, D), F32)
    if split:
        rows_in = [pl.BlockSpec((TM, D), lambda i: (jnp.minimum(i, nc - 1), 0)),
                   pl.BlockSpec((TM, D), lambda i: (jnp.maximum(i - nc, 0), 0))]
    else:
        rows_in = [row]
    return _grid_call(
        body, name=name, nsteps=t // TM,
        in_specs=rows_in + [pl.BlockSpec((2, 3, D), lambda i: (0, 0, 0)), pl.BlockSpec((1, D), lambda i: (0, 0)), ANY, ANY],
        out_specs=[row, pl.BlockSpec((TM, 2 * D_FF), lambda i: (i, 0)), row] + ([row] if split else []),
        out_shape=[rshape, jax.ShapeDtypeStruct((t, 2 * D_FF), BF16), rshape] + ([rshape] if split else []),
        scratch_shapes=[pltpu.VMEM((2 * D_FF, D), BF16), pltpu.VMEM((D_FF, D), BF16), pltpu.VMEM((TM, D_FF), BF16)],
        args=(*(h if split else (h,)), mod, g, win, wout), rider=rider)


def ffn_bwd_a(dy, acc, gu, mod, wout, name, rider=None):
    t = dy.shape[0]
    nt = t // TM

    def body(dy_ref, acc_ref, gu_ref, mod_ref, wout_hbm, dp_ref, dwout_hbm, dgate_ref, wout_v, dwout_v):
        i = pl.program_id(0)

        @pl.when(i == 0)
        def _():
            pltpu.sync_copy(wout_hbm, wout_v)
            dwout_v[...] = jnp.zeros_like(dwout_v)
            dgate_ref[...] = jnp.zeros_like(dgate_ref)

        dy_ = dy_ref[...]
        ic = _kind(i)
        _acc2(dgate_ref, 0, 0.5 * dy_ * acc_ref[...], ic)
        daccb = ((0.5 * _sel(ic, mod_ref, 2)) * dy_).astype(BF16)
        for j in range(D_FF // HT):
            blk, ublk = slice(j * HT, (j + 1) * HT), slice(D_FF + j * HT, D_FF + (j + 1) * HT)
            da = _dot_nt(daccb, wout_v[blk, :])
            gg = gu_ref[:, blk].astype(F32)
            uu = gu_ref[:, ublk].astype(F32)
            s = _sigmoid(gg)
            sl = gg * s
            dwout_v[blk, :] += _dot_tn((sl * uu).astype(BF16), daccb)
            dp_ref[:, blk] = (da * uu * (s + sl * (1.0 - s))).astype(BF16)
            dp_ref[:, ublk] = (da * sl).astype(BF16)

        @pl.when(i == nt - 1)
        def _():
            pltpu.sync_copy(dwout_v, dwout_hbm)

    row = pl.BlockSpec((TM, D), lambda i: (i, 0))
    wide = pl.BlockSpec((TM, 2 * D_FF), lambda i: (i, 0))
    return _grid_call(
        body, name=name, nsteps=nt,
        in_specs=[row, row, wide, pl.BlockSpec((2, 3, D), lambda i: (0, 0, 0)), ANY],
        out_specs=[wide, ANY, pl.BlockSpec((2, 1, D), lambda i: (0, 0, 0))],
        out_shape=[jax.ShapeDtypeStruct((t, 2 * D_FF), BF16), jax.ShapeDtypeStruct((D_FF, D), F32),
                   jax.ShapeDtypeStruct((2, 1, D), F32)],
        scratch_shapes=[pltpu.VMEM((D_FF, D), BF16), pltpu.VMEM((D_FF, D), F32)],
        args=(dy, acc, gu, mod, wout), rider=rider)


def ffn_bwd_b(dy, h, dp, mod, g, win, name, rider=None, latent_only=False):
    t = dy.shape[0]
    nt = t // TM
    nc = CTX // TM

    def body(dy_ref, h_ref, dp_ref, mod_ref, g_ref, win_hbm, dh_ref, dwin_hbm, dmod_ref, dg_ref, win_v, dwin_v):
        i = pl.program_id(0)

        @pl.when(i == 0)
        def _():
            pltpu.sync_copy(win_hbm, win_v)
            dwin_v[...] = jnp.zeros_like(dwin_v)
            dmod_ref[...] = jnp.zeros_like(dmod_ref)
            dg_ref[...] = jnp.zeros_like(dg_ref)

        ic = _kind(i)
        gain = g_ref[...]
        scale = _sel(ic, mod_ref, 1)
        z, n, r = _norm_mod(h_ref[...], gain, _sel(ic, mod_ref, 0), scale)
        zb = z.astype(BF16)
        dz = _dot(dp_ref[...], win_v[...])
        for j in range(2 * D_FF // WT):
            blk = slice(j * WT, (j + 1) * WT)
            dwin_v[blk, :] += _dot_tn(dp_ref[:, blk], zb)
        _acc2(dmod_ref, 0, dz, ic)
        _acc2(dmod_ref, 1, dz * (n * gain), ic)
        dg_ref[...] += jnp.sum(dz * (1.0 + scale) * n, axis=0, keepdims=True)
        dh_ref[...] = dy_ref[...] + _norm_mod_bwd(dz, n, r, gain, scale)

        @pl.when(i == nt - 1)
        def _():
            pltpu.sync_copy(dwin_v, dwin_hbm)

    row = pl.BlockSpec((TM, D), lambda i: (i, 0))
    if latent_only:
        dh_spec = pl.BlockSpec((TM, D), lambda i: (jnp.maximum(i - nc, 0), 0))
        dh_shape = jax.ShapeDtypeStruct((t - CTX, D), F32)
    else:
        dh_spec, dh_shape = row, jax.ShapeDtypeStruct((t, D), F32)
    return _grid_call(
        body, name=name, nsteps=nt,
        in_specs=[row, row, pl.BlockSpec((TM, 2 * D_FF), lambda i: (i, 0)),
                  pl.BlockSpec((2, 3, D), lambda i: (0, 0, 0)), pl.BlockSpec((1, D), lambda i: (0, 0)), ANY],
        out_specs=[dh_spec, ANY, pl.BlockSpec((2, 2, D), lambda i: (0, 0, 0)), pl.BlockSpec((1, D), lambda i: (0, 0))],
        out_shape=[dh_shape, jax.ShapeDtypeStruct((2 * D_FF, D), F32),
                   jax.ShapeDtypeStruct((2, 2, D), F32), jax.ShapeDtypeStruct((1, D), F32)],
        scratch_shapes=[pltpu.VMEM((2 * D_FF, D), BF16), pltpu.VMEM((2 * D_FF, D), F32)],
        args=(dy, h, dp, mod, g, win), rider=rider)


def ffn_bwd_dw(h, dp, mod, g, name, rider=None):
    t = h.shape[0]
    nt = t // TM

    def body(h_ref, dp_ref, mod_ref, g_ref, dwin_hbm, dwin_v):
        i = pl.program_id(0)

        @pl.when(i == 0)
        def _():
            dwin_v[...] = jnp.zeros_like(dwin_v)

        ic = _kind(i)
        z, _, _ = _norm_mod(h_ref[...], g_ref[...], _sel(ic, mod_ref, 0), _sel(ic, mod_ref, 1))
        zb = z.astype(BF16)
        for j in range(2 * D_FF // WT):
            blk = slice(j * WT, (j + 1) * WT)
            dwin_v[blk, :] += _dot_tn(dp_ref[:, blk], zb)

        @pl.when(i == nt - 1)
        def _():
            pltpu.sync_copy(dwin_v, dwin_hbm)

    return _grid_call(
        body, name=name, nsteps=nt,
        in_specs=[pl.BlockSpec((TM, D), lambda i: (i, 0)), pl.BlockSpec((TM, 2 * D_FF), lambda i: (i, 0)),
                  pl.BlockSpec((2, 3, D), lambda i: (0, 0, 0)), pl.BlockSpec((1, D), lambda i: (0, 0))],
        out_specs=[ANY], out_shape=[jax.ShapeDtypeStruct((2 * D_FF, D), F32)],
        scratch_shapes=[pltpu.VMEM((2 * D_FF, D), F32)],
        args=(h, dp, mod, g), rider=rider)


def ffn_bwd_dh(dy, h, dp, mod, g, win, name, tiles, carry=None, rider=None):
    t = dy.shape[0]
    nc = CTX // TM
    t0, t1 = tiles

    def body(*refs):
        if carry is None:
            dy_ref, h_ref, dp_ref, mod_ref, g_ref, win_hbm, dh_ref, dmod_ref, dg_ref, win_v = refs
        else:
            dy_ref, h_ref, dp_ref, mod_ref, g_ref, win_hbm, _, dmod0_ref, dg0_ref, dh_ref, dmod_ref, dg_ref, win_v = refs
        i = pl.program_id(0)

        @pl.when(i == 0)
        def _():
            pltpu.sync_copy(win_hbm, win_v)
            dmod_ref[...] = jnp.zeros_like(dmod_ref) if carry is None else dmod0_ref[...]
            dg_ref[...] = jnp.zeros_like(dg_ref) if carry is None else dg0_ref[...]

        ic = _kind(i + t0)
        gain = g_ref[...]
        scale = _sel(ic, mod_ref, 1)
        _, n, r = _norm_mod(h_ref[...], gain, _sel(ic, mod_ref, 0), scale)
        dz = _dot(dp_ref[...], win_v[...])
        _acc2(dmod_ref, 0, dz, ic)
        _acc2(dmod_ref, 1, dz * (n * gain), ic)
        dg_ref[...] += jnp.sum(dz * (1.0 + scale) * n, axis=0, keepdims=True)
        dh_ref[...] = dy_ref[...] + _norm_mod_bwd(dz, n, r, gain, scale)

    row = pl.BlockSpec((TM, D), lambda i: (i + t0, 0))
    small = [pl.BlockSpec((2, 2, D), lambda i: (0, 0, 0)), pl.BlockSpec((1, D), lambda i: (0, 0))]
    in_specs = [row, row, pl.BlockSpec((TM, 2 * D_FF), lambda i: (i + t0, 0)),
                pl.BlockSpec((2, 3, D), lambda i: (0, 0, 0)), pl.BlockSpec((1, D), lambda i: (0, 0)), ANY]
    args = (dy, h, dp, mod, g, win)
    if carry is not None:
        in_specs += [ANY] + small
        args += tuple(carry)
    return _grid_call(
        body, name=name, nsteps=t1 - t0, in_specs=in_specs,
        out_specs=[pl.BlockSpec((TM, D), lambda i: (jnp.maximum(i + t0 - nc, 0), 0))] + small,
        out_shape=[jax.ShapeDtypeStruct((t - CTX, D), F32), jax.ShapeDtypeStruct((2, 2, D), F32),
                   jax.ShapeDtypeStruct((1, D), F32)],
        scratch_shapes=[pltpu.VMEM((2 * D_FF, D), BF16)],
        args=args, rider=rider, aliases=None if carry is None else {6: 0})


def mix_in_fwd(h, mod, g, wmix, name):
    t = h.shape[0]

    def body(h_ref, mod_ref, g_ref, w_hbm, p_ref, w_v):
        i = pl.program_id(0)

        @pl.when(i == 0)
        def _():
            pltpu.sync_copy(w_hbm, w_v)

        ic = _kind(i)
        z, _, _ = _norm_mod(h_ref[...], g_ref[...], _sel(ic, mod_ref, 0), _sel(ic, mod_ref, 1))
        zb = z.astype(BF16)
        for dd in range(N_DEV):
            p_ref[:, dd * N_MIX_SHARD:(dd + 1) * N_MIX_SHARD] = _dot(zb, w_v[dd])

    return pl.pallas_call(
        body, name=name, grid=(t // TM,),
        in_specs=[pl.BlockSpec((TM, D), lambda i: (i, 0)), pl.BlockSpec((2, 3, D), lambda i: (0, 0, 0)),
                  pl.BlockSpec((1, D), lambda i: (0, 0)), ANY],
        out_specs=pl.BlockSpec((TM, IN_PROJ), lambda i: (i, 0)),
        out_shape=jax.ShapeDtypeStruct((t, IN_PROJ), F32),
        scratch_shapes=[pltpu.VMEM((N_DEV, D, N_MIX_SHARD), BF16)],
        compiler_params=_cp(),
    )(h, mod, g, wmix)


def _halo_specs(nt, tile_of):
    nb = nt * (TM // HALO)
    main = pl.BlockSpec((TM, LRU_W), lambda s: (tile_of(s), 0))
    prev = pl.BlockSpec((HALO, LRU_W), lambda s: (jnp.maximum(tile_of(s) * (TM // HALO) - 1, 0), 0))
    nxt = pl.BlockSpec((HALO, LRU_W), lambda s: (jnp.minimum((tile_of(s) + 1) * (TM // HALO), nb - 1), 0))
    return main, prev, nxt


def _ext(tile, nt, main, prev, nxt):
    has_prev = jnp.logical_and(tile != 0, tile != 1)
    has_next = jnp.logical_and(tile != 0, tile != nt - 1)
    return jnp.concatenate([jnp.where(has_prev, prev, 0.0), main, jnp.where(has_next, nxt, 0.0)], axis=0)


def _shifted(ext, off):
    n = ext.shape[0]
    return pltpu.roll(ext, (-off) % n, 0)[HALO:HALO + TM]


def _conv(ext, cw_ref, cb_ref):
    xc = cb_ref[...] + cw_ref[0:1, :] * _shifted(ext, -2)
    for k in range(1, CONV_W):
        xc = xc + cw_ref[k:k + 1, :] * _shifted(ext, k - 2)
    return xc


def _log1p(y):
    return jnp.where(y < 1e-2, y * (1.0 - y * (0.5 - y * (1.0 / 3.0 - 0.25 * y))), jnp.log(1.0 + y))


def _softplus_neg(lam):
    return jnp.maximum(-lam, 0.0) + _log1p(jnp.exp(-jnp.abs(lam)))


def _one_minus_exp(x, exp_half):
    p = x * (1.0 + x * (1 / 2 + x * (1 / 6 + x * (1 / 24))))
    return jnp.where(x > -0.1, -p, 1.0 - exp_half * exp_half)


def _gates(xc, wr, br, wi, bi, lam):
    xb = xc.astype(BF16)
    r = _sigmoid(_dot(xb, wr) + br)
    ig = _sigmoid(_dot(xb, wi) + bi)
    sp = _softplus_neg(lam)
    log_a = -RG_C * r * sp
    a = jnp.exp(log_a)
    mult = jnp.sqrt(_one_minus_exp(2.0 * log_a, a))
    return r, ig, sp, a, mult


def _scan(a, b, reverse):
    n = a.shape[0]
    row = lax.broadcasted_iota(jnp.int32, a.shape, 0)
    s = 1
    while s < n:
        if s < HALO:
            if reverse:
                keep = row < n - s
                a_s = jnp.where(keep, pltpu.roll(a, n - s, 0), 1.0)
                b_s = jnp.where(keep, pltpu.roll(b, n - s, 0), 0.0)
            else:
                keep = row >= s
                a_s = jnp.where(keep, pltpu.roll(a, s, 0), 1.0)
                b_s = jnp.where(keep, pltpu.roll(b, s, 0), 0.0)
            b = a * b_s + b
            a = a * a_s
        elif reverse:
            b = jnp.concatenate([a[:n - s] * b[s:] + b[:n - s], b[n - s:]], axis=0)
            a = jnp.concatenate([a[:n - s] * a[s:], a[n - s:]], axis=0)
        else:
            b = jnp.concatenate([b[:s], a[s:] * b[:n - s] + b[s:]], axis=0)
            a = jnp.concatenate([a[:s], a[s:] * a[:n - s]], axis=0)
        s *= 2
    return a, b


def lru_fwd(p, conv_w, conv_b, wr, br, wi, bi, lam, reverse, name, rider=None):
    t = p.shape[0]
    nt = t // TM

    def tile_of(s):
        return jnp.where(s == 0, 0, nt - s) if reverse else s

    def body(x_ref, xp_ref, xn_ref, cw_ref, cb_ref, wr_ref, br_ref, wi_ref, bi_ref, lam_ref, h_ref, carry):
        s = pl.program_id(0)
        tile = tile_of(s)

        @pl.when(s == 0)
        def _():
            carry[...] = jnp.zeros_like(carry)

        ext = _ext(tile, nt, x_ref[...], xp_ref[...], xn_ref[...])
        xc = _conv(ext, cw_ref, cb_ref)
        _, ig, _, a, mult = _gates(xc, wr_ref[...], br_ref[...], wi_ref[...], bi_ref[...], lam_ref[...])
        a_cum, hl = _scan(a, mult * (ig * xc), reverse)
        hh = hl + a_cum * carry[...]
        h_ref[...] = hh
        carry[...] = hh[0:1, :] if reverse else hh[TM - 1:TM, :]

    main, prev, nxt = _halo_specs(nt, tile_of)
    vec = pl.BlockSpec((1, LRU_W), lambda s: (0, 0))
    mat = pl.BlockSpec((LRU_W, LRU_W), lambda s: (0, 0))
    outs, got = _grid_call(
        body, name=name, nsteps=nt,
        in_specs=[main, prev, nxt, pl.BlockSpec((CONV_W, LRU_W), lambda s: (0, 0)), vec, mat, vec, mat, vec, vec],
        out_specs=[main],
        out_shape=[jax.ShapeDtypeStruct((t, LRU_W), F32)],
        scratch_shapes=[pltpu.VMEM((1, LRU_W), F32)],
        args=(p, p, p, conv_w, conv_b, wr, br, wi, bi, lam), rider=rider)
    return outs[0], got


def lru_bwd(p, hs, dhs, conv_w, conv_b, wr, br, wi, bi, lam, reverse, name):
    t = p.shape[0]
    nt = t // TM
    bpt = TM // HALO

    def tile_of(s):
        return jnp.where(s == nt - 1, 0, s + 1) if reverse else nt - 1 - s

    def hprev_block(s):
        tile = tile_of(s)
        if reverse:
            return (jnp.where(tile == nt - 1, 0, jnp.minimum((tile + 1) * bpt, nt * bpt - 1)), 0)
        return (jnp.maximum(tile * bpt - 1, 0), 0)

    def body(x_ref, xp_ref, xn_ref, h_ref, hp_ref, dh_ref, cw_ref, cb_ref, wr_ref, br_ref, wi_ref, bi_ref, lam_ref,
             dxc_ref, dwr_ref, dwi_ref, dbr_ref, dbi_ref, dlam_ref, carry):
        s = pl.program_id(0)
        tile = tile_of(s)

        @pl.when(s == 0)
        def _():
            carry[...] = jnp.zeros_like(carry)
            for ref in (dwr_ref, dwi_ref, dbr_ref, dbi_ref, dlam_ref):
                ref[...] = jnp.zeros_like(ref)

        ext = _ext(tile, nt, x_ref[...], xp_ref[...], xn_ref[...])
        xc = _conv(ext, cw_ref, cb_ref)
        wr_, wi_ = wr_ref[...], wi_ref[...]
        r, ig, sp, a, mult = _gates(xc, wr_, br_ref[...], wi_, bi_ref[...], lam_ref[...])
        gated = ig * xc
        row = lax.broadcasted_iota(jnp.int32, (TM, LRU_W), 0)
        hh = h_ref[...]
        start = jnp.where(tile != 0, hp_ref[0:1, :] if reverse else hp_ref[HALO - 1:HALO, :], 0.0)
        if reverse:
            edge = row == TM - 1
            hprev = jnp.where(edge, start, pltpu.roll(hh, TM - 1, 0))
            coef = jnp.where(row == 0, 0.0, pltpu.roll(a, 1, 0))
            bb = dh_ref[...] + jnp.where(row == 0, carry[...], 0.0)
        else:
            edge = row == 0
            hprev = jnp.where(edge, start, pltpu.roll(hh, 1, 0))
            coef = jnp.where(row == TM - 1, 0.0, pltpu.roll(a, TM - 1, 0))
            bb = dh_ref[...] + jnp.where(row == TM - 1, carry[...], 0.0)
        _, lmb = _scan(coef, bb, not reverse)
        al = a * lmb
        carry[...] = al[TM - 1:TM, :] if reverse else al[0:1, :]

        dgated = lmb * mult
        dloga = (lmb * hprev) * a - (lmb * gated) * (a * a) / mult
        dpre_r = (dloga * (-RG_C * sp)) * r * (1.0 - r)
        dpre_i = (dgated * xc) * ig * (1.0 - ig)
        drb, dib = dpre_r.astype(BF16), dpre_i.astype(BF16)
        xb = xc.astype(BF16)
        dxc_ref[...] = dgated * ig + _dot_nt(drb, wr_) + _dot_nt(dib, wi_)
        dwr_ref[...] += _dot_tn(xb, drb)
        dwi_ref[...] += _dot_tn(xb, dib)
        dbr_ref[...] += jnp.sum(dpre_r, axis=0, keepdims=True)
        dbi_ref[...] += jnp.sum(dpre_i, axis=0, keepdims=True)
        dlam_ref[...] += jnp.sum(dloga * (-RG_C * r), axis=0, keepdims=True)

        @pl.when(s == nt - 1)
        def _():
            dlam_ref[...] = dlam_ref[...] * (-_sigmoid(-lam_ref[...]))

    main, prev, nxt = _halo_specs(nt, tile_of)
    vec = pl.BlockSpec((1, LRU_W), lambda s: (0, 0))
    mat = pl.BlockSpec((LRU_W, LRU_W), lambda s: (0, 0))
    vshape = jax.ShapeDtypeStruct((1, LRU_W), F32)
    mshape = jax.ShapeDtypeStruct((LRU_W, LRU_W), F32)
    return pl.pallas_call(
        body, name=name, grid=(nt,),
        in_specs=[main, prev, nxt, main, pl.BlockSpec((HALO, LRU_W), hprev_block), main,
                  pl.BlockSpec((CONV_W, LRU_W), lambda s: (0, 0)), vec, mat, vec, mat, vec, vec],
        out_specs=[main, mat, mat, vec, vec, vec],
        out_shape=[jax.ShapeDtypeStruct((t, LRU_W), F32), mshape, mshape, vshape, vshape, vshape],
        scratch_shapes=[pltpu.VMEM((1, LRU_W), F32)],
        compiler_params=_cp(),
    )(p, p, p, hs, hs, dhs, conv_w, conv_b, wr, br, wi, bi, lam)


GELU_C = 0.7978845608028654
GELU_A = 0.044715


def _gelu(x):
    th = jnp.tanh(GELU_C * (x + GELU_A * x * x * x))
    return 0.5 * x * (1.0 + th), th


def _sgu(v, gain, w_ref, bt_ref):
    mu = jnp.mean(v, axis=-1, keepdims=True)
    xc = v - mu
    rs = lax.rsqrt(jnp.mean(xc * xc, axis=-1, keepdims=True) + EPS)
    vhat = xc * rs
    vnb = (vhat * gain).astype(BF16)
    chunks = []
    for ch in range(TM // CHUNK):
        zs = []
        for gi in range(GROUPS):
            vb = vnb[ch * CHUNK:(ch + 1) * CHUNK, gi * GROUP_DIM:(gi + 1) * GROUP_DIM]
            zs.append(_dot(w_ref[gi].astype(BF16), vb) + bt_ref[:, gi:gi + 1])
        chunks.append(jnp.concatenate(zs, axis=1))
    return jnp.concatenate(chunks, axis=0), vhat, rs, vnb


def _pcols(k):
    return pl.BlockSpec((TM, LRU_W), lambda i: (i, k))


def mix_out_fwd(h, p, hf, hb, mod, sgu_g, sgu_w, sgu_bt, womix, name):
    t = h.shape[0]

    def body(h_ref, gl_ref, u_ref, v_ref, hf_ref, hb_ref, mod_ref, sg_ref, sw_ref, sb_ref, w_hbm, out_ref, o_ref, w_v):
        i = pl.program_id(0)

        @pl.when(i == 0)
        def _():
            pltpu.sync_copy(w_hbm, w_v)

        ic = _kind(i)
        ge, _ = _gelu(gl_ref[...])
        y_lru = (hf_ref[...] + hb_ref[...]) * ge
        z, _, _, _ = _sgu(v_ref[...], sg_ref[...], sw_ref, sb_ref)
        y = jnp.concatenate([y_lru, u_ref[...] * z], axis=1).astype(BF16)
        o = _dot(y, w_v[...])
        o_ref[...] = o
        out_ref[...] = h_ref[...] + _sel(ic, mod_ref, 2) * o

    row = pl.BlockSpec((TM, D), lambda i: (i, 0))
    half = pl.BlockSpec((TM, LRU_W), lambda i: (i, 0))
    return pl.pallas_call(
        body, name=name, grid=(t // TM,),
        in_specs=[row, _pcols(1), _pcols(2), _pcols(3), half, half, pl.BlockSpec((2, 3, D), lambda i: (0, 0, 0)),
                  pl.BlockSpec((1, MLP_W), lambda i: (0, 0)), pl.BlockSpec((GROUPS, CHUNK, CHUNK), lambda i: (0, 0, 0)),
                  pl.BlockSpec((CHUNK, GROUPS), lambda i: (0, 0)), ANY],
        out_specs=[row, row],
        out_shape=[jax.ShapeDtypeStruct((t, D), F32), jax.ShapeDtypeStruct((t, D), F32)],
        scratch_shapes=[pltpu.VMEM((D, D), BF16)],
        compiler_params=_cp(),
    )(h, p, p, p, hf, hb, mod, sgu_g, sgu_w, sgu_bt, womix)


def mix_out_bwd(dy, p, hf, hb, o, mod, sgu_g, sgu_w, sgu_bt, womix, name, rider=None):
    t = dy.shape[0]

    def body(dy_ref, gl_ref, u_ref, v_ref, hf_ref, hb_ref, o_ref, mod_ref, sg_ref, sw_ref, sb_ref, w_hbm,
             dhs_ref, dp_ref, dw_ref, dgate_ref, dsg_ref, dsw_ref, dsb_ref, w_v):
        i = pl.program_id(0)

        @pl.when(i == 0)
        def _():
            pltpu.sync_copy(w_hbm, w_v)
            for ref in (dw_ref, dgate_ref, dsg_ref, dsw_ref, dsb_ref):
                ref[...] = jnp.zeros_like(ref)

        ic = _kind(i)
        dy_ = dy_ref[...]
        _acc2(dgate_ref, 0, dy_ * o_ref[...], ic)
        dob = (_sel(ic, mod_ref, 2) * dy_).astype(BF16)

        gl = gl_ref[...]
        ge, th = _gelu(gl)
        hsum = hf_ref[...] + hb_ref[...]
        gain = sg_ref[...]
        uu = u_ref[...]
        z, vhat, rs, vnb = _sgu(v_ref[...], gain, sw_ref, sb_ref)
        y = jnp.concatenate([hsum * ge, uu * z], axis=1).astype(BF16)
        dw_ref[...] += _dot_tn(y, dob)
        dyy = _dot_nt(dob, w_v[...])
        dyl, dys = dyy[:, :LRU_W], dyy[:, LRU_W:]

        dhs_ref[...] = dyl * ge
        dge = 0.5 * (1.0 + th) + 0.5 * gl * (1.0 - th * th) * (GELU_C * (1.0 + 3.0 * GELU_A * gl * gl))
        dp_ref[:, 0:LRU_W] = dyl * hsum * dge
        dp_ref[:, LRU_W:2 * LRU_W] = dys * z

        dz = dys * uu
        dzb = dz.astype(BF16)
        dvn_chunks, dsb_cols = [], [jnp.zeros((CHUNK, 1), F32)] * GROUPS
        for ch in range(TM // CHUNK):
            cols = []
            for gi in range(GROUPS):
                rs_, cs_ = slice(ch * CHUNK, (ch + 1) * CHUNK), slice(gi * GROUP_DIM, (gi + 1) * GROUP_DIM)
                dzg = dzb[rs_, cs_]
                dsb_cols[gi] = dsb_cols[gi] + jnp.sum(dz[rs_, cs_], axis=1, keepdims=True)
                dsw_ref[gi] += _dot_nt(dzg, vnb[rs_, cs_])
                cols.append(_dot_tn(sw_ref[gi].astype(BF16), dzg))
            dvn_chunks.append(jnp.concatenate(cols, axis=1))
        dsb_ref[...] += jnp.concatenate(dsb_cols, axis=1)
        dvn = jnp.concatenate(dvn_chunks, axis=0)
        dsg_ref[...] += jnp.sum(dvn * vhat, axis=0, keepdims=True)
        dvh = dvn * gain
        dp_ref[:, 2 * LRU_W:3 * LRU_W] = rs * (dvh - jnp.mean(dvh, axis=-1, keepdims=True)
                                               - vhat * jnp.mean(dvh * vhat, axis=-1, keepdims=True))

    row = pl.BlockSpec((TM, D), lambda i: (i, 0))
    half = pl.BlockSpec((TM, LRU_W), lambda i: (i, 0))
    const2 = lambda i: (0, 0)
    const3 = lambda i: (0, 0, 0)
    return _grid_call(
        body, name=name, nsteps=t // TM,
        in_specs=[row, _pcols(1), _pcols(2), _pcols(3), half, half, row, pl.BlockSpec((2, 3, D), const3),
                  pl.BlockSpec((1, MLP_W), const2), pl.BlockSpec((GROUPS, CHUNK, CHUNK), const3),
                  pl.BlockSpec((CHUNK, GROUPS), const2), ANY],
        out_specs=[half, pl.BlockSpec((TM, 3 * LRU_W), lambda i: (i, 0)), pl.BlockSpec((D, D), const2),
                   pl.BlockSpec((2, 1, D), const3), pl.BlockSpec((1, MLP_W), const2),
                   pl.BlockSpec((GROUPS, CHUNK, CHUNK), const3), pl.BlockSpec((CHUNK, GROUPS), const2)],
        out_shape=[jax.ShapeDtypeStruct((t, LRU_W), F32), jax.ShapeDtypeStruct((t, 3 * LRU_W), F32),
                   jax.ShapeDtypeStruct((D, D), F32), jax.ShapeDtypeStruct((2, 1, D), F32),
                   jax.ShapeDtypeStruct((1, MLP_W), F32), jax.ShapeDtypeStruct((GROUPS, CHUNK, CHUNK), F32),
                   jax.ShapeDtypeStruct((CHUNK, GROUPS), F32)],
        scratch_shapes=[pltpu.VMEM((D, D), BF16)],
        args=(dy, p, p, p, hf, hb, o, mod, sgu_g, sgu_w, sgu_bt, womix), rider=rider)


def mix_in_bwd(dy, h, p, dxf, dxb, dprest, mod, g, conv_w, wmix, name, rider=None):
    t = dy.shape[0]
    nt = t // TM

    def body(dy_ref, h_ref, x_ref, xp_ref, xn_ref, f_ref, fp_ref, fn_ref, b_ref, bp_ref, bn_ref, dpr_ref, mod_ref,
             g_ref, cw_ref, w_hbm, dh_ref, dw_hbm, dmod_ref, dg_ref, dcw_ref, dcb_ref, w_v, dw_v):
        i = pl.program_id(0)

        @pl.when(i == 0)
        def _():
            pltpu.sync_copy(w_hbm, w_v)
            dw_v[...] = jnp.zeros_like(dw_v)
            for ref in (dmod_ref, dg_ref, dcw_ref, dcb_ref):
                ref[...] = jnp.zeros_like(ref)

        dmain = f_ref[...] + b_ref[...]
        dext = _ext(i, nt, dmain, fp_ref[...] + bp_ref[...], fn_ref[...] + bn_ref[...])
        xext = _ext(i, nt, x_ref[...], xp_ref[...], xn_ref[...])
        dxl = cw_ref[0:1, :] * _shifted(dext, 2)
        for k in range(1, CONV_W):
            dxl = dxl + cw_ref[k:k + 1, :] * _shifted(dext, 2 - k)
        dcw_ref[...] += jnp.concatenate(
            [jnp.sum(dmain * _shifted(xext, k - 2), axis=0, keepdims=True) for k in range(CONV_W)], axis=0)
        dcb_ref[...] += jnp.sum(dmain, axis=0, keepdims=True)

        ic = _kind(i)
        gain = g_ref[...]
        scale = _sel(ic, mod_ref, 1)
        z, n, r = _norm_mod(h_ref[...], gain, _sel(ic, mod_ref, 0), scale)
        zb = z.astype(BF16)
        dpb = jnp.concatenate([dxl, dpr_ref[...]], axis=1).astype(BF16)
        dz = jnp.zeros((TM, D), F32)
        for dd in range(N_DEV):
            dpd = dpb[:, dd * N_MIX_SHARD:(dd + 1) * N_MIX_SHARD]
            dz = dz + _dot_nt(dpd, w_v[dd])
            dw_v[dd] += _dot_tn(zb, dpd)
        _acc2(dmod_ref, 0, dz, ic)
        _acc2(dmod_ref, 1, dz * (n * gain), ic)
        dg_ref[...] += jnp.sum(dz * (1.0 + scale) * n, axis=0, keepdims=True)
        dh_ref[...] = dy_ref[...] + _norm_mod_bwd(dz, n, r, gain, scale)

        @pl.when(i == nt - 1)
        def _():
            pltpu.sync_copy(dw_v, dw_hbm)

    main, prev, nxt = _halo_specs(nt, lambda s: s)
    row = pl.BlockSpec((TM, D), lambda i: (i, 0))
    const2 = lambda i: (0, 0)
    return _grid_call(
        body, name=name, nsteps=nt,
        in_specs=[row, row, main, prev, nxt, main, prev, nxt, main, prev, nxt,
                  pl.BlockSpec((TM, 3 * LRU_W), lambda i: (i, 0)), pl.BlockSpec((2, 3, D), lambda i: (0, 0, 0)),
                  pl.BlockSpec((1, D), const2), pl.BlockSpec((CONV_W, LRU_W), const2), ANY],
        out_specs=[row, ANY, pl.BlockSpec((2, 2, D), lambda i: (0, 0, 0)), pl.BlockSpec((1, D), const2),
                   pl.BlockSpec((CONV_W, LRU_W), const2), pl.BlockSpec((1, LRU_W), const2)],
        out_shape=[jax.ShapeDtypeStruct((t, D), F32), jax.ShapeDtypeStruct((N_DEV, D, N_MIX_SHARD), F32),
                   jax.ShapeDtypeStruct((2, 2, D), F32), jax.ShapeDtypeStruct((1, D), F32),
                   jax.ShapeDtypeStruct((CONV_W, LRU_W), F32), jax.ShapeDtypeStruct((1, LRU_W), F32)],
        scratch_shapes=[pltpu.VMEM((N_DEV, D, N_MIX_SHARD), BF16), pltpu.VMEM((N_DEV, D, N_MIX_SHARD), F32)],
        args=(dy, h, p, p, p, dxf, dxf, dxf, dxb, dxb, dxb, dprest, mod, g, conv_w, wmix), rider=rider)


def loss_head(h, g, target, name):
    t = h.shape[0]
    nc = CTX // TM

    def body(h_ref, g_ref, t_ref, dh_ref, loss_ref, dg_ref):
        i = pl.program_id(0)

        @pl.when(i == 0)
        def _():
            loss_ref[...] = jnp.zeros_like(loss_ref)
            dg_ref[...] = jnp.zeros_like(dg_ref)

        @pl.when(i < nc)
        def _():
            dh_ref[...] = jnp.zeros_like(dh_ref)

        @pl.when(i >= nc)
        def _():
            hh = h_ref[...]
            gain = g_ref[...]
            r = lax.rsqrt(jnp.mean(hh * hh, axis=-1, keepdims=True) + EPS)
            n = hh * r
            err = n * gain - t_ref[...]
            loss_ref[...] += 0.5 * jnp.sum(jnp.mean(err * err, axis=-1, keepdims=True))
            dy = err * (1.0 / D)
            dg_ref[...] += jnp.sum(dy * n, axis=0, keepdims=True)
            dn = dy * gain
            dh_ref[...] = r * (dn - n * jnp.mean(dn * n, axis=-1, keepdims=True))

    row = pl.BlockSpec((TM, D), lambda i: (i, 0))
    return pl.pallas_call(
        body, name=name, grid=(t // TM,),
        in_specs=[row, pl.BlockSpec((1, D), lambda i: (0, 0)),
                  pl.BlockSpec((TM, D), lambda i: (jnp.maximum(i - nc, 0), 0))],
        out_specs=[row, pl.BlockSpec((8, 128), lambda i: (0, 0)), pl.BlockSpec((1, D), lambda i: (0, 0))],
        out_shape=[jax.ShapeDtypeStruct((t, D), F32), jax.ShapeDtypeStruct((8, 128), F32),
                   jax.ShapeDtypeStruct((1, D), F32)],
        compiler_params=_cp(),
    )(h, g, target)


def _block_diag(w):
    eye = jnp.eye(HEADS, dtype=w.dtype)
    return jnp.einsum("dhij,hk->dhikj", w, eye).reshape(2, LRU_W, LRU_W)


def _block_diag_inv(full):
    f = full.reshape(2, HEADS, HEAD_DIM, HEADS, HEAD_DIM)
    return jnp.stack([f[:, hd, :, hd, :] for hd in range(HEADS)], axis=1)


def small_layer(g1, gm, g2, conv_w, conv_b, w_r, b_r, w_i, b_i, lam, sgu_g, sgu_w, sgu_b):
    return dict(g1=g1[None, :], gm=gm[None, :], g2=g2[None, :], conv_w=conv_w, conv_b=conv_b[None, :],
                wr=_block_diag(w_r).astype(BF16), br=b_r[:, None, :], wi=_block_diag(w_i).astype(BF16),
                bi=b_i[:, None, :], lam=lam[:, None, :], sgu_g=sgu_g[None, :], sgu_w=sgu_w, sgu_bt=sgu_b.T)


def small_grads(g):
    out = dict(mix_norm_g=g["gm"][0], ffn2_norm_g=g["g2"][0], lru_conv_w=g["conv_w"],
               lru_conv_b=g["conv_b"][0], lru_w_r=_block_diag_inv(g["wr"]), lru_b_r=g["br"][:, 0, :],
               lru_w_i=_block_diag_inv(g["wi"]), lru_b_i=g["bi"][:, 0, :], lru_lambda=g["lam"][:, 0, :],
               sgu_norm_g=g["sgu_g"][0], sgu_w=g["sgu_w"], sgu_b=g["sgu_bt"].T)
    if "g1" in g:
        out["ffn1_norm_g"] = g["g1"][0]
    return out


BIG_KEYS = ("win1", "wout1", "wmix", "womix", "win2", "wout2")


def _as_blocks(key, g):
    return g if key == "wmix" else g.reshape(N_DEV, g.shape[0] // N_DEV, D)


def _gathered(key, a):
    return a if key == "wmix" else a.reshape(N_DEV * a.shape[1], D)


class _ReduceScatter:
    def __init__(self, c_idx, where):
        self.c_idx, self.where = c_idx, where
        self.out = {}

    def pair(self, group):
        return pair_rider([g for _, g in group])

    def after_pair(self, group, recv1, tag):
        parts = [pair_sum(g, r, self.c_idx, f"pair_sum_{tag}_{i}") for i, ((_, g), r) in enumerate(zip(group, recv1))]
        return chips_rider(parts)

    def after_chips(self, group, recv1, recv2, tag):
        for i, ((key, g), r1, r2) in enumerate(zip(group, recv1, recv2)):
            self.out[key] = final_sum(g, r1, r2, self.where, f"final_sum_{tag}_{i}")


def fwd_bwd(ctx_rows, x_rows, target, mods, shards, smalls, final_g, c_idx, where):
    assert CTX == TM and len(shards) == 2

    def gather(keys_by_layer):
        return GatherRider([shards[l][k] for l, k in keys_by_layer])

    def put(full, keys_by_layer, got):
        for (l, k), a in zip(keys_by_layer, got):
            full[l][k] = _gathered(k, a)

    full = [dict(s) for s in smalls]
    first = [(0, "win1"), (0, "wout1")]
    put(full, first, run_alone(gather(first), pl.ANY, "gather_w0"))
    riders = {
        "ffn1_fwd_0": [(0, "wmix"), (0, "womix"), (0, "win2")],
        "lru_fwd_0_0": [(0, "wout2")],
        "ffn2_fwd_0": [(1, "win1"), (1, "wout1")],
        "ffn1_fwd_1": [(1, "wmix"), (1, "womix"), (1, "win2")],
        "lru_fwd_1_0": [(1, "wout2")],
    }

    def ffn(which, l, h):
        name = f"ffn{which}_fwd_{l}"
        w = full[l]
        keys = riders.get(name)
        m = mods[l][:, 0:3] if which == 1 else mods[l][:, 6:9]
        outs, got = ffn_fwd(h, m, w[f"g{which}"], w[f"win{which}"], w[f"wout{which}"], name,
                            rider=gather(keys) if keys else None)
        if keys:
            put(full, keys, got)
        return outs

    saved = []
    h = (ctx_rows, x_rows)
    for l in range(2):
        mm = mods[l][:, 3:6]
        outs = ffn(1, l, h)
        h1, gu1, acc1 = outs[:3]
        hin = outs[3] if l == 0 else h
        w = full[l]
        p = mix_in_fwd(h1, mm, w["gm"], w["wmix"], f"mix_in_fwd_{l}")
        hs = []
        for d in range(2):
            keys = riders.get(f"lru_fwd_{l}_{d}")
            hd, got = lru_fwd(p, w["conv_w"], w["conv_b"], w["wr"][d], w["br"][d], w["wi"][d], w["bi"][d], w["lam"][d],
                              bool(d), f"lru_fwd_{l}_{d}", rider=gather(keys) if keys else None)
            if keys:
                put(full, keys, got)
            hs.append(hd)
        h2, o = mix_out_fwd(h1, p, hs[0], hs[1], mm, w["sgu_g"], w["sgu_w"], w["sgu_bt"], w["womix"], f"mix_out_fwd_{l}")
        h3, gu2, acc2 = ffn(2, l, h2)
        saved.append((hin, h1, h2, gu1, acc1, p, hs, o, gu2, acc2))
        h = h3
    dh, loss, dgf = loss_head(h, final_g, target, "loss_head")

    rs = _ReduceScatter(c_idx, where)
    grads, dmods, sums = [None, None], [None, None], [None, None]
    pending = None
    for l in (1, 0):
        w = full[l]
        m1, mm, m2 = mods[l][:, 0:3], mods[l][:, 3:6], mods[l][:, 6:9]
        hin, h1, h2, gu1, acc1, p, hs, o, gu2, acc2 = saved[l]
        g = {}
        rs.out = {}
        both = Riders([rs.pair(pending[0]), GatherRider([small_pack])]) if pending else None
        (dp2, g["wout2"], dgate2), got = ffn_bwd_a(dh, acc2, gu2, m2, w["wout2"], f"ffn2_bwd_a_{l}", rider=both)
        if pending:
            r1, (small_all,) = both.split(got)
        chips = rs.after_pair(pending[0], r1, pending[1]) if pending else None
        (dh, g["win2"], dmod2, g["g2"]), r2 = ffn_bwd_b(dh, h2, dp2, m2, w["g2"], w["win2"], f"ffn2_bwd_b_{l}", rider=chips)
        if pending:
            rs.after_chips(pending[0], r1, r2, pending[1])
            sums[l + 1].update(rs.out)
            rs.out = {}

        grp = [(k, _as_blocks(k, g[k])) for k in ("win2", "wout2")]
        (dhs, dprest, g["womix"], dgatem, g["sgu_g"], g["sgu_w"], g["sgu_bt"]), r1 = mix_out_bwd(
            dh, p, hs[0], hs[1], o, mm, w["sgu_g"], w["sgu_w"], w["sgu_bt"], w["womix"], f"mix_out_bwd_{l}",
            rider=rs.pair(grp))
        chips = rs.after_pair(grp, r1, f"a{l}")
        dx, per_dir = [], []
        for d in range(2):
            out = lru_bwd(p, hs[d], dhs, w["conv_w"], w["conv_b"], w["wr"][d], w["br"][d], w["wi"][d], w["bi"][d],
                          w["lam"][d], bool(d), f"lru_bwd_{l}_{d}")
            dx.append(out[0])
            per_dir.append(out[1:])
        for k, nm in enumerate(("wr", "wi", "br", "bi", "lam")):
            g[nm] = jnp.stack([per_dir[0][k], per_dir[1][k]])
        (dh, g["wmix"], dmodm, g["gm"], g["conv_w"], g["conv_b"]), r2 = mix_in_bwd(
            dh, h1, p, dx[0], dx[1], dprest, mm, w["gm"], w["conv_w"], w["wmix"], f"mix_in_bwd_{l}", rider=chips)
        rs.after_chips(grp, r1, r2, f"a{l}")
        sums[l] = dict(rs.out)
        rs.out = {}

        if l == 1:
            (dp1, g["wout1"], dgate1), _ = ffn_bwd_a(dh, acc1, gu1, m1, w["wout1"], f"ffn1_bwd_a_{l}")
            (dh, g["win1"], dmod1, g["g1"]), _ = ffn_bwd_b(dh, hin, dp1, m1, w["g1"], w["win1"], f"ffn1_bwd_b_{l}")
            pending = ([(k, _as_blocks(k, g[k])) for k in ("womix", "wmix", "wout1", "win1")], f"b{l}")
            per = small_grads(g)
            small_pack = _pack([per[n] for n in LAYER_SMALL])
        else:
            g_mix = [(k, _as_blocks(k, g[k])) for k in ("womix", "wmix")]
            (dp1, g["wout1"], dgate1), r1_mix = ffn_bwd_a(dh, acc1, gu1, m1, w["wout1"], f"ffn1_bwd_a_{l}",
                                                          rider=rs.pair(g_mix))
            g_out = [("wout1", _as_blocks("wout1", g["wout1"]))]
            per = small_grads(g)
            three = Riders([rs.after_pair(g_mix, r1_mix, f"b{l}"), rs.pair(g_out),
                            GatherRider([_pack([per[n] for n in LAYER_SMALL[1:]])])])
            (g["win1"],), got = ffn_bwd_dw(hin, dp1, m1, w["g1"], f"ffn1_bwd_dw_{l}", rider=three)
            r2_mix, r1_out, (small0_all,) = three.split(got)
            rs.after_chips(g_mix, r1_mix, r2_mix, f"b{l}")
            g_in = [("win1", _as_blocks("win1", g["win1"]))]
            both = Riders([rs.after_pair(g_out, r1_out, f"c{l}"), rs.pair(g_in)])
            nt = dh.shape[0] // TM
            cut0, cut1 = (3 * nt) // 8, nt - max(nt // 16, 1)
            part, got = ffn_bwd_dh(dh, hin, dp1, m1, w["g1"], w["win1"], f"ffn1_bwd_dh0_{l}", (0, cut0), rider=both)
            r2_out, r1_in = both.split(got)
            rs.after_chips(g_out, r1_out, r2_out, f"c{l}")
            part, r2_in = ffn_bwd_dh(dh, hin, dp1, m1, w["g1"], w["win1"], f"ffn1_bwd_dh1_{l}", (cut0, cut1), carry=part,
                                     rider=rs.after_pair(g_in, r1_in, f"d{l}"))
            rs.after_chips(g_in, r1_in, r2_in, f"d{l}")
            (dh, dmod1, g["g1"]), _ = ffn_bwd_dh(dh, hin, dp1, m1, w["g1"], w["win1"], f"ffn1_bwd_dh2_{l}", (cut1, nt),
                                                 carry=part)
            sums[l].update(rs.out)
        dmods[l] = jnp.concatenate([dmod1, dgate1, dmodm, dgatem, dmod2, dgate2], axis=1)
        grads[l] = g
    return loss, dh, jnp.stack(dmods), grads, (small0_all, small_all), sums, dgf


def _row_block(r, c, limit=262144):
    best = 8
    for rb in range(8, r + 1, 8):
        if r % rb == 0 and rb * c <= limit:
            best = rb
    return best


def pair_sum(grads, recv, c_idx, name):
    _, r, c = grads.shape
    rb = _row_block(r, c)

    def body(c_ref, g_ref, r_ref, o_ref):
        o_ref[...] = (g_ref[...] + r_ref[...]).astype(BF16)

    return pl.pallas_call(
        body, name=name,
        grid_spec=pltpu.PrefetchScalarGridSpec(
            num_scalar_prefetch=1, grid=(4, r // rb),
            in_specs=[pl.BlockSpec((1, rb, c), lambda j, i, c_ref: (2 * j + c_ref[0], i, 0)),
                      pl.BlockSpec((1, rb, c), lambda j, i, c_ref: (j, i, 0))],
            out_specs=pl.BlockSpec((1, rb, c), lambda j, i, c_ref: (j, i, 0))),
        out_shape=jax.ShapeDtypeStruct((4, r, c), BF16),
        compiler_params=_cp(2),
    )(c_idx, grads, recv)


def final_sum(grads, recv1, recv2, where, name):
    _, r, c = grads.shape
    rb = _row_block(r, c)

    def body(w_ref, g_ref, r1_ref, r2_ref, o_ref):
        far = (r2_ref[0].astype(F32) + r2_ref[1].astype(F32)) + r2_ref[2].astype(F32)
        o_ref[...] = (g_ref[0] + r1_ref[0]) + far

    return pl.pallas_call(
        body, name=name,
        grid_spec=pltpu.PrefetchScalarGridSpec(
            num_scalar_prefetch=1, grid=(r // rb,),
            in_specs=[pl.BlockSpec((1, rb, c), lambda i, w_ref: (w_ref[0], i, 0)),
                      pl.BlockSpec((1, rb, c), lambda i, w_ref: (w_ref[1], i, 0)),
                      pl.BlockSpec((3, rb, c), lambda i, w_ref: (0, i, 0))],
            out_specs=pl.BlockSpec((rb, c), lambda i, w_ref: (i, 0))),
        out_shape=jax.ShapeDtypeStruct((r, c), F32),
        compiler_params=_cp(1),
    )(where, grads, recv1, recv2)


ADA_ROWS = 16


def _silu(v):
    return v * _sigmoid(v)


def ada_fwd(cond, w_ada, b_slab, name):
    def body(c_ref, w_ref, b_ref, o_ref):
        s = _silu(c_ref[...]).astype(BF16)
        o_ref[0] = _dot(s, w_ref[0].astype(BF16)) + b_ref[0]

    return pl.pallas_call(
        body, name=name, grid=(DEPTH,),
        in_specs=[pl.BlockSpec((ADA_ROWS, D), lambda l: (0, 0)), pl.BlockSpec((1, D, ADA_SHARD), lambda l: (l, 0, 0)),
                  pl.BlockSpec((1, 1, ADA_SHARD), lambda l: (l, 0, 0))],
        out_specs=pl.BlockSpec((1, ADA_ROWS, ADA_SHARD), lambda l: (l, 0, 0)),
        out_shape=jax.ShapeDtypeStruct((DEPTH, ADA_ROWS, ADA_SHARD), F32),
        compiler_params=_cp(),
    )(cond, w_ada, b_slab)


def ada_bwd(cond, dm_sample, dm_ctx, w_ada, name):
    def body(c_ref, ds_ref, dc_ref, w_ref, gw_ref, dsc_ref):
        @pl.when(pl.program_id(0) == 0)
        def _():
            dsc_ref[...] = jnp.zeros_like(dsc_ref)

        s = _silu(c_ref[...]).astype(BF16)
        dcs = dc_ref[0]
        tot = dcs[0:1]
        for j in range(1, N_DEV):
            tot = tot + dcs[j:j + 1]
        tot8 = jnp.where(lax.broadcasted_iota(jnp.int32, (N_DEV, ADA_SHARD), 0) == 0, tot, 0.0)
        dm = jnp.concatenate([ds_ref[0], tot8], axis=0).astype(BF16)
        gw_ref[0] = _dot_tn(s, dm)
        dsc_ref[...] += _dot_nt(dm, w_ref[0].astype(BF16))[N_DEV:N_DEV + 1]

    slab = pl.BlockSpec((1, N_DEV, ADA_SHARD), lambda l: (l, 0, 0))
    wspec = pl.BlockSpec((1, D, ADA_SHARD), lambda l: (l, 0, 0))
    return pl.pallas_call(
        body, name=name, grid=(DEPTH,),
        in_specs=[pl.BlockSpec((ADA_ROWS, D), lambda l: (0, 0)), slab, slab, wspec],
        out_specs=[wspec, pl.BlockSpec((1, D), lambda l: (0, 0))],
        out_shape=[jax.ShapeDtypeStruct((DEPTH, D, ADA_SHARD), F32), jax.ShapeDtypeStruct((1, D), F32)],
        compiler_params=_cp(),
    )(cond, dm_sample, dm_ctx, w_ada)


def sum_over_devices(parts, name, silu_rows=0, w=None):
    _, r, c = parts.shape

    def body(*refs):
        p_ref, o_ref = refs[0], refs[-1]
        tot = p_ref[0]
        for j in range(1, N_DEV):
            tot = tot + p_ref[j]
        o_ref[...] = tot
        if silu_rows:
            wv = refs[1][...]
            s = _sigmoid(wv)
            o_ref[0:silu_rows, :] = tot[0:silu_rows, :] * (s * (1.0 + wv * (1.0 - s)))

    vm = pl.BlockSpec(memory_space=pltpu.VMEM)
    args = (parts,) if w is None else (parts, w)
    return pl.pallas_call(
        body, name=name, in_specs=[vm] * len(args), out_specs=vm,
        out_shape=jax.ShapeDtypeStruct((r, c), F32),
        compiler_params=pltpu.CompilerParams(vmem_limit_bytes=VMEM_LIMIT),
    )(*args)


def sum_dmods(dm_all, name):
    def body(d_ref, o_ref):
        for l in range(DEPTH):
            tot = d_ref[0, l]
            for j in range(1, N_DEV):
                tot = tot + d_ref[j, l]
            o_ref[l:l + 1, :] = tot[0:1] + tot[1:2]

    vm = pl.BlockSpec(memory_space=pltpu.VMEM)
    return pl.pallas_call(
        body, name=name, in_specs=[vm], out_specs=vm,
        out_shape=jax.ShapeDtypeStruct((DEPTH, N_MOD * D), F32),
    )(dm_all)


def adamw(w, g, m, v, name, rider=None):
    r, c = w.shape
    rb = _row_block(r, c, limit=131072)

    def body(w_ref, g_ref, m_ref, v_ref, d_ref, nm_ref, nv_ref):
        g_ = g_ref[...]
        nm = B1 * m_ref[...] + (1.0 - B1) * g_
        nv = B2 * v_ref[...] + (1.0 - B2) * (g_ * g_)
        nm_ref[...] = nm
        nv_ref[...] = nv
        m_hat = nm / (1.0 - B1 ** STEP)
        v_hat = nv / (1.0 - B2 ** STEP)
        d_ref[...] = -LR * (m_hat / (jnp.sqrt(v_hat) + ADAM_EPS) + WD * w_ref[...])

    blk = pl.BlockSpec((rb, c), lambda i: (i, 0))
    shp = jax.ShapeDtypeStruct((r, c), F32)
    return _grid_call(body, name=name, nsteps=r // rb, in_specs=[blk] * 4, out_specs=[blk] * 3, out_shape=[shp] * 3,
                      scratch_shapes=[], args=(w, g, m, v), rider=rider)


def _adamw_math(w, g, m, v):
    nm = B1 * m + (1.0 - B1) * g
    nv = B2 * v + (1.0 - B2) * (g * g)
    m_hat = nm / (1.0 - B1 ** STEP)
    v_hat = nv / (1.0 - B2 ** STEP)
    return -LR * (m_hat / (jnp.sqrt(v_hat) + ADAM_EPS) + WD * w), nm, nv


def adamw_layers(w, g_layers, m, v, name):
    _, r, c = w.shape
    assert len(g_layers) == DEPTH == 2 and g_layers[0].shape == (r, c)
    rb = _row_block(r, c, limit=131072)
    nb = r // rb

    def body(w_ref, g0_ref, g1_ref, m_ref, v_ref, go_ref, d_ref, nm_ref, nv_ref):
        g = jnp.where(pl.program_id(0) == 0, g0_ref[...], g1_ref[...])
        go_ref[0], d_ref[0], nm_ref[0], nv_ref[0] = (g,) + _adamw_math(w_ref[0], g, m_ref[0], v_ref[0])

    blk = pl.BlockSpec((1, rb, c), lambda l, i: (l, i, 0))
    g0_spec = pl.BlockSpec((rb, c), lambda l, i: (jnp.where(l == 0, i, nb - 1), 0))
    g1_spec = pl.BlockSpec((rb, c), lambda l, i: (jnp.where(l == 0, 0, i), 0))
    shp = jax.ShapeDtypeStruct(w.shape, F32)
    return pl.pallas_call(
        body, name=name, grid=(DEPTH, nb), in_specs=[blk, g0_spec, g1_spec, blk, blk], out_specs=[blk] * 4,
        out_shape=[shp] * 4, compiler_params=_cp(2),
    )(w, g_layers[0], g_layers[1], m, v)


def _adamw_nd(w, g, m, v, name, rider=None):
    shape = w.shape
    flat = lambda a: a.reshape(-1, shape[-1])
    outs, got = adamw(flat(w), flat(g), flat(m), flat(v), name, rider=rider)
    return tuple(o.reshape(shape) for o in outs), got


LANES = 128


PACK_UNIT = 8 * LANES


ADAMW_SMALL_ROWS = 512


def _pack(arrays, row_multiple=8):
    pieces, n = [], 0
    for a in arrays:
        pieces.append(a.reshape(-1).astype(F32))
        pad = (-a.size) % PACK_UNIT
        if pad:
            pieces.append(jnp.zeros((pad,), F32))
        n += a.size + pad
    tail = (-n) % (row_multiple * LANES)
    if tail:
        pieces.append(jnp.zeros((tail,), F32))
    return jnp.concatenate(pieces).reshape(-1, LANES)


def _unpack(packed, shapes):
    out, r0 = [], 0
    lead = packed.shape[:-2]
    for shp in shapes:
        size = 1
        for s in shp:
            size *= s
        nr = 8 * -(-size // PACK_UNIT)
        blk = packed[..., r0:r0 + nr, :].reshape(lead + (nr * LANES,))[..., :size]
        out.append(blk.reshape(lead + tuple(shp)))
        r0 += nr
    return out


WEIGHTS = ["c_ctx", "w_ada", "b_ada", "ffn1_norm_g", "ffn1_w_in", "ffn1_w_out", "mix_norm_g", "w_in_mix", "lru_conv_w",
           "lru_conv_b", "lru_w_r", "lru_b_r", "lru_w_i", "lru_b_i", "lru_lambda", "sgu_norm_g", "sgu_w", "sgu_b",
           "w_out_mix", "ffn2_norm_g", "ffn2_w_in", "ffn2_w_out", "final_norm_g"]
BIG = ["w_ada", "ffn1_w_in", "ffn1_w_out", "w_in_mix", "w_out_mix", "ffn2_w_in", "ffn2_w_out"]
SHARDED_SMALL = ["lru_conv_w", "lru_b_r", "lru_b_i", "lru_lambda"]
LAYER_SMALL = ["ffn1_norm_g", "mix_norm_g", "ffn2_norm_g", "lru_conv_w", "lru_conv_b", "lru_w_r", "lru_b_r", "lru_w_i",
               "lru_b_i", "lru_lambda", "sgu_norm_g", "sgu_w", "sgu_b"]
LRU_SHARD = LRU_W // N_DEV


def _widen(a):
    return jnp.moveaxis(a, 0, -2).reshape(a.shape[1:-1] + (LRU_W,))


def kernel(x, c, ctx, c_ctx, w_ada, b_ada, ffn1_norm_g, ffn1_w_in, ffn1_w_out, mix_norm_g, w_in_mix, lru_conv_w, lru_conv_b, lru_w_r, lru_b_r, lru_w_i, lru_b_i, lru_lambda, sgu_norm_g, sgu_w, sgu_b, w_out_mix, ffn2_norm_g, ffn2_w_in, ffn2_w_out, final_norm_g, loss_target, m_c_ctx, m_w_ada, m_b_ada, m_ffn1_norm_g, m_ffn1_w_in, m_ffn1_w_out, m_mix_norm_g, m_w_in_mix, m_lru_conv_w, m_lru_conv_b, m_lru_w_r, m_lru_b_r, m_lru_w_i, m_lru_b_i, m_lru_lambda, m_sgu_norm_g, m_sgu_w, m_sgu_b, m_w_out_mix, m_ffn2_norm_g, m_ffn2_w_in, m_ffn2_w_out, m_final_norm_g, v_c_ctx, v_w_ada, v_b_ada, v_ffn1_norm_g, v_ffn1_w_in, v_ffn1_w_out, v_mix_norm_g, v_w_in_mix, v_lru_conv_w, v_lru_conv_b, v_lru_w_r, v_lru_b_r, v_lru_w_i, v_lru_b_i, v_lru_lambda, v_sgu_norm_g, v_sgu_w, v_sgu_b, v_w_out_mix, v_ffn2_norm_g, v_ffn2_w_in, v_ffn2_w_out, v_final_norm_g):
    given = dict(locals())
    W = {n: given[n] for n in WEIGHTS}
    M = {n: given["m_" + n] for n in WEIGHTS}
    V = {n: given["v_" + n] for n in WEIGHTS}
    xi, yi, ci = _position()
    me = 4 * xi + 2 * yi + ci
    chip = 2 * xi + yi

    sharded_shapes = [W[n].shape for n in SHARDED_SMALL]
    got = run_alone(GatherRider([_pack([c[0]] + [W[n] for n in SHARDED_SMALL])]), pltpu.VMEM, "gather_small")[0]
    parts = _unpack(got, [(D,)] + sharded_shapes)
    c_all = parts[0]
    wide = {n: _widen(a) for n, a in zip(SHARDED_SMALL, parts[1:])}
    cond = jnp.concatenate([c_all, c_ctx[None, :], jnp.zeros((ADA_ROWS - N_DEV - 1, D), F32)], axis=0)
    b_slab = lax.dynamic_slice_in_dim(b_ada, me * ADA_SHARD, ADA_SHARD, axis=1)[:, None, :]
    slabs = ada_fwd(cond, w_ada, b_slab, "ada_fwd")
    mall = run_alone(GatherRider([slabs.reshape(DEPTH * ADA_ROWS, ADA_SHARD)]), pltpu.VMEM, "gather_mod")[0]
    mall = mall.reshape(N_DEV, DEPTH, ADA_ROWS, ADA_SHARD)
    m_sample = lax.dynamic_index_in_dim(mall, me, axis=2, keepdims=False)
    m_ctx = mall[:, :, N_DEV, :]
    mods = jnp.stack([jnp.transpose(m, (1, 0, 2)).reshape(DEPTH, N_MOD, D) for m in (m_ctx, m_sample)], axis=1)

    shards, smalls = [], []
    tr = lambda a: jnp.swapaxes(a, 1, 2)
    for l in range(DEPTH):
        sh = dict(win1=tr(ffn1_w_in)[l], wout1=ffn1_w_out[l], wmix=w_in_mix[l], womix=w_out_mix[l], win2=tr(ffn2_w_in)[l],
                  wout2=ffn2_w_out[l])
        shards.append({k: a.astype(BF16) for k, a in sh.items()})
        smalls.append(small_layer(ffn1_norm_g[l], mix_norm_g[l], ffn2_norm_g[l], wide["lru_conv_w"][l], lru_conv_b[l],
                                  lru_w_r[l], wide["lru_b_r"][l], lru_w_i[l], wide["lru_b_i"][l], wide["lru_lambda"][l],
                                  sgu_norm_g[l], sgu_w[l], sgu_b[l]))

    c_idx = ci.reshape(1).astype(jnp.int32)
    where = jnp.stack([me, chip]).astype(jnp.int32)
    loss_blk, dx, dmods, grads, (small0_all, small1_all), gsum, dgf = fwd_bwd(
        ctx[0], x[0], loss_target[0], mods, shards, smalls, final_norm_g[None, :], c_idx, where)
    smalls_shape = {n: (W[n].shape[1:-1] + (LRU_W,)) if n in SHARDED_SMALL else W[n].shape[1:] for n in LAYER_SMALL}
    G, delta, new_m, new_v = {}, {}, {}, {}
    for key, n in (("win1", "ffn1_w_in"), ("wout1", "ffn1_w_out"), ("wmix", "w_in_mix"), ("womix", "w_out_mix"),
                   ("win2", "ffn2_w_in"), ("wout2", "ffn2_w_out")):
        t_in = tr if key in ("win1", "win2") else (lambda a: a)
        outs = adamw_layers(t_in(W[n]), [gsum[l][key] for l in range(DEPTH)], t_in(M[n]), t_in(V[n]), f"adamw_{n}")
        G[n], delta[n], new_m[n], new_v[n] = [t_in(o) for o in outs]

    n_rows = DEPTH * 2 * N_MOD
    dm_rows = jnp.concatenate([dmods.reshape(n_rows, D), jnp.zeros((-n_rows % 8, D), F32)], axis=0)
    dm_all = run_alone(GatherRider([dm_rows]), pltpu.VMEM, "gather_dmod")[0][:, :n_rows]
    dm_all = dm_all.reshape(N_DEV, DEPTH, 2, N_MOD * D)
    mine = lax.dynamic_slice_in_dim(dm_all, me * ADA_SHARD, ADA_SHARD, axis=3)
    G["w_ada"], dsc = ada_bwd(cond, jnp.transpose(mine[:, :, 1, :], (1, 0, 2)), jnp.transpose(mine[:, :, 0, :], (1, 0, 2)),
                              w_ada, "ada_bwd")
    G["b_ada"] = sum_dmods(dm_all, "sum_dmods")
    (delta["w_ada"], new_m["w_ada"], new_v["w_ada"]), _ = _adamw_nd(w_ada, G["w_ada"], m_w_ada, v_w_ada, "adamw_w_ada")

    head_all = run_alone(GatherRider([_pack([dsc[0], dgf[0], grads[0]["g1"][0]])]), pltpu.VMEM, "gather_head_grads")[0]
    head = _unpack(sum_over_devices(head_all, "sum_head_grads", silu_rows=D // LANES, w=c_ctx.reshape(D // LANES, LANES)),
                   [(D,), (D,), (D,)])
    shapes = [smalls_shape[n] for n in LAYER_SMALL]
    sum0 = [head[2]] + _unpack(sum_over_devices(small0_all, "sum_small_grads_0"), shapes[1:])
    sum1 = _unpack(sum_over_devices(small1_all, "sum_small_grads_1"), shapes)
    G["c_ctx"], G["final_norm_g"] = head[0], head[1]
    for n, a0, a1 in zip(LAYER_SMALL, sum0, sum1):
        a = jnp.stack([a0, a1])
        G[n] = lax.dynamic_slice_in_dim(a, me * LRU_SHARD, LRU_SHARD, axis=a.ndim - 1) if n in SHARDED_SMALL else a

    rest = [n for n in WEIGHTS if n not in BIG]
    shapes = [W[n].shape for n in rest]
    outs, _ = adamw(*[_pack([src[n] for n in rest], row_multiple=ADAMW_SMALL_ROWS) for src in (W, G, M, V)], "adamw_small")
    for dst, packed in zip((delta, new_m, new_v), outs):
        for n, a in zip(rest, _unpack(packed, shapes)):
            dst[n] = a

    loss = lax.psum(loss_blk[0, 0], ("x", "y", "c"))
    grad_x = dx[None]
    return (loss, grad_x, *[G[n] for n in WEIGHTS], *[delta[n] for n in WEIGHTS], *[new_m[n] for n in WEIGHTS],
            *[new_v[n] for n in WEIGHTS])
```

```python
import functools

import jax
import jax.numpy as jnp
from jax import lax
from jax.experimental import pallas as pl
from jax.experimental.pallas import tpu as pltpu

F32 = jnp.float32
BF16 = jnp.bfloat16

D = 1024
CTX = 256
DEPTH = 2
EPS = 1e-6
D_FF = 2816
LRU_W = 512
HEADS = 8
HEAD_DIM = 64
CONV_W = 4
RG_C = 8.0
GROUPS = 4
GROUP_DIM = 128
CHUNK = 128
MLP_W = 512
IN_PROJ = 2048
N_MOD = 9
N_DEV = 8

LR = 0.001
B1 = 0.9
B2 = 0.999
ADAM_EPS = 1e-08
WD = 0.01
STEP = 10

FF_IN_SHARD = 2 * D_FF // N_DEV
FF_OUT_SHARD = D_FF // N_DEV
HT = 256
WT = 512
N_MIX_SHARD = IN_PROJ // N_DEV
OMIX_SHARD = D // N_DEV
ADA_SHARD = N_MOD * D // N_DEV

TM = 256
HALO = 8
VMEM_LIMIT = 60 * 1024 * 1024

MESH = pl.DeviceIdType.MESH
ANY = pl.BlockSpec(memory_space=pl.ANY)


def _cp(n_axes=1):
    return pltpu.CompilerParams(dimension_semantics=("arbitrary",) * n_axes, vmem_limit_bytes=VMEM_LIMIT)


def _position():
    return lax.axis_index("x"), lax.axis_index("y"), lax.axis_index("c")


class GatherRider:
    def __init__(self, shards):
        n = len(shards)
        self.n = n
        self.ins = list(shards)
        self.out_shape = [jax.ShapeDtypeStruct((N_DEV,) + s.shape, s.dtype) for s in shards]
        self.sems = [pltpu.SemaphoreType.DMA((n, 7)), pltpu.SemaphoreType.DMA((n, 7)), pltpu.SemaphoreType.DMA((n,))]

    def _ctx(self, outs, sems):
        x, y, c = _position()
        chips = [(1 - x, y), (x, 1 - y), (1 - x, 1 - y)]

        def copy(t, k, block, to, src=None):
            dst = outs[t].at[4 * block[0] + 2 * block[1] + block[2]]
            return pltpu.make_async_remote_copy(
                src_ref=dst if src is None else src, dst_ref=dst, send_sem=sems[0].at[t, k],
                recv_sem=sems[1].at[t, k], device_id=to, device_id_type=MESH)

        return (x, y, c), (x, y, 1 - c), chips, copy

    def _local(self, ins, outs, sems, t):
        x, y, c = _position()
        return pltpu.make_async_copy(ins[t], outs[t].at[4 * x + 2 * y + c], sems[2].at[t])

    def _first(self, ins, outs, sems, t):
        me, sibling, chips, copy = self._ctx(outs, sems)
        return [copy(t, 0, me, sibling, src=ins[t])] + [copy(t, 1 + j, me, (*chip, me[2]), src=ins[t])
                                                         for j, chip in enumerate(chips)]

    def start(self, ins, outs, sems):
        for t in range(self.n):
            self._local(ins, outs, sems, t).start()
            for cp in self._first(ins, outs, sems, t):
                cp.start()

    def mid(self, ins, outs, sems):
        me, sibling, chips, copy = self._ctx(outs, sems)
        for j, chip in enumerate(chips):
            for t in range(self.n):
                copy(t, 1 + j, (*chip, me[2]), me).wait_recv()
                copy(t, 4 + j, (*chip, me[2]), sibling).start()

    def finish(self, ins, outs, sems):
        me, sibling, chips, copy = self._ctx(outs, sems)
        for t in range(self.n):
            copy(t, 0, sibling, me).wait_recv()
            for j, chip in enumerate(chips):
                copy(t, 4 + j, (*chip, 1 - me[2]), me).wait_recv()
        for t in range(self.n):
            for cp in self._first(ins, outs, sems, t):
                cp.wait_send()
            for j, chip in enumerate(chips):
                copy(t, 4 + j, (*chip, me[2]), sibling).wait_send()
            self._local(ins, outs, sems, t).wait()


class ExchangeRider:
    def __init__(self, tensors, plan, n_slots):
        n = len(tensors)
        self.n, self.plan = n, plan
        self.ins = list(tensors)
        self.out_shape = [jax.ShapeDtypeStruct((n_slots,) + s.shape[1:], s.dtype) for s in tensors]
        self.sems = [pltpu.SemaphoreType.DMA((n, n_slots)), pltpu.SemaphoreType.DMA((n, n_slots))]

    def _copies(self, ins, outs, sems):
        return [pltpu.make_async_remote_copy(
            src_ref=ins[t].at[block], dst_ref=outs[t].at[k], send_sem=sems[0].at[t, k], recv_sem=sems[1].at[t, k],
            device_id=to, device_id_type=MESH)
            for t in range(self.n) for k, (block, to) in enumerate(self.plan(*_position()))]

    def start(self, ins, outs, sems):
        for cp in self._copies(ins, outs, sems):
            cp.start()

    def mid(self, ins, outs, sems):
        pass

    def finish(self, ins, outs, sems):
        for cp in self._copies(ins, outs, sems):
            cp.wait()


class Riders:
    def __init__(self, riders):
        self.riders = list(riders)
        self.ins = [a for r in self.riders for a in r.ins]
        self.out_shape = [s for r in self.riders for s in r.out_shape]
        self.sems = [s for r in self.riders for s in r.sems]

    def _each(self, ins, outs, sems):
        i = o = s = 0
        for r in self.riders:
            ni, no, ns = len(r.ins), len(r.out_shape), len(r.sems)
            yield r, ins[i:i + ni], outs[o:o + no], sems[s:s + ns]
            i, o, s = i + ni, o + no, s + ns

    def start(self, ins, outs, sems):
        for r, a, b, c in self._each(ins, outs, sems):
            r.start(a, b, c)

    def mid(self, ins, outs, sems):
        for r, a, b, c in self._each(ins, outs, sems):
            r.mid(a, b, c)

    def finish(self, ins, outs, sems):
        for r, a, b, c in self._each(ins, outs, sems):
            r.finish(a, b, c)

    def split(self, outs):
        res, o = [], 0
        for r in self.riders:
            res.append(list(outs[o:o + len(r.out_shape)]))
            o += len(r.out_shape)
        return res


def pair_rider(grads):
    def plan(x, y, c):
        return [(4 * cx + 2 * cy + (1 - c), (x, y, 1 - c)) for cx in range(2) for cy in range(2)]
    return ExchangeRider(grads, plan, 4)


def chips_rider(parts):
    def plan(x, y, c):
        return [(2 * cx + cy, (cx, cy, c)) for cx, cy in [(1 - x, y), (x, 1 - y), (1 - x, 1 - y)]]
    return ExchangeRider(parts, plan, 3)


def run_alone(rider, space, name):
    ni = len(rider.ins)
    no = len(rider.out_shape)

    def body(*refs):
        ins, outs, sems = refs[:ni], refs[ni:ni + no], refs[ni + no:]
        rider.start(ins, outs, sems)
        rider.mid(ins, outs, sems)
        rider.finish(ins, outs, sems)

    spec = pl.BlockSpec(memory_space=space)
    return pl.pallas_call(
        body, name=name, in_specs=[spec] * ni, out_specs=[spec] * no, out_shape=rider.out_shape,
        scratch_shapes=rider.sems, compiler_params=pltpu.CompilerParams(vmem_limit_bytes=VMEM_LIMIT),
    )(*rider.ins)


def _grid_call(body, *, name, nsteps, in_specs, out_specs, out_shape, scratch_shapes, args, rider=None, aliases=None):
    aliases = aliases or {}
    if rider is None:
        outs = pl.pallas_call(body, name=name, grid=(nsteps,), in_specs=in_specs, out_specs=out_specs,
                              out_shape=out_shape, scratch_shapes=scratch_shapes, input_output_aliases=aliases,
                              compiler_params=_cp())(*args)
        return outs, []
    ni, no, ns = len(in_specs), len(out_specs), len(scratch_shapes)
    ri, ro = len(rider.ins), len(rider.out_shape)

    def wrapped(*refs):
        ins, refs = refs[:ni], refs[ni:]
        r_ins, refs = refs[:ri], refs[ri:]
        outs, refs = refs[:no], refs[no:]
        r_outs, refs = refs[:ro], refs[ro:]
        scratch, r_sems = refs[:ns], refs[ns:]
        s = pl.program_id(0)

        @pl.when(s == 0)
        def _():
            rider.start(r_ins, r_outs, r_sems)

        body(*ins, *outs, *scratch)

        @pl.when(s == (3 * nsteps) // 4)
        def _():
            rider.mid(r_ins, r_outs, r_sems)

        @pl.when(s == nsteps - 1)
        def _():
            rider.finish(r_ins, r_outs, r_sems)

    outs = pl.pallas_call(
        wrapped, name=name, grid=(nsteps,), in_specs=list(in_specs) + [ANY] * ri, out_specs=list(out_specs) + [ANY] * ro,
        out_shape=list(out_shape) + rider.out_shape, scratch_shapes=list(scratch_shapes) + rider.sems,
        input_output_aliases=aliases, compiler_params=_cp())(*args, *rider.ins)
    return outs[:no], outs[no:]


def _dot(a, b):
    return jnp.dot(a, b, preferred_element_type=F32)


def _dot_nt(a, b):
    return lax.dot_general(a, b, (((1,), (1,)), ((), ())), preferred_element_type=F32)


def _dot_tn(a, b):
    return lax.dot_general(a, b, (((0,), (0,)), ((), ())), preferred_element_type=F32)


def _sigmoid(x):
    return 1.0 / (1.0 + jnp.exp(-x))


def _kind(i):
    return jnp.where(i < CTX // TM, 0, 1)


def _sel(kind, mod_ref, k):
    return mod_ref[kind, k:k + 1, :]


def _acc2(ref, k, val, kind):
    ref[kind, k:k + 1, :] += jnp.sum(val, axis=0, keepdims=True)


def _norm_mod(h, g, shift, scale):
    r = lax.rsqrt(jnp.mean(h * h, axis=-1, keepdims=True) + EPS)
    n = h * r
    return (n * g) * (1.0 + scale) + shift, n, r


def _norm_mod_bwd(dz, n, r, g, scale):
    dn = dz * (g * (1.0 + scale))
    return r * (dn - n * jnp.mean(dn * n, axis=-1, keepdims=True))


def ffn_fwd(h, mod, g, win, wout, name, rider=None):
    split = isinstance(h, tuple)
    nc = CTX // TM
    t = h[0].shape[0] + h[1].shape[0] if split else h.shape[0]

    def body(*refs):
        if split:
            c_ref, x_ref, mod_ref, g_ref, win_hbm, wout_hbm, out_ref, gu_ref, acc_ref, h0_ref, win_v, wout_v, a_v = refs
        else:
            h_ref, mod_ref, g_ref, win_hbm, wout_hbm, out_ref, gu_ref, acc_ref, win_v, wout_v, a_v = refs
        i = pl.program_id(0)

        @pl.when(i == 0)
        def _():
            pltpu.sync_copy(win_hbm, win_v)
            pltpu.sync_copy(wout_hbm, wout_v)

        if split:
            hh = jnp.where(i < nc, c_ref[...], x_ref[...])
            h0_ref[...] = hh
        else:
            hh = h_ref[...]
        ic = _kind(i)
        z, _, _ = _norm_mod(hh, g_ref[...], _sel(ic, mod_ref, 0), _sel(ic, mod_ref, 1))
        zb = z.astype(BF16)
        for j in range(D_FF // HT):
            gb, ub = slice(j * HT, (j + 1) * HT), slice(D_FF + j * HT, D_FF + (j + 1) * HT)
            gg = _dot_nt(zb, win_v[gb, :])
            uu = _dot_nt(zb, win_v[ub, :])
            gu_ref[:, gb] = gg.astype(BF16)
            gu_ref[:, ub] = uu.astype(BF16)
            a_v[:, gb] = ((gg * _sigmoid(gg)) * uu).astype(BF16)
        acc = _dot(a_v[...], wout_v[...])
        acc_ref[...] = acc
        out_ref[...] = hh + (0.5 * _sel(ic, mod_ref, 2)) * acc

    row = pl.BlockSpec((TM, D), lambda i: (i, 0))
    rshape = jax.ShapeDtypeStruct((t, D), F32)
    if split:
        rows_in = [pl.BlockSpec((TM, D), lambda i: (jnp.minimum(i, nc - 1), 0)),
                   pl.BlockSpec((TM, D), lambda i: (jnp.maximum(i - nc, 0), 0))]
    else:
        rows_in = [row]
    return _grid_call(
        body, name=name, nsteps=t // TM,
        in_specs=rows_in + [pl.BlockSpec((2, 3, D), lambda i: (0, 0, 0)), pl.BlockSpec((1, D), lambda i: (0, 0)), ANY, ANY],
        out_specs=[row, pl.BlockSpec((TM, 2 * D_FF), lambda i: (i, 0)), row] + ([row] if split else []),
        out_shape=[rshape, jax.ShapeDtypeStruct((t, 2 * D_FF), BF16), rshape] + ([rshape] if split else []),
        scratch_shapes=[pltpu.VMEM((2 * D_FF, D), BF16), pltpu.VMEM((D_FF, D), BF16), pltpu.VMEM((TM, D_FF), BF16)],
        args=(*(h if split else (h,)), mod, g, win, wout), rider=rider)


def ffn_bwd_a(dy, acc, gu, mod, wout, name, rider=None):
    t = dy.shape[0]
    nt = t // TM

    def body(dy_ref, acc_ref, gu_ref, mod_ref, wout_hbm, dp_ref, dwout_hbm, dgate_ref, wout_v, dwout_v):
        i = pl.program_id(0)

        @pl.when(i == 0)
        def _():
            pltpu.sync_copy(wout_hbm, wout_v)
            dwout_v[...] = jnp.zeros_like(dwout_v)
            dgate_ref[...] = jnp.zeros_like(dgate_ref)

        dy_ = dy_ref[...]
        ic = _kind(i)
        _acc2(dgate_ref, 0, 0.5 * dy_ * acc_ref[...], ic)
        daccb = ((0.5 * _sel(ic, mod_ref, 2)) * dy_).astype(BF16)
        for j in range(D_FF // HT):
            blk, ublk = slice(j * HT, (j + 1) * HT), slice(D_FF + j * HT, D_FF + (j + 1) * HT)
            da = _dot_nt(daccb, wout_v[blk, :])
            gg = gu_ref[:, blk].astype(F32)
            uu = gu_ref[:, ublk].astype(F32)
            s = _sigmoid(gg)
            sl = gg * s
            dwout_v[blk, :] += _dot_tn((sl * uu).astype(BF16), daccb)
            dp_ref[:, blk] = (da * uu * (s + sl * (1.0 - s))).astype(BF16)
            dp_ref[:, ublk] = (da * sl).astype(BF16)

        @pl.when(i == nt - 1)
        def _():
            pltpu.sync_copy(dwout_v, dwout_hbm)

    row = pl.BlockSpec((TM, D), lambda i: (i, 0))
    wide = pl.BlockSpec((TM, 2 * D_FF), lambda i: (i, 0))
    return _grid_call(
        body, name=name, nsteps=nt,
        in_specs=[row, row, wide, pl.BlockSpec((2, 3, D), lambda i: (0, 0, 0)), ANY],
        out_specs=[wide, ANY, pl.BlockSpec((2, 1, D), lambda i: (0, 0, 0))],
        out_shape=[jax.ShapeDtypeStruct((t, 2 * D_FF), BF16), jax.ShapeDtypeStruct((D_FF, D), F32),
                   jax.ShapeDtypeStruct((2, 1, D), F32)],
        scratch_shapes=[pltpu.VMEM((D_FF, D), BF16), pltpu.VMEM((D_FF, D), F32)],
        args=(dy, acc, gu, mod, wout), rider=rider)


def ffn_bwd_b(dy, h, dp, mod, g, win, name, rider=None, latent_only=False):
    t = dy.shape[0]
    nt = t // TM
    nc = CTX // TM

    def body(dy_ref, h_ref, dp_ref, mod_ref, g_ref, win_hbm, dh_ref, dwin_hbm, dmod_ref, dg_ref, win_v, dwin_v):
        i = pl.program_id(0)

        @pl.when(i == 0)
        def _():
            pltpu.sync_copy(win_hbm, win_v)
            dwin_v[...] = jnp.zeros_like(dwin_v)
            dmod_ref[...] = jnp.zeros_like(dmod_ref)
            dg_ref[...] = jnp.zeros_like(dg_ref)

        ic = _kind(i)
        gain = g_ref[...]
        scale = _sel(ic, mod_ref, 1)
        z, n, r = _norm_mod(h_ref[...], gain, _sel(ic, mod_ref, 0), scale)
        zb = z.astype(BF16)
        dz = _dot(dp_ref[...], win_v[...])
        for j in range(2 * D_FF // WT):
            blk = slice(j * WT, (j + 1) * WT)
            dwin_v[blk, :] += _dot_tn(dp_ref[:, blk], zb)
        _acc2(dmod_ref, 0, dz, ic)
        _acc2(dmod_ref, 1, dz * (n * gain), ic)
        dg_ref[...] += jnp.sum(dz * (1.0 + scale) * n, axis=0, keepdims=True)
        dh_ref[...] = dy_ref[...] + _norm_mod_bwd(dz, n, r, gain, scale)

        @pl.when(i == nt - 1)
        def _():
            pltpu.sync_copy(dwin_v, dwin_hbm)

    row = pl.BlockSpec((TM, D), lambda i: (i, 0))
    if latent_only:
        dh_spec = pl.BlockSpec((TM, D), lambda i: (jnp.maximum(i - nc, 0), 0))
        dh_shape = jax.ShapeDtypeStruct((t - CTX, D), F32)
    else:
        dh_spec, dh_shape = row, jax.ShapeDtypeStruct((t, D), F32)
    return _grid_call(
        body, name=name, nsteps=nt,
        in_specs=[row, row, pl.BlockSpec((TM, 2 * D_FF), lambda i: (i, 0)),
                  pl.BlockSpec((2, 3, D), lambda i: (0, 0, 0)), pl.BlockSpec((1, D), lambda i: (0, 0)), ANY],
        out_specs=[dh_spec, ANY, pl.BlockSpec((2, 2, D), lambda i: (0, 0, 0)), pl.BlockSpec((1, D), lambda i: (0, 0))],
        out_shape=[dh_shape, jax.ShapeDtypeStruct((2 * D_FF, D), F32),
                   jax.ShapeDtypeStruct((2, 2, D), F32), jax.ShapeDtypeStruct((1, D), F32)],
        scratch_shapes=[pltpu.VMEM((2 * D_FF, D), BF16), pltpu.VMEM((2 * D_FF, D), F32)],
        args=(dy, h, dp, mod, g, win), rider=rider)


def ffn_bwd_dw(h, dp, mod, g, name, rider=None):
    t = h.shape[0]
    nt = t // TM

    def body(h_ref, dp_ref, mod_ref, g_ref, dwin_hbm, dwin_v):
        i = pl.program_id(0)

        @pl.when(i == 0)
        def _():
            dwin_v[...] = jnp.zeros_like(dwin_v)

        ic = _kind(i)
        z, _, _ = _norm_mod(h_ref[...], g_ref[...], _sel(ic, mod_ref, 0), _sel(ic, mod_ref, 1))
        zb = z.astype(BF16)
        for j in range(2 * D_FF // WT):
            blk = slice(j * WT, (j + 1) * WT)
            dwin_v[blk, :] += _dot_tn(dp_ref[:, blk], zb)

        @pl.when(i == nt - 1)
        def _():
            pltpu.sync_copy(dwin_v, dwin_hbm)

    return _grid_call(
        body, name=name, nsteps=nt,
        in_specs=[pl.BlockSpec((TM, D), lambda i: (i, 0)), pl.BlockSpec((TM, 2 * D_FF), lambda i: (i, 0)),
                  pl.BlockSpec((2, 3, D), lambda i: (0, 0, 0)), pl.BlockSpec((1, D), lambda i: (0, 0))],
        out_specs=[ANY], out_shape=[jax.ShapeDtypeStruct((2 * D_FF, D), F32)],
        scratch_shapes=[pltpu.VMEM((2 * D_FF, D), F32)],
        args=(h, dp, mod, g), rider=rider)


def ffn_bwd_dh(dy, h, dp, mod, g, win, name, tiles, carry=None, rider=None):
    t = dy.shape[0]
    nc = CTX // TM
    t0, t1 = tiles

    def body(*refs):
        if carry is None:
            dy_ref, h_ref, dp_ref, mod_ref, g_ref, win_hbm, dh_ref, dmod_ref, dg_ref, win_v = refs
        else:
            dy_ref, h_ref, dp_ref, mod_ref, g_ref, win_hbm, _, dmod0_ref, dg0_ref, dh_ref, dmod_ref, dg_ref, win_v = refs
        i = pl.program_id(0)

        @pl.when(i == 0)
        def _():
            pltpu.sync_copy(win_hbm, win_v)
            dmod_ref[...] = jnp.zeros_like(dmod_ref) if carry is None else dmod0_ref[...]
            dg_ref[...] = jnp.zeros_like(dg_ref) if carry is None else dg0_ref[...]

        ic = _kind(i + t0)
        gain = g_ref[...]
        scale = _sel(ic, mod_ref, 1)
        _, n, r = _norm_mod(h_ref[...], gain, _sel(ic, mod_ref, 0), scale)
        dz = _dot(dp_ref[...], win_v[...])
        _acc2(dmod_ref, 0, dz, ic)
        _acc2(dmod_ref, 1, dz * (n * gain), ic)
        dg_ref[...] += jnp.sum(dz * (1.0 + scale) * n, axis=0, keepdims=True)
        dh_ref[...] = dy_ref[...] + _norm_mod_bwd(dz, n, r, gain, scale)

    row = pl.BlockSpec((TM, D), lambda i: (i + t0, 0))
    small = [pl.BlockSpec((2, 2, D), lambda i: (0, 0, 0)), pl.BlockSpec((1, D), lambda i: (0, 0))]
    in_specs = [row, row, pl.BlockSpec((TM, 2 * D_FF), lambda i: (i + t0, 0)),
                pl.BlockSpec((2, 3, D), lambda i: (0, 0, 0)), pl.BlockSpec((1, D), lambda i: (0, 0)), ANY]
    args = (dy, h, dp, mod, g, win)
    if carry is not None:
        in_specs += [ANY] + small
        args += tuple(carry)
    return _grid_call(
        body, name=name, nsteps=t1 - t0, in_specs=in_specs,
        out_specs=[pl.BlockSpec((TM, D), lambda i: (jnp.maximum(i + t0 - nc, 0), 0))] + small,
        out_shape=[jax.ShapeDtypeStruct((t - CTX, D), F32), jax.ShapeDtypeStruct((2, 2, D), F32),
                   jax.ShapeDtypeStruct((1, D), F32)],
        scratch_shapes=[pltpu.VMEM((2 * D_FF, D), BF16)],
        args=args, rider=rider, aliases=None if carry is None else {6: 0})


def mix_in_fwd(h, mod, g, wmix, name):
    t = h.shape[0]

    def body(h_ref, mod_ref, g_ref, w_hbm, p_ref, w_v):
        i = pl.program_id(0)

        @pl.when(i == 0)
        def _():
            pltpu.sync_copy(w_hbm, w_v)

        ic = _kind(i)
        z, _, _ = _norm_mod(h_ref[...], g_ref[...], _sel(ic, mod_ref, 0), _sel(ic, mod_ref, 1))
        zb = z.astype(BF16)
        for dd in range(N_DEV):
            p_ref[:, dd * N_MIX_SHARD:(dd + 1) * N_MIX_SHARD] = _dot(zb, w_v[dd])

    return pl.pallas_call(
        body, name=name, grid=(t // TM,),
        in_specs=[pl.BlockSpec((TM, D), lambda i: (i, 0)), pl.BlockSpec((2, 3, D), lambda i: (0, 0, 0)),
                  pl.BlockSpec((1, D), lambda i: (0, 0)), ANY],
        out_specs=pl.BlockSpec((TM, IN_PROJ), lambda i: (i, 0)),
        out_shape=jax.ShapeDtypeStruct((t, IN_PROJ), F32),
        scratch_shapes=[pltpu.VMEM((N_DEV, D, N_MIX_SHARD), BF16)],
        compiler_params=_cp(),
    )(h, mod, g, wmix)


def _halo_specs(nt, tile_of):
    nb = nt * (TM // HALO)
    main = pl.BlockSpec((TM, LRU_W), lambda s: (tile_of(s), 0))
    prev = pl.BlockSpec((HALO, LRU_W), lambda s: (jnp.maximum(tile_of(s) * (TM // HALO) - 1, 0), 0))
    nxt = pl.BlockSpec((HALO, LRU_W), lambda s: (jnp.minimum((tile_of(s) + 1) * (TM // HALO), nb - 1), 0))
    return main, prev, nxt


def _ext(tile, nt, main, prev, nxt):
    has_prev = jnp.logical_and(tile != 0, tile != 1)
    has_next = jnp.logical_and(tile != 0, tile != nt - 1)
    return jnp.concatenate([jnp.where(has_prev, prev, 0.0), main, jnp.where(has_next, nxt, 0.0)], axis=0)


def _shifted(ext, off):
    n = ext.shape[0]
    return pltpu.roll(ext, (-off) % n, 0)[HALO:HALO + TM]


def _conv(ext, cw_ref, cb_ref):
    xc = cb_ref[...] + cw_ref[0:1, :] * _shifted(ext, -2)
    for k in range(1, CONV_W):
        xc = xc + cw_ref[k:k + 1, :] * _shifted(ext, k - 2)
    return xc


def _log1p(y):
    return jnp.where(y < 1e-2, y * (1.0 - y * (0.5 - y * (1.0 / 3.0 - 0.25 * y))), jnp.log(1.0 + y))


def _softplus_neg(lam):
    return jnp.maximum(-lam, 0.0) + _log1p(jnp.exp(-jnp.abs(lam)))


def _one_minus_exp(x, exp_half):
    p = x * (1.0 + x * (1 / 2 + x * (1 / 6 + x * (1 / 24))))
    return jnp.where(x > -0.1, -p, 1.0 - exp_half * exp_half)


def _gates(xc, wr, br, wi, bi, lam):
    xb = xc.astype(BF16)
    r = _sigmoid(_dot(xb, wr) + br)
    ig = _sigmoid(_dot(xb, wi) + bi)
    sp = _softplus_neg(lam)
    log_a = -RG_C * r * sp
    a = jnp.exp(log_a)
    mult = jnp.sqrt(_one_minus_exp(2.0 * log_a, a))
    return r, ig, sp, a, mult


def _scan(a, b, reverse):
    n = a.shape[0]
    row = lax.broadcasted_iota(jnp.int32, a.shape, 0)
    s = 1
    while s < n:
        if s < HALO:
            if reverse:
                keep = row < n - s
                a_s = jnp.where(keep, pltpu.roll(a, n - s, 0), 1.0)
                b_s = jnp.where(keep, pltpu.roll(b, n - s, 0), 0.0)
            else:
                keep = row >= s
                a_s = jnp.where(keep, pltpu.roll(a, s, 0), 1.0)
                b_s = jnp.where(keep, pltpu.roll(b, s, 0), 0.0)
            b = a * b_s + b
            a = a * a_s
        elif reverse:
            b = jnp.concatenate([a[:n - s] * b[s:] + b[:n - s], b[n - s:]], axis=0)
            a = jnp.concatenate([a[:n - s] * a[s:], a[n - s:]], axis=0)
        else:
            b = jnp.concatenate([b[:s], a[s:] * b[:n - s] + b[s:]], axis=0)
            a = jnp.concatenate([a[:s], a[s:] * a[:n - s]], axis=0)
        s *= 2
    return a, b


def lru_fwd(p, conv_w, conv_b, wr, br, wi, bi, lam, reverse, name, rider=None):
    t = p.shape[0]
    nt = t // TM

    def tile_of(s):
        return jnp.where(s == 0, 0, nt - s) if reverse else s

    def body(x_ref, xp_ref, xn_ref, cw_ref, cb_ref, wr_ref, br_ref, wi_ref, bi_ref, lam_ref, h_ref, carry):
        s = pl.program_id(0)
        tile = tile_of(s)

        @pl.when(s == 0)
        def _():
            carry[...] = jnp.zeros_like(carry)

        ext = _ext(tile, nt, x_ref[...], xp_ref[...], xn_ref[...])
        xc = _conv(ext, cw_ref, cb_ref)
        _, ig, _, a, mult = _gates(xc, wr_ref[...], br_ref[...], wi_ref[...], bi_ref[...], lam_ref[...])
        a_cum, hl = _scan(a, mult * (ig * xc), reverse)
        hh = hl + a_cum * carry[...]
        h_ref[...] = hh
        carry[...] = hh[0:1, :] if reverse else hh[TM - 1:TM, :]

    main, prev, nxt = _halo_specs(nt, tile_of)
    vec = pl.BlockSpec((1, LRU_W), lambda s: (0, 0))
    mat = pl.BlockSpec((LRU_W, LRU_W), lambda s: (0, 0))
    outs, got = _grid_call(
        body, name=name, nsteps=nt,
        in_specs=[main, prev, nxt, pl.BlockSpec((CONV_W, LRU_W), lambda s: (0, 0)), vec, mat, vec, mat, vec, vec],
        out_specs=[main],
        out_shape=[jax.ShapeDtypeStruct((t, LRU_W), F32)],
        scratch_shapes=[pltpu.VMEM((1, LRU_W), F32)],
        args=(p, p, p, conv_w, conv_b, wr, br, wi, bi, lam), rider=rider)
    return outs[0], got


def lru_bwd(p, hs, dhs, conv_w, conv_b, wr, br, wi, bi, lam, reverse, name):
    t = p.shape[0]
    nt = t // TM
    bpt = TM // HALO

    def tile_of(s):
        return jnp.where(s == nt - 1, 0, s + 1) if reverse else nt - 1 - s

    def hprev_block(s):
        tile = tile_of(s)
        if reverse:
            return (jnp.where(tile == nt - 1, 0, jnp.minimum((tile + 1) * bpt, nt * bpt - 1)), 0)
        return (jnp.maximum(tile * bpt - 1, 0), 0)

    def body(x_ref, xp_ref, xn_ref, h_ref, hp_ref, dh_ref, cw_ref, cb_ref, wr_ref, br_ref, wi_ref, bi_ref, lam_ref,
             dxc_ref, dwr_ref, dwi_ref, dbr_ref, dbi_ref, dlam_ref, carry):
        s = pl.program_id(0)
        tile = tile_of(s)

        @pl.when(s == 0)
        def _():
            carry[...] = jnp.zeros_like(carry)
            for ref in (dwr_ref, dwi_ref, dbr_ref, dbi_ref, dlam_ref):
                ref[...] = jnp.zeros_like(ref)

        ext = _ext(tile, nt, x_ref[...], xp_ref[...], xn_ref[...])
        xc = _conv(ext, cw_ref, cb_ref)
        wr_, wi_ = wr_ref[...], wi_ref[...]
        r, ig, sp, a, mult = _gates(xc, wr_, br_ref[...], wi_, bi_ref[...], lam_ref[...])
        gated = ig * xc
        row = lax.broadcasted_iota(jnp.int32, (TM, LRU_W), 0)
        hh = h_ref[...]
        start = jnp.where(tile != 0, hp_ref[0:1, :] if reverse else hp_ref[HALO - 1:HALO, :], 0.0)
        if reverse:
            edge = row == TM - 1
            hprev = jnp.where(edge, start, pltpu.roll(hh, TM - 1, 0))
            coef = jnp.where(row == 0, 0.0, pltpu.roll(a, 1, 0))
            bb = dh_ref[...] + jnp.where(row == 0, carry[...], 0.0)
        else:
            edge = row == 0
            hprev = jnp.where(edge, start, pltpu.roll(hh, 1, 0))
            coef = jnp.where(row == TM - 1, 0.0, pltpu.roll(a, TM - 1, 0))
            bb = dh_ref[...] + jnp.where(row == TM - 1, carry[...], 0.0)
        _, lmb = _scan(coef, bb, not reverse)
        al = a * lmb
        carry[...] = al[TM - 1:TM, :] if reverse else al[0:1, :]

        dgated = lmb * mult
        dloga = (lmb * hprev) * a - (lmb * gated) * (a * a) / mult
        dpre_r = (dloga * (-RG_C * sp)) * r * (1.0 - r)
        dpre_i = (dgated * xc) * ig * (1.0 - ig)
        drb, dib = dpre_r.astype(BF16), dpre_i.astype(BF16)
        xb = xc.astype(BF16)
        dxc_ref[...] = dgated * ig + _dot_nt(drb, wr_) + _dot_nt(dib, wi_)
        dwr_ref[...] += _dot_tn(xb, drb)
        dwi_ref[...] += _dot_tn(xb, dib)
        dbr_ref[...] += jnp.sum(dpre_r, axis=0, keepdims=True)
        dbi_ref[...] += jnp.sum(dpre_i, axis=0, keepdims=True)
        dlam_ref[...] += jnp.sum(dloga * (-RG_C * r), axis=0, keepdims=True)

        @pl.when(s == nt - 1)
        def _():
            dlam_ref[...] = dlam_ref[...] * (-_sigmoid(-lam_ref[...]))

    main, prev, nxt = _halo_specs(nt, tile_of)
    vec = pl.BlockSpec((1, LRU_W), lambda s: (0, 0))
    mat = pl.BlockSpec((LRU_W, LRU_W), lambda s: (0, 0))
    vshape = jax.ShapeDtypeStruct((1, LRU_W), F32)
    mshape = jax.ShapeDtypeStruct((LRU_W, LRU_W), F32)
    return pl.pallas_call(
        body, name=name, grid=(nt,),
        in_specs=[main, prev, nxt, main, pl.BlockSpec((HALO, LRU_W), hprev_block), main,
                  pl.BlockSpec((CONV_W, LRU_W), lambda s: (0, 0)), vec, mat, vec, mat, vec, vec],
        out_specs=[main, mat, mat, vec, vec, vec],
        out_shape=[jax.ShapeDtypeStruct((t, LRU_W), F32), mshape, mshape, vshape, vshape, vshape],
        scratch_shapes=[pltpu.VMEM((1, LRU_W), F32)],
        compiler_params=_cp(),
    )(p, p, p, hs, hs, dhs, conv_w, conv_b, wr, br, wi, bi, lam)


GELU_C = 0.7978845608028654
GELU_A = 0.044715


def _gelu(x):
    th = jnp.tanh(GELU_C * (x + GELU_A * x * x * x))
    return 0.5 * x * (1.0 + th), th


def _sgu(v, gain, w_ref, bt_ref):
    mu = jnp.mean(v, axis=-1, keepdims=True)
    xc = v - mu
    rs = lax.rsqrt(jnp.mean(xc * xc, axis=-1, keepdims=True) + EPS)
    vhat = xc * rs
    vnb = (vhat * gain).astype(BF16)
    chunks = []
    for ch in range(TM // CHUNK):
        zs = []
        for gi in range(GROUPS):
            vb = vnb[ch * CHUNK:(ch + 1) * CHUNK, gi * GROUP_DIM:(gi + 1) * GROUP_DIM]
            zs.append(_dot(w_ref[gi].astype(BF16), vb) + bt_ref[:, gi:gi + 1])
        chunks.append(jnp.concatenate(zs, axis=1))
    return jnp.concatenate(chunks, axis=0), vhat, rs, vnb


def _pcols(k):
    return pl.BlockSpec((TM, LRU_W), lambda i: (i, k))


def mix_out_fwd(h, p, hf, hb, mod, sgu_g, sgu_w, sgu_bt, womix, name):
    t = h.shape[0]

    def body(h_ref, gl_ref, u_ref, v_ref, hf_ref, hb_ref, mod_ref, sg_ref, sw_ref, sb_ref, w_hbm, out_ref, o_ref, w_v):
        i = pl.program_id(0)

        @pl.when(i == 0)
        def _():
            pltpu.sync_copy(w_hbm, w_v)

        ic = _kind(i)
        ge, _ = _gelu(gl_ref[...])
        y_lru = (hf_ref[...] + hb_ref[...]) * ge
        z, _, _, _ = _sgu(v_ref[...], sg_ref[...], sw_ref, sb_ref)
        y = jnp.concatenate([y_lru, u_ref[...] * z], axis=1).astype(BF16)
        o = _dot(y, w_v[...])
        o_ref[...] = o
        out_ref[...] = h_ref[...] + _sel(ic, mod_ref, 2) * o

    row = pl.BlockSpec((TM, D), lambda i: (i, 0))
    half = pl.BlockSpec((TM, LRU_W), lambda i: (i, 0))
    return pl.pallas_call(
        body, name=name, grid=(t // TM,),
        in_specs=[row, _pcols(1), _pcols(2), _pcols(3), half, half, pl.BlockSpec((2, 3, D), lambda i: (0, 0, 0)),
                  pl.BlockSpec((1, MLP_W), lambda i: (0, 0)), pl.BlockSpec((GROUPS, CHUNK, CHUNK), lambda i: (0, 0, 0)),
                  pl.BlockSpec((CHUNK, GROUPS), lambda i: (0, 0)), ANY],
        out_specs=[row, row],
        out_shape=[jax.ShapeDtypeStruct((t, D), F32), jax.ShapeDtypeStruct((t, D), F32)],
        scratch_shapes=[pltpu.VMEM((D, D), BF16)],
        compiler_params=_cp(),
    )(h, p, p, p, hf, hb, mod, sgu_g, sgu_w, sgu_bt, womix)


def mix_out_bwd(dy, p, hf, hb, o, mod, sgu_g, sgu_w, sgu_bt, womix, name, rider=None):
    t = dy.shape[0]

    def body(dy_ref, gl_ref, u_ref, v_ref, hf_ref, hb_ref, o_ref, mod_ref, sg_ref, sw_ref, sb_ref, w_hbm,
             dhs_ref, dp_ref, dw_ref, dgate_ref, dsg_ref, dsw_ref, dsb_ref, w_v):
        i = pl.program_id(0)

        @pl.when(i == 0)
        def _():
            pltpu.sync_copy(w_hbm, w_v)
            for ref in (dw_ref, dgate_ref, dsg_ref, dsw_ref, dsb_ref):
                ref[...] = jnp.zeros_like(ref)

        ic = _kind(i)
        dy_ = dy_ref[...]
        _acc2(dgate_ref, 0, dy_ * o_ref[...], ic)
        dob = (_sel(ic, mod_ref, 2) * dy_).astype(BF16)

        gl = gl_ref[...]
        ge, th = _gelu(gl)
        hsum = hf_ref[...] + hb_ref[...]
        gain = sg_ref[...]
        uu = u_ref[...]
        z, vhat, rs, vnb = _sgu(v_ref[...], gain, sw_ref, sb_ref)
        y = jnp.concatenate([hsum * ge, uu * z], axis=1).astype(BF16)
        dw_ref[...] += _dot_tn(y, dob)
        dyy = _dot_nt(dob, w_v[...])
        dyl, dys = dyy[:, :LRU_W], dyy[:, LRU_W:]

        dhs_ref[...] = dyl * ge
        dge = 0.5 * (1.0 + th) + 0.5 * gl * (1.0 - th * th) * (GELU_C * (1.0 + 3.0 * GELU_A * gl * gl))
        dp_ref[:, 0:LRU_W] = dyl * hsum * dge
        dp_ref[:, LRU_W:2 * LRU_W] = dys * z

        dz = dys * uu
        dzb = dz.astype(BF16)
        dvn_chunks, dsb_cols = [], [jnp.zeros((CHUNK, 1), F32)] * GROUPS
        for ch in range(TM // CHUNK):
            cols = []
            for gi in range(GROUPS):
                rs_, cs_ = slice(ch * CHUNK, (ch + 1) * CHUNK), slice(gi * GROUP_DIM, (gi + 1) * GROUP_DIM)
                dzg = dzb[rs_, cs_]
                dsb_cols[gi] = dsb_cols[gi] + jnp.sum(dz[rs_, cs_], axis=1, keepdims=True)
                dsw_ref[gi] += _dot_nt(dzg, vnb[rs_, cs_])
                cols.append(_dot_tn(sw_ref[gi].astype(BF16), dzg))
            dvn_chunks.append(jnp.concatenate(cols, axis=1))
        dsb_ref[...] += jnp.concatenate(dsb_cols, axis=1)
        dvn = jnp.concatenate(dvn_chunks, axis=0)
        dsg_ref[...] += jnp.sum(dvn * vhat, axis=0, keepdims=True)
        dvh = dvn * gain
        dp_ref[:, 2 * LRU_W:3 * LRU_W] = rs * (dvh - jnp.mean(dvh, axis=-1, keepdims=True)
                                               - vhat * jnp.mean(dvh * vhat, axis=-1, keepdims=True))

    row = pl.BlockSpec((TM, D), lambda i: (i, 0))
    half = pl.BlockSpec((TM, LRU_W), lambda i: (i, 0))
    const2 = lambda i: (0, 0)
    const3 = lambda i: (0, 0, 0)
    return _grid_call(
        body, name=name, nsteps=t // TM,
        in_specs=[row, _pcols(1), _pcols(2), _pcols(3), half, half, row, pl.BlockSpec((2, 3, D), const3),
                  pl.BlockSpec((1, MLP_W), const2), pl.BlockSpec((GROUPS, CHUNK, CHUNK), const3),
                  pl.BlockSpec((CHUNK, GROUPS), const2), ANY],
        out_specs=[half, pl.BlockSpec((TM, 3 * LRU_W), lambda i: (i, 0)), pl.BlockSpec((D, D), const2),
                   pl.BlockSpec((2, 1, D), const3), pl.BlockSpec((1, MLP_W), const2),
                   pl.BlockSpec((GROUPS, CHUNK, CHUNK), const3), pl.BlockSpec((CHUNK, GROUPS), const2)],
        out_shape=[jax.ShapeDtypeStruct((t, LRU_W), F32), jax.ShapeDtypeStruct((t, 3 * LRU_W), F32),
                   jax.ShapeDtypeStruct((D, D), F32), jax.ShapeDtypeStruct((2, 1, D), F32),
                   jax.ShapeDtypeStruct((1, MLP_W), F32), jax.ShapeDtypeStruct((GROUPS, CHUNK, CHUNK), F32),
                   jax.ShapeDtypeStruct((CHUNK, GROUPS), F32)],
        scratch_shapes=[pltpu.VMEM((D, D), BF16)],
        args=(dy, p, p, p, hf, hb, o, mod, sgu_g, sgu_w, sgu_bt, womix), rider=rider)


def mix_in_bwd(dy, h, p, dxf, dxb, dprest, mod, g, conv_w, wmix, name, rider=None):
    t = dy.shape[0]
    nt = t // TM

    def body(dy_ref, h_ref, x_ref, xp_ref, xn_ref, f_ref, fp_ref, fn_ref, b_ref, bp_ref, bn_ref, dpr_ref, mod_ref,
             g_ref, cw_ref, w_hbm, dh_ref, dw_hbm, dmod_ref, dg_ref, dcw_ref, dcb_ref, w_v, dw_v):
        i = pl.program_id(0)

        @pl.when(i == 0)
        def _():
            pltpu.sync_copy(w_hbm, w_v)
            dw_v[...] = jnp.zeros_like(dw_v)
            for ref in (dmod_ref, dg_ref, dcw_ref, dcb_ref):
                ref[...] = jnp.zeros_like(ref)

        dmain = f_ref[...] + b_ref[...]
        dext = _ext(i, nt, dmain, fp_ref[...] + bp_ref[...], fn_ref[...] + bn_ref[...])
        xext = _ext(i, nt, x_ref[...], xp_ref[...], xn_ref[...])
        dxl = cw_ref[0:1, :] * _shifted(dext, 2)
        for k in range(1, CONV_W):
            dxl = dxl + cw_ref[k:k + 1, :] * _shifted(dext, 2 - k)
        dcw_ref[...] += jnp.concatenate(
            [jnp.sum(dmain * _shifted(xext, k - 2), axis=0, keepdims=True) for k in range(CONV_W)], axis=0)
        dcb_ref[...] += jnp.sum(dmain, axis=0, keepdims=True)

        ic = _kind(i)
        gain = g_ref[...]
        scale = _sel(ic, mod_ref, 1)
        z, n, r = _norm_mod(h_ref[...], gain, _sel(ic, mod_ref, 0), scale)
        zb = z.astype(BF16)
        dpb = jnp.concatenate([dxl, dpr_ref[...]], axis=1).astype(BF16)
        dz = jnp.zeros((TM, D), F32)
        for dd in range(N_DEV):
            dpd = dpb[:, dd * N_MIX_SHARD:(dd + 1) * N_MIX_SHARD]
            dz = dz + _dot_nt(dpd, w_v[dd])
            dw_v[dd] += _dot_tn(zb, dpd)
        _acc2(dmod_ref, 0, dz, ic)
        _acc2(dmod_ref, 1, dz * (n * gain), ic)
        dg_ref[...] += jnp.sum(dz * (1.0 + scale) * n, axis=0, keepdims=True)
        dh_ref[...] = dy_ref[...] + _norm_mod_bwd(dz, n, r, gain, scale)

        @pl.when(i == nt - 1)
        def _():
            pltpu.sync_copy(dw_v, dw_hbm)

    main, prev, nxt = _halo_specs(nt, lambda s: s)
    row = pl.BlockSpec((TM, D), lambda i: (i, 0))
    const2 = lambda i: (0, 0)
    return _grid_call(
        body, name=name, nsteps=nt,
        in_specs=[row, row, main, prev, nxt, main, prev, nxt, main, prev, nxt,
                  pl.BlockSpec((TM, 3 * LRU_W), lambda i: (i, 0)), pl.BlockSpec((2, 3, D), lambda i: (0, 0, 0)),
                  pl.BlockSpec((1, D), const2), pl.BlockSpec((CONV_W, LRU_W), const2), ANY],
        out_specs=[row, ANY, pl.BlockSpec((2, 2, D), lambda i: (0, 0, 0)), pl.BlockSpec((1, D), const2),
                   pl.BlockSpec((CONV_W, LRU_W), const2), pl.BlockSpec((1, LRU_W), const2)],
        out_shape=[jax.ShapeDtypeStruct((t, D), F32), jax.ShapeDtypeStruct((N_DEV, D, N_MIX_SHARD), F32),
                   jax.ShapeDtypeStruct((2, 2, D), F32), jax.ShapeDtypeStruct((1, D), F32),
                   jax.ShapeDtypeStruct((CONV_W, LRU_W), F32), jax.ShapeDtypeStruct((1, LRU_W), F32)],
        scratch_shapes=[pltpu.VMEM((N_DEV, D, N_MIX_SHARD), BF16), pltpu.VMEM((N_DEV, D, N_MIX_SHARD), F32)],
        args=(dy, h, p, p, p, dxf, dxf, dxf, dxb, dxb, dxb, dprest, mod, g, conv_w, wmix), rider=rider)


def loss_head(h, g, target, name):
    t = h.shape[0]
    nc = CTX // TM

    def body(h_ref, g_ref, t_ref, dh_ref, loss_ref, dg_ref):
        i = pl.program_id(0)

        @pl.when(i == 0)
        def _():
            loss_ref[...] = jnp.zeros_like(loss_ref)
            dg_ref[...] = jnp.zeros_like(dg_ref)

        @pl.when(i < nc)
        def _():
            dh_ref[...] = jnp.zeros_like(dh_ref)

        @pl.when(i >= nc)
        def _():
            hh = h_ref[...]
            gain = g_ref[...]
            r = lax.rsqrt(jnp.mean(hh * hh, axis=-1, keepdims=True) + EPS)
            n = hh * r
            err = n * gain - t_ref[...]
            loss_ref[...] += 0.5 * jnp.sum(jnp.mean(err * err, axis=-1, keepdims=True))
            dy = err * (1.0 / D)
            dg_ref[...] += jnp.sum(dy * n, axis=0, keepdims=True)
            dn = dy * gain
            dh_ref[...] = r * (dn - n * jnp.mean(dn * n, axis=-1, keepdims=True))

    row = pl.BlockSpec((TM, D), lambda i: (i, 0))
    return pl.pallas_call(
        body, name=name, grid=(t // TM,),
        in_specs=[row, pl.BlockSpec((1, D), lambda i: (0, 0)),
                  pl.BlockSpec((TM, D), lambda i: (jnp.maximum(i - nc, 0), 0))],
        out_specs=[row, pl.BlockSpec((8, 128), lambda i: (0, 0)), pl.BlockSpec((1, D), lambda i: (0, 0))],
        out_shape=[jax.ShapeDtypeStruct((t, D), F32), jax.ShapeDtypeStruct((8, 128), F32),
                   jax.ShapeDtypeStruct((1, D), F32)],
        compiler_params=_cp(),
    )(h, g, target)


def _block_diag(w):
    eye = jnp.eye(HEADS, dtype=w.dtype)
    return jnp.einsum("dhij,hk->dhikj", w, eye).reshape(2, LRU_W, LRU_W)


def _block_diag_inv(full):
    f = full.reshape(2, HEADS, HEAD_DIM, HEADS, HEAD_DIM)
    return jnp.stack([f[:, hd, :, hd, :] for hd in range(HEADS)], axis=1)


def small_layer(g1, gm, g2, conv_w, conv_b, w_r, b_r, w_i, b_i, lam, sgu_g, sgu_w, sgu_b):
    return dict(g1=g1[None, :], gm=gm[None, :], g2=g2[None, :], conv_w=conv_w, conv_b=conv_b[None, :],
                wr=_block_diag(w_r).astype(BF16), br=b_r[:, None, :], wi=_block_diag(w_i).astype(BF16),
                bi=b_i[:, None, :], lam=lam[:, None, :], sgu_g=sgu_g[None, :], sgu_w=sgu_w, sgu_bt=sgu_b.T)


def small_grads(g):
    out = dict(mix_norm_g=g["gm"][0], ffn2_norm_g=g["g2"][0], lru_conv_w=g["conv_w"],
               lru_conv_b=g["conv_b"][0], lru_w_r=_block_diag_inv(g["wr"]), lru_b_r=g["br"][:, 0, :],
               lru_w_i=_block_diag_inv(g["wi"]), lru_b_i=g["bi"][:, 0, :], lru_lambda=g["lam"][:, 0, :],
               sgu_norm_g=g["sgu_g"][0], sgu_w=g["sgu_w"], sgu_b=g["sgu_bt"].T)
    if "g1" in g:
        out["ffn1_norm_g"] = g["g1"][0]
    return out


BIG_KEYS = ("win1", "wout1", "wmix", "womix", "win2", "wout2")


def _as_blocks(key, g):
    return g if key == "wmix" else g.reshape(N_DEV, g.shape[0] // N_DEV, D)


def _gathered(key, a):
    return a if key == "wmix" else a.reshape(N_DEV * a.shape[1], D)


class _ReduceScatter:
    def __init__(self, c_idx, where):
        self.c_idx, self.where = c_idx, where
        self.out = {}

    def pair(self, group):
        return pair_rider([g for _, g in group])

    def after_pair(self, group, recv1, tag):
        parts = [pair_sum(g, r, self.c_idx, f"pair_sum_{tag}_{i}") for i, ((_, g), r) in enumerate(zip(group, recv1))]
        return chips_rider(parts)

    def after_chips(self, group, recv1, recv2, tag):
        for (key, g), r1, r2 in zip(group, recv1, recv2):
            self.out[key] = (g, r1, r2)


FIRST_WEIGHTS = [(0, "win1"), (0, "wout1")]


def fwd_bwd(ctx_rows, x_rows, target, mods, shards, first_weights, smalls, final_g, c_idx, where):
    assert CTX == TM and len(shards) == 2

    def gather(keys_by_layer):
        return GatherRider([shards[l][k] for l, k in keys_by_layer])

    def put(full, keys_by_layer, got):
        for (l, k), a in zip(keys_by_layer, got):
            full[l][k] = _gathered(k, a)

    full = [dict(s) for s in smalls]
    put(full, FIRST_WEIGHTS, first_weights)
    riders = {
        "ffn1_fwd_0": [(0, "wmix"), (0, "womix"), (0, "win2")],
        "lru_fwd_0_0": [(0, "wout2")],
        "ffn2_fwd_0": [(1, "win1"), (1, "wout1")],
        "ffn1_fwd_1": [(1, "wmix"), (1, "womix"), (1, "win2")],
        "lru_fwd_1_0": [(1, "wout2")],
    }

    def ffn(which, l, h):
        name = f"ffn{which}_fwd_{l}"
        w = full[l]
        keys = riders.get(name)
        m = mods[l][:, 0:3] if which == 1 else mods[l][:, 6:9]
        outs, got = ffn_fwd(h, m, w[f"g{which}"], w[f"win{which}"], w[f"wout{which}"], name,
                            rider=gather(keys) if keys else None)
        if keys:
            put(full, keys, got)
        return outs

    saved = []
    h = (ctx_rows, x_rows)
    for l in range(2):
        mm = mods[l][:, 3:6]
        outs = ffn(1, l, h)
        h1, gu1, acc1 = outs[:3]
        hin = outs[3] if l == 0 else h
        w = full[l]
        p = mix_in_fwd(h1, mm, w["gm"], w["wmix"], f"mix_in_fwd_{l}")
        hs = []
        for d in range(2):
            keys = riders.get(f"lru_fwd_{l}_{d}")
            hd, got = lru_fwd(p, w["conv_w"], w["conv_b"], w["wr"][d], w["br"][d], w["wi"][d], w["bi"][d], w["lam"][d],
                              bool(d), f"lru_fwd_{l}_{d}", rider=gather(keys) if keys else None)
            if keys:
                put(full, keys, got)
            hs.append(hd)
        h2, o = mix_out_fwd(h1, p, hs[0], hs[1], mm, w["sgu_g"], w["sgu_w"], w["sgu_bt"], w["womix"], f"mix_out_fwd_{l}")
        h3, gu2, acc2 = ffn(2, l, h2)
        saved.append((hin, h1, h2, gu1, acc1, p, hs, o, gu2, acc2))
        h = h3
    dh, loss, dgf = loss_head(h, final_g, target, "loss_head")

    rs = _ReduceScatter(c_idx, where)
    grads, dmods, sums = [None, None], [None, None], [None, None]
    pending = None
    for l in (1, 0):
        w = full[l]
        m1, mm, m2 = mods[l][:, 0:3], mods[l][:, 3:6], mods[l][:, 6:9]
        hin, h1, h2, gu1, acc1, p, hs, o, gu2, acc2 = saved[l]
        g = {}
        rs.out = {}
        both = Riders([rs.pair(pending[0]), GatherRider([small_pack])]) if pending else None
        (dp2, g["wout2"], dgate2), got = ffn_bwd_a(dh, acc2, gu2, m2, w["wout2"], f"ffn2_bwd_a_{l}", rider=both)
        if pending:
            r1, (small_all,) = both.split(got)
        chips = rs.after_pair(pending[0], r1, pending[1]) if pending else None
        (dh, g["win2"], dmod2, g["g2"]), r2 = ffn_bwd_b(dh, h2, dp2, m2, w["g2"], w["win2"], f"ffn2_bwd_b_{l}", rider=chips)
        if pending:
            rs.after_chips(pending[0], r1, r2, pending[1])
            sums[l + 1].update(rs.out)
            rs.out = {}

        grp = [(k, _as_blocks(k, g[k])) for k in ("win2", "wout2")]
        (dhs, dprest, g["womix"], dgatem, g["sgu_g"], g["sgu_w"], g["sgu_bt"]), r1 = mix_out_bwd(
            dh, p, hs[0], hs[1], o, mm, w["sgu_g"], w["sgu_w"], w["sgu_bt"], w["womix"], f"mix_out_bwd_{l}",
            rider=rs.pair(grp))
        chips = rs.after_pair(grp, r1, f"a{l}")
        dx, per_dir = [], []
        for d in range(2):
            out = lru_bwd(p, hs[d], dhs, w["conv_w"], w["conv_b"], w["wr"][d], w["br"][d], w["wi"][d], w["bi"][d],
                          w["lam"][d], bool(d), f"lru_bwd_{l}_{d}")
            dx.append(out[0])
            per_dir.append(out[1:])
        for k, nm in enumerate(("wr", "wi", "br", "bi", "lam")):
            g[nm] = jnp.stack([per_dir[0][k], per_dir[1][k]])
        (dh, g["wmix"], dmodm, g["gm"], g["conv_w"], g["conv_b"]), r2 = mix_in_bwd(
            dh, h1, p, dx[0], dx[1], dprest, mm, w["gm"], w["conv_w"], w["wmix"], f"mix_in_bwd_{l}", rider=chips)
        rs.after_chips(grp, r1, r2, f"a{l}")
        sums[l] = dict(rs.out)
        rs.out = {}

        if l == 1:
            (dp1, g["wout1"], dgate1), _ = ffn_bwd_a(dh, acc1, gu1, m1, w["wout1"], f"ffn1_bwd_a_{l}")
            (dh, g["win1"], dmod1, g["g1"]), _ = ffn_bwd_b(dh, hin, dp1, m1, w["g1"], w["win1"], f"ffn1_bwd_b_{l}")
            pending = ([(k, _as_blocks(k, g[k])) for k in ("womix", "wmix", "wout1", "win1")], f"b{l}")
            per = small_grads(g)
            small_pack = _pack([per[n] for n in LAYER_SMALL])
        else:
            g_mix = [(k, _as_blocks(k, g[k])) for k in ("womix", "wmix")]
            (dp1, g["wout1"], dgate1), r1_mix = ffn_bwd_a(dh, acc1, gu1, m1, w["wout1"], f"ffn1_bwd_a_{l}",
                                                          rider=rs.pair(g_mix))
            g_out = [("wout1", _as_blocks("wout1", g["wout1"]))]
            per = small_grads(g)
            three = Riders([rs.after_pair(g_mix, r1_mix, f"b{l}"), rs.pair(g_out),
                            GatherRider([_pack([per[n] for n in LAYER_SMALL[1:]])])])
            (g["win1"],), got = ffn_bwd_dw(hin, dp1, m1, w["g1"], f"ffn1_bwd_dw_{l}", rider=three)
            r2_mix, r1_out, (small0_all,) = three.split(got)
            rs.after_chips(g_mix, r1_mix, r2_mix, f"b{l}")
            g_in = [("win1", _as_blocks("win1", g["win1"]))]
            both = Riders([rs.after_pair(g_out, r1_out, f"c{l}"), rs.pair(g_in)])
            nt = dh.shape[0] // TM
            cut0, cut1 = (3 * nt) // 8, nt - max(nt // 16, 1)
            part, got = ffn_bwd_dh(dh, hin, dp1, m1, w["g1"], w["win1"], f"ffn1_bwd_dh0_{l}", (0, cut0), rider=both)
            r2_out, r1_in = both.split(got)
            rs.after_chips(g_out, r1_out, r2_out, f"c{l}")
            part, r2_in = ffn_bwd_dh(dh, hin, dp1, m1, w["g1"], w["win1"], f"ffn1_bwd_dh1_{l}", (cut0, cut1), carry=part,
                                     rider=rs.after_pair(g_in, r1_in, f"d{l}"))
            rs.after_chips(g_in, r1_in, r2_in, f"d{l}")
            (dh, dmod1, g["g1"]), _ = ffn_bwd_dh(dh, hin, dp1, m1, w["g1"], w["win1"], f"ffn1_bwd_dh2_{l}", (cut1, nt),
                                                 carry=part)
            sums[l].update(rs.out)
        dmods[l] = jnp.concatenate([dmod1, dgate1, dmodm, dgatem, dmod2, dgate2], axis=1)
        grads[l] = g
    return loss, dh, jnp.stack(dmods), grads, (small0_all, small_all), sums, dgf


def _row_block(r, c, limit=262144):
    best = 8
    for rb in range(8, r + 1, 8):
        if r % rb == 0 and rb * c <= limit:
            best = rb
    return best


def pair_sum(grads, recv, c_idx, name):
    _, r, c = grads.shape
    rb = _row_block(r, c)

    def body(c_ref, g_ref, r_ref, o_ref):
        o_ref[...] = (g_ref[...] + r_ref[...]).astype(BF16)

    return pl.pallas_call(
        body, name=name,
        grid_spec=pltpu.PrefetchScalarGridSpec(
            num_scalar_prefetch=1, grid=(4, r // rb),
            in_specs=[pl.BlockSpec((1, rb, c), lambda j, i, c_ref: (2 * j + c_ref[0], i, 0)),
                      pl.BlockSpec((1, rb, c), lambda j, i, c_ref: (j, i, 0))],
            out_specs=pl.BlockSpec((1, rb, c), lambda j, i, c_ref: (j, i, 0))),
        out_shape=jax.ShapeDtypeStruct((4, r, c), BF16),
        compiler_params=_cp(2),
    )(c_idx, grads, recv)


ADA_ROWS = 16


def _silu(v):
    return v * _sigmoid(v)


def ada_fwd(cond, w_ada, b_slab, name):
    def body(c_ref, w_ref, b_ref, o_ref):
        s = _silu(c_ref[...]).astype(BF16)
        o_ref[0] = _dot(s, w_ref[0].astype(BF16)) + b_ref[0]

    return pl.pallas_call(
        body, name=name, grid=(DEPTH,),
        in_specs=[pl.BlockSpec((ADA_ROWS, D), lambda l: (0, 0)), pl.BlockSpec((1, D, ADA_SHARD), lambda l: (l, 0, 0)),
                  pl.BlockSpec((1, 1, ADA_SHARD), lambda l: (l, 0, 0))],
        out_specs=pl.BlockSpec((1, ADA_ROWS, ADA_SHARD), lambda l: (l, 0, 0)),
        out_shape=jax.ShapeDtypeStruct((DEPTH, ADA_ROWS, ADA_SHARD), F32),
        compiler_params=_cp(),
    )(cond, w_ada, b_slab)


def ada_bwd(cond, dm_sample, dm_ctx, w_ada, name):
    def body(c_ref, ds_ref, dc_ref, w_ref, gw_ref, dsc_ref):
        @pl.when(pl.program_id(0) == 0)
        def _():
            dsc_ref[...] = jnp.zeros_like(dsc_ref)

        s = _silu(c_ref[...]).astype(BF16)
        dcs = dc_ref[0]
        tot = dcs[0:1]
        for j in range(1, N_DEV):
            tot = tot + dcs[j:j + 1]
        tot8 = jnp.where(lax.broadcasted_iota(jnp.int32, (N_DEV, ADA_SHARD), 0) == 0, tot, 0.0)
        dm = jnp.concatenate([ds_ref[0], tot8], axis=0).astype(BF16)
        gw_ref[0] = _dot_tn(s, dm)
        dsc_ref[...] += _dot_nt(dm, w_ref[0].astype(BF16))[N_DEV:N_DEV + 1]

    slab = pl.BlockSpec((1, N_DEV, ADA_SHARD), lambda l: (l, 0, 0))
    wspec = pl.BlockSpec((1, D, ADA_SHARD), lambda l: (l, 0, 0))
    return pl.pallas_call(
        body, name=name, grid=(DEPTH,),
        in_specs=[pl.BlockSpec((ADA_ROWS, D), lambda l: (0, 0)), slab, slab, wspec],
        out_specs=[wspec, pl.BlockSpec((1, D), lambda l: (0, 0))],
        out_shape=[jax.ShapeDtypeStruct((DEPTH, D, ADA_SHARD), F32), jax.ShapeDtypeStruct((1, D), F32)],
        compiler_params=_cp(),
    )(cond, dm_sample, dm_ctx, w_ada)


def sum_over_devices(parts, name, silu_rows=0, w=None):
    _, r, c = parts.shape

    def body(*refs):
        p_ref, o_ref = refs[0], refs[-1]
        tot = p_ref[0]
        for j in range(1, N_DEV):
            tot = tot + p_ref[j]
        o_ref[...] = tot
        if silu_rows:
            wv = refs[1][...]
            s = _sigmoid(wv)
            o_ref[0:silu_rows, :] = tot[0:silu_rows, :] * (s * (1.0 + wv * (1.0 - s)))

    vm = pl.BlockSpec(memory_space=pltpu.VMEM)
    args = (parts,) if w is None else (parts, w)
    return pl.pallas_call(
        body, name=name, in_specs=[vm] * len(args), out_specs=vm,
        out_shape=jax.ShapeDtypeStruct((r, c), F32),
        compiler_params=pltpu.CompilerParams(vmem_limit_bytes=VMEM_LIMIT),
    )(*args)


def sum_dmods(dm_all, name):
    def body(d_ref, o_ref):
        for l in range(DEPTH):
            tot = d_ref[0, l]
            for j in range(1, N_DEV):
                tot = tot + d_ref[j, l]
            o_ref[l:l + 1, :] = tot[0:1] + tot[1:2]

    vm = pl.BlockSpec(memory_space=pltpu.VMEM)
    return pl.pallas_call(
        body, name=name, in_specs=[vm], out_specs=vm,
        out_shape=jax.ShapeDtypeStruct((DEPTH, N_MOD * D), F32),
    )(dm_all)


ADAMW_BLOCK = 512 * 1024


def adamw(w, g, m, v, name, rider=None):
    r, c = w.shape
    rb = _row_block(r, c, limit=ADAMW_BLOCK)

    def body(w_ref, g_ref, m_ref, v_ref, d_ref, nm_ref, nv_ref):
        g_ = g_ref[...]
        nm = B1 * m_ref[...] + (1.0 - B1) * g_
        nv = B2 * v_ref[...] + (1.0 - B2) * (g_ * g_)
        nm_ref[...] = nm
        nv_ref[...] = nv
        m_hat = nm / (1.0 - B1 ** STEP)
        v_hat = nv / (1.0 - B2 ** STEP)
        d_ref[...] = -LR * (m_hat / (jnp.sqrt(v_hat) + ADAM_EPS) + WD * w_ref[...])

    blk = pl.BlockSpec((rb, c), lambda i: (i, 0))
    shp = jax.ShapeDtypeStruct((r, c), F32)
    return _grid_call(body, name=name, nsteps=r // rb, in_specs=[blk] * 4, out_specs=[blk] * 3, out_shape=[shp] * 3,
                      scratch_shapes=[], args=(w, g, m, v), rider=rider)


def _adamw_math(w, g, m, v):
    nm = B1 * m + (1.0 - B1) * g
    nv = B2 * v + (1.0 - B2) * (g * g)
    m_hat = nm / (1.0 - B1 ** STEP)
    v_hat = nv / (1.0 - B2 ** STEP)
    return -LR * (m_hat / (jnp.sqrt(v_hat) + ADAM_EPS) + WD * w), nm, nv


def adamw_layers(w, parts, m, v, where, name):
    _, r, c = w.shape
    assert len(parts) == DEPTH == 2 and parts[0][0].shape == (N_DEV, r, c)
    rb = _row_block(r, c, limit=ADAMW_BLOCK // 2)
    nb = r // rb

    def body(where_ref, w_ref, a0, b0, c0, a1, b1, c1, m_ref, v_ref, go_ref, d_ref, nm_ref, nv_ref):
        def total(mine, pair, far):
            return (mine[0] + pair[0]) + ((far[0].astype(F32) + far[1].astype(F32)) + far[2].astype(F32))

        g = jnp.where(pl.program_id(0) == 0, total(a0, b0, c0), total(a1, b1, c1))
        go_ref[0], d_ref[0], nm_ref[0], nv_ref[0] = (g,) + _adamw_math(w_ref[0], g, m_ref[0], v_ref[0])

    blk = pl.BlockSpec((1, rb, c), lambda l, i, wr: (l, i, 0))

    def layer_specs(layer):
        row = (lambda l, i: jnp.where(l == 0, i, nb - 1)) if layer == 0 else (lambda l, i: jnp.where(l == 0, 0, i))
        return [pl.BlockSpec((1, rb, c), lambda l, i, wr: (wr[0], row(l, i), 0)),
                pl.BlockSpec((1, rb, c), lambda l, i, wr: (wr[1], row(l, i), 0)),
                pl.BlockSpec((3, rb, c), lambda l, i, wr: (0, row(l, i), 0))]

    shp = jax.ShapeDtypeStruct(w.shape, F32)
    return pl.pallas_call(
        body, name=name,
        grid_spec=pltpu.PrefetchScalarGridSpec(
            num_scalar_prefetch=1, grid=(DEPTH, nb),
            in_specs=[blk] + layer_specs(0) + layer_specs(1) + [blk, blk], out_specs=[blk] * 4),
        out_shape=[shp] * 4, compiler_params=_cp(2),
    )(where, w, *parts[0], *parts[1], m, v)


def _adamw_nd(w, g, m, v, name, rider=None):
    shape = w.shape
    flat = lambda a: a.reshape(-1, shape[-1])
    outs, got = adamw(flat(w), flat(g), flat(m), flat(v), name, rider=rider)
    return tuple(o.reshape(shape) for o in outs), got


LANES = 128


PACK_UNIT = 8 * LANES


ADAMW_SMALL_ROWS = 512


def _pack(arrays, row_multiple=8):
    pieces, n = [], 0
    for a in arrays:
        pieces.append(a.reshape(-1).astype(F32))
        pad = (-a.size) % PACK_UNIT
        if pad:
            pieces.append(jnp.zeros((pad,), F32))
        n += a.size + pad
    tail = (-n) % (row_multiple * LANES)
    if tail:
        pieces.append(jnp.zeros((tail,), F32))
    return jnp.concatenate(pieces).reshape(-1, LANES)


def _unpack(packed, shapes):
    out, r0 = [], 0
    lead = packed.shape[:-2]
    for shp in shapes:
        size = 1
        for s in shp:
            size *= s
        nr = 8 * -(-size // PACK_UNIT)
        blk = packed[..., r0:r0 + nr, :].reshape(lead + (nr * LANES,))[..., :size]
        out.append(blk.reshape(lead + tuple(shp)))
        r0 += nr
    return out


WEIGHTS = ["c_ctx", "w_ada", "b_ada", "ffn1_norm_g", "ffn1_w_in", "ffn1_w_out", "mix_norm_g", "w_in_mix", "lru_conv_w",
           "lru_conv_b", "lru_w_r", "lru_b_r", "lru_w_i", "lru_b_i", "lru_lambda", "sgu_norm_g", "sgu_w", "sgu_b",
           "w_out_mix", "ffn2_norm_g", "ffn2_w_in", "ffn2_w_out", "final_norm_g"]
BIG = ["w_ada", "ffn1_w_in", "ffn1_w_out", "w_in_mix", "w_out_mix", "ffn2_w_in", "ffn2_w_out"]
SHARDED_SMALL = ["lru_conv_w", "lru_b_r", "lru_b_i", "lru_lambda"]
LAYER_SMALL = ["ffn1_norm_g", "mix_norm_g", "ffn2_norm_g", "lru_conv_w", "lru_conv_b", "lru_w_r", "lru_b_r", "lru_w_i",
               "lru_b_i", "lru_lambda", "sgu_norm_g", "sgu_w", "sgu_b"]
LRU_SHARD = LRU_W // N_DEV


def _widen(a):
    return jnp.moveaxis(a, 0, -2).reshape(a.shape[1:-1] + (LRU_W,))


def kernel(x, c, ctx, c_ctx, w_ada, b_ada, ffn1_norm_g, ffn1_w_in, ffn1_w_out, mix_norm_g, w_in_mix, lru_conv_w, lru_conv_b, lru_w_r, lru_b_r, lru_w_i, lru_b_i, lru_lambda, sgu_norm_g, sgu_w, sgu_b, w_out_mix, ffn2_norm_g, ffn2_w_in, ffn2_w_out, final_norm_g, loss_target, m_c_ctx, m_w_ada, m_b_ada, m_ffn1_norm_g, m_ffn1_w_in, m_ffn1_w_out, m_mix_norm_g, m_w_in_mix, m_lru_conv_w, m_lru_conv_b, m_lru_w_r, m_lru_b_r, m_lru_w_i, m_lru_b_i, m_lru_lambda, m_sgu_norm_g, m_sgu_w, m_sgu_b, m_w_out_mix, m_ffn2_norm_g, m_ffn2_w_in, m_ffn2_w_out, m_final_norm_g, v_c_ctx, v_w_ada, v_b_ada, v_ffn1_norm_g, v_ffn1_w_in, v_ffn1_w_out, v_mix_norm_g, v_w_in_mix, v_lru_conv_w, v_lru_conv_b, v_lru_w_r, v_lru_b_r, v_lru_w_i, v_lru_b_i, v_lru_lambda, v_sgu_norm_g, v_sgu_w, v_sgu_b, v_w_out_mix, v_ffn2_norm_g, v_ffn2_w_in, v_ffn2_w_out, v_final_norm_g):
    given = dict(locals())
    W = {n: given[n] for n in WEIGHTS}
    M = {n: given["m_" + n] for n in WEIGHTS}
    V = {n: given["v_" + n] for n in WEIGHTS}
    xi, yi, ci = _position()
    me = 4 * xi + 2 * yi + ci
    chip = 2 * xi + yi

    shards = []
    tr = lambda a: jnp.swapaxes(a, 1, 2)
    for l in range(DEPTH):
        sh = dict(win1=tr(ffn1_w_in)[l], wout1=ffn1_w_out[l], wmix=w_in_mix[l], womix=w_out_mix[l], win2=tr(ffn2_w_in)[l],
                  wout2=ffn2_w_out[l])
        shards.append({k: a.astype(BF16) for k, a in sh.items()})

    sharded_shapes = [W[n].shape for n in SHARDED_SMALL]
    both = Riders([GatherRider([_pack([c[0]] + [W[n] for n in SHARDED_SMALL])]),
                   GatherRider([shards[0][k] for _, k in FIRST_WEIGHTS])])
    (got,), first_weights = both.split(run_alone(both, pl.ANY, "gather_first"))
    parts = _unpack(got, [(D,)] + sharded_shapes)
    c_all = parts[0]
    wide = {n: _widen(a) for n, a in zip(SHARDED_SMALL, parts[1:])}
    cond = jnp.concatenate([c_all, c_ctx[None, :], jnp.zeros((ADA_ROWS - N_DEV - 1, D), F32)], axis=0)
    b_slab = lax.dynamic_slice_in_dim(b_ada, me * ADA_SHARD, ADA_SHARD, axis=1)[:, None, :]
    slabs = ada_fwd(cond, w_ada, b_slab, "ada_fwd")
    mall = run_alone(GatherRider([slabs.reshape(DEPTH * ADA_ROWS, ADA_SHARD)]), pltpu.VMEM, "gather_mod")[0]
    mall = mall.reshape(N_DEV, DEPTH, ADA_ROWS, ADA_SHARD)
    m_sample = lax.dynamic_index_in_dim(mall, me, axis=2, keepdims=False)
    m_ctx = mall[:, :, N_DEV, :]
    mods = jnp.stack([jnp.transpose(m, (1, 0, 2)).reshape(DEPTH, N_MOD, D) for m in (m_ctx, m_sample)], axis=1)

    smalls = []
    for l in range(DEPTH):
        smalls.append(small_layer(ffn1_norm_g[l], mix_norm_g[l], ffn2_norm_g[l], wide["lru_conv_w"][l], lru_conv_b[l],
                                  lru_w_r[l], wide["lru_b_r"][l], lru_w_i[l], wide["lru_b_i"][l], wide["lru_lambda"][l],
                                  sgu_norm_g[l], sgu_w[l], sgu_b[l]))

    c_idx = ci.reshape(1).astype(jnp.int32)
    where = jnp.stack([me, chip]).astype(jnp.int32)
    loss_blk, dx, dmods, grads, (small0_all, small1_all), gsum, dgf = fwd_bwd(
        ctx[0], x[0], loss_target[0], mods, shards, first_weights, smalls, final_norm_g[None, :], c_idx, where)
    smalls_shape = {n: (W[n].shape[1:-1] + (LRU_W,)) if n in SHARDED_SMALL else W[n].shape[1:] for n in LAYER_SMALL}
    G, delta, new_m, new_v = {}, {}, {}, {}
    for key, n in (("win1", "ffn1_w_in"), ("wout1", "ffn1_w_out"), ("wmix", "w_in_mix"), ("womix", "w_out_mix"),
                   ("win2", "ffn2_w_in"), ("wout2", "ffn2_w_out")):
        t_in = tr if key in ("win1", "win2") else (lambda a: a)
        outs = adamw_layers(t_in(W[n]), [gsum[l][key] for l in range(DEPTH)], t_in(M[n]), t_in(V[n]), where,
                            f"adamw_{n}")
        G[n], delta[n], new_m[n], new_v[n] = [t_in(o) for o in outs]

    n_rows = DEPTH * 2 * N_MOD
    dm_rows = jnp.concatenate([dmods.reshape(n_rows, D), jnp.zeros((-n_rows % 8, D), F32)], axis=0)
    dm_all = run_alone(GatherRider([dm_rows]), pltpu.VMEM, "gather_dmod")[0][:, :n_rows]
    dm_all = dm_all.reshape(N_DEV, DEPTH, 2, N_MOD * D)
    mine = lax.dynamic_slice_in_dim(dm_all, me * ADA_SHARD, ADA_SHARD, axis=3)
    G["w_ada"], dsc = ada_bwd(cond, jnp.transpose(mine[:, :, 1, :], (1, 0, 2)), jnp.transpose(mine[:, :, 0, :], (1, 0, 2)),
                              w_ada, "ada_bwd")
    G["b_ada"] = sum_dmods(dm_all, "sum_dmods")
    (delta["w_ada"], new_m["w_ada"], new_v["w_ada"]), _ = _adamw_nd(w_ada, G["w_ada"], m_w_ada, v_w_ada, "adamw_w_ada")

    head_all = run_alone(GatherRider([_pack([dsc[0], dgf[0], grads[0]["g1"][0]])]), pltpu.VMEM, "gather_head_grads")[0]
    head = _unpack(sum_over_devices(head_all, "sum_head_grads", silu_rows=D // LANES, w=c_ctx.reshape(D // LANES, LANES)),
                   [(D,), (D,), (D,)])
    shapes = [smalls_shape[n] for n in LAYER_SMALL]
    sum0 = [head[2]] + _unpack(sum_over_devices(small0_all, "sum_small_grads_0"), shapes[1:])
    sum1 = _unpack(sum_over_devices(small1_all, "sum_small_grads_1"), shapes)
    G["c_ctx"], G["final_norm_g"] = head[0], head[1]
    for n, a0, a1 in zip(LAYER_SMALL, sum0, sum1):
        a = jnp.stack([a0, a1])
        G[n] = lax.dynamic_slice_in_dim(a, me * LRU_SHARD, LRU_SHARD, axis=a.ndim - 1) if n in SHARDED_SMALL else a

    rest = [n for n in WEIGHTS if n not in BIG]
    shapes = [W[n].shape for n in rest]
    outs, _ = adamw(*[_pack([src[n] for n in rest], row_multiple=ADAMW_SMALL_ROWS) for src in (W, G, M, V)], "adamw_small")
    for dst, packed in zip((delta, new_m, new_v), outs):
        for n, a in zip(rest, _unpack(packed, shapes)):
            dst[n] = a

    loss = lax.psum(loss_blk[0, 0], ("x", "y", "c"))
    grad_x = dx[None]
    return (loss, grad_x, *[G[n] for n in WEIGHTS], *[delta[n] for n in WEIGHTS], *[new_m[n] for n in WEIGHTS],
            *[new_v[n] for n in WEIGHTS])
```

```python
import functools

import jax
import jax.numpy as jnp
from jax import lax
from jax.experimental import pallas as pl
from jax.experimental.pallas import tpu as pltpu

F32 = jnp.float32
BF16 = jnp.bfloat16

D = 1024
CTX = 256
DEPTH = 2
EPS = 1e-6
D_FF = 2816
LRU_W = 512
HEADS = 8
HEAD_DIM = 64
CONV_W = 4
RG_C = 8.0
GROUPS = 4
GROUP_DIM = 128
CHUNK = 128
MLP_W = 512
IN_PROJ = 2048
N_MOD = 9
N_DEV = 8

LR = 0.001
B1 = 0.9
B2 = 0.999
ADAM_EPS = 1e-08
WD = 0.01
STEP = 10

FF_IN_SHARD = 2 * D_FF // N_DEV
FF_OUT_SHARD = D_FF // N_DEV
HT = 256
WT = 512
N_MIX_SHARD = IN_PROJ // N_DEV
OMIX_SHARD = D // N_DEV
ADA_SHARD = N_MOD * D // N_DEV

TM = 256
HALO = 8
VMEM_LIMIT = 60 * 1024 * 1024

MESH = pl.DeviceIdType.MESH
ANY = pl.BlockSpec(memory_space=pl.ANY)


def _cp(n_axes=1):
    return pltpu.CompilerParams(dimension_semantics=("arbitrary",) * n_axes, vmem_limit_bytes=VMEM_LIMIT)


def _position():
    return lax.axis_index("x"), lax.axis_index("y"), lax.axis_index("c")


class GatherRider:
    def __init__(self, shards):
        n = len(shards)
        self.n = n
        self.ins = list(shards)
        self.out_shape = [jax.ShapeDtypeStruct((N_DEV,) + s.shape, s.dtype) for s in shards]
        self.sems = [pltpu.SemaphoreType.DMA((n, 7)), pltpu.SemaphoreType.DMA((n, 7)), pltpu.SemaphoreType.DMA((n,))]

    def _ctx(self, outs, sems):
        x, y, c = _position()
        chips = [(1 - x, y), (x, 1 - y), (1 - x, 1 - y)]

        def copy(t, k, block, to, src=None):
            dst = outs[t].at[4 * block[0] + 2 * block[1] + block[2]]
            return pltpu.make_async_remote_copy(
                src_ref=dst if src is None else src, dst_ref=dst, send_sem=sems[0].at[t, k],
                recv_sem=sems[1].at[t, k], device_id=to, device_id_type=MESH)

        return (x, y, c), (x, y, 1 - c), chips, copy

    def _local(self, ins, outs, sems, t):
        x, y, c = _position()
        return pltpu.make_async_copy(ins[t], outs[t].at[4 * x + 2 * y + c], sems[2].at[t])

    def _first(self, ins, outs, sems, t):
        me, sibling, chips, copy = self._ctx(outs, sems)
        return [copy(t, 0, me, sibling, src=ins[t])] + [copy(t, 1 + j, me, (*chip, me[2]), src=ins[t])
                                                         for j, chip in enumerate(chips)]

    def start(self, ins, outs, sems):
        for t in range(self.n):
            self._local(ins, outs, sems, t).start()
            for cp in self._first(ins, outs, sems, t):
                cp.start()

    def mid(self, ins, outs, sems):
        me, sibling, chips, copy = self._ctx(outs, sems)
        for j, chip in enumerate(chips):
            for t in range(self.n):
                copy(t, 1 + j, (*chip, me[2]), me).wait_recv()
                copy(t, 4 + j, (*chip, me[2]), sibling).start()

    def finish(self, ins, outs, sems):
        me, sibling, chips, copy = self._ctx(outs, sems)
        for t in range(self.n):
            copy(t, 0, sibling, me).wait_recv()
            for j, chip in enumerate(chips):
                copy(t, 4 + j, (*chip, 1 - me[2]), me).wait_recv()
        for t in range(self.n):
            for cp in self._first(ins, outs, sems, t):
                cp.wait_send()
            for j, chip in enumerate(chips):
                copy(t, 4 + j, (*chip, me[2]), sibling).wait_send()
            self._local(ins, outs, sems, t).wait()


class ExchangeRider:
    def __init__(self, tensors, plan, n_slots):
        n = len(tensors)
        self.n, self.plan = n, plan
        self.ins = list(tensors)
        self.out_shape = [jax.ShapeDtypeStruct((n_slots,) + s.shape[1:], s.dtype) for s in tensors]
        self.sems = [pltpu.SemaphoreType.DMA((n, n_slots)), pltpu.SemaphoreType.DMA((n, n_slots))]

    def _copies(self, ins, outs, sems):
        return [pltpu.make_async_remote_copy(
            src_ref=ins[t].at[block], dst_ref=outs[t].at[k], send_sem=sems[0].at[t, k], recv_sem=sems[1].at[t, k],
            device_id=to, device_id_type=MESH)
            for t in range(self.n) for k, (block, to) in enumerate(self.plan(*_position()))]

    def start(self, ins, outs, sems):
        for cp in self._copies(ins, outs, sems):
            cp.start()

    def mid(self, ins, outs, sems):
        pass

    def finish(self, ins, outs, sems):
        for cp in self._copies(ins, outs, sems):
            cp.wait()


class Riders:
    def __init__(self, riders):
        self.riders = list(riders)
        self.ins = [a for r in self.riders for a in r.ins]
        self.out_shape = [s for r in self.riders for s in r.out_shape]
        self.sems = [s for r in self.riders for s in r.sems]

    def _each(self, ins, outs, sems):
        i = o = s = 0
        for r in self.riders:
            ni, no, ns = len(r.ins), len(r.out_shape), len(r.sems)
            yield r, ins[i:i + ni], outs[o:o + no], sems[s:s + ns]
            i, o, s = i + ni, o + no, s + ns

    def start(self, ins, outs, sems):
        for r, a, b, c in self._each(ins, outs, sems):
            r.start(a, b, c)

    def mid(self, ins, outs, sems):
        for r, a, b, c in self._each(ins, outs, sems):
            r.mid(a, b, c)

    def finish(self, ins, outs, sems):
        for r, a, b, c in self._each(ins, outs, sems):
            r.finish(a, b, c)

    def split(self, outs):
        res, o = [], 0
        for r in self.riders:
            res.append(list(outs[o:o + len(r.out_shape)]))
            o += len(r.out_shape)
        return res


def pair_rider(grads):
    def plan(x, y, c):
        return [(4 * cx + 2 * cy + (1 - c), (x, y, 1 - c)) for cx in range(2) for cy in range(2)]
    return ExchangeRider(grads, plan, 4)


def chips_rider(parts):
    def plan(x, y, c):
        return [(2 * cx + cy, (cx, cy, c)) for cx, cy in [(1 - x, y), (x, 1 - y), (1 - x, 1 - y)]]
    return ExchangeRider(parts, plan, 3)


def run_alone(rider, space, name):
    ni = len(rider.ins)
    no = len(rider.out_shape)

    def body(*refs):
        ins, outs, sems = refs[:ni], refs[ni:ni + no], refs[ni + no:]
        rider.start(ins, outs, sems)
        rider.mid(ins, outs, sems)
        rider.finish(ins, outs, sems)

    spec = pl.BlockSpec(memory_space=space)
    return pl.pallas_call(
        body, name=name, in_specs=[spec] * ni, out_specs=[spec] * no, out_shape=rider.out_shape,
        scratch_shapes=rider.sems, compiler_params=pltpu.CompilerParams(vmem_limit_bytes=VMEM_LIMIT),
    )(*rider.ins)


def _grid_call(body, *, name, nsteps, in_specs, out_specs, out_shape, scratch_shapes, args, rider=None, aliases=None):
    aliases = aliases or {}
    if rider is None:
        outs = pl.pallas_call(body, name=name, grid=(nsteps,), in_specs=in_specs, out_specs=out_specs,
                              out_shape=out_shape, scratch_shapes=scratch_shapes, input_output_aliases=aliases,
                              compiler_params=_cp())(*args)
        return outs, []
    ni, no, ns = len(in_specs), len(out_specs), len(scratch_shapes)
    ri, ro = len(rider.ins), len(rider.out_shape)

    def wrapped(*refs):
        ins, refs = refs[:ni], refs[ni:]
        r_ins, refs = refs[:ri], refs[ri:]
        outs, refs = refs[:no], refs[no:]
        r_outs, refs = refs[:ro], refs[ro:]
        scratch, r_sems = refs[:ns], refs[ns:]
        s = pl.program_id(0)

        @pl.when(s == 0)
        def _():
            rider.start(r_ins, r_outs, r_sems)

        body(*ins, *outs, *scratch)

        @pl.when(s == (3 * nsteps) // 4)
        def _():
            rider.mid(r_ins, r_outs, r_sems)

        @pl.when(s == nsteps - 1)
        def _():
            rider.finish(r_ins, r_outs, r_sems)

    outs = pl.pallas_call(
        wrapped, name=name, grid=(nsteps,), in_specs=list(in_specs) + [ANY] * ri, out_specs=list(out_specs) + [ANY] * ro,
        out_shape=list(out_shape) + rider.out_shape, scratch_shapes=list(scratch_shapes) + rider.sems,
        input_output_aliases=aliases, compiler_params=_cp())(*args, *rider.ins)
    return outs[:no], outs[no:]


def _dot(a, b):
    return jnp.dot(a, b, preferred_element_type=F32)


def _dot_nt(a, b):
    return lax.dot_general(a, b, (((1,), (1,)), ((), ())), preferred_element_type=F32)


def _dot_tn(a, b):
    return lax.dot_general(a, b, (((0,), (0,)), ((), ())), preferred_element_type=F32)


def _sigmoid(x):
    return 1.0 / (1.0 + jnp.exp(-x))


def _kind(i):
    return jnp.where(i < CTX // TM, 0, 1)


def _sel(kind, mod_ref, k):
    return mod_ref[kind, k:k + 1, :]


def _acc2(ref, k, val, kind):
    ref[kind, k:k + 1, :] += jnp.sum(val, axis=0, keepdims=True)


def _norm_mod(h, g, shift, scale):
    r = lax.rsqrt(jnp.mean(h * h, axis=-1, keepdims=True) + EPS)
    n = h * r
    return (n * g) * (1.0 + scale) + shift, n, r


def _norm_mod_bwd(dz, n, r, g, scale):
    dn = dz * (g * (1.0 + scale))
    return r * (dn - n * jnp.mean(dn * n, axis=-1, keepdims=True))


def ffn_fwd(h, mod, g, win, wout, name, rider=None, loss=None):
    split = isinstance(h, tuple)
    nc = CTX // TM
    t = h[0].shape[0] + h[1].shape[0] if split else h.shape[0]

    def body(*refs):
        refs = list(refs)
        win_v, wout_v, a_v = refs[-3:]
        rows = refs[:2] if split else refs[:1]
        mod_ref, g_ref, win_hbm, wout_hbm = refs[len(rows):len(rows) + 4]
        rest = refs[len(rows) + 4:-3]
        if loss is not None:
            fg_ref, tgt_ref, rest = rest[0], rest[1], rest[2:]
        out_ref, gu_ref, acc_ref, rest = rest[0], rest[1], rest[2], rest[3:]
        i = pl.program_id(0)

        @pl.when(i == 0)
        def _():
            pltpu.sync_copy(win_hbm, win_v)
            pltpu.sync_copy(wout_hbm, wout_v)

        if split:
            hh = jnp.where(i < nc, rows[0][...], rows[1][...])
            rest[0][...] = hh
        else:
            hh = rows[0][...]
        ic = _kind(i)
        z, _, _ = _norm_mod(hh, g_ref[...], _sel(ic, mod_ref, 0), _sel(ic, mod_ref, 1))
        zb = z.astype(BF16)
        for j in range(D_FF // HT):
            gb, ub = slice(j * HT, (j + 1) * HT), slice(D_FF + j * HT, D_FF + (j + 1) * HT)
            gg = _dot_nt(zb, win_v[gb, :])
            uu = _dot_nt(zb, win_v[ub, :])
            gu_ref[:, gb] = gg.astype(BF16)
            gu_ref[:, ub] = uu.astype(BF16)
            a_v[:, gb] = ((gg * _sigmoid(gg)) * uu).astype(BF16)
        acc = _dot(a_v[...], wout_v[...])
        acc_ref[...] = acc
        hn = hh + (0.5 * _sel(ic, mod_ref, 2)) * acc
        if loss is None:
            out_ref[...] = hn
        else:
            loss_ref, dgf_ref = rest

            @pl.when(i == 0)
            def _():
                loss_ref[...] = jnp.zeros_like(loss_ref)
                dgf_ref[...] = jnp.zeros_like(dgf_ref)

            @pl.when(i < nc)
            def _():
                out_ref[...] = jnp.zeros_like(out_ref)

            @pl.when(i >= nc)
            def _():
                gain = fg_ref[...]
                r = lax.rsqrt(jnp.mean(hn * hn, axis=-1, keepdims=True) + EPS)
                n = hn * r
                err = n * gain - tgt_ref[...]
                loss_ref[...] += 0.5 * jnp.sum(jnp.mean(err * err, axis=-1, keepdims=True))
                dy = err * (1.0 / D)
                dgf_ref[...] += jnp.sum(dy * n, axis=0, keepdims=True)
                dn = dy * gain
                out_ref[...] = r * (dn - n * jnp.mean(dn * n, axis=-1, keepdims=True))

    row = pl.BlockSpec((TM, D), lambda i: (i, 0))
    vec = pl.BlockSpec((1, D), lambda i: (0, 0))
    rshape = jax.ShapeDtypeStruct((t, D), F32)
    if split:
        rows_in = [pl.BlockSpec((TM, D), lambda i: (jnp.minimum(i, nc - 1), 0)),
                   pl.BlockSpec((TM, D), lambda i: (jnp.maximum(i - nc, 0), 0))]
    else:
        rows_in = [row]
    in_specs = rows_in + [pl.BlockSpec((2, 3, D), lambda i: (0, 0, 0)), vec, ANY, ANY]
    out_specs = [row, pl.BlockSpec((TM, 2 * D_FF), lambda i: (i, 0)), row] + ([row] if split else [])
    out_shape = [rshape, jax.ShapeDtypeStruct((t, 2 * D_FF), BF16), rshape] + ([rshape] if split else [])
    args = (*(h if split else (h,)), mod, g, win, wout)
    if loss is not None:
        in_specs += [vec, pl.BlockSpec((TM, D), lambda i: (jnp.maximum(i - nc, 0), 0))]
        out_specs += [pl.BlockSpec((8, 128), lambda i: (0, 0)), vec]
        out_shape += [jax.ShapeDtypeStruct((8, 128), F32), jax.ShapeDtypeStruct((1, D), F32)]
        args += tuple(loss)
    return _grid_call(
        body, name=name, nsteps=t // TM, in_specs=in_specs, out_specs=out_specs, out_shape=out_shape,
        scratch_shapes=[pltpu.VMEM((2 * D_FF, D), BF16), pltpu.VMEM((D_FF, D), BF16), pltpu.VMEM((TM, D_FF), BF16)],
        args=args, rider=rider)


def ffn_bwd_a(dy, acc, gu, mod, wout, name, rider=None):
    t = dy.shape[0]
    nt = t // TM

    def body(dy_ref, acc_ref, gu_ref, mod_ref, wout_hbm, dp_ref, dwout_hbm, dgate_ref, wout_v, dwout_v):
        i = pl.program_id(0)

        @pl.when(i == 0)
        def _():
            pltpu.sync_copy(wout_hbm, wout_v)
            dwout_v[...] = jnp.zeros_like(dwout_v)
            dgate_ref[...] = jnp.zeros_like(dgate_ref)

        dy_ = dy_ref[...]
        ic = _kind(i)
        _acc2(dgate_ref, 0, 0.5 * dy_ * acc_ref[...], ic)
        daccb = ((0.5 * _sel(ic, mod_ref, 2)) * dy_).astype(BF16)
        for j in range(D_FF // HT):
            blk, ublk = slice(j * HT, (j + 1) * HT), slice(D_FF + j * HT, D_FF + (j + 1) * HT)
            da = _dot_nt(daccb, wout_v[blk, :])
            gg = gu_ref[:, blk].astype(F32)
            uu = gu_ref[:, ublk].astype(F32)
            s = _sigmoid(gg)
            sl = gg * s
            dwout_v[blk, :] += _dot_tn((sl * uu).astype(BF16), daccb)
            dp_ref[:, blk] = (da * uu * (s + sl * (1.0 - s))).astype(BF16)
            dp_ref[:, ublk] = (da * sl).astype(BF16)

        @pl.when(i == nt - 1)
        def _():
            pltpu.sync_copy(dwout_v, dwout_hbm)

    row = pl.BlockSpec((TM, D), lambda i: (i, 0))
    wide = pl.BlockSpec((TM, 2 * D_FF), lambda i: (i, 0))
    return _grid_call(
        body, name=name, nsteps=nt,
        in_specs=[row, row, wide, pl.BlockSpec((2, 3, D), lambda i: (0, 0, 0)), ANY],
        out_specs=[wide, ANY, pl.BlockSpec((2, 1, D), lambda i: (0, 0, 0))],
        out_shape=[jax.ShapeDtypeStruct((t, 2 * D_FF), BF16), jax.ShapeDtypeStruct((D_FF, D), F32),
                   jax.ShapeDtypeStruct((2, 1, D), F32)],
        scratch_shapes=[pltpu.VMEM((D_FF, D), BF16), pltpu.VMEM((D_FF, D), F32)],
        args=(dy, acc, gu, mod, wout), rider=rider)


def ffn_bwd_b(dy, h, dp, mod, g, win, name, rider=None, latent_only=False):
    t = dy.shape[0]
    nt = t // TM
    nc = CTX // TM

    def body(dy_ref, h_ref, dp_ref, mod_ref, g_ref, win_hbm, dh_ref, dwin_hbm, dmod_ref, dg_ref, win_v, dwin_v):
        i = pl.program_id(0)

        @pl.when(i == 0)
        def _():
            pltpu.sync_copy(win_hbm, win_v)
            dwin_v[...] = jnp.zeros_like(dwin_v)
            dmod_ref[...] = jnp.zeros_like(dmod_ref)
            dg_ref[...] = jnp.zeros_like(dg_ref)

        ic = _kind(i)
        gain = g_ref[...]
        scale = _sel(ic, mod_ref, 1)
        z, n, r = _norm_mod(h_ref[...], gain, _sel(ic, mod_ref, 0), scale)
        zb = z.astype(BF16)
        dz = _dot(dp_ref[...], win_v[...])
        for j in range(2 * D_FF // WT):
            blk = slice(j * WT, (j + 1) * WT)
            dwin_v[blk, :] += _dot_tn(dp_ref[:, blk], zb)
        _acc2(dmod_ref, 0, dz, ic)
        _acc2(dmod_ref, 1, dz * (n * gain), ic)
        dg_ref[...] += jnp.sum(dz * (1.0 + scale) * n, axis=0, keepdims=True)
        dh_ref[...] = dy_ref[...] + _norm_mod_bwd(dz, n, r, gain, scale)

        @pl.when(i == nt - 1)
        def _():
            pltpu.sync_copy(dwin_v, dwin_hbm)

    row = pl.BlockSpec((TM, D), lambda i: (i, 0))
    if latent_only:
        dh_spec = pl.BlockSpec((TM, D), lambda i: (jnp.maximum(i - nc, 0), 0))
        dh_shape = jax.ShapeDtypeStruct((t - CTX, D), F32)
    else:
        dh_spec, dh_shape = row, jax.ShapeDtypeStruct((t, D), F32)
    return _grid_call(
        body, name=name, nsteps=nt,
        in_specs=[row, row, pl.BlockSpec((TM, 2 * D_FF), lambda i: (i, 0)),
                  pl.BlockSpec((2, 3, D), lambda i: (0, 0, 0)), pl.BlockSpec((1, D), lambda i: (0, 0)), ANY],
        out_specs=[dh_spec, ANY, pl.BlockSpec((2, 2, D), lambda i: (0, 0, 0)), pl.BlockSpec((1, D), lambda i: (0, 0))],
        out_shape=[dh_shape, jax.ShapeDtypeStruct((2 * D_FF, D), F32),
                   jax.ShapeDtypeStruct((2, 2, D), F32), jax.ShapeDtypeStruct((1, D), F32)],
        scratch_shapes=[pltpu.VMEM((2 * D_FF, D), BF16), pltpu.VMEM((2 * D_FF, D), F32)],
        args=(dy, h, dp, mod, g, win), rider=rider)


def ffn_bwd_dw(h, dp, mod, g, name, rider=None):
    t = h.shape[0]
    nt = t // TM

    def body(h_ref, dp_ref, mod_ref, g_ref, dwin_hbm, dwin_v):
        i = pl.program_id(0)

        @pl.when(i == 0)
        def _():
            dwin_v[...] = jnp.zeros_like(dwin_v)

        ic = _kind(i)
        z, _, _ = _norm_mod(h_ref[...], g_ref[...], _sel(ic, mod_ref, 0), _sel(ic, mod_ref, 1))
        zb = z.astype(BF16)
        for j in range(2 * D_FF // WT):
            blk = slice(j * WT, (j + 1) * WT)
            dwin_v[blk, :] += _dot_tn(dp_ref[:, blk], zb)

        @pl.when(i == nt - 1)
        def _():
            pltpu.sync_copy(dwin_v, dwin_hbm)

    return _grid_call(
        body, name=name, nsteps=nt,
        in_specs=[pl.BlockSpec((TM, D), lambda i: (i, 0)), pl.BlockSpec((TM, 2 * D_FF), lambda i: (i, 0)),
                  pl.BlockSpec((2, 3, D), lambda i: (0, 0, 0)), pl.BlockSpec((1, D), lambda i: (0, 0))],
        out_specs=[ANY], out_shape=[jax.ShapeDtypeStruct((2 * D_FF, D), F32)],
        scratch_shapes=[pltpu.VMEM((2 * D_FF, D), F32)],
        args=(h, dp, mod, g), rider=rider)


def ffn_bwd_dh(dy, h, dp, mod, g, win, name, tiles, carry=None, rider=None):
    t = dy.shape[0]
    nc = CTX // TM
    t0, t1 = tiles

    def body(*refs):
        if carry is None:
            dy_ref, h_ref, dp_ref, mod_ref, g_ref, win_hbm, dh_ref, dmod_ref, dg_ref, win_v = refs
        else:
            dy_ref, h_ref, dp_ref, mod_ref, g_ref, win_hbm, _, dmod0_ref, dg0_ref, dh_ref, dmod_ref, dg_ref, win_v = refs
        i = pl.program_id(0)

        @pl.when(i == 0)
        def _():
            pltpu.sync_copy(win_hbm, win_v)
            dmod_ref[...] = jnp.zeros_like(dmod_ref) if carry is None else dmod0_ref[...]
            dg_ref[...] = jnp.zeros_like(dg_ref) if carry is None else dg0_ref[...]

        ic = _kind(i + t0)
        gain = g_ref[...]
        scale = _sel(ic, mod_ref, 1)
        _, n, r = _norm_mod(h_ref[...], gain, _sel(ic, mod_ref, 0), scale)
        dz = _dot(dp_ref[...], win_v[...])
        _acc2(dmod_ref, 0, dz, ic)
        _acc2(dmod_ref, 1, dz * (n * gain), ic)
        dg_ref[...] += jnp.sum(dz * (1.0 + scale) * n, axis=0, keepdims=True)
        dh_ref[...] = dy_ref[...] + _norm_mod_bwd(dz, n, r, gain, scale)

    row = pl.BlockSpec((TM, D), lambda i: (i + t0, 0))
    small = [pl.BlockSpec((2, 2, D), lambda i: (0, 0, 0)), pl.BlockSpec((1, D), lambda i: (0, 0))]
    in_specs = [row, row, pl.BlockSpec((TM, 2 * D_FF), lambda i: (i + t0, 0)),
                pl.BlockSpec((2, 3, D), lambda i: (0, 0, 0)), pl.BlockSpec((1, D), lambda i: (0, 0)), ANY]
    args = (dy, h, dp, mod, g, win)
    if carry is not None:
        in_specs += [ANY] + small
        args += tuple(carry)
    return _grid_call(
        body, name=name, nsteps=t1 - t0, in_specs=in_specs,
        out_specs=[pl.BlockSpec((TM, D), lambda i: (jnp.maximum(i + t0 - nc, 0), 0))] + small,
        out_shape=[jax.ShapeDtypeStruct((t - CTX, D), F32), jax.ShapeDtypeStruct((2, 2, D), F32),
                   jax.ShapeDtypeStruct((1, D), F32)],
        scratch_shapes=[pltpu.VMEM((2 * D_FF, D), BF16)],
        args=args, rider=rider, aliases=None if carry is None else {6: 0})


def mix_in_fwd(h, mod, g, wmix, name):
    t = h.shape[0]

    def body(h_ref, mod_ref, g_ref, w_hbm, p_ref, w_v):
        i = pl.program_id(0)

        @pl.when(i == 0)
        def _():
            pltpu.sync_copy(w_hbm, w_v)

        ic = _kind(i)
        z, _, _ = _norm_mod(h_ref[...], g_ref[...], _sel(ic, mod_ref, 0), _sel(ic, mod_ref, 1))
        zb = z.astype(BF16)
        for dd in range(N_DEV):
            p_ref[:, dd * N_MIX_SHARD:(dd + 1) * N_MIX_SHARD] = _dot(zb, w_v[dd])

    return pl.pallas_call(
        body, name=name, grid=(t // TM,),
        in_specs=[pl.BlockSpec((TM, D), lambda i: (i, 0)), pl.BlockSpec((2, 3, D), lambda i: (0, 0, 0)),
                  pl.BlockSpec((1, D), lambda i: (0, 0)), ANY],
        out_specs=pl.BlockSpec((TM, IN_PROJ), lambda i: (i, 0)),
        out_shape=jax.ShapeDtypeStruct((t, IN_PROJ), F32),
        scratch_shapes=[pltpu.VMEM((N_DEV, D, N_MIX_SHARD), BF16)],
        compiler_params=_cp(),
    )(h, mod, g, wmix)


def _halo_specs(nt, tile_of):
    nb = nt * (TM // HALO)
    main = pl.BlockSpec((TM, LRU_W), lambda s: (tile_of(s), 0))
    prev = pl.BlockSpec((HALO, LRU_W), lambda s: (jnp.maximum(tile_of(s) * (TM // HALO) - 1, 0), 0))
    nxt = pl.BlockSpec((HALO, LRU_W), lambda s: (jnp.minimum((tile_of(s) + 1) * (TM // HALO), nb - 1), 0))
    return main, prev, nxt


def _ext(tile, nt, main, prev, nxt):
    has_prev = jnp.logical_and(tile != 0, tile != 1)
    has_next = jnp.logical_and(tile != 0, tile != nt - 1)
    return jnp.concatenate([jnp.where(has_prev, prev, 0.0), main, jnp.where(has_next, nxt, 0.0)], axis=0)


def _shifted(ext, off):
    n = ext.shape[0]
    return pltpu.roll(ext, (-off) % n, 0)[HALO:HALO + TM]


def _conv(ext, cw_ref, cb_ref):
    xc = cb_ref[...] + cw_ref[0:1, :] * _shifted(ext, -2)
    for k in range(1, CONV_W):
        xc = xc + cw_ref[k:k + 1, :] * _shifted(ext, k - 2)
    return xc


def _log1p(y):
    return jnp.where(y < 1e-2, y * (1.0 - y * (0.5 - y * (1.0 / 3.0 - 0.25 * y))), jnp.log(1.0 + y))


def _softplus_neg(lam):
    return jnp.maximum(-lam, 0.0) + _log1p(jnp.exp(-jnp.abs(lam)))


def _one_minus_exp(x, exp_half):
    p = x * (1.0 + x * (1 / 2 + x * (1 / 6 + x * (1 / 24))))
    return jnp.where(x > -0.1, -p, 1.0 - exp_half * exp_half)


def _gates(xc, wr, br, wi, bi, lam):
    xb = xc.astype(BF16)
    r = _sigmoid(_dot(xb, wr) + br)
    ig = _sigmoid(_dot(xb, wi) + bi)
    sp = _softplus_neg(lam)
    log_a = -RG_C * r * sp
    a = jnp.exp(log_a)
    mult = jnp.sqrt(_one_minus_exp(2.0 * log_a, a))
    return r, ig, sp, a, mult


def _scan(a, b, reverse):
    n = a.shape[0]
    row = lax.broadcasted_iota(jnp.int32, a.shape, 0)
    s = 1
    while s < n:
        if s < HALO:
            if reverse:
                keep = row < n - s
                a_s = jnp.where(keep, pltpu.roll(a, n - s, 0), 1.0)
                b_s = jnp.where(keep, pltpu.roll(b, n - s, 0), 0.0)
            else:
                keep = row >= s
                a_s = jnp.where(keep, pltpu.roll(a, s, 0), 1.0)
                b_s = jnp.where(keep, pltpu.roll(b, s, 0), 0.0)
            b = a * b_s + b
            a = a * a_s
        elif reverse:
            b = jnp.concatenate([a[:n - s] * b[s:] + b[:n - s], b[n - s:]], axis=0)
            a = jnp.concatenate([a[:n - s] * a[s:], a[n - s:]], axis=0)
        else:
            b = jnp.concatenate([b[:s], a[s:] * b[:n - s] + b[s:]], axis=0)
            a = jnp.concatenate([a[:s], a[s:] * a[:n - s]], axis=0)
        s *= 2
    return a, b


def lru_fwd(p, conv_w, conv_b, wr, br, wi, bi, lam, reverse, name, rider=None):
    t = p.shape[0]
    nt = t // TM

    def tile_of(s):
        return jnp.where(s == 0, 0, nt - s) if reverse else s

    def body(x_ref, xp_ref, xn_ref, cw_ref, cb_ref, wr_ref, br_ref, wi_ref, bi_ref, lam_ref, h_ref, carry):
        s = pl.program_id(0)
        tile = tile_of(s)

        @pl.when(s == 0)
        def _():
            carry[...] = jnp.zeros_like(carry)

        ext = _ext(tile, nt, x_ref[...], xp_ref[...], xn_ref[...])
        xc = _conv(ext, cw_ref, cb_ref)
        _, ig, _, a, mult = _gates(xc, wr_ref[...], br_ref[...], wi_ref[...], bi_ref[...], lam_ref[...])
        a_cum, hl = _scan(a, mult * (ig * xc), reverse)
        hh = hl + a_cum * carry[...]
        h_ref[...] = hh
        carry[...] = hh[0:1, :] if reverse else hh[TM - 1:TM, :]

    main, prev, nxt = _halo_specs(nt, tile_of)
    vec = pl.BlockSpec((1, LRU_W), lambda s: (0, 0))
    mat = pl.BlockSpec((LRU_W, LRU_W), lambda s: (0, 0))
    outs, got = _grid_call(
        body, name=name, nsteps=nt,
        in_specs=[main, prev, nxt, pl.BlockSpec((CONV_W, LRU_W), lambda s: (0, 0)), vec, mat, vec, mat, vec, vec],
        out_specs=[main],
        out_shape=[jax.ShapeDtypeStruct((t, LRU_W), F32)],
        scratch_shapes=[pltpu.VMEM((1, LRU_W), F32)],
        args=(p, p, p, conv_w, conv_b, wr, br, wi, bi, lam), rider=rider)
    return outs[0], got


def lru_bwd(p, hs, dhs, conv_w, conv_b, wr, br, wi, bi, lam, reverse, name):
    t = p.shape[0]
    nt = t // TM
    bpt = TM // HALO

    def tile_of(s):
        return jnp.where(s == nt - 1, 0, s + 1) if reverse else nt - 1 - s

    def hprev_block(s):
        tile = tile_of(s)
        if reverse:
            return (jnp.where(tile == nt - 1, 0, jnp.minimum((tile + 1) * bpt, nt * bpt - 1)), 0)
        return (jnp.maximum(tile * bpt - 1, 0), 0)

    def body(x_ref, xp_ref, xn_ref, h_ref, hp_ref, dh_ref, cw_ref, cb_ref, wr_ref, br_ref, wi_ref, bi_ref, lam_ref,
             dxc_ref, dwr_ref, dwi_ref, dbr_ref, dbi_ref, dlam_ref, carry):
        s = pl.program_id(0)
        tile = tile_of(s)

        @pl.when(s == 0)
        def _():
            carry[...] = jnp.zeros_like(carry)
            for ref in (dwr_ref, dwi_ref, dbr_ref, dbi_ref, dlam_ref):
                ref[...] = jnp.zeros_like(ref)

        ext = _ext(tile, nt, x_ref[...], xp_ref[...], xn_ref[...])
        xc = _conv(ext, cw_ref, cb_ref)
        wr_, wi_ = wr_ref[...], wi_ref[...]
        r, ig, sp, a, mult = _gates(xc, wr_, br_ref[...], wi_, bi_ref[...], lam_ref[...])
        gated = ig * xc
        row = lax.broadcasted_iota(jnp.int32, (TM, LRU_W), 0)
        hh = h_ref[...]
        start = jnp.where(tile != 0, hp_ref[0:1, :] if reverse else hp_ref[HALO - 1:HALO, :], 0.0)
        if reverse:
            edge = row == TM - 1
            hprev = jnp.where(edge, start, pltpu.roll(hh, TM - 1, 0))
            coef = jnp.where(row == 0, 0.0, pltpu.roll(a, 1, 0))
            bb = dh_ref[...] + jnp.where(row == 0, carry[...], 0.0)
        else:
            edge = row == 0
            hprev = jnp.where(edge, start, pltpu.roll(hh, 1, 0))
            coef = jnp.where(row == TM - 1, 0.0, pltpu.roll(a, TM - 1, 0))
            bb = dh_ref[...] + jnp.where(row == TM - 1, carry[...], 0.0)
        _, lmb = _scan(coef, bb, not reverse)
        al = a * lmb
        carry[...] = al[TM - 1:TM, :] if reverse else al[0:1, :]

        dgated = lmb * mult
        dloga = (lmb * hprev) * a - (lmb * gated) * (a * a) / mult
        dpre_r = (dloga * (-RG_C * sp)) * r * (1.0 - r)
        dpre_i = (dgated * xc) * ig * (1.0 - ig)
        drb, dib = dpre_r.astype(BF16), dpre_i.astype(BF16)
        xb = xc.astype(BF16)
        dxc_ref[...] = dgated * ig + _dot_nt(drb, wr_) + _dot_nt(dib, wi_)
        dwr_ref[...] += _dot_tn(xb, drb)
        dwi_ref[...] += _dot_tn(xb, dib)
        dbr_ref[...] += jnp.sum(dpre_r, axis=0, keepdims=True)
        dbi_ref[...] += jnp.sum(dpre_i, axis=0, keepdims=True)
        dlam_ref[...] += jnp.sum(dloga * (-RG_C * r), axis=0, keepdims=True)

        @pl.when(s == nt - 1)
        def _():
            dlam_ref[...] = dlam_ref[...] * (-_sigmoid(-lam_ref[...]))

    main, prev, nxt = _halo_specs(nt, tile_of)
    vec = pl.BlockSpec((1, LRU_W), lambda s: (0, 0))
    mat = pl.BlockSpec((LRU_W, LRU_W), lambda s: (0, 0))
    vshape = jax.ShapeDtypeStruct((1, LRU_W), F32)
    mshape = jax.ShapeDtypeStruct((LRU_W, LRU_W), F32)
    return pl.pallas_call(
        body, name=name, grid=(nt,),
        in_specs=[main, prev, nxt, main, pl.BlockSpec((HALO, LRU_W), hprev_block), main,
                  pl.BlockSpec((CONV_W, LRU_W), lambda s: (0, 0)), vec, mat, vec, mat, vec, vec],
        out_specs=[main, mat, mat, vec, vec, vec],
        out_shape=[jax.ShapeDtypeStruct((t, LRU_W), F32), mshape, mshape, vshape, vshape, vshape],
        scratch_shapes=[pltpu.VMEM((1, LRU_W), F32)],
        compiler_params=_cp(),
    )(p, p, p, hs, hs, dhs, conv_w, conv_b, wr, br, wi, bi, lam)


GELU_C = 0.7978845608028654
GELU_A = 0.044715


def _gelu(x):
    th = jnp.tanh(GELU_C * (x + GELU_A * x * x * x))
    return 0.5 * x * (1.0 + th), th


def _sgu(v, gain, w_ref, bt_ref):
    mu = jnp.mean(v, axis=-1, keepdims=True)
    xc = v - mu
    rs = lax.rsqrt(jnp.mean(xc * xc, axis=-1, keepdims=True) + EPS)
    vhat = xc * rs
    vnb = (vhat * gain).astype(BF16)
    chunks = []
    for ch in range(TM // CHUNK):
        zs = []
        for gi in range(GROUPS):
            vb = vnb[ch * CHUNK:(ch + 1) * CHUNK, gi * GROUP_DIM:(gi + 1) * GROUP_DIM]
            zs.append(_dot(w_ref[gi].astype(BF16), vb) + bt_ref[:, gi:gi + 1])
        chunks.append(jnp.concatenate(zs, axis=1))
    return jnp.concatenate(chunks, axis=0), vhat, rs, vnb


def _pcols(k):
    return pl.BlockSpec((TM, LRU_W), lambda i: (i, k))


def mix_out_fwd(h, p, hf, hb, mod, sgu_g, sgu_w, sgu_bt, womix, name):
    t = h.shape[0]

    def body(h_ref, gl_ref, u_ref, v_ref, hf_ref, hb_ref, mod_ref, sg_ref, sw_ref, sb_ref, w_hbm, out_ref, o_ref, w_v):
        i = pl.program_id(0)

        @pl.when(i == 0)
        def _():
            pltpu.sync_copy(w_hbm, w_v)

        ic = _kind(i)
        ge, _ = _gelu(gl_ref[...])
        y_lru = (hf_ref[...] + hb_ref[...]) * ge
        z, _, _, _ = _sgu(v_ref[...], sg_ref[...], sw_ref, sb_ref)
        y = jnp.concatenate([y_lru, u_ref[...] * z], axis=1).astype(BF16)
        o = _dot(y, w_v[...])
        o_ref[...] = o
        out_ref[...] = h_ref[...] + _sel(ic, mod_ref, 2) * o

    row = pl.BlockSpec((TM, D), lambda i: (i, 0))
    half = pl.BlockSpec((TM, LRU_W), lambda i: (i, 0))
    return pl.pallas_call(
        body, name=name, grid=(t // TM,),
        in_specs=[row, _pcols(1), _pcols(2), _pcols(3), half, half, pl.BlockSpec((2, 3, D), lambda i: (0, 0, 0)),
                  pl.BlockSpec((1, MLP_W), lambda i: (0, 0)), pl.BlockSpec((GROUPS, CHUNK, CHUNK), lambda i: (0, 0, 0)),
                  pl.BlockSpec((CHUNK, GROUPS), lambda i: (0, 0)), ANY],
        out_specs=[row, row],
        out_shape=[jax.ShapeDtypeStruct((t, D), F32), jax.ShapeDtypeStruct((t, D), F32)],
        scratch_shapes=[pltpu.VMEM((D, D), BF16)],
        compiler_params=_cp(),
    )(h, p, p, p, hf, hb, mod, sgu_g, sgu_w, sgu_bt, womix)


def mix_out_bwd(dy, p, hf, hb, o, mod, sgu_g, sgu_w, sgu_bt, womix, name, rider=None):
    t = dy.shape[0]

    def body(dy_ref, gl_ref, u_ref, v_ref, hf_ref, hb_ref, o_ref, mod_ref, sg_ref, sw_ref, sb_ref, w_hbm,
             dhs_ref, dp_ref, dw_ref, dgate_ref, dsg_ref, dsw_ref, dsb_ref, w_v):
        i = pl.program_id(0)

        @pl.when(i == 0)
        def _():
            pltpu.sync_copy(w_hbm, w_v)
            for ref in (dw_ref, dgate_ref, dsg_ref, dsw_ref, dsb_ref):
                ref[...] = jnp.zeros_like(ref)

        ic = _kind(i)
        dy_ = dy_ref[...]
        _acc2(dgate_ref, 0, dy_ * o_ref[...], ic)
        dob = (_sel(ic, mod_ref, 2) * dy_).astype(BF16)

        gl = gl_ref[...]
        ge, th = _gelu(gl)
        hsum = hf_ref[...] + hb_ref[...]
        gain = sg_ref[...]
        uu = u_ref[...]
        z, vhat, rs, vnb = _sgu(v_ref[...], gain, sw_ref, sb_ref)
        y = jnp.concatenate([hsum * ge, uu * z], axis=1).astype(BF16)
        dw_ref[...] += _dot_tn(y, dob)
        dyy = _dot_nt(dob, w_v[...])
        dyl, dys = dyy[:, :LRU_W], dyy[:, LRU_W:]

        dhs_ref[...] = dyl * ge
        dge = 0.5 * (1.0 + th) + 0.5 * gl * (1.0 - th * th) * (GELU_C * (1.0 + 3.0 * GELU_A * gl * gl))
        dp_ref[:, 0:LRU_W] = dyl * hsum * dge
        dp_ref[:, LRU_W:2 * LRU_W] = dys * z

        dz = dys * uu
        dzb = dz.astype(BF16)
        dvn_chunks, dsb_cols = [], [jnp.zeros((CHUNK, 1), F32)] * GROUPS
        for ch in range(TM // CHUNK):
            cols = []
            for gi in range(GROUPS):
                rs_, cs_ = slice(ch * CHUNK, (ch + 1) * CHUNK), slice(gi * GROUP_DIM, (gi + 1) * GROUP_DIM)
                dzg = dzb[rs_, cs_]
                dsb_cols[gi] = dsb_cols[gi] + jnp.sum(dz[rs_, cs_], axis=1, keepdims=True)
                dsw_ref[gi] += _dot_nt(dzg, vnb[rs_, cs_])
                cols.append(_dot_tn(sw_ref[gi].astype(BF16), dzg))
            dvn_chunks.append(jnp.concatenate(cols, axis=1))
        dsb_ref[...] += jnp.concatenate(dsb_cols, axis=1)
        dvn = jnp.concatenate(dvn_chunks, axis=0)
        dsg_ref[...] += jnp.sum(dvn * vhat, axis=0, keepdims=True)
        dvh = dvn * gain
        dp_ref[:, 2 * LRU_W:3 * LRU_W] = rs * (dvh - jnp.mean(dvh, axis=-1, keepdims=True)
                                               - vhat * jnp.mean(dvh * vhat, axis=-1, keepdims=True))

    row = pl.BlockSpec((TM, D), lambda i: (i, 0))
    half = pl.BlockSpec((TM, LRU_W), lambda i: (i, 0))
    const2 = lambda i: (0, 0)
    const3 = lambda i: (0, 0, 0)
    return _grid_call(
        body, name=name, nsteps=t // TM,
        in_specs=[row, _pcols(1), _pcols(2), _pcols(3), half, half, row, pl.BlockSpec((2, 3, D), const3),
                  pl.BlockSpec((1, MLP_W), const2), pl.BlockSpec((GROUPS, CHUNK, CHUNK), const3),
                  pl.BlockSpec((CHUNK, GROUPS), const2), ANY],
        out_specs=[half, pl.BlockSpec((TM, 3 * LRU_W), lambda i: (i, 0)), pl.BlockSpec((D, D), const2),
                   pl.BlockSpec((2, 1, D), const3), pl.BlockSpec((1, MLP_W), const2),
                   pl.BlockSpec((GROUPS, CHUNK, CHUNK), const3), pl.BlockSpec((CHUNK, GROUPS), const2)],
        out_shape=[jax.ShapeDtypeStruct((t, LRU_W), F32), jax.ShapeDtypeStruct((t, 3 * LRU_W), F32),
                   jax.ShapeDtypeStruct((D, D), F32), jax.ShapeDtypeStruct((2, 1, D), F32),
                   jax.ShapeDtypeStruct((1, MLP_W), F32), jax.ShapeDtypeStruct((GROUPS, CHUNK, CHUNK), F32),
                   jax.ShapeDtypeStruct((CHUNK, GROUPS), F32)],
        scratch_shapes=[pltpu.VMEM((D, D), BF16)],
        args=(dy, p, p, p, hf, hb, o, mod, sgu_g, sgu_w, sgu_bt, womix), rider=rider)


def mix_in_bwd(dy, h, p, dxf, dxb, dprest, mod, g, conv_w, wmix, name, rider=None):
    t = dy.shape[0]
    nt = t // TM

    def body(dy_ref, h_ref, x_ref, xp_ref, xn_ref, f_ref, fp_ref, fn_ref, b_ref, bp_ref, bn_ref, dpr_ref, mod_ref,
             g_ref, cw_ref, w_hbm, dh_ref, dw_hbm, dmod_ref, dg_ref, dcw_ref, dcb_ref, w_v, dw_v):
        i = pl.program_id(0)

        @pl.when(i == 0)
        def _():
            pltpu.sync_copy(w_hbm, w_v)
            dw_v[...] = jnp.zeros_like(dw_v)
            for ref in (dmod_ref, dg_ref, dcw_ref, dcb_ref):
                ref[...] = jnp.zeros_like(ref)

        dmain = f_ref[...] + b_ref[...]
        dext = _ext(i, nt, dmain, fp_ref[...] + bp_ref[...], fn_ref[...] + bn_ref[...])
        xext = _ext(i, nt, x_ref[...], xp_ref[...], xn_ref[...])
        dxl = cw_ref[0:1, :] * _shifted(dext, 2)
        for k in range(1, CONV_W):
            dxl = dxl + cw_ref[k:k + 1, :] * _shifted(dext, 2 - k)
        dcw_ref[...] += jnp.concatenate(
            [jnp.sum(dmain * _shifted(xext, k - 2), axis=0, keepdims=True) for k in range(CONV_W)], axis=0)
        dcb_ref[...] += jnp.sum(dmain, axis=0, keepdims=True)

        ic = _kind(i)
        gain = g_ref[...]
        scale = _sel(ic, mod_ref, 1)
        z, n, r = _norm_mod(h_ref[...], gain, _sel(ic, mod_ref, 0), scale)
        zb = z.astype(BF16)
        dpb = jnp.concatenate([dxl, dpr_ref[...]], axis=1).astype(BF16)
        dz = jnp.zeros((TM, D), F32)
        for dd in range(N_DEV):
            dpd = dpb[:, dd * N_MIX_SHARD:(dd + 1) * N_MIX_SHARD]
            dz = dz + _dot_nt(dpd, w_v[dd])
            dw_v[dd] += _dot_tn(zb, dpd)
        _acc2(dmod_ref, 0, dz, ic)
        _acc2(dmod_ref, 1, dz * (n * gain), ic)
        dg_ref[...] += jnp.sum(dz * (1.0 + scale) * n, axis=0, keepdims=True)
        dh_ref[...] = dy_ref[...] + _norm_mod_bwd(dz, n, r, gain, scale)

        @pl.when(i == nt - 1)
        def _():
            pltpu.sync_copy(dw_v, dw_hbm)

    main, prev, nxt = _halo_specs(nt, lambda s: s)
    row = pl.BlockSpec((TM, D), lambda i: (i, 0))
    const2 = lambda i: (0, 0)
    return _grid_call(
        body, name=name, nsteps=nt,
        in_specs=[row, row, main, prev, nxt, main, prev, nxt, main, prev, nxt,
                  pl.BlockSpec((TM, 3 * LRU_W), lambda i: (i, 0)), pl.BlockSpec((2, 3, D), lambda i: (0, 0, 0)),
                  pl.BlockSpec((1, D), const2), pl.BlockSpec((CONV_W, LRU_W), const2), ANY],
        out_specs=[row, ANY, pl.BlockSpec((2, 2, D), lambda i: (0, 0, 0)), pl.BlockSpec((1, D), const2),
                   pl.BlockSpec((CONV_W, LRU_W), const2), pl.BlockSpec((1, LRU_W), const2)],
        out_shape=[jax.ShapeDtypeStruct((t, D), F32), jax.ShapeDtypeStruct((N_DEV, D, N_MIX_SHARD), F32),
                   jax.ShapeDtypeStruct((2, 2, D), F32), jax.ShapeDtypeStruct((1, D), F32),
                   jax.ShapeDtypeStruct((CONV_W, LRU_W), F32), jax.ShapeDtypeStruct((1, LRU_W), F32)],
        scratch_shapes=[pltpu.VMEM((N_DEV, D, N_MIX_SHARD), BF16), pltpu.VMEM((N_DEV, D, N_MIX_SHARD), F32)],
        args=(dy, h, p, p, p, dxf, dxf, dxf, dxb, dxb, dxb, dprest, mod, g, conv_w, wmix), rider=rider)


def _block_diag(w):
    eye = jnp.eye(HEADS, dtype=w.dtype)
    return jnp.einsum("dhij,hk->dhikj", w, eye).reshape(2, LRU_W, LRU_W)


def _block_diag_inv(full):
    f = full.reshape(2, HEADS, HEAD_DIM, HEADS, HEAD_DIM)
    return jnp.stack([f[:, hd, :, hd, :] for hd in range(HEADS)], axis=1)


def small_layer(g1, gm, g2, conv_w, conv_b, w_r, b_r, w_i, b_i, lam, sgu_g, sgu_w, sgu_b):
    return dict(g1=g1[None, :], gm=gm[None, :], g2=g2[None, :], conv_w=conv_w, conv_b=conv_b[None, :],
                wr=_block_diag(w_r).astype(BF16), br=b_r[:, None, :], wi=_block_diag(w_i).astype(BF16),
                bi=b_i[:, None, :], lam=lam[:, None, :], sgu_g=sgu_g[None, :], sgu_w=sgu_w, sgu_bt=sgu_b.T)


def small_grads(g):
    out = dict(mix_norm_g=g["gm"][0], ffn2_norm_g=g["g2"][0], lru_conv_w=g["conv_w"],
               lru_conv_b=g["conv_b"][0], lru_w_r=_block_diag_inv(g["wr"]), lru_b_r=g["br"][:, 0, :],
               lru_w_i=_block_diag_inv(g["wi"]), lru_b_i=g["bi"][:, 0, :], lru_lambda=g["lam"][:, 0, :],
               sgu_norm_g=g["sgu_g"][0], sgu_w=g["sgu_w"], sgu_b=g["sgu_bt"].T)
    if "g1" in g:
        out["ffn1_norm_g"] = g["g1"][0]
    return out


BIG_KEYS = ("win1", "wout1", "wmix", "womix", "win2", "wout2")


def _as_blocks(key, g):
    return g if key == "wmix" else g.reshape(N_DEV, g.shape[0] // N_DEV, D)


def _gathered(key, a):
    return a if key == "wmix" else a.reshape(N_DEV * a.shape[1], D)


class _ReduceScatter:
    def __init__(self, c_idx, where):
        self.c_idx, self.where = c_idx, where
        self.out = {}

    def pair(self, group):
        return pair_rider([g for _, g in group])

    def after_pair(self, group, recv1, tag):
        return chips_rider(pair_sum([g for _, g in group], list(recv1), self.c_idx, f"pair_sum_{tag}"))

    def after_chips(self, group, recv1, recv2, tag):
        for (key, g), r1, r2 in zip(group, recv1, recv2):
            self.out[key] = (g, r1, r2)


FIRST_WEIGHTS = [(0, "win1"), (0, "wout1")]


def fwd_bwd(ctx_rows, x_rows, target, mods, shards, first_weights, smalls, final_g, c_idx, where):
    assert CTX == TM and len(shards) == 2

    def gather(keys_by_layer):
        return GatherRider([shards[l][k] for l, k in keys_by_layer])

    def put(full, keys_by_layer, got):
        for (l, k), a in zip(keys_by_layer, got):
            full[l][k] = _gathered(k, a)

    full = [dict(s) for s in smalls]
    put(full, FIRST_WEIGHTS, first_weights)
    riders = {
        "ffn1_fwd_0": [(0, "wmix"), (0, "womix"), (0, "win2")],
        "lru_fwd_0_0": [(0, "wout2")],
        "ffn2_fwd_0": [(1, "win1"), (1, "wout1")],
        "ffn1_fwd_1": [(1, "wmix"), (1, "womix"), (1, "win2")],
        "lru_fwd_1_0": [(1, "wout2")],
    }

    def ffn(which, l, h):
        name = f"ffn{which}_fwd_{l}"
        w = full[l]
        keys = riders.get(name)
        m = mods[l][:, 0:3] if which == 1 else mods[l][:, 6:9]
        last = (which, l) == (2, 1)
        outs, got = ffn_fwd(h, m, w[f"g{which}"], w[f"win{which}"], w[f"wout{which}"], name,
                            rider=gather(keys) if keys else None, loss=(final_g, target) if last else None)
        if keys:
            put(full, keys, got)
        return outs

    saved = []
    h = (ctx_rows, x_rows)
    for l in range(2):
        mm = mods[l][:, 3:6]
        outs = ffn(1, l, h)
        h1, gu1, acc1 = outs[:3]
        hin = outs[3] if l == 0 else h
        w = full[l]
        p = mix_in_fwd(h1, mm, w["gm"], w["wmix"], f"mix_in_fwd_{l}")
        hs = []
        for d in range(2):
            keys = riders.get(f"lru_fwd_{l}_{d}")
            hd, got = lru_fwd(p, w["conv_w"], w["conv_b"], w["wr"][d], w["br"][d], w["wi"][d], w["bi"][d], w["lam"][d],
                              bool(d), f"lru_fwd_{l}_{d}", rider=gather(keys) if keys else None)
            if keys:
                put(full, keys, got)
            hs.append(hd)
        h2, o = mix_out_fwd(h1, p, hs[0], hs[1], mm, w["sgu_g"], w["sgu_w"], w["sgu_bt"], w["womix"], f"mix_out_fwd_{l}")
        outs = ffn(2, l, h2)
        h, gu2, acc2 = outs[:3]
        saved.append((hin, h1, h2, gu1, acc1, p, hs, o, gu2, acc2))
    dh, (loss, dgf) = h, outs[3:]

    rs = _ReduceScatter(c_idx, where)
    grads, dmods, sums = [None, None], [None, None], [None, None]
    pending = None
    for l in (1, 0):
        w = full[l]
        m1, mm, m2 = mods[l][:, 0:3], mods[l][:, 3:6], mods[l][:, 6:9]
        hin, h1, h2, gu1, acc1, p, hs, o, gu2, acc2 = saved[l]
        g = {}
        rs.out = {}
        both = Riders([rs.pair(pending[0]), GatherRider([small_pack])]) if pending else None
        (dp2, g["wout2"], dgate2), got = ffn_bwd_a(dh, acc2, gu2, m2, w["wout2"], f"ffn2_bwd_a_{l}", rider=both)
        if pending:
            r1, (small_all,) = both.split(got)
        chips = rs.after_pair(pending[0], r1, pending[1]) if pending else None
        (dh, g["win2"], dmod2, g["g2"]), r2 = ffn_bwd_b(dh, h2, dp2, m2, w["g2"], w["win2"], f"ffn2_bwd_b_{l}", rider=chips)
        if pending:
            rs.after_chips(pending[0], r1, r2, pending[1])
            sums[l + 1].update(rs.out)
            rs.out = {}

        grp = [(k, _as_blocks(k, g[k])) for k in ("win2", "wout2")]
        (dhs, dprest, g["womix"], dgatem, g["sgu_g"], g["sgu_w"], g["sgu_bt"]), r1 = mix_out_bwd(
            dh, p, hs[0], hs[1], o, mm, w["sgu_g"], w["sgu_w"], w["sgu_bt"], w["womix"], f"mix_out_bwd_{l}",
            rider=rs.pair(grp))
        chips = rs.after_pair(grp, r1, f"a{l}")
        dx, per_dir = [], []
        for d in range(2):
            out = lru_bwd(p, hs[d], dhs, w["conv_w"], w["conv_b"], w["wr"][d], w["br"][d], w["wi"][d], w["bi"][d],
                          w["lam"][d], bool(d), f"lru_bwd_{l}_{d}")
            dx.append(out[0])
            per_dir.append(out[1:])
        for k, nm in enumerate(("wr", "wi", "br", "bi", "lam")):
            g[nm] = jnp.stack([per_dir[0][k], per_dir[1][k]])
        (dh, g["wmix"], dmodm, g["gm"], g["conv_w"], g["conv_b"]), r2 = mix_in_bwd(
            dh, h1, p, dx[0], dx[1], dprest, mm, w["gm"], w["conv_w"], w["wmix"], f"mix_in_bwd_{l}", rider=chips)
        rs.after_chips(grp, r1, r2, f"a{l}")
        sums[l] = dict(rs.out)
        rs.out = {}

        if l == 1:
            (dp1, g["wout1"], dgate1), _ = ffn_bwd_a(dh, acc1, gu1, m1, w["wout1"], f"ffn1_bwd_a_{l}")
            (dh, g["win1"], dmod1, g["g1"]), _ = ffn_bwd_b(dh, hin, dp1, m1, w["g1"], w["win1"], f"ffn1_bwd_b_{l}")
            pending = ([(k, _as_blocks(k, g[k])) for k in ("womix", "wmix", "wout1", "win1")], f"b{l}")
            per = small_grads(g)
            small_pack = _pack([per[n] for n in LAYER_SMALL])
        else:
            g_mix = [(k, _as_blocks(k, g[k])) for k in ("womix", "wmix")]
            (dp1, g["wout1"], dgate1), r1_mix = ffn_bwd_a(dh, acc1, gu1, m1, w["wout1"], f"ffn1_bwd_a_{l}",
                                                          rider=rs.pair(g_mix))
            g_out = [("wout1", _as_blocks("wout1", g["wout1"]))]
            per = small_grads(g)
            three = Riders([rs.after_pair(g_mix, r1_mix, f"b{l}"), rs.pair(g_out),
                            GatherRider([_pack([per[n] for n in LAYER_SMALL[1:]])])])
            (g["win1"],), got = ffn_bwd_dw(hin, dp1, m1, w["g1"], f"ffn1_bwd_dw_{l}", rider=three)
            r2_mix, r1_out, (small0_all,) = three.split(got)
            rs.after_chips(g_mix, r1_mix, r2_mix, f"b{l}")
            g_in = [("win1", _as_blocks("win1", g["win1"]))]
            both = Riders([rs.after_pair(g_out, r1_out, f"c{l}"), rs.pair(g_in)])
            nt = dh.shape[0] // TM
            cut0, cut1 = (3 * nt) // 8, nt - max(nt // 16, 1)
            part, got = ffn_bwd_dh(dh, hin, dp1, m1, w["g1"], w["win1"], f"ffn1_bwd_dh0_{l}", (0, cut0), rider=both)
            r2_out, r1_in = both.split(got)
            rs.after_chips(g_out, r1_out, r2_out, f"c{l}")
            part, r2_in = ffn_bwd_dh(dh, hin, dp1, m1, w["g1"], w["win1"], f"ffn1_bwd_dh1_{l}", (cut0, cut1), carry=part,
                                     rider=rs.after_pair(g_in, r1_in, f"d{l}"))
            rs.after_chips(g_in, r1_in, r2_in, f"d{l}")
            (dh, dmod1, g["g1"]), _ = ffn_bwd_dh(dh, hin, dp1, m1, w["g1"], w["win1"], f"ffn1_bwd_dh2_{l}", (cut1, nt),
                                                 carry=part)
            sums[l].update(rs.out)
        dmods[l] = jnp.concatenate([dmod1, dgate1, dmodm, dgatem, dmod2, dgate2], axis=1)
        grads[l] = g
    return loss, dh, jnp.stack(dmods), grads, (small0_all, small_all), sums, dgf


def _row_block(r, c, limit=262144):
    best = 8
    for rb in range(8, r + 1, 8):
        if r % rb == 0 and rb * c <= limit:
            best = rb
    return best


PAIR_SUM_SPLIT = 2


def pair_sum(grads, recv, c_idx, name):
    n = len(grads)

    def body(c_ref, *refs):
        for t in range(n):
            refs[2 * n + t][...] = (refs[t][...] + refs[n + t][...]).astype(BF16)

    mine, theirs, outs = [], [], []
    for g in grads:
        _, r, c = g.shape
        rb = r // PAIR_SUM_SPLIT
        assert rb % 16 == 0
        mine.append(pl.BlockSpec((1, rb, c), lambda j, i, c_ref: (2 * j + c_ref[0], i, 0)))
        theirs.append(pl.BlockSpec((1, rb, c), lambda j, i, c_ref: (j, i, 0)))
        outs.append(jax.ShapeDtypeStruct((4, r, c), BF16))
    return pl.pallas_call(
        body, name=name,
        grid_spec=pltpu.PrefetchScalarGridSpec(num_scalar_prefetch=1, grid=(4, PAIR_SUM_SPLIT), in_specs=mine + theirs,
                                               out_specs=list(theirs)),
        out_shape=outs, compiler_params=_cp(2),
    )(c_idx, *grads, *recv)


ADA_ROWS = 16


def _silu(v):
    return v * _sigmoid(v)


def ada_fwd(cond, w_ada, b_slab, name):
    def body(c_ref, w_ref, b_ref, o_ref):
        s = _silu(c_ref[...]).astype(BF16)
        o_ref[0] = _dot(s, w_ref[0].astype(BF16)) + b_ref[0]

    return pl.pallas_call(
        body, name=name, grid=(DEPTH,),
        in_specs=[pl.BlockSpec((ADA_ROWS, D), lambda l: (0, 0)), pl.BlockSpec((1, D, ADA_SHARD), lambda l: (l, 0, 0)),
                  pl.BlockSpec((1, 1, ADA_SHARD), lambda l: (l, 0, 0))],
        out_specs=pl.BlockSpec((1, ADA_ROWS, ADA_SHARD), lambda l: (l, 0, 0)),
        out_shape=jax.ShapeDtypeStruct((DEPTH, ADA_ROWS, ADA_SHARD), F32),
        compiler_params=_cp(),
    )(cond, w_ada, b_slab)


def ada_bwd(cond, dm_sample, dm_ctx, w_ada, name):
    def body(c_ref, ds_ref, dc_ref, w_ref, gw_ref, dsc_ref):
        @pl.when(pl.program_id(0) == 0)
        def _():
            dsc_ref[...] = jnp.zeros_like(dsc_ref)

        s = _silu(c_ref[...]).astype(BF16)
        dcs = dc_ref[0]
        tot = dcs[0:1]
        for j in range(1, N_DEV):
            tot = tot + dcs[j:j + 1]
        tot8 = jnp.where(lax.broadcasted_iota(jnp.int32, (N_DEV, ADA_SHARD), 0) == 0, tot, 0.0)
        dm = jnp.concatenate([ds_ref[0], tot8], axis=0).astype(BF16)
        gw_ref[0] = _dot_tn(s, dm)
        dsc_ref[...] += _dot_nt(dm, w_ref[0].astype(BF16))[N_DEV:N_DEV + 1]

    slab = pl.BlockSpec((1, N_DEV, ADA_SHARD), lambda l: (l, 0, 0))
    wspec = pl.BlockSpec((1, D, ADA_SHARD), lambda l: (l, 0, 0))
    return pl.pallas_call(
        body, name=name, grid=(DEPTH,),
        in_specs=[pl.BlockSpec((ADA_ROWS, D), lambda l: (0, 0)), slab, slab, wspec],
        out_specs=[wspec, pl.BlockSpec((1, D), lambda l: (0, 0))],
        out_shape=[jax.ShapeDtypeStruct((DEPTH, D, ADA_SHARD), F32), jax.ShapeDtypeStruct((1, D), F32)],
        compiler_params=_cp(),
    )(cond, dm_sample, dm_ctx, w_ada)


def sum_over_devices(parts, name, silu_rows=0, w=None):
    _, r, c = parts.shape

    def body(*refs):
        p_ref, o_ref = refs[0], refs[-1]
        tot = p_ref[0]
        for j in range(1, N_DEV):
            tot = tot + p_ref[j]
        o_ref[...] = tot
        if silu_rows:
            wv = refs[1][...]
            s = _sigmoid(wv)
            o_ref[0:silu_rows, :] = tot[0:silu_rows, :] * (s * (1.0 + wv * (1.0 - s)))

    vm = pl.BlockSpec(memory_space=pltpu.VMEM)
    args = (parts,) if w is None else (parts, w)
    return pl.pallas_call(
        body, name=name, in_specs=[vm] * len(args), out_specs=vm,
        out_shape=jax.ShapeDtypeStruct((r, c), F32),
        compiler_params=pltpu.CompilerParams(vmem_limit_bytes=VMEM_LIMIT),
    )(*args)


def sum_dmods(dm_all, name):
    def body(d_ref, o_ref):
        for l in range(DEPTH):
            tot = d_ref[0, l]
            for j in range(1, N_DEV):
                tot = tot + d_ref[j, l]
            o_ref[l:l + 1, :] = tot[0:1] + tot[1:2]

    vm = pl.BlockSpec(memory_space=pltpu.VMEM)
    return pl.pallas_call(
        body, name=name, in_specs=[vm], out_specs=vm,
        out_shape=jax.ShapeDtypeStruct((DEPTH, N_MOD * D), F32),
    )(dm_all)


ADAMW_BLOCK = 512 * 1024


def adamw(w, g, m, v, name, rider=None):
    r, c = w.shape
    rb = _row_block(r, c, limit=ADAMW_BLOCK)

    def body(w_ref, g_ref, m_ref, v_ref, d_ref, nm_ref, nv_ref):
        g_ = g_ref[...]
        nm = B1 * m_ref[...] + (1.0 - B1) * g_
        nv = B2 * v_ref[...] + (1.0 - B2) * (g_ * g_)
        nm_ref[...] = nm
        nv_ref[...] = nv
        m_hat = nm / (1.0 - B1 ** STEP)
        v_hat = nv / (1.0 - B2 ** STEP)
        d_ref[...] = -LR * (m_hat / (jnp.sqrt(v_hat) + ADAM_EPS) + WD * w_ref[...])

    blk = pl.BlockSpec((rb, c), lambda i: (i, 0))
    shp = jax.ShapeDtypeStruct((r, c), F32)
    return _grid_call(body, name=name, nsteps=r // rb, in_specs=[blk] * 4, out_specs=[blk] * 3, out_shape=[shp] * 3,
                      scratch_shapes=[], args=(w, g, m, v), rider=rider)


def _adamw_math(w, g, m, v):
    nm = B1 * m + (1.0 - B1) * g
    nv = B2 * v + (1.0 - B2) * (g * g)
    m_hat = nm / (1.0 - B1 ** STEP)
    v_hat = nv / (1.0 - B2 ** STEP)
    return -LR * (m_hat / (jnp.sqrt(v_hat) + ADAM_EPS) + WD * w), nm, nv


def adamw_layers(w, parts, m, v, where, name):
    _, r, c = w.shape
    assert len(parts) == DEPTH == 2 and parts[0][0].shape == (N_DEV, r, c)
    rb = _row_block(r, c, limit=ADAMW_BLOCK // 2)
    nb = r // rb

    def body(where_ref, w_ref, a0, b0, c0, a1, b1, c1, m_ref, v_ref, go_ref, d_ref, nm_ref, nv_ref):
        def total(mine, pair, far):
            return (mine[0] + pair[0]) + ((far[0].astype(F32) + far[1].astype(F32)) + far[2].astype(F32))

        g = jnp.where(pl.program_id(0) == 0, total(a0, b0, c0), total(a1, b1, c1))
        go_ref[0], d_ref[0], nm_ref[0], nv_ref[0] = (g,) + _adamw_math(w_ref[0], g, m_ref[0], v_ref[0])

    blk = pl.BlockSpec((1, rb, c), lambda l, i, wr: (l, i, 0))

    def layer_specs(layer):
        row = (lambda l, i: jnp.where(l == 0, i, nb - 1)) if layer == 0 else (lambda l, i: jnp.where(l == 0, 0, i))
        return [pl.BlockSpec((1, rb, c), lambda l, i, wr: (wr[0], row(l, i), 0)),
                pl.BlockSpec((1, rb, c), lambda l, i, wr: (wr[1], row(l, i), 0)),
                pl.BlockSpec((3, rb, c), lambda l, i, wr: (0, row(l, i), 0))]

    shp = jax.ShapeDtypeStruct(w.shape, F32)
    return pl.pallas_call(
        body, name=name,
        grid_spec=pltpu.PrefetchScalarGridSpec(
            num_scalar_prefetch=1, grid=(DEPTH, nb),
            in_specs=[blk] + layer_specs(0) + layer_specs(1) + [blk, blk], out_specs=[blk] * 4),
        out_shape=[shp] * 4, compiler_params=_cp(2),
    )(where, w, *parts[0], *parts[1], m, v)


def _adamw_nd(w, g, m, v, name, rider=None):
    shape = w.shape
    flat = lambda a: a.reshape(-1, shape[-1])
    outs, got = adamw(flat(w), flat(g), flat(m), flat(v), name, rider=rider)
    return tuple(o.reshape(shape) for o in outs), got


LANES = 128


PACK_UNIT = 8 * LANES


ADAMW_SMALL_ROWS = 512


def _pack(arrays, row_multiple=8):
    pieces, n = [], 0
    for a in arrays:
        pieces.append(a.reshape(-1).astype(F32))
        pad = (-a.size) % PACK_UNIT
        if pad:
            pieces.append(jnp.zeros((pad,), F32))
        n += a.size + pad
    tail = (-n) % (row_multiple * LANES)
    if tail:
        pieces.append(jnp.zeros((tail,), F32))
    return jnp.concatenate(pieces).reshape(-1, LANES)


def _unpack(packed, shapes):
    out, r0 = [], 0
    lead = packed.shape[:-2]
    for shp in shapes:
        size = 1
        for s in shp:
            size *= s
        nr = 8 * -(-size // PACK_UNIT)
        blk = packed[..., r0:r0 + nr, :].reshape(lead + (nr * LANES,))[..., :size]
        out.append(blk.reshape(lead + tuple(shp)))
        r0 += nr
    return out


WEIGHTS = ["c_ctx", "w_ada", "b_ada", "ffn1_norm_g", "ffn1_w_in", "ffn1_w_out", "mix_norm_g", "w_in_mix", "lru_conv_w",
           "lru_conv_b", "lru_w_r", "lru_b_r", "lru_w_i", "lru_b_i", "lru_lambda", "sgu_norm_g", "sgu_w", "sgu_b",
           "w_out_mix", "ffn2_norm_g", "ffn2_w_in", "ffn2_w_out", "final_norm_g"]
BIG = ["w_ada", "ffn1_w_in", "ffn1_w_out", "w_in_mix", "w_out_mix", "ffn2_w_in", "ffn2_w_out"]
SHARDED_SMALL = ["lru_conv_w", "lru_b_r", "lru_b_i", "lru_lambda"]
LAYER_SMALL = ["ffn1_norm_g", "mix_norm_g", "ffn2_norm_g", "lru_conv_w", "lru_conv_b", "lru_w_r", "lru_b_r", "lru_w_i",
               "lru_b_i", "lru_lambda", "sgu_norm_g", "sgu_w", "sgu_b"]
LRU_SHARD = LRU_W // N_DEV


def _widen(a):
    return jnp.moveaxis(a, 0, -2).reshape(a.shape[1:-1] + (LRU_W,))


def kernel(x, c, ctx, c_ctx, w_ada, b_ada, ffn1_norm_g, ffn1_w_in, ffn1_w_out, mix_norm_g, w_in_mix, lru_conv_w, lru_conv_b, lru_w_r, lru_b_r, lru_w_i, lru_b_i, lru_lambda, sgu_norm_g, sgu_w, sgu_b, w_out_mix, ffn2_norm_g, ffn2_w_in, ffn2_w_out, final_norm_g, loss_target, m_c_ctx, m_w_ada, m_b_ada, m_ffn1_norm_g, m_ffn1_w_in, m_ffn1_w_out, m_mix_norm_g, m_w_in_mix, m_lru_conv_w, m_lru_conv_b, m_lru_w_r, m_lru_b_r, m_lru_w_i, m_lru_b_i, m_lru_lambda, m_sgu_norm_g, m_sgu_w, m_sgu_b, m_w_out_mix, m_ffn2_norm_g, m_ffn2_w_in, m_ffn2_w_out, m_final_norm_g, v_c_ctx, v_w_ada, v_b_ada, v_ffn1_norm_g, v_ffn1_w_in, v_ffn1_w_out, v_mix_norm_g, v_w_in_mix, v_lru_conv_w, v_lru_conv_b, v_lru_w_r, v_lru_b_r, v_lru_w_i, v_lru_b_i, v_lru_lambda, v_sgu_norm_g, v_sgu_w, v_sgu_b, v_w_out_mix, v_ffn2_norm_g, v_ffn2_w_in, v_ffn2_w_out, v_final_norm_g):
    given = dict(locals())
    W = {n: given[n] for n in WEIGHTS}
    M = {n: given["m_" + n] for n in WEIGHTS}
    V = {n: given["v_" + n] for n in WEIGHTS}
    xi, yi, ci = _position()
    me = 4 * xi + 2 * yi + ci
    chip = 2 * xi + yi

    shards = []
    tr = lambda a: jnp.swapaxes(a, 1, 2)
    for l in range(DEPTH):
        sh = dict(win1=tr(ffn1_w_in)[l], wout1=ffn1_w_out[l], wmix=w_in_mix[l], womix=w_out_mix[l], win2=tr(ffn2_w_in)[l],
                  wout2=ffn2_w_out[l])
        shards.append({k: a.astype(BF16) for k, a in sh.items()})

    sharded_shapes = [W[n].shape for n in SHARDED_SMALL]
    both = Riders([GatherRider([_pack([c[0]] + [W[n] for n in SHARDED_SMALL])]),
                   GatherRider([shards[0][k] for _, k in FIRST_WEIGHTS])])
    (got,), first_weights = both.split(run_alone(both, pl.ANY, "gather_first"))
    parts = _unpack(got, [(D,)] + sharded_shapes)
    c_all = parts[0]
    wide = {n: _widen(a) for n, a in zip(SHARDED_SMALL, parts[1:])}
    cond = jnp.concatenate([c_all, c_ctx[None, :], jnp.zeros((ADA_ROWS - N_DEV - 1, D), F32)], axis=0)
    b_slab = lax.dynamic_slice_in_dim(b_ada, me * ADA_SHARD, ADA_SHARD, axis=1)[:, None, :]
    slabs = ada_fwd(cond, w_ada, b_slab, "ada_fwd")
    mall = run_alone(GatherRider([slabs.reshape(DEPTH * ADA_ROWS, ADA_SHARD)]), pltpu.VMEM, "gather_mod")[0]
    mall = mall.reshape(N_DEV, DEPTH, ADA_ROWS, ADA_SHARD)
    m_sample = lax.dynamic_index_in_dim(mall, me, axis=2, keepdims=False)
    m_ctx = mall[:, :, N_DEV, :]
    mods = jnp.stack([jnp.transpose(m, (1, 0, 2)).reshape(DEPTH, N_MOD, D) for m in (m_ctx, m_sample)], axis=1)

    smalls = []
    for l in range(DEPTH):
        smalls.append(small_layer(ffn1_norm_g[l], mix_norm_g[l], ffn2_norm_g[l], wide["lru_conv_w"][l], lru_conv_b[l],
                                  lru_w_r[l], wide["lru_b_r"][l], lru_w_i[l], wide["lru_b_i"][l], wide["lru_lambda"][l],
                                  sgu_norm_g[l], sgu_w[l], sgu_b[l]))

    c_idx = ci.reshape(1).astype(jnp.int32)
    where = jnp.stack([me, chip]).astype(jnp.int32)
    loss_blk, dx, dmods, grads, (small0_all, small1_all), gsum, dgf = fwd_bwd(
        ctx[0], x[0], loss_target[0], mods, shards, first_weights, smalls, final_norm_g[None, :], c_idx, where)
    smalls_shape = {n: (W[n].shape[1:-1] + (LRU_W,)) if n in SHARDED_SMALL else W[n].shape[1:] for n in LAYER_SMALL}
    G, delta, new_m, new_v = {}, {}, {}, {}
    for key, n in (("win1", "ffn1_w_in"), ("wout1", "ffn1_w_out"), ("wmix", "w_in_mix"), ("womix", "w_out_mix"),
                   ("win2", "ffn2_w_in"), ("wout2", "ffn2_w_out")):
        t_in = tr if key in ("win1", "win2") else (lambda a: a)
        outs = adamw_layers(t_in(W[n]), [gsum[l][key] for l in range(DEPTH)], t_in(M[n]), t_in(V[n]), where,
                            f"adamw_{n}")
        G[n], delta[n], new_m[n], new_v[n] = [t_in(o) for o in outs]

    n_rows = DEPTH * 2 * N_MOD
    dm_rows = jnp.concatenate([dmods.reshape(n_rows, D), jnp.zeros((-n_rows % 8, D), F32)], axis=0)
    dm_all = run_alone(GatherRider([dm_rows]), pltpu.VMEM, "gather_dmod")[0][:, :n_rows]
    dm_all = dm_all.reshape(N_DEV, DEPTH, 2, N_MOD * D)
    mine = lax.dynamic_slice_in_dim(dm_all, me * ADA_SHARD, ADA_SHARD, axis=3)
    G["w_ada"], dsc = ada_bwd(cond, jnp.transpose(mine[:, :, 1, :], (1, 0, 2)), jnp.transpose(mine[:, :, 0, :], (1, 0, 2)),
                              w_ada, "ada_bwd")
    G["b_ada"] = sum_dmods(dm_all, "sum_dmods")
    (delta["w_ada"], new_m["w_ada"], new_v["w_ada"]), _ = _adamw_nd(w_ada, G["w_ada"], m_w_ada, v_w_ada, "adamw_w_ada")

    head_all = run_alone(GatherRider([_pack([dsc[0], dgf[0], grads[0]["g1"][0]])]), pltpu.VMEM, "gather_head_grads")[0]
    head = _unpack(sum_over_devices(head_all, "sum_head_grads", silu_rows=D // LANES, w=c_ctx.reshape(D // LANES, LANES)),
                   [(D,), (D,), (D,)])
    shapes = [smalls_shape[n] for n in LAYER_SMALL]
    sum0 = [head[2]] + _unpack(sum_over_devices(small0_all, "sum_small_grads_0"), shapes[1:])
    sum1 = _unpack(sum_over_devices(small1_all, "sum_small_grads_1"), shapes)
    G["c_ctx"], G["final_norm_g"] = head[0], head[1]
    for n, a0, a1 in zip(LAYER_SMALL, sum0, sum1):
        a = jnp.stack([a0, a1])
        G[n] = lax.dynamic_slice_in_dim(a, me * LRU_SHARD, LRU_SHARD, axis=a.ndim - 1) if n in SHARDED_SMALL else a

    rest = [n for n in WEIGHTS if n not in BIG]
    shapes = [W[n].shape for n in rest]
    outs, _ = adamw(*[_pack([src[n] for n in rest], row_multiple=ADAMW_SMALL_ROWS) for src in (W, G, M, V)], "adamw_small")
    for dst, packed in zip((delta, new_m, new_v), outs):
        for n, a in zip(rest, _unpack(packed, shapes)):
            dst[n] = a

    loss = lax.psum(loss_blk[0, 0], ("x", "y", "c"))
    grad_x = dx[None]
    return (loss, grad_x, *[G[n] for n in WEIGHTS], *[delta[n] for n in WEIGHTS], *[new_m[n] for n in WEIGHTS],
            *[new_v[n] for n in WEIGHTS])
```

```python
import functools

import jax
import jax.numpy as jnp
from jax import lax
from jax.experimental import pallas as pl
from jax.experimental.pallas import tpu as pltpu

F32 = jnp.float32
BF16 = jnp.bfloat16

D = 1024
CTX = 256
DEPTH = 2
EPS = 1e-6
D_FF = 2816
LRU_W = 512
HEADS = 8
HEAD_DIM = 64
CONV_W = 4
RG_C = 8.0
GROUPS = 4
GROUP_DIM = 128
CHUNK = 128
MLP_W = 512
IN_PROJ = 2048
N_MOD = 9
N_DEV = 8

LR = 0.001
B1 = 0.9
B2 = 0.999
ADAM_EPS = 1e-08
WD = 0.01
STEP = 10

FF_IN_SHARD = 2 * D_FF // N_DEV
FF_OUT_SHARD = D_FF // N_DEV
HT = 256
WT = 512
N_MIX_SHARD = IN_PROJ // N_DEV
OMIX_SHARD = D // N_DEV
ADA_SHARD = N_MOD * D // N_DEV

TM = 256
SUB = 3
TMX = SUB * TM
HALO = 8
VMEM_LIMIT = 60 * 1024 * 1024

MESH = pl.DeviceIdType.MESH
ANY = pl.BlockSpec(memory_space=pl.ANY)


def _cp(n_axes=1):
    return pltpu.CompilerParams(dimension_semantics=("arbitrary",) * n_axes, vmem_limit_bytes=VMEM_LIMIT)


def _position():
    return lax.axis_index("x"), lax.axis_index("y"), lax.axis_index("c")


class GatherRider:
    def __init__(self, shards):
        n = len(shards)
        self.n = n
        self.ins = list(shards)
        self.out_shape = [jax.ShapeDtypeStruct((N_DEV,) + s.shape, s.dtype) for s in shards]
        self.sems = [pltpu.SemaphoreType.DMA((n, 7)), pltpu.SemaphoreType.DMA((n, 7)), pltpu.SemaphoreType.DMA((n,))]

    def _ctx(self, outs, sems):
        x, y, c = _position()
        chips = [(1 - x, y), (x, 1 - y), (1 - x, 1 - y)]

        def copy(t, k, block, to, src=None):
            dst = outs[t].at[4 * block[0] + 2 * block[1] + block[2]]
            return pltpu.make_async_remote_copy(
                src_ref=dst if src is None else src, dst_ref=dst, send_sem=sems[0].at[t, k],
                recv_sem=sems[1].at[t, k], device_id=to, device_id_type=MESH)

        return (x, y, c), (x, y, 1 - c), chips, copy

    def _local(self, ins, outs, sems, t):
        x, y, c = _position()
        return pltpu.make_async_copy(ins[t], outs[t].at[4 * x + 2 * y + c], sems[2].at[t])

    def _first(self, ins, outs, sems, t):
        me, sibling, chips, copy = self._ctx(outs, sems)
        return [copy(t, 0, me, sibling, src=ins[t])] + [copy(t, 1 + j, me, (*chip, me[2]), src=ins[t])
                                                         for j, chip in enumerate(chips)]

    def start(self, ins, outs, sems):
        for t in range(self.n):
            self._local(ins, outs, sems, t).start()
            for cp in self._first(ins, outs, sems, t):
                cp.start()

    def mid(self, ins, outs, sems):
        me, sibling, chips, copy = self._ctx(outs, sems)
        for j, chip in enumerate(chips):
            for t in range(self.n):
                copy(t, 1 + j, (*chip, me[2]), me).wait_recv()
                copy(t, 4 + j, (*chip, me[2]), sibling).start()

    def finish(self, ins, outs, sems):
        me, sibling, chips, copy = self._ctx(outs, sems)
        for t in range(self.n):
            copy(t, 0, sibling, me).wait_recv()
            for j, chip in enumerate(chips):
                copy(t, 4 + j, (*chip, 1 - me[2]), me).wait_recv()
        for t in range(self.n):
            for cp in self._first(ins, outs, sems, t):
                cp.wait_send()
            for j, chip in enumerate(chips):
                copy(t, 4 + j, (*chip, me[2]), sibling).wait_send()
            self._local(ins, outs, sems, t).wait()


class ExchangeRider:
    def __init__(self, tensors, plan, n_slots):
        n = len(tensors)
        self.n, self.plan = n, plan
        self.ins = list(tensors)
        self.out_shape = [jax.ShapeDtypeStruct((n_slots,) + s.shape[1:], s.dtype) for s in tensors]
        self.sems = [pltpu.SemaphoreType.DMA((n, n_slots)), pltpu.SemaphoreType.DMA((n, n_slots))]

    def _copies(self, ins, outs, sems):
        return [pltpu.make_async_remote_copy(
            src_ref=ins[t].at[block], dst_ref=outs[t].at[k], send_sem=sems[0].at[t, k], recv_sem=sems[1].at[t, k],
            device_id=to, device_id_type=MESH)
            for t in range(self.n) for k, (block, to) in enumerate(self.plan(*_position()))]

    def start(self, ins, outs, sems):
        for cp in self._copies(ins, outs, sems):
            cp.start()

    def mid(self, ins, outs, sems):
        pass

    def finish(self, ins, outs, sems):
        for cp in self._copies(ins, outs, sems):
            cp.wait()


class Riders:
    def __init__(self, riders):
        self.riders = list(riders)
        self.ins = [a for r in self.riders for a in r.ins]
        self.out_shape = [s for r in self.riders for s in r.out_shape]
        self.sems = [s for r in self.riders for s in r.sems]

    def _each(self, ins, outs, sems):
        i = o = s = 0
        for r in self.riders:
            ni, no, ns = len(r.ins), len(r.out_shape), len(r.sems)
            yield r, ins[i:i + ni], outs[o:o + no], sems[s:s + ns]
            i, o, s = i + ni, o + no, s + ns

    def start(self, ins, outs, sems):
        for r, a, b, c in self._each(ins, outs, sems):
            r.start(a, b, c)

    def mid(self, ins, outs, sems):
        for r, a, b, c in self._each(ins, outs, sems):
            r.mid(a, b, c)

    def finish(self, ins, outs, sems):
        for r, a, b, c in self._each(ins, outs, sems):
            r.finish(a, b, c)

    def split(self, outs):
        res, o = [], 0
        for r in self.riders:
            res.append(list(outs[o:o + len(r.out_shape)]))
            o += len(r.out_shape)
        return res


def pair_rider(grads):
    def plan(x, y, c):
        return [(4 * cx + 2 * cy + (1 - c), (x, y, 1 - c)) for cx in range(2) for cy in range(2)]
    return ExchangeRider(grads, plan, 4)


def chips_rider(parts):
    def plan(x, y, c):
        return [(2 * cx + cy, (cx, cy, c)) for cx, cy in [(1 - x, y), (x, 1 - y), (1 - x, 1 - y)]]
    return ExchangeRider(parts, plan, 3)


def run_alone(rider, space, name):
    ni = len(rider.ins)
    no = len(rider.out_shape)

    def body(*refs):
        ins, outs, sems = refs[:ni], refs[ni:ni + no], refs[ni + no:]
        rider.start(ins, outs, sems)
        rider.mid(ins, outs, sems)
        rider.finish(ins, outs, sems)

    spec = pl.BlockSpec(memory_space=space)
    return pl.pallas_call(
        body, name=name, in_specs=[spec] * ni, out_specs=[spec] * no, out_shape=rider.out_shape,
        scratch_shapes=rider.sems, compiler_params=pltpu.CompilerParams(vmem_limit_bytes=VMEM_LIMIT),
    )(*rider.ins)


def _grid_call(body, *, name, nsteps, in_specs, out_specs, out_shape, scratch_shapes, args, rider=None, aliases=None):
    aliases = aliases or {}
    if rider is None:
        outs = pl.pallas_call(body, name=name, grid=(nsteps,), in_specs=in_specs, out_specs=out_specs,
                              out_shape=out_shape, scratch_shapes=scratch_shapes, input_output_aliases=aliases,
                              compiler_params=_cp())(*args)
        return outs, []
    ni, no, ns = len(in_specs), len(out_specs), len(scratch_shapes)
    ri, ro = len(rider.ins), len(rider.out_shape)

    def wrapped(*refs):
        ins, refs = refs[:ni], refs[ni:]
        r_ins, refs = refs[:ri], refs[ri:]
        outs, refs = refs[:no], refs[no:]
        r_outs, refs = refs[:ro], refs[ro:]
        scratch, r_sems = refs[:ns], refs[ns:]
        s = pl.program_id(0)

        @pl.when(s == 0)
        def _():
            rider.start(r_ins, r_outs, r_sems)

        body(*ins, *outs, *scratch)

        @pl.when(s == (3 * nsteps) // 4)
        def _():
            rider.mid(r_ins, r_outs, r_sems)

        @pl.when(s == nsteps - 1)
        def _():
            rider.finish(r_ins, r_outs, r_sems)

    outs = pl.pallas_call(
        wrapped, name=name, grid=(nsteps,), in_specs=list(in_specs) + [ANY] * ri, out_specs=list(out_specs) + [ANY] * ro,
        out_shape=list(out_shape) + rider.out_shape, scratch_shapes=list(scratch_shapes) + rider.sems,
        input_output_aliases=aliases, compiler_params=_cp())(*args, *rider.ins)
    return outs[:no], outs[no:]


def _dot(a, b):
    return jnp.dot(a, b, preferred_element_type=F32)


def _dot_nt(a, b):
    return lax.dot_general(a, b, (((1,), (1,)), ((), ())), preferred_element_type=F32)


def _dot_tn(a, b):
    return lax.dot_general(a, b, (((0,), (0,)), ((), ())), preferred_element_type=F32)


def _sigmoid(x):
    return 1.0 / (1.0 + jnp.exp(-x))


def _kind(i):
    return jnp.where(i < CTX // TM, 0, 1)


def _sel(kind, mod_ref, k):
    return mod_ref[kind, k:k + 1, :]


def _acc2(ref, k, val, kind):
    ref[kind, k:k + 1, :] += jnp.sum(val, axis=0, keepdims=True)


def _norm_mod(h, g, shift, scale):
    r = lax.rsqrt(jnp.mean(h * h, axis=-1, keepdims=True) + EPS)
    n = h * r
    return (n * g) * (1.0 + scale) + shift, n, r


def _norm_mod_bwd(dz, n, r, g, scale):
    dn = dz * (g * (1.0 + scale))
    return r * (dn - n * jnp.mean(dn * n, axis=-1, keepdims=True))


def ffn_fwd(h, mod, g, win, wout, name, rider=None, loss=None):
    split = isinstance(h, tuple)
    nc = CTX // TM
    t = h[0].shape[0] + h[1].shape[0] if split else h.shape[0]

    def body(*refs):
        refs = list(refs)
        win_v, wout_v, a_v = refs[-3:]
        rows = refs[:2] if split else refs[:1]
        mod_ref, g_ref, win_hbm, wout_hbm = refs[len(rows):len(rows) + 4]
        rest = refs[len(rows) + 4:-3]
        if loss is not None:
            fg_ref, tgt_ref, rest = rest[0], rest[1], rest[2:]
        out_ref, gu_ref, acc_ref, rest = rest[0], rest[1], rest[2], rest[3:]
        i = pl.program_id(0)

        @pl.when(i == 0)
        def _():
            pltpu.sync_copy(win_hbm, win_v)
            pltpu.sync_copy(wout_hbm, wout_v)

        if split:
            hh = jnp.where(i < nc, rows[0][...], rows[1][...])
            rest[0][...] = hh
        else:
            hh = rows[0][...]
        ic = _kind(i)
        z, _, _ = _norm_mod(hh, g_ref[...], _sel(ic, mod_ref, 0), _sel(ic, mod_ref, 1))
        zb = z.astype(BF16)
        for j in range(D_FF // HT):
            gb, ub = slice(j * HT, (j + 1) * HT), slice(D_FF + j * HT, D_FF + (j + 1) * HT)
            gg = _dot_nt(zb, win_v[gb, :])
            uu = _dot_nt(zb, win_v[ub, :])
            gu_ref[:, gb] = gg.astype(BF16)
            gu_ref[:, ub] = uu.astype(BF16)
            a_v[:, gb] = ((gg * _sigmoid(gg)) * uu).astype(BF16)
        acc = _dot(a_v[...], wout_v[...])
        acc_ref[...] = acc
        hn = hh + (0.5 * _sel(ic, mod_ref, 2)) * acc
        if loss is None:
            out_ref[...] = hn
        else:
            loss_ref, dgf_ref = rest

            @pl.when(i == 0)
            def _():
                loss_ref[...] = jnp.zeros_like(loss_ref)
                dgf_ref[...] = jnp.zeros_like(dgf_ref)

            @pl.when(i < nc)
            def _():
                out_ref[...] = jnp.zeros_like(out_ref)

            @pl.when(i >= nc)
            def _():
                gain = fg_ref[...]
                r = lax.rsqrt(jnp.mean(hn * hn, axis=-1, keepdims=True) + EPS)
                n = hn * r
                err = n * gain - tgt_ref[...]
                loss_ref[...] += 0.5 * jnp.sum(jnp.mean(err * err, axis=-1, keepdims=True))
                dy = err * (1.0 / D)
                dgf_ref[...] += jnp.sum(dy * n, axis=0, keepdims=True)
                dn = dy * gain
                out_ref[...] = r * (dn - n * jnp.mean(dn * n, axis=-1, keepdims=True))

    row = pl.BlockSpec((TM, D), lambda i: (i, 0))
    vec = pl.BlockSpec((1, D), lambda i: (0, 0))
    rshape = jax.ShapeDtypeStruct((t, D), F32)
    if split:
        rows_in = [pl.BlockSpec((TM, D), lambda i: (jnp.minimum(i, nc - 1), 0)),
                   pl.BlockSpec((TM, D), lambda i: (jnp.maximum(i - nc, 0), 0))]
    else:
        rows_in = [row]
    in_specs = rows_in + [pl.BlockSpec((2, 3, D), lambda i: (0, 0, 0)), vec, ANY, ANY]
    out_specs = [row, pl.BlockSpec((TM, 2 * D_FF), lambda i: (i, 0)), row] + ([row] if split else [])
    out_shape = [rshape, jax.ShapeDtypeStruct((t, 2 * D_FF), BF16), rshape] + ([rshape] if split else [])
    args = (*(h if split else (h,)), mod, g, win, wout)
    if loss is not None:
        in_specs += [vec, pl.BlockSpec((TM, D), lambda i: (jnp.maximum(i - nc, 0), 0))]
        out_specs += [pl.BlockSpec((8, 128), lambda i: (0, 0)), vec]
        out_shape += [jax.ShapeDtypeStruct((8, 128), F32), jax.ShapeDtypeStruct((1, D), F32)]
        args += tuple(loss)
    return _grid_call(
        body, name=name, nsteps=t // TM, in_specs=in_specs, out_specs=out_specs, out_shape=out_shape,
        scratch_shapes=[pltpu.VMEM((2 * D_FF, D), BF16), pltpu.VMEM((D_FF, D), BF16), pltpu.VMEM((TM, D_FF), BF16)],
        args=args, rider=rider)


def ffn_bwd_a(dy, acc, gu, mod, wout, name, rider=None):
    t = dy.shape[0]
    nt = t // TM

    def body(dy_ref, acc_ref, gu_ref, mod_ref, wout_hbm, dp_ref, dwout_hbm, dgate_ref, wout_v, dwout_v):
        i = pl.program_id(0)

        @pl.when(i == 0)
        def _():
            pltpu.sync_copy(wout_hbm, wout_v)
            dwout_v[...] = jnp.zeros_like(dwout_v)
            dgate_ref[...] = jnp.zeros_like(dgate_ref)

        dy_ = dy_ref[...]
        ic = _kind(i)
        _acc2(dgate_ref, 0, 0.5 * dy_ * acc_ref[...], ic)
        daccb = ((0.5 * _sel(ic, mod_ref, 2)) * dy_).astype(BF16)
        for j in range(D_FF // HT):
            blk, ublk = slice(j * HT, (j + 1) * HT), slice(D_FF + j * HT, D_FF + (j + 1) * HT)
            da = _dot_nt(daccb, wout_v[blk, :])
            gg = gu_ref[:, blk].astype(F32)
            uu = gu_ref[:, ublk].astype(F32)
            s = _sigmoid(gg)
            sl = gg * s
            dwout_v[blk, :] += _dot_tn((sl * uu).astype(BF16), daccb)
            dp_ref[:, blk] = (da * uu * (s + sl * (1.0 - s))).astype(BF16)
            dp_ref[:, ublk] = (da * sl).astype(BF16)

        @pl.when(i == nt - 1)
        def _():
            pltpu.sync_copy(dwout_v, dwout_hbm)

    row = pl.BlockSpec((TM, D), lambda i: (i, 0))
    wide = pl.BlockSpec((TM, 2 * D_FF), lambda i: (i, 0))
    return _grid_call(
        body, name=name, nsteps=nt,
        in_specs=[row, row, wide, pl.BlockSpec((2, 3, D), lambda i: (0, 0, 0)), ANY],
        out_specs=[wide, ANY, pl.BlockSpec((2, 1, D), lambda i: (0, 0, 0))],
        out_shape=[jax.ShapeDtypeStruct((t, 2 * D_FF), BF16), jax.ShapeDtypeStruct((D_FF, D), F32),
                   jax.ShapeDtypeStruct((2, 1, D), F32)],
        scratch_shapes=[pltpu.VMEM((D_FF, D), BF16), pltpu.VMEM((D_FF, D), F32)],
        args=(dy, acc, gu, mod, wout), rider=rider)


def ffn_bwd_b(dy, h, dp, mod, g, win, name, rider=None, latent_only=False):
    t = dy.shape[0]
    nt = t // TM
    nc = CTX // TM

    def body(dy_ref, h_ref, dp_ref, mod_ref, g_ref, win_hbm, dh_ref, dwin_hbm, dmod_ref, dg_ref, win_v, dwin_v):
        i = pl.program_id(0)

        @pl.when(i == 0)
        def _():
            pltpu.sync_copy(win_hbm, win_v)
            dwin_v[...] = jnp.zeros_like(dwin_v)
            dmod_ref[...] = jnp.zeros_like(dmod_ref)
            dg_ref[...] = jnp.zeros_like(dg_ref)

        ic = _kind(i)
        gain = g_ref[...]
        scale = _sel(ic, mod_ref, 1)
        z, n, r = _norm_mod(h_ref[...], gain, _sel(ic, mod_ref, 0), scale)
        zb = z.astype(BF16)
        dz = _dot(dp_ref[...], win_v[...])
        for j in range(2 * D_FF // WT):
            blk = slice(j * WT, (j + 1) * WT)
            dwin_v[blk, :] += _dot_tn(dp_ref[:, blk], zb)
        _acc2(dmod_ref, 0, dz, ic)
        _acc2(dmod_ref, 1, dz * (n * gain), ic)
        dg_ref[...] += jnp.sum(dz * (1.0 + scale) * n, axis=0, keepdims=True)
        dh_ref[...] = dy_ref[...] + _norm_mod_bwd(dz, n, r, gain, scale)

        @pl.when(i == nt - 1)
        def _():
            pltpu.sync_copy(dwin_v, dwin_hbm)

    row = pl.BlockSpec((TM, D), lambda i: (i, 0))
    if latent_only:
        dh_spec = pl.BlockSpec((TM, D), lambda i: (jnp.maximum(i - nc, 0), 0))
        dh_shape = jax.ShapeDtypeStruct((t - CTX, D), F32)
    else:
        dh_spec, dh_shape = row, jax.ShapeDtypeStruct((t, D), F32)
    return _grid_call(
        body, name=name, nsteps=nt,
        in_specs=[row, row, pl.BlockSpec((TM, 2 * D_FF), lambda i: (i, 0)),
                  pl.BlockSpec((2, 3, D), lambda i: (0, 0, 0)), pl.BlockSpec((1, D), lambda i: (0, 0)), ANY],
        out_specs=[dh_spec, ANY, pl.BlockSpec((2, 2, D), lambda i: (0, 0, 0)), pl.BlockSpec((1, D), lambda i: (0, 0))],
        out_shape=[dh_shape, jax.ShapeDtypeStruct((2 * D_FF, D), F32),
                   jax.ShapeDtypeStruct((2, 2, D), F32), jax.ShapeDtypeStruct((1, D), F32)],
        scratch_shapes=[pltpu.VMEM((2 * D_FF, D), BF16), pltpu.VMEM((2 * D_FF, D), F32)],
        args=(dy, h, dp, mod, g, win), rider=rider)


def ffn_bwd_dw(h, dp, mod, g, name, rider=None):
    t = h.shape[0]
    nt = t // TM

    def body(h_ref, dp_ref, mod_ref, g_ref, dwin_hbm, dwin_v):
        i = pl.program_id(0)

        @pl.when(i == 0)
        def _():
            dwin_v[...] = jnp.zeros_like(dwin_v)

        ic = _kind(i)
        z, _, _ = _norm_mod(h_ref[...], g_ref[...], _sel(ic, mod_ref, 0), _sel(ic, mod_ref, 1))
        zb = z.astype(BF16)
        for j in range(2 * D_FF // WT):
            blk = slice(j * WT, (j + 1) * WT)
            dwin_v[blk, :] += _dot_tn(dp_ref[:, blk], zb)

        @pl.when(i == nt - 1)
        def _():
            pltpu.sync_copy(dwin_v, dwin_hbm)

    return _grid_call(
        body, name=name, nsteps=nt,
        in_specs=[pl.BlockSpec((TM, D), lambda i: (i, 0)), pl.BlockSpec((TM, 2 * D_FF), lambda i: (i, 0)),
                  pl.BlockSpec((2, 3, D), lambda i: (0, 0, 0)), pl.BlockSpec((1, D), lambda i: (0, 0))],
        out_specs=[ANY], out_shape=[jax.ShapeDtypeStruct((2 * D_FF, D), F32)],
        scratch_shapes=[pltpu.VMEM((2 * D_FF, D), F32)],
        args=(h, dp, mod, g), rider=rider)


def ffn_bwd_dh(dy, h, dp, mod, g, win, name, tiles, carry=None, rider=None):
    t = dy.shape[0]
    nc = CTX // TM
    t0, t1 = tiles

    def body(*refs):
        if carry is None:
            dy_ref, h_ref, dp_ref, mod_ref, g_ref, win_hbm, dh_ref, dmod_ref, dg_ref, win_v = refs
        else:
            dy_ref, h_ref, dp_ref, mod_ref, g_ref, win_hbm, _, dmod0_ref, dg0_ref, dh_ref, dmod_ref, dg_ref, win_v = refs
        i = pl.program_id(0)

        @pl.when(i == 0)
        def _():
            pltpu.sync_copy(win_hbm, win_v)
            dmod_ref[...] = jnp.zeros_like(dmod_ref) if carry is None else dmod0_ref[...]
            dg_ref[...] = jnp.zeros_like(dg_ref) if carry is None else dg0_ref[...]

        ic = _kind(i + t0)
        gain = g_ref[...]
        scale = _sel(ic, mod_ref, 1)
        _, n, r = _norm_mod(h_ref[...], gain, _sel(ic, mod_ref, 0), scale)
        dz = _dot(dp_ref[...], win_v[...])
        _acc2(dmod_ref, 0, dz, ic)
        _acc2(dmod_ref, 1, dz * (n * gain), ic)
        dg_ref[...] += jnp.sum(dz * (1.0 + scale) * n, axis=0, keepdims=True)
        dh_ref[...] = dy_ref[...] + _norm_mod_bwd(dz, n, r, gain, scale)

    row = pl.BlockSpec((TM, D), lambda i: (i + t0, 0))
    small = [pl.BlockSpec((2, 2, D), lambda i: (0, 0, 0)), pl.BlockSpec((1, D), lambda i: (0, 0))]
    in_specs = [row, row, pl.BlockSpec((TM, 2 * D_FF), lambda i: (i + t0, 0)),
                pl.BlockSpec((2, 3, D), lambda i: (0, 0, 0)), pl.BlockSpec((1, D), lambda i: (0, 0)), ANY]
    args = (dy, h, dp, mod, g, win)
    if carry is not None:
        in_specs += [ANY] + small
        args += tuple(carry)
    return _grid_call(
        body, name=name, nsteps=t1 - t0, in_specs=in_specs,
        out_specs=[pl.BlockSpec((TM, D), lambda i: (jnp.maximum(i + t0 - nc, 0), 0))] + small,
        out_shape=[jax.ShapeDtypeStruct((t - CTX, D), F32), jax.ShapeDtypeStruct((2, 2, D), F32),
                   jax.ShapeDtypeStruct((1, D), F32)],
        scratch_shapes=[pltpu.VMEM((2 * D_FF, D), BF16)],
        args=args, rider=rider, aliases=None if carry is None else {6: 0})


def mix_in_fwd(h, mod, g, wmix, name):
    t = h.shape[0]

    def body(h_ref, mod_ref, g_ref, w_hbm, p_ref, w_v):
        i = pl.program_id(0)

        @pl.when(i == 0)
        def _():
            pltpu.sync_copy(w_hbm, w_v)

        zs = []
        for k in range(SUB):
            ic = _kind(SUB * i + k)
            z, _, _ = _norm_mod(h_ref[pl.ds(k * TM, TM), :], g_ref[...], _sel(ic, mod_ref, 0), _sel(ic, mod_ref, 1))
            zs.append(z.astype(BF16))
        zb = jnp.concatenate(zs, axis=0)
        for dd in range(N_DEV):
            p_ref[:, dd * N_MIX_SHARD:(dd + 1) * N_MIX_SHARD] = _dot(zb, w_v[dd])

    return pl.pallas_call(
        body, name=name, grid=(t // TMX,),
        in_specs=[pl.BlockSpec((TMX, D), lambda i: (i, 0)), pl.BlockSpec((2, 3, D), lambda i: (0, 0, 0)),
                  pl.BlockSpec((1, D), lambda i: (0, 0)), ANY],
        out_specs=pl.BlockSpec((TMX, IN_PROJ), lambda i: (i, 0)),
        out_shape=jax.ShapeDtypeStruct((t, IN_PROJ), F32),
        scratch_shapes=[pltpu.VMEM((N_DEV, D, N_MIX_SHARD), BF16)],
        compiler_params=_cp(),
    )(h, mod, g, wmix)


def _halo_specs(nt, tile_of, rows=TM):
    nb = nt * (rows // HALO)
    main = pl.BlockSpec((rows, LRU_W), lambda s: (tile_of(s), 0))
    prev = pl.BlockSpec((HALO, LRU_W), lambda s: (jnp.maximum(tile_of(s) * (rows // HALO) - 1, 0), 0))
    nxt = pl.BlockSpec((HALO, LRU_W), lambda s: (jnp.minimum((tile_of(s) + 1) * (rows // HALO), nb - 1), 0))
    return main, prev, nxt


def _ext(tile, nt, main, prev, nxt):
    has_prev = jnp.logical_and(tile != 0, tile != 1)
    has_next = jnp.logical_and(tile != 0, tile != nt - 1)
    return jnp.concatenate([jnp.where(has_prev, prev, 0.0), main, jnp.where(has_next, nxt, 0.0)], axis=0)


def _shifted(ext, off):
    n = ext.shape[0]
    return pltpu.roll(ext, (-off) % n, 0)[HALO:HALO + TM]


def _conv(ext, cw_ref, cb_ref):
    xc = cb_ref[...] + cw_ref[0:1, :] * _shifted(ext, -2)
    for k in range(1, CONV_W):
        xc = xc + cw_ref[k:k + 1, :] * _shifted(ext, k - 2)
    return xc


def _log1p(y):
    return jnp.where(y < 1e-2, y * (1.0 - y * (0.5 - y * (1.0 / 3.0 - 0.25 * y))), jnp.log(1.0 + y))


def _softplus_neg(lam):
    return jnp.maximum(-lam, 0.0) + _log1p(jnp.exp(-jnp.abs(lam)))


def _one_minus_exp(x, exp_half):
    p = x * (1.0 + x * (1 / 2 + x * (1 / 6 + x * (1 / 24))))
    return jnp.where(x > -0.1, -p, 1.0 - exp_half * exp_half)


def _gates(xc, wr, br, wi, bi, lam):
    xb = xc.astype(BF16)
    r = _sigmoid(_dot(xb, wr) + br)
    ig = _sigmoid(_dot(xb, wi) + bi)
    sp = _softplus_neg(lam)
    log_a = -RG_C * r * sp
    a = jnp.exp(log_a)
    mult = jnp.sqrt(_one_minus_exp(2.0 * log_a, a))
    return r, ig, sp, a, mult


def _scan(a, b, reverse):
    n = a.shape[0]
    row = lax.broadcasted_iota(jnp.int32, a.shape, 0)
    s = 1
    while s < n:
        if s < HALO:
            if reverse:
                keep = row < n - s
                a_s = jnp.where(keep, pltpu.roll(a, n - s, 0), 1.0)
                b_s = jnp.where(keep, pltpu.roll(b, n - s, 0), 0.0)
            else:
                keep = row >= s
                a_s = jnp.where(keep, pltpu.roll(a, s, 0), 1.0)
                b_s = jnp.where(keep, pltpu.roll(b, s, 0), 0.0)
            b = a * b_s + b
            a = a * a_s
        elif reverse:
            b = jnp.concatenate([a[:n - s] * b[s:] + b[:n - s], b[n - s:]], axis=0)
            a = jnp.concatenate([a[:n - s] * a[s:], a[n - s:]], axis=0)
        else:
            b = jnp.concatenate([b[:s], a[s:] * b[:n - s] + b[s:]], axis=0)
            a = jnp.concatenate([a[:s], a[s:] * a[:n - s]], axis=0)
        s *= 2
    return a, b


def lru_fwd(p, conv_w, conv_b, wr, br, wi, bi, lam, reverse, name, rider=None):
    t = p.shape[0]
    nt = t // TM

    def tile_of(s):
        return jnp.where(s == 0, 0, nt - s) if reverse else s

    def body(x_ref, xp_ref, xn_ref, cw_ref, cb_ref, wr_ref, br_ref, wi_ref, bi_ref, lam_ref, h_ref, carry):
        s = pl.program_id(0)
        tile = tile_of(s)

        @pl.when(s == 0)
        def _():
            carry[...] = jnp.zeros_like(carry)

        ext = _ext(tile, nt, x_ref[...], xp_ref[...], xn_ref[...])
        xc = _conv(ext, cw_ref, cb_ref)
        _, ig, _, a, mult = _gates(xc, wr_ref[...], br_ref[...], wi_ref[...], bi_ref[...], lam_ref[...])
        a_cum, hl = _scan(a, mult * (ig * xc), reverse)
        hh = hl + a_cum * carry[...]
        h_ref[...] = hh
        carry[...] = hh[0:1, :] if reverse else hh[TM - 1:TM, :]

    main, prev, nxt = _halo_specs(nt, tile_of)
    vec = pl.BlockSpec((1, LRU_W), lambda s: (0, 0))
    mat = pl.BlockSpec((LRU_W, LRU_W), lambda s: (0, 0))
    outs, got = _grid_call(
        body, name=name, nsteps=nt,
        in_specs=[main, prev, nxt, pl.BlockSpec((CONV_W, LRU_W), lambda s: (0, 0)), vec, mat, vec, mat, vec, vec],
        out_specs=[main],
        out_shape=[jax.ShapeDtypeStruct((t, LRU_W), F32)],
        scratch_shapes=[pltpu.VMEM((1, LRU_W), F32)],
        args=(p, p, p, conv_w, conv_b, wr, br, wi, bi, lam), rider=rider)
    return outs[0], got


def lru_bwd(p, hs, dhs, conv_w, conv_b, wr, br, wi, bi, lam, reverse, name):
    t = p.shape[0]
    nt = t // TM
    bpt = TM // HALO

    def tile_of(s):
        return jnp.where(s == nt - 1, 0, s + 1) if reverse else nt - 1 - s

    def hprev_block(s):
        tile = tile_of(s)
        if reverse:
            return (jnp.where(tile == nt - 1, 0, jnp.minimum((tile + 1) * bpt, nt * bpt - 1)), 0)
        return (jnp.maximum(tile * bpt - 1, 0), 0)

    def body(x_ref, xp_ref, xn_ref, h_ref, hp_ref, dh_ref, cw_ref, cb_ref, wr_ref, br_ref, wi_ref, bi_ref, lam_ref,
             dxc_ref, dwr_ref, dwi_ref, dbr_ref, dbi_ref, dlam_ref, carry):
        s = pl.program_id(0)
        tile = tile_of(s)

        @pl.when(s == 0)
        def _():
            carry[...] = jnp.zeros_like(carry)
            for ref in (dwr_ref, dwi_ref, dbr_ref, dbi_ref, dlam_ref):
                ref[...] = jnp.zeros_like(ref)

        ext = _ext(tile, nt, x_ref[...], xp_ref[...], xn_ref[...])
        xc = _conv(ext, cw_ref, cb_ref)
        wr_, wi_ = wr_ref[...], wi_ref[...]
        r, ig, sp, a, mult = _gates(xc, wr_, br_ref[...], wi_, bi_ref[...], lam_ref[...])
        gated = ig * xc
        row = lax.broadcasted_iota(jnp.int32, (TM, LRU_W), 0)
        hh = h_ref[...]
        start = jnp.where(tile != 0, hp_ref[0:1, :] if reverse else hp_ref[HALO - 1:HALO, :], 0.0)
        if reverse:
            edge = row == TM - 1
            hprev = jnp.where(edge, start, pltpu.roll(hh, TM - 1, 0))
            coef = jnp.where(row == 0, 0.0, pltpu.roll(a, 1, 0))
            bb = dh_ref[...] + jnp.where(row == 0, carry[...], 0.0)
        else:
            edge = row == 0
            hprev = jnp.where(edge, start, pltpu.roll(hh, 1, 0))
            coef = jnp.where(row == TM - 1, 0.0, pltpu.roll(a, TM - 1, 0))
            bb = dh_ref[...] + jnp.where(row == TM - 1, carry[...], 0.0)
        _, lmb = _scan(coef, bb, not reverse)
        al = a * lmb
        carry[...] = al[TM - 1:TM, :] if reverse else al[0:1, :]

        dgated = lmb * mult
        dloga = (lmb * hprev) * a - (lmb * gated) * (a * a) / mult
        dpre_r = (dloga * (-RG_C * sp)) * r * (1.0 - r)
        dpre_i = (dgated * xc) * ig * (1.0 - ig)
        drb, dib = dpre_r.astype(BF16), dpre_i.astype(BF16)
        xb = xc.astype(BF16)
        dxc_ref[...] = dgated * ig + _dot_nt(drb, wr_) + _dot_nt(dib, wi_)
        dwr_ref[...] += _dot_tn(xb, drb)
        dwi_ref[...] += _dot_tn(xb, dib)
        dbr_ref[...] += jnp.sum(dpre_r, axis=0, keepdims=True)
        dbi_ref[...] += jnp.sum(dpre_i, axis=0, keepdims=True)
        dlam_ref[...] += jnp.sum(dloga * (-RG_C * r), axis=0, keepdims=True)

        @pl.when(s == nt - 1)
        def _():
            dlam_ref[...] = dlam_ref[...] * (-_sigmoid(-lam_ref[...]))

    main, prev, nxt = _halo_specs(nt, tile_of)
    vec = pl.BlockSpec((1, LRU_W), lambda s: (0, 0))
    mat = pl.BlockSpec((LRU_W, LRU_W), lambda s: (0, 0))
    vshape = jax.ShapeDtypeStruct((1, LRU_W), F32)
    mshape = jax.ShapeDtypeStruct((LRU_W, LRU_W), F32)
    return pl.pallas_call(
        body, name=name, grid=(nt,),
        in_specs=[main, prev, nxt, main, pl.BlockSpec((HALO, LRU_W), hprev_block), main,
                  pl.BlockSpec((CONV_W, LRU_W), lambda s: (0, 0)), vec, mat, vec, mat, vec, vec],
        out_specs=[main, mat, mat, vec, vec, vec],
        out_shape=[jax.ShapeDtypeStruct((t, LRU_W), F32), mshape, mshape, vshape, vshape, vshape],
        scratch_shapes=[pltpu.VMEM((1, LRU_W), F32)],
        compiler_params=_cp(),
    )(p, p, p, hs, hs, dhs, conv_w, conv_b, wr, br, wi, bi, lam)


GELU_C = 0.7978845608028654
GELU_A = 0.044715


def _gelu(x):
    th = jnp.tanh(GELU_C * (x + GELU_A * x * x * x))
    return 0.5 * x * (1.0 + th), th


def _sgu(v, gain, w_ref, bt_ref):
    mu = jnp.mean(v, axis=-1, keepdims=True)
    xc = v - mu
    rs = lax.rsqrt(jnp.mean(xc * xc, axis=-1, keepdims=True) + EPS)
    vhat = xc * rs
    vnb = (vhat * gain).astype(BF16)
    chunks = []
    for ch in range(TM // CHUNK):
        zs = []
        for gi in range(GROUPS):
            vb = vnb[ch * CHUNK:(ch + 1) * CHUNK, gi * GROUP_DIM:(gi + 1) * GROUP_DIM]
            zs.append(_dot(w_ref[gi].astype(BF16), vb) + bt_ref[:, gi:gi + 1])
        chunks.append(jnp.concatenate(zs, axis=1))
    return jnp.concatenate(chunks, axis=0), vhat, rs, vnb


def _pcols(k, rows=TM):
    return pl.BlockSpec((rows, LRU_W), lambda i: (i, k))


def mix_out_fwd(h, p, hf, hb, mod, sgu_g, sgu_w, sgu_bt, womix, name):
    t = h.shape[0]

    def body(h_ref, gl_ref, u_ref, v_ref, hf_ref, hb_ref, mod_ref, sg_ref, sw_ref, sb_ref, w_hbm, out_ref, o_ref, w_v):
        i = pl.program_id(0)

        @pl.when(i == 0)
        def _():
            pltpu.sync_copy(w_hbm, w_v)

        ys = []
        for k in range(SUB):
            rows = pl.ds(k * TM, TM)
            ge, _ = _gelu(gl_ref[rows, :])
            y_lru = (hf_ref[rows, :] + hb_ref[rows, :]) * ge
            z, _, _, _ = _sgu(v_ref[rows, :], sg_ref[...], sw_ref, sb_ref)
            ys.append(jnp.concatenate([y_lru, u_ref[rows, :] * z], axis=1).astype(BF16))
        o = _dot(jnp.concatenate(ys, axis=0), w_v[...])
        o_ref[...] = o
        for k in range(SUB):
            rows = pl.ds(k * TM, TM)
            out_ref[rows, :] = h_ref[rows, :] + _sel(_kind(SUB * i + k), mod_ref, 2) * o[k * TM:(k + 1) * TM]

    row = pl.BlockSpec((TMX, D), lambda i: (i, 0))
    half = pl.BlockSpec((TMX, LRU_W), lambda i: (i, 0))
    return pl.pallas_call(
        body, name=name, grid=(t // TMX,),
        in_specs=[row, _pcols(1, TMX), _pcols(2, TMX), _pcols(3, TMX), half, half, pl.BlockSpec((2, 3, D), lambda i: (0, 0, 0)),
                  pl.BlockSpec((1, MLP_W), lambda i: (0, 0)), pl.BlockSpec((GROUPS, CHUNK, CHUNK), lambda i: (0, 0, 0)),
                  pl.BlockSpec((CHUNK, GROUPS), lambda i: (0, 0)), ANY],
        out_specs=[row, row],
        out_shape=[jax.ShapeDtypeStruct((t, D), F32), jax.ShapeDtypeStruct((t, D), F32)],
        scratch_shapes=[pltpu.VMEM((D, D), BF16)],
        compiler_params=_cp(),
    )(h, p, p, p, hf, hb, mod, sgu_g, sgu_w, sgu_bt, womix)


def mix_out_bwd(dy, p, hf, hb, o, mod, sgu_g, sgu_w, sgu_bt, womix, name, rider=None):
    t = dy.shape[0]

    def body(dy_ref, gl_ref, u_ref, v_ref, hf_ref, hb_ref, o_ref, mod_ref, sg_ref, sw_ref, sb_ref, w_hbm,
             dhs_ref, dp_ref, dw_ref, dgate_ref, dsg_ref, dsw_ref, dsb_ref, w_v):
        i = pl.program_id(0)

        @pl.when(i == 0)
        def _():
            pltpu.sync_copy(w_hbm, w_v)
            for ref in (dw_ref, dgate_ref, dsg_ref, dsw_ref, dsb_ref):
                ref[...] = jnp.zeros_like(ref)

        ys, dobs = [], []
        for k in range(SUB):
            rows = pl.ds(k * TM, TM)
            ic = _kind(SUB * i + k)
            dy_ = dy_ref[rows, :]
            _acc2(dgate_ref, 0, dy_ * o_ref[rows, :], ic)
            dob = (_sel(ic, mod_ref, 2) * dy_).astype(BF16)

            gl = gl_ref[rows, :]
            ge, th = _gelu(gl)
            hsum = hf_ref[rows, :] + hb_ref[rows, :]
            gain = sg_ref[...]
            uu = u_ref[rows, :]
            z, vhat, rs, vnb = _sgu(v_ref[rows, :], gain, sw_ref, sb_ref)
            ys.append(jnp.concatenate([hsum * ge, uu * z], axis=1).astype(BF16))
            dobs.append(dob)
            dyy = _dot_nt(dob, w_v[...])
            dyl, dys = dyy[:, :LRU_W], dyy[:, LRU_W:]

            dhs_ref[rows, :] = dyl * ge
            dge = 0.5 * (1.0 + th) + 0.5 * gl * (1.0 - th * th) * (GELU_C * (1.0 + 3.0 * GELU_A * gl * gl))
            dp_ref[rows, 0:LRU_W] = (dyl * hsum * dge).astype(BF16)
            dp_ref[rows, LRU_W:2 * LRU_W] = (dys * z).astype(BF16)

            dz = dys * uu
            dzb = dz.astype(BF16)
            dvn_chunks, dsb_cols = [], [jnp.zeros((CHUNK, 1), F32)] * GROUPS
            for ch in range(TM // CHUNK):
                cols = []
                for gi in range(GROUPS):
                    rs_, cs_ = slice(ch * CHUNK, (ch + 1) * CHUNK), slice(gi * GROUP_DIM, (gi + 1) * GROUP_DIM)
                    dzg = dzb[rs_, cs_]
                    dsb_cols[gi] = dsb_cols[gi] + jnp.sum(dz[rs_, cs_], axis=1, keepdims=True)
                    dsw_ref[gi] += _dot_nt(dzg, vnb[rs_, cs_])
                    cols.append(_dot_tn(sw_ref[gi].astype(BF16), dzg))
                dvn_chunks.append(jnp.concatenate(cols, axis=1))
            dsb_ref[...] += jnp.concatenate(dsb_cols, axis=1)
            dvn = jnp.concatenate(dvn_chunks, axis=0)
            dsg_ref[...] += jnp.sum(dvn * vhat, axis=0, keepdims=True)
            dvh = dvn * gain
            dv = rs * (dvh - jnp.mean(dvh, axis=-1, keepdims=True) - vhat * jnp.mean(dvh * vhat, axis=-1, keepdims=True))
            dp_ref[rows, 2 * LRU_W:3 * LRU_W] = dv.astype(BF16)
        dw_ref[...] += _dot_tn(jnp.concatenate(ys, axis=0), jnp.concatenate(dobs, axis=0))

    row = pl.BlockSpec((TMX, D), lambda i: (i, 0))
    half = pl.BlockSpec((TMX, LRU_W), lambda i: (i, 0))
    const2 = lambda i: (0, 0)
    const3 = lambda i: (0, 0, 0)
    return _grid_call(
        body, name=name, nsteps=t // TMX,
        in_specs=[row, _pcols(1, TMX), _pcols(2, TMX), _pcols(3, TMX), half, half, row, pl.BlockSpec((2, 3, D), const3),
                  pl.BlockSpec((1, MLP_W), const2), pl.BlockSpec((GROUPS, CHUNK, CHUNK), const3),
                  pl.BlockSpec((CHUNK, GROUPS), const2), ANY],
        out_specs=[half, pl.BlockSpec((TMX, 3 * LRU_W), lambda i: (i, 0)), pl.BlockSpec((D, D), const2),
                   pl.BlockSpec((2, 1, D), const3), pl.BlockSpec((1, MLP_W), const2),
                   pl.BlockSpec((GROUPS, CHUNK, CHUNK), const3), pl.BlockSpec((CHUNK, GROUPS), const2)],
        out_shape=[jax.ShapeDtypeStruct((t, LRU_W), F32), jax.ShapeDtypeStruct((t, 3 * LRU_W), BF16),
                   jax.ShapeDtypeStruct((D, D), F32), jax.ShapeDtypeStruct((2, 1, D), F32),
                   jax.ShapeDtypeStruct((1, MLP_W), F32), jax.ShapeDtypeStruct((GROUPS, CHUNK, CHUNK), F32),
                   jax.ShapeDtypeStruct((CHUNK, GROUPS), F32)],
        scratch_shapes=[pltpu.VMEM((D, D), BF16)],
        args=(dy, p, p, p, hf, hb, o, mod, sgu_g, sgu_w, sgu_bt, womix), rider=rider)


def mix_in_bwd(dy, h, p, dxf, dxb, dprest, mod, g, conv_w, wmix, name, rider=None):
    t = dy.shape[0]
    nt = t // TM

    def body(dy_ref, h_ref, x_ref, xp_ref, xn_ref, f_ref, fp_ref, fn_ref, b_ref, bp_ref, bn_ref, dpr_ref, mod_ref,
             g_ref, cw_ref, w_hbm, dh_ref, dw_hbm, dmod_ref, dg_ref, dcw_ref, dcb_ref, w_v, dw_v):
        i = pl.program_id(0)

        @pl.when(i == 0)
        def _():
            pltpu.sync_copy(w_hbm, w_v)
            dw_v[...] = jnp.zeros_like(dw_v)
            for ref in (dmod_ref, dg_ref, dcw_ref, dcb_ref):
                ref[...] = jnp.zeros_like(ref)

        def around(ref, before, after, k):
            prev = ref[pl.ds(k * TM - HALO, HALO), :] if k > 0 else before[...]
            nxt = ref[pl.ds((k + 1) * TM, HALO), :] if k < SUB - 1 else after[...]
            return ref[pl.ds(k * TM, TM), :], prev, nxt

        gain = g_ref[...]
        zs, dps, rs = [], [], []
        for k in range(SUB):
            f_m, f_p, f_n = around(f_ref, fp_ref, fn_ref, k)
            b_m, b_p, b_n = around(b_ref, bp_ref, bn_ref, k)
            dmain = f_m + b_m
            dext = _ext(SUB * i + k, nt, dmain, f_p + b_p, f_n + b_n)
            xext = _ext(SUB * i + k, nt, *around(x_ref, xp_ref, xn_ref, k))
            dxl = cw_ref[0:1, :] * _shifted(dext, 2)
            for tap in range(1, CONV_W):
                dxl = dxl + cw_ref[tap:tap + 1, :] * _shifted(dext, 2 - tap)
            dcw_ref[...] += jnp.concatenate(
                [jnp.sum(dmain * _shifted(xext, tap - 2), axis=0, keepdims=True) for tap in range(CONV_W)], axis=0)
            dcb_ref[...] += jnp.sum(dmain, axis=0, keepdims=True)

            ic = _kind(SUB * i + k)
            rows = pl.ds(k * TM, TM)
            z, _, r = _norm_mod(h_ref[rows, :], gain, _sel(ic, mod_ref, 0), _sel(ic, mod_ref, 1))
            zs.append(z.astype(BF16))
            rs.append(r)
            dps.append(jnp.concatenate([dxl.astype(BF16), dpr_ref[rows, :]], axis=1))
        zb = jnp.concatenate(zs, axis=0)
        dpb = jnp.concatenate(dps, axis=0)
        dz_all = jnp.zeros((TMX, D), F32)
        for dd in range(N_DEV):
            dpd = dpb[:, dd * N_MIX_SHARD:(dd + 1) * N_MIX_SHARD]
            dz_all = dz_all + _dot_nt(dpd, w_v[dd])
            dw_v[dd] += _dot_tn(zb, dpd)
        for k in range(SUB):
            ic = _kind(SUB * i + k)
            rows = pl.ds(k * TM, TM)
            scale = _sel(ic, mod_ref, 1)
            dz = dz_all[k * TM:(k + 1) * TM]
            n = h_ref[rows, :] * rs[k]
            _acc2(dmod_ref, 0, dz, ic)
            _acc2(dmod_ref, 1, dz * (n * gain), ic)
            dg_ref[...] += jnp.sum(dz * (1.0 + scale) * n, axis=0, keepdims=True)
            dh_ref[rows, :] = dy_ref[rows, :] + _norm_mod_bwd(dz, n, rs[k], gain, scale)

        @pl.when(i == nt // SUB - 1)
        def _():
            pltpu.sync_copy(dw_v, dw_hbm)

    main, prev, nxt = _halo_specs(nt // SUB, lambda s: s, TMX)
    row = pl.BlockSpec((TMX, D), lambda i: (i, 0))
    const2 = lambda i: (0, 0)
    return _grid_call(
        body, name=name, nsteps=nt // SUB,
        in_specs=[row, row, main, prev, nxt, main, prev, nxt, main, prev, nxt,
                  pl.BlockSpec((TMX, 3 * LRU_W), lambda i: (i, 0)), pl.BlockSpec((2, 3, D), lambda i: (0, 0, 0)),
                  pl.BlockSpec((1, D), const2), pl.BlockSpec((CONV_W, LRU_W), const2), ANY],
        out_specs=[row, ANY, pl.BlockSpec((2, 2, D), lambda i: (0, 0, 0)), pl.BlockSpec((1, D), const2),
                   pl.BlockSpec((CONV_W, LRU_W), const2), pl.BlockSpec((1, LRU_W), const2)],
        out_shape=[jax.ShapeDtypeStruct((t, D), F32), jax.ShapeDtypeStruct((N_DEV, D, N_MIX_SHARD), F32),
                   jax.ShapeDtypeStruct((2, 2, D), F32), jax.ShapeDtypeStruct((1, D), F32),
                   jax.ShapeDtypeStruct((CONV_W, LRU_W), F32), jax.ShapeDtypeStruct((1, LRU_W), F32)],
        scratch_shapes=[pltpu.VMEM((N_DEV, D, N_MIX_SHARD), BF16), pltpu.VMEM((N_DEV, D, N_MIX_SHARD), F32)],
        args=(dy, h, p, p, p, dxf, dxf, dxf, dxb, dxb, dxb, dprest, mod, g, conv_w, wmix), rider=rider)


def _block_diag(w):
    eye = jnp.eye(HEADS, dtype=w.dtype)
    return jnp.einsum("dhij,hk->dhikj", w, eye).reshape(2, LRU_W, LRU_W)


def _block_diag_inv(full):
    f = full.reshape(2, HEADS, HEAD_DIM, HEADS, HEAD_DIM)
    return jnp.stack([f[:, hd, :, hd, :] for hd in range(HEADS)], axis=1)


def small_layer(g1, gm, g2, conv_w, conv_b, w_r, b_r, w_i, b_i, lam, sgu_g, sgu_w, sgu_b):
    return dict(g1=g1[None, :], gm=gm[None, :], g2=g2[None, :], conv_w=conv_w, conv_b=conv_b[None, :],
                wr=_block_diag(w_r).astype(BF16), br=b_r[:, None, :], wi=_block_diag(w_i).astype(BF16),
                bi=b_i[:, None, :], lam=lam[:, None, :], sgu_g=sgu_g[None, :], sgu_w=sgu_w, sgu_bt=sgu_b.T)


def small_grads(g):
    out = dict(mix_norm_g=g["gm"][0], ffn2_norm_g=g["g2"][0], lru_conv_w=g["conv_w"],
               lru_conv_b=g["conv_b"][0], lru_w_r=_block_diag_inv(g["wr"]), lru_b_r=g["br"][:, 0, :],
               lru_w_i=_block_diag_inv(g["wi"]), lru_b_i=g["bi"][:, 0, :], lru_lambda=g["lam"][:, 0, :],
               sgu_norm_g=g["sgu_g"][0], sgu_w=g["sgu_w"], sgu_b=g["sgu_bt"].T)
    if "g1" in g:
        out["ffn1_norm_g"] = g["g1"][0]
    return out


BIG_KEYS = ("win1", "wout1", "wmix", "womix", "win2", "wout2")


def _as_blocks(key, g):
    return g if key == "wmix" else g.reshape(N_DEV, g.shape[0] // N_DEV, D)


def _gathered(key, a):
    return a if key == "wmix" else a.reshape(N_DEV * a.shape[1], D)


class _ReduceScatter:
    def __init__(self, c_idx, where):
        self.c_idx, self.where = c_idx, where
        self.out = {}

    def pair(self, group):
        return pair_rider([g for _, g in group])

    def after_pair(self, group, recv1, tag):
        return chips_rider(pair_sum([g for _, g in group], list(recv1), self.c_idx, f"pair_sum_{tag}"))

    def after_chips(self, group, recv1, recv2, tag):
        for (key, g), r1, r2 in zip(group, recv1, recv2):
            self.out[key] = (g, r1, r2)


FIRST_WEIGHTS = [(0, "win1"), (0, "wout1")]


def fwd_bwd(ctx_rows, x_rows, target, mods, shards, first_weights, smalls, final_g, c_idx, where):
    assert CTX == TM and len(shards) == 2

    def gather(keys_by_layer):
        return GatherRider([shards[l][k] for l, k in keys_by_layer])

    def put(full, keys_by_layer, got):
        for (l, k), a in zip(keys_by_layer, got):
            full[l][k] = _gathered(k, a)

    full = [dict(s) for s in smalls]
    put(full, FIRST_WEIGHTS, first_weights)
    riders = {
        "ffn1_fwd_0": [(0, "wmix"), (0, "womix"), (0, "win2")],
        "lru_fwd_0_0": [(0, "wout2")],
        "ffn2_fwd_0": [(1, "win1"), (1, "wout1")],
        "ffn1_fwd_1": [(1, "wmix"), (1, "womix"), (1, "win2")],
        "lru_fwd_1_0": [(1, "wout2")],
    }

    def ffn(which, l, h):
        name = f"ffn{which}_fwd_{l}"
        w = full[l]
        keys = riders.get(name)
        m = mods[l][:, 0:3] if which == 1 else mods[l][:, 6:9]
        last = (which, l) == (2, 1)
        outs, got = ffn_fwd(h, m, w[f"g{which}"], w[f"win{which}"], w[f"wout{which}"], name,
                            rider=gather(keys) if keys else None, loss=(final_g, target) if last else None)
        if keys:
            put(full, keys, got)
        return outs

    saved = []
    h = (ctx_rows, x_rows)
    for l in range(2):
        mm = mods[l][:, 3:6]
        outs = ffn(1, l, h)
        h1, gu1, acc1 = outs[:3]
        hin = outs[3] if l == 0 else h
        w = full[l]
        p = mix_in_fwd(h1, mm, w["gm"], w["wmix"], f"mix_in_fwd_{l}")
        hs = []
        for d in range(2):
            keys = riders.get(f"lru_fwd_{l}_{d}")
            hd, got = lru_fwd(p, w["conv_w"], w["conv_b"], w["wr"][d], w["br"][d], w["wi"][d], w["bi"][d], w["lam"][d],
                              bool(d), f"lru_fwd_{l}_{d}", rider=gather(keys) if keys else None)
            if keys:
                put(full, keys, got)
            hs.append(hd)
        h2, o = mix_out_fwd(h1, p, hs[0], hs[1], mm, w["sgu_g"], w["sgu_w"], w["sgu_bt"], w["womix"], f"mix_out_fwd_{l}")
        outs = ffn(2, l, h2)
        h, gu2, acc2 = outs[:3]
        saved.append((hin, h1, h2, gu1, acc1, p, hs, o, gu2, acc2))
    dh, (loss, dgf) = h, outs[3:]

    rs = _ReduceScatter(c_idx, where)
    grads, dmods, sums = [None, None], [None, None], [None, None]
    pending = None
    for l in (1, 0):
        w = full[l]
        m1, mm, m2 = mods[l][:, 0:3], mods[l][:, 3:6], mods[l][:, 6:9]
        hin, h1, h2, gu1, acc1, p, hs, o, gu2, acc2 = saved[l]
        g = {}
        rs.out = {}
        both = Riders([rs.pair(pending[0]), GatherRider([small_pack])]) if pending else None
        (dp2, g["wout2"], dgate2), got = ffn_bwd_a(dh, acc2, gu2, m2, w["wout2"], f"ffn2_bwd_a_{l}", rider=both)
        if pending:
            r1, (small_all,) = both.split(got)
        chips = rs.after_pair(pending[0], r1, pending[1]) if pending else None
        (dh, g["win2"], dmod2, g["g2"]), r2 = ffn_bwd_b(dh, h2, dp2, m2, w["g2"], w["win2"], f"ffn2_bwd_b_{l}", rider=chips)
        if pending:
            rs.after_chips(pending[0], r1, r2, pending[1])
            sums[l + 1].update(rs.out)
            rs.out = {}

        grp = [(k, _as_blocks(k, g[k])) for k in ("win2", "wout2")]
        (dhs, dprest, g["womix"], dgatem, g["sgu_g"], g["sgu_w"], g["sgu_bt"]), r1 = mix_out_bwd(
            dh, p, hs[0], hs[1], o, mm, w["sgu_g"], w["sgu_w"], w["sgu_bt"], w["womix"], f"mix_out_bwd_{l}",
            rider=rs.pair(grp))
        chips = rs.after_pair(grp, r1, f"a{l}")
        dx, per_dir = [], []
        for d in range(2):
            out = lru_bwd(p, hs[d], dhs, w["conv_w"], w["conv_b"], w["wr"][d], w["br"][d], w["wi"][d], w["bi"][d],
                          w["lam"][d], bool(d), f"lru_bwd_{l}_{d}")
            dx.append(out[0])
            per_dir.append(out[1:])
        for k, nm in enumerate(("wr", "wi", "br", "bi", "lam")):
            g[nm] = jnp.stack([per_dir[0][k], per_dir[1][k]])
        (dh, g["wmix"], dmodm, g["gm"], g["conv_w"], g["conv_b"]), r2 = mix_in_bwd(
            dh, h1, p, dx[0], dx[1], dprest, mm, w["gm"], w["conv_w"], w["wmix"], f"mix_in_bwd_{l}", rider=chips)
        rs.after_chips(grp, r1, r2, f"a{l}")
        sums[l] = dict(rs.out)
        rs.out = {}

        if l == 1:
            (dp1, g["wout1"], dgate1), _ = ffn_bwd_a(dh, acc1, gu1, m1, w["wout1"], f"ffn1_bwd_a_{l}")
            (dh, g["win1"], dmod1, g["g1"]), _ = ffn_bwd_b(dh, hin, dp1, m1, w["g1"], w["win1"], f"ffn1_bwd_b_{l}")
            pending = ([(k, _as_blocks(k, g[k])) for k in ("womix", "wmix", "wout1", "win1")], f"b{l}")
            per = small_grads(g)
            small_pack = _pack([per[n] for n in LAYER_SMALL])
        else:
            g_mix = [(k, _as_blocks(k, g[k])) for k in ("womix", "wmix")]
            (dp1, g["wout1"], dgate1), r1_mix = ffn_bwd_a(dh, acc1, gu1, m1, w["wout1"], f"ffn1_bwd_a_{l}",
                                                          rider=rs.pair(g_mix))
            g_out = [("wout1", _as_blocks("wout1", g["wout1"]))]
            per = small_grads(g)
            three = Riders([rs.after_pair(g_mix, r1_mix, f"b{l}"), rs.pair(g_out),
                            GatherRider([_pack([per[n] for n in LAYER_SMALL[1:]])])])
            (g["win1"],), got = ffn_bwd_dw(hin, dp1, m1, w["g1"], f"ffn1_bwd_dw_{l}", rider=three)
            r2_mix, r1_out, (small0_all,) = three.split(got)
            rs.after_chips(g_mix, r1_mix, r2_mix, f"b{l}")
            g_in = [("win1", _as_blocks("win1", g["win1"]))]
            both = Riders([rs.after_pair(g_out, r1_out, f"c{l}"), rs.pair(g_in)])
            nt = dh.shape[0] // TM
            cut0, cut1 = (3 * nt) // 8, nt - max(nt // 16, 1)
            part, got = ffn_bwd_dh(dh, hin, dp1, m1, w["g1"], w["win1"], f"ffn1_bwd_dh0_{l}", (0, cut0), rider=both)
            r2_out, r1_in = both.split(got)
            rs.after_chips(g_out, r1_out, r2_out, f"c{l}")
            part, r2_in = ffn_bwd_dh(dh, hin, dp1, m1, w["g1"], w["win1"], f"ffn1_bwd_dh1_{l}", (cut0, cut1), carry=part,
                                     rider=rs.after_pair(g_in, r1_in, f"d{l}"))
            rs.after_chips(g_in, r1_in, r2_in, f"d{l}")
            (dh, dmod1, g["g1"]), _ = ffn_bwd_dh(dh, hin, dp1, m1, w["g1"], w["win1"], f"ffn1_bwd_dh2_{l}", (cut1, nt),
                                                 carry=part)
            sums[l].update(rs.out)
        dmods[l] = jnp.concatenate([dmod1, dgate1, dmodm, dgatem, dmod2, dgate2], axis=1)
        grads[l] = g
    return loss, dh, jnp.stack(dmods), grads, (small0_all, small_all), sums, dgf


def _row_block(r, c, limit=262144):
    best = 8
    for rb in range(8, r + 1, 8):
        if r % rb == 0 and rb * c <= limit:
            best = rb
    return best


PAIR_SUM_SPLIT = 2


def pair_sum(grads, recv, c_idx, name):
    n = len(grads)

    def body(c_ref, *refs):
        for t in range(n):
            refs[2 * n + t][...] = (refs[t][...] + refs[n + t][...]).astype(BF16)

    mine, theirs, outs = [], [], []
    for g in grads:
        _, r, c = g.shape
        rb = r // PAIR_SUM_SPLIT
        assert rb % 16 == 0
        mine.append(pl.BlockSpec((1, rb, c), lambda j, i, c_ref: (2 * j + c_ref[0], i, 0)))
        theirs.append(pl.BlockSpec((1, rb, c), lambda j, i, c_ref: (j, i, 0)))
        outs.append(jax.ShapeDtypeStruct((4, r, c), BF16))
    return pl.pallas_call(
        body, name=name,
        grid_spec=pltpu.PrefetchScalarGridSpec(num_scalar_prefetch=1, grid=(4, PAIR_SUM_SPLIT), in_specs=mine + theirs,
                                               out_specs=list(theirs)),
        out_shape=outs, compiler_params=_cp(2),
    )(c_idx, *grads, *recv)


ADA_ROWS = 16


def _silu(v):
    return v * _sigmoid(v)


def ada_fwd(cond, w_ada, b_slab, name):
    def body(c_ref, w_ref, b_ref, o_ref):
        s = _silu(c_ref[...]).astype(BF16)
        o_ref[0] = _dot(s, w_ref[0].astype(BF16)) + b_ref[0]

    return pl.pallas_call(
        body, name=name, grid=(DEPTH,),
        in_specs=[pl.BlockSpec((ADA_ROWS, D), lambda l: (0, 0)), pl.BlockSpec((1, D, ADA_SHARD), lambda l: (l, 0, 0)),
                  pl.BlockSpec((1, 1, ADA_SHARD), lambda l: (l, 0, 0))],
        out_specs=pl.BlockSpec((1, ADA_ROWS, ADA_SHARD), lambda l: (l, 0, 0)),
        out_shape=jax.ShapeDtypeStruct((DEPTH, ADA_ROWS, ADA_SHARD), F32),
        compiler_params=_cp(),
    )(cond, w_ada, b_slab)


def ada_bwd(cond, dm_sample, dm_ctx, w_ada, name):
    def body(c_ref, ds_ref, dc_ref, w_ref, gw_ref, dsc_ref):
        @pl.when(pl.program_id(0) == 0)
        def _():
            dsc_ref[...] = jnp.zeros_like(dsc_ref)

        s = _silu(c_ref[...]).astype(BF16)
        dcs = dc_ref[0]
        tot = dcs[0:1]
        for j in range(1, N_DEV):
            tot = tot + dcs[j:j + 1]
        tot8 = jnp.where(lax.broadcasted_iota(jnp.int32, (N_DEV, ADA_SHARD), 0) == 0, tot, 0.0)
        dm = jnp.concatenate([ds_ref[0], tot8], axis=0).astype(BF16)
        gw_ref[0] = _dot_tn(s, dm)
        dsc_ref[...] += _dot_nt(dm, w_ref[0].astype(BF16))[N_DEV:N_DEV + 1]

    slab = pl.BlockSpec((1, N_DEV, ADA_SHARD), lambda l: (l, 0, 0))
    wspec = pl.BlockSpec((1, D, ADA_SHARD), lambda l: (l, 0, 0))
    return pl.pallas_call(
        body, name=name, grid=(DEPTH,),
        in_specs=[pl.BlockSpec((ADA_ROWS, D), lambda l: (0, 0)), slab, slab, wspec],
        out_specs=[wspec, pl.BlockSpec((1, D), lambda l: (0, 0))],
        out_shape=[jax.ShapeDtypeStruct((DEPTH, D, ADA_SHARD), F32), jax.ShapeDtypeStruct((1, D), F32)],
        compiler_params=_cp(),
    )(cond, dm_sample, dm_ctx, w_ada)


def sum_over_devices(parts, name, silu_rows=0, w=None):
    _, r, c = parts.shape

    def body(*refs):
        p_ref, o_ref = refs[0], refs[-1]
        tot = p_ref[0]
        for j in range(1, N_DEV):
            tot = tot + p_ref[j]
        o_ref[...] = tot
        if silu_rows:
            wv = refs[1][...]
            s = _sigmoid(wv)
            o_ref[0:silu_rows, :] = tot[0:silu_rows, :] * (s * (1.0 + wv * (1.0 - s)))

    vm = pl.BlockSpec(memory_space=pltpu.VMEM)
    args = (parts,) if w is None else (parts, w)
    return pl.pallas_call(
        body, name=name, in_specs=[vm] * len(args), out_specs=vm,
        out_shape=jax.ShapeDtypeStruct((r, c), F32),
        compiler_params=pltpu.CompilerParams(vmem_limit_bytes=VMEM_LIMIT),
    )(*args)


def sum_dmods(dm_all, name):
    def body(d_ref, o_ref):
        for l in range(DEPTH):
            tot = d_ref[0, l]
            for j in range(1, N_DEV):
                tot = tot + d_ref[j, l]
            o_ref[l:l + 1, :] = tot[0:1] + tot[1:2]

    vm = pl.BlockSpec(memory_space=pltpu.VMEM)
    return pl.pallas_call(
        body, name=name, in_specs=[vm], out_specs=vm,
        out_shape=jax.ShapeDtypeStruct((DEPTH, N_MOD * D), F32),
    )(dm_all)


ADAMW_BLOCK = 512 * 1024


def adamw(w, g, m, v, name, rider=None):
    r, c = w.shape
    rb = _row_block(r, c, limit=ADAMW_BLOCK)

    def body(w_ref, g_ref, m_ref, v_ref, d_ref, nm_ref, nv_ref):
        g_ = g_ref[...]
        nm = B1 * m_ref[...] + (1.0 - B1) * g_
        nv = B2 * v_ref[...] + (1.0 - B2) * (g_ * g_)
        nm_ref[...] = nm
        nv_ref[...] = nv
        m_hat = nm / (1.0 - B1 ** STEP)
        v_hat = nv / (1.0 - B2 ** STEP)
        d_ref[...] = -LR * (m_hat / (jnp.sqrt(v_hat) + ADAM_EPS) + WD * w_ref[...])

    blk = pl.BlockSpec((rb, c), lambda i: (i, 0))
    shp = jax.ShapeDtypeStruct((r, c), F32)
    return _grid_call(body, name=name, nsteps=r // rb, in_specs=[blk] * 4, out_specs=[blk] * 3, out_shape=[shp] * 3,
                      scratch_shapes=[], args=(w, g, m, v), rider=rider)


def _adamw_math(w, g, m, v):
    nm = B1 * m + (1.0 - B1) * g
    nv = B2 * v + (1.0 - B2) * (g * g)
    m_hat = nm / (1.0 - B1 ** STEP)
    v_hat = nv / (1.0 - B2 ** STEP)
    return -LR * (m_hat / (jnp.sqrt(v_hat) + ADAM_EPS) + WD * w), nm, nv


def adamw_layers(w, parts, m, v, where, name):
    _, r, c = w.shape
    assert len(parts) == DEPTH == 2 and parts[0][0].shape == (N_DEV, r, c)
    rb = _row_block(r, c, limit=ADAMW_BLOCK // 2)
    nb = r // rb

    def body(where_ref, w_ref, a0, b0, c0, a1, b1, c1, m_ref, v_ref, go_ref, d_ref, nm_ref, nv_ref):
        def total(mine, pair, far):
            return (mine[0] + pair[0]) + ((far[0].astype(F32) + far[1].astype(F32)) + far[2].astype(F32))

        g = jnp.where(pl.program_id(0) == 0, total(a0, b0, c0), total(a1, b1, c1))
        go_ref[0], d_ref[0], nm_ref[0], nv_ref[0] = (g,) + _adamw_math(w_ref[0], g, m_ref[0], v_ref[0])

    blk = pl.BlockSpec((1, rb, c), lambda l, i, wr: (l, i, 0))

    def layer_specs(layer):
        row = (lambda l, i: jnp.where(l == 0, i, nb - 1)) if layer == 0 else (lambda l, i: jnp.where(l == 0, 0, i))
        return [pl.BlockSpec((1, rb, c), lambda l, i, wr: (wr[0], row(l, i), 0)),
                pl.BlockSpec((1, rb, c), lambda l, i, wr: (wr[1], row(l, i), 0)),
                pl.BlockSpec((3, rb, c), lambda l, i, wr: (0, row(l, i), 0))]

    shp = jax.ShapeDtypeStruct(w.shape, F32)
    return pl.pallas_call(
        body, name=name,
        grid_spec=pltpu.PrefetchScalarGridSpec(
            num_scalar_prefetch=1, grid=(DEPTH, nb),
            in_specs=[blk] + layer_specs(0) + layer_specs(1) + [blk, blk], out_specs=[blk] * 4),
        out_shape=[shp] * 4, compiler_params=_cp(2),
    )(where, w, *parts[0], *parts[1], m, v)


def _adamw_nd(w, g, m, v, name, rider=None):
    shape = w.shape
    flat = lambda a: a.reshape(-1, shape[-1])
    outs, got = adamw(flat(w), flat(g), flat(m), flat(v), name, rider=rider)
    return tuple(o.reshape(shape) for o in outs), got


LANES = 128


PACK_UNIT = 8 * LANES


ADAMW_SMALL_ROWS = 512


def _pack(arrays, row_multiple=8):
    pieces, n = [], 0
    for a in arrays:
        pieces.append(a.reshape(-1).astype(F32))
        pad = (-a.size) % PACK_UNIT
        if pad:
            pieces.append(jnp.zeros((pad,), F32))
        n += a.size + pad
    tail = (-n) % (row_multiple * LANES)
    if tail:
        pieces.append(jnp.zeros((tail,), F32))
    return jnp.concatenate(pieces).reshape(-1, LANES)


def _unpack(packed, shapes):
    out, r0 = [], 0
    lead = packed.shape[:-2]
    for shp in shapes:
        size = 1
        for s in shp:
            size *= s
        nr = 8 * -(-size // PACK_UNIT)
        blk = packed[..., r0:r0 + nr, :].reshape(lead + (nr * LANES,))[..., :size]
        out.append(blk.reshape(lead + tuple(shp)))
        r0 += nr
    return out


WEIGHTS = ["c_ctx", "w_ada", "b_ada", "ffn1_norm_g", "ffn1_w_in", "ffn1_w_out", "mix_norm_g", "w_in_mix", "lru_conv_w",
           "lru_conv_b", "lru_w_r", "lru_b_r", "lru_w_i", "lru_b_i", "lru_lambda", "sgu_norm_g", "sgu_w", "sgu_b",
           "w_out_mix", "ffn2_norm_g", "ffn2_w_in", "ffn2_w_out", "final_norm_g"]
BIG = ["w_ada", "ffn1_w_in", "ffn1_w_out", "w_in_mix", "w_out_mix", "ffn2_w_in", "ffn2_w_out"]
SHARDED_SMALL = ["lru_conv_w", "lru_b_r", "lru_b_i", "lru_lambda"]
LAYER_SMALL = ["ffn1_norm_g", "mix_norm_g", "ffn2_norm_g", "lru_conv_w", "lru_conv_b", "lru_w_r", "lru_b_r", "lru_w_i",
               "lru_b_i", "lru_lambda", "sgu_norm_g", "sgu_w", "sgu_b"]
LRU_SHARD = LRU_W // N_DEV


def _widen(a):
    return jnp.moveaxis(a, 0, -2).reshape(a.shape[1:-1] + (LRU_W,))


def kernel(x, c, ctx, c_ctx, w_ada, b_ada, ffn1_norm_g, ffn1_w_in, ffn1_w_out, mix_norm_g, w_in_mix, lru_conv_w, lru_conv_b, lru_w_r, lru_b_r, lru_w_i, lru_b_i, lru_lambda, sgu_norm_g, sgu_w, sgu_b, w_out_mix, ffn2_norm_g, ffn2_w_in, ffn2_w_out, final_norm_g, loss_target, m_c_ctx, m_w_ada, m_b_ada, m_ffn1_norm_g, m_ffn1_w_in, m_ffn1_w_out, m_mix_norm_g, m_w_in_mix, m_lru_conv_w, m_lru_conv_b, m_lru_w_r, m_lru_b_r, m_lru_w_i, m_lru_b_i, m_lru_lambda, m_sgu_norm_g, m_sgu_w, m_sgu_b, m_w_out_mix, m_ffn2_norm_g, m_ffn2_w_in, m_ffn2_w_out, m_final_norm_g, v_c_ctx, v_w_ada, v_b_ada, v_ffn1_norm_g, v_ffn1_w_in, v_ffn1_w_out, v_mix_norm_g, v_w_in_mix, v_lru_conv_w, v_lru_conv_b, v_lru_w_r, v_lru_b_r, v_lru_w_i, v_lru_b_i, v_lru_lambda, v_sgu_norm_g, v_sgu_w, v_sgu_b, v_w_out_mix, v_ffn2_norm_g, v_ffn2_w_in, v_ffn2_w_out, v_final_norm_g):
    given = dict(locals())
    W = {n: given[n] for n in WEIGHTS}
    M = {n: given["m_" + n] for n in WEIGHTS}
    V = {n: given["v_" + n] for n in WEIGHTS}
    xi, yi, ci = _position()
    me = 4 * xi + 2 * yi + ci
    chip = 2 * xi + yi

    shards = []
    tr = lambda a: jnp.swapaxes(a, 1, 2)
    for l in range(DEPTH):
        sh = dict(win1=tr(ffn1_w_in)[l], wout1=ffn1_w_out[l], wmix=w_in_mix[l], womix=w_out_mix[l], win2=tr(ffn2_w_in)[l],
                  wout2=ffn2_w_out[l])
        shards.append({k: a.astype(BF16) for k, a in sh.items()})

    sharded_shapes = [W[n].shape for n in SHARDED_SMALL]
    both = Riders([GatherRider([_pack([c[0]] + [W[n] for n in SHARDED_SMALL])]),
                   GatherRider([shards[0][k] for _, k in FIRST_WEIGHTS])])
    (got,), first_weights = both.split(run_alone(both, pl.ANY, "gather_first"))
    parts = _unpack(got, [(D,)] + sharded_shapes)
    c_all = parts[0]
    wide = {n: _widen(a) for n, a in zip(SHARDED_SMALL, parts[1:])}
    cond = jnp.concatenate([c_all, c_ctx[None, :], jnp.zeros((ADA_ROWS - N_DEV - 1, D), F32)], axis=0)
    b_slab = lax.dynamic_slice_in_dim(b_ada, me * ADA_SHARD, ADA_SHARD, axis=1)[:, None, :]
    slabs = ada_fwd(cond, w_ada, b_slab, "ada_fwd")
    mall = run_alone(GatherRider([slabs.reshape(DEPTH * ADA_ROWS, ADA_SHARD)]), pltpu.VMEM, "gather_mod")[0]
    mall = mall.reshape(N_DEV, DEPTH, ADA_ROWS, ADA_SHARD)
    m_sample = lax.dynamic_index_in_dim(mall, me, axis=2, keepdims=False)
    m_ctx = mall[:, :, N_DEV, :]
    mods = jnp.stack([jnp.transpose(m, (1, 0, 2)).reshape(DEPTH, N_MOD, D) for m in (m_ctx, m_sample)], axis=1)

    smalls = []
    for l in range(DEPTH):
        smalls.append(small_layer(ffn1_norm_g[l], mix_norm_g[l], ffn2_norm_g[l], wide["lru_conv_w"][l], lru_conv_b[l],
                                  lru_w_r[l], wide["lru_b_r"][l], lru_w_i[l], wide["lru_b_i"][l], wide["lru_lambda"][l],
                                  sgu_norm_g[l], sgu_w[l], sgu_b[l]))

    c_idx = ci.reshape(1).astype(jnp.int32)
    where = jnp.stack([me, chip]).astype(jnp.int32)
    loss_blk, dx, dmods, grads, (small0_all, small1_all), gsum, dgf = fwd_bwd(
        ctx[0], x[0], loss_target[0], mods, shards, first_weights, smalls, final_norm_g[None, :], c_idx, where)
    smalls_shape = {n: (W[n].shape[1:-1] + (LRU_W,)) if n in SHARDED_SMALL else W[n].shape[1:] for n in LAYER_SMALL}
    G, delta, new_m, new_v = {}, {}, {}, {}
    for key, n in (("win1", "ffn1_w_in"), ("wout1", "ffn1_w_out"), ("wmix", "w_in_mix"), ("womix", "w_out_mix"),
                   ("win2", "ffn2_w_in"), ("wout2", "ffn2_w_out")):
        t_in = tr if key in ("win1", "win2") else (lambda a: a)
        outs = adamw_layers(t_in(W[n]), [gsum[l][key] for l in range(DEPTH)], t_in(M[n]), t_in(V[n]), where,
                            f"adamw_{n}")
        G[n], delta[n], new_m[n], new_v[n] = [t_in(o) for o in outs]

    n_rows = DEPTH * 2 * N_MOD
    dm_rows = jnp.concatenate([dmods.reshape(n_rows, D), jnp.zeros((-n_rows % 8, D), F32)], axis=0)
    dm_all = run_alone(GatherRider([dm_rows]), pltpu.VMEM, "gather_dmod")[0][:, :n_rows]
    dm_all = dm_all.reshape(N_DEV, DEPTH, 2, N_MOD * D)
    mine = lax.dynamic_slice_in_dim(dm_all, me * ADA_SHARD, ADA_SHARD, axis=3)
    G["w_ada"], dsc = ada_bwd(cond, jnp.transpose(mine[:, :, 1, :], (1, 0, 2)), jnp.transpose(mine[:, :, 0, :], (1, 0, 2)),
                              w_ada, "ada_bwd")
    G["b_ada"] = sum_dmods(dm_all, "sum_dmods")
    (delta["w_ada"], new_m["w_ada"], new_v["w_ada"]), _ = _adamw_nd(w_ada, G["w_ada"], m_w_ada, v_w_ada, "adamw_w_ada")

    head_all = run_alone(GatherRider([_pack([dsc[0], dgf[0], grads[0]["g1"][0]])]), pltpu.VMEM, "gather_head_grads")[0]
    head = _unpack(sum_over_devices(head_all, "sum_head_grads", silu_rows=D // LANES, w=c_ctx.reshape(D // LANES, LANES)),
                   [(D,), (D,), (D,)])
    shapes = [smalls_shape[n] for n in LAYER_SMALL]
    sum0 = [head[2]] + _unpack(sum_over_devices(small0_all, "sum_small_grads_0"), shapes[1:])
    sum1 = _unpack(sum_over_devices(small1_all, "sum_small_grads_1"), shapes)
    G["c_ctx"], G["final_norm_g"] = head[0], head[1]
    for n, a0, a1 in zip(LAYER_SMALL, sum0, sum1):
        a = jnp.stack([a0, a1])
        G[n] = lax.dynamic_slice_in_dim(a, me * LRU_SHARD, LRU_SHARD, axis=a.ndim - 1) if n in SHARDED_SMALL else a

    rest = [n for n in WEIGHTS if n not in BIG]
    shapes = [W[n].shape for n in rest]
    outs, _ = adamw(*[_pack([src[n] for n in rest], row_multiple=ADAMW_SMALL_ROWS) for src in (W, G, M, V)], "adamw_small")
    for dst, packed in zip((delta, new_m, new_v), outs):
        for n, a in zip(rest, _unpack(packed, shapes)):
            dst[n] = a

    loss = lax.psum(loss_blk[0, 0], ("x", "y", "c"))
    grad_x = dx[None]
    return (loss, grad_x, *[G[n] for n in WEIGHTS], *[delta[n] for n in WEIGHTS], *[new_m[n] for n in WEIGHTS],
            *[new_v[n] for n in WEIGHTS])
```

```python
import functools

import jax
import jax.numpy as jnp
from jax import lax
from jax.experimental import pallas as pl
from jax.experimental.pallas import tpu as pltpu

F32 = jnp.float32
BF16 = jnp.bfloat16

D = 1024
CTX = 256
DEPTH = 2
EPS = 1e-6
D_FF = 2816
LRU_W = 512
HEADS = 8
HEAD_DIM = 64
CONV_W = 4
RG_C = 8.0
GROUPS = 4
GROUP_DIM = 128
CHUNK = 128
MLP_W = 512
IN_PROJ = 2048
N_MOD = 9
N_DEV = 8

LR = 0.001
B1 = 0.9
B2 = 0.999
ADAM_EPS = 1e-08
WD = 0.01
STEP = 10

FF_IN_SHARD = 2 * D_FF // N_DEV
FF_OUT_SHARD = D_FF // N_DEV
HT = 256
WT = 512
N_MIX_SHARD = IN_PROJ // N_DEV
OMIX_SHARD = D // N_DEV
ADA_SHARD = N_MOD * D // N_DEV

TM = 256
SUB = 3
TMX = SUB * TM
HALO = 8
VMEM_LIMIT = 60 * 1024 * 1024

MESH = pl.DeviceIdType.MESH
ANY = pl.BlockSpec(memory_space=pl.ANY)


def _cp(n_axes=1):
    return pltpu.CompilerParams(dimension_semantics=("arbitrary",) * n_axes, vmem_limit_bytes=VMEM_LIMIT)


def _position():
    return lax.axis_index("x"), lax.axis_index("y"), lax.axis_index("c")


class GatherRider:
    def __init__(self, shards):
        n = len(shards)
        self.n = n
        self.ins = list(shards)
        self.out_shape = [jax.ShapeDtypeStruct((N_DEV,) + s.shape, s.dtype) for s in shards]
        self.sems = [pltpu.SemaphoreType.DMA((n, 7)), pltpu.SemaphoreType.DMA((n, 7)), pltpu.SemaphoreType.DMA((n,))]

    def _ctx(self, outs, sems):
        x, y, c = _position()
        chips = [(1 - x, y), (x, 1 - y), (1 - x, 1 - y)]

        def copy(t, k, block, to, src=None):
            dst = outs[t].at[4 * block[0] + 2 * block[1] + block[2]]
            return pltpu.make_async_remote_copy(
                src_ref=dst if src is None else src, dst_ref=dst, send_sem=sems[0].at[t, k],
                recv_sem=sems[1].at[t, k], device_id=to, device_id_type=MESH)

        return (x, y, c), (x, y, 1 - c), chips, copy

    def _local(self, ins, outs, sems, t):
        x, y, c = _position()
        return pltpu.make_async_copy(ins[t], outs[t].at[4 * x + 2 * y + c], sems[2].at[t])

    def _first(self, ins, outs, sems, t):
        me, sibling, chips, copy = self._ctx(outs, sems)
        return [copy(t, 0, me, sibling, src=ins[t])] + [copy(t, 1 + j, me, (*chip, me[2]), src=ins[t])
                                                         for j, chip in enumerate(chips)]

    def start(self, ins, outs, sems):
        for t in range(self.n):
            self._local(ins, outs, sems, t).start()
            for cp in self._first(ins, outs, sems, t):
                cp.start()

    def mid(self, ins, outs, sems):
        me, sibling, chips, copy = self._ctx(outs, sems)
        for j, chip in enumerate(chips):
            for t in range(self.n):
                copy(t, 1 + j, (*chip, me[2]), me).wait_recv()
                copy(t, 4 + j, (*chip, me[2]), sibling).start()

    def finish(self, ins, outs, sems):
        me, sibling, chips, copy = self._ctx(outs, sems)
        for t in range(self.n):
            copy(t, 0, sibling, me).wait_recv()
            for j, chip in enumerate(chips):
                copy(t, 4 + j, (*chip, 1 - me[2]), me).wait_recv()
        for t in range(self.n):
            for cp in self._first(ins, outs, sems, t):
                cp.wait_send()
            for j, chip in enumerate(chips):
                copy(t, 4 + j, (*chip, me[2]), sibling).wait_send()
            self._local(ins, outs, sems, t).wait()


class ExchangeRider:
    def __init__(self, tensors, plan, n_slots):
        n = len(tensors)
        self.n, self.plan = n, plan
        self.ins = list(tensors)
        self.out_shape = [jax.ShapeDtypeStruct((n_slots,) + s.shape[1:], s.dtype) for s in tensors]
        self.sems = [pltpu.SemaphoreType.DMA((n, n_slots)), pltpu.SemaphoreType.DMA((n, n_slots))]

    def _copies(self, ins, outs, sems):
        return [pltpu.make_async_remote_copy(
            src_ref=ins[t].at[block], dst_ref=outs[t].at[k], send_sem=sems[0].at[t, k], recv_sem=sems[1].at[t, k],
            device_id=to, device_id_type=MESH)
            for t in range(self.n) for k, (block, to) in enumerate(self.plan(*_position()))]

    def start(self, ins, outs, sems):
        for cp in self._copies(ins, outs, sems):
            cp.start()

    def mid(self, ins, outs, sems):
        pass

    def finish(self, ins, outs, sems):
        for cp in self._copies(ins, outs, sems):
            cp.wait()


class Riders:
    def __init__(self, riders):
        self.riders = list(riders)
        self.ins = [a for r in self.riders for a in r.ins]
        self.out_shape = [s for r in self.riders for s in r.out_shape]
        self.sems = [s for r in self.riders for s in r.sems]

    def _each(self, ins, outs, sems):
        i = o = s = 0
        for r in self.riders:
            ni, no, ns = len(r.ins), len(r.out_shape), len(r.sems)
            yield r, ins[i:i + ni], outs[o:o + no], sems[s:s + ns]
            i, o, s = i + ni, o + no, s + ns

    def start(self, ins, outs, sems):
        for r, a, b, c in self._each(ins, outs, sems):
            r.start(a, b, c)

    def mid(self, ins, outs, sems):
        for r, a, b, c in self._each(ins, outs, sems):
            r.mid(a, b, c)

    def finish(self, ins, outs, sems):
        for r, a, b, c in self._each(ins, outs, sems):
            r.finish(a, b, c)

    def split(self, outs):
        res, o = [], 0
        for r in self.riders:
            res.append(list(outs[o:o + len(r.out_shape)]))
            o += len(r.out_shape)
        return res


def pair_rider(grads):
    def plan(x, y, c):
        return [(4 * cx + 2 * cy + (1 - c), (x, y, 1 - c)) for cx in range(2) for cy in range(2)]
    return ExchangeRider(grads, plan, 4)


def chips_rider(parts):
    def plan(x, y, c):
        return [(2 * cx + cy, (cx, cy, c)) for cx, cy in [(1 - x, y), (x, 1 - y), (1 - x, 1 - y)]]
    return ExchangeRider(parts, plan, 3)


def run_alone(rider, space, name):
    ni = len(rider.ins)
    no = len(rider.out_shape)

    def body(*refs):
        ins, outs, sems = refs[:ni], refs[ni:ni + no], refs[ni + no:]
        rider.start(ins, outs, sems)
        rider.mid(ins, outs, sems)
        rider.finish(ins, outs, sems)

    spec = pl.BlockSpec(memory_space=space)
    return pl.pallas_call(
        body, name=name, in_specs=[spec] * ni, out_specs=[spec] * no, out_shape=rider.out_shape,
        scratch_shapes=rider.sems, compiler_params=pltpu.CompilerParams(vmem_limit_bytes=VMEM_LIMIT),
    )(*rider.ins)


def _grid_call(body, *, name, nsteps, in_specs, out_specs, out_shape, scratch_shapes, args, rider=None, aliases=None):
    aliases = aliases or {}
    if rider is None:
        outs = pl.pallas_call(body, name=name, grid=(nsteps,), in_specs=in_specs, out_specs=out_specs,
                              out_shape=out_shape, scratch_shapes=scratch_shapes, input_output_aliases=aliases,
                              compiler_params=_cp())(*args)
        return outs, []
    ni, no, ns = len(in_specs), len(out_specs), len(scratch_shapes)
    ri, ro = len(rider.ins), len(rider.out_shape)

    def wrapped(*refs):
        ins, refs = refs[:ni], refs[ni:]
        r_ins, refs = refs[:ri], refs[ri:]
        outs, refs = refs[:no], refs[no:]
        r_outs, refs = refs[:ro], refs[ro:]
        scratch, r_sems = refs[:ns], refs[ns:]
        s = pl.program_id(0)

        @pl.when(s == 0)
        def _():
            rider.start(r_ins, r_outs, r_sems)

        body(*ins, *outs, *scratch)

        @pl.when(s == (3 * nsteps) // 4)
        def _():
            rider.mid(r_ins, r_outs, r_sems)

        @pl.when(s == nsteps - 1)
        def _():
            rider.finish(r_ins, r_outs, r_sems)

    outs = pl.pallas_call(
        wrapped, name=name, grid=(nsteps,), in_specs=list(in_specs) + [ANY] * ri, out_specs=list(out_specs) + [ANY] * ro,
        out_shape=list(out_shape) + rider.out_shape, scratch_shapes=list(scratch_shapes) + rider.sems,
        input_output_aliases=aliases, compiler_params=_cp())(*args, *rider.ins)
    return outs[:no], outs[no:]


def _dot(a, b):
    return jnp.dot(a, b, preferred_element_type=F32)


def _dot_nt(a, b):
    return lax.dot_general(a, b, (((1,), (1,)), ((), ())), preferred_element_type=F32)


def _dot_tn(a, b):
    return lax.dot_general(a, b, (((0,), (0,)), ((), ())), preferred_element_type=F32)


def _sigmoid(x):
    return 1.0 / (1.0 + jnp.exp(-x))


def _kind(i):
    return jnp.where(i < CTX // TM, 0, 1)


def _sel(kind, mod_ref, k):
    return mod_ref[kind, k:k + 1, :]


def _acc2(ref, k, val, kind):
    ref[kind, k:k + 1, :] += jnp.sum(val, axis=0, keepdims=True)


def _norm_mod(h, g, shift, scale):
    r = lax.rsqrt(jnp.mean(h * h, axis=-1, keepdims=True) + EPS)
    n = h * r
    return (n * g) * (1.0 + scale) + shift, n, r


def _norm_mod_bwd(dz, n, r, g, scale):
    dn = dz * (g * (1.0 + scale))
    return r * (dn - n * jnp.mean(dn * n, axis=-1, keepdims=True))


def ffn_fwd(h, mod, g, win, wout, name, rider=None, loss=None):
    split = isinstance(h, tuple)
    nc = CTX // TM
    t = h[0].shape[0] + h[1].shape[0] if split else h.shape[0]

    def body(*refs):
        refs = list(refs)
        win_v, wout_v, a_v = refs[-3:]
        rows = refs[:2] if split else refs[:1]
        mod_ref, g_ref, win_hbm, wout_hbm = refs[len(rows):len(rows) + 4]
        rest = refs[len(rows) + 4:-3]
        if loss is not None:
            fg_ref, tgt_ref, rest = rest[0], rest[1], rest[2:]
        out_ref, gu_ref, acc_ref, rest = rest[0], rest[1], rest[2], rest[3:]
        i = pl.program_id(0)

        @pl.when(i == 0)
        def _():
            pltpu.sync_copy(win_hbm, win_v)
            pltpu.sync_copy(wout_hbm, wout_v)

        if split:
            hh = jnp.where(i < nc, rows[0][...], rows[1][...])
            rest[0][...] = hh
        else:
            hh = rows[0][...]
        ic = _kind(i)
        z, _, _ = _norm_mod(hh, g_ref[...], _sel(ic, mod_ref, 0), _sel(ic, mod_ref, 1))
        zb = z.astype(BF16)
        for j in range(D_FF // HT):
            gb, ub = slice(j * HT, (j + 1) * HT), slice(D_FF + j * HT, D_FF + (j + 1) * HT)
            gg = _dot_nt(zb, win_v[gb, :])
            uu = _dot_nt(zb, win_v[ub, :])
            gu_ref[:, gb] = gg.astype(BF16)
            gu_ref[:, ub] = uu.astype(BF16)
            a_v[:, gb] = ((gg * _sigmoid(gg)) * uu).astype(BF16)
        acc = _dot(a_v[...], wout_v[...])
        acc_ref[...] = acc
        hn = hh + (0.5 * _sel(ic, mod_ref, 2)) * acc
        if loss is None:
            out_ref[...] = hn
        else:
            loss_ref, dgf_ref = rest

            @pl.when(i == 0)
            def _():
                loss_ref[...] = jnp.zeros_like(loss_ref)
                dgf_ref[...] = jnp.zeros_like(dgf_ref)

            @pl.when(i < nc)
            def _():
                out_ref[...] = jnp.zeros_like(out_ref)

            @pl.when(i >= nc)
            def _():
                gain = fg_ref[...]
                r = lax.rsqrt(jnp.mean(hn * hn, axis=-1, keepdims=True) + EPS)
                n = hn * r
                err = n * gain - tgt_ref[...]
                loss_ref[...] += 0.5 * jnp.sum(jnp.mean(err * err, axis=-1, keepdims=True))
                dy = err * (1.0 / D)
                dgf_ref[...] += jnp.sum(dy * n, axis=0, keepdims=True)
                dn = dy * gain
                out_ref[...] = r * (dn - n * jnp.mean(dn * n, axis=-1, keepdims=True))

    row = pl.BlockSpec((TM, D), lambda i: (i, 0))
    vec = pl.BlockSpec((1, D), lambda i: (0, 0))
    rshape = jax.ShapeDtypeStruct((t, D), F32)
    if split:
        rows_in = [pl.BlockSpec((TM, D), lambda i: (jnp.minimum(i, nc - 1), 0)),
                   pl.BlockSpec((TM, D), lambda i: (jnp.maximum(i - nc, 0), 0))]
    else:
        rows_in = [row]
    in_specs = rows_in + [pl.BlockSpec((2, 3, D), lambda i: (0, 0, 0)), vec, ANY, ANY]
    out_specs = [row, pl.BlockSpec((TM, 2 * D_FF), lambda i: (i, 0)), row] + ([row] if split else [])
    out_shape = [rshape, jax.ShapeDtypeStruct((t, 2 * D_FF), BF16), rshape] + ([rshape] if split else [])
    args = (*(h if split else (h,)), mod, g, win, wout)
    if loss is not None:
        in_specs += [vec, pl.BlockSpec((TM, D), lambda i: (jnp.maximum(i - nc, 0), 0))]
        out_specs += [pl.BlockSpec((8, 128), lambda i: (0, 0)), vec]
        out_shape += [jax.ShapeDtypeStruct((8, 128), F32), jax.ShapeDtypeStruct((1, D), F32)]
        args += tuple(loss)
    return _grid_call(
        body, name=name, nsteps=t // TM, in_specs=in_specs, out_specs=out_specs, out_shape=out_shape,
        scratch_shapes=[pltpu.VMEM((2 * D_FF, D), BF16), pltpu.VMEM((D_FF, D), BF16), pltpu.VMEM((TM, D_FF), BF16)],
        args=args, rider=rider)


def ffn_bwd_a(dy, acc, gu, mod, wout, name, rider=None):
    t = dy.shape[0]
    nt = t // TM

    def body(dy_ref, acc_ref, gu_ref, mod_ref, wout_hbm, dp_ref, dwout_hbm, dgate_ref, wout_v, dwout_v):
        i = pl.program_id(0)

        @pl.when(i == 0)
        def _():
            pltpu.sync_copy(wout_hbm, wout_v)
            dwout_v[...] = jnp.zeros_like(dwout_v)
            dgate_ref[...] = jnp.zeros_like(dgate_ref)

        dy_ = dy_ref[...]
        ic = _kind(i)
        _acc2(dgate_ref, 0, 0.5 * dy_ * acc_ref[...], ic)
        daccb = ((0.5 * _sel(ic, mod_ref, 2)) * dy_).astype(BF16)
        for j in range(D_FF // HT):
            blk, ublk = slice(j * HT, (j + 1) * HT), slice(D_FF + j * HT, D_FF + (j + 1) * HT)
            da = _dot_nt(daccb, wout_v[blk, :])
            gg = gu_ref[:, blk].astype(F32)
            uu = gu_ref[:, ublk].astype(F32)
            s = _sigmoid(gg)
            sl = gg * s
            dwout_v[blk, :] += _dot_tn((sl * uu).astype(BF16), daccb)
            dp_ref[:, blk] = (da * uu * (s + sl * (1.0 - s))).astype(BF16)
            dp_ref[:, ublk] = (da * sl).astype(BF16)

        @pl.when(i == nt - 1)
        def _():
            pltpu.sync_copy(dwout_v, dwout_hbm)

    row = pl.BlockSpec((TM, D), lambda i: (i, 0))
    wide = pl.BlockSpec((TM, 2 * D_FF), lambda i: (i, 0))
    return _grid_call(
        body, name=name, nsteps=nt,
        in_specs=[row, row, wide, pl.BlockSpec((2, 3, D), lambda i: (0, 0, 0)), ANY],
        out_specs=[wide, ANY, pl.BlockSpec((2, 1, D), lambda i: (0, 0, 0))],
        out_shape=[jax.ShapeDtypeStruct((t, 2 * D_FF), BF16), jax.ShapeDtypeStruct((D_FF, D), F32),
                   jax.ShapeDtypeStruct((2, 1, D), F32)],
        scratch_shapes=[pltpu.VMEM((D_FF, D), BF16), pltpu.VMEM((D_FF, D), F32)],
        args=(dy, acc, gu, mod, wout), rider=rider)


def ffn_bwd_b(dy, h, dp, mod, g, win, name, rider=None, latent_only=False):
    t = dy.shape[0]
    nt = t // TM
    nc = CTX // TM

    def body(dy_ref, h_ref, dp_ref, mod_ref, g_ref, win_hbm, dh_ref, dwin_hbm, dmod_ref, dg_ref, win_v, dwin_v):
        i = pl.program_id(0)

        @pl.when(i == 0)
        def _():
            pltpu.sync_copy(win_hbm, win_v)
            dwin_v[...] = jnp.zeros_like(dwin_v)
            dmod_ref[...] = jnp.zeros_like(dmod_ref)
            dg_ref[...] = jnp.zeros_like(dg_ref)

        ic = _kind(i)
        gain = g_ref[...]
        scale = _sel(ic, mod_ref, 1)
        z, n, r = _norm_mod(h_ref[...], gain, _sel(ic, mod_ref, 0), scale)
        zb = z.astype(BF16)
        dz = _dot(dp_ref[...], win_v[...])
        for j in range(2 * D_FF // WT):
            blk = slice(j * WT, (j + 1) * WT)
            dwin_v[blk, :] += _dot_tn(dp_ref[:, blk], zb)
        _acc2(dmod_ref, 0, dz, ic)
        _acc2(dmod_ref, 1, dz * (n * gain), ic)
        dg_ref[...] += jnp.sum(dz * (1.0 + scale) * n, axis=0, keepdims=True)
        dh_ref[...] = dy_ref[...] + _norm_mod_bwd(dz, n, r, gain, scale)

        @pl.when(i == nt - 1)
        def _():
            pltpu.sync_copy(dwin_v, dwin_hbm)

    row = pl.BlockSpec((TM, D), lambda i: (i, 0))
    if latent_only:
        dh_spec = pl.BlockSpec((TM, D), lambda i: (jnp.maximum(i - nc, 0), 0))
        dh_shape = jax.ShapeDtypeStruct((t - CTX, D), F32)
    else:
        dh_spec, dh_shape = row, jax.ShapeDtypeStruct((t, D), F32)
    return _grid_call(
        body, name=name, nsteps=nt,
        in_specs=[row, row, pl.BlockSpec((TM, 2 * D_FF), lambda i: (i, 0)),
                  pl.BlockSpec((2, 3, D), lambda i: (0, 0, 0)), pl.BlockSpec((1, D), lambda i: (0, 0)), ANY],
        out_specs=[dh_spec, ANY, pl.BlockSpec((2, 2, D), lambda i: (0, 0, 0)), pl.BlockSpec((1, D), lambda i: (0, 0))],
        out_shape=[dh_shape, jax.ShapeDtypeStruct((2 * D_FF, D), F32),
                   jax.ShapeDtypeStruct((2, 2, D), F32), jax.ShapeDtypeStruct((1, D), F32)],
        scratch_shapes=[pltpu.VMEM((2 * D_FF, D), BF16), pltpu.VMEM((2 * D_FF, D), F32)],
        args=(dy, h, dp, mod, g, win), rider=rider)


def ffn_bwd_dw(h, dp, mod, g, name, rider=None):
    t = h.shape[0]
    nt = t // TM

    def body(h_ref, dp_ref, mod_ref, g_ref, dwin_hbm, dwin_v):
        i = pl.program_id(0)

        @pl.when(i == 0)
        def _():
            dwin_v[...] = jnp.zeros_like(dwin_v)

        zs = []
        for k in range(SUB):
            ic = _kind(SUB * i + k)
            z, _, _ = _norm_mod(h_ref[pl.ds(k * TM, TM), :], g_ref[...], _sel(ic, mod_ref, 0), _sel(ic, mod_ref, 1))
            zs.append(z.astype(BF16))
        zb = jnp.concatenate(zs, axis=0)
        for j in range(2 * D_FF // WT):
            blk = slice(j * WT, (j + 1) * WT)
            dwin_v[blk, :] += _dot_tn(dp_ref[:, blk], zb)

        @pl.when(i == nt // SUB - 1)
        def _():
            pltpu.sync_copy(dwin_v, dwin_hbm)

    return _grid_call(
        body, name=name, nsteps=nt // SUB,
        in_specs=[pl.BlockSpec((TMX, D), lambda i: (i, 0)), pl.BlockSpec((TMX, 2 * D_FF), lambda i: (i, 0)),
                  pl.BlockSpec((2, 3, D), lambda i: (0, 0, 0)), pl.BlockSpec((1, D), lambda i: (0, 0))],
        out_specs=[ANY], out_shape=[jax.ShapeDtypeStruct((2 * D_FF, D), F32)],
        scratch_shapes=[pltpu.VMEM((2 * D_FF, D), F32)],
        args=(h, dp, mod, g), rider=rider)


def ffn_bwd_dh(dy, h, dp, mod, g, win, name, tiles, carry=None, rider=None):
    t = dy.shape[0]
    nc = CTX // TM
    t0, t1 = tiles

    def body(*refs):
        if carry is None:
            dy_ref, h_ref, dp_ref, mod_ref, g_ref, win_hbm, dh_ref, dmod_ref, dg_ref, win_v = refs
        else:
            dy_ref, h_ref, dp_ref, mod_ref, g_ref, win_hbm, _, dmod0_ref, dg0_ref, dh_ref, dmod_ref, dg_ref, win_v = refs
        i = pl.program_id(0)

        @pl.when(i == 0)
        def _():
            pltpu.sync_copy(win_hbm, win_v)
            dmod_ref[...] = jnp.zeros_like(dmod_ref) if carry is None else dmod0_ref[...]
            dg_ref[...] = jnp.zeros_like(dg_ref) if carry is None else dg0_ref[...]

        ic = _kind(i + t0)
        gain = g_ref[...]
        scale = _sel(ic, mod_ref, 1)
        _, n, r = _norm_mod(h_ref[...], gain, _sel(ic, mod_ref, 0), scale)
        dz = _dot(dp_ref[...], win_v[...])
        _acc2(dmod_ref, 0, dz, ic)
        _acc2(dmod_ref, 1, dz * (n * gain), ic)
        dg_ref[...] += jnp.sum(dz * (1.0 + scale) * n, axis=0, keepdims=True)
        dh_ref[...] = dy_ref[...] + _norm_mod_bwd(dz, n, r, gain, scale)

    row = pl.BlockSpec((TM, D), lambda i: (i + t0, 0))
    small = [pl.BlockSpec((2, 2, D), lambda i: (0, 0, 0)), pl.BlockSpec((1, D), lambda i: (0, 0))]
    in_specs = [row, row, pl.BlockSpec((TM, 2 * D_FF), lambda i: (i + t0, 0)),
                pl.BlockSpec((2, 3, D), lambda i: (0, 0, 0)), pl.BlockSpec((1, D), lambda i: (0, 0)), ANY]
    args = (dy, h, dp, mod, g, win)
    if carry is not None:
        in_specs += [ANY] + small
        args += tuple(carry)
    return _grid_call(
        body, name=name, nsteps=t1 - t0, in_specs=in_specs,
        out_specs=[pl.BlockSpec((TM, D), lambda i: (jnp.maximum(i + t0 - nc, 0), 0))] + small,
        out_shape=[jax.ShapeDtypeStruct((t - CTX, D), F32), jax.ShapeDtypeStruct((2, 2, D), F32),
                   jax.ShapeDtypeStruct((1, D), F32)],
        scratch_shapes=[pltpu.VMEM((2 * D_FF, D), BF16)],
        args=args, rider=rider, aliases=None if carry is None else {6: 0})


def mix_in_fwd(h, mod, g, wmix, name):
    t = h.shape[0]

    def body(h_ref, mod_ref, g_ref, w_hbm, p_ref, w_v):
        i = pl.program_id(0)

        @pl.when(i == 0)
        def _():
            pltpu.sync_copy(w_hbm, w_v)

        zs = []
        for k in range(SUB):
            ic = _kind(SUB * i + k)
            z, _, _ = _norm_mod(h_ref[pl.ds(k * TM, TM), :], g_ref[...], _sel(ic, mod_ref, 0), _sel(ic, mod_ref, 1))
            zs.append(z.astype(BF16))
        zb = jnp.concatenate(zs, axis=0)
        for dd in range(N_DEV):
            p_ref[:, dd * N_MIX_SHARD:(dd + 1) * N_MIX_SHARD] = _dot(zb, w_v[dd])

    return pl.pallas_call(
        body, name=name, grid=(t // TMX,),
        in_specs=[pl.BlockSpec((TMX, D), lambda i: (i, 0)), pl.BlockSpec((2, 3, D), lambda i: (0, 0, 0)),
                  pl.BlockSpec((1, D), lambda i: (0, 0)), ANY],
        out_specs=pl.BlockSpec((TMX, IN_PROJ), lambda i: (i, 0)),
        out_shape=jax.ShapeDtypeStruct((t, IN_PROJ), F32),
        scratch_shapes=[pltpu.VMEM((N_DEV, D, N_MIX_SHARD), BF16)],
        compiler_params=_cp(),
    )(h, mod, g, wmix)


def _halo_specs(nt, tile_of, rows=TM):
    nb = nt * (rows // HALO)
    main = pl.BlockSpec((rows, LRU_W), lambda s: (tile_of(s), 0))
    prev = pl.BlockSpec((HALO, LRU_W), lambda s: (jnp.maximum(tile_of(s) * (rows // HALO) - 1, 0), 0))
    nxt = pl.BlockSpec((HALO, LRU_W), lambda s: (jnp.minimum((tile_of(s) + 1) * (rows // HALO), nb - 1), 0))
    return main, prev, nxt


def _ext(tile, nt, main, prev, nxt):
    has_prev = jnp.logical_and(tile != 0, tile != 1)
    has_next = jnp.logical_and(tile != 0, tile != nt - 1)
    return jnp.concatenate([jnp.where(has_prev, prev, 0.0), main, jnp.where(has_next, nxt, 0.0)], axis=0)


def _shifted(ext, off):
    n = ext.shape[0]
    return pltpu.roll(ext, (-off) % n, 0)[HALO:HALO + TM]


def _conv(ext, cw_ref, cb_ref):
    xc = cb_ref[...] + cw_ref[0:1, :] * _shifted(ext, -2)
    for k in range(1, CONV_W):
        xc = xc + cw_ref[k:k + 1, :] * _shifted(ext, k - 2)
    return xc


def _log1p(y):
    return jnp.where(y < 1e-2, y * (1.0 - y * (0.5 - y * (1.0 / 3.0 - 0.25 * y))), jnp.log(1.0 + y))


def _softplus_neg(lam):
    return jnp.maximum(-lam, 0.0) + _log1p(jnp.exp(-jnp.abs(lam)))


def _one_minus_exp(x, exp_half):
    p = x * (1.0 + x * (1 / 2 + x * (1 / 6 + x * (1 / 24))))
    return jnp.where(x > -0.1, -p, 1.0 - exp_half * exp_half)


def _gates(xc, wr, br, wi, bi, lam):
    xb = xc.astype(BF16)
    r = _sigmoid(_dot(xb, wr) + br)
    ig = _sigmoid(_dot(xb, wi) + bi)
    sp = _softplus_neg(lam)
    log_a = -RG_C * r * sp
    a = jnp.exp(log_a)
    mult = jnp.sqrt(_one_minus_exp(2.0 * log_a, a))
    return r, ig, sp, a, mult


def _scan(a, b, reverse):
    n = a.shape[0]
    row = lax.broadcasted_iota(jnp.int32, a.shape, 0)
    s = 1
    while s < n:
        if s < HALO:
            if reverse:
                keep = row < n - s
                a_s = jnp.where(keep, pltpu.roll(a, n - s, 0), 1.0)
                b_s = jnp.where(keep, pltpu.roll(b, n - s, 0), 0.0)
            else:
                keep = row >= s
                a_s = jnp.where(keep, pltpu.roll(a, s, 0), 1.0)
                b_s = jnp.where(keep, pltpu.roll(b, s, 0), 0.0)
            b = a * b_s + b
            a = a * a_s
        elif reverse:
            b = jnp.concatenate([a[:n - s] * b[s:] + b[:n - s], b[n - s:]], axis=0)
            a = jnp.concatenate([a[:n - s] * a[s:], a[n - s:]], axis=0)
        else:
            b = jnp.concatenate([b[:s], a[s:] * b[:n - s] + b[s:]], axis=0)
            a = jnp.concatenate([a[:s], a[s:] * a[:n - s]], axis=0)
        s *= 2
    return a, b


def lru_fwd(p, conv_w, conv_b, wr, br, wi, bi, lam, reverse, name, rider=None):
    t = p.shape[0]
    nt = t // TM

    def tile_of(s):
        return jnp.where(s == 0, 0, nt - s) if reverse else s

    def body(x_ref, xp_ref, xn_ref, cw_ref, cb_ref, wr_ref, br_ref, wi_ref, bi_ref, lam_ref, h_ref, carry):
        s = pl.program_id(0)
        tile = tile_of(s)

        @pl.when(s == 0)
        def _():
            carry[...] = jnp.zeros_like(carry)

        ext = _ext(tile, nt, x_ref[...], xp_ref[...], xn_ref[...])
        xc = _conv(ext, cw_ref, cb_ref)
        _, ig, _, a, mult = _gates(xc, wr_ref[...], br_ref[...], wi_ref[...], bi_ref[...], lam_ref[...])
        a_cum, hl = _scan(a, mult * (ig * xc), reverse)
        hh = hl + a_cum * carry[...]
        h_ref[...] = hh
        carry[...] = hh[0:1, :] if reverse else hh[TM - 1:TM, :]

    main, prev, nxt = _halo_specs(nt, tile_of)
    vec = pl.BlockSpec((1, LRU_W), lambda s: (0, 0))
    mat = pl.BlockSpec((LRU_W, LRU_W), lambda s: (0, 0))
    outs, got = _grid_call(
        body, name=name, nsteps=nt,
        in_specs=[main, prev, nxt, pl.BlockSpec((CONV_W, LRU_W), lambda s: (0, 0)), vec, mat, vec, mat, vec, vec],
        out_specs=[main],
        out_shape=[jax.ShapeDtypeStruct((t, LRU_W), F32)],
        scratch_shapes=[pltpu.VMEM((1, LRU_W), F32)],
        args=(p, p, p, conv_w, conv_b, wr, br, wi, bi, lam), rider=rider)
    return outs[0], got


def lru_bwd(p, hs, dhs, conv_w, conv_b, wr, br, wi, bi, lam, reverse, name):
    t = p.shape[0]
    nt = t // TM
    bpt = TM // HALO

    def tile_of(s):
        return jnp.where(s == nt - 1, 0, s + 1) if reverse else nt - 1 - s

    def hprev_block(s):
        tile = tile_of(s)
        if reverse:
            return (jnp.where(tile == nt - 1, 0, jnp.minimum((tile + 1) * bpt, nt * bpt - 1)), 0)
        return (jnp.maximum(tile * bpt - 1, 0), 0)

    def body(x_ref, xp_ref, xn_ref, h_ref, hp_ref, dh_ref, cw_ref, cb_ref, wr_ref, br_ref, wi_ref, bi_ref, lam_ref,
             dxc_ref, dwr_ref, dwi_ref, dbr_ref, dbi_ref, dlam_ref, carry):
        s = pl.program_id(0)
        tile = tile_of(s)

        @pl.when(s == 0)
        def _():
            carry[...] = jnp.zeros_like(carry)
            for ref in (dwr_ref, dwi_ref, dbr_ref, dbi_ref, dlam_ref):
                ref[...] = jnp.zeros_like(ref)

        ext = _ext(tile, nt, x_ref[...], xp_ref[...], xn_ref[...])
        xc = _conv(ext, cw_ref, cb_ref)
        wr_, wi_ = wr_ref[...], wi_ref[...]
        r, ig, sp, a, mult = _gates(xc, wr_, br_ref[...], wi_, bi_ref[...], lam_ref[...])
        gated = ig * xc
        row = lax.broadcasted_iota(jnp.int32, (TM, LRU_W), 0)
        hh = h_ref[...]
        start = jnp.where(tile != 0, hp_ref[0:1, :] if reverse else hp_ref[HALO - 1:HALO, :], 0.0)
        if reverse:
            edge = row == TM - 1
            hprev = jnp.where(edge, start, pltpu.roll(hh, TM - 1, 0))
            coef = jnp.where(row == 0, 0.0, pltpu.roll(a, 1, 0))
            bb = dh_ref[...] + jnp.where(row == 0, carry[...], 0.0)
        else:
            edge = row == 0
            hprev = jnp.where(edge, start, pltpu.roll(hh, 1, 0))
            coef = jnp.where(row == TM - 1, 0.0, pltpu.roll(a, TM - 1, 0))
            bb = dh_ref[...] + jnp.where(row == TM - 1, carry[...], 0.0)
        _, lmb = _scan(coef, bb, not reverse)
        al = a * lmb
        carry[...] = al[TM - 1:TM, :] if reverse else al[0:1, :]

        dgated = lmb * mult
        dloga = (lmb * hprev) * a - (lmb * gated) * (a * a) / mult
        dpre_r = (dloga * (-RG_C * sp)) * r * (1.0 - r)
        dpre_i = (dgated * xc) * ig * (1.0 - ig)
        drb, dib = dpre_r.astype(BF16), dpre_i.astype(BF16)
        xb = xc.astype(BF16)
        dxc_ref[...] = dgated * ig + _dot_nt(drb, wr_) + _dot_nt(dib, wi_)
        dwr_ref[...] += _dot_tn(xb, drb)
        dwi_ref[...] += _dot_tn(xb, dib)
        dbr_ref[...] += jnp.sum(dpre_r, axis=0, keepdims=True)
        dbi_ref[...] += jnp.sum(dpre_i, axis=0, keepdims=True)
        dlam_ref[...] += jnp.sum(dloga * (-RG_C * r), axis=0, keepdims=True)

        @pl.when(s == nt - 1)
        def _():
            dlam_ref[...] = dlam_ref[...] * (-_sigmoid(-lam_ref[...]))

    main, prev, nxt = _halo_specs(nt, tile_of)
    vec = pl.BlockSpec((1, LRU_W), lambda s: (0, 0))
    mat = pl.BlockSpec((LRU_W, LRU_W), lambda s: (0, 0))
    vshape = jax.ShapeDtypeStruct((1, LRU_W), F32)
    mshape = jax.ShapeDtypeStruct((LRU_W, LRU_W), F32)
    return pl.pallas_call(
        body, name=name, grid=(nt,),
        in_specs=[main, prev, nxt, main, pl.BlockSpec((HALO, LRU_W), hprev_block), main,
                  pl.BlockSpec((CONV_W, LRU_W), lambda s: (0, 0)), vec, mat, vec, mat, vec, vec],
        out_specs=[main, mat, mat, vec, vec, vec],
        out_shape=[jax.ShapeDtypeStruct((t, LRU_W), F32), mshape, mshape, vshape, vshape, vshape],
        scratch_shapes=[pltpu.VMEM((1, LRU_W), F32)],
        compiler_params=_cp(),
    )(p, p, p, hs, hs, dhs, conv_w, conv_b, wr, br, wi, bi, lam)


GELU_C = 0.7978845608028654
GELU_A = 0.044715


def _gelu(x):
    th = jnp.tanh(GELU_C * (x + GELU_A * x * x * x))
    return 0.5 * x * (1.0 + th), th


def _sgu(v, gain, w_ref, bt_ref):
    mu = jnp.mean(v, axis=-1, keepdims=True)
    xc = v - mu
    rs = lax.rsqrt(jnp.mean(xc * xc, axis=-1, keepdims=True) + EPS)
    vhat = xc * rs
    vnb = (vhat * gain).astype(BF16)
    chunks = []
    for ch in range(TM // CHUNK):
        zs = []
        for gi in range(GROUPS):
            vb = vnb[ch * CHUNK:(ch + 1) * CHUNK, gi * GROUP_DIM:(gi + 1) * GROUP_DIM]
            zs.append(_dot(w_ref[gi].astype(BF16), vb) + bt_ref[:, gi:gi + 1])
        chunks.append(jnp.concatenate(zs, axis=1))
    return jnp.concatenate(chunks, axis=0), vhat, rs, vnb


def _pcols(k, rows=TM):
    return pl.BlockSpec((rows, LRU_W), lambda i: (i, k))


def mix_out_fwd(h, p, hf, hb, mod, sgu_g, sgu_w, sgu_bt, womix, name):
    t = h.shape[0]

    def body(h_ref, gl_ref, u_ref, v_ref, hf_ref, hb_ref, mod_ref, sg_ref, sw_ref, sb_ref, w_hbm, out_ref, o_ref, w_v):
        i = pl.program_id(0)

        @pl.when(i == 0)
        def _():
            pltpu.sync_copy(w_hbm, w_v)

        ys = []
        for k in range(SUB):
            rows = pl.ds(k * TM, TM)
            ge, _ = _gelu(gl_ref[rows, :])
            y_lru = (hf_ref[rows, :] + hb_ref[rows, :]) * ge
            z, _, _, _ = _sgu(v_ref[rows, :], sg_ref[...], sw_ref, sb_ref)
            ys.append(jnp.concatenate([y_lru, u_ref[rows, :] * z], axis=1).astype(BF16))
        o = _dot(jnp.concatenate(ys, axis=0), w_v[...])
        o_ref[...] = o
        for k in range(SUB):
            rows = pl.ds(k * TM, TM)
            out_ref[rows, :] = h_ref[rows, :] + _sel(_kind(SUB * i + k), mod_ref, 2) * o[k * TM:(k + 1) * TM]

    row = pl.BlockSpec((TMX, D), lambda i: (i, 0))
    half = pl.BlockSpec((TMX, LRU_W), lambda i: (i, 0))
    return pl.pallas_call(
        body, name=name, grid=(t // TMX,),
        in_specs=[row, _pcols(1, TMX), _pcols(2, TMX), _pcols(3, TMX), half, half, pl.BlockSpec((2, 3, D), lambda i: (0, 0, 0)),
                  pl.BlockSpec((1, MLP_W), lambda i: (0, 0)), pl.BlockSpec((GROUPS, CHUNK, CHUNK), lambda i: (0, 0, 0)),
                  pl.BlockSpec((CHUNK, GROUPS), lambda i: (0, 0)), ANY],
        out_specs=[row, row],
        out_shape=[jax.ShapeDtypeStruct((t, D), F32), jax.ShapeDtypeStruct((t, D), F32)],
        scratch_shapes=[pltpu.VMEM((D, D), BF16)],
        compiler_params=_cp(),
    )(h, p, p, p, hf, hb, mod, sgu_g, sgu_w, sgu_bt, womix)


def mix_out_bwd(dy, p, hf, hb, o, mod, sgu_g, sgu_w, sgu_bt, womix, name, rider=None):
    t = dy.shape[0]

    def body(dy_ref, gl_ref, u_ref, v_ref, hf_ref, hb_ref, o_ref, mod_ref, sg_ref, sw_ref, sb_ref, w_hbm,
             dhs_ref, dp_ref, dw_ref, dgate_ref, dsg_ref, dsw_ref, dsb_ref, w_v):
        i = pl.program_id(0)

        @pl.when(i == 0)
        def _():
            pltpu.sync_copy(w_hbm, w_v)
            for ref in (dw_ref, dgate_ref, dsg_ref, dsw_ref, dsb_ref):
                ref[...] = jnp.zeros_like(ref)

        ys, dobs = [], []
        for k in range(SUB):
            rows = pl.ds(k * TM, TM)
            ic = _kind(SUB * i + k)
            dy_ = dy_ref[rows, :]
            _acc2(dgate_ref, 0, dy_ * o_ref[rows, :], ic)
            dob = (_sel(ic, mod_ref, 2) * dy_).astype(BF16)

            gl = gl_ref[rows, :]
            ge, th = _gelu(gl)
            hsum = hf_ref[rows, :] + hb_ref[rows, :]
            gain = sg_ref[...]
            uu = u_ref[rows, :]
            z, vhat, rs, vnb = _sgu(v_ref[rows, :], gain, sw_ref, sb_ref)
            ys.append(jnp.concatenate([hsum * ge, uu * z], axis=1).astype(BF16))
            dobs.append(dob)
            dyy = _dot_nt(dob, w_v[...])
            dyl, dys = dyy[:, :LRU_W], dyy[:, LRU_W:]

            dhs_ref[rows, :] = dyl * ge
            dge = 0.5 * (1.0 + th) + 0.5 * gl * (1.0 - th * th) * (GELU_C * (1.0 + 3.0 * GELU_A * gl * gl))
            dp_ref[rows, 0:LRU_W] = (dyl * hsum * dge).astype(BF16)
            dp_ref[rows, LRU_W:2 * LRU_W] = (dys * z).astype(BF16)

            dz = dys * uu
            dzb = dz.astype(BF16)
            dvn_chunks, dsb_cols = [], [jnp.zeros((CHUNK, 1), F32)] * GROUPS
            for ch in range(TM // CHUNK):
                cols = []
                for gi in range(GROUPS):
                    rs_, cs_ = slice(ch * CHUNK, (ch + 1) * CHUNK), slice(gi * GROUP_DIM, (gi + 1) * GROUP_DIM)
                    dzg = dzb[rs_, cs_]
                    dsb_cols[gi] = dsb_cols[gi] + jnp.sum(dz[rs_, cs_], axis=1, keepdims=True)
                    dsw_ref[gi] += _dot_nt(dzg, vnb[rs_, cs_])
                    cols.append(_dot_tn(sw_ref[gi].astype(BF16), dzg))
                dvn_chunks.append(jnp.concatenate(cols, axis=1))
            dsb_ref[...] += jnp.concatenate(dsb_cols, axis=1)
            dvn = jnp.concatenate(dvn_chunks, axis=0)
            dsg_ref[...] += jnp.sum(dvn * vhat, axis=0, keepdims=True)
            dvh = dvn * gain
            dv = rs * (dvh - jnp.mean(dvh, axis=-1, keepdims=True) - vhat * jnp.mean(dvh * vhat, axis=-1, keepdims=True))
            dp_ref[rows, 2 * LRU_W:3 * LRU_W] = dv.astype(BF16)
        dw_ref[...] += _dot_tn(jnp.concatenate(ys, axis=0), jnp.concatenate(dobs, axis=0))

    row = pl.BlockSpec((TMX, D), lambda i: (i, 0))
    half = pl.BlockSpec((TMX, LRU_W), lambda i: (i, 0))
    const2 = lambda i: (0, 0)
    const3 = lambda i: (0, 0, 0)
    return _grid_call(
        body, name=name, nsteps=t // TMX,
        in_specs=[row, _pcols(1, TMX), _pcols(2, TMX), _pcols(3, TMX), half, half, row, pl.BlockSpec((2, 3, D), const3),
                  pl.BlockSpec((1, MLP_W), const2), pl.BlockSpec((GROUPS, CHUNK, CHUNK), const3),
                  pl.BlockSpec((CHUNK, GROUPS), const2), ANY],
        out_specs=[half, pl.BlockSpec((TMX, 3 * LRU_W), lambda i: (i, 0)), pl.BlockSpec((D, D), const2),
                   pl.BlockSpec((2, 1, D), const3), pl.BlockSpec((1, MLP_W), const2),
                   pl.BlockSpec((GROUPS, CHUNK, CHUNK), const3), pl.BlockSpec((CHUNK, GROUPS), const2)],
        out_shape=[jax.ShapeDtypeStruct((t, LRU_W), F32), jax.ShapeDtypeStruct((t, 3 * LRU_W), BF16),
                   jax.ShapeDtypeStruct((D, D), F32), jax.ShapeDtypeStruct((2, 1, D), F32),
                   jax.ShapeDtypeStruct((1, MLP_W), F32), jax.ShapeDtypeStruct((GROUPS, CHUNK, CHUNK), F32),
                   jax.ShapeDtypeStruct((CHUNK, GROUPS), F32)],
        scratch_shapes=[pltpu.VMEM((D, D), BF16)],
        args=(dy, p, p, p, hf, hb, o, mod, sgu_g, sgu_w, sgu_bt, womix), rider=rider)


def mix_in_bwd(dy, h, p, dxf, dxb, dprest, mod, g, conv_w, wmix, name, rider=None):
    t = dy.shape[0]
    nt = t // TM

    def body(dy_ref, h_ref, x_ref, xp_ref, xn_ref, f_ref, fp_ref, fn_ref, b_ref, bp_ref, bn_ref, dpr_ref, mod_ref,
             g_ref, cw_ref, w_hbm, dh_ref, dw_hbm, dmod_ref, dg_ref, dcw_ref, dcb_ref, w_v, dw_v):
        i = pl.program_id(0)

        @pl.when(i == 0)
        def _():
            pltpu.sync_copy(w_hbm, w_v)
            dw_v[...] = jnp.zeros_like(dw_v)
            for ref in (dmod_ref, dg_ref, dcw_ref, dcb_ref):
                ref[...] = jnp.zeros_like(ref)

        def around(ref, before, after, k):
            prev = ref[pl.ds(k * TM - HALO, HALO), :] if k > 0 else before[...]
            nxt = ref[pl.ds((k + 1) * TM, HALO), :] if k < SUB - 1 else after[...]
            return ref[pl.ds(k * TM, TM), :], prev, nxt

        gain = g_ref[...]
        zs, dps, rs = [], [], []
        for k in range(SUB):
            f_m, f_p, f_n = around(f_ref, fp_ref, fn_ref, k)
            b_m, b_p, b_n = around(b_ref, bp_ref, bn_ref, k)
            dmain = f_m + b_m
            dext = _ext(SUB * i + k, nt, dmain, f_p + b_p, f_n + b_n)
            xext = _ext(SUB * i + k, nt, *around(x_ref, xp_ref, xn_ref, k))
            dxl = cw_ref[0:1, :] * _shifted(dext, 2)
            for tap in range(1, CONV_W):
                dxl = dxl + cw_ref[tap:tap + 1, :] * _shifted(dext, 2 - tap)
            dcw_ref[...] += jnp.concatenate(
                [jnp.sum(dmain * _shifted(xext, tap - 2), axis=0, keepdims=True) for tap in range(CONV_W)], axis=0)
            dcb_ref[...] += jnp.sum(dmain, axis=0, keepdims=True)

            ic = _kind(SUB * i + k)
            rows = pl.ds(k * TM, TM)
            z, _, r = _norm_mod(h_ref[rows, :], gain, _sel(ic, mod_ref, 0), _sel(ic, mod_ref, 1))
            zs.append(z.astype(BF16))
            rs.append(r)
            dps.append(jnp.concatenate([dxl.astype(BF16), dpr_ref[rows, :]], axis=1))
        zb = jnp.concatenate(zs, axis=0)
        dpb = jnp.concatenate(dps, axis=0)
        dz_all = jnp.zeros((TMX, D), F32)
        for dd in range(N_DEV):
            dpd = dpb[:, dd * N_MIX_SHARD:(dd + 1) * N_MIX_SHARD]
            dz_all = dz_all + _dot_nt(dpd, w_v[dd])
            dw_v[dd] += _dot_tn(zb, dpd)
        for k in range(SUB):
            ic = _kind(SUB * i + k)
            rows = pl.ds(k * TM, TM)
            scale = _sel(ic, mod_ref, 1)
            dz = dz_all[k * TM:(k + 1) * TM]
            n = h_ref[rows, :] * rs[k]
            _acc2(dmod_ref, 0, dz, ic)
            _acc2(dmod_ref, 1, dz * (n * gain), ic)
            dg_ref[...] += jnp.sum(dz * (1.0 + scale) * n, axis=0, keepdims=True)
            dh_ref[rows, :] = dy_ref[rows, :] + _norm_mod_bwd(dz, n, rs[k], gain, scale)

        @pl.when(i == nt // SUB - 1)
        def _():
            pltpu.sync_copy(dw_v, dw_hbm)

    main, prev, nxt = _halo_specs(nt // SUB, lambda s: s, TMX)
    row = pl.BlockSpec((TMX, D), lambda i: (i, 0))
    const2 = lambda i: (0, 0)
    return _grid_call(
        body, name=name, nsteps=nt // SUB,
        in_specs=[row, row, main, prev, nxt, main, prev, nxt, main, prev, nxt,
                  pl.BlockSpec((TMX, 3 * LRU_W), lambda i: (i, 0)), pl.BlockSpec((2, 3, D), lambda i: (0, 0, 0)),
                  pl.BlockSpec((1, D), const2), pl.BlockSpec((CONV_W, LRU_W), const2), ANY],
        out_specs=[row, ANY, pl.BlockSpec((2, 2, D), lambda i: (0, 0, 0)), pl.BlockSpec((1, D), const2),
                   pl.BlockSpec((CONV_W, LRU_W), const2), pl.BlockSpec((1, LRU_W), const2)],
        out_shape=[jax.ShapeDtypeStruct((t, D), F32), jax.ShapeDtypeStruct((N_DEV, D, N_MIX_SHARD), F32),
                   jax.ShapeDtypeStruct((2, 2, D), F32), jax.ShapeDtypeStruct((1, D), F32),
                   jax.ShapeDtypeStruct((CONV_W, LRU_W), F32), jax.ShapeDtypeStruct((1, LRU_W), F32)],
        scratch_shapes=[pltpu.VMEM((N_DEV, D, N_MIX_SHARD), BF16), pltpu.VMEM((N_DEV, D, N_MIX_SHARD), F32)],
        args=(dy, h, p, p, p, dxf, dxf, dxf, dxb, dxb, dxb, dprest, mod, g, conv_w, wmix), rider=rider)


def _block_diag(w):
    eye = jnp.eye(HEADS, dtype=w.dtype)
    return jnp.einsum("dhij,hk->dhikj", w, eye).reshape(2, LRU_W, LRU_W)


def _block_diag_inv(full):
    f = full.reshape(2, HEADS, HEAD_DIM, HEADS, HEAD_DIM)
    return jnp.stack([f[:, hd, :, hd, :] for hd in range(HEADS)], axis=1)


def small_layer(g1, gm, g2, conv_w, conv_b, w_r, b_r, w_i, b_i, lam, sgu_g, sgu_w, sgu_b):
    return dict(g1=g1[None, :], gm=gm[None, :], g2=g2[None, :], conv_w=conv_w, conv_b=conv_b[None, :],
                wr=_block_diag(w_r).astype(BF16), br=b_r[:, None, :], wi=_block_diag(w_i).astype(BF16),
                bi=b_i[:, None, :], lam=lam[:, None, :], sgu_g=sgu_g[None, :], sgu_w=sgu_w, sgu_bt=sgu_b.T)


def small_grads(g):
    out = dict(mix_norm_g=g["gm"][0], ffn2_norm_g=g["g2"][0], lru_conv_w=g["conv_w"],
               lru_conv_b=g["conv_b"][0], lru_w_r=_block_diag_inv(g["wr"]), lru_b_r=g["br"][:, 0, :],
               lru_w_i=_block_diag_inv(g["wi"]), lru_b_i=g["bi"][:, 0, :], lru_lambda=g["lam"][:, 0, :],
               sgu_norm_g=g["sgu_g"][0], sgu_w=g["sgu_w"], sgu_b=g["sgu_bt"].T)
    if "g1" in g:
        out["ffn1_norm_g"] = g["g1"][0]
    return out


BIG_KEYS = ("win1", "wout1", "wmix", "womix", "win2", "wout2")


def _as_blocks(key, g):
    return g if key == "wmix" else g.reshape(N_DEV, g.shape[0] // N_DEV, D)


def _gathered(key, a):
    return a if key == "wmix" else a.reshape(N_DEV * a.shape[1], D)


class _ReduceScatter:
    def __init__(self, c_idx, where):
        self.c_idx, self.where = c_idx, where
        self.out = {}

    def pair(self, group):
        return pair_rider([g for _, g in group])

    def after_pair(self, group, recv1, tag):
        return chips_rider(pair_sum([g for _, g in group], list(recv1), self.c_idx, f"pair_sum_{tag}"))

    def after_chips(self, group, recv1, recv2, tag):
        for (key, g), r1, r2 in zip(group, recv1, recv2):
            self.out[key] = (g, r1, r2)


FIRST_WEIGHTS = [(0, "win1"), (0, "wout1")]


def fwd_bwd(ctx_rows, x_rows, target, mods, shards, first_weights, smalls, final_g, c_idx, where):
    assert CTX == TM and len(shards) == 2

    def gather(keys_by_layer):
        return GatherRider([shards[l][k] for l, k in keys_by_layer])

    def put(full, keys_by_layer, got):
        for (l, k), a in zip(keys_by_layer, got):
            full[l][k] = _gathered(k, a)

    full = [dict(s) for s in smalls]
    put(full, FIRST_WEIGHTS, first_weights)
    riders = {
        "ffn1_fwd_0": [(0, "wmix"), (0, "womix"), (0, "win2")],
        "lru_fwd_0_0": [(0, "wout2")],
        "ffn2_fwd_0": [(1, "win1"), (1, "wout1")],
        "ffn1_fwd_1": [(1, "wmix"), (1, "womix"), (1, "win2")],
        "lru_fwd_1_0": [(1, "wout2")],
    }

    def ffn(which, l, h):
        name = f"ffn{which}_fwd_{l}"
        w = full[l]
        keys = riders.get(name)
        m = mods[l][:, 0:3] if which == 1 else mods[l][:, 6:9]
        last = (which, l) == (2, 1)
        outs, got = ffn_fwd(h, m, w[f"g{which}"], w[f"win{which}"], w[f"wout{which}"], name,
                            rider=gather(keys) if keys else None, loss=(final_g, target) if last else None)
        if keys:
            put(full, keys, got)
        return outs

    saved = []
    h = (ctx_rows, x_rows)
    for l in range(2):
        mm = mods[l][:, 3:6]
        outs = ffn(1, l, h)
        h1, gu1, acc1 = outs[:3]
        hin = outs[3] if l == 0 else h
        w = full[l]
        p = mix_in_fwd(h1, mm, w["gm"], w["wmix"], f"mix_in_fwd_{l}")
        hs = []
        for d in range(2):
            keys = riders.get(f"lru_fwd_{l}_{d}")
            hd, got = lru_fwd(p, w["conv_w"], w["conv_b"], w["wr"][d], w["br"][d], w["wi"][d], w["bi"][d], w["lam"][d],
                              bool(d), f"lru_fwd_{l}_{d}", rider=gather(keys) if keys else None)
            if keys:
                put(full, keys, got)
            hs.append(hd)
        h2, o = mix_out_fwd(h1, p, hs[0], hs[1], mm, w["sgu_g"], w["sgu_w"], w["sgu_bt"], w["womix"], f"mix_out_fwd_{l}")
        outs = ffn(2, l, h2)
        h, gu2, acc2 = outs[:3]
        saved.append((hin, h1, h2, gu1, acc1, p, hs, o, gu2, acc2))
    dh, (loss, dgf) = h, outs[3:]

    rs = _ReduceScatter(c_idx, where)
    grads, dmods, sums = [None, None], [None, None], [None, None]
    pending = None
    for l in (1, 0):
        w = full[l]
        m1, mm, m2 = mods[l][:, 0:3], mods[l][:, 3:6], mods[l][:, 6:9]
        hin, h1, h2, gu1, acc1, p, hs, o, gu2, acc2 = saved[l]
        g = {}
        rs.out = {}
        both = Riders([rs.pair(pending[0]), GatherRider([small_pack])]) if pending else None
        (dp2, g["wout2"], dgate2), got = ffn_bwd_a(dh, acc2, gu2, m2, w["wout2"], f"ffn2_bwd_a_{l}", rider=both)
        if pending:
            r1, (small_all,) = both.split(got)
        chips = rs.after_pair(pending[0], r1, pending[1]) if pending else None
        (dh, g["win2"], dmod2, g["g2"]), r2 = ffn_bwd_b(dh, h2, dp2, m2, w["g2"], w["win2"], f"ffn2_bwd_b_{l}", rider=chips)
        if pending:
            rs.after_chips(pending[0], r1, r2, pending[1])
            sums[l + 1].update(rs.out)
            rs.out = {}

        grp = [(k, _as_blocks(k, g[k])) for k in ("win2", "wout2")]
        (dhs, dprest, g["womix"], dgatem, g["sgu_g"], g["sgu_w"], g["sgu_bt"]), r1 = mix_out_bwd(
            dh, p, hs[0], hs[1], o, mm, w["sgu_g"], w["sgu_w"], w["sgu_bt"], w["womix"], f"mix_out_bwd_{l}",
            rider=rs.pair(grp))
        chips = rs.after_pair(grp, r1, f"a{l}")
        dx, per_dir = [], []
        for d in range(2):
            out = lru_bwd(p, hs[d], dhs, w["conv_w"], w["conv_b"], w["wr"][d], w["br"][d], w["wi"][d], w["bi"][d],
                          w["lam"][d], bool(d), f"lru_bwd_{l}_{d}")
            dx.append(out[0])
            per_dir.append(out[1:])
        for k, nm in enumerate(("wr", "wi", "br", "bi", "lam")):
            g[nm] = jnp.stack([per_dir[0][k], per_dir[1][k]])
        (dh, g["wmix"], dmodm, g["gm"], g["conv_w"], g["conv_b"]), r2 = mix_in_bwd(
            dh, h1, p, dx[0], dx[1], dprest, mm, w["gm"], w["conv_w"], w["wmix"], f"mix_in_bwd_{l}", rider=chips)
        rs.after_chips(grp, r1, r2, f"a{l}")
        sums[l] = dict(rs.out)
        rs.out = {}

        if l == 1:
            (dp1, g["wout1"], dgate1), _ = ffn_bwd_a(dh, acc1, gu1, m1, w["wout1"], f"ffn1_bwd_a_{l}")
            (dh, g["win1"], dmod1, g["g1"]), _ = ffn_bwd_b(dh, hin, dp1, m1, w["g1"], w["win1"], f"ffn1_bwd_b_{l}")
            pending = ([(k, _as_blocks(k, g[k])) for k in ("womix", "wmix", "wout1", "win1")], f"b{l}")
            per = small_grads(g)
            small_pack = _pack([per[n] for n in LAYER_SMALL])
        else:
            g_mix = [(k, _as_blocks(k, g[k])) for k in ("womix", "wmix")]
            (dp1, g["wout1"], dgate1), r1_mix = ffn_bwd_a(dh, acc1, gu1, m1, w["wout1"], f"ffn1_bwd_a_{l}",
                                                          rider=rs.pair(g_mix))
            g_out = [("wout1", _as_blocks("wout1", g["wout1"]))]
            per = small_grads(g)
            three = Riders([rs.after_pair(g_mix, r1_mix, f"b{l}"), rs.pair(g_out),
                            GatherRider([_pack([per[n] for n in LAYER_SMALL[1:]])])])
            (g["win1"],), got = ffn_bwd_dw(hin, dp1, m1, w["g1"], f"ffn1_bwd_dw_{l}", rider=three)
            r2_mix, r1_out, (small0_all,) = three.split(got)
            rs.after_chips(g_mix, r1_mix, r2_mix, f"b{l}")
            g_in = [("win1", _as_blocks("win1", g["win1"]))]
            both = Riders([rs.after_pair(g_out, r1_out, f"c{l}"), rs.pair(g_in)])
            nt = dh.shape[0] // TM
            cut0, cut1 = (3 * nt) // 8, nt - max(nt // 16, 1)
            part, got = ffn_bwd_dh(dh, hin, dp1, m1, w["g1"], w["win1"], f"ffn1_bwd_dh0_{l}", (0, cut0), rider=both)
            r2_out, r1_in = both.split(got)
            rs.after_chips(g_out, r1_out, r2_out, f"c{l}")
            part, r2_in = ffn_bwd_dh(dh, hin, dp1, m1, w["g1"], w["win1"], f"ffn1_bwd_dh1_{l}", (cut0, cut1), carry=part,
                                     rider=rs.after_pair(g_in, r1_in, f"d{l}"))
            rs.after_chips(g_in, r1_in, r2_in, f"d{l}")
            (dh, dmod1, g["g1"]), _ = ffn_bwd_dh(dh, hin, dp1, m1, w["g1"], w["win1"], f"ffn1_bwd_dh2_{l}", (cut1, nt),
                                                 carry=part)
            sums[l].update(rs.out)
        dmods[l] = jnp.concatenate([dmod1, dgate1, dmodm, dgatem, dmod2, dgate2], axis=1)
        grads[l] = g
    return loss, dh, jnp.stack(dmods), grads, (small0_all, small_all), sums, dgf


def _row_block(r, c, limit=262144):
    best = 8
    for rb in range(8, r + 1, 8):
        if r % rb == 0 and rb * c <= limit:
            best = rb
    return best


PAIR_SUM_SPLIT = 2


def pair_sum(grads, recv, c_idx, name):
    n = len(grads)

    def body(c_ref, *refs):
        for t in range(n):
            refs[2 * n + t][...] = (refs[t][...] + refs[n + t][...]).astype(BF16)

    mine, theirs, outs = [], [], []
    for g in grads:
        _, r, c = g.shape
        rb = r // PAIR_SUM_SPLIT
        assert rb % 16 == 0
        mine.append(pl.BlockSpec((1, rb, c), lambda j, i, c_ref: (2 * j + c_ref[0], i, 0)))
        theirs.append(pl.BlockSpec((1, rb, c), lambda j, i, c_ref: (j, i, 0)))
        outs.append(jax.ShapeDtypeStruct((4, r, c), BF16))
    return pl.pallas_call(
        body, name=name,
        grid_spec=pltpu.PrefetchScalarGridSpec(num_scalar_prefetch=1, grid=(4, PAIR_SUM_SPLIT), in_specs=mine + theirs,
                                               out_specs=list(theirs)),
        out_shape=outs, compiler_params=_cp(2),
    )(c_idx, *grads, *recv)


ADA_ROWS = 16


def _silu(v):
    return v * _sigmoid(v)


def ada_fwd(cond, w_ada, b_slab, name):
    def body(c_ref, w_ref, b_ref, o_ref):
        s = _silu(c_ref[...]).astype(BF16)
        o_ref[0] = _dot(s, w_ref[0].astype(BF16)) + b_ref[0]

    return pl.pallas_call(
        body, name=name, grid=(DEPTH,),
        in_specs=[pl.BlockSpec((ADA_ROWS, D), lambda l: (0, 0)), pl.BlockSpec((1, D, ADA_SHARD), lambda l: (l, 0, 0)),
                  pl.BlockSpec((1, 1, ADA_SHARD), lambda l: (l, 0, 0))],
        out_specs=pl.BlockSpec((1, ADA_ROWS, ADA_SHARD), lambda l: (l, 0, 0)),
        out_shape=jax.ShapeDtypeStruct((DEPTH, ADA_ROWS, ADA_SHARD), F32),
        compiler_params=_cp(),
    )(cond, w_ada, b_slab)


def ada_bwd(cond, dm_sample, dm_ctx, w_ada, name):
    def body(c_ref, ds_ref, dc_ref, w_ref, gw_ref, dsc_ref):
        @pl.when(pl.program_id(0) == 0)
        def _():
            dsc_ref[...] = jnp.zeros_like(dsc_ref)

        s = _silu(c_ref[...]).astype(BF16)
        dcs = dc_ref[0]
        tot = dcs[0:1]
        for j in range(1, N_DEV):
            tot = tot + dcs[j:j + 1]
        tot8 = jnp.where(lax.broadcasted_iota(jnp.int32, (N_DEV, ADA_SHARD), 0) == 0, tot, 0.0)
        dm = jnp.concatenate([ds_ref[0], tot8], axis=0).astype(BF16)
        gw_ref[0] = _dot_tn(s, dm)
        dsc_ref[...] += _dot_nt(dm, w_ref[0].astype(BF16))[N_DEV:N_DEV + 1]

    slab = pl.BlockSpec((1, N_DEV, ADA_SHARD), lambda l: (l, 0, 0))
    wspec = pl.BlockSpec((1, D, ADA_SHARD), lambda l: (l, 0, 0))
    return pl.pallas_call(
        body, name=name, grid=(DEPTH,),
        in_specs=[pl.BlockSpec((ADA_ROWS, D), lambda l: (0, 0)), slab, slab, wspec],
        out_specs=[wspec, pl.BlockSpec((1, D), lambda l: (0, 0))],
        out_shape=[jax.ShapeDtypeStruct((DEPTH, D, ADA_SHARD), F32), jax.ShapeDtypeStruct((1, D), F32)],
        compiler_params=_cp(),
    )(cond, dm_sample, dm_ctx, w_ada)


def sum_over_devices(parts, name, silu_rows=0, w=None):
    _, r, c = parts.shape

    def body(*refs):
        p_ref, o_ref = refs[0], refs[-1]
        tot = p_ref[0]
        for j in range(1, N_DEV):
            tot = tot + p_ref[j]
        o_ref[...] = tot
        if silu_rows:
            wv = refs[1][...]
            s = _sigmoid(wv)
            o_ref[0:silu_rows, :] = tot[0:silu_rows, :] * (s * (1.0 + wv * (1.0 - s)))

    vm = pl.BlockSpec(memory_space=pltpu.VMEM)
    args = (parts,) if w is None else (parts, w)
    return pl.pallas_call(
        body, name=name, in_specs=[vm] * len(args), out_specs=vm,
        out_shape=jax.ShapeDtypeStruct((r, c), F32),
        compiler_params=pltpu.CompilerParams(vmem_limit_bytes=VMEM_LIMIT),
    )(*args)


def sum_dmods(dm_all, name):
    def body(d_ref, o_ref):
        for l in range(DEPTH):
            tot = d_ref[0, l]
            for j in range(1, N_DEV):
                tot = tot + d_ref[j, l]
            o_ref[l:l + 1, :] = tot[0:1] + tot[1:2]

    vm = pl.BlockSpec(memory_space=pltpu.VMEM)
    return pl.pallas_call(
        body, name=name, in_specs=[vm], out_specs=vm,
        out_shape=jax.ShapeDtypeStruct((DEPTH, N_MOD * D), F32),
    )(dm_all)


ADAMW_BLOCK = 512 * 1024


def adamw(w, g, m, v, name, rider=None):
    r, c = w.shape
    rb = _row_block(r, c, limit=ADAMW_BLOCK)

    def body(w_ref, g_ref, m_ref, v_ref, d_ref, nm_ref, nv_ref):
        g_ = g_ref[...]
        nm = B1 * m_ref[...] + (1.0 - B1) * g_
        nv = B2 * v_ref[...] + (1.0 - B2) * (g_ * g_)
        nm_ref[...] = nm
        nv_ref[...] = nv
        m_hat = nm / (1.0 - B1 ** STEP)
        v_hat = nv / (1.0 - B2 ** STEP)
        d_ref[...] = -LR * (m_hat / (jnp.sqrt(v_hat) + ADAM_EPS) + WD * w_ref[...])

    blk = pl.BlockSpec((rb, c), lambda i: (i, 0))
    shp = jax.ShapeDtypeStruct((r, c), F32)
    return _grid_call(body, name=name, nsteps=r // rb, in_specs=[blk] * 4, out_specs=[blk] * 3, out_shape=[shp] * 3,
                      scratch_shapes=[], args=(w, g, m, v), rider=rider)


def _adamw_math(w, g, m, v):
    nm = B1 * m + (1.0 - B1) * g
    nv = B2 * v + (1.0 - B2) * (g * g)
    m_hat = nm / (1.0 - B1 ** STEP)
    v_hat = nv / (1.0 - B2 ** STEP)
    return -LR * (m_hat / (jnp.sqrt(v_hat) + ADAM_EPS) + WD * w), nm, nv


def adamw_layers(w, parts, m, v, where, name):
    _, r, c = w.shape
    assert len(parts) == DEPTH == 2 and parts[0][0].shape == (N_DEV, r, c)
    rb = _row_block(r, c, limit=ADAMW_BLOCK // 2)
    nb = r // rb

    def body(where_ref, w_ref, a0, b0, c0, a1, b1, c1, m_ref, v_ref, go_ref, d_ref, nm_ref, nv_ref):
        def total(mine, pair, far):
            return (mine[0] + pair[0]) + ((far[0].astype(F32) + far[1].astype(F32)) + far[2].astype(F32))

        g = jnp.where(pl.program_id(0) == 0, total(a0, b0, c0), total(a1, b1, c1))
        go_ref[0], d_ref[0], nm_ref[0], nv_ref[0] = (g,) + _adamw_math(w_ref[0], g, m_ref[0], v_ref[0])

    blk = pl.BlockSpec((1, rb, c), lambda l, i, wr: (l, i, 0))

    def layer_specs(layer):
        row = (lambda l, i: jnp.where(l == 0, i, nb - 1)) if layer == 0 else (lambda l, i: jnp.where(l == 0, 0, i))
        return [pl.BlockSpec((1, rb, c), lambda l, i, wr: (wr[0], row(l, i), 0)),
                pl.BlockSpec((1, rb, c), lambda l, i, wr: (wr[1], row(l, i), 0)),
                pl.BlockSpec((3, rb, c), lambda l, i, wr: (0, row(l, i), 0))]

    shp = jax.ShapeDtypeStruct(w.shape, F32)
    return pl.pallas_call(
        body, name=name,
        grid_spec=pltpu.PrefetchScalarGridSpec(
            num_scalar_prefetch=1, grid=(DEPTH, nb),
            in_specs=[blk] + layer_specs(0) + layer_specs(1) + [blk, blk], out_specs=[blk] * 4),
        out_shape=[shp] * 4, compiler_params=_cp(2),
    )(where, w, *parts[0], *parts[1], m, v)


def _adamw_nd(w, g, m, v, name, rider=None):
    shape = w.shape
    flat = lambda a: a.reshape(-1, shape[-1])
    outs, got = adamw(flat(w), flat(g), flat(m), flat(v), name, rider=rider)
    return tuple(o.reshape(shape) for o in outs), got


LANES = 128


PACK_UNIT = 8 * LANES


ADAMW_SMALL_ROWS = 512


def _pack(arrays, row_multiple=8):
    pieces, n = [], 0
    for a in arrays:
        pieces.append(a.reshape(-1).astype(F32))
        pad = (-a.size) % PACK_UNIT
        if pad:
            pieces.append(jnp.zeros((pad,), F32))
        n += a.size + pad
    tail = (-n) % (row_multiple * LANES)
    if tail:
        pieces.append(jnp.zeros((tail,), F32))
    return jnp.concatenate(pieces).reshape(-1, LANES)


def _unpack(packed, shapes):
    out, r0 = [], 0
    lead = packed.shape[:-2]
    for shp in shapes:
        size = 1
        for s in shp:
            size *= s
        nr = 8 * -(-size // PACK_UNIT)
        blk = packed[..., r0:r0 + nr, :].reshape(lead + (nr * LANES,))[..., :size]
        out.append(blk.reshape(lead + tuple(shp)))
        r0 += nr
    return out


WEIGHTS = ["c_ctx", "w_ada", "b_ada", "ffn1_norm_g", "ffn1_w_in", "ffn1_w_out", "mix_norm_g", "w_in_mix", "lru_conv_w",
           "lru_conv_b", "lru_w_r", "lru_b_r", "lru_w_i", "lru_b_i", "lru_lambda", "sgu_norm_g", "sgu_w", "sgu_b",
           "w_out_mix", "ffn2_norm_g", "ffn2_w_in", "ffn2_w_out", "final_norm_g"]
BIG = ["w_ada", "ffn1_w_in", "ffn1_w_out", "w_in_mix", "w_out_mix", "ffn2_w_in", "ffn2_w_out"]
SHARDED_SMALL = ["lru_conv_w", "lru_b_r", "lru_b_i", "lru_lambda"]
LAYER_SMALL = ["ffn1_norm_g", "mix_norm_g", "ffn2_norm_g", "lru_conv_w", "lru_conv_b", "lru_w_r", "lru_b_r", "lru_w_i",
               "lru_b_i", "lru_lambda", "sgu_norm_g", "sgu_w", "sgu_b"]
LRU_SHARD = LRU_W // N_DEV


def _widen(a):
    return jnp.moveaxis(a, 0, -2).reshape(a.shape[1:-1] + (LRU_W,))


def kernel(x, c, ctx, c_ctx, w_ada, b_ada, ffn1_norm_g, ffn1_w_in, ffn1_w_out, mix_norm_g, w_in_mix, lru_conv_w, lru_conv_b, lru_w_r, lru_b_r, lru_w_i, lru_b_i, lru_lambda, sgu_norm_g, sgu_w, sgu_b, w_out_mix, ffn2_norm_g, ffn2_w_in, ffn2_w_out, final_norm_g, loss_target, m_c_ctx, m_w_ada, m_b_ada, m_ffn1_norm_g, m_ffn1_w_in, m_ffn1_w_out, m_mix_norm_g, m_w_in_mix, m_lru_conv_w, m_lru_conv_b, m_lru_w_r, m_lru_b_r, m_lru_w_i, m_lru_b_i, m_lru_lambda, m_sgu_norm_g, m_sgu_w, m_sgu_b, m_w_out_mix, m_ffn2_norm_g, m_ffn2_w_in, m_ffn2_w_out, m_final_norm_g, v_c_ctx, v_w_ada, v_b_ada, v_ffn1_norm_g, v_ffn1_w_in, v_ffn1_w_out, v_mix_norm_g, v_w_in_mix, v_lru_conv_w, v_lru_conv_b, v_lru_w_r, v_lru_b_r, v_lru_w_i, v_lru_b_i, v_lru_lambda, v_sgu_norm_g, v_sgu_w, v_sgu_b, v_w_out_mix, v_ffn2_norm_g, v_ffn2_w_in, v_ffn2_w_out, v_final_norm_g):
    given = dict(locals())
    W = {n: given[n] for n in WEIGHTS}
    M = {n: given["m_" + n] for n in WEIGHTS}
    V = {n: given["v_" + n] for n in WEIGHTS}
    xi, yi, ci = _position()
    me = 4 * xi + 2 * yi + ci
    chip = 2 * xi + yi

    shards = []
    tr = lambda a: jnp.swapaxes(a, 1, 2)
    for l in range(DEPTH):
        sh = dict(win1=tr(ffn1_w_in)[l], wout1=ffn1_w_out[l], wmix=w_in_mix[l], womix=w_out_mix[l], win2=tr(ffn2_w_in)[l],
                  wout2=ffn2_w_out[l])
        shards.append({k: a.astype(BF16) for k, a in sh.items()})

    sharded_shapes = [W[n].shape for n in SHARDED_SMALL]
    both = Riders([GatherRider([_pack([c[0]] + [W[n] for n in SHARDED_SMALL])]),
                   GatherRider([shards[0][k] for _, k in FIRST_WEIGHTS])])
    (got,), first_weights = both.split(run_alone(both, pl.ANY, "gather_first"))
    parts = _unpack(got, [(D,)] + sharded_shapes)
    c_all = parts[0]
    wide = {n: _widen(a) for n, a in zip(SHARDED_SMALL, parts[1:])}
    cond = jnp.concatenate([c_all, c_ctx[None, :], jnp.zeros((ADA_ROWS - N_DEV - 1, D), F32)], axis=0)
    b_slab = lax.dynamic_slice_in_dim(b_ada, me * ADA_SHARD, ADA_SHARD, axis=1)[:, None, :]
    slabs = ada_fwd(cond, w_ada, b_slab, "ada_fwd")
    mall = run_alone(GatherRider([slabs.reshape(DEPTH * ADA_ROWS, ADA_SHARD)]), pltpu.VMEM, "gather_mod")[0]
    mall = mall.reshape(N_DEV, DEPTH, ADA_ROWS, ADA_SHARD)
    m_sample = lax.dynamic_index_in_dim(mall, me, axis=2, keepdims=False)
    m_ctx = mall[:, :, N_DEV, :]
    mods = jnp.stack([jnp.transpose(m, (1, 0, 2)).reshape(DEPTH, N_MOD, D) for m in (m_ctx, m_sample)], axis=1)

    smalls = []
    for l in range(DEPTH):
        smalls.append(small_layer(ffn1_norm_g[l], mix_norm_g[l], ffn2_norm_g[l], wide["lru_conv_w"][l], lru_conv_b[l],
                                  lru_w_r[l], wide["lru_b_r"][l], lru_w_i[l], wide["lru_b_i"][l], wide["lru_lambda"][l],
                                  sgu_norm_g[l], sgu_w[l], sgu_b[l]))

    c_idx = ci.reshape(1).astype(jnp.int32)
    where = jnp.stack([me, chip]).astype(jnp.int32)
    loss_blk, dx, dmods, grads, (small0_all, small1_all), gsum, dgf = fwd_bwd(
        ctx[0], x[0], loss_target[0], mods, shards, first_weights, smalls, final_norm_g[None, :], c_idx, where)
    smalls_shape = {n: (W[n].shape[1:-1] + (LRU_W,)) if n in SHARDED_SMALL else W[n].shape[1:] for n in LAYER_SMALL}
    G, delta, new_m, new_v = {}, {}, {}, {}
    for key, n in (("win1", "ffn1_w_in"), ("wout1", "ffn1_w_out"), ("wmix", "w_in_mix"), ("womix", "w_out_mix"),
                   ("win2", "ffn2_w_in"), ("wout2", "ffn2_w_out")):
        t_in = tr if key in ("win1", "win2") else (lambda a: a)
        outs = adamw_layers(t_in(W[n]), [gsum[l][key] for l in range(DEPTH)], t_in(M[n]), t_in(V[n]), where,
                            f"adamw_{n}")
        G[n], delta[n], new_m[n], new_v[n] = [t_in(o) for o in outs]

    n_rows = DEPTH * 2 * N_MOD
    dm_rows = jnp.concatenate([dmods.reshape(n_rows, D), jnp.zeros((-n_rows % 8, D), F32)], axis=0)
    dm_all = run_alone(GatherRider([dm_rows]), pltpu.VMEM, "gather_dmod")[0][:, :n_rows]
    dm_all = dm_all.reshape(N_DEV, DEPTH, 2, N_MOD * D)
    mine = lax.dynamic_slice_in_dim(dm_all, me * ADA_SHARD, ADA_SHARD, axis=3)
    G["w_ada"], dsc = ada_bwd(cond, jnp.transpose(mine[:, :, 1, :], (1, 0, 2)), jnp.transpose(mine[:, :, 0, :], (1, 0, 2)),
                              w_ada, "ada_bwd")
    G["b_ada"] = sum_dmods(dm_all, "sum_dmods")
    (delta["w_ada"], new_m["w_ada"], new_v["w_ada"]), _ = _adamw_nd(w_ada, G["w_ada"], m_w_ada, v_w_ada, "adamw_w_ada")

    head_all = run_alone(GatherRider([_pack([dsc[0], dgf[0], grads[0]["g1"][0], loss_blk])]), pltpu.VMEM,
                         "gather_head_grads")[0]
    head = _unpack(sum_over_devices(head_all, "sum_head_grads", silu_rows=D // LANES, w=c_ctx.reshape(D // LANES, LANES)),
                   [(D,), (D,), (D,), loss_blk.shape])
    shapes = [smalls_shape[n] for n in LAYER_SMALL]
    sum0 = [head[2]] + _unpack(sum_over_devices(small0_all, "sum_small_grads_0"), shapes[1:])
    sum1 = _unpack(sum_over_devices(small1_all, "sum_small_grads_1"), shapes)
    G["c_ctx"], G["final_norm_g"] = head[0], head[1]
    for n, a0, a1 in zip(LAYER_SMALL, sum0, sum1):
        a = jnp.stack([a0, a1])
        G[n] = lax.dynamic_slice_in_dim(a, me * LRU_SHARD, LRU_SHARD, axis=a.ndim - 1) if n in SHARDED_SMALL else a

    rest = [n for n in WEIGHTS if n not in BIG]
    shapes = [W[n].shape for n in rest]
    outs, _ = adamw(*[_pack([src[n] for n in rest], row_multiple=ADAMW_SMALL_ROWS) for src in (W, G, M, V)], "adamw_small")
    for dst, packed in zip((delta, new_m, new_v), outs):
        for n, a in zip(rest, _unpack(packed, shapes)):
            dst[n] = a

    loss = head[3][0, 0]
    grad_x = dx[None]
    return (loss, grad_x, *[G[n] for n in WEIGHTS], *[delta[n] for n in WEIGHTS], *[new_m[n] for n in WEIGHTS],
            *[new_v[n] for n in WEIGHTS])
```

```python
import jax
import jax.numpy as jnp
from jax import lax
from jax.experimental import pallas as pl
from jax.experimental.pallas import tpu as pltpu

F32 = jnp.float32
BF16 = jnp.bfloat16

D = 1024
CTX = 256
DEPTH = 2
EPS = 1e-6
D_FF = 2816
LRU_W = 512
HEADS = 8
HEAD_DIM = 64
CONV_W = 4
RG_C = 8.0
GROUPS = 4
GROUP_DIM = 128
CHUNK = 128
MLP_W = 512
IN_PROJ = 2048
N_MOD = 9
N_DEV = 8

LR = 0.001
B1 = 0.9
B2 = 0.999
ADAM_EPS = 1e-08
WD = 0.01
STEP = 10

HT = 256
WT = 512
N_MIX_SHARD = IN_PROJ // N_DEV
ADA_SHARD = N_MOD * D // N_DEV

TM = 256
SUB = 3
TMX = SUB * TM
HALO = 8
VMEM_LIMIT = 60 * 1024 * 1024

MESH = pl.DeviceIdType.MESH
ANY = pl.BlockSpec(memory_space=pl.ANY)


def _cp(n_axes=1):
    return pltpu.CompilerParams(dimension_semantics=("arbitrary",) * n_axes, vmem_limit_bytes=VMEM_LIMIT)


def _position():
    return lax.axis_index("x"), lax.axis_index("y"), lax.axis_index("c")


class GatherRider:
    def __init__(self, shards):
        n = len(shards)
        self.n = n
        self.ins = list(shards)
        self.out_shape = [jax.ShapeDtypeStruct((N_DEV,) + s.shape, s.dtype) for s in shards]
        self.sems = [pltpu.SemaphoreType.DMA((n, 7)), pltpu.SemaphoreType.DMA((n, 7)), pltpu.SemaphoreType.DMA((n,))]

    def _ctx(self, outs, sems):
        x, y, c = _position()
        chips = [(1 - x, y), (x, 1 - y), (1 - x, 1 - y)]

        def copy(t, k, block, to, src=None):
            dst = outs[t].at[4 * block[0] + 2 * block[1] + block[2]]
            return pltpu.make_async_remote_copy(
                src_ref=dst if src is None else src, dst_ref=dst, send_sem=sems[0].at[t, k],
                recv_sem=sems[1].at[t, k], device_id=to, device_id_type=MESH)

        return (x, y, c), (x, y, 1 - c), chips, copy

    def _local(self, ins, outs, sems, t):
        x, y, c = _position()
        return pltpu.make_async_copy(ins[t], outs[t].at[4 * x + 2 * y + c], sems[2].at[t])

    def _first(self, ins, outs, sems, t):
        me, sibling, chips, copy = self._ctx(outs, sems)
        return [copy(t, 0, me, sibling, src=ins[t])] + [copy(t, 1 + j, me, (*chip, me[2]), src=ins[t])
                                                         for j, chip in enumerate(chips)]

    def start(self, ins, outs, sems):
        for t in range(self.n):
            self._local(ins, outs, sems, t).start()
            for cp in self._first(ins, outs, sems, t):
                cp.start()

    def mid(self, ins, outs, sems):
        me, sibling, chips, copy = self._ctx(outs, sems)
        for j, chip in enumerate(chips):
            for t in range(self.n):
                copy(t, 1 + j, (*chip, me[2]), me).wait_recv()
                copy(t, 4 + j, (*chip, me[2]), sibling).start()

    def finish(self, ins, outs, sems):
        me, sibling, chips, copy = self._ctx(outs, sems)
        for t in range(self.n):
            copy(t, 0, sibling, me).wait_recv()
            for j, chip in enumerate(chips):
                copy(t, 4 + j, (*chip, 1 - me[2]), me).wait_recv()
        for t in range(self.n):
            for cp in self._first(ins, outs, sems, t):
                cp.wait_send()
            for j, chip in enumerate(chips):
                copy(t, 4 + j, (*chip, me[2]), sibling).wait_send()
            self._local(ins, outs, sems, t).wait()


class ExchangeRider:
    def __init__(self, tensors, plan, n_slots):
        n = len(tensors)
        self.n, self.plan = n, plan
        self.ins = list(tensors)
        self.out_shape = [jax.ShapeDtypeStruct((n_slots,) + s.shape[1:], s.dtype) for s in tensors]
        self.sems = [pltpu.SemaphoreType.DMA((n, n_slots)), pltpu.SemaphoreType.DMA((n, n_slots))]

    def _copies(self, ins, outs, sems):
        return [pltpu.make_async_remote_copy(
            src_ref=ins[t].at[block], dst_ref=outs[t].at[k], send_sem=sems[0].at[t, k], recv_sem=sems[1].at[t, k],
            device_id=to, device_id_type=MESH)
            for t in range(self.n) for k, (block, to) in enumerate(self.plan(*_position()))]

    def start(self, ins, outs, sems):
        for cp in self._copies(ins, outs, sems):
            cp.start()

    def mid(self, ins, outs, sems):
        pass

    def finish(self, ins, outs, sems):
        for cp in self._copies(ins, outs, sems):
            cp.wait()


class Riders:
    def __init__(self, riders):
        self.riders = list(riders)
        self.ins = [a for r in self.riders for a in r.ins]
        self.out_shape = [s for r in self.riders for s in r.out_shape]
        self.sems = [s for r in self.riders for s in r.sems]

    def _each(self, ins, outs, sems):
        i = o = s = 0
        for r in self.riders:
            ni, no, ns = len(r.ins), len(r.out_shape), len(r.sems)
            yield r, ins[i:i + ni], outs[o:o + no], sems[s:s + ns]
            i, o, s = i + ni, o + no, s + ns

    def start(self, ins, outs, sems):
        for r, a, b, c in self._each(ins, outs, sems):
            r.start(a, b, c)

    def mid(self, ins, outs, sems):
        for r, a, b, c in self._each(ins, outs, sems):
            r.mid(a, b, c)

    def finish(self, ins, outs, sems):
        for r, a, b, c in self._each(ins, outs, sems):
            r.finish(a, b, c)

    def split(self, outs):
        res, o = [], 0
        for r in self.riders:
            res.append(list(outs[o:o + len(r.out_shape)]))
            o += len(r.out_shape)
        return res


def pair_rider(grads):
    def plan(x, y, c):
        return [(4 * cx + 2 * cy + (1 - c), (x, y, 1 - c)) for cx in range(2) for cy in range(2)]
    return ExchangeRider(grads, plan, 4)


def chips_rider(parts):
    def plan(x, y, c):
        return [(2 * cx + cy, (cx, cy, c)) for cx, cy in [(1 - x, y), (x, 1 - y), (1 - x, 1 - y)]]
    return ExchangeRider(parts, plan, 3)


def run_alone(rider, space, name):
    ni = len(rider.ins)
    no = len(rider.out_shape)

    def body(*refs):
        ins, outs, sems = refs[:ni], refs[ni:ni + no], refs[ni + no:]
        rider.start(ins, outs, sems)
        rider.mid(ins, outs, sems)
        rider.finish(ins, outs, sems)

    spec = pl.BlockSpec(memory_space=space)
    return pl.pallas_call(
        body, name=name, in_specs=[spec] * ni, out_specs=[spec] * no, out_shape=rider.out_shape,
        scratch_shapes=rider.sems, compiler_params=pltpu.CompilerParams(vmem_limit_bytes=VMEM_LIMIT),
    )(*rider.ins)


def _grid_call(body, *, name, nsteps, in_specs, out_specs, out_shape, scratch_shapes, args, rider=None, aliases=None):
    aliases = aliases or {}
    if rider is None:
        outs = pl.pallas_call(body, name=name, grid=(nsteps,), in_specs=in_specs, out_specs=out_specs,
                              out_shape=out_shape, scratch_shapes=scratch_shapes, input_output_aliases=aliases,
                              compiler_params=_cp())(*args)
        return outs, []
    ni, no, ns = len(in_specs), len(out_specs), len(scratch_shapes)
    ri, ro = len(rider.ins), len(rider.out_shape)

    def wrapped(*refs):
        ins, refs = refs[:ni], refs[ni:]
        r_ins, refs = refs[:ri], refs[ri:]
        outs, refs = refs[:no], refs[no:]
        r_outs, refs = refs[:ro], refs[ro:]
        scratch, r_sems = refs[:ns], refs[ns:]
        s = pl.program_id(0)

        @pl.when(s == 0)
        def _():
            rider.start(r_ins, r_outs, r_sems)

        body(*ins, *outs, *scratch)

        @pl.when(s == (3 * nsteps) // 4)
        def _():
            rider.mid(r_ins, r_outs, r_sems)

        @pl.when(s == nsteps - 1)
        def _():
            rider.finish(r_ins, r_outs, r_sems)

    outs = pl.pallas_call(
        wrapped, name=name, grid=(nsteps,), in_specs=list(in_specs) + [ANY] * ri, out_specs=list(out_specs) + [ANY] * ro,
        out_shape=list(out_shape) + rider.out_shape, scratch_shapes=list(scratch_shapes) + rider.sems,
        input_output_aliases=aliases, compiler_params=_cp())(*args, *rider.ins)
    return outs[:no], outs[no:]


def _dot(a, b):
    return jnp.dot(a, b, preferred_element_type=F32)


def _dot_nt(a, b):
    return lax.dot_general(a, b, (((1,), (1,)), ((), ())), preferred_element_type=F32)


def _dot_tn(a, b):
    return lax.dot_general(a, b, (((0,), (0,)), ((), ())), preferred_element_type=F32)


def _sigmoid(x):
    return 1.0 / (1.0 + jnp.exp(-x))


def _kind(i):
    return jnp.where(i < CTX // TM, 0, 1)


def _sel(kind, mod_ref, k):
    return mod_ref[kind, k:k + 1, :]


def _acc2(ref, k, val, kind):
    ref[kind, k:k + 1, :] += jnp.sum(val, axis=0, keepdims=True)


def _norm_mod(h, g, shift, scale):
    r = lax.rsqrt(jnp.mean(h * h, axis=-1, keepdims=True) + EPS)
    n = h * r
    return (n * g) * (1.0 + scale) + shift, n, r


def _norm_mod_bwd(dz, n, r, g, scale):
    dn = dz * (g * (1.0 + scale))
    return r * (dn - n * jnp.mean(dn * n, axis=-1, keepdims=True))


def ffn_fwd(h, mod, g, win, wout, name, rider=None, loss=None):
    split = isinstance(h, tuple)
    nc = CTX // TM
    t = h[0].shape[0] + h[1].shape[0] if split else h.shape[0]

    def body(*refs):
        refs = list(refs)
        win_v, wout_v, a_v = refs[-3:]
        rows = refs[:2] if split else refs[:1]
        mod_ref, g_ref, win_hbm, wout_hbm = refs[len(rows):len(rows) + 4]
        rest = refs[len(rows) + 4:-3]
        if loss is not None:
            fg_ref, tgt_ref, rest = rest[0], rest[1], rest[2:]
        out_ref, gu_ref, acc_ref, rest = rest[0], rest[1], rest[2], rest[3:]
        i = pl.program_id(0)

        @pl.when(i == 0)
        def _():
            pltpu.sync_copy(win_hbm, win_v)
            pltpu.sync_copy(wout_hbm, wout_v)

        if split:
            hh = jnp.where(i < nc, rows[0][...], rows[1][...])
            rest[0][...] = hh
        else:
            hh = rows[0][...]
        ic = _kind(i)
        z, _, _ = _norm_mod(hh, g_ref[...], _sel(ic, mod_ref, 0), _sel(ic, mod_ref, 1))
        zb = z.astype(BF16)
        for j in range(D_FF // HT):
            gb, ub = slice(j * HT, (j + 1) * HT), slice(D_FF + j * HT, D_FF + (j + 1) * HT)
            gg = _dot_nt(zb, win_v[gb, :])
            uu = _dot_nt(zb, win_v[ub, :])
            gu_ref[:, gb] = gg.astype(BF16)
            gu_ref[:, ub] = uu.astype(BF16)
            a_v[:, gb] = ((gg * _sigmoid(gg)) * uu).astype(BF16)
        acc = _dot(a_v[...], wout_v[...])
        acc_ref[...] = acc
        hn = hh + (0.5 * _sel(ic, mod_ref, 2)) * acc
        if loss is None:
            out_ref[...] = hn
        else:
            loss_ref, dgf_ref = rest

            @pl.when(i == 0)
            def _():
                loss_ref[...] = jnp.zeros_like(loss_ref)
                dgf_ref[...] = jnp.zeros_like(dgf_ref)

            @pl.when(i < nc)
            def _():
                out_ref[...] = jnp.zeros_like(out_ref)

            @pl.when(i >= nc)
            def _():
                gain = fg_ref[...]
                r = lax.rsqrt(jnp.mean(hn * hn, axis=-1, keepdims=True) + EPS)
                n = hn * r
                err = n * gain - tgt_ref[...]
                loss_ref[...] += 0.5 * jnp.sum(jnp.mean(err * err, axis=-1, keepdims=True))
                dy = err * (1.0 / D)
                dgf_ref[...] += jnp.sum(dy * n, axis=0, keepdims=True)
                dn = dy * gain
                out_ref[...] = r * (dn - n * jnp.mean(dn * n, axis=-1, keepdims=True))

    row = pl.BlockSpec((TM, D), lambda i: (i, 0))
    vec = pl.BlockSpec((1, D), lambda i: (0, 0))
    rshape = jax.ShapeDtypeStruct((t, D), F32)
    if split:
        rows_in = [pl.BlockSpec((TM, D), lambda i: (jnp.minimum(i, nc - 1), 0)),
                   pl.BlockSpec((TM, D), lambda i: (jnp.maximum(i - nc, 0), 0))]
    else:
        rows_in = [row]
    in_specs = rows_in + [pl.BlockSpec((2, 3, D), lambda i: (0, 0, 0)), vec, ANY, ANY]
    out_specs = [row, pl.BlockSpec((TM, 2 * D_FF), lambda i: (i, 0)), row] + ([row] if split else [])
    out_shape = [rshape, jax.ShapeDtypeStruct((t, 2 * D_FF), BF16), rshape] + ([rshape] if split else [])
    args = (*(h if split else (h,)), mod, g, win, wout)
    if loss is not None:
        in_specs += [vec, pl.BlockSpec((TM, D), lambda i: (jnp.maximum(i - nc, 0), 0))]
        out_specs += [pl.BlockSpec((8, 128), lambda i: (0, 0)), vec]
        out_shape += [jax.ShapeDtypeStruct((8, 128), F32), jax.ShapeDtypeStruct((1, D), F32)]
        args += tuple(loss)
    return _grid_call(
        body, name=name, nsteps=t // TM, in_specs=in_specs, out_specs=out_specs, out_shape=out_shape,
        scratch_shapes=[pltpu.VMEM((2 * D_FF, D), BF16), pltpu.VMEM((D_FF, D), BF16), pltpu.VMEM((TM, D_FF), BF16)],
        args=args, rider=rider)


def ffn_bwd_a(dy, acc, gu, mod, wout, name, rider=None):
    t = dy.shape[0]
    nt = t // TM

    def body(dy_ref, acc_ref, gu_ref, mod_ref, wout_hbm, dp_ref, dwout_hbm, dgate_ref, wout_v, dwout_v):
        i = pl.program_id(0)

        @pl.when(i == 0)
        def _():
            pltpu.sync_copy(wout_hbm, wout_v)
            dwout_v[...] = jnp.zeros_like(dwout_v)
            dgate_ref[...] = jnp.zeros_like(dgate_ref)

        dy_ = dy_ref[...]
        ic = _kind(i)
        _acc2(dgate_ref, 0, 0.5 * dy_ * acc_ref[...], ic)
        daccb = ((0.5 * _sel(ic, mod_ref, 2)) * dy_).astype(BF16)
        for j in range(D_FF // HT):
            blk, ublk = slice(j * HT, (j + 1) * HT), slice(D_FF + j * HT, D_FF + (j + 1) * HT)
            da = _dot_nt(daccb, wout_v[blk, :])
            gg = gu_ref[:, blk].astype(F32)
            uu = gu_ref[:, ublk].astype(F32)
            s = _sigmoid(gg)
            sl = gg * s
            dwout_v[blk, :] += _dot_tn((sl * uu).astype(BF16), daccb)
            dp_ref[:, blk] = (da * uu * (s + sl * (1.0 - s))).astype(BF16)
            dp_ref[:, ublk] = (da * sl).astype(BF16)

        @pl.when(i == nt - 1)
        def _():
            pltpu.sync_copy(dwout_v, dwout_hbm)

    row = pl.BlockSpec((TM, D), lambda i: (i, 0))
    wide = pl.BlockSpec((TM, 2 * D_FF), lambda i: (i, 0))
    return _grid_call(
        body, name=name, nsteps=nt,
        in_specs=[row, row, wide, pl.BlockSpec((2, 3, D), lambda i: (0, 0, 0)), ANY],
        out_specs=[wide, ANY, pl.BlockSpec((2, 1, D), lambda i: (0, 0, 0))],
        out_shape=[jax.ShapeDtypeStruct((t, 2 * D_FF), BF16), jax.ShapeDtypeStruct((D_FF, D), F32),
                   jax.ShapeDtypeStruct((2, 1, D), F32)],
        scratch_shapes=[pltpu.VMEM((D_FF, D), BF16), pltpu.VMEM((D_FF, D), F32)],
        args=(dy, acc, gu, mod, wout), rider=rider)


def ffn_bwd_b(dy, h, dp, mod, g, win, name, rider=None, latent_only=False):
    t = dy.shape[0]
    nt = t // TM
    nc = CTX // TM

    def body(dy_ref, h_ref, dp_ref, mod_ref, g_ref, win_hbm, dh_ref, dwin_hbm, dmod_ref, dg_ref, win_v, dwin_v):
        i = pl.program_id(0)

        @pl.when(i == 0)
        def _():
            pltpu.sync_copy(win_hbm, win_v)
            dwin_v[...] = jnp.zeros_like(dwin_v)
            dmod_ref[...] = jnp.zeros_like(dmod_ref)
            dg_ref[...] = jnp.zeros_like(dg_ref)

        ic = _kind(i)
        gain = g_ref[...]
        scale = _sel(ic, mod_ref, 1)
        z, n, r = _norm_mod(h_ref[...], gain, _sel(ic, mod_ref, 0), scale)
        zb = z.astype(BF16)
        dz = _dot(dp_ref[...], win_v[...])
        for j in range(2 * D_FF // WT):
            blk = slice(j * WT, (j + 1) * WT)
            dwin_v[blk, :] += _dot_tn(dp_ref[:, blk], zb)
        _acc2(dmod_ref, 0, dz, ic)
        _acc2(dmod_ref, 1, dz * (n * gain), ic)
        dg_ref[...] += jnp.sum(dz * (1.0 + scale) * n, axis=0, keepdims=True)
        dh_ref[...] = dy_ref[...] + _norm_mod_bwd(dz, n, r, gain, scale)

        @pl.when(i == nt - 1)
        def _():
            pltpu.sync_copy(dwin_v, dwin_hbm)

    row = pl.BlockSpec((TM, D), lambda i: (i, 0))
    if latent_only:
        dh_spec = pl.BlockSpec((TM, D), lambda i: (jnp.maximum(i - nc, 0), 0))
        dh_shape = jax.ShapeDtypeStruct((t - CTX, D), F32)
    else:
        dh_spec, dh_shape = row, jax.ShapeDtypeStruct((t, D), F32)
    return _grid_call(
        body, name=name, nsteps=nt,
        in_specs=[row, row, pl.BlockSpec((TM, 2 * D_FF), lambda i: (i, 0)),
                  pl.BlockSpec((2, 3, D), lambda i: (0, 0, 0)), pl.BlockSpec((1, D), lambda i: (0, 0)), ANY],
        out_specs=[dh_spec, ANY, pl.BlockSpec((2, 2, D), lambda i: (0, 0, 0)), pl.BlockSpec((1, D), lambda i: (0, 0))],
        out_shape=[dh_shape, jax.ShapeDtypeStruct((2 * D_FF, D), F32),
                   jax.ShapeDtypeStruct((2, 2, D), F32), jax.ShapeDtypeStruct((1, D), F32)],
        scratch_shapes=[pltpu.VMEM((2 * D_FF, D), BF16), pltpu.VMEM((2 * D_FF, D), F32)],
        args=(dy, h, dp, mod, g, win), rider=rider)


def ffn_bwd_dw(h, dp, mod, g, name, rider=None):
    t = h.shape[0]
    nt = t // TM

    def body(h_ref, dp_ref, mod_ref, g_ref, dwin_hbm, dwin_v):
        i = pl.program_id(0)

        @pl.when(i == 0)
        def _():
            dwin_v[...] = jnp.zeros_like(dwin_v)

        zs = []
        for k in range(SUB):
            ic = _kind(SUB * i + k)
            z, _, _ = _norm_mod(h_ref[pl.ds(k * TM, TM), :], g_ref[...], _sel(ic, mod_ref, 0), _sel(ic, mod_ref, 1))
            zs.append(z.astype(BF16))
        zb = jnp.concatenate(zs, axis=0)
        for j in range(2 * D_FF // WT):
            blk = slice(j * WT, (j + 1) * WT)
            dwin_v[blk, :] += _dot_tn(dp_ref[:, blk], zb)

        @pl.when(i == nt // SUB - 1)
        def _():
            pltpu.sync_copy(dwin_v, dwin_hbm)

    return _grid_call(
        body, name=name, nsteps=nt // SUB,
        in_specs=[pl.BlockSpec((TMX, D), lambda i: (i, 0)), pl.BlockSpec((TMX, 2 * D_FF), lambda i: (i, 0)),
                  pl.BlockSpec((2, 3, D), lambda i: (0, 0, 0)), pl.BlockSpec((1, D), lambda i: (0, 0))],
        out_specs=[ANY], out_shape=[jax.ShapeDtypeStruct((2 * D_FF, D), F32)],
        scratch_shapes=[pltpu.VMEM((2 * D_FF, D), F32)],
        args=(h, dp, mod, g), rider=rider)


def ffn_bwd_dh(dy, h, dp, mod, g, win, name, tiles, carry=None, rider=None):
    t = dy.shape[0]
    nc = CTX // TM
    t0, t1 = tiles

    def body(*refs):
        if carry is None:
            dy_ref, h_ref, dp_ref, mod_ref, g_ref, win_hbm, dh_ref, dmod_ref, dg_ref, win_v = refs
        else:
            dy_ref, h_ref, dp_ref, mod_ref, g_ref, win_hbm, _, dmod0_ref, dg0_ref, dh_ref, dmod_ref, dg_ref, win_v = refs
        i = pl.program_id(0)

        @pl.when(i == 0)
        def _():
            pltpu.sync_copy(win_hbm, win_v)
            dmod_ref[...] = jnp.zeros_like(dmod_ref) if carry is None else dmod0_ref[...]
            dg_ref[...] = jnp.zeros_like(dg_ref) if carry is None else dg0_ref[...]

        ic = _kind(i + t0)
        gain = g_ref[...]
        scale = _sel(ic, mod_ref, 1)
        _, n, r = _norm_mod(h_ref[...], gain, _sel(ic, mod_ref, 0), scale)
        dz = _dot(dp_ref[...], win_v[...])
        _acc2(dmod_ref, 0, dz, ic)
        _acc2(dmod_ref, 1, dz * (n * gain), ic)
        dg_ref[...] += jnp.sum(dz * (1.0 + scale) * n, axis=0, keepdims=True)
        dh_ref[...] = dy_ref[...] + _norm_mod_bwd(dz, n, r, gain, scale)

    row = pl.BlockSpec((TM, D), lambda i: (i + t0, 0))
    small = [pl.BlockSpec((2, 2, D), lambda i: (0, 0, 0)), pl.BlockSpec((1, D), lambda i: (0, 0))]
    in_specs = [row, row, pl.BlockSpec((TM, 2 * D_FF), lambda i: (i + t0, 0)),
                pl.BlockSpec((2, 3, D), lambda i: (0, 0, 0)), pl.BlockSpec((1, D), lambda i: (0, 0)), ANY]
    args = (dy, h, dp, mod, g, win)
    if carry is not None:
        in_specs += [ANY] + small
        args += tuple(carry)
    return _grid_call(
        body, name=name, nsteps=t1 - t0, in_specs=in_specs,
        out_specs=[pl.BlockSpec((TM, D), lambda i: (jnp.maximum(i + t0 - nc, 0), 0))] + small,
        out_shape=[jax.ShapeDtypeStruct((t - CTX, D), F32), jax.ShapeDtypeStruct((2, 2, D), F32),
                   jax.ShapeDtypeStruct((1, D), F32)],
        scratch_shapes=[pltpu.VMEM((2 * D_FF, D), BF16)],
        args=args, rider=rider, aliases=None if carry is None else {6: 0})


def mix_in_fwd(h, mod, g, wmix, name):
    t = h.shape[0]

    def body(h_ref, mod_ref, g_ref, w_hbm, p_ref, w_v):
        i = pl.program_id(0)

        @pl.when(i == 0)
        def _():
            pltpu.sync_copy(w_hbm, w_v)

        zs = []
        for k in range(SUB):
            ic = _kind(SUB * i + k)
            z, _, _ = _norm_mod(h_ref[pl.ds(k * TM, TM), :], g_ref[...], _sel(ic, mod_ref, 0), _sel(ic, mod_ref, 1))
            zs.append(z.astype(BF16))
        zb = jnp.concatenate(zs, axis=0)
        for dd in range(N_DEV):
            p_ref[:, dd * N_MIX_SHARD:(dd + 1) * N_MIX_SHARD] = _dot(zb, w_v[dd])

    return pl.pallas_call(
        body, name=name, grid=(t // TMX,),
        in_specs=[pl.BlockSpec((TMX, D), lambda i: (i, 0)), pl.BlockSpec((2, 3, D), lambda i: (0, 0, 0)),
                  pl.BlockSpec((1, D), lambda i: (0, 0)), ANY],
        out_specs=pl.BlockSpec((TMX, IN_PROJ), lambda i: (i, 0)),
        out_shape=jax.ShapeDtypeStruct((t, IN_PROJ), F32),
        scratch_shapes=[pltpu.VMEM((N_DEV, D, N_MIX_SHARD), BF16)],
        compiler_params=_cp(),
    )(h, mod, g, wmix)


def _halo_specs(nt, tile_of, rows=TM):
    nb = nt * (rows // HALO)
    main = pl.BlockSpec((rows, LRU_W), lambda s: (tile_of(s), 0))
    prev = pl.BlockSpec((HALO, LRU_W), lambda s: (jnp.maximum(tile_of(s) * (rows // HALO) - 1, 0), 0))
    nxt = pl.BlockSpec((HALO, LRU_W), lambda s: (jnp.minimum((tile_of(s) + 1) * (rows // HALO), nb - 1), 0))
    return main, prev, nxt


def _ext(tile, nt, main, prev, nxt):
    has_prev = jnp.logical_and(tile != 0, tile != 1)
    has_next = jnp.logical_and(tile != 0, tile != nt - 1)
    return jnp.concatenate([jnp.where(has_prev, prev, 0.0), main, jnp.where(has_next, nxt, 0.0)], axis=0)


def _shifted(ext, off):
    n = ext.shape[0]
    return pltpu.roll(ext, (-off) % n, 0)[HALO:HALO + TM]


def _conv(ext, cw_ref, cb_ref):
    xc = cb_ref[...] + cw_ref[0:1, :] * _shifted(ext, -2)
    for k in range(1, CONV_W):
        xc = xc + cw_ref[k:k + 1, :] * _shifted(ext, k - 2)
    return xc


def _log1p(y):
    return jnp.where(y < 1e-2, y * (1.0 - y * (0.5 - y * (1.0 / 3.0 - 0.25 * y))), jnp.log(1.0 + y))


def _softplus_neg(lam):
    return jnp.maximum(-lam, 0.0) + _log1p(jnp.exp(-jnp.abs(lam)))


def _one_minus_exp(x, exp_half):
    p = x * (1.0 + x * (1 / 2 + x * (1 / 6 + x * (1 / 24))))
    return jnp.where(x > -0.1, -p, 1.0 - exp_half * exp_half)


def _gates(xc, wr, br, wi, bi, lam):
    xb = xc.astype(BF16)
    r = _sigmoid(_dot(xb, wr) + br)
    ig = _sigmoid(_dot(xb, wi) + bi)
    sp = _softplus_neg(lam)
    log_a = -RG_C * r * sp
    a = jnp.exp(log_a)
    mult = jnp.sqrt(_one_minus_exp(2.0 * log_a, a))
    return r, ig, sp, a, mult


def _scan(a, b, reverse):
    n = a.shape[0]
    row = lax.broadcasted_iota(jnp.int32, a.shape, 0)
    s = 1
    while s < n:
        if s < HALO:
            if reverse:
                keep = row < n - s
                a_s = jnp.where(keep, pltpu.roll(a, n - s, 0), 1.0)
                b_s = jnp.where(keep, pltpu.roll(b, n - s, 0), 0.0)
            else:
                keep = row >= s
                a_s = jnp.where(keep, pltpu.roll(a, s, 0), 1.0)
                b_s = jnp.where(keep, pltpu.roll(b, s, 0), 0.0)
            b = a * b_s + b
            a = a * a_s
        elif reverse:
            b = jnp.concatenate([a[:n - s] * b[s:] + b[:n - s], b[n - s:]], axis=0)
            a = jnp.concatenate([a[:n - s] * a[s:], a[n - s:]], axis=0)
        else:
            b = jnp.concatenate([b[:s], a[s:] * b[:n - s] + b[s:]], axis=0)
            a = jnp.concatenate([a[:s], a[s:] * a[:n - s]], axis=0)
        s *= 2
    return a, b


def lru_fwd(p, conv_w, conv_b, wr, br, wi, bi, lam, reverse, name, rider=None):
    t = p.shape[0]
    nt = t // TM

    def tile_of(s):
        return jnp.where(s == 0, 0, nt - s) if reverse else s

    def body(x_ref, xp_ref, xn_ref, cw_ref, cb_ref, wr_ref, br_ref, wi_ref, bi_ref, lam_ref, h_ref, carry):
        s = pl.program_id(0)
        tile = tile_of(s)

        @pl.when(s == 0)
        def _():
            carry[...] = jnp.zeros_like(carry)

        ext = _ext(tile, nt, x_ref[...], xp_ref[...], xn_ref[...])
        xc = _conv(ext, cw_ref, cb_ref)
        _, ig, _, a, mult = _gates(xc, wr_ref[...], br_ref[...], wi_ref[...], bi_ref[...], lam_ref[...])
        a_cum, hl = _scan(a, mult * (ig * xc), reverse)
        hh = hl + a_cum * carry[...]
        h_ref[...] = hh
        carry[...] = hh[0:1, :] if reverse else hh[TM - 1:TM, :]

    main, prev, nxt = _halo_specs(nt, tile_of)
    vec = pl.BlockSpec((1, LRU_W), lambda s: (0, 0))
    mat = pl.BlockSpec((LRU_W, LRU_W), lambda s: (0, 0))
    outs, got = _grid_call(
        body, name=name, nsteps=nt,
        in_specs=[main, prev, nxt, pl.BlockSpec((CONV_W, LRU_W), lambda s: (0, 0)), vec, mat, vec, mat, vec, vec],
        out_specs=[main],
        out_shape=[jax.ShapeDtypeStruct((t, LRU_W), F32)],
        scratch_shapes=[pltpu.VMEM((1, LRU_W), F32)],
        args=(p, p, p, conv_w, conv_b, wr, br, wi, bi, lam), rider=rider)
    return outs[0], got


def lru_bwd(p, hs, dhs, conv_w, conv_b, wr, br, wi, bi, lam, reverse, name):
    t = p.shape[0]
    nt = t // TM
    bpt = TM // HALO

    def tile_of(s):
        return jnp.where(s == nt - 1, 0, s + 1) if reverse else nt - 1 - s

    def hprev_block(s):
        tile = tile_of(s)
        if reverse:
            return (jnp.where(tile == nt - 1, 0, jnp.minimum((tile + 1) * bpt, nt * bpt - 1)), 0)
        return (jnp.maximum(tile * bpt - 1, 0), 0)

    def body(x_ref, xp_ref, xn_ref, h_ref, hp_ref, dh_ref, cw_ref, cb_ref, wr_ref, br_ref, wi_ref, bi_ref, lam_ref,
             dxc_ref, dwr_out, dwi_out, dbr_ref, dbi_ref, dlam_ref, carry, dwr_ref, dwi_ref):
        s = pl.program_id(0)
        tile = tile_of(s)

        @pl.when(s == 0)
        def _():
            carry[...] = jnp.zeros_like(carry)
            for ref in (dwr_ref, dwi_ref, dbr_ref, dbi_ref, dlam_ref):
                ref[...] = jnp.zeros_like(ref)

        ext = _ext(tile, nt, x_ref[...], xp_ref[...], xn_ref[...])
        xc = _conv(ext, cw_ref, cb_ref)
        wr_, wi_ = wr_ref[...], wi_ref[...]
        r, ig, sp, a, mult = _gates(xc, wr_, br_ref[...], wi_, bi_ref[...], lam_ref[...])
        gated = ig * xc
        row = lax.broadcasted_iota(jnp.int32, (TM, LRU_W), 0)
        hh = h_ref[...]
        start = jnp.where(tile != 0, hp_ref[0:1, :] if reverse else hp_ref[HALO - 1:HALO, :], 0.0)
        if reverse:
            edge = row == TM - 1
            hprev = jnp.where(edge, start, pltpu.roll(hh, TM - 1, 0))
            coef = jnp.where(row == 0, 0.0, pltpu.roll(a, 1, 0))
            bb = dh_ref[...] + jnp.where(row == 0, carry[...], 0.0)
        else:
            edge = row == 0
            hprev = jnp.where(edge, start, pltpu.roll(hh, 1, 0))
            coef = jnp.where(row == TM - 1, 0.0, pltpu.roll(a, TM - 1, 0))
            bb = dh_ref[...] + jnp.where(row == TM - 1, carry[...], 0.0)
        _, lmb = _scan(coef, bb, not reverse)
        al = a * lmb
        carry[...] = al[TM - 1:TM, :] if reverse else al[0:1, :]

        dgated = lmb * mult
        dloga = (lmb * hprev) * a - (lmb * gated) * (a * a) / mult
        dpre_r = (dloga * (-RG_C * sp)) * r * (1.0 - r)
        dpre_i = (dgated * xc) * ig * (1.0 - ig)
        drb, dib = dpre_r.astype(BF16), dpre_i.astype(BF16)
        xb = xc.astype(BF16)
        dxc_ref[...] = dgated * ig + _dot_nt(drb, wr_) + _dot_nt(dib, wi_)
        dwr_ref[...] += _dot_tn(xb, drb)
        dwi_ref[...] += _dot_tn(xb, dib)
        dbr_ref[...] += jnp.sum(dpre_r, axis=0, keepdims=True)
        dbi_ref[...] += jnp.sum(dpre_i, axis=0, keepdims=True)
        dlam_ref[...] += jnp.sum(dloga * (-RG_C * r), axis=0, keepdims=True)

        @pl.when(s == nt - 1)
        def _():
            dlam_ref[...] = dlam_ref[...] * (-_sigmoid(-lam_ref[...]))
            for hd in range(HEADS):
                blk = pl.ds(hd * HEAD_DIM, HEAD_DIM)
                dwr_out[hd] = dwr_ref[blk, blk]
                dwi_out[hd] = dwi_ref[blk, blk]

    main, prev, nxt = _halo_specs(nt, tile_of)
    vec = pl.BlockSpec((1, LRU_W), lambda s: (0, 0))
    mat = pl.BlockSpec((LRU_W, LRU_W), lambda s: (0, 0))
    heads = pl.BlockSpec((HEADS, HEAD_DIM, HEAD_DIM), lambda s: (0, 0, 0))
    vshape = jax.ShapeDtypeStruct((1, LRU_W), F32)
    hshape = jax.ShapeDtypeStruct((HEADS, HEAD_DIM, HEAD_DIM), F32)
    return pl.pallas_call(
        body, name=name, grid=(nt,),
        in_specs=[main, prev, nxt, main, pl.BlockSpec((HALO, LRU_W), hprev_block), main,
                  pl.BlockSpec((CONV_W, LRU_W), lambda s: (0, 0)), vec, mat, vec, mat, vec, vec],
        out_specs=[main, heads, heads, vec, vec, vec],
        out_shape=[jax.ShapeDtypeStruct((t, LRU_W), F32), hshape, hshape, vshape, vshape, vshape],
        scratch_shapes=[pltpu.VMEM((1, LRU_W), F32), pltpu.VMEM((LRU_W, LRU_W), F32), pltpu.VMEM((LRU_W, LRU_W), F32)],
        compiler_params=_cp(),
    )(p, p, p, hs, hs, dhs, conv_w, conv_b, wr, br, wi, bi, lam)


GELU_C = 0.7978845608028654
GELU_A = 0.044715


def _gelu(x):
    th = jnp.tanh(GELU_C * (x + GELU_A * x * x * x))
    return 0.5 * x * (1.0 + th), th


def _sgu(v, gain, w_ref, bt_ref):
    mu = jnp.mean(v, axis=-1, keepdims=True)
    xc = v - mu
    rs = lax.rsqrt(jnp.mean(xc * xc, axis=-1, keepdims=True) + EPS)
    vhat = xc * rs
    vnb = (vhat * gain).astype(BF16)
    chunks = []
    for ch in range(TM // CHUNK):
        zs = []
        for gi in range(GROUPS):
            vb = vnb[ch * CHUNK:(ch + 1) * CHUNK, gi * GROUP_DIM:(gi + 1) * GROUP_DIM]
            zs.append(_dot(w_ref[gi].astype(BF16), vb) + bt_ref[:, gi:gi + 1])
        chunks.append(jnp.concatenate(zs, axis=1))
    return jnp.concatenate(chunks, axis=0), vhat, rs, vnb


def _pcols(k, rows=TM):
    return pl.BlockSpec((rows, LRU_W), lambda i: (i, k))


def mix_out_fwd(h, p, hf, hb, mod, sgu_g, sgu_w, sgu_bt, womix, name):
    t = h.shape[0]

    def body(h_ref, gl_ref, u_ref, v_ref, hf_ref, hb_ref, mod_ref, sg_ref, sw_ref, sb_ref, w_hbm, out_ref, o_ref, w_v):
        i = pl.program_id(0)

        @pl.when(i == 0)
        def _():
            pltpu.sync_copy(w_hbm, w_v)

        ys = []
        for k in range(SUB):
            rows = pl.ds(k * TM, TM)
            ge, _ = _gelu(gl_ref[rows, :])
            y_lru = (hf_ref[rows, :] + hb_ref[rows, :]) * ge
            z, _, _, _ = _sgu(v_ref[rows, :], sg_ref[...], sw_ref, sb_ref)
            ys.append(jnp.concatenate([y_lru, u_ref[rows, :] * z], axis=1).astype(BF16))
        o = _dot(jnp.concatenate(ys, axis=0), w_v[...])
        o_ref[...] = o
        for k in range(SUB):
            rows = pl.ds(k * TM, TM)
            out_ref[rows, :] = h_ref[rows, :] + _sel(_kind(SUB * i + k), mod_ref, 2) * o[k * TM:(k + 1) * TM]

    row = pl.BlockSpec((TMX, D), lambda i: (i, 0))
    half = pl.BlockSpec((TMX, LRU_W), lambda i: (i, 0))
    return pl.pallas_call(
        body, name=name, grid=(t // TMX,),
        in_specs=[row, _pcols(1, TMX), _pcols(2, TMX), _pcols(3, TMX), half, half, pl.BlockSpec((2, 3, D), lambda i: (0, 0, 0)),
                  pl.BlockSpec((1, MLP_W), lambda i: (0, 0)), pl.BlockSpec((GROUPS, CHUNK, CHUNK), lambda i: (0, 0, 0)),
                  pl.BlockSpec((CHUNK, GROUPS), lambda i: (0, 0)), ANY],
        out_specs=[row, row],
        out_shape=[jax.ShapeDtypeStruct((t, D), F32), jax.ShapeDtypeStruct((t, D), F32)],
        scratch_shapes=[pltpu.VMEM((D, D), BF16)],
        compiler_params=_cp(),
    )(h, p, p, p, hf, hb, mod, sgu_g, sgu_w, sgu_bt, womix)


def mix_out_bwd(dy, p, hf, hb, o, mod, sgu_g, sgu_w, sgu_bt, womix, name, rider=None):
    t = dy.shape[0]

    def body(dy_ref, gl_ref, u_ref, v_ref, hf_ref, hb_ref, o_ref, mod_ref, sg_ref, sw_ref, sb_ref, w_hbm,
             dhs_ref, dp_ref, dw_ref, dgate_ref, dsg_ref, dsw_ref, dsb_ref, w_v):
        i = pl.program_id(0)

        @pl.when(i == 0)
        def _():
            pltpu.sync_copy(w_hbm, w_v)
            for ref in (dw_ref, dgate_ref, dsg_ref, dsw_ref, dsb_ref):
                ref[...] = jnp.zeros_like(ref)

        ys, dobs = [], []
        for k in range(SUB):
            rows = pl.ds(k * TM, TM)
            ic = _kind(SUB * i + k)
            dy_ = dy_ref[rows, :]
            _acc2(dgate_ref, 0, dy_ * o_ref[rows, :], ic)
            dob = (_sel(ic, mod_ref, 2) * dy_).astype(BF16)

            gl = gl_ref[rows, :]
            ge, th = _gelu(gl)
            hsum = hf_ref[rows, :] + hb_ref[rows, :]
            gain = sg_ref[...]
            uu = u_ref[rows, :]
            z, vhat, rs, vnb = _sgu(v_ref[rows, :], gain, sw_ref, sb_ref)
            ys.append(jnp.concatenate([hsum * ge, uu * z], axis=1).astype(BF16))
            dobs.append(dob)
            dyy = _dot_nt(dob, w_v[...])
            dyl, dys = dyy[:, :LRU_W], dyy[:, LRU_W:]

            dhs_ref[rows, :] = dyl * ge
            dge = 0.5 * (1.0 + th) + 0.5 * gl * (1.0 - th * th) * (GELU_C * (1.0 + 3.0 * GELU_A * gl * gl))
            dp_ref[rows, 0:LRU_W] = (dyl * hsum * dge).astype(BF16)
            dp_ref[rows, LRU_W:2 * LRU_W] = (dys * z).astype(BF16)

            dz = dys * uu
            dzb = dz.astype(BF16)
            dvn_chunks, dsb_cols = [], [jnp.zeros((CHUNK, 1), F32)] * GROUPS
            for ch in range(TM // CHUNK):
                cols = []
                for gi in range(GROUPS):
                    rs_, cs_ = slice(ch * CHUNK, (ch + 1) * CHUNK), slice(gi * GROUP_DIM, (gi + 1) * GROUP_DIM)
                    dzg = dzb[rs_, cs_]
                    dsb_cols[gi] = dsb_cols[gi] + jnp.sum(dz[rs_, cs_], axis=1, keepdims=True)
                    dsw_ref[gi] += _dot_nt(dzg, vnb[rs_, cs_])
                    cols.append(_dot_tn(sw_ref[gi].astype(BF16), dzg))
                dvn_chunks.append(jnp.concatenate(cols, axis=1))
            dsb_ref[...] += jnp.concatenate(dsb_cols, axis=1)
            dvn = jnp.concatenate(dvn_chunks, axis=0)
            dsg_ref[...] += jnp.sum(dvn * vhat, axis=0, keepdims=True)
            dvh = dvn * gain
            dv = rs * (dvh - jnp.mean(dvh, axis=-1, keepdims=True) - vhat * jnp.mean(dvh * vhat, axis=-1, keepdims=True))
            dp_ref[rows, 2 * LRU_W:3 * LRU_W] = dv.astype(BF16)
        dw_ref[...] += _dot_tn(jnp.concatenate(ys, axis=0), jnp.concatenate(dobs, axis=0))

    row = pl.BlockSpec((TMX, D), lambda i: (i, 0))
    half = pl.BlockSpec((TMX, LRU_W), lambda i: (i, 0))
    const2 = lambda i: (0, 0)
    const3 = lambda i: (0, 0, 0)
    return _grid_call(
        body, name=name, nsteps=t // TMX,
        in_specs=[row, _pcols(1, TMX), _pcols(2, TMX), _pcols(3, TMX), half, half, row, pl.BlockSpec((2, 3, D), const3),
                  pl.BlockSpec((1, MLP_W), const2), pl.BlockSpec((GROUPS, CHUNK, CHUNK), const3),
                  pl.BlockSpec((CHUNK, GROUPS), const2), ANY],
        out_specs=[half, pl.BlockSpec((TMX, 3 * LRU_W), lambda i: (i, 0)), pl.BlockSpec((D, D), const2),
                   pl.BlockSpec((2, 1, D), const3), pl.BlockSpec((1, MLP_W), const2),
                   pl.BlockSpec((GROUPS, CHUNK, CHUNK), const3), pl.BlockSpec((CHUNK, GROUPS), const2)],
        out_shape=[jax.ShapeDtypeStruct((t, LRU_W), F32), jax.ShapeDtypeStruct((t, 3 * LRU_W), BF16),
                   jax.ShapeDtypeStruct((D, D), F32), jax.ShapeDtypeStruct((2, 1, D), F32),
                   jax.ShapeDtypeStruct((1, MLP_W), F32), jax.ShapeDtypeStruct((GROUPS, CHUNK, CHUNK), F32),
                   jax.ShapeDtypeStruct((CHUNK, GROUPS), F32)],
        scratch_shapes=[pltpu.VMEM((D, D), BF16)],
        args=(dy, p, p, p, hf, hb, o, mod, sgu_g, sgu_w, sgu_bt, womix), rider=rider)


def mix_in_bwd(dy, h, p, dxf, dxb, dprest, mod, g, conv_w, wmix, name, rider=None):
    t = dy.shape[0]
    nt = t // TM

    def body(dy_ref, h_ref, x_ref, xp_ref, xn_ref, f_ref, fp_ref, fn_ref, b_ref, bp_ref, bn_ref, dpr_ref, mod_ref,
             g_ref, cw_ref, w_hbm, dh_ref, dw_hbm, dmod_ref, dg_ref, dcw_ref, dcb_ref, w_v, dw_v):
        i = pl.program_id(0)

        @pl.when(i == 0)
        def _():
            pltpu.sync_copy(w_hbm, w_v)
            dw_v[...] = jnp.zeros_like(dw_v)
            for ref in (dmod_ref, dg_ref, dcw_ref, dcb_ref):
                ref[...] = jnp.zeros_like(ref)

        def around(ref, before, after, k):
            prev = ref[pl.ds(k * TM - HALO, HALO), :] if k > 0 else before[...]
            nxt = ref[pl.ds((k + 1) * TM, HALO), :] if k < SUB - 1 else after[...]
            return ref[pl.ds(k * TM, TM), :], prev, nxt

        gain = g_ref[...]
        zs, dps, rs = [], [], []
        for k in range(SUB):
            f_m, f_p, f_n = around(f_ref, fp_ref, fn_ref, k)
            b_m, b_p, b_n = around(b_ref, bp_ref, bn_ref, k)
            dmain = f_m + b_m
            dext = _ext(SUB * i + k, nt, dmain, f_p + b_p, f_n + b_n)
            xext = _ext(SUB * i + k, nt, *around(x_ref, xp_ref, xn_ref, k))
            dxl = cw_ref[0:1, :] * _shifted(dext, 2)
            for tap in range(1, CONV_W):
                dxl = dxl + cw_ref[tap:tap + 1, :] * _shifted(dext, 2 - tap)
            dcw_ref[...] += jnp.concatenate(
                [jnp.sum(dmain * _shifted(xext, tap - 2), axis=0, keepdims=True) for tap in range(CONV_W)], axis=0)
            dcb_ref[...] += jnp.sum(dmain, axis=0, keepdims=True)

            ic = _kind(SUB * i + k)
            rows = pl.ds(k * TM, TM)
            z, _, r = _norm_mod(h_ref[rows, :], gain, _sel(ic, mod_ref, 0), _sel(ic, mod_ref, 1))
            zs.append(z.astype(BF16))
            rs.append(r)
            dps.append(jnp.concatenate([dxl.astype(BF16), dpr_ref[rows, :]], axis=1))
        zb = jnp.concatenate(zs, axis=0)
        dpb = jnp.concatenate(dps, axis=0)
        dz_all = jnp.zeros((TMX, D), F32)
        for dd in range(N_DEV):
            dpd = dpb[:, dd * N_MIX_SHARD:(dd + 1) * N_MIX_SHARD]
            dz_all = dz_all + _dot_nt(dpd, w_v[dd])
            dw_v[dd] += _dot_tn(zb, dpd)
        for k in range(SUB):
            ic = _kind(SUB * i + k)
            rows = pl.ds(k * TM, TM)
            scale = _sel(ic, mod_ref, 1)
            dz = dz_all[k * TM:(k + 1) * TM]
            n = h_ref[rows, :] * rs[k]
            _acc2(dmod_ref, 0, dz, ic)
            _acc2(dmod_ref, 1, dz * (n * gain), ic)
            dg_ref[...] += jnp.sum(dz * (1.0 + scale) * n, axis=0, keepdims=True)
            dh_ref[rows, :] = dy_ref[rows, :] + _norm_mod_bwd(dz, n, rs[k], gain, scale)

        @pl.when(i == nt // SUB - 1)
        def _():
            pltpu.sync_copy(dw_v, dw_hbm)

    main, prev, nxt = _halo_specs(nt // SUB, lambda s: s, TMX)
    row = pl.BlockSpec((TMX, D), lambda i: (i, 0))
    const2 = lambda i: (0, 0)
    return _grid_call(
        body, name=name, nsteps=nt // SUB,
        in_specs=[row, row, main, prev, nxt, main, prev, nxt, main, prev, nxt,
                  pl.BlockSpec((TMX, 3 * LRU_W), lambda i: (i, 0)), pl.BlockSpec((2, 3, D), lambda i: (0, 0, 0)),
                  pl.BlockSpec((1, D), const2), pl.BlockSpec((CONV_W, LRU_W), const2), ANY],
        out_specs=[row, ANY, pl.BlockSpec((2, 2, D), lambda i: (0, 0, 0)), pl.BlockSpec((1, D), const2),
                   pl.BlockSpec((CONV_W, LRU_W), const2), pl.BlockSpec((1, LRU_W), const2)],
        out_shape=[jax.ShapeDtypeStruct((t, D), F32), jax.ShapeDtypeStruct((N_DEV, D, N_MIX_SHARD), F32),
                   jax.ShapeDtypeStruct((2, 2, D), F32), jax.ShapeDtypeStruct((1, D), F32),
                   jax.ShapeDtypeStruct((CONV_W, LRU_W), F32), jax.ShapeDtypeStruct((1, LRU_W), F32)],
        scratch_shapes=[pltpu.VMEM((N_DEV, D, N_MIX_SHARD), BF16), pltpu.VMEM((N_DEV, D, N_MIX_SHARD), F32)],
        args=(dy, h, p, p, p, dxf, dxf, dxf, dxb, dxb, dxb, dprest, mod, g, conv_w, wmix), rider=rider)


def _block_diag(w):
    eye = jnp.eye(HEADS, dtype=w.dtype)
    return jnp.einsum("dhij,hk->dhikj", w, eye).reshape(2, LRU_W, LRU_W)


def small_layer(g1, gm, g2, conv_w, conv_b, w_r, b_r, w_i, b_i, lam, sgu_g, sgu_w, sgu_b):
    return dict(g1=g1[None, :], gm=gm[None, :], g2=g2[None, :], conv_w=conv_w, conv_b=conv_b[None, :],
                wr=_block_diag(w_r).astype(BF16), br=b_r[:, None, :], wi=_block_diag(w_i).astype(BF16),
                bi=b_i[:, None, :], lam=lam[:, None, :], sgu_g=sgu_g[None, :], sgu_w=sgu_w, sgu_bt=sgu_b.T)


def small_grads(g):
    out = dict(mix_norm_g=g["gm"][0], ffn2_norm_g=g["g2"][0], lru_conv_w=g["conv_w"],
               lru_conv_b=g["conv_b"][0], lru_w_r=g["wr"], lru_b_r=g["br"][:, 0, :],
               lru_w_i=g["wi"], lru_b_i=g["bi"][:, 0, :], lru_lambda=g["lam"][:, 0, :],
               sgu_norm_g=g["sgu_g"][0], sgu_w=g["sgu_w"], sgu_b=g["sgu_bt"].T)
    if "g1" in g:
        out["ffn1_norm_g"] = g["g1"][0]
    return out


def _as_blocks(key, g):
    return g if key == "wmix" else g.reshape(N_DEV, g.shape[0] // N_DEV, D)


def _gathered(key, a):
    return a if key == "wmix" else a.reshape(N_DEV * a.shape[1], D)


class _ReduceScatter:
    def __init__(self, c_idx, where):
        self.c_idx, self.where = c_idx, where
        self.out = {}

    def pair(self, group):
        return pair_rider([g for _, g in group])

    def after_pair(self, group, recv1, tag):
        return chips_rider(pair_sum([g for _, g in group], list(recv1), self.c_idx, f"pair_sum_{tag}"))

    def after_chips(self, group, recv1, recv2, tag):
        for (key, g), r1, r2 in zip(group, recv1, recv2):
            self.out[key] = (g, r1, r2)


FIRST_WEIGHTS = [(0, "win1"), (0, "wout1")]


def fwd_bwd(ctx_rows, x_rows, target, mods, shards, first_weights, smalls, final_g, c_idx, where):
    assert CTX == TM and len(shards) == 2

    def gather(keys_by_layer):
        return GatherRider([shards[l][k] for l, k in keys_by_layer])

    def put(full, keys_by_layer, got):
        for (l, k), a in zip(keys_by_layer, got):
            full[l][k] = _gathered(k, a)

    full = [dict(s) for s in smalls]
    put(full, FIRST_WEIGHTS, first_weights)
    riders = {
        "ffn1_fwd_0": [(0, "wmix"), (0, "womix"), (0, "win2")],
        "lru_fwd_0_0": [(0, "wout2")],
        "ffn2_fwd_0": [(1, "win1"), (1, "wout1")],
        "ffn1_fwd_1": [(1, "wmix"), (1, "womix"), (1, "win2")],
        "lru_fwd_1_0": [(1, "wout2")],
    }

    def ffn(which, l, h):
        name = f"ffn{which}_fwd_{l}"
        w = full[l]
        keys = riders.get(name)
        m = mods[l][:, 0:3] if which == 1 else mods[l][:, 6:9]
        last = (which, l) == (2, 1)
        outs, got = ffn_fwd(h, m, w[f"g{which}"], w[f"win{which}"], w[f"wout{which}"], name,
                            rider=gather(keys) if keys else None, loss=(final_g, target) if last else None)
        if keys:
            put(full, keys, got)
        return outs

    saved = []
    h = (ctx_rows, x_rows)
    for l in range(2):
        mm = mods[l][:, 3:6]
        outs = ffn(1, l, h)
        h1, gu1, acc1 = outs[:3]
        hin = outs[3] if l == 0 else h
        w = full[l]
        p = mix_in_fwd(h1, mm, w["gm"], w["wmix"], f"mix_in_fwd_{l}")
        hs = []
        for d in range(2):
            keys = riders.get(f"lru_fwd_{l}_{d}")
            hd, got = lru_fwd(p, w["conv_w"], w["conv_b"], w["wr"][d], w["br"][d], w["wi"][d], w["bi"][d], w["lam"][d],
                              bool(d), f"lru_fwd_{l}_{d}", rider=gather(keys) if keys else None)
            if keys:
                put(full, keys, got)
            hs.append(hd)
        h2, o = mix_out_fwd(h1, p, hs[0], hs[1], mm, w["sgu_g"], w["sgu_w"], w["sgu_bt"], w["womix"], f"mix_out_fwd_{l}")
        outs = ffn(2, l, h2)
        h, gu2, acc2 = outs[:3]
        saved.append((hin, h1, h2, gu1, acc1, p, hs, o, gu2, acc2))
    dh, (loss, dgf) = h, outs[3:]

    rs = _ReduceScatter(c_idx, where)
    grads, dmods, sums = [None, None], [None, None], [None, None]
    pending = None
    for l in (1, 0):
        w = full[l]
        m1, mm, m2 = mods[l][:, 0:3], mods[l][:, 3:6], mods[l][:, 6:9]
        hin, h1, h2, gu1, acc1, p, hs, o, gu2, acc2 = saved[l]
        g = {}
        rs.out = {}
        both = Riders([rs.pair(pending[0]), GatherRider([small_pack])]) if pending else None
        (dp2, g["wout2"], dgate2), got = ffn_bwd_a(dh, acc2, gu2, m2, w["wout2"], f"ffn2_bwd_a_{l}", rider=both)
        if pending:
            r1, (small_all,) = both.split(got)
        chips = rs.after_pair(pending[0], r1, pending[1]) if pending else None
        (dh, g["win2"], dmod2, g["g2"]), r2 = ffn_bwd_b(dh, h2, dp2, m2, w["g2"], w["win2"], f"ffn2_bwd_b_{l}", rider=chips)
        if pending:
            rs.after_chips(pending[0], r1, r2, pending[1])
            sums[l + 1].update(rs.out)
            rs.out = {}

        grp = [(k, _as_blocks(k, g[k])) for k in ("win2", "wout2")]
        (dhs, dprest, g["womix"], dgatem, g["sgu_g"], g["sgu_w"], g["sgu_bt"]), r1 = mix_out_bwd(
            dh, p, hs[0], hs[1], o, mm, w["sgu_g"], w["sgu_w"], w["sgu_bt"], w["womix"], f"mix_out_bwd_{l}",
            rider=rs.pair(grp))
        chips = rs.after_pair(grp, r1, f"a{l}")
        dx, per_dir = [], []
        for d in range(2):
            out = lru_bwd(p, hs[d], dhs, w["conv_w"], w["conv_b"], w["wr"][d], w["br"][d], w["wi"][d], w["bi"][d],
                          w["lam"][d], bool(d), f"lru_bwd_{l}_{d}")
            dx.append(out[0])
            per_dir.append(out[1:])
        for k, nm in enumerate(("wr", "wi", "br", "bi", "lam")):
            g[nm] = jnp.stack([per_dir[0][k], per_dir[1][k]])
        (dh, g["wmix"], dmodm, g["gm"], g["conv_w"], g["conv_b"]), r2 = mix_in_bwd(
            dh, h1, p, dx[0], dx[1], dprest, mm, w["gm"], w["conv_w"], w["wmix"], f"mix_in_bwd_{l}", rider=chips)
        rs.after_chips(grp, r1, r2, f"a{l}")
        sums[l] = dict(rs.out)
        rs.out = {}

        if l == 1:
            (dp1, g["wout1"], dgate1), _ = ffn_bwd_a(dh, acc1, gu1, m1, w["wout1"], f"ffn1_bwd_a_{l}")
            (dh, g["win1"], dmod1, g["g1"]), _ = ffn_bwd_b(dh, hin, dp1, m1, w["g1"], w["win1"], f"ffn1_bwd_b_{l}")
            pending = ([(k, _as_blocks(k, g[k])) for k in ("womix", "wmix", "wout1", "win1")], f"b{l}")
            per = small_grads(g)
            small_pack = _pack([per[n] for n in LAYER_SMALL])
        else:
            g_mix = [(k, _as_blocks(k, g[k])) for k in ("womix", "wmix")]
            (dp1, g["wout1"], dgate1), r1_mix = ffn_bwd_a(dh, acc1, gu1, m1, w["wout1"], f"ffn1_bwd_a_{l}",
                                                          rider=rs.pair(g_mix))
            g_out = [("wout1", _as_blocks("wout1", g["wout1"]))]
            per = small_grads(g)
            three = Riders([rs.after_pair(g_mix, r1_mix, f"b{l}"), rs.pair(g_out),
                            GatherRider([_pack([per[n] for n in LAYER_SMALL[1:]])])])
            (g["win1"],), got = ffn_bwd_dw(hin, dp1, m1, w["g1"], f"ffn1_bwd_dw_{l}", rider=three)
            r2_mix, r1_out, (small0_all,) = three.split(got)
            rs.after_chips(g_mix, r1_mix, r2_mix, f"b{l}")
            g_in = [("win1", _as_blocks("win1", g["win1"]))]
            both = Riders([rs.after_pair(g_out, r1_out, f"c{l}"), rs.pair(g_in)])
            nt = dh.shape[0] // TM
            cut0, cut1 = (3 * nt) // 8, nt - max(nt // 16, 1)
            part, got = ffn_bwd_dh(dh, hin, dp1, m1, w["g1"], w["win1"], f"ffn1_bwd_dh0_{l}", (0, cut0), rider=both)
            r2_out, r1_in = both.split(got)
            rs.after_chips(g_out, r1_out, r2_out, f"c{l}")
            part, r2_in = ffn_bwd_dh(dh, hin, dp1, m1, w["g1"], w["win1"], f"ffn1_bwd_dh1_{l}", (cut0, cut1), carry=part,
                                     rider=rs.after_pair(g_in, r1_in, f"d{l}"))
            rs.after_chips(g_in, r1_in, r2_in, f"d{l}")
            (dh, dmod1, g["g1"]), _ = ffn_bwd_dh(dh, hin, dp1, m1, w["g1"], w["win1"], f"ffn1_bwd_dh2_{l}", (cut1, nt),
                                                 carry=part)
            sums[l].update(rs.out)
        dmods[l] = jnp.concatenate([dmod1, dgate1, dmodm, dgatem, dmod2, dgate2], axis=1)
        grads[l] = g
    return loss, dh, jnp.stack(dmods), grads, (small0_all, small_all), sums, dgf


def _row_block(r, c, limit=262144):
    best = 8
    for rb in range(8, r + 1, 8):
        if r % rb == 0 and rb * c <= limit:
            best = rb
    return best


PAIR_SUM_SPLIT = 2


def pair_sum(grads, recv, c_idx, name):
    n = len(grads)

    def body(c_ref, *refs):
        for t in range(n):
            refs[2 * n + t][...] = (refs[t][...] + refs[n + t][...]).astype(BF16)

    mine, theirs, outs = [], [], []
    for g in grads:
        _, r, c = g.shape
        rb = r // PAIR_SUM_SPLIT
        assert rb % 16 == 0
        mine.append(pl.BlockSpec((1, rb, c), lambda j, i, c_ref: (2 * j + c_ref[0], i, 0)))
        theirs.append(pl.BlockSpec((1, rb, c), lambda j, i, c_ref: (j, i, 0)))
        outs.append(jax.ShapeDtypeStruct((4, r, c), BF16))
    return pl.pallas_call(
        body, name=name,
        grid_spec=pltpu.PrefetchScalarGridSpec(num_scalar_prefetch=1, grid=(4, PAIR_SUM_SPLIT), in_specs=mine + theirs,
                                               out_specs=list(theirs)),
        out_shape=outs, compiler_params=_cp(2),
    )(c_idx, *grads, *recv)


ADA_ROWS = 16


def _silu(v):
    return v * _sigmoid(v)


def ada_fwd(cond, w_ada, b_slab, name):
    def body(c_ref, w_ref, b_ref, o_ref):
        s = _silu(c_ref[...]).astype(BF16)
        o_ref[0] = _dot(s, w_ref[0].astype(BF16)) + b_ref[0]

    return pl.pallas_call(
        body, name=name, grid=(DEPTH,),
        in_specs=[pl.BlockSpec((ADA_ROWS, D), lambda l: (0, 0)), pl.BlockSpec((1, D, ADA_SHARD), lambda l: (l, 0, 0)),
                  pl.BlockSpec((1, 1, ADA_SHARD), lambda l: (l, 0, 0))],
        out_specs=pl.BlockSpec((1, ADA_ROWS, ADA_SHARD), lambda l: (l, 0, 0)),
        out_shape=jax.ShapeDtypeStruct((DEPTH, ADA_ROWS, ADA_SHARD), F32),
        compiler_params=_cp(),
    )(cond, w_ada, b_slab)


def ada_bwd(cond, dm_sample, dm_ctx, w_ada, name):
    def body(c_ref, ds_ref, dc_ref, w_ref, gw_ref, dsc_ref):
        @pl.when(pl.program_id(0) == 0)
        def _():
            dsc_ref[...] = jnp.zeros_like(dsc_ref)

        s = _silu(c_ref[...]).astype(BF16)
        dcs = dc_ref[0]
        tot = dcs[0:1]
        for j in range(1, N_DEV):
            tot = tot + dcs[j:j + 1]
        tot8 = jnp.where(lax.broadcasted_iota(jnp.int32, (N_DEV, ADA_SHARD), 0) == 0, tot, 0.0)
        dm = jnp.concatenate([ds_ref[0], tot8], axis=0).astype(BF16)
        gw_ref[0] = _dot_tn(s, dm)
        dsc_ref[...] += _dot_nt(dm, w_ref[0].astype(BF16))[N_DEV:N_DEV + 1]

    slab = pl.BlockSpec((1, N_DEV, ADA_SHARD), lambda l: (l, 0, 0))
    wspec = pl.BlockSpec((1, D, ADA_SHARD), lambda l: (l, 0, 0))
    return pl.pallas_call(
        body, name=name, grid=(DEPTH,),
        in_specs=[pl.BlockSpec((ADA_ROWS, D), lambda l: (0, 0)), slab, slab, wspec],
        out_specs=[wspec, pl.BlockSpec((1, D), lambda l: (0, 0))],
        out_shape=[jax.ShapeDtypeStruct((DEPTH, D, ADA_SHARD), F32), jax.ShapeDtypeStruct((1, D), F32)],
        compiler_params=_cp(),
    )(cond, dm_sample, dm_ctx, w_ada)


def sum_over_devices(parts, name, silu_rows=0, w=None):
    _, r, c = parts.shape

    def body(*refs):
        p_ref, o_ref = refs[0], refs[-1]
        tot = p_ref[0]
        for j in range(1, N_DEV):
            tot = tot + p_ref[j]
        o_ref[...] = tot
        if silu_rows:
            wv = refs[1][...]
            s = _sigmoid(wv)
            o_ref[0:silu_rows, :] = tot[0:silu_rows, :] * (s * (1.0 + wv * (1.0 - s)))

    vm = pl.BlockSpec(memory_space=pltpu.VMEM)
    args = (parts,) if w is None else (parts, w)
    return pl.pallas_call(
        body, name=name, in_specs=[vm] * len(args), out_specs=vm,
        out_shape=jax.ShapeDtypeStruct((r, c), F32),
        compiler_params=pltpu.CompilerParams(vmem_limit_bytes=VMEM_LIMIT),
    )(*args)


def sum_dmods(dm_all, name):
    def body(d_ref, o_ref):
        for l in range(DEPTH):
            tot = d_ref[0, l]
            for j in range(1, N_DEV):
                tot = tot + d_ref[j, l]
            o_ref[l:l + 1, :] = tot[0:1] + tot[1:2]

    vm = pl.BlockSpec(memory_space=pltpu.VMEM)
    return pl.pallas_call(
        body, name=name, in_specs=[vm], out_specs=vm,
        out_shape=jax.ShapeDtypeStruct((DEPTH, N_MOD * D), F32),
    )(dm_all)


ADAMW_BLOCK = 512 * 1024


def adamw(w, g, m, v, name, rider=None):
    r, c = w.shape
    rb = _row_block(r, c, limit=ADAMW_BLOCK)

    def body(w_ref, g_ref, m_ref, v_ref, d_ref, nm_ref, nv_ref):
        g_ = g_ref[...]
        nm = B1 * m_ref[...] + (1.0 - B1) * g_
        nv = B2 * v_ref[...] + (1.0 - B2) * (g_ * g_)
        nm_ref[...] = nm
        nv_ref[...] = nv
        m_hat = nm / (1.0 - B1 ** STEP)
        v_hat = nv / (1.0 - B2 ** STEP)
        d_ref[...] = -LR * (m_hat / (jnp.sqrt(v_hat) + ADAM_EPS) + WD * w_ref[...])

    blk = pl.BlockSpec((rb, c), lambda i: (i, 0))
    shp = jax.ShapeDtypeStruct((r, c), F32)
    return _grid_call(body, name=name, nsteps=r // rb, in_specs=[blk] * 4, out_specs=[blk] * 3, out_shape=[shp] * 3,
                      scratch_shapes=[], args=(w, g, m, v), rider=rider)


def _adamw_math(w, g, m, v):
    nm = B1 * m + (1.0 - B1) * g
    nv = B2 * v + (1.0 - B2) * (g * g)
    m_hat = nm / (1.0 - B1 ** STEP)
    v_hat = nv / (1.0 - B2 ** STEP)
    return -LR * (m_hat / (jnp.sqrt(v_hat) + ADAM_EPS) + WD * w), nm, nv


def adamw_layers(w, parts, m, v, where, name):
    _, r, c = w.shape
    assert len(parts) == DEPTH == 2 and parts[0][0].shape == (N_DEV, r, c)
    rb = _row_block(r, c, limit=ADAMW_BLOCK // 2)
    nb = r // rb

    def body(where_ref, w_ref, a0, b0, c0, a1, b1, c1, m_ref, v_ref, go_ref, d_ref, nm_ref, nv_ref):
        def total(mine, pair, far):
            return (mine[0] + pair[0]) + ((far[0].astype(F32) + far[1].astype(F32)) + far[2].astype(F32))

        g = jnp.where(pl.program_id(0) == 0, total(a0, b0, c0), total(a1, b1, c1))
        go_ref[0], d_ref[0], nm_ref[0], nv_ref[0] = (g,) + _adamw_math(w_ref[0], g, m_ref[0], v_ref[0])

    blk = pl.BlockSpec((1, rb, c), lambda l, i, wr: (l, i, 0))

    def layer_specs(layer):
        row = (lambda l, i: jnp.where(l == 0, i, nb - 1)) if layer == 0 else (lambda l, i: jnp.where(l == 0, 0, i))
        return [pl.BlockSpec((1, rb, c), lambda l, i, wr: (wr[0], row(l, i), 0)),
                pl.BlockSpec((1, rb, c), lambda l, i, wr: (wr[1], row(l, i), 0)),
                pl.BlockSpec((3, rb, c), lambda l, i, wr: (0, row(l, i), 0))]

    shp = jax.ShapeDtypeStruct(w.shape, F32)
    return pl.pallas_call(
        body, name=name,
        grid_spec=pltpu.PrefetchScalarGridSpec(
            num_scalar_prefetch=1, grid=(DEPTH, nb),
            in_specs=[blk] + layer_specs(0) + layer_specs(1) + [blk, blk], out_specs=[blk] * 4),
        out_shape=[shp] * 4, compiler_params=_cp(2),
    )(where, w, *parts[0], *parts[1], m, v)


def _adamw_nd(w, g, m, v, name, rider=None):
    shape = w.shape
    flat = lambda a: a.reshape(-1, shape[-1])
    outs, got = adamw(flat(w), flat(g), flat(m), flat(v), name, rider=rider)
    return tuple(o.reshape(shape) for o in outs), got


LANES = 128


PACK_UNIT = 8 * LANES


ADAMW_SMALL_ROWS = 512


def _pack(arrays, row_multiple=8):
    pieces, n = [], 0
    for a in arrays:
        pieces.append(a.reshape(-1).astype(F32))
        pad = (-a.size) % PACK_UNIT
        if pad:
            pieces.append(jnp.zeros((pad,), F32))
        n += a.size + pad
    tail = (-n) % (row_multiple * LANES)
    if tail:
        pieces.append(jnp.zeros((tail,), F32))
    return jnp.concatenate(pieces).reshape(-1, LANES)


def _unpack(packed, shapes):
    out, r0 = [], 0
    lead = packed.shape[:-2]
    for shp in shapes:
        size = 1
        for s in shp:
            size *= s
        nr = 8 * -(-size // PACK_UNIT)
        blk = packed[..., r0:r0 + nr, :].reshape(lead + (nr * LANES,))[..., :size]
        out.append(blk.reshape(lead + tuple(shp)))
        r0 += nr
    return out


WEIGHTS = ["c_ctx", "w_ada", "b_ada", "ffn1_norm_g", "ffn1_w_in", "ffn1_w_out", "mix_norm_g", "w_in_mix", "lru_conv_w",
           "lru_conv_b", "lru_w_r", "lru_b_r", "lru_w_i", "lru_b_i", "lru_lambda", "sgu_norm_g", "sgu_w", "sgu_b",
           "w_out_mix", "ffn2_norm_g", "ffn2_w_in", "ffn2_w_out", "final_norm_g"]
BIG = ["w_ada", "ffn1_w_in", "ffn1_w_out", "w_in_mix", "w_out_mix", "ffn2_w_in", "ffn2_w_out"]
SHARDED_SMALL = ["lru_conv_w", "lru_b_r", "lru_b_i", "lru_lambda"]
LAYER_SMALL = ["ffn1_norm_g", "mix_norm_g", "ffn2_norm_g", "lru_conv_w", "lru_conv_b", "lru_w_r", "lru_b_r", "lru_w_i",
               "lru_b_i", "lru_lambda", "sgu_norm_g", "sgu_w", "sgu_b"]
LRU_SHARD = LRU_W // N_DEV


def _widen(a):
    return jnp.moveaxis(a, 0, -2).reshape(a.shape[1:-1] + (LRU_W,))


def kernel(x, c, ctx, c_ctx, w_ada, b_ada, ffn1_norm_g, ffn1_w_in, ffn1_w_out, mix_norm_g, w_in_mix, lru_conv_w, lru_conv_b, lru_w_r, lru_b_r, lru_w_i, lru_b_i, lru_lambda, sgu_norm_g, sgu_w, sgu_b, w_out_mix, ffn2_norm_g, ffn2_w_in, ffn2_w_out, final_norm_g, loss_target, m_c_ctx, m_w_ada, m_b_ada, m_ffn1_norm_g, m_ffn1_w_in, m_ffn1_w_out, m_mix_norm_g, m_w_in_mix, m_lru_conv_w, m_lru_conv_b, m_lru_w_r, m_lru_b_r, m_lru_w_i, m_lru_b_i, m_lru_lambda, m_sgu_norm_g, m_sgu_w, m_sgu_b, m_w_out_mix, m_ffn2_norm_g, m_ffn2_w_in, m_ffn2_w_out, m_final_norm_g, v_c_ctx, v_w_ada, v_b_ada, v_ffn1_norm_g, v_ffn1_w_in, v_ffn1_w_out, v_mix_norm_g, v_w_in_mix, v_lru_conv_w, v_lru_conv_b, v_lru_w_r, v_lru_b_r, v_lru_w_i, v_lru_b_i, v_lru_lambda, v_sgu_norm_g, v_sgu_w, v_sgu_b, v_w_out_mix, v_ffn2_norm_g, v_ffn2_w_in, v_ffn2_w_out, v_final_norm_g):
    given = dict(locals())
    W = {n: given[n] for n in WEIGHTS}
    M = {n: given["m_" + n] for n in WEIGHTS}
    V = {n: given["v_" + n] for n in WEIGHTS}
    xi, yi, ci = _position()
    me = 4 * xi + 2 * yi + ci
    chip = 2 * xi + yi

    shards = []
    tr = lambda a: jnp.swapaxes(a, 1, 2)
    for l in range(DEPTH):
        sh = dict(win1=tr(ffn1_w_in)[l], wout1=ffn1_w_out[l], wmix=w_in_mix[l], womix=w_out_mix[l], win2=tr(ffn2_w_in)[l],
                  wout2=ffn2_w_out[l])
        shards.append({k: a.astype(BF16) for k, a in sh.items()})

    sharded_shapes = [W[n].shape for n in SHARDED_SMALL]
    both = Riders([GatherRider([_pack([c[0]] + [W[n] for n in SHARDED_SMALL])]),
                   GatherRider([shards[0][k] for _, k in FIRST_WEIGHTS])])
    (got,), first_weights = both.split(run_alone(both, pl.ANY, "gather_first"))
    parts = _unpack(got, [(D,)] + sharded_shapes)
    c_all = parts[0]
    wide = {n: _widen(a) for n, a in zip(SHARDED_SMALL, parts[1:])}
    cond = jnp.concatenate([c_all, c_ctx[None, :], jnp.zeros((ADA_ROWS - N_DEV - 1, D), F32)], axis=0)
    b_slab = lax.dynamic_slice_in_dim(b_ada, me * ADA_SHARD, ADA_SHARD, axis=1)[:, None, :]
    slabs = ada_fwd(cond, w_ada, b_slab, "ada_fwd")
    mall = run_alone(GatherRider([slabs.reshape(DEPTH * ADA_ROWS, ADA_SHARD)]), pltpu.VMEM, "gather_mod")[0]
    mall = mall.reshape(N_DEV, DEPTH, ADA_ROWS, ADA_SHARD)
    m_sample = lax.dynamic_index_in_dim(mall, me, axis=2, keepdims=False)
    m_ctx = mall[:, :, N_DEV, :]
    mods = jnp.stack([jnp.transpose(m, (1, 0, 2)).reshape(DEPTH, N_MOD, D) for m in (m_ctx, m_sample)], axis=1)

    smalls = []
    for l in range(DEPTH):
        smalls.append(small_layer(ffn1_norm_g[l], mix_norm_g[l], ffn2_norm_g[l], wide["lru_conv_w"][l], lru_conv_b[l],
                                  lru_w_r[l], wide["lru_b_r"][l], lru_w_i[l], wide["lru_b_i"][l], wide["lru_lambda"][l],
                                  sgu_norm_g[l], sgu_w[l], sgu_b[l]))

    c_idx = ci.reshape(1).astype(jnp.int32)
    where = jnp.stack([me, chip]).astype(jnp.int32)
    loss_blk, dx, dmods, grads, (small0_all, small1_all), gsum, dgf = fwd_bwd(
        ctx[0], x[0], loss_target[0], mods, shards, first_weights, smalls, final_norm_g[None, :], c_idx, where)
    smalls_shape = {n: (W[n].shape[1:-1] + (LRU_W,)) if n in SHARDED_SMALL else W[n].shape[1:] for n in LAYER_SMALL}
    G, delta, new_m, new_v = {}, {}, {}, {}
    for key, n in (("win1", "ffn1_w_in"), ("wout1", "ffn1_w_out"), ("wmix", "w_in_mix"), ("womix", "w_out_mix"),
                   ("win2", "ffn2_w_in"), ("wout2", "ffn2_w_out")):
        t_in = tr if key in ("win1", "win2") else (lambda a: a)
        outs = adamw_layers(t_in(W[n]), [gsum[l][key] for l in range(DEPTH)], t_in(M[n]), t_in(V[n]), where,
                            f"adamw_{n}")
        G[n], delta[n], new_m[n], new_v[n] = [t_in(o) for o in outs]

    n_rows = DEPTH * 2 * N_MOD
    dm_rows = jnp.concatenate([dmods.reshape(n_rows, D), jnp.zeros((-n_rows % 8, D), F32)], axis=0)
    dm_all = run_alone(GatherRider([dm_rows]), pltpu.VMEM, "gather_dmod")[0][:, :n_rows]
    dm_all = dm_all.reshape(N_DEV, DEPTH, 2, N_MOD * D)
    mine = lax.dynamic_slice_in_dim(dm_all, me * ADA_SHARD, ADA_SHARD, axis=3)
    G["w_ada"], dsc = ada_bwd(cond, jnp.transpose(mine[:, :, 1, :], (1, 0, 2)), jnp.transpose(mine[:, :, 0, :], (1, 0, 2)),
                              w_ada, "ada_bwd")
    G["b_ada"] = sum_dmods(dm_all, "sum_dmods")
    (delta["w_ada"], new_m["w_ada"], new_v["w_ada"]), _ = _adamw_nd(w_ada, G["w_ada"], m_w_ada, v_w_ada, "adamw_w_ada")

    head_all = run_alone(GatherRider([_pack([dsc[0], dgf[0], grads[0]["g1"][0], loss_blk])]), pltpu.VMEM,
                         "gather_head_grads")[0]
    head = _unpack(sum_over_devices(head_all, "sum_head_grads", silu_rows=D // LANES, w=c_ctx.reshape(D // LANES, LANES)),
                   [(D,), (D,), (D,), loss_blk.shape])
    shapes = [smalls_shape[n] for n in LAYER_SMALL]
    sum0 = [head[2]] + _unpack(sum_over_devices(small0_all, "sum_small_grads_0"), shapes[1:])
    sum1 = _unpack(sum_over_devices(small1_all, "sum_small_grads_1"), shapes)
    G["c_ctx"], G["final_norm_g"] = head[0], head[1]
    for n, a0, a1 in zip(LAYER_SMALL, sum0, sum1):
        a = jnp.stack([a0, a1])
        G[n] = lax.dynamic_slice_in_dim(a, me * LRU_SHARD, LRU_SHARD, axis=a.ndim - 1) if n in SHARDED_SMALL else a

    rest = [n for n in WEIGHTS if n not in BIG]
    shapes = [W[n].shape for n in rest]
    outs, _ = adamw(*[_pack([src[n] for n in rest], row_multiple=ADAMW_SMALL_ROWS) for src in (W, G, M, V)], "adamw_small")
    for dst, packed in zip((delta, new_m, new_v), outs):
        for n, a in zip(rest, _unpack(packed, shapes)):
            dst[n] = a

    loss = head[3][0, 0]
    grad_x = dx[None]
    return (loss, grad_x, *[G[n] for n in WEIGHTS], *[delta[n] for n in WEIGHTS], *[new_m[n] for n in WEIGHTS],
            *[new_v[n] for n in WEIGHTS])
```

```python
import jax
import jax.numpy as jnp
from jax import lax
from jax.experimental import pallas as pl
from jax.experimental.pallas import tpu as pltpu

F32 = jnp.float32
BF16 = jnp.bfloat16

D = 1024
CTX = 256
DEPTH = 2
EPS = 1e-6
D_FF = 2816
LRU_W = 512
HEADS = 8
HEAD_DIM = 64
CONV_W = 4
RG_C = 8.0
GROUPS = 4
GROUP_DIM = 128
CHUNK = 128
MLP_W = 512
IN_PROJ = 2048
N_MOD = 9
N_DEV = 8

LR = 0.001
B1 = 0.9
B2 = 0.999
ADAM_EPS = 1e-08
WD = 0.01
STEP = 10

HT = 256
WT = 512
N_MIX_SHARD = IN_PROJ // N_DEV
ADA_SHARD = N_MOD * D // N_DEV

TM = 256
SUB = 3
TMX = SUB * TM
HALO = 8
VMEM_LIMIT = 60 * 1024 * 1024

MESH = pl.DeviceIdType.MESH
ANY = pl.BlockSpec(memory_space=pl.ANY)


def _cp(n_axes=1):
    return pltpu.CompilerParams(dimension_semantics=("arbitrary",) * n_axes, vmem_limit_bytes=VMEM_LIMIT)


def _position():
    return lax.axis_index("x"), lax.axis_index("y"), lax.axis_index("c")


class GatherRider:
    def __init__(self, shards):
        n = len(shards)
        self.n = n
        self.ins = list(shards)
        self.out_shape = [jax.ShapeDtypeStruct((N_DEV,) + s.shape, s.dtype) for s in shards]
        self.sems = [pltpu.SemaphoreType.DMA((n, 7)), pltpu.SemaphoreType.DMA((n, 7)), pltpu.SemaphoreType.DMA((n,))]

    def _ctx(self, outs, sems):
        x, y, c = _position()
        chips = [(1 - x, y), (x, 1 - y), (1 - x, 1 - y)]

        def copy(t, k, block, to, src=None):
            dst = outs[t].at[4 * block[0] + 2 * block[1] + block[2]]
            return pltpu.make_async_remote_copy(
                src_ref=dst if src is None else src, dst_ref=dst, send_sem=sems[0].at[t, k],
                recv_sem=sems[1].at[t, k], device_id=to, device_id_type=MESH)

        return (x, y, c), (x, y, 1 - c), chips, copy

    def _local(self, ins, outs, sems, t):
        x, y, c = _position()
        return pltpu.make_async_copy(ins[t], outs[t].at[4 * x + 2 * y + c], sems[2].at[t])

    def _first(self, ins, outs, sems, t):
        me, sibling, chips, copy = self._ctx(outs, sems)
        return [copy(t, 0, me, sibling, src=ins[t])] + [copy(t, 1 + j, me, (*chip, me[2]), src=ins[t])
                                                         for j, chip in enumerate(chips)]

    def start(self, ins, outs, sems):
        for t in range(self.n):
            self._local(ins, outs, sems, t).start()
            for cp in self._first(ins, outs, sems, t):
                cp.start()

    def mid(self, ins, outs, sems):
        me, sibling, chips, copy = self._ctx(outs, sems)
        for j, chip in enumerate(chips):
            for t in range(self.n):
                copy(t, 1 + j, (*chip, me[2]), me).wait_recv()
                copy(t, 4 + j, (*chip, me[2]), sibling).start()

    def finish(self, ins, outs, sems):
        me, sibling, chips, copy = self._ctx(outs, sems)
        for t in range(self.n):
            copy(t, 0, sibling, me).wait_recv()
            for j, chip in enumerate(chips):
                copy(t, 4 + j, (*chip, 1 - me[2]), me).wait_recv()
        for t in range(self.n):
            for cp in self._first(ins, outs, sems, t):
                cp.wait_send()
            for j, chip in enumerate(chips):
                copy(t, 4 + j, (*chip, me[2]), sibling).wait_send()
            self._local(ins, outs, sems, t).wait()


class ExchangeRider:
    def __init__(self, tensors, plan, n_slots):
        n = len(tensors)
        self.n, self.plan = n, plan
        self.ins = list(tensors)
        self.out_shape = [jax.ShapeDtypeStruct((n_slots,) + s.shape[1:], s.dtype) for s in tensors]
        self.sems = [pltpu.SemaphoreType.DMA((n, n_slots)), pltpu.SemaphoreType.DMA((n, n_slots))]

    def _copies(self, ins, outs, sems):
        return [pltpu.make_async_remote_copy(
            src_ref=ins[t].at[block], dst_ref=outs[t].at[k], send_sem=sems[0].at[t, k], recv_sem=sems[1].at[t, k],
            device_id=to, device_id_type=MESH)
            for t in range(self.n) for k, (block, to) in enumerate(self.plan(*_position()))]

    def start(self, ins, outs, sems):
        for cp in self._copies(ins, outs, sems):
            cp.start()

    def mid(self, ins, outs, sems):
        pass

    def finish(self, ins, outs, sems):
        for cp in self._copies(ins, outs, sems):
            cp.wait()


class Riders:
    def __init__(self, riders):
        self.riders = list(riders)
        self.ins = [a for r in self.riders for a in r.ins]
        self.out_shape = [s for r in self.riders for s in r.out_shape]
        self.sems = [s for r in self.riders for s in r.sems]

    def _each(self, ins, outs, sems):
        i = o = s = 0
        for r in self.riders:
            ni, no, ns = len(r.ins), len(r.out_shape), len(r.sems)
            yield r, ins[i:i + ni], outs[o:o + no], sems[s:s + ns]
            i, o, s = i + ni, o + no, s + ns

    def start(self, ins, outs, sems):
        for r, a, b, c in self._each(ins, outs, sems):
            r.start(a, b, c)

    def mid(self, ins, outs, sems):
        for r, a, b, c in self._each(ins, outs, sems):
            r.mid(a, b, c)

    def finish(self, ins, outs, sems):
        for r, a, b, c in self._each(ins, outs, sems):
            r.finish(a, b, c)

    def split(self, outs):
        res, o = [], 0
        for r in self.riders:
            res.append(list(outs[o:o + len(r.out_shape)]))
            o += len(r.out_shape)
        return res


def pair_rider(grads):
    def plan(x, y, c):
        return [(4 * cx + 2 * cy + (1 - c), (x, y, 1 - c)) for cx in range(2) for cy in range(2)]
    return ExchangeRider(grads, plan, 4)


def chips_rider(parts):
    def plan(x, y, c):
        return [(2 * cx + cy, (cx, cy, c)) for cx, cy in [(1 - x, y), (x, 1 - y), (1 - x, 1 - y)]]
    return ExchangeRider(parts, plan, 3)


def run_alone(rider, space, name):
    ni = len(rider.ins)
    no = len(rider.out_shape)

    def body(*refs):
        ins, outs, sems = refs[:ni], refs[ni:ni + no], refs[ni + no:]
        rider.start(ins, outs, sems)
        rider.mid(ins, outs, sems)
        rider.finish(ins, outs, sems)

    spec = pl.BlockSpec(memory_space=space)
    return pl.pallas_call(
        body, name=name, in_specs=[spec] * ni, out_specs=[spec] * no, out_shape=rider.out_shape,
        scratch_shapes=rider.sems, compiler_params=pltpu.CompilerParams(vmem_limit_bytes=VMEM_LIMIT),
    )(*rider.ins)


def _grid_call(body, *, name, nsteps, in_specs, out_specs, out_shape, scratch_shapes, args, rider=None, aliases=None):
    aliases = aliases or {}
    if rider is None:
        outs = pl.pallas_call(body, name=name, grid=(nsteps,), in_specs=in_specs, out_specs=out_specs,
                              out_shape=out_shape, scratch_shapes=scratch_shapes, input_output_aliases=aliases,
                              compiler_params=_cp())(*args)
        return outs, []
    ni, no, ns = len(in_specs), len(out_specs), len(scratch_shapes)
    ri, ro = len(rider.ins), len(rider.out_shape)

    def wrapped(*refs):
        ins, refs = refs[:ni], refs[ni:]
        r_ins, refs = refs[:ri], refs[ri:]
        outs, refs = refs[:no], refs[no:]
        r_outs, refs = refs[:ro], refs[ro:]
        scratch, r_sems = refs[:ns], refs[ns:]
        s = pl.program_id(0)

        @pl.when(s == 0)
        def _():
            rider.start(r_ins, r_outs, r_sems)

        body(*ins, *outs, *scratch)

        @pl.when(s == (3 * nsteps) // 4)
        def _():
            rider.mid(r_ins, r_outs, r_sems)

        @pl.when(s == nsteps - 1)
        def _():
            rider.finish(r_ins, r_outs, r_sems)

    outs = pl.pallas_call(
        wrapped, name=name, grid=(nsteps,), in_specs=list(in_specs) + [ANY] * ri, out_specs=list(out_specs) + [ANY] * ro,
        out_shape=list(out_shape) + rider.out_shape, scratch_shapes=list(scratch_shapes) + rider.sems,
        input_output_aliases=aliases, compiler_params=_cp())(*args, *rider.ins)
    return outs[:no], outs[no:]


def _dot(a, b):
    return jnp.dot(a, b, preferred_element_type=F32)


def _dot_nt(a, b):
    return lax.dot_general(a, b, (((1,), (1,)), ((), ())), preferred_element_type=F32)


def _dot_tn(a, b):
    return lax.dot_general(a, b, (((0,), (0,)), ((), ())), preferred_element_type=F32)


def _sigmoid(x):
    return 1.0 / (1.0 + jnp.exp(-x))


def _kind(i):
    return jnp.where(i < CTX // TM, 0, 1)


def _sel(kind, mod_ref, k):
    return mod_ref[kind, k:k + 1, :]


def _acc2(ref, k, val, kind):
    ref[kind, k:k + 1, :] += jnp.sum(val, axis=0, keepdims=True)


def _norm_mod(h, g, shift, scale):
    r = lax.rsqrt(jnp.mean(h * h, axis=-1, keepdims=True) + EPS)
    n = h * r
    return (n * g) * (1.0 + scale) + shift, n, r


def _norm_mod_bwd(dz, n, r, g, scale):
    dn = dz * (g * (1.0 + scale))
    return r * (dn - n * jnp.mean(dn * n, axis=-1, keepdims=True))


def ffn_fwd(h, mod, g, win, wout, name, rider=None, loss=None):
    split = isinstance(h, tuple)
    nc = CTX // TM
    t = h[0].shape[0] + h[1].shape[0] if split else h.shape[0]

    def body(*refs):
        refs = list(refs)
        win_v, wout_v, a_v = refs[-3:]
        rows = refs[:2] if split else refs[:1]
        mod_ref, g_ref, win_hbm, wout_hbm = refs[len(rows):len(rows) + 4]
        rest = refs[len(rows) + 4:-3]
        if loss is not None:
            fg_ref, tgt_ref, rest = rest[0], rest[1], rest[2:]
        out_ref, gu_ref, acc_ref, rest = rest[0], rest[1], rest[2], rest[3:]
        i = pl.program_id(0)

        @pl.when(i == 0)
        def _():
            pltpu.sync_copy(win_hbm, win_v)
            pltpu.sync_copy(wout_hbm, wout_v)

        if split:
            hh = jnp.where(i < nc, rows[0][...], rows[1][...])
            rest[0][...] = hh
        else:
            hh = rows[0][...]
        ic = _kind(i)
        z, _, _ = _norm_mod(hh, g_ref[...], _sel(ic, mod_ref, 0), _sel(ic, mod_ref, 1))
        zb = z.astype(BF16)
        for j in range(D_FF // HT):
            gb, ub = slice(j * HT, (j + 1) * HT), slice(D_FF + j * HT, D_FF + (j + 1) * HT)
            gg = _dot_nt(zb, win_v[gb, :])
            uu = _dot_nt(zb, win_v[ub, :])
            gu_ref[:, gb] = gg.astype(BF16)
            gu_ref[:, ub] = uu.astype(BF16)
            a_v[:, gb] = ((gg * _sigmoid(gg)) * uu).astype(BF16)
        acc = _dot(a_v[...], wout_v[...])
        acc_ref[...] = acc
        hn = hh + (0.5 * _sel(ic, mod_ref, 2)) * acc
        if loss is None:
            out_ref[...] = hn
        else:
            loss_ref, dgf_ref = rest

            @pl.when(i == 0)
            def _():
                loss_ref[...] = jnp.zeros_like(loss_ref)
                dgf_ref[...] = jnp.zeros_like(dgf_ref)

            @pl.when(i < nc)
            def _():
                out_ref[...] = jnp.zeros_like(out_ref)

            @pl.when(i >= nc)
            def _():
                gain = fg_ref[...]
                r = lax.rsqrt(jnp.mean(hn * hn, axis=-1, keepdims=True) + EPS)
                n = hn * r
                err = n * gain - tgt_ref[...]
                loss_ref[...] += 0.5 * jnp.sum(jnp.mean(err * err, axis=-1, keepdims=True))
                dy = err * (1.0 / D)
                dgf_ref[...] += jnp.sum(dy * n, axis=0, keepdims=True)
                dn = dy * gain
                out_ref[...] = r * (dn - n * jnp.mean(dn * n, axis=-1, keepdims=True))

    row = pl.BlockSpec((TM, D), lambda i: (i, 0))
    vec = pl.BlockSpec((1, D), lambda i: (0, 0))
    rshape = jax.ShapeDtypeStruct((t, D), F32)
    if split:
        rows_in = [pl.BlockSpec((TM, D), lambda i: (jnp.minimum(i, nc - 1), 0)),
                   pl.BlockSpec((TM, D), lambda i: (jnp.maximum(i - nc, 0), 0))]
    else:
        rows_in = [row]
    in_specs = rows_in + [pl.BlockSpec((2, 3, D), lambda i: (0, 0, 0)), vec, ANY, ANY]
    out_specs = [row, pl.BlockSpec((TM, 2 * D_FF), lambda i: (i, 0)), row] + ([row] if split else [])
    out_shape = [rshape, jax.ShapeDtypeStruct((t, 2 * D_FF), BF16), rshape] + ([rshape] if split else [])
    args = (*(h if split else (h,)), mod, g, win, wout)
    if loss is not None:
        in_specs += [vec, pl.BlockSpec((TM, D), lambda i: (jnp.maximum(i - nc, 0), 0))]
        out_specs += [pl.BlockSpec((8, 128), lambda i: (0, 0)), vec]
        out_shape += [jax.ShapeDtypeStruct((8, 128), F32), jax.ShapeDtypeStruct((1, D), F32)]
        args += tuple(loss)
    return _grid_call(
        body, name=name, nsteps=t // TM, in_specs=in_specs, out_specs=out_specs, out_shape=out_shape,
        scratch_shapes=[pltpu.VMEM((2 * D_FF, D), BF16), pltpu.VMEM((D_FF, D), BF16), pltpu.VMEM((TM, D_FF), BF16)],
        args=args, rider=rider)


def ffn_bwd_a(dy, acc, gu, mod, wout, name, rider=None):
    t = dy.shape[0]
    nt = t // TM

    def body(dy_ref, acc_ref, gu_ref, mod_ref, wout_hbm, dp_ref, dwout_hbm, dgate_ref, wout_v, dwout_v):
        i = pl.program_id(0)

        @pl.when(i == 0)
        def _():
            pltpu.sync_copy(wout_hbm, wout_v)
            dwout_v[...] = jnp.zeros_like(dwout_v)
            dgate_ref[...] = jnp.zeros_like(dgate_ref)

        dy_ = dy_ref[...]
        ic = _kind(i)
        _acc2(dgate_ref, 0, 0.5 * dy_ * acc_ref[...], ic)
        daccb = ((0.5 * _sel(ic, mod_ref, 2)) * dy_).astype(BF16)
        for j in range(D_FF // HT):
            blk, ublk = slice(j * HT, (j + 1) * HT), slice(D_FF + j * HT, D_FF + (j + 1) * HT)
            da = _dot_nt(daccb, wout_v[blk, :])
            gg = gu_ref[:, blk].astype(F32)
            uu = gu_ref[:, ublk].astype(F32)
            s = _sigmoid(gg)
            sl = gg * s
            dwout_v[blk, :] += _dot_tn((sl * uu).astype(BF16), daccb)
            dp_ref[:, blk] = (da * uu * (s + sl * (1.0 - s))).astype(BF16)
            dp_ref[:, ublk] = (da * sl).astype(BF16)

        @pl.when(i == nt - 1)
        def _():
            pltpu.sync_copy(dwout_v, dwout_hbm)

    row = pl.BlockSpec((TM, D), lambda i: (i, 0))
    wide = pl.BlockSpec((TM, 2 * D_FF), lambda i: (i, 0))
    return _grid_call(
        body, name=name, nsteps=nt,
        in_specs=[row, row, wide, pl.BlockSpec((2, 3, D), lambda i: (0, 0, 0)), ANY],
        out_specs=[wide, ANY, pl.BlockSpec((2, 1, D), lambda i: (0, 0, 0))],
        out_shape=[jax.ShapeDtypeStruct((t, 2 * D_FF), BF16), jax.ShapeDtypeStruct((D_FF, D), F32),
                   jax.ShapeDtypeStruct((2, 1, D), F32)],
        scratch_shapes=[pltpu.VMEM((D_FF, D), BF16), pltpu.VMEM((D_FF, D), F32)],
        args=(dy, acc, gu, mod, wout), rider=rider)


def ffn_bwd_b(dy, h, dp, mod, g, win, name, rider=None, latent_only=False):
    t = dy.shape[0]
    nt = t // TM
    nc = CTX // TM

    def body(dy_ref, h_ref, dp_ref, mod_ref, g_ref, win_hbm, dh_ref, dwin_hbm, dmod_ref, dg_ref, win_v, dwin_v):
        i = pl.program_id(0)

        @pl.when(i == 0)
        def _():
            pltpu.sync_copy(win_hbm, win_v)
            dwin_v[...] = jnp.zeros_like(dwin_v)
            dmod_ref[...] = jnp.zeros_like(dmod_ref)
            dg_ref[...] = jnp.zeros_like(dg_ref)

        ic = _kind(i)
        gain = g_ref[...]
        scale = _sel(ic, mod_ref, 1)
        z, n, r = _norm_mod(h_ref[...], gain, _sel(ic, mod_ref, 0), scale)
        zb = z.astype(BF16)
        dz = _dot(dp_ref[...], win_v[...])
        for j in range(2 * D_FF // WT):
            blk = slice(j * WT, (j + 1) * WT)
            dwin_v[blk, :] += _dot_tn(dp_ref[:, blk], zb)
        _acc2(dmod_ref, 0, dz, ic)
        _acc2(dmod_ref, 1, dz * (n * gain), ic)
        dg_ref[...] += jnp.sum(dz * (1.0 + scale) * n, axis=0, keepdims=True)
        dh_ref[...] = dy_ref[...] + _norm_mod_bwd(dz, n, r, gain, scale)

        @pl.when(i == nt - 1)
        def _():
            pltpu.sync_copy(dwin_v, dwin_hbm)

    row = pl.BlockSpec((TM, D), lambda i: (i, 0))
    if latent_only:
        dh_spec = pl.BlockSpec((TM, D), lambda i: (jnp.maximum(i - nc, 0), 0))
        dh_shape = jax.ShapeDtypeStruct((t - CTX, D), F32)
    else:
        dh_spec, dh_shape = row, jax.ShapeDtypeStruct((t, D), F32)
    return _grid_call(
        body, name=name, nsteps=nt,
        in_specs=[row, row, pl.BlockSpec((TM, 2 * D_FF), lambda i: (i, 0)),
                  pl.BlockSpec((2, 3, D), lambda i: (0, 0, 0)), pl.BlockSpec((1, D), lambda i: (0, 0)), ANY],
        out_specs=[dh_spec, ANY, pl.BlockSpec((2, 2, D), lambda i: (0, 0, 0)), pl.BlockSpec((1, D), lambda i: (0, 0))],
        out_shape=[dh_shape, jax.ShapeDtypeStruct((2 * D_FF, D), F32),
                   jax.ShapeDtypeStruct((2, 2, D), F32), jax.ShapeDtypeStruct((1, D), F32)],
        scratch_shapes=[pltpu.VMEM((2 * D_FF, D), BF16), pltpu.VMEM((2 * D_FF, D), F32)],
        args=(dy, h, dp, mod, g, win), rider=rider)


def ffn_bwd_dw(h, dp, mod, g, name, rider=None):
    t = h.shape[0]
    nt = t // TM

    def body(h_ref, dp_ref, mod_ref, g_ref, dwin_hbm, dwin_v):
        i = pl.program_id(0)

        @pl.when(i == 0)
        def _():
            dwin_v[...] = jnp.zeros_like(dwin_v)

        zs = []
        for k in range(SUB):
            ic = _kind(SUB * i + k)
            z, _, _ = _norm_mod(h_ref[pl.ds(k * TM, TM), :], g_ref[...], _sel(ic, mod_ref, 0), _sel(ic, mod_ref, 1))
            zs.append(z.astype(BF16))
        zb = jnp.concatenate(zs, axis=0)
        for j in range(2 * D_FF // WT):
            blk = slice(j * WT, (j + 1) * WT)
            dwin_v[blk, :] += _dot_tn(dp_ref[:, blk], zb)

        @pl.when(i == nt // SUB - 1)
        def _():
            pltpu.sync_copy(dwin_v, dwin_hbm)

    return _grid_call(
        body, name=name, nsteps=nt // SUB,
        in_specs=[pl.BlockSpec((TMX, D), lambda i: (i, 0)), pl.BlockSpec((TMX, 2 * D_FF), lambda i: (i, 0)),
                  pl.BlockSpec((2, 3, D), lambda i: (0, 0, 0)), pl.BlockSpec((1, D), lambda i: (0, 0))],
        out_specs=[ANY], out_shape=[jax.ShapeDtypeStruct((2 * D_FF, D), F32)],
        scratch_shapes=[pltpu.VMEM((2 * D_FF, D), F32)],
        args=(h, dp, mod, g), rider=rider)


def ffn_bwd_dh(dy, h, dp, mod, g, win, name, tiles, carry=None, rider=None):
    t = dy.shape[0]
    nc = CTX // TM
    t0, t1 = tiles

    def body(*refs):
        if carry is None:
            dy_ref, h_ref, dp_ref, mod_ref, g_ref, win_hbm, dh_ref, dmod_ref, dg_ref, win_v = refs
        else:
            dy_ref, h_ref, dp_ref, mod_ref, g_ref, win_hbm, _, dmod0_ref, dg0_ref, dh_ref, dmod_ref, dg_ref, win_v = refs
        i = pl.program_id(0)

        @pl.when(i == 0)
        def _():
            pltpu.sync_copy(win_hbm, win_v)
            dmod_ref[...] = jnp.zeros_like(dmod_ref) if carry is None else dmod0_ref[...]
            dg_ref[...] = jnp.zeros_like(dg_ref) if carry is None else dg0_ref[...]

        ic = _kind(i + t0)
        gain = g_ref[...]
        scale = _sel(ic, mod_ref, 1)
        _, n, r = _norm_mod(h_ref[...], gain, _sel(ic, mod_ref, 0), scale)
        dz = _dot(dp_ref[...], win_v[...])
        _acc2(dmod_ref, 0, dz, ic)
        _acc2(dmod_ref, 1, dz * (n * gain), ic)
        dg_ref[...] += jnp.sum(dz * (1.0 + scale) * n, axis=0, keepdims=True)
        dh_ref[...] = dy_ref[...] + _norm_mod_bwd(dz, n, r, gain, scale)

    row = pl.BlockSpec((TM, D), lambda i: (i + t0, 0))
    small = [pl.BlockSpec((2, 2, D), lambda i: (0, 0, 0)), pl.BlockSpec((1, D), lambda i: (0, 0))]
    in_specs = [row, row, pl.BlockSpec((TM, 2 * D_FF), lambda i: (i + t0, 0)),
                pl.BlockSpec((2, 3, D), lambda i: (0, 0, 0)), pl.BlockSpec((1, D), lambda i: (0, 0)), ANY]
    args = (dy, h, dp, mod, g, win)
    if carry is not None:
        in_specs += [ANY] + small
        args += tuple(carry)
    return _grid_call(
        body, name=name, nsteps=t1 - t0, in_specs=in_specs,
        out_specs=[pl.BlockSpec((TM, D), lambda i: (jnp.maximum(i + t0 - nc, 0), 0))] + small,
        out_shape=[jax.ShapeDtypeStruct((t - CTX, D), F32), jax.ShapeDtypeStruct((2, 2, D), F32),
                   jax.ShapeDtypeStruct((1, D), F32)],
        scratch_shapes=[pltpu.VMEM((2 * D_FF, D), BF16)],
        args=args, rider=rider, aliases=None if carry is None else {6: 0})


def mix_in_fwd(h, mod, g, wmix, name):
    t = h.shape[0]

    def body(h_ref, mod_ref, g_ref, w_hbm, p_ref, w_v):
        i = pl.program_id(0)

        @pl.when(i == 0)
        def _():
            pltpu.sync_copy(w_hbm, w_v)

        zs = []
        for k in range(SUB):
            ic = _kind(SUB * i + k)
            z, _, _ = _norm_mod(h_ref[pl.ds(k * TM, TM), :], g_ref[...], _sel(ic, mod_ref, 0), _sel(ic, mod_ref, 1))
            zs.append(z.astype(BF16))
        zb = jnp.concatenate(zs, axis=0)
        for dd in range(N_DEV):
            p_ref[:, dd * N_MIX_SHARD:(dd + 1) * N_MIX_SHARD] = _dot(zb, w_v[dd])

    return pl.pallas_call(
        body, name=name, grid=(t // TMX,),
        in_specs=[pl.BlockSpec((TMX, D), lambda i: (i, 0)), pl.BlockSpec((2, 3, D), lambda i: (0, 0, 0)),
                  pl.BlockSpec((1, D), lambda i: (0, 0)), ANY],
        out_specs=pl.BlockSpec((TMX, IN_PROJ), lambda i: (i, 0)),
        out_shape=jax.ShapeDtypeStruct((t, IN_PROJ), F32),
        scratch_shapes=[pltpu.VMEM((N_DEV, D, N_MIX_SHARD), BF16)],
        compiler_params=_cp(),
    )(h, mod, g, wmix)


def _halo_specs(nt, tile_of, rows=TM):
    nb = nt * (rows // HALO)
    main = pl.BlockSpec((rows, LRU_W), lambda s: (tile_of(s), 0))
    prev = pl.BlockSpec((HALO, LRU_W), lambda s: (jnp.maximum(tile_of(s) * (rows // HALO) - 1, 0), 0))
    nxt = pl.BlockSpec((HALO, LRU_W), lambda s: (jnp.minimum((tile_of(s) + 1) * (rows // HALO), nb - 1), 0))
    return main, prev, nxt


def _ext(tile, nt, main, prev, nxt):
    has_prev = jnp.logical_and(tile != 0, tile != 1)
    has_next = jnp.logical_and(tile != 0, tile != nt - 1)
    return jnp.concatenate([jnp.where(has_prev, prev, 0.0), main, jnp.where(has_next, nxt, 0.0)], axis=0)


def _shifted(ext, off):
    n = ext.shape[0]
    return pltpu.roll(ext, (-off) % n, 0)[HALO:HALO + TM]


def _conv(ext, cw_ref, cb_ref):
    xc = cb_ref[...] + cw_ref[0:1, :] * _shifted(ext, -2)
    for k in range(1, CONV_W):
        xc = xc + cw_ref[k:k + 1, :] * _shifted(ext, k - 2)
    return xc


def _log1p(y):
    return jnp.where(y < 1e-2, y * (1.0 - y * (0.5 - y * (1.0 / 3.0 - 0.25 * y))), jnp.log(1.0 + y))


def _softplus_neg(lam):
    return jnp.maximum(-lam, 0.0) + _log1p(jnp.exp(-jnp.abs(lam)))


def _one_minus_exp(x, exp_half):
    p = x * (1.0 + x * (1 / 2 + x * (1 / 6 + x * (1 / 24))))
    return jnp.where(x > -0.1, -p, 1.0 - exp_half * exp_half)


def _gates(xc, wr, br, wi, bi, lam):
    xb = xc.astype(BF16)
    r = _sigmoid(_dot(xb, wr) + br)
    ig = _sigmoid(_dot(xb, wi) + bi)
    sp = _softplus_neg(lam)
    log_a = -RG_C * r * sp
    a = jnp.exp(log_a)
    mult = jnp.sqrt(_one_minus_exp(2.0 * log_a, a))
    return r, ig, sp, a, mult


def _scan(a, b, reverse):
    n = a.shape[0]
    row = lax.broadcasted_iota(jnp.int32, a.shape, 0)
    s = 1
    while s < n:
        if s < HALO:
            if reverse:
                keep = row < n - s
                a_s = jnp.where(keep, pltpu.roll(a, n - s, 0), 1.0)
                b_s = jnp.where(keep, pltpu.roll(b, n - s, 0), 0.0)
            else:
                keep = row >= s
                a_s = jnp.where(keep, pltpu.roll(a, s, 0), 1.0)
                b_s = jnp.where(keep, pltpu.roll(b, s, 0), 0.0)
            b = a * b_s + b
            a = a * a_s
        elif reverse:
            b = jnp.concatenate([a[:n - s] * b[s:] + b[:n - s], b[n - s:]], axis=0)
            a = jnp.concatenate([a[:n - s] * a[s:], a[n - s:]], axis=0)
        else:
            b = jnp.concatenate([b[:s], a[s:] * b[:n - s] + b[s:]], axis=0)
            a = jnp.concatenate([a[:s], a[s:] * a[:n - s]], axis=0)
        s *= 2
    return a, b


def _direction_specs(d):
    return (pl.BlockSpec((None, 1, LRU_W), lambda s: (d, 0, 0)), pl.BlockSpec((None, LRU_W, LRU_W), lambda s: (d, 0, 0)))


def lru_fwd(p, conv_w, conv_b, wr, br, wi, bi, lam, reverse, name, rider=None):
    t = p.shape[0]
    nt = t // TM

    def tile_of(s):
        return jnp.where(s == 0, 0, nt - s) if reverse else s

    def body(x_ref, xp_ref, xn_ref, cw_ref, cb_ref, wr_ref, br_ref, wi_ref, bi_ref, lam_ref, h_ref, carry):
        s = pl.program_id(0)
        tile = tile_of(s)

        @pl.when(s == 0)
        def _():
            carry[...] = jnp.zeros_like(carry)

        ext = _ext(tile, nt, x_ref[...], xp_ref[...], xn_ref[...])
        xc = _conv(ext, cw_ref, cb_ref)
        _, ig, _, a, mult = _gates(xc, wr_ref[...], br_ref[...], wi_ref[...], bi_ref[...], lam_ref[...])
        a_cum, hl = _scan(a, mult * (ig * xc), reverse)
        hh = hl + a_cum * carry[...]
        h_ref[...] = hh
        carry[...] = hh[0:1, :] if reverse else hh[TM - 1:TM, :]

    main, prev, nxt = _halo_specs(nt, tile_of)
    vec = pl.BlockSpec((1, LRU_W), lambda s: (0, 0))
    dvec, dmat = _direction_specs(int(reverse))
    outs, got = _grid_call(
        body, name=name, nsteps=nt,
        in_specs=[main, prev, nxt, pl.BlockSpec((CONV_W, LRU_W), lambda s: (0, 0)), vec, dmat, dvec, dmat, dvec, dvec],
        out_specs=[main],
        out_shape=[jax.ShapeDtypeStruct((t, LRU_W), F32)],
        scratch_shapes=[pltpu.VMEM((1, LRU_W), F32)],
        args=(p, p, p, conv_w, conv_b, wr, br, wi, bi, lam), rider=rider)
    return outs[0], got


def lru_bwd(p, hs, dhs, conv_w, conv_b, wr, br, wi, bi, lam, reverse, name):
    t = p.shape[0]
    nt = t // TM
    bpt = TM // HALO

    def tile_of(s):
        return jnp.where(s == nt - 1, 0, s + 1) if reverse else nt - 1 - s

    def hprev_block(s):
        tile = tile_of(s)
        if reverse:
            return (jnp.where(tile == nt - 1, 0, jnp.minimum((tile + 1) * bpt, nt * bpt - 1)), 0)
        return (jnp.maximum(tile * bpt - 1, 0), 0)

    def body(x_ref, xp_ref, xn_ref, h_ref, hp_ref, dh_ref, cw_ref, cb_ref, wr_ref, br_ref, wi_ref, bi_ref, lam_ref,
             dxc_ref, dwr_out, dwi_out, dbr_ref, dbi_ref, dlam_ref, carry, dwr_ref, dwi_ref):
        s = pl.program_id(0)
        tile = tile_of(s)

        @pl.when(s == 0)
        def _():
            carry[...] = jnp.zeros_like(carry)
            for ref in (dwr_ref, dwi_ref, dbr_ref, dbi_ref, dlam_ref):
                ref[...] = jnp.zeros_like(ref)

        ext = _ext(tile, nt, x_ref[...], xp_ref[...], xn_ref[...])
        xc = _conv(ext, cw_ref, cb_ref)
        wr_, wi_ = wr_ref[...], wi_ref[...]
        r, ig, sp, a, mult = _gates(xc, wr_, br_ref[...], wi_, bi_ref[...], lam_ref[...])
        gated = ig * xc
        row = lax.broadcasted_iota(jnp.int32, (TM, LRU_W), 0)
        hh = h_ref[...]
        start = jnp.where(tile != 0, hp_ref[0:1, :] if reverse else hp_ref[HALO - 1:HALO, :], 0.0)
        if reverse:
            edge = row == TM - 1
            hprev = jnp.where(edge, start, pltpu.roll(hh, TM - 1, 0))
            coef = jnp.where(row == 0, 0.0, pltpu.roll(a, 1, 0))
            bb = dh_ref[...] + jnp.where(row == 0, carry[...], 0.0)
        else:
            edge = row == 0
            hprev = jnp.where(edge, start, pltpu.roll(hh, 1, 0))
            coef = jnp.where(row == TM - 1, 0.0, pltpu.roll(a, TM - 1, 0))
            bb = dh_ref[...] + jnp.where(row == TM - 1, carry[...], 0.0)
        _, lmb = _scan(coef, bb, not reverse)
        al = a * lmb
        carry[...] = al[TM - 1:TM, :] if reverse else al[0:1, :]

        dgated = lmb * mult
        dloga = (lmb * hprev) * a - (lmb * gated) * (a * a) / mult
        dpre_r = (dloga * (-RG_C * sp)) * r * (1.0 - r)
        dpre_i = (dgated * xc) * ig * (1.0 - ig)
        drb, dib = dpre_r.astype(BF16), dpre_i.astype(BF16)
        xb = xc.astype(BF16)
        dxc_ref[...] = dgated * ig + _dot_nt(drb, wr_) + _dot_nt(dib, wi_)
        dwr_ref[...] += _dot_tn(xb, drb)
        dwi_ref[...] += _dot_tn(xb, dib)
        dbr_ref[...] += jnp.sum(dpre_r, axis=0, keepdims=True)
        dbi_ref[...] += jnp.sum(dpre_i, axis=0, keepdims=True)
        dlam_ref[...] += jnp.sum(dloga * (-RG_C * r), axis=0, keepdims=True)

        @pl.when(s == nt - 1)
        def _():
            dlam_ref[...] = dlam_ref[...] * (-_sigmoid(-lam_ref[...]))
            for hd in range(HEADS):
                blk = pl.ds(hd * HEAD_DIM, HEAD_DIM)
                dwr_out[hd] = dwr_ref[blk, blk]
                dwi_out[hd] = dwi_ref[blk, blk]

    main, prev, nxt = _halo_specs(nt, tile_of)
    vec = pl.BlockSpec((1, LRU_W), lambda s: (0, 0))
    dvec, dmat = _direction_specs(int(reverse))
    heads =pl.BlockSpec((HEADS, HEAD_DIM, HEAD_DIM), lambda s: (0, 0, 0))
    vshape = jax.ShapeDtypeStruct((1, LRU_W), F32)
    hshape = jax.ShapeDtypeStruct((HEADS, HEAD_DIM, HEAD_DIM), F32)
    return pl.pallas_call(
        body, name=name, grid=(nt,),
        in_specs=[main, prev, nxt, main, pl.BlockSpec((HALO, LRU_W), hprev_block), main,
                  pl.BlockSpec((CONV_W, LRU_W), lambda s: (0, 0)), vec, dmat, dvec, dmat, dvec, dvec],
        out_specs=[main, heads, heads, vec, vec, vec],
        out_shape=[jax.ShapeDtypeStruct((t, LRU_W), F32), hshape, hshape, vshape, vshape, vshape],
        scratch_shapes=[pltpu.VMEM((1, LRU_W), F32), pltpu.VMEM((LRU_W, LRU_W), F32), pltpu.VMEM((LRU_W, LRU_W), F32)],
        compiler_params=_cp(),
    )(p, p, p, hs, hs, dhs, conv_w, conv_b, wr, br, wi, bi, lam)


GELU_C = 0.7978845608028654
GELU_A = 0.044715


def _gelu(x):
    th = jnp.tanh(GELU_C * (x + GELU_A * x * x * x))
    return 0.5 * x * (1.0 + th), th


def _sgu(v, gain, w_ref, bt_ref):
    mu = jnp.mean(v, axis=-1, keepdims=True)
    xc = v - mu
    rs = lax.rsqrt(jnp.mean(xc * xc, axis=-1, keepdims=True) + EPS)
    vhat = xc * rs
    vnb = (vhat * gain).astype(BF16)
    chunks = []
    for ch in range(TM // CHUNK):
        zs = []
        for gi in range(GROUPS):
            vb = vnb[ch * CHUNK:(ch + 1) * CHUNK, gi * GROUP_DIM:(gi + 1) * GROUP_DIM]
            zs.append(_dot(w_ref[gi].astype(BF16), vb) + bt_ref[:, gi:gi + 1])
        chunks.append(jnp.concatenate(zs, axis=1))
    return jnp.concatenate(chunks, axis=0), vhat, rs, vnb


def _pcols(k, rows=TM):
    return pl.BlockSpec((rows, LRU_W), lambda i: (i, k))


def mix_out_fwd(h, p, hf, hb, mod, sgu_g, sgu_w, sgu_bt, womix, name):
    t = h.shape[0]

    def body(h_ref, gl_ref, u_ref, v_ref, hf_ref, hb_ref, mod_ref, sg_ref, sw_ref, sb_ref, w_hbm, out_ref, o_ref, w_v):
        i = pl.program_id(0)

        @pl.when(i == 0)
        def _():
            pltpu.sync_copy(w_hbm, w_v)

        ys = []
        for k in range(SUB):
            rows = pl.ds(k * TM, TM)
            ge, _ = _gelu(gl_ref[rows, :])
            y_lru = (hf_ref[rows, :] + hb_ref[rows, :]) * ge
            z, _, _, _ = _sgu(v_ref[rows, :], sg_ref[...], sw_ref, sb_ref)
            ys.append(jnp.concatenate([y_lru, u_ref[rows, :] * z], axis=1).astype(BF16))
        o = _dot(jnp.concatenate(ys, axis=0), w_v[...])
        o_ref[...] = o
        for k in range(SUB):
            rows = pl.ds(k * TM, TM)
            out_ref[rows, :] = h_ref[rows, :] + _sel(_kind(SUB * i + k), mod_ref, 2) * o[k * TM:(k + 1) * TM]

    row = pl.BlockSpec((TMX, D), lambda i: (i, 0))
    half = pl.BlockSpec((TMX, LRU_W), lambda i: (i, 0))
    return pl.pallas_call(
        body, name=name, grid=(t // TMX,),
        in_specs=[row, _pcols(1, TMX), _pcols(2, TMX), _pcols(3, TMX), half, half, pl.BlockSpec((2, 3, D), lambda i: (0, 0, 0)),
                  pl.BlockSpec((1, MLP_W), lambda i: (0, 0)), pl.BlockSpec((GROUPS, CHUNK, CHUNK), lambda i: (0, 0, 0)),
                  pl.BlockSpec((CHUNK, GROUPS), lambda i: (0, 0)), ANY],
        out_specs=[row, row],
        out_shape=[jax.ShapeDtypeStruct((t, D), F32), jax.ShapeDtypeStruct((t, D), F32)],
        scratch_shapes=[pltpu.VMEM((D, D), BF16)],
        compiler_params=_cp(),
    )(h, p, p, p, hf, hb, mod, sgu_g, sgu_w, sgu_bt, womix)


def mix_out_bwd(dy, p, hf, hb, o, mod, sgu_g, sgu_w, sgu_bt, womix, name, rider=None):
    t = dy.shape[0]

    def body(dy_ref, gl_ref, u_ref, v_ref, hf_ref, hb_ref, o_ref, mod_ref, sg_ref, sw_ref, sb_ref, w_hbm,
             dhs_ref, dp_ref, dw_ref, dgate_ref, dsg_ref, dsw_ref, dsb_ref, w_v):
        i = pl.program_id(0)

        @pl.when(i == 0)
        def _():
            pltpu.sync_copy(w_hbm, w_v)
            for ref in (dw_ref, dgate_ref, dsg_ref, dsw_ref, dsb_ref):
                ref[...] = jnp.zeros_like(ref)

        ys, dobs = [], []
        for k in range(SUB):
            rows = pl.ds(k * TM, TM)
            ic = _kind(SUB * i + k)
            dy_ = dy_ref[rows, :]
            _acc2(dgate_ref, 0, dy_ * o_ref[rows, :], ic)
            dob = (_sel(ic, mod_ref, 2) * dy_).astype(BF16)

            gl = gl_ref[rows, :]
            ge, th = _gelu(gl)
            hsum = hf_ref[rows, :] + hb_ref[rows, :]
            gain = sg_ref[...]
            uu = u_ref[rows, :]
            z, vhat, rs, vnb = _sgu(v_ref[rows, :], gain, sw_ref, sb_ref)
            ys.append(jnp.concatenate([hsum * ge, uu * z], axis=1).astype(BF16))
            dobs.append(dob)
            dyy = _dot_nt(dob, w_v[...])
            dyl, dys = dyy[:, :LRU_W], dyy[:, LRU_W:]

            dhs_ref[rows, :] = dyl * ge
            dge = 0.5 * (1.0 + th) + 0.5 * gl * (1.0 - th * th) * (GELU_C * (1.0 + 3.0 * GELU_A * gl * gl))
            dp_ref[rows, 0:LRU_W] = (dyl * hsum * dge).astype(BF16)
            dp_ref[rows, LRU_W:2 * LRU_W] = (dys * z).astype(BF16)

            dz = dys * uu
            dzb = dz.astype(BF16)
            dvn_chunks, dsb_cols = [], [jnp.zeros((CHUNK, 1), F32)] * GROUPS
            for ch in range(TM // CHUNK):
                cols = []
                for gi in range(GROUPS):
                    rs_, cs_ = slice(ch * CHUNK, (ch + 1) * CHUNK), slice(gi * GROUP_DIM, (gi + 1) * GROUP_DIM)
                    dzg = dzb[rs_, cs_]
                    dsb_cols[gi] = dsb_cols[gi] + jnp.sum(dz[rs_, cs_], axis=1, keepdims=True)
                    dsw_ref[gi] += _dot_nt(dzg, vnb[rs_, cs_])
                    cols.append(_dot_tn(sw_ref[gi].astype(BF16), dzg))
                dvn_chunks.append(jnp.concatenate(cols, axis=1))
            dsb_ref[...] += jnp.concatenate(dsb_cols, axis=1)
            dvn = jnp.concatenate(dvn_chunks, axis=0)
            dsg_ref[...] += jnp.sum(dvn * vhat, axis=0, keepdims=True)
            dvh = dvn * gain
            dv = rs * (dvh - jnp.mean(dvh, axis=-1, keepdims=True) - vhat * jnp.mean(dvh * vhat, axis=-1, keepdims=True))
            dp_ref[rows, 2 * LRU_W:3 * LRU_W] = dv.astype(BF16)
        dw_ref[...] += _dot_tn(jnp.concatenate(ys, axis=0), jnp.concatenate(dobs, axis=0))

    row = pl.BlockSpec((TMX, D), lambda i: (i, 0))
    half = pl.BlockSpec((TMX, LRU_W), lambda i: (i, 0))
    const2 = lambda i: (0, 0)
    const3 = lambda i: (0, 0, 0)
    return _grid_call(
        body, name=name, nsteps=t // TMX,
        in_specs=[row, _pcols(1, TMX), _pcols(2, TMX), _pcols(3, TMX), half, half, row, pl.BlockSpec((2, 3, D), const3),
                  pl.BlockSpec((1, MLP_W), const2), pl.BlockSpec((GROUPS, CHUNK, CHUNK), const3),
                  pl.BlockSpec((CHUNK, GROUPS), const2), ANY],
        out_specs=[half, pl.BlockSpec((TMX, 3 * LRU_W), lambda i: (i, 0)), pl.BlockSpec((D, D), const2),
                   pl.BlockSpec((2, 1, D), const3), pl.BlockSpec((1, MLP_W), const2),
                   pl.BlockSpec((GROUPS, CHUNK, CHUNK), const3), pl.BlockSpec((CHUNK, GROUPS), const2)],
        out_shape=[jax.ShapeDtypeStruct((t, LRU_W), F32), jax.ShapeDtypeStruct((t, 3 * LRU_W), BF16),
                   jax.ShapeDtypeStruct((D, D), F32), jax.ShapeDtypeStruct((2, 1, D), F32),
                   jax.ShapeDtypeStruct((1, MLP_W), F32), jax.ShapeDtypeStruct((GROUPS, CHUNK, CHUNK), F32),
                   jax.ShapeDtypeStruct((CHUNK, GROUPS), F32)],
        scratch_shapes=[pltpu.VMEM((D, D), BF16)],
        args=(dy, p, p, p, hf, hb, o, mod, sgu_g, sgu_w, sgu_bt, womix), rider=rider)


def mix_in_bwd(dy, h, p, dxf, dxb, dprest, mod, g, conv_w, wmix, name, rider=None):
    t = dy.shape[0]
    nt = t // TM

    def body(dy_ref, h_ref, x_ref, xp_ref, xn_ref, f_ref, fp_ref, fn_ref, b_ref, bp_ref, bn_ref, dpr_ref, mod_ref,
             g_ref, cw_ref, w_hbm, dh_ref, dw_hbm, dmod_ref, dg_ref, dcw_ref, dcb_ref, w_v, dw_v):
        i = pl.program_id(0)

        @pl.when(i == 0)
        def _():
            pltpu.sync_copy(w_hbm, w_v)
            dw_v[...] = jnp.zeros_like(dw_v)
            for ref in (dmod_ref, dg_ref, dcw_ref, dcb_ref):
                ref[...] = jnp.zeros_like(ref)

        def around(ref, before, after, k):
            prev = ref[pl.ds(k * TM - HALO, HALO), :] if k > 0 else before[...]
            nxt = ref[pl.ds((k + 1) * TM, HALO), :] if k < SUB - 1 else after[...]
            return ref[pl.ds(k * TM, TM), :], prev, nxt

        gain = g_ref[...]
        zs, dps, rs = [], [], []
        for k in range(SUB):
            f_m, f_p, f_n = around(f_ref, fp_ref, fn_ref, k)
            b_m, b_p, b_n = around(b_ref, bp_ref, bn_ref, k)
            dmain = f_m + b_m
            dext = _ext(SUB * i + k, nt, dmain, f_p + b_p, f_n + b_n)
            xext = _ext(SUB * i + k, nt, *around(x_ref, xp_ref, xn_ref, k))
            dxl = cw_ref[0:1, :] * _shifted(dext, 2)
            for tap in range(1, CONV_W):
                dxl = dxl + cw_ref[tap:tap + 1, :] * _shifted(dext, 2 - tap)
            dcw_ref[...] += jnp.concatenate(
                [jnp.sum(dmain * _shifted(xext, tap - 2), axis=0, keepdims=True) for tap in range(CONV_W)], axis=0)
            dcb_ref[...] += jnp.sum(dmain, axis=0, keepdims=True)

            ic = _kind(SUB * i + k)
            rows = pl.ds(k * TM, TM)
            z, _, r = _norm_mod(h_ref[rows, :], gain, _sel(ic, mod_ref, 0), _sel(ic, mod_ref, 1))
            zs.append(z.astype(BF16))
            rs.append(r)
            dps.append(jnp.concatenate([dxl.astype(BF16), dpr_ref[rows, :]], axis=1))
        zb = jnp.concatenate(zs, axis=0)
        dpb = jnp.concatenate(dps, axis=0)
        dz_all = jnp.zeros((TMX, D), F32)
        for dd in range(N_DEV):
            dpd = dpb[:, dd * N_MIX_SHARD:(dd + 1) * N_MIX_SHARD]
            dz_all = dz_all + _dot_nt(dpd, w_v[dd])
            dw_v[dd] += _dot_tn(zb, dpd)
        for k in range(SUB):
            ic = _kind(SUB * i + k)
            rows = pl.ds(k * TM, TM)
            scale = _sel(ic, mod_ref, 1)
            dz = dz_all[k * TM:(k + 1) * TM]
            n = h_ref[rows, :] * rs[k]
            _acc2(dmod_ref, 0, dz, ic)
            _acc2(dmod_ref, 1, dz * (n * gain), ic)
            dg_ref[...] += jnp.sum(dz * (1.0 + scale) * n, axis=0, keepdims=True)
            dh_ref[rows, :] = dy_ref[rows, :] + _norm_mod_bwd(dz, n, rs[k], gain, scale)

        @pl.when(i == nt // SUB - 1)
        def _():
            pltpu.sync_copy(dw_v, dw_hbm)

    main, prev, nxt = _halo_specs(nt // SUB, lambda s: s, TMX)
    row = pl.BlockSpec((TMX, D), lambda i: (i, 0))
    const2 = lambda i: (0, 0)
    return _grid_call(
        body, name=name, nsteps=nt // SUB,
        in_specs=[row, row, main, prev, nxt, main, prev, nxt, main, prev, nxt,
                  pl.BlockSpec((TMX, 3 * LRU_W), lambda i: (i, 0)), pl.BlockSpec((2, 3, D), lambda i: (0, 0, 0)),
                  pl.BlockSpec((1, D), const2), pl.BlockSpec((CONV_W, LRU_W), const2), ANY],
        out_specs=[row, ANY, pl.BlockSpec((2, 2, D), lambda i: (0, 0, 0)), pl.BlockSpec((1, D), const2),
                   pl.BlockSpec((CONV_W, LRU_W), const2), pl.BlockSpec((1, LRU_W), const2)],
        out_shape=[jax.ShapeDtypeStruct((t, D), F32), jax.ShapeDtypeStruct((N_DEV, D, N_MIX_SHARD), F32),
                   jax.ShapeDtypeStruct((2, 2, D), F32), jax.ShapeDtypeStruct((1, D), F32),
                   jax.ShapeDtypeStruct((CONV_W, LRU_W), F32), jax.ShapeDtypeStruct((1, LRU_W), F32)],
        scratch_shapes=[pltpu.VMEM((N_DEV, D, N_MIX_SHARD), BF16), pltpu.VMEM((N_DEV, D, N_MIX_SHARD), F32)],
        args=(dy, h, p, p, p, dxf, dxf, dxf, dxb, dxb, dxb, dprest, mod, g, conv_w, wmix), rider=rider)


def _block_diag(w):
    eye = jnp.eye(HEADS, dtype=w.dtype)
    return jnp.einsum("dhij,hk->dhikj", w, eye).reshape(2, LRU_W, LRU_W)


def small_layer(g1, gm, g2, conv_w, conv_b, w_r, b_r, w_i, b_i, lam, sgu_g, sgu_w, sgu_b):
    return dict(g1=g1[None, :], gm=gm[None, :], g2=g2[None, :], conv_w=conv_w, conv_b=conv_b[None, :],
                wr=_block_diag(w_r).astype(BF16), br=b_r[:, None, :], wi=_block_diag(w_i).astype(BF16),
                bi=b_i[:, None, :], lam=lam[:, None, :], sgu_g=sgu_g[None, :], sgu_w=sgu_w, sgu_bt=sgu_b.T)


def small_grads(g):
    out = dict(mix_norm_g=g["gm"][0], ffn2_norm_g=g["g2"][0], lru_conv_w=g["conv_w"],
               lru_conv_b=g["conv_b"][0], lru_w_r=g["wr"], lru_b_r=g["br"][:, 0, :],
               lru_w_i=g["wi"], lru_b_i=g["bi"][:, 0, :], lru_lambda=g["lam"][:, 0, :],
               sgu_norm_g=g["sgu_g"][0], sgu_w=g["sgu_w"], sgu_b=g["sgu_bt"].T)
    if "g1" in g:
        out["ffn1_norm_g"] = g["g1"][0]
    return out


def _as_blocks(key, g):
    return g if key == "wmix" else g.reshape(N_DEV, g.shape[0] // N_DEV, D)


def _gathered(key, a):
    return a if key == "wmix" else a.reshape(N_DEV * a.shape[1], D)


class _ReduceScatter:
    def __init__(self, c_idx, where):
        self.c_idx, self.where = c_idx, where
        self.out = {}

    def pair(self, group):
        return pair_rider([g for _, g in group])

    def after_pair(self, group, recv1, tag):
        return chips_rider(pair_sum([g for _, g in group], list(recv1), self.c_idx, f"pair_sum_{tag}"))

    def after_chips(self, group, recv1, recv2, tag):
        for (key, g), r1, r2 in zip(group, recv1, recv2):
            self.out[key] = (g, r1, r2)


FIRST_WEIGHTS = [(0, "win1"), (0, "wout1")]


def fwd_bwd(ctx_rows, x_rows, target, mods, shards, first_weights, smalls, final_g, c_idx, where):
    assert CTX == TM and len(shards) == 2

    def gather(keys_by_layer):
        return GatherRider([shards[l][k] for l, k in keys_by_layer])

    def put(full, keys_by_layer, got):
        for (l, k), a in zip(keys_by_layer, got):
            full[l][k] = _gathered(k, a)

    full = [dict(s) for s in smalls]
    put(full, FIRST_WEIGHTS, first_weights)
    riders = {
        "ffn1_fwd_0": [(0, "wmix"), (0, "womix"), (0, "win2")],
        "lru_fwd_0_0": [(0, "wout2")],
        "ffn2_fwd_0": [(1, "win1"), (1, "wout1")],
        "ffn1_fwd_1": [(1, "wmix"), (1, "womix"), (1, "win2")],
        "lru_fwd_1_0": [(1, "wout2")],
    }

    def ffn(which, l, h):
        name = f"ffn{which}_fwd_{l}"
        w = full[l]
        keys = riders.get(name)
        m = mods[l][:, 0:3] if which == 1 else mods[l][:, 6:9]
        last = (which, l) == (2, 1)
        outs, got = ffn_fwd(h, m, w[f"g{which}"], w[f"win{which}"], w[f"wout{which}"], name,
                            rider=gather(keys) if keys else None, loss=(final_g, target) if last else None)
        if keys:
            put(full, keys, got)
        return outs

    saved = []
    h = (ctx_rows, x_rows)
    for l in range(2):
        mm = mods[l][:, 3:6]
        outs = ffn(1, l, h)
        h1, gu1, acc1 = outs[:3]
        hin = outs[3] if l == 0 else h
        w = full[l]
        p = mix_in_fwd(h1, mm, w["gm"], w["wmix"], f"mix_in_fwd_{l}")
        hs = []
        for d in range(2):
            keys = riders.get(f"lru_fwd_{l}_{d}")
            hd, got = lru_fwd(p, w["conv_w"], w["conv_b"], w["wr"], w["br"], w["wi"], w["bi"], w["lam"],
                              bool(d), f"lru_fwd_{l}_{d}", rider=gather(keys) if keys else None)
            if keys:
                put(full, keys, got)
            hs.append(hd)
        h2, o = mix_out_fwd(h1, p, hs[0], hs[1], mm, w["sgu_g"], w["sgu_w"], w["sgu_bt"], w["womix"], f"mix_out_fwd_{l}")
        outs = ffn(2, l, h2)
        h, gu2, acc2 = outs[:3]
        saved.append((hin, h1, h2, gu1, acc1, p, hs, o, gu2, acc2))
    dh, (loss, dgf) = h, outs[3:]

    rs = _ReduceScatter(c_idx, where)
    grads, dmods, sums = [None, None], [None, None], [None, None]
    pending = None
    for l in (1, 0):
        w = full[l]
        m1, mm, m2 = mods[l][:, 0:3], mods[l][:, 3:6], mods[l][:, 6:9]
        hin, h1, h2, gu1, acc1, p, hs, o, gu2, acc2 = saved[l]
        g = {}
        rs.out = {}
        both = Riders([rs.pair(pending[0]), GatherRider([small_pack])]) if pending else None
        (dp2, g["wout2"], dgate2), got = ffn_bwd_a(dh, acc2, gu2, m2, w["wout2"], f"ffn2_bwd_a_{l}", rider=both)
        if pending:
            r1, (small_all,) = both.split(got)
        chips = rs.after_pair(pending[0], r1, pending[1]) if pending else None
        (dh, g["win2"], dmod2, g["g2"]), r2 = ffn_bwd_b(dh, h2, dp2, m2, w["g2"], w["win2"], f"ffn2_bwd_b_{l}", rider=chips)
        if pending:
            rs.after_chips(pending[0], r1, r2, pending[1])
            sums[l + 1].update(rs.out)
            rs.out = {}

        grp = [(k, _as_blocks(k, g[k])) for k in ("win2", "wout2")]
        (dhs, dprest, g["womix"], dgatem, g["sgu_g"], g["sgu_w"], g["sgu_bt"]), r1 = mix_out_bwd(
            dh, p, hs[0], hs[1], o, mm, w["sgu_g"], w["sgu_w"], w["sgu_bt"], w["womix"], f"mix_out_bwd_{l}",
            rider=rs.pair(grp))
        chips = rs.after_pair(grp, r1, f"a{l}")
        dx, per_dir = [], []
        for d in range(2):
            out = lru_bwd(p, hs[d], dhs, w["conv_w"], w["conv_b"], w["wr"], w["br"], w["wi"], w["bi"], w["lam"],
                          bool(d), f"lru_bwd_{l}_{d}")
            dx.append(out[0])
            per_dir.append(out[1:])
        for k, nm in enumerate(("wr", "wi", "br", "bi", "lam")):
            g[nm] = jnp.stack([per_dir[0][k], per_dir[1][k]])
        (dh, g["wmix"], dmodm, g["gm"], g["conv_w"], g["conv_b"]), r2 = mix_in_bwd(
            dh, h1, p, dx[0], dx[1], dprest, mm, w["gm"], w["conv_w"], w["wmix"], f"mix_in_bwd_{l}", rider=chips)
        rs.after_chips(grp, r1, r2, f"a{l}")
        sums[l] = dict(rs.out)
        rs.out = {}

        if l == 1:
            (dp1, g["wout1"], dgate1), _ = ffn_bwd_a(dh, acc1, gu1, m1, w["wout1"], f"ffn1_bwd_a_{l}")
            (dh, g["win1"], dmod1, g["g1"]), _ = ffn_bwd_b(dh, hin, dp1, m1, w["g1"], w["win1"], f"ffn1_bwd_b_{l}")
            pending = ([(k, _as_blocks(k, g[k])) for k in ("womix", "wmix", "wout1", "win1")], f"b{l}")
            per = small_grads(g)
            small_pack = _pack([per[n] for n in LAYER_SMALL])
        else:
            g_mix = [(k, _as_blocks(k, g[k])) for k in ("womix", "wmix")]
            (dp1, g["wout1"], dgate1), r1_mix = ffn_bwd_a(dh, acc1, gu1, m1, w["wout1"], f"ffn1_bwd_a_{l}",
                                                          rider=rs.pair(g_mix))
            g_out = [("wout1", _as_blocks("wout1", g["wout1"]))]
            per = small_grads(g)
            three = Riders([rs.after_pair(g_mix, r1_mix, f"b{l}"), rs.pair(g_out),
                            GatherRider([_pack([per[n] for n in LAYER_SMALL[1:]])])])
            (g["win1"],), got = ffn_bwd_dw(hin, dp1, m1, w["g1"], f"ffn1_bwd_dw_{l}", rider=three)
            r2_mix, r1_out, (small0_all,) = three.split(got)
            rs.after_chips(g_mix, r1_mix, r2_mix, f"b{l}")
            g_in = [("win1", _as_blocks("win1", g["win1"]))]
            both = Riders([rs.after_pair(g_out, r1_out, f"c{l}"), rs.pair(g_in)])
            nt = dh.shape[0] // TM
            cut0, cut1 = (3 * nt) // 8, nt - max(nt // 16, 1)
            part, got = ffn_bwd_dh(dh, hin, dp1, m1, w["g1"], w["win1"], f"ffn1_bwd_dh0_{l}", (0, cut0), rider=both)
            r2_out, r1_in = both.split(got)
            rs.after_chips(g_out, r1_out, r2_out, f"c{l}")
            part, r2_in = ffn_bwd_dh(dh, hin, dp1, m1, w["g1"], w["win1"], f"ffn1_bwd_dh1_{l}", (cut0, cut1), carry=part,
                                     rider=rs.after_pair(g_in, r1_in, f"d{l}"))
            rs.after_chips(g_in, r1_in, r2_in, f"d{l}")
            (dh, dmod1, g["g1"]), _ = ffn_bwd_dh(dh, hin, dp1, m1, w["g1"], w["win1"], f"ffn1_bwd_dh2_{l}", (cut1, nt),
                                                 carry=part)
            sums[l].update(rs.out)
        dmods[l] = jnp.concatenate([dmod1, dgate1, dmodm, dgatem, dmod2, dgate2], axis=1)
        grads[l] = g
    return loss, dh, jnp.stack(dmods), grads, (small0_all, small_all), sums, dgf


def _row_block(r, c, limit=262144):
    best = 8
    for rb in range(8, r + 1, 8):
        if r % rb == 0 and rb * c <= limit:
            best = rb
    return best


PAIR_SUM_SPLIT = 2


def pair_sum(grads, recv, c_idx, name):
    n = len(grads)

    def body(c_ref, *refs):
        for t in range(n):
            refs[2 * n + t][...] = (refs[t][...] + refs[n + t][...]).astype(BF16)

    mine, theirs, outs = [], [], []
    for g in grads:
        _, r, c = g.shape
        rb = r // PAIR_SUM_SPLIT
        assert rb % 16 == 0
        mine.append(pl.BlockSpec((1, rb, c), lambda j, i, c_ref: (2 * j + c_ref[0], i, 0)))
        theirs.append(pl.BlockSpec((1, rb, c), lambda j, i, c_ref: (j, i, 0)))
        outs.append(jax.ShapeDtypeStruct((4, r, c), BF16))
    return pl.pallas_call(
        body, name=name,
        grid_spec=pltpu.PrefetchScalarGridSpec(num_scalar_prefetch=1, grid=(4, PAIR_SUM_SPLIT), in_specs=mine + theirs,
                                               out_specs=list(theirs)),
        out_shape=outs, compiler_params=_cp(2),
    )(c_idx, *grads, *recv)


ADA_ROWS = 16


def _silu(v):
    return v * _sigmoid(v)


def ada_fwd(cond, w_ada, b_slab, name):
    def body(c_ref, w_ref, b_ref, o_ref):
        s = _silu(c_ref[...]).astype(BF16)
        o_ref[0] = _dot(s, w_ref[0].astype(BF16)) + b_ref[0]

    return pl.pallas_call(
        body, name=name, grid=(DEPTH,),
        in_specs=[pl.BlockSpec((ADA_ROWS, D), lambda l: (0, 0)), pl.BlockSpec((1, D, ADA_SHARD), lambda l: (l, 0, 0)),
                  pl.BlockSpec((1, 1, ADA_SHARD), lambda l: (l, 0, 0))],
        out_specs=pl.BlockSpec((1, ADA_ROWS, ADA_SHARD), lambda l: (l, 0, 0)),
        out_shape=jax.ShapeDtypeStruct((DEPTH, ADA_ROWS, ADA_SHARD), F32),
        compiler_params=_cp(),
    )(cond, w_ada, b_slab)


def ada_bwd(cond, dm_sample, dm_ctx, w_ada, name):
    def body(c_ref, ds_ref, dc_ref, w_ref, gw_ref, dsc_ref):
        @pl.when(pl.program_id(0) == 0)
        def _():
            dsc_ref[...] = jnp.zeros_like(dsc_ref)

        s = _silu(c_ref[...]).astype(BF16)
        dcs = dc_ref[0]
        tot = dcs[0:1]
        for j in range(1, N_DEV):
            tot = tot + dcs[j:j + 1]
        tot8 = jnp.where(lax.broadcasted_iota(jnp.int32, (N_DEV, ADA_SHARD), 0) == 0, tot, 0.0)
        dm = jnp.concatenate([ds_ref[0], tot8], axis=0).astype(BF16)
        gw_ref[0] = _dot_tn(s, dm)
        dsc_ref[...] += _dot_nt(dm, w_ref[0].astype(BF16))[N_DEV:N_DEV + 1]

    slab = pl.BlockSpec((1, N_DEV, ADA_SHARD), lambda l: (l, 0, 0))
    wspec = pl.BlockSpec((1, D, ADA_SHARD), lambda l: (l, 0, 0))
    return pl.pallas_call(
        body, name=name, grid=(DEPTH,),
        in_specs=[pl.BlockSpec((ADA_ROWS, D), lambda l: (0, 0)), slab, slab, wspec],
        out_specs=[wspec, pl.BlockSpec((1, D), lambda l: (0, 0))],
        out_shape=[jax.ShapeDtypeStruct((DEPTH, D, ADA_SHARD), F32), jax.ShapeDtypeStruct((1, D), F32)],
        compiler_params=_cp(),
    )(cond, dm_sample, dm_ctx, w_ada)


def sum_over_devices(parts, name, silu_rows=0, w=None):
    _, r, c = parts.shape

    def body(*refs):
        p_ref, o_ref = refs[0], refs[-1]
        tot = p_ref[0]
        for j in range(1, N_DEV):
            tot = tot + p_ref[j]
        o_ref[...] = tot
        if silu_rows:
            wv = refs[1][...]
            s = _sigmoid(wv)
            o_ref[0:silu_rows, :] = tot[0:silu_rows, :] * (s * (1.0 + wv * (1.0 - s)))

    vm = pl.BlockSpec(memory_space=pltpu.VMEM)
    args = (parts,) if w is None else (parts, w)
    return pl.pallas_call(
        body, name=name, in_specs=[vm] * len(args), out_specs=vm,
        out_shape=jax.ShapeDtypeStruct((r, c), F32),
        compiler_params=pltpu.CompilerParams(vmem_limit_bytes=VMEM_LIMIT),
    )(*args)


def sum_dmods(dm_all, name):
    def body(d_ref, o_ref):
        for l in range(DEPTH):
            tot = d_ref[0, l]
            for j in range(1, N_DEV):
                tot = tot + d_ref[j, l]
            o_ref[l:l + 1, :] = tot[0:1] + tot[1:2]

    vm = pl.BlockSpec(memory_space=pltpu.VMEM)
    return pl.pallas_call(
        body, name=name, in_specs=[vm], out_specs=vm,
        out_shape=jax.ShapeDtypeStruct((DEPTH, N_MOD * D), F32),
    )(dm_all)


ADAMW_BLOCK = 512 * 1024


def adamw(w, g, m, v, name, rider=None):
    r, c = w.shape
    rb = _row_block(r, c, limit=ADAMW_BLOCK)

    def body(w_ref, g_ref, m_ref, v_ref, d_ref, nm_ref, nv_ref):
        g_ = g_ref[...]
        nm = B1 * m_ref[...] + (1.0 - B1) * g_
        nv = B2 * v_ref[...] + (1.0 - B2) * (g_ * g_)
        nm_ref[...] = nm
        nv_ref[...] = nv
        m_hat = nm / (1.0 - B1 ** STEP)
        v_hat = nv / (1.0 - B2 ** STEP)
        d_ref[...] = -LR * (m_hat / (jnp.sqrt(v_hat) + ADAM_EPS) + WD * w_ref[...])

    blk = pl.BlockSpec((rb, c), lambda i: (i, 0))
    shp = jax.ShapeDtypeStruct((r, c), F32)
    return _grid_call(body, name=name, nsteps=r // rb, in_specs=[blk] * 4, out_specs=[blk] * 3, out_shape=[shp] * 3,
                      scratch_shapes=[], args=(w, g, m, v), rider=rider)


def _adamw_math(w, g, m, v):
    nm = B1 * m + (1.0 - B1) * g
    nv = B2 * v + (1.0 - B2) * (g * g)
    m_hat = nm / (1.0 - B1 ** STEP)
    v_hat = nv / (1.0 - B2 ** STEP)
    return -LR * (m_hat / (jnp.sqrt(v_hat) + ADAM_EPS) + WD * w), nm, nv


def adamw_layers(w, parts, m, v, where, name):
    _, r, c = w.shape
    assert len(parts) == DEPTH == 2 and parts[0][0].shape == (N_DEV, r, c)
    rb = _row_block(r, c, limit=ADAMW_BLOCK // 2)
    nb = r // rb

    def body(where_ref, w_ref, a0, b0, c0, a1, b1, c1, m_ref, v_ref, go_ref, d_ref, nm_ref, nv_ref):
        def total(mine, pair, far):
            return (mine[0] + pair[0]) + ((far[0].astype(F32) + far[1].astype(F32)) + far[2].astype(F32))

        g = jnp.where(pl.program_id(0) == 0, total(a0, b0, c0), total(a1, b1, c1))
        go_ref[0], d_ref[0], nm_ref[0], nv_ref[0] = (g,) + _adamw_math(w_ref[0], g, m_ref[0], v_ref[0])

    blk = pl.BlockSpec((1, rb, c), lambda l, i, wr: (l, i, 0))

    def layer_specs(layer):
        row = (lambda l, i: jnp.where(l == 0, i, nb - 1)) if layer == 0 else (lambda l, i: jnp.where(l == 0, 0, i))
        return [pl.BlockSpec((1, rb, c), lambda l, i, wr: (wr[0], row(l, i), 0)),
                pl.BlockSpec((1, rb, c), lambda l, i, wr: (wr[1], row(l, i), 0)),
                pl.BlockSpec((3, rb, c), lambda l, i, wr: (0, row(l, i), 0))]

    shp = jax.ShapeDtypeStruct(w.shape, F32)
    return pl.pallas_call(
        body, name=name,
        grid_spec=pltpu.PrefetchScalarGridSpec(
            num_scalar_prefetch=1, grid=(DEPTH, nb),
            in_specs=[blk] + layer_specs(0) + layer_specs(1) + [blk, blk], out_specs=[blk] * 4),
        out_shape=[shp] * 4, compiler_params=_cp(2),
    )(where, w, *parts[0], *parts[1], m, v)


def _adamw_nd(w, g, m, v, name, rider=None):
    shape = w.shape
    flat = lambda a: a.reshape(-1, shape[-1])
    outs, got = adamw(flat(w), flat(g), flat(m), flat(v), name, rider=rider)
    return tuple(o.reshape(shape) for o in outs), got


LANES = 128


PACK_UNIT = 8 * LANES


ADAMW_SMALL_ROWS = 512


def _pack(arrays, row_multiple=8):
    pieces, n = [], 0
    for a in arrays:
        pieces.append(a.reshape(-1).astype(F32))
        pad = (-a.size) % PACK_UNIT
        if pad:
            pieces.append(jnp.zeros((pad,), F32))
        n += a.size + pad
    tail = (-n) % (row_multiple * LANES)
    if tail:
        pieces.append(jnp.zeros((tail,), F32))
    return jnp.concatenate(pieces).reshape(-1, LANES)


def _unpack(packed, shapes):
    out, r0 = [], 0
    lead = packed.shape[:-2]
    for shp in shapes:
        size = 1
        for s in shp:
            size *= s
        nr = 8 * -(-size // PACK_UNIT)
        blk = packed[..., r0:r0 + nr, :].reshape(lead + (nr * LANES,))[..., :size]
        out.append(blk.reshape(lead + tuple(shp)))
        r0 += nr
    return out


WEIGHTS = ["c_ctx", "w_ada", "b_ada", "ffn1_norm_g", "ffn1_w_in", "ffn1_w_out", "mix_norm_g", "w_in_mix", "lru_conv_w",
           "lru_conv_b", "lru_w_r", "lru_b_r", "lru_w_i", "lru_b_i", "lru_lambda", "sgu_norm_g", "sgu_w", "sgu_b",
           "w_out_mix", "ffn2_norm_g", "ffn2_w_in", "ffn2_w_out", "final_norm_g"]
BIG = ["w_ada", "ffn1_w_in", "ffn1_w_out", "w_in_mix", "w_out_mix", "ffn2_w_in", "ffn2_w_out"]
SHARDED_SMALL = ["lru_conv_w", "lru_b_r", "lru_b_i", "lru_lambda"]
LAYER_SMALL = ["ffn1_norm_g", "mix_norm_g", "ffn2_norm_g", "lru_conv_w", "lru_conv_b", "lru_w_r", "lru_b_r", "lru_w_i",
               "lru_b_i", "lru_lambda", "sgu_norm_g", "sgu_w", "sgu_b"]
LRU_SHARD = LRU_W // N_DEV


def _widen(a):
    return jnp.moveaxis(a, 0, -2).reshape(a.shape[1:-1] + (LRU_W,))


def kernel(x, c, ctx, c_ctx, w_ada, b_ada, ffn1_norm_g, ffn1_w_in, ffn1_w_out, mix_norm_g, w_in_mix, lru_conv_w, lru_conv_b, lru_w_r, lru_b_r, lru_w_i, lru_b_i, lru_lambda, sgu_norm_g, sgu_w, sgu_b, w_out_mix, ffn2_norm_g, ffn2_w_in, ffn2_w_out, final_norm_g, loss_target, m_c_ctx, m_w_ada, m_b_ada, m_ffn1_norm_g, m_ffn1_w_in, m_ffn1_w_out, m_mix_norm_g, m_w_in_mix, m_lru_conv_w, m_lru_conv_b, m_lru_w_r, m_lru_b_r, m_lru_w_i, m_lru_b_i, m_lru_lambda, m_sgu_norm_g, m_sgu_w, m_sgu_b, m_w_out_mix, m_ffn2_norm_g, m_ffn2_w_in, m_ffn2_w_out, m_final_norm_g, v_c_ctx, v_w_ada, v_b_ada, v_ffn1_norm_g, v_ffn1_w_in, v_ffn1_w_out, v_mix_norm_g, v_w_in_mix, v_lru_conv_w, v_lru_conv_b, v_lru_w_r, v_lru_b_r, v_lru_w_i, v_lru_b_i, v_lru_lambda, v_sgu_norm_g, v_sgu_w, v_sgu_b, v_w_out_mix, v_ffn2_norm_g, v_ffn2_w_in, v_ffn2_w_out, v_final_norm_g):
    given = dict(locals())
    W = {n: given[n] for n in WEIGHTS}
    M = {n: given["m_" + n] for n in WEIGHTS}
    V = {n: given["v_" + n] for n in WEIGHTS}
    xi, yi, ci = _position()
    me = 4 * xi + 2 * yi + ci
    chip = 2 * xi + yi

    shards = []
    tr = lambda a: jnp.swapaxes(a, 1, 2)
    for l in range(DEPTH):
        sh = dict(win1=tr(ffn1_w_in)[l], wout1=ffn1_w_out[l], wmix=w_in_mix[l], womix=w_out_mix[l], win2=tr(ffn2_w_in)[l],
                  wout2=ffn2_w_out[l])
        shards.append({k: a.astype(BF16) for k, a in sh.items()})

    sharded_shapes = [W[n].shape for n in SHARDED_SMALL]
    both = Riders([GatherRider([_pack([c[0]] + [W[n] for n in SHARDED_SMALL])]),
                   GatherRider([shards[0][k] for _, k in FIRST_WEIGHTS])])
    (got,), first_weights = both.split(run_alone(both, pl.ANY, "gather_first"))
    parts = _unpack(got, [(D,)] + sharded_shapes)
    c_all = parts[0]
    wide = {n: _widen(a) for n, a in zip(SHARDED_SMALL, parts[1:])}
    cond = jnp.concatenate([c_all, c_ctx[None, :], jnp.zeros((ADA_ROWS - N_DEV - 1, D), F32)], axis=0)
    b_slab = lax.dynamic_slice_in_dim(b_ada, me * ADA_SHARD, ADA_SHARD, axis=1)[:, None, :]
    slabs = ada_fwd(cond, w_ada, b_slab, "ada_fwd")
    mall = run_alone(GatherRider([slabs.reshape(DEPTH * ADA_ROWS, ADA_SHARD)]), pltpu.VMEM, "gather_mod")[0]
    mall = mall.reshape(N_DEV, DEPTH, ADA_ROWS, ADA_SHARD)
    m_sample = lax.dynamic_index_in_dim(mall, me, axis=2, keepdims=False)
    m_ctx = mall[:, :, N_DEV, :]
    mods = jnp.stack([jnp.transpose(m, (1, 0, 2)).reshape(DEPTH, N_MOD, D) for m in (m_ctx, m_sample)], axis=1)

    smalls = []
    for l in range(DEPTH):
        smalls.append(small_layer(ffn1_norm_g[l], mix_norm_g[l], ffn2_norm_g[l], wide["lru_conv_w"][l], lru_conv_b[l],
                                  lru_w_r[l], wide["lru_b_r"][l], lru_w_i[l], wide["lru_b_i"][l], wide["lru_lambda"][l],
                                  sgu_norm_g[l], sgu_w[l], sgu_b[l]))

    c_idx = ci.reshape(1).astype(jnp.int32)
    where = jnp.stack([me, chip]).astype(jnp.int32)
    loss_blk, dx, dmods, grads, (small0_all, small1_all), gsum, dgf = fwd_bwd(
        ctx[0], x[0], loss_target[0], mods, shards, first_weights, smalls, final_norm_g[None, :], c_idx, where)
    smalls_shape = {n: (W[n].shape[1:-1] + (LRU_W,)) if n in SHARDED_SMALL else W[n].shape[1:] for n in LAYER_SMALL}
    G, delta, new_m, new_v = {}, {}, {}, {}
    for key, n in (("win1", "ffn1_w_in"), ("wout1", "ffn1_w_out"), ("wmix", "w_in_mix"), ("womix", "w_out_mix"),
                   ("win2", "ffn2_w_in"), ("wout2", "ffn2_w_out")):
        t_in = tr if key in ("win1", "win2") else (lambda a: a)
        outs = adamw_layers(t_in(W[n]), [gsum[l][key] for l in range(DEPTH)], t_in(M[n]), t_in(V[n]), where,
                            f"adamw_{n}")
        G[n], delta[n], new_m[n], new_v[n] = [t_in(o) for o in outs]

    n_rows = DEPTH * 2 * N_MOD
    dm_rows = jnp.concatenate([dmods.reshape(n_rows, D), jnp.zeros((-n_rows % 8, D), F32)], axis=0)
    dm_all = run_alone(GatherRider([dm_rows]), pltpu.VMEM, "gather_dmod")[0][:, :n_rows]
    dm_all = dm_all.reshape(N_DEV, DEPTH, 2, N_MOD * D)
    mine = lax.dynamic_slice_in_dim(dm_all, me * ADA_SHARD, ADA_SHARD, axis=3)
    G["w_ada"], dsc = ada_bwd(cond, jnp.transpose(mine[:, :, 1, :], (1, 0, 2)), jnp.transpose(mine[:, :, 0, :], (1, 0, 2)),
                              w_ada, "ada_bwd")
    G["b_ada"] = sum_dmods(dm_all, "sum_dmods")
    (delta["w_ada"], new_m["w_ada"], new_v["w_ada"]), _ = _adamw_nd(w_ada, G["w_ada"], m_w_ada, v_w_ada, "adamw_w_ada")

    head_all = run_alone(GatherRider([_pack([dsc[0], dgf[0], grads[0]["g1"][0], loss_blk])]), pltpu.VMEM,
                         "gather_head_grads")[0]
    head = _unpack(sum_over_devices(head_all, "sum_head_grads", silu_rows=D // LANES, w=c_ctx.reshape(D // LANES, LANES)),
                   [(D,), (D,), (D,), loss_blk.shape])
    shapes = [smalls_shape[n] for n in LAYER_SMALL]
    sum0 = [head[2]] + _unpack(sum_over_devices(small0_all, "sum_small_grads_0"), shapes[1:])
    sum1 = _unpack(sum_over_devices(small1_all, "sum_small_grads_1"), shapes)
    G["c_ctx"], G["final_norm_g"] = head[0], head[1]
    for n, a0, a1 in zip(LAYER_SMALL, sum0, sum1):
        a = jnp.stack([a0, a1])
        G[n] = lax.dynamic_slice_in_dim(a, me * LRU_SHARD, LRU_SHARD, axis=a.ndim - 1) if n in SHARDED_SMALL else a

    rest = [n for n in WEIGHTS if n not in BIG]
    shapes = [W[n].shape for n in rest]
    outs, _ = adamw(*[_pack([src[n] for n in rest], row_multiple=ADAMW_SMALL_ROWS) for src in (W, G, M, V)], "adamw_small")
    for dst, packed in zip((delta, new_m, new_v), outs):
        for n, a in zip(rest, _unpack(packed, shapes)):
            dst[n] = a

    loss = head[3][0, 0]
    grad_x = dx[None]
    return (loss, grad_x, *[G[n] for n in WEIGHTS], *[delta[n] for n in WEIGHTS], *[new_m[n] for n in WEIGHTS],
            *[new_v[n] for n in WEIGHTS])
```

```python
import jax
import jax.numpy as jnp
from jax import lax
from jax.experimental import pallas as pl
from jax.experimental.pallas import tpu as pltpu

F32 = jnp.float32
BF16 = jnp.bfloat16

D = 1024
CTX = 256
DEPTH = 2
EPS = 1e-6
D_FF = 2816
LRU_W = 512
HEADS = 8
HEAD_DIM = 64
CONV_W = 4
RG_C = 8.0
GROUPS = 4
GROUP_DIM = 128
CHUNK = 128
MLP_W = 512
IN_PROJ = 2048
N_MOD = 9
N_DEV = 8

LR = 0.001
B1 = 0.9
B2 = 0.999
ADAM_EPS = 1e-08
WD = 0.01
STEP = 10

HT = 256
WT = 512
N_MIX_SHARD = IN_PROJ // N_DEV
ADA_SHARD = N_MOD * D // N_DEV

TM = 256
SUB = 3
TMX = SUB * TM
HALO = 8
VMEM_LIMIT = 60 * 1024 * 1024

MESH = pl.DeviceIdType.MESH
ANY = pl.BlockSpec(memory_space=pl.ANY)


def _cp(n_axes=1):
    return pltpu.CompilerParams(dimension_semantics=("arbitrary",) * n_axes, vmem_limit_bytes=VMEM_LIMIT)


def _position():
    return lax.axis_index("x"), lax.axis_index("y"), lax.axis_index("c")


class GatherRider:
    def __init__(self, shards):
        n = len(shards)
        self.n = n
        self.ins = list(shards)
        self.out_shape = [jax.ShapeDtypeStruct((N_DEV,) + s.shape, s.dtype) for s in shards]
        self.sems = [pltpu.SemaphoreType.DMA((n, 7)), pltpu.SemaphoreType.DMA((n, 7)), pltpu.SemaphoreType.DMA((n,))]

    def _ctx(self, outs, sems):
        x, y, c = _position()
        chips = [(1 - x, y), (x, 1 - y), (1 - x, 1 - y)]

        def copy(t, k, block, to, src=None):
            dst = outs[t].at[4 * block[0] + 2 * block[1] + block[2]]
            return pltpu.make_async_remote_copy(
                src_ref=dst if src is None else src, dst_ref=dst, send_sem=sems[0].at[t, k],
                recv_sem=sems[1].at[t, k], device_id=to, device_id_type=MESH)

        return (x, y, c), (x, y, 1 - c), chips, copy

    def _local(self, ins, outs, sems, t):
        x, y, c = _position()
        return pltpu.make_async_copy(ins[t], outs[t].at[4 * x + 2 * y + c], sems[2].at[t])

    def _first(self, ins, outs, sems, t):
        me, sibling, chips, copy = self._ctx(outs, sems)
        return [copy(t, 0, me, sibling, src=ins[t])] + [copy(t, 1 + j, me, (*chip, me[2]), src=ins[t])
                                                         for j, chip in enumerate(chips)]

    def start(self, ins, outs, sems):
        for t in range(self.n):
            self._local(ins, outs, sems, t).start()
            for cp in self._first(ins, outs, sems, t):
                cp.start()

    def mid(self, ins, outs, sems):
        me, sibling, chips, copy = self._ctx(outs, sems)
        for j, chip in enumerate(chips):
            for t in range(self.n):
                copy(t, 1 + j, (*chip, me[2]), me).wait_recv()
                copy(t, 4 + j, (*chip, me[2]), sibling).start()

    def finish(self, ins, outs, sems):
        me, sibling, chips, copy = self._ctx(outs, sems)
        for t in range(self.n):
            copy(t, 0, sibling, me).wait_recv()
            for j, chip in enumerate(chips):
                copy(t, 4 + j, (*chip, 1 - me[2]), me).wait_recv()
        for t in range(self.n):
            for cp in self._first(ins, outs, sems, t):
                cp.wait_send()
            for j, chip in enumerate(chips):
                copy(t, 4 + j, (*chip, me[2]), sibling).wait_send()
            self._local(ins, outs, sems, t).wait()


class ExchangeRider:
    def __init__(self, tensors, plan, n_slots):
        n = len(tensors)
        self.n, self.plan = n, plan
        self.ins = list(tensors)
        self.out_shape = [jax.ShapeDtypeStruct((n_slots,) + s.shape[1:], s.dtype) for s in tensors]
        self.sems = [pltpu.SemaphoreType.DMA((n, n_slots)), pltpu.SemaphoreType.DMA((n, n_slots))]

    def _copies(self, ins, outs, sems):
        return [pltpu.make_async_remote_copy(
            src_ref=ins[t].at[block], dst_ref=outs[t].at[k], send_sem=sems[0].at[t, k], recv_sem=sems[1].at[t, k],
            device_id=to, device_id_type=MESH)
            for t in range(self.n) for k, (block, to) in enumerate(self.plan(*_position()))]

    def start(self, ins, outs, sems):
        for cp in self._copies(ins, outs, sems):
            cp.start()

    def mid(self, ins, outs, sems):
        pass

    def finish(self, ins, outs, sems):
        for cp in self._copies(ins, outs, sems):
            cp.wait()


class Riders:
    def __init__(self, riders):
        self.riders = list(riders)
        self.ins = [a for r in self.riders for a in r.ins]
        self.out_shape = [s for r in self.riders for s in r.out_shape]
        self.sems = [s for r in self.riders for s in r.sems]

    def _each(self, ins, outs, sems):
        i = o = s = 0
        for r in self.riders:
            ni, no, ns = len(r.ins), len(r.out_shape), len(r.sems)
            yield r, ins[i:i + ni], outs[o:o + no], sems[s:s + ns]
            i, o, s = i + ni, o + no, s + ns

    def start(self, ins, outs, sems):
        for r, a, b, c in self._each(ins, outs, sems):
            r.start(a, b, c)

    def mid(self, ins, outs, sems):
        for r, a, b, c in self._each(ins, outs, sems):
            r.mid(a, b, c)

    def finish(self, ins, outs, sems):
        for r, a, b, c in self._each(ins, outs, sems):
            r.finish(a, b, c)

    def split(self, outs):
        res, o = [], 0
        for r in self.riders:
            res.append(list(outs[o:o + len(r.out_shape)]))
            o += len(r.out_shape)
        return res


def pair_rider(grads):
    def plan(x, y, c):
        return [(4 * cx + 2 * cy + (1 - c), (x, y, 1 - c)) for cx in range(2) for cy in range(2)]
    return ExchangeRider(grads, plan, 4)


def chips_rider(parts):
    def plan(x, y, c):
        return [(2 * cx + cy, (cx, cy, c)) for cx, cy in [(1 - x, y), (x, 1 - y), (1 - x, 1 - y)]]
    return ExchangeRider(parts, plan, 3)


def run_alone(rider, space, name):
    ni = len(rider.ins)
    no = len(rider.out_shape)

    def body(*refs):
        ins, outs, sems = refs[:ni], refs[ni:ni + no], refs[ni + no:]
        rider.start(ins, outs, sems)
        rider.mid(ins, outs, sems)
        rider.finish(ins, outs, sems)

    spec = pl.BlockSpec(memory_space=space)
    return pl.pallas_call(
        body, name=name, in_specs=[spec] * ni, out_specs=[spec] * no, out_shape=rider.out_shape,
        scratch_shapes=rider.sems, compiler_params=pltpu.CompilerParams(vmem_limit_bytes=VMEM_LIMIT),
    )(*rider.ins)


def _grid_call(body, *, name, nsteps, in_specs, out_specs, out_shape, scratch_shapes, args, rider=None, aliases=None):
    aliases = aliases or {}
    if rider is None:
        outs = pl.pallas_call(body, name=name, grid=(nsteps,), in_specs=in_specs, out_specs=out_specs,
                              out_shape=out_shape, scratch_shapes=scratch_shapes, input_output_aliases=aliases,
                              compiler_params=_cp())(*args)
        return outs, []
    ni, no, ns = len(in_specs), len(out_specs), len(scratch_shapes)
    ri, ro = len(rider.ins), len(rider.out_shape)

    def wrapped(*refs):
        ins, refs = refs[:ni], refs[ni:]
        r_ins, refs = refs[:ri], refs[ri:]
        outs, refs = refs[:no], refs[no:]
        r_outs, refs = refs[:ro], refs[ro:]
        scratch, r_sems = refs[:ns], refs[ns:]
        s = pl.program_id(0)

        @pl.when(s == 0)
        def _():
            rider.start(r_ins, r_outs, r_sems)

        body(*ins, *outs, *scratch)

        @pl.when(s == (3 * nsteps) // 4)
        def _():
            rider.mid(r_ins, r_outs, r_sems)

        @pl.when(s == nsteps - 1)
        def _():
            rider.finish(r_ins, r_outs, r_sems)

    outs = pl.pallas_call(
        wrapped, name=name, grid=(nsteps,), in_specs=list(in_specs) + [ANY] * ri, out_specs=list(out_specs) + [ANY] * ro,
        out_shape=list(out_shape) + rider.out_shape, scratch_shapes=list(scratch_shapes) + rider.sems,
        input_output_aliases=aliases, compiler_params=_cp())(*args, *rider.ins)
    return outs[:no], outs[no:]


def _dot(a, b):
    return jnp.dot(a, b, preferred_element_type=F32)


def _dot_nt(a, b):
    return lax.dot_general(a, b, (((1,), (1,)), ((), ())), preferred_element_type=F32)


def _dot_tn(a, b):
    return lax.dot_general(a, b, (((0,), (0,)), ((), ())), preferred_element_type=F32)


def _sigmoid(x):
    return 1.0 / (1.0 + jnp.exp(-x))


def _kind(i):
    return jnp.where(i < CTX // TM, 0, 1)


def _sel(kind, mod_ref, k):
    return mod_ref[kind, k:k + 1, :]


def _acc2(ref, k, val, kind):
    ref[kind, k:k + 1, :] += jnp.sum(val, axis=0, keepdims=True)


def _norm_mod(h, g, shift, scale):
    r = lax.rsqrt(jnp.mean(h * h, axis=-1, keepdims=True) + EPS)
    n = h * r
    return (n * g) * (1.0 + scale) + shift, n, r


def _norm_mod_bwd(dz, n, r, g, scale):
    dn = dz * (g * (1.0 + scale))
    return r * (dn - n * jnp.mean(dn * n, axis=-1, keepdims=True))


def ffn_fwd(h, mod, g, win, wout, name, rider=None, loss=None):
    split = isinstance(h, tuple)
    nc = CTX // TM
    t = h[0].shape[0] + h[1].shape[0] if split else h.shape[0]

    def body(*refs):
        refs = list(refs)
        win_v, wout_v, a_v = refs[-3:]
        rows = refs[:2] if split else refs[:1]
        mod_ref, g_ref, win_hbm, wout_hbm = refs[len(rows):len(rows) + 4]
        rest = refs[len(rows) + 4:-3]
        if loss is not None:
            fg_ref, tgt_ref, rest = rest[0], rest[1], rest[2:]
        out_ref, gu_ref, acc_ref, rest = rest[0], rest[1], rest[2], rest[3:]
        i = pl.program_id(0)

        @pl.when(i == 0)
        def _():
            pltpu.sync_copy(win_hbm, win_v)
            pltpu.sync_copy(wout_hbm, wout_v)

        if split:
            hh = jnp.where(i < nc, rows[0][...], rows[1][...])
            rest[0][...] = hh
        else:
            hh = rows[0][...]
        ic = _kind(i)
        z, _, _ = _norm_mod(hh, g_ref[...], _sel(ic, mod_ref, 0), _sel(ic, mod_ref, 1))
        zb = z.astype(BF16)
        for j in range(D_FF // HT):
            gb, ub = slice(j * HT, (j + 1) * HT), slice(D_FF + j * HT, D_FF + (j + 1) * HT)
            gg = _dot_nt(zb, win_v[gb, :])
            uu = _dot_nt(zb, win_v[ub, :])
            gu_ref[:, gb] = gg.astype(BF16)
            gu_ref[:, ub] = uu.astype(BF16)
            a_v[:, gb] = ((gg * _sigmoid(gg)) * uu).astype(BF16)
        acc = _dot(a_v[...], wout_v[...])
        acc_ref[...] = acc
        hn = hh + (0.5 * _sel(ic, mod_ref, 2)) * acc
        if loss is None:
            out_ref[...] = hn
        else:
            loss_ref, dgf_ref = rest

            @pl.when(i == 0)
            def _():
                loss_ref[...] = jnp.zeros_like(loss_ref)
                dgf_ref[...] = jnp.zeros_like(dgf_ref)

            @pl.when(i < nc)
            def _():
                out_ref[...] = jnp.zeros_like(out_ref)

            @pl.when(i >= nc)
            def _():
                gain = fg_ref[...]
                r = lax.rsqrt(jnp.mean(hn * hn, axis=-1, keepdims=True) + EPS)
                n = hn * r
                err = n * gain - tgt_ref[...]
                loss_ref[...] += 0.5 * jnp.sum(jnp.mean(err * err, axis=-1, keepdims=True))
                dy = err * (1.0 / D)
                dgf_ref[...] += jnp.sum(dy * n, axis=0, keepdims=True)
                dn = dy * gain
                out_ref[...] = r * (dn - n * jnp.mean(dn * n, axis=-1, keepdims=True))

    row = pl.BlockSpec((TM, D), lambda i: (i, 0))
    vec = pl.BlockSpec((1, D), lambda i: (0, 0))
    rshape = jax.ShapeDtypeStruct((t, D), F32)
    if split:
        rows_in = [pl.BlockSpec((TM, D), lambda i: (jnp.minimum(i, nc - 1), 0)),
                   pl.BlockSpec((TM, D), lambda i: (jnp.maximum(i - nc, 0), 0))]
    else:
        rows_in = [row]
    in_specs = rows_in + [pl.BlockSpec((2, 3, D), lambda i: (0, 0, 0)), vec, ANY, ANY]
    out_specs = [row, pl.BlockSpec((TM, 2 * D_FF), lambda i: (i, 0)), row] + ([row] if split else [])
    out_shape = [rshape, jax.ShapeDtypeStruct((t, 2 * D_FF), BF16), rshape] + ([rshape] if split else [])
    args = (*(h if split else (h,)), mod, g, win, wout)
    if loss is not None:
        in_specs += [vec, pl.BlockSpec((TM, D), lambda i: (jnp.maximum(i - nc, 0), 0))]
        out_specs += [pl.BlockSpec((8, 128), lambda i: (0, 0)), vec]
        out_shape += [jax.ShapeDtypeStruct((8, 128), F32), jax.ShapeDtypeStruct((1, D), F32)]
        args += tuple(loss)
    return _grid_call(
        body, name=name, nsteps=t // TM, in_specs=in_specs, out_specs=out_specs, out_shape=out_shape,
        scratch_shapes=[pltpu.VMEM((2 * D_FF, D), BF16), pltpu.VMEM((D_FF, D), BF16), pltpu.VMEM((TM, D_FF), BF16)],
        args=args, rider=rider)


def ffn_bwd_a(dy, acc, gu, mod, wout, name, rider=None):
    t = dy.shape[0]
    nt = t // TM

    def body(dy_ref, acc_ref, gu_ref, mod_ref, wout_hbm, dp_ref, dwout_hbm, dgate_ref, wout_v, dwout_v):
        i = pl.program_id(0)

        @pl.when(i == 0)
        def _():
            pltpu.sync_copy(wout_hbm, wout_v)
            dwout_v[...] = jnp.zeros_like(dwout_v)
            dgate_ref[...] = jnp.zeros_like(dgate_ref)

        dy_ = dy_ref[...]
        ic = _kind(i)
        _acc2(dgate_ref, 0, 0.5 * dy_ * acc_ref[...], ic)
        daccb = ((0.5 * _sel(ic, mod_ref, 2)) * dy_).astype(BF16)
        for j in range(D_FF // HT):
            blk, ublk = slice(j * HT, (j + 1) * HT), slice(D_FF + j * HT, D_FF + (j + 1) * HT)
            da = _dot_nt(daccb, wout_v[blk, :])
            gg = gu_ref[:, blk].astype(F32)
            uu = gu_ref[:, ublk].astype(F32)
            s = _sigmoid(gg)
            sl = gg * s
            dwout_v[blk, :] += _dot_tn((sl * uu).astype(BF16), daccb)
            dp_ref[:, blk] = (da * uu * (s + sl * (1.0 - s))).astype(BF16)
            dp_ref[:, ublk] = (da * sl).astype(BF16)

        @pl.when(i == nt - 1)
        def _():
            pltpu.sync_copy(dwout_v, dwout_hbm)

    row = pl.BlockSpec((TM, D), lambda i: (i, 0))
    wide = pl.BlockSpec((TM, 2 * D_FF), lambda i: (i, 0))
    return _grid_call(
        body, name=name, nsteps=nt,
        in_specs=[row, row, wide, pl.BlockSpec((2, 3, D), lambda i: (0, 0, 0)), ANY],
        out_specs=[wide, ANY, pl.BlockSpec((2, 1, D), lambda i: (0, 0, 0))],
        out_shape=[jax.ShapeDtypeStruct((t, 2 * D_FF), BF16), jax.ShapeDtypeStruct((D_FF, D), F32),
                   jax.ShapeDtypeStruct((2, 1, D), F32)],
        scratch_shapes=[pltpu.VMEM((D_FF, D), BF16), pltpu.VMEM((D_FF, D), F32)],
        args=(dy, acc, gu, mod, wout), rider=rider)


def ffn_bwd_b(dy, h, dp, mod, g, win, name, rider=None, latent_only=False):
    t = dy.shape[0]
    nt = t // TM
    nc = CTX // TM

    def body(dy_ref, h_ref, dp_ref, mod_ref, g_ref, win_hbm, dh_ref, dwin_hbm, dmod_ref, dg_ref, win_v, dwin_v):
        i = pl.program_id(0)

        @pl.when(i == 0)
        def _():
            pltpu.sync_copy(win_hbm, win_v)
            dwin_v[...] = jnp.zeros_like(dwin_v)
            dmod_ref[...] = jnp.zeros_like(dmod_ref)
            dg_ref[...] = jnp.zeros_like(dg_ref)

        ic = _kind(i)
        gain = g_ref[...]
        scale = _sel(ic, mod_ref, 1)
        z, n, r = _norm_mod(h_ref[...], gain, _sel(ic, mod_ref, 0), scale)
        zb = z.astype(BF16)
        dz = _dot(dp_ref[...], win_v[...])
        for j in range(2 * D_FF // WT):
            blk = slice(j * WT, (j + 1) * WT)
            dwin_v[blk, :] += _dot_tn(dp_ref[:, blk], zb)
        _acc2(dmod_ref, 0, dz, ic)
        _acc2(dmod_ref, 1, dz * (n * gain), ic)
        dg_ref[...] += jnp.sum(dz * (1.0 + scale) * n, axis=0, keepdims=True)
        dh_ref[...] = dy_ref[...] + _norm_mod_bwd(dz, n, r, gain, scale)

        @pl.when(i == nt - 1)
        def _():
            pltpu.sync_copy(dwin_v, dwin_hbm)

    row = pl.BlockSpec((TM, D), lambda i: (i, 0))
    if latent_only:
        dh_spec = pl.BlockSpec((TM, D), lambda i: (jnp.maximum(i - nc, 0), 0))
        dh_shape = jax.ShapeDtypeStruct((t - CTX, D), F32)
    else:
        dh_spec, dh_shape = row, jax.ShapeDtypeStruct((t, D), F32)
    return _grid_call(
        body, name=name, nsteps=nt,
        in_specs=[row, row, pl.BlockSpec((TM, 2 * D_FF), lambda i: (i, 0)),
                  pl.BlockSpec((2, 3, D), lambda i: (0, 0, 0)), pl.BlockSpec((1, D), lambda i: (0, 0)), ANY],
        out_specs=[dh_spec, ANY, pl.BlockSpec((2, 2, D), lambda i: (0, 0, 0)), pl.BlockSpec((1, D), lambda i: (0, 0))],
        out_shape=[dh_shape, jax.ShapeDtypeStruct((2 * D_FF, D), F32),
                   jax.ShapeDtypeStruct((2, 2, D), F32), jax.ShapeDtypeStruct((1, D), F32)],
        scratch_shapes=[pltpu.VMEM((2 * D_FF, D), BF16), pltpu.VMEM((2 * D_FF, D), F32)],
        args=(dy, h, dp, mod, g, win), rider=rider)


def ffn_bwd_dw(h, dp, mod, g, name, rider=None):
    t = h.shape[0]
    nt = t // TM

    def body(h_ref, dp_ref, mod_ref, g_ref, dwin_hbm, dwin_v):
        i = pl.program_id(0)

        @pl.when(i == 0)
        def _():
            dwin_v[...] = jnp.zeros_like(dwin_v)

        zs = []
        for k in range(SUB):
            ic = _kind(SUB * i + k)
            z, _, _ = _norm_mod(h_ref[pl.ds(k * TM, TM), :], g_ref[...], _sel(ic, mod_ref, 0), _sel(ic, mod_ref, 1))
            zs.append(z.astype(BF16))
        zb = jnp.concatenate(zs, axis=0)
        for j in range(2 * D_FF // WT):
            blk = slice(j * WT, (j + 1) * WT)
            dwin_v[blk, :] += _dot_tn(dp_ref[:, blk], zb)

        @pl.when(i == nt // SUB - 1)
        def _():
            pltpu.sync_copy(dwin_v, dwin_hbm)

    return _grid_call(
        body, name=name, nsteps=nt // SUB,
        in_specs=[pl.BlockSpec((TMX, D), lambda i: (i, 0)), pl.BlockSpec((TMX, 2 * D_FF), lambda i: (i, 0)),
                  pl.BlockSpec((2, 3, D), lambda i: (0, 0, 0)), pl.BlockSpec((1, D), lambda i: (0, 0))],
        out_specs=[ANY], out_shape=[jax.ShapeDtypeStruct((2 * D_FF, D), F32)],
        scratch_shapes=[pltpu.VMEM((2 * D_FF, D), F32)],
        args=(h, dp, mod, g), rider=rider)


def ffn_bwd_dh(dy, h, dp, mod, g, win, name, blocks, carry=None, rider=None):
    t = dy.shape[0]
    b0, b1 = blocks

    def body(*refs):
        if carry is None:
            dy_ref, h_ref, dp_ref, mod_ref, g_ref, win_hbm, dh_ref, dmod_ref, dg_ref, win_v = refs
        else:
            dy_ref, h_ref, dp_ref, mod_ref, g_ref, win_hbm, _, dmod0_ref, dg0_ref, dh_ref, dmod_ref, dg_ref, win_v = refs
        i = pl.program_id(0)

        @pl.when(i == 0)
        def _():
            pltpu.sync_copy(win_hbm, win_v)
            dmod_ref[...] = jnp.zeros_like(dmod_ref) if carry is None else dmod0_ref[...]
            dg_ref[...] = jnp.zeros_like(dg_ref) if carry is None else dg0_ref[...]

        gain = g_ref[...]
        dz_all = _dot(dp_ref[...], win_v[...])
        for k in range(SUB):
            ic = _kind(SUB * (i + b0) + k)
            rows = pl.ds(k * TM, TM)
            scale = _sel(ic, mod_ref, 1)
            _, n, r = _norm_mod(h_ref[rows, :], gain, _sel(ic, mod_ref, 0), scale)
            dz = dz_all[k * TM:(k + 1) * TM]
            _acc2(dmod_ref, 0, dz, ic)
            _acc2(dmod_ref, 1, dz * (n * gain), ic)
            dg_ref[...] += jnp.sum(dz * (1.0 + scale) * n, axis=0, keepdims=True)
            dh_ref[rows, :] = dy_ref[rows, :] + _norm_mod_bwd(dz, n, r, gain, scale)

    row = pl.BlockSpec((TMX, D), lambda i: (i + b0, 0))
    small = [pl.BlockSpec((2, 2, D), lambda i: (0, 0, 0)), pl.BlockSpec((1, D), lambda i: (0, 0))]
    in_specs = [row, row, pl.BlockSpec((TMX, 2 * D_FF), lambda i: (i + b0, 0)),
                pl.BlockSpec((2, 3, D), lambda i: (0, 0, 0)), pl.BlockSpec((1, D), lambda i: (0, 0)), ANY]
    args = (dy, h, dp, mod, g, win)
    if carry is not None:
        in_specs += [ANY] + small
        args += tuple(carry)
    return _grid_call(
        body, name=name, nsteps=b1 - b0, in_specs=in_specs, out_specs=[row] + small,
        out_shape=[jax.ShapeDtypeStruct((t, D), F32), jax.ShapeDtypeStruct((2, 2, D), F32),
                   jax.ShapeDtypeStruct((1, D), F32)],
        scratch_shapes=[pltpu.VMEM((2 * D_FF, D), BF16)],
        args=args, rider=rider, aliases=None if carry is None else {6: 0})


def mix_in_fwd(h, mod, g, wmix, name):
    t = h.shape[0]

    def body(h_ref, mod_ref, g_ref, w_hbm, p_ref, w_v):
        i = pl.program_id(0)

        @pl.when(i == 0)
        def _():
            pltpu.sync_copy(w_hbm, w_v)

        zs = []
        for k in range(SUB):
            ic = _kind(SUB * i + k)
            z, _, _ = _norm_mod(h_ref[pl.ds(k * TM, TM), :], g_ref[...], _sel(ic, mod_ref, 0), _sel(ic, mod_ref, 1))
            zs.append(z.astype(BF16))
        zb = jnp.concatenate(zs, axis=0)
        for dd in range(N_DEV):
            p_ref[:, dd * N_MIX_SHARD:(dd + 1) * N_MIX_SHARD] = _dot(zb, w_v[dd])

    return pl.pallas_call(
        body, name=name, grid=(t // TMX,),
        in_specs=[pl.BlockSpec((TMX, D), lambda i: (i, 0)), pl.BlockSpec((2, 3, D), lambda i: (0, 0, 0)),
                  pl.BlockSpec((1, D), lambda i: (0, 0)), ANY],
        out_specs=pl.BlockSpec((TMX, IN_PROJ), lambda i: (i, 0)),
        out_shape=jax.ShapeDtypeStruct((t, IN_PROJ), F32),
        scratch_shapes=[pltpu.VMEM((N_DEV, D, N_MIX_SHARD), BF16)],
        compiler_params=_cp(),
    )(h, mod, g, wmix)


def _halo_specs(nt, tile_of, rows=TM):
    nb = nt * (rows // HALO)
    main = pl.BlockSpec((rows, LRU_W), lambda s: (tile_of(s), 0))
    prev = pl.BlockSpec((HALO, LRU_W), lambda s: (jnp.maximum(tile_of(s) * (rows // HALO) - 1, 0), 0))
    nxt = pl.BlockSpec((HALO, LRU_W), lambda s: (jnp.minimum((tile_of(s) + 1) * (rows // HALO), nb - 1), 0))
    return main, prev, nxt


def _ext(tile, nt, main, prev, nxt):
    has_prev = jnp.logical_and(tile != 0, tile != 1)
    has_next = jnp.logical_and(tile != 0, tile != nt - 1)
    return jnp.concatenate([jnp.where(has_prev, prev, 0.0), main, jnp.where(has_next, nxt, 0.0)], axis=0)


def _shifted(ext, off):
    n = ext.shape[0]
    return pltpu.roll(ext, (-off) % n, 0)[HALO:HALO + TM]


def _conv(ext, cw_ref, cb_ref):
    xc = cb_ref[...] + cw_ref[0:1, :] * _shifted(ext, -2)
    for k in range(1, CONV_W):
        xc = xc + cw_ref[k:k + 1, :] * _shifted(ext, k - 2)
    return xc


def _log1p(y):
    return jnp.where(y < 1e-2, y * (1.0 - y * (0.5 - y * (1.0 / 3.0 - 0.25 * y))), jnp.log(1.0 + y))


def _softplus_neg(lam):
    return jnp.maximum(-lam, 0.0) + _log1p(jnp.exp(-jnp.abs(lam)))


def _one_minus_exp(x, exp_half):
    p = x * (1.0 + x * (1 / 2 + x * (1 / 6 + x * (1 / 24))))
    return jnp.where(x > -0.1, -p, 1.0 - exp_half * exp_half)


def _gates(xc, wr, br, wi, bi, lam):
    xb = xc.astype(BF16)
    r = _sigmoid(_dot(xb, wr) + br)
    ig = _sigmoid(_dot(xb, wi) + bi)
    sp = _softplus_neg(lam)
    log_a = -RG_C * r * sp
    a = jnp.exp(log_a)
    mult = jnp.sqrt(_one_minus_exp(2.0 * log_a, a))
    return r, ig, sp, a, mult


def _scan(a, b, reverse):
    n = a.shape[0]
    row = lax.broadcasted_iota(jnp.int32, a.shape, 0)
    s = 1
    while s < n:
        if s < HALO:
            if reverse:
                keep = row < n - s
                a_s = jnp.where(keep, pltpu.roll(a, n - s, 0), 1.0)
                b_s = jnp.where(keep, pltpu.roll(b, n - s, 0), 0.0)
            else:
                keep = row >= s
                a_s = jnp.where(keep, pltpu.roll(a, s, 0), 1.0)
                b_s = jnp.where(keep, pltpu.roll(b, s, 0), 0.0)
            b = a * b_s + b
            a = a * a_s
        elif reverse:
            b = jnp.concatenate([a[:n - s] * b[s:] + b[:n - s], b[n - s:]], axis=0)
            a = jnp.concatenate([a[:n - s] * a[s:], a[n - s:]], axis=0)
        else:
            b = jnp.concatenate([b[:s], a[s:] * b[:n - s] + b[s:]], axis=0)
            a = jnp.concatenate([a[:s], a[s:] * a[:n - s]], axis=0)
        s *= 2
    return a, b


def _direction_specs(d):
    return (pl.BlockSpec((None, 1, LRU_W), lambda s: (d, 0, 0)), pl.BlockSpec((None, LRU_W, LRU_W), lambda s: (d, 0, 0)))


def lru_fwd(p, conv_w, conv_b, wr, br, wi, bi, lam, reverse, name, rider=None):
    t = p.shape[0]
    nt = t // TM

    def tile_of(s):
        return jnp.where(s == 0, 0, nt - s) if reverse else s

    def body(x_ref, xp_ref, xn_ref, cw_ref, cb_ref, wr_ref, br_ref, wi_ref, bi_ref, lam_ref, h_ref, carry):
        s = pl.program_id(0)
        tile = tile_of(s)

        @pl.when(s == 0)
        def _():
            carry[...] = jnp.zeros_like(carry)

        ext = _ext(tile, nt, x_ref[...], xp_ref[...], xn_ref[...])
        xc = _conv(ext, cw_ref, cb_ref)
        _, ig, _, a, mult = _gates(xc, wr_ref[...], br_ref[...], wi_ref[...], bi_ref[...], lam_ref[...])
        a_cum, hl = _scan(a, mult * (ig * xc), reverse)
        hh = hl + a_cum * carry[...]
        h_ref[...] = hh
        carry[...] = hh[0:1, :] if reverse else hh[TM - 1:TM, :]

    main, prev, nxt = _halo_specs(nt, tile_of)
    vec = pl.BlockSpec((1, LRU_W), lambda s: (0, 0))
    dvec, dmat = _direction_specs(int(reverse))
    outs, got = _grid_call(
        body, name=name, nsteps=nt,
        in_specs=[main, prev, nxt, pl.BlockSpec((CONV_W, LRU_W), lambda s: (0, 0)), vec, dmat, dvec, dmat, dvec, dvec],
        out_specs=[main],
        out_shape=[jax.ShapeDtypeStruct((t, LRU_W), F32)],
        scratch_shapes=[pltpu.VMEM((1, LRU_W), F32)],
        args=(p, p, p, conv_w, conv_b, wr, br, wi, bi, lam), rider=rider)
    return outs[0], got


def lru_bwd(p, hs, dhs, conv_w, conv_b, wr, br, wi, bi, lam, reverse, name):
    t = p.shape[0]
    nt = t // TM
    bpt = TM // HALO

    def tile_of(s):
        return jnp.where(s == nt - 1, 0, s + 1) if reverse else nt - 1 - s

    def hprev_block(s):
        tile = tile_of(s)
        if reverse:
            return (jnp.where(tile == nt - 1, 0, jnp.minimum((tile + 1) * bpt, nt * bpt - 1)), 0)
        return (jnp.maximum(tile * bpt - 1, 0), 0)

    def body(x_ref, xp_ref, xn_ref, h_ref, hp_ref, dh_ref, cw_ref, cb_ref, wr_ref, br_ref, wi_ref, bi_ref, lam_ref,
             dxc_ref, dwr_out, dwi_out, dbr_ref, dbi_ref, dlam_ref, carry, dwr_ref, dwi_ref):
        s = pl.program_id(0)
        tile = tile_of(s)

        @pl.when(s == 0)
        def _():
            carry[...] = jnp.zeros_like(carry)
            for ref in (dwr_ref, dwi_ref, dbr_ref, dbi_ref, dlam_ref):
                ref[...] = jnp.zeros_like(ref)

        ext = _ext(tile, nt, x_ref[...], xp_ref[...], xn_ref[...])
        xc = _conv(ext, cw_ref, cb_ref)
        wr_, wi_ = wr_ref[...], wi_ref[...]
        r, ig, sp, a, mult = _gates(xc, wr_, br_ref[...], wi_, bi_ref[...], lam_ref[...])
        gated = ig * xc
        row = lax.broadcasted_iota(jnp.int32, (TM, LRU_W), 0)
        hh = h_ref[...]
        start = jnp.where(tile != 0, hp_ref[0:1, :] if reverse else hp_ref[HALO - 1:HALO, :], 0.0)
        if reverse:
            edge = row == TM - 1
            hprev = jnp.where(edge, start, pltpu.roll(hh, TM - 1, 0))
            coef = jnp.where(row == 0, 0.0, pltpu.roll(a, 1, 0))
            bb = dh_ref[...] + jnp.where(row == 0, carry[...], 0.0)
        else:
            edge = row == 0
            hprev = jnp.where(edge, start, pltpu.roll(hh, 1, 0))
            coef = jnp.where(row == TM - 1, 0.0, pltpu.roll(a, TM - 1, 0))
            bb = dh_ref[...] + jnp.where(row == TM - 1, carry[...], 0.0)
        _, lmb = _scan(coef, bb, not reverse)
        al = a * lmb
        carry[...] = al[TM - 1:TM, :] if reverse else al[0:1, :]

        dgated = lmb * mult
        dloga = (lmb * hprev) * a - (lmb * gated) * (a * a) / mult
        dpre_r = (dloga * (-RG_C * sp)) * r * (1.0 - r)
        dpre_i = (dgated * xc) * ig * (1.0 - ig)
        drb, dib = dpre_r.astype(BF16), dpre_i.astype(BF16)
        xb = xc.astype(BF16)
        dxc_ref[...] = dgated * ig + _dot_nt(drb, wr_) + _dot_nt(dib, wi_)
        dwr_ref[...] += _dot_tn(xb, drb)
        dwi_ref[...] += _dot_tn(xb, dib)
        dbr_ref[...] += jnp.sum(dpre_r, axis=0, keepdims=True)
        dbi_ref[...] += jnp.sum(dpre_i, axis=0, keepdims=True)
        dlam_ref[...] += jnp.sum(dloga * (-RG_C * r), axis=0, keepdims=True)

        @pl.when(s == nt - 1)
        def _():
            dlam_ref[...] = dlam_ref[...] * (-_sigmoid(-lam_ref[...]))
            for hd in range(HEADS):
                blk = pl.ds(hd * HEAD_DIM, HEAD_DIM)
                dwr_out[hd] = dwr_ref[blk, blk]
                dwi_out[hd] = dwi_ref[blk, blk]

    main, prev, nxt = _halo_specs(nt, tile_of)
    vec = pl.BlockSpec((1, LRU_W), lambda s: (0, 0))
    dvec, dmat = _direction_specs(int(reverse))
    heads =pl.BlockSpec((HEADS, HEAD_DIM, HEAD_DIM), lambda s: (0, 0, 0))
    vshape = jax.ShapeDtypeStruct((1, LRU_W), F32)
    hshape = jax.ShapeDtypeStruct((HEADS, HEAD_DIM, HEAD_DIM), F32)
    return pl.pallas_call(
        body, name=name, grid=(nt,),
        in_specs=[main, prev, nxt, main, pl.BlockSpec((HALO, LRU_W), hprev_block), main,
                  pl.BlockSpec((CONV_W, LRU_W), lambda s: (0, 0)), vec, dmat, dvec, dmat, dvec, dvec],
        out_specs=[main, heads, heads, vec, vec, vec],
        out_shape=[jax.ShapeDtypeStruct((t, LRU_W), F32), hshape, hshape, vshape, vshape, vshape],
        scratch_shapes=[pltpu.VMEM((1, LRU_W), F32), pltpu.VMEM((LRU_W, LRU_W), F32), pltpu.VMEM((LRU_W, LRU_W), F32)],
        compiler_params=_cp(),
    )(p, p, p, hs, hs, dhs, conv_w, conv_b, wr, br, wi, bi, lam)


GELU_C = 0.7978845608028654
GELU_A = 0.044715


def _gelu(x):
    th = jnp.tanh(GELU_C * (x + GELU_A * x * x * x))
    return 0.5 * x * (1.0 + th), th


def _sgu(v, gain, w_ref, bt_ref):
    mu = jnp.mean(v, axis=-1, keepdims=True)
    xc = v - mu
    rs = lax.rsqrt(jnp.mean(xc * xc, axis=-1, keepdims=True) + EPS)
    vhat = xc * rs
    vnb = (vhat * gain).astype(BF16)
    chunks = []
    for ch in range(TM // CHUNK):
        zs = []
        for gi in range(GROUPS):
            vb = vnb[ch * CHUNK:(ch + 1) * CHUNK, gi * GROUP_DIM:(gi + 1) * GROUP_DIM]
            zs.append(_dot(w_ref[gi].astype(BF16), vb) + bt_ref[:, gi:gi + 1])
        chunks.append(jnp.concatenate(zs, axis=1))
    return jnp.concatenate(chunks, axis=0), vhat, rs, vnb


def _pcols(k, rows=TM):
    return pl.BlockSpec((rows, LRU_W), lambda i: (i, k))


def mix_out_fwd(h, p, hf, hb, mod, sgu_g, sgu_w, sgu_bt, womix, name):
    t = h.shape[0]

    def body(h_ref, gl_ref, u_ref, v_ref, hf_ref, hb_ref, mod_ref, sg_ref, sw_ref, sb_ref, w_hbm, out_ref, o_ref, w_v):
        i = pl.program_id(0)

        @pl.when(i == 0)
        def _():
            pltpu.sync_copy(w_hbm, w_v)

        ys = []
        for k in range(SUB):
            rows = pl.ds(k * TM, TM)
            ge, _ = _gelu(gl_ref[rows, :])
            y_lru = (hf_ref[rows, :] + hb_ref[rows, :]) * ge
            z, _, _, _ = _sgu(v_ref[rows, :], sg_ref[...], sw_ref, sb_ref)
            ys.append(jnp.concatenate([y_lru, u_ref[rows, :] * z], axis=1).astype(BF16))
        o = _dot(jnp.concatenate(ys, axis=0), w_v[...])
        o_ref[...] = o
        for k in range(SUB):
            rows = pl.ds(k * TM, TM)
            out_ref[rows, :] = h_ref[rows, :] + _sel(_kind(SUB * i + k), mod_ref, 2) * o[k * TM:(k + 1) * TM]

    row = pl.BlockSpec((TMX, D), lambda i: (i, 0))
    half = pl.BlockSpec((TMX, LRU_W), lambda i: (i, 0))
    return pl.pallas_call(
        body, name=name, grid=(t // TMX,),
        in_specs=[row, _pcols(1, TMX), _pcols(2, TMX), _pcols(3, TMX), half, half, pl.BlockSpec((2, 3, D), lambda i: (0, 0, 0)),
                  pl.BlockSpec((1, MLP_W), lambda i: (0, 0)), pl.BlockSpec((GROUPS, CHUNK, CHUNK), lambda i: (0, 0, 0)),
                  pl.BlockSpec((CHUNK, GROUPS), lambda i: (0, 0)), ANY],
        out_specs=[row, row],
        out_shape=[jax.ShapeDtypeStruct((t, D), F32), jax.ShapeDtypeStruct((t, D), F32)],
        scratch_shapes=[pltpu.VMEM((D, D), BF16)],
        compiler_params=_cp(),
    )(h, p, p, p, hf, hb, mod, sgu_g, sgu_w, sgu_bt, womix)


def mix_out_bwd(dy, p, hf, hb, o, mod, sgu_g, sgu_w, sgu_bt, womix, name, rider=None):
    t = dy.shape[0]

    def body(dy_ref, gl_ref, u_ref, v_ref, hf_ref, hb_ref, o_ref, mod_ref, sg_ref, sw_ref, sb_ref, w_hbm,
             dhs_ref, dp_ref, dw_ref, dgate_ref, dsg_ref, dsw_ref, dsb_ref, w_v):
        i = pl.program_id(0)

        @pl.when(i == 0)
        def _():
            pltpu.sync_copy(w_hbm, w_v)
            for ref in (dw_ref, dgate_ref, dsg_ref, dsw_ref, dsb_ref):
                ref[...] = jnp.zeros_like(ref)

        ys, dobs = [], []
        for k in range(SUB):
            rows = pl.ds(k * TM, TM)
            ic = _kind(SUB * i + k)
            dy_ = dy_ref[rows, :]
            _acc2(dgate_ref, 0, dy_ * o_ref[rows, :], ic)
            dob = (_sel(ic, mod_ref, 2) * dy_).astype(BF16)

            gl = gl_ref[rows, :]
            ge, th = _gelu(gl)
            hsum = hf_ref[rows, :] + hb_ref[rows, :]
            gain = sg_ref[...]
            uu = u_ref[rows, :]
            z, vhat, rs, vnb = _sgu(v_ref[rows, :], gain, sw_ref, sb_ref)
            ys.append(jnp.concatenate([hsum * ge, uu * z], axis=1).astype(BF16))
            dobs.append(dob)
            dyy = _dot_nt(dob, w_v[...])
            dyl, dys = dyy[:, :LRU_W], dyy[:, LRU_W:]

            dhs_ref[rows, :] = dyl * ge
            dge = 0.5 * (1.0 + th) + 0.5 * gl * (1.0 - th * th) * (GELU_C * (1.0 + 3.0 * GELU_A * gl * gl))
            dp_ref[rows, 0:LRU_W] = (dyl * hsum * dge).astype(BF16)
            dp_ref[rows, LRU_W:2 * LRU_W] = (dys * z).astype(BF16)

            dz = dys * uu
            dzb = dz.astype(BF16)
            dvn_chunks, dsb_cols = [], [jnp.zeros((CHUNK, 1), F32)] * GROUPS
            for ch in range(TM // CHUNK):
                cols = []
                for gi in range(GROUPS):
                    rs_, cs_ = slice(ch * CHUNK, (ch + 1) * CHUNK), slice(gi * GROUP_DIM, (gi + 1) * GROUP_DIM)
                    dzg = dzb[rs_, cs_]
                    dsb_cols[gi] = dsb_cols[gi] + jnp.sum(dz[rs_, cs_], axis=1, keepdims=True)
                    dsw_ref[gi] += _dot_nt(dzg, vnb[rs_, cs_])
                    cols.append(_dot_tn(sw_ref[gi].astype(BF16), dzg))
                dvn_chunks.append(jnp.concatenate(cols, axis=1))
            dsb_ref[...] += jnp.concatenate(dsb_cols, axis=1)
            dvn = jnp.concatenate(dvn_chunks, axis=0)
            dsg_ref[...] += jnp.sum(dvn * vhat, axis=0, keepdims=True)
            dvh = dvn * gain
            dv = rs * (dvh - jnp.mean(dvh, axis=-1, keepdims=True) - vhat * jnp.mean(dvh * vhat, axis=-1, keepdims=True))
            dp_ref[rows, 2 * LRU_W:3 * LRU_W] = dv.astype(BF16)
        dw_ref[...] += _dot_tn(jnp.concatenate(ys, axis=0), jnp.concatenate(dobs, axis=0))

    row = pl.BlockSpec((TMX, D), lambda i: (i, 0))
    half = pl.BlockSpec((TMX, LRU_W), lambda i: (i, 0))
    const2 = lambda i: (0, 0)
    const3 = lambda i: (0, 0, 0)
    return _grid_call(
        body, name=name, nsteps=t // TMX,
        in_specs=[row, _pcols(1, TMX), _pcols(2, TMX), _pcols(3, TMX), half, half, row, pl.BlockSpec((2, 3, D), const3),
                  pl.BlockSpec((1, MLP_W), const2), pl.BlockSpec((GROUPS, CHUNK, CHUNK), const3),
                  pl.BlockSpec((CHUNK, GROUPS), const2), ANY],
        out_specs=[half, pl.BlockSpec((TMX, 3 * LRU_W), lambda i: (i, 0)), pl.BlockSpec((D, D), const2),
                   pl.BlockSpec((2, 1, D), const3), pl.BlockSpec((1, MLP_W), const2),
                   pl.BlockSpec((GROUPS, CHUNK, CHUNK), const3), pl.BlockSpec((CHUNK, GROUPS), const2)],
        out_shape=[jax.ShapeDtypeStruct((t, LRU_W), F32), jax.ShapeDtypeStruct((t, 3 * LRU_W), BF16),
                   jax.ShapeDtypeStruct((D, D), F32), jax.ShapeDtypeStruct((2, 1, D), F32),
                   jax.ShapeDtypeStruct((1, MLP_W), F32), jax.ShapeDtypeStruct((GROUPS, CHUNK, CHUNK), F32),
                   jax.ShapeDtypeStruct((CHUNK, GROUPS), F32)],
        scratch_shapes=[pltpu.VMEM((D, D), BF16)],
        args=(dy, p, p, p, hf, hb, o, mod, sgu_g, sgu_w, sgu_bt, womix), rider=rider)


def mix_in_bwd(dy, h, p, dxf, dxb, dprest, mod, g, conv_w, wmix, name, rider=None):
    t = dy.shape[0]
    nt = t // TM

    def body(dy_ref, h_ref, x_ref, xp_ref, xn_ref, f_ref, fp_ref, fn_ref, b_ref, bp_ref, bn_ref, dpr_ref, mod_ref,
             g_ref, cw_ref, w_hbm, dh_ref, dw_hbm, dmod_ref, dg_ref, dcw_ref, dcb_ref, w_v, dw_v):
        i = pl.program_id(0)

        @pl.when(i == 0)
        def _():
            pltpu.sync_copy(w_hbm, w_v)
            dw_v[...] = jnp.zeros_like(dw_v)
            for ref in (dmod_ref, dg_ref, dcw_ref, dcb_ref):
                ref[...] = jnp.zeros_like(ref)

        def around(ref, before, after, k):
            prev = ref[pl.ds(k * TM - HALO, HALO), :] if k > 0 else before[...]
            nxt = ref[pl.ds((k + 1) * TM, HALO), :] if k < SUB - 1 else after[...]
            return ref[pl.ds(k * TM, TM), :], prev, nxt

        gain = g_ref[...]
        zs, dps, rs = [], [], []
        for k in range(SUB):
            f_m, f_p, f_n = around(f_ref, fp_ref, fn_ref, k)
            b_m, b_p, b_n = around(b_ref, bp_ref, bn_ref, k)
            dmain = f_m + b_m
            dext = _ext(SUB * i + k, nt, dmain, f_p + b_p, f_n + b_n)
            xext = _ext(SUB * i + k, nt, *around(x_ref, xp_ref, xn_ref, k))
            dxl = cw_ref[0:1, :] * _shifted(dext, 2)
            for tap in range(1, CONV_W):
                dxl = dxl + cw_ref[tap:tap + 1, :] * _shifted(dext, 2 - tap)
            dcw_ref[...] += jnp.concatenate(
                [jnp.sum(dmain * _shifted(xext, tap - 2), axis=0, keepdims=True) for tap in range(CONV_W)], axis=0)
            dcb_ref[...] += jnp.sum(dmain, axis=0, keepdims=True)

            ic = _kind(SUB * i + k)
            rows = pl.ds(k * TM, TM)
            z, _, r = _norm_mod(h_ref[rows, :], gain, _sel(ic, mod_ref, 0), _sel(ic, mod_ref, 1))
            zs.append(z.astype(BF16))
            rs.append(r)
            dps.append(jnp.concatenate([dxl.astype(BF16), dpr_ref[rows, :]], axis=1))
        zb = jnp.concatenate(zs, axis=0)
        dpb = jnp.concatenate(dps, axis=0)
        dz_all = jnp.zeros((TMX, D), F32)
        for dd in range(N_DEV):
            dpd = dpb[:, dd * N_MIX_SHARD:(dd + 1) * N_MIX_SHARD]
            dz_all = dz_all + _dot_nt(dpd, w_v[dd])
            dw_v[dd] += _dot_tn(zb, dpd)
        for k in range(SUB):
            ic = _kind(SUB * i + k)
            rows = pl.ds(k * TM, TM)
            scale = _sel(ic, mod_ref, 1)
            dz = dz_all[k * TM:(k + 1) * TM]
            n = h_ref[rows, :] * rs[k]
            _acc2(dmod_ref, 0, dz, ic)
            _acc2(dmod_ref, 1, dz * (n * gain), ic)
            dg_ref[...] += jnp.sum(dz * (1.0 + scale) * n, axis=0, keepdims=True)
            dh_ref[rows, :] = dy_ref[rows, :] + _norm_mod_bwd(dz, n, rs[k], gain, scale)

        @pl.when(i == nt // SUB - 1)
        def _():
            pltpu.sync_copy(dw_v, dw_hbm)

    main, prev, nxt = _halo_specs(nt // SUB, lambda s: s, TMX)
    row = pl.BlockSpec((TMX, D), lambda i: (i, 0))
    const2 = lambda i: (0, 0)
    return _grid_call(
        body, name=name, nsteps=nt // SUB,
        in_specs=[row, row, main, prev, nxt, main, prev, nxt, main, prev, nxt,
                  pl.BlockSpec((TMX, 3 * LRU_W), lambda i: (i, 0)), pl.BlockSpec((2, 3, D), lambda i: (0, 0, 0)),
                  pl.BlockSpec((1, D), const2), pl.BlockSpec((CONV_W, LRU_W), const2), ANY],
        out_specs=[row, ANY, pl.BlockSpec((2, 2, D), lambda i: (0, 0, 0)), pl.BlockSpec((1, D), const2),
                   pl.BlockSpec((CONV_W, LRU_W), const2), pl.BlockSpec((1, LRU_W), const2)],
        out_shape=[jax.ShapeDtypeStruct((t, D), F32), jax.ShapeDtypeStruct((N_DEV, D, N_MIX_SHARD), F32),
                   jax.ShapeDtypeStruct((2, 2, D), F32), jax.ShapeDtypeStruct((1, D), F32),
                   jax.ShapeDtypeStruct((CONV_W, LRU_W), F32), jax.ShapeDtypeStruct((1, LRU_W), F32)],
        scratch_shapes=[pltpu.VMEM((N_DEV, D, N_MIX_SHARD), BF16), pltpu.VMEM((N_DEV, D, N_MIX_SHARD), F32)],
        args=(dy, h, p, p, p, dxf, dxf, dxf, dxb, dxb, dxb, dprest, mod, g, conv_w, wmix), rider=rider)


def _block_diag(w):
    eye = jnp.eye(HEADS, dtype=w.dtype)
    return jnp.einsum("dhij,hk->dhikj", w, eye).reshape(2, LRU_W, LRU_W)


def small_layer(g1, gm, g2, conv_w, conv_b, w_r, b_r, w_i, b_i, lam, sgu_g, sgu_w, sgu_b):
    return dict(g1=g1[None, :], gm=gm[None, :], g2=g2[None, :], conv_w=conv_w, conv_b=conv_b[None, :],
                wr=_block_diag(w_r).astype(BF16), br=b_r[:, None, :], wi=_block_diag(w_i).astype(BF16),
                bi=b_i[:, None, :], lam=lam[:, None, :], sgu_g=sgu_g[None, :], sgu_w=sgu_w, sgu_bt=sgu_b.T)


def small_grads(g):
    out = dict(mix_norm_g=g["gm"][0], ffn2_norm_g=g["g2"][0], lru_conv_w=g["conv_w"],
               lru_conv_b=g["conv_b"][0], lru_w_r=g["wr"], lru_b_r=g["br"][:, 0, :],
               lru_w_i=g["wi"], lru_b_i=g["bi"][:, 0, :], lru_lambda=g["lam"][:, 0, :],
               sgu_norm_g=g["sgu_g"][0], sgu_w=g["sgu_w"], sgu_b=g["sgu_bt"].T)
    if "g1" in g:
        out["ffn1_norm_g"] = g["g1"][0]
    return out


def _as_blocks(key, g):
    return g if key == "wmix" else g.reshape(N_DEV, g.shape[0] // N_DEV, D)


def _gathered(key, a):
    return a if key == "wmix" else a.reshape(N_DEV * a.shape[1], D)


class _ReduceScatter:
    def __init__(self, c_idx, where):
        self.c_idx, self.where = c_idx, where
        self.out = {}

    def pair(self, group):
        return pair_rider([g for _, g in group])

    def after_pair(self, group, recv1, tag):
        return chips_rider(pair_sum([g for _, g in group], list(recv1), self.c_idx, f"pair_sum_{tag}"))

    def after_chips(self, group, recv1, recv2, tag):
        for (key, g), r1, r2 in zip(group, recv1, recv2):
            self.out[key] = (g, r1, r2)


FIRST_WEIGHTS = [(0, "win1"), (0, "wout1")]


def fwd_bwd(ctx_rows, x_rows, target, mods, shards, first_weights, smalls, final_g, c_idx, where):
    assert CTX == TM and len(shards) == 2

    def gather(keys_by_layer):
        return GatherRider([shards[l][k] for l, k in keys_by_layer])

    def put(full, keys_by_layer, got):
        for (l, k), a in zip(keys_by_layer, got):
            full[l][k] = _gathered(k, a)

    full = [dict(s) for s in smalls]
    put(full, FIRST_WEIGHTS, first_weights)
    riders = {
        "ffn1_fwd_0": [(0, "wmix"), (0, "womix"), (0, "win2")],
        "lru_fwd_0_0": [(0, "wout2")],
        "ffn2_fwd_0": [(1, "win1"), (1, "wout1")],
        "ffn1_fwd_1": [(1, "wmix"), (1, "womix"), (1, "win2")],
        "lru_fwd_1_0": [(1, "wout2")],
    }

    def ffn(which, l, h):
        name = f"ffn{which}_fwd_{l}"
        w = full[l]
        keys = riders.get(name)
        m = mods[l][:, 0:3] if which == 1 else mods[l][:, 6:9]
        last = (which, l) == (2, 1)
        outs, got = ffn_fwd(h, m, w[f"g{which}"], w[f"win{which}"], w[f"wout{which}"], name,
                            rider=gather(keys) if keys else None, loss=(final_g, target) if last else None)
        if keys:
            put(full, keys, got)
        return outs

    saved = []
    h = (ctx_rows, x_rows)
    for l in range(2):
        mm = mods[l][:, 3:6]
        outs = ffn(1, l, h)
        h1, gu1, acc1 = outs[:3]
        hin = outs[3] if l == 0 else h
        w = full[l]
        p = mix_in_fwd(h1, mm, w["gm"], w["wmix"], f"mix_in_fwd_{l}")
        hs = []
        for d in range(2):
            keys = riders.get(f"lru_fwd_{l}_{d}")
            hd, got = lru_fwd(p, w["conv_w"], w["conv_b"], w["wr"], w["br"], w["wi"], w["bi"], w["lam"],
                              bool(d), f"lru_fwd_{l}_{d}", rider=gather(keys) if keys else None)
            if keys:
                put(full, keys, got)
            hs.append(hd)
        h2, o = mix_out_fwd(h1, p, hs[0], hs[1], mm, w["sgu_g"], w["sgu_w"], w["sgu_bt"], w["womix"], f"mix_out_fwd_{l}")
        outs = ffn(2, l, h2)
        h, gu2, acc2 = outs[:3]
        saved.append((hin, h1, h2, gu1, acc1, p, hs, o, gu2, acc2))
    dh, (loss, dgf) = h, outs[3:]

    rs = _ReduceScatter(c_idx, where)
    grads, dmods, sums = [None, None], [None, None], [None, None]
    pending = None
    for l in (1, 0):
        w = full[l]
        m1, mm, m2 = mods[l][:, 0:3], mods[l][:, 3:6], mods[l][:, 6:9]
        hin, h1, h2, gu1, acc1, p, hs, o, gu2, acc2 = saved[l]
        g = {}
        rs.out = {}
        both = Riders([rs.pair(pending[0]), GatherRider([small_pack])]) if pending else None
        (dp2, g["wout2"], dgate2), got = ffn_bwd_a(dh, acc2, gu2, m2, w["wout2"], f"ffn2_bwd_a_{l}", rider=both)
        if pending:
            r1, (small_all,) = both.split(got)
        chips = rs.after_pair(pending[0], r1, pending[1]) if pending else None
        (dh, g["win2"], dmod2, g["g2"]), r2 = ffn_bwd_b(dh, h2, dp2, m2, w["g2"], w["win2"], f"ffn2_bwd_b_{l}", rider=chips)
        if pending:
            rs.after_chips(pending[0], r1, r2, pending[1])
            sums[l + 1].update(rs.out)
            rs.out = {}

        grp = [(k, _as_blocks(k, g[k])) for k in ("win2", "wout2")]
        (dhs, dprest, g["womix"], dgatem, g["sgu_g"], g["sgu_w"], g["sgu_bt"]), r1 = mix_out_bwd(
            dh, p, hs[0], hs[1], o, mm, w["sgu_g"], w["sgu_w"], w["sgu_bt"], w["womix"], f"mix_out_bwd_{l}",
            rider=rs.pair(grp))
        chips = rs.after_pair(grp, r1, f"a{l}")
        dx, per_dir = [], []
        for d in range(2):
            out = lru_bwd(p, hs[d], dhs, w["conv_w"], w["conv_b"], w["wr"], w["br"], w["wi"], w["bi"], w["lam"],
                          bool(d), f"lru_bwd_{l}_{d}")
            dx.append(out[0])
            per_dir.append(out[1:])
        for k, nm in enumerate(("wr", "wi", "br", "bi", "lam")):
            g[nm] = jnp.stack([per_dir[0][k], per_dir[1][k]])
        (dh, g["wmix"], dmodm, g["gm"], g["conv_w"], g["conv_b"]), r2 = mix_in_bwd(
            dh, h1, p, dx[0], dx[1], dprest, mm, w["gm"], w["conv_w"], w["wmix"], f"mix_in_bwd_{l}", rider=chips)
        rs.after_chips(grp, r1, r2, f"a{l}")
        sums[l] = dict(rs.out)
        rs.out = {}

        if l == 1:
            (dp1, g["wout1"], dgate1), _ = ffn_bwd_a(dh, acc1, gu1, m1, w["wout1"], f"ffn1_bwd_a_{l}")
            (dh, g["win1"], dmod1, g["g1"]), _ = ffn_bwd_b(dh, hin, dp1, m1, w["g1"], w["win1"], f"ffn1_bwd_b_{l}")
            pending = ([(k, _as_blocks(k, g[k])) for k in ("womix", "wmix", "wout1", "win1")], f"b{l}")
            per = small_grads(g)
            small_pack = _pack([per[n] for n in LAYER_SMALL])
        else:
            g_mix = [(k, _as_blocks(k, g[k])) for k in ("womix", "wmix")]
            (dp1, g["wout1"], dgate1), r1_mix = ffn_bwd_a(dh, acc1, gu1, m1, w["wout1"], f"ffn1_bwd_a_{l}",
                                                          rider=rs.pair(g_mix))
            g_out = [("wout1", _as_blocks("wout1", g["wout1"]))]
            per = small_grads(g)
            three = Riders([rs.after_pair(g_mix, r1_mix, f"b{l}"), rs.pair(g_out),
                            GatherRider([_pack([per[n] for n in LAYER_SMALL[1:]])])])
            (g["win1"],), got = ffn_bwd_dw(hin, dp1, m1, w["g1"], f"ffn1_bwd_dw_{l}", rider=three)
            r2_mix, r1_out, (small0_all,) = three.split(got)
            rs.after_chips(g_mix, r1_mix, r2_mix, f"b{l}")
            g_in = [("win1", _as_blocks("win1", g["win1"]))]
            both = Riders([rs.after_pair(g_out, r1_out, f"c{l}"), rs.pair(g_in)])
            nb = dh.shape[0] // TMX
            cut = max((3 * nb) // 8, 1)
            part, got = ffn_bwd_dh(dh, hin, dp1, m1, w["g1"], w["win1"], f"ffn1_bwd_dh0_{l}", (0, cut), rider=both)
            r2_out, r1_in = both.split(got)
            rs.after_chips(g_out, r1_out, r2_out, f"c{l}")
            (dh, dmod1, g["g1"]), r2_in = ffn_bwd_dh(dh, hin, dp1, m1, w["g1"], w["win1"], f"ffn1_bwd_dh1_{l}", (cut, nb),
                                                     carry=part, rider=rs.after_pair(g_in, r1_in, f"d{l}"))
            rs.after_chips(g_in, r1_in, r2_in, f"d{l}")
            dh = dh[CTX:]
            sums[l].update(rs.out)
        dmods[l] = jnp.concatenate([dmod1, dgate1, dmodm, dgatem, dmod2, dgate2], axis=1)
        grads[l] = g
    return loss, dh, jnp.stack(dmods), grads, (small0_all, small_all), sums, dgf


def _row_block(r, c, limit=262144):
    best = 8
    for rb in range(8, r + 1, 8):
        if r % rb == 0 and rb * c <= limit:
            best = rb
    return best


PAIR_SUM_SPLIT = 2


def pair_sum(grads, recv, c_idx, name):
    n = len(grads)

    def body(c_ref, *refs):
        for t in range(n):
            refs[2 * n + t][...] = (refs[t][...] + refs[n + t][...]).astype(BF16)

    mine, theirs, outs = [], [], []
    for g in grads:
        _, r, c = g.shape
        rb = r // PAIR_SUM_SPLIT
        assert rb % 16 == 0
        mine.append(pl.BlockSpec((1, rb, c), lambda j, i, c_ref: (2 * j + c_ref[0], i, 0)))
        theirs.append(pl.BlockSpec((1, rb, c), lambda j, i, c_ref: (j, i, 0)))
        outs.append(jax.ShapeDtypeStruct((4, r, c), BF16))
    return pl.pallas_call(
        body, name=name,
        grid_spec=pltpu.PrefetchScalarGridSpec(num_scalar_prefetch=1, grid=(4, PAIR_SUM_SPLIT), in_specs=mine + theirs,
                                               out_specs=list(theirs)),
        out_shape=outs, compiler_params=_cp(2),
    )(c_idx, *grads, *recv)


ADA_ROWS = 16


def _silu(v):
    return v * _sigmoid(v)


def ada_fwd(cond, w_ada, b_slab, name):
    def body(c_ref, w_ref, b_ref, o_ref):
        s = _silu(c_ref[...]).astype(BF16)
        o_ref[0] = _dot(s, w_ref[0].astype(BF16)) + b_ref[0]

    return pl.pallas_call(
        body, name=name, grid=(DEPTH,),
        in_specs=[pl.BlockSpec((ADA_ROWS, D), lambda l: (0, 0)), pl.BlockSpec((1, D, ADA_SHARD), lambda l: (l, 0, 0)),
                  pl.BlockSpec((1, 1, ADA_SHARD), lambda l: (l, 0, 0))],
        out_specs=pl.BlockSpec((1, ADA_ROWS, ADA_SHARD), lambda l: (l, 0, 0)),
        out_shape=jax.ShapeDtypeStruct((DEPTH, ADA_ROWS, ADA_SHARD), F32),
        compiler_params=_cp(),
    )(cond, w_ada, b_slab)


def ada_bwd(cond, dm_sample, dm_ctx, w_ada, name):
    def body(c_ref, ds_ref, dc_ref, w_ref, gw_ref, dsc_ref):
        @pl.when(pl.program_id(0) == 0)
        def _():
            dsc_ref[...] = jnp.zeros_like(dsc_ref)

        s = _silu(c_ref[...]).astype(BF16)
        dcs = dc_ref[0]
        tot = dcs[0:1]
        for j in range(1, N_DEV):
            tot = tot + dcs[j:j + 1]
        tot8 = jnp.where(lax.broadcasted_iota(jnp.int32, (N_DEV, ADA_SHARD), 0) == 0, tot, 0.0)
        dm = jnp.concatenate([ds_ref[0], tot8], axis=0).astype(BF16)
        gw_ref[0] = _dot_tn(s, dm)
        dsc_ref[...] += _dot_nt(dm, w_ref[0].astype(BF16))[N_DEV:N_DEV + 1]

    slab = pl.BlockSpec((1, N_DEV, ADA_SHARD), lambda l: (l, 0, 0))
    wspec = pl.BlockSpec((1, D, ADA_SHARD), lambda l: (l, 0, 0))
    return pl.pallas_call(
        body, name=name, grid=(DEPTH,),
        in_specs=[pl.BlockSpec((ADA_ROWS, D), lambda l: (0, 0)), slab, slab, wspec],
        out_specs=[wspec, pl.BlockSpec((1, D), lambda l: (0, 0))],
        out_shape=[jax.ShapeDtypeStruct((DEPTH, D, ADA_SHARD), F32), jax.ShapeDtypeStruct((1, D), F32)],
        compiler_params=_cp(),
    )(cond, dm_sample, dm_ctx, w_ada)


def sum_over_devices(parts, name, silu_rows=0, w=None):
    _, r, c = parts.shape

    def body(*refs):
        p_ref, o_ref = refs[0], refs[-1]
        tot = p_ref[0]
        for j in range(1, N_DEV):
            tot = tot + p_ref[j]
        o_ref[...] = tot
        if silu_rows:
            wv = refs[1][...]
            s = _sigmoid(wv)
            o_ref[0:silu_rows, :] = tot[0:silu_rows, :] * (s * (1.0 + wv * (1.0 - s)))

    vm = pl.BlockSpec(memory_space=pltpu.VMEM)
    args = (parts,) if w is None else (parts, w)
    return pl.pallas_call(
        body, name=name, in_specs=[vm] * len(args), out_specs=vm,
        out_shape=jax.ShapeDtypeStruct((r, c), F32),
        compiler_params=pltpu.CompilerParams(vmem_limit_bytes=VMEM_LIMIT),
    )(*args)


def sum_dmods(dm_all, name):
    def body(d_ref, o_ref):
        for l in range(DEPTH):
            tot = d_ref[0, l]
            for j in range(1, N_DEV):
                tot = tot + d_ref[j, l]
            o_ref[l:l + 1, :] = tot[0:1] + tot[1:2]

    vm = pl.BlockSpec(memory_space=pltpu.VMEM)
    return pl.pallas_call(
        body, name=name, in_specs=[vm], out_specs=vm,
        out_shape=jax.ShapeDtypeStruct((DEPTH, N_MOD * D), F32),
    )(dm_all)


ADAMW_BLOCK = 512 * 1024


def adamw(w, g, m, v, name, rider=None):
    r, c = w.shape
    rb = _row_block(r, c, limit=ADAMW_BLOCK)

    def body(w_ref, g_ref, m_ref, v_ref, d_ref, nm_ref, nv_ref):
        g_ = g_ref[...]
        nm = B1 * m_ref[...] + (1.0 - B1) * g_
        nv = B2 * v_ref[...] + (1.0 - B2) * (g_ * g_)
        nm_ref[...] = nm
        nv_ref[...] = nv
        m_hat = nm / (1.0 - B1 ** STEP)
        v_hat = nv / (1.0 - B2 ** STEP)
        d_ref[...] = -LR * (m_hat / (jnp.sqrt(v_hat) + ADAM_EPS) + WD * w_ref[...])

    blk = pl.BlockSpec((rb, c), lambda i: (i, 0))
    shp = jax.ShapeDtypeStruct((r, c), F32)
    return _grid_call(body, name=name, nsteps=r // rb, in_specs=[blk] * 4, out_specs=[blk] * 3, out_shape=[shp] * 3,
                      scratch_shapes=[], args=(w, g, m, v), rider=rider)


def _adamw_math(w, g, m, v):
    nm = B1 * m + (1.0 - B1) * g
    nv = B2 * v + (1.0 - B2) * (g * g)
    m_hat = nm / (1.0 - B1 ** STEP)
    v_hat = nv / (1.0 - B2 ** STEP)
    return -LR * (m_hat / (jnp.sqrt(v_hat) + ADAM_EPS) + WD * w), nm, nv


def adamw_layers(w, parts, m, v, where, name):
    _, r, c = w.shape
    assert len(parts) == DEPTH == 2 and parts[0][0].shape == (N_DEV, r, c)
    rb = _row_block(r, c, limit=ADAMW_BLOCK // 2)
    nb = r // rb

    def body(where_ref, w_ref, a0, b0, c0, a1, b1, c1, m_ref, v_ref, go_ref, d_ref, nm_ref, nv_ref):
        def total(mine, pair, far):
            return (mine[0] + pair[0]) + ((far[0].astype(F32) + far[1].astype(F32)) + far[2].astype(F32))

        g = jnp.where(pl.program_id(0) == 0, total(a0, b0, c0), total(a1, b1, c1))
        go_ref[0], d_ref[0], nm_ref[0], nv_ref[0] = (g,) + _adamw_math(w_ref[0], g, m_ref[0], v_ref[0])

    blk = pl.BlockSpec((1, rb, c), lambda l, i, wr: (l, i, 0))

    def layer_specs(layer):
        row = (lambda l, i: jnp.where(l == 0, i, nb - 1)) if layer == 0 else (lambda l, i: jnp.where(l == 0, 0, i))
        return [pl.BlockSpec((1, rb, c), lambda l, i, wr: (wr[0], row(l, i), 0)),
                pl.BlockSpec((1, rb, c), lambda l, i, wr: (wr[1], row(l, i), 0)),
                pl.BlockSpec((3, rb, c), lambda l, i, wr: (0, row(l, i), 0))]

    shp = jax.ShapeDtypeStruct(w.shape, F32)
    return pl.pallas_call(
        body, name=name,
        grid_spec=pltpu.PrefetchScalarGridSpec(
            num_scalar_prefetch=1, grid=(DEPTH, nb),
            in_specs=[blk] + layer_specs(0) + layer_specs(1) + [blk, blk], out_specs=[blk] * 4),
        out_shape=[shp] * 4, compiler_params=_cp(2),
    )(where, w, *parts[0], *parts[1], m, v)


def _adamw_nd(w, g, m, v, name, rider=None):
    shape = w.shape
    flat = lambda a: a.reshape(-1, shape[-1])
    outs, got = adamw(flat(w), flat(g), flat(m), flat(v), name, rider=rider)
    return tuple(o.reshape(shape) for o in outs), got


LANES = 128


PACK_UNIT = 8 * LANES


ADAMW_SMALL_ROWS = 512


def _pack(arrays, row_multiple=8):
    pieces, n = [], 0
    for a in arrays:
        pieces.append(a.reshape(-1).astype(F32))
        pad = (-a.size) % PACK_UNIT
        if pad:
            pieces.append(jnp.zeros((pad,), F32))
        n += a.size + pad
    tail = (-n) % (row_multiple * LANES)
    if tail:
        pieces.append(jnp.zeros((tail,), F32))
    return jnp.concatenate(pieces).reshape(-1, LANES)


def _unpack(packed, shapes):
    out, r0 = [], 0
    lead = packed.shape[:-2]
    for shp in shapes:
        size = 1
        for s in shp:
            size *= s
        nr = 8 * -(-size // PACK_UNIT)
        blk = packed[..., r0:r0 + nr, :].reshape(lead + (nr * LANES,))[..., :size]
        out.append(blk.reshape(lead + tuple(shp)))
        r0 += nr
    return out


WEIGHTS = ["c_ctx", "w_ada", "b_ada", "ffn1_norm_g", "ffn1_w_in", "ffn1_w_out", "mix_norm_g", "w_in_mix", "lru_conv_w",
           "lru_conv_b", "lru_w_r", "lru_b_r", "lru_w_i", "lru_b_i", "lru_lambda", "sgu_norm_g", "sgu_w", "sgu_b",
           "w_out_mix", "ffn2_norm_g", "ffn2_w_in", "ffn2_w_out", "final_norm_g"]
BIG = ["w_ada", "ffn1_w_in", "ffn1_w_out", "w_in_mix", "w_out_mix", "ffn2_w_in", "ffn2_w_out"]
SHARDED_SMALL = ["lru_conv_w", "lru_b_r", "lru_b_i", "lru_lambda"]
LAYER_SMALL = ["ffn1_norm_g", "mix_norm_g", "ffn2_norm_g", "lru_conv_w", "lru_conv_b", "lru_w_r", "lru_b_r", "lru_w_i",
               "lru_b_i", "lru_lambda", "sgu_norm_g", "sgu_w", "sgu_b"]
LRU_SHARD = LRU_W // N_DEV


def _widen(a):
    return jnp.moveaxis(a, 0, -2).reshape(a.shape[1:-1] + (LRU_W,))


def kernel(x, c, ctx, c_ctx, w_ada, b_ada, ffn1_norm_g, ffn1_w_in, ffn1_w_out, mix_norm_g, w_in_mix, lru_conv_w, lru_conv_b, lru_w_r, lru_b_r, lru_w_i, lru_b_i, lru_lambda, sgu_norm_g, sgu_w, sgu_b, w_out_mix, ffn2_norm_g, ffn2_w_in, ffn2_w_out, final_norm_g, loss_target, m_c_ctx, m_w_ada, m_b_ada, m_ffn1_norm_g, m_ffn1_w_in, m_ffn1_w_out, m_mix_norm_g, m_w_in_mix, m_lru_conv_w, m_lru_conv_b, m_lru_w_r, m_lru_b_r, m_lru_w_i, m_lru_b_i, m_lru_lambda, m_sgu_norm_g, m_sgu_w, m_sgu_b, m_w_out_mix, m_ffn2_norm_g, m_ffn2_w_in, m_ffn2_w_out, m_final_norm_g, v_c_ctx, v_w_ada, v_b_ada, v_ffn1_norm_g, v_ffn1_w_in, v_ffn1_w_out, v_mix_norm_g, v_w_in_mix, v_lru_conv_w, v_lru_conv_b, v_lru_w_r, v_lru_b_r, v_lru_w_i, v_lru_b_i, v_lru_lambda, v_sgu_norm_g, v_sgu_w, v_sgu_b, v_w_out_mix, v_ffn2_norm_g, v_ffn2_w_in, v_ffn2_w_out, v_final_norm_g):
    given = dict(locals())
    W = {n: given[n] for n in WEIGHTS}
    M = {n: given["m_" + n] for n in WEIGHTS}
    V = {n: given["v_" + n] for n in WEIGHTS}
    xi, yi, ci = _position()
    me = 4 * xi + 2 * yi + ci
    chip = 2 * xi + yi

    shards = []
    tr = lambda a: jnp.swapaxes(a, 1, 2)
    for l in range(DEPTH):
        sh = dict(win1=tr(ffn1_w_in)[l], wout1=ffn1_w_out[l], wmix=w_in_mix[l], womix=w_out_mix[l], win2=tr(ffn2_w_in)[l],
                  wout2=ffn2_w_out[l])
        shards.append({k: a.astype(BF16) for k, a in sh.items()})

    sharded_shapes = [W[n].shape for n in SHARDED_SMALL]
    both = Riders([GatherRider([_pack([c[0]] + [W[n] for n in SHARDED_SMALL])]),
                   GatherRider([shards[0][k] for _, k in FIRST_WEIGHTS])])
    (got,), first_weights = both.split(run_alone(both, pl.ANY, "gather_first"))
    parts = _unpack(got, [(D,)] + sharded_shapes)
    c_all = parts[0]
    wide = {n: _widen(a) for n, a in zip(SHARDED_SMALL, parts[1:])}
    cond = jnp.concatenate([c_all, c_ctx[None, :], jnp.zeros((ADA_ROWS - N_DEV - 1, D), F32)], axis=0)
    b_slab = lax.dynamic_slice_in_dim(b_ada, me * ADA_SHARD, ADA_SHARD, axis=1)[:, None, :]
    slabs = ada_fwd(cond, w_ada, b_slab, "ada_fwd")
    mall = run_alone(GatherRider([slabs.reshape(DEPTH * ADA_ROWS, ADA_SHARD)]), pltpu.VMEM, "gather_mod")[0]
    mall = mall.reshape(N_DEV, DEPTH, ADA_ROWS, ADA_SHARD)
    m_sample = lax.dynamic_index_in_dim(mall, me, axis=2, keepdims=False)
    m_ctx = mall[:, :, N_DEV, :]
    mods = jnp.stack([jnp.transpose(m, (1, 0, 2)).reshape(DEPTH, N_MOD, D) for m in (m_ctx, m_sample)], axis=1)

    smalls = []
    for l in range(DEPTH):
        smalls.append(small_layer(ffn1_norm_g[l], mix_norm_g[l], ffn2_norm_g[l], wide["lru_conv_w"][l], lru_conv_b[l],
                                  lru_w_r[l], wide["lru_b_r"][l], lru_w_i[l], wide["lru_b_i"][l], wide["lru_lambda"][l],
                                  sgu_norm_g[l], sgu_w[l], sgu_b[l]))

    c_idx = ci.reshape(1).astype(jnp.int32)
    where = jnp.stack([me, chip]).astype(jnp.int32)
    loss_blk, dx, dmods, grads, (small0_all, small1_all), gsum, dgf = fwd_bwd(
        ctx[0], x[0], loss_target[0], mods, shards, first_weights, smalls, final_norm_g[None, :], c_idx, where)
    smalls_shape = {n: (W[n].shape[1:-1] + (LRU_W,)) if n in SHARDED_SMALL else W[n].shape[1:] for n in LAYER_SMALL}
    G, delta, new_m, new_v = {}, {}, {}, {}
    for key, n in (("win1", "ffn1_w_in"), ("wout1", "ffn1_w_out"), ("wmix", "w_in_mix"), ("womix", "w_out_mix"),
                   ("win2", "ffn2_w_in"), ("wout2", "ffn2_w_out")):
        t_in = tr if key in ("win1", "win2") else (lambda a: a)
        outs = adamw_layers(t_in(W[n]), [gsum[l][key] for l in range(DEPTH)], t_in(M[n]), t_in(V[n]), where,
                            f"adamw_{n}")
        G[n], delta[n], new_m[n], new_v[n] = [t_in(o) for o in outs]

    n_rows = DEPTH * 2 * N_MOD
    dm_rows = jnp.concatenate([dmods.reshape(n_rows, D), jnp.zeros((-n_rows % 8, D), F32)], axis=0)
    dm_all = run_alone(GatherRider([dm_rows]), pltpu.VMEM, "gather_dmod")[0][:, :n_rows]
    dm_all = dm_all.reshape(N_DEV, DEPTH, 2, N_MOD * D)
    mine = lax.dynamic_slice_in_dim(dm_all, me * ADA_SHARD, ADA_SHARD, axis=3)
    G["w_ada"], dsc = ada_bwd(cond, jnp.transpose(mine[:, :, 1, :], (1, 0, 2)), jnp.transpose(mine[:, :, 0, :], (1, 0, 2)),
                              w_ada, "ada_bwd")
    G["b_ada"] = sum_dmods(dm_all, "sum_dmods")
    (delta["w_ada"], new_m["w_ada"], new_v["w_ada"]), _ = _adamw_nd(w_ada, G["w_ada"], m_w_ada, v_w_ada, "adamw_w_ada")

    head_all = run_alone(GatherRider([_pack([dsc[0], dgf[0], grads[0]["g1"][0], loss_blk])]), pltpu.VMEM,
                         "gather_head_grads")[0]
    head = _unpack(sum_over_devices(head_all, "sum_head_grads", silu_rows=D // LANES, w=c_ctx.reshape(D // LANES, LANES)),
                   [(D,), (D,), (D,), loss_blk.shape])
    shapes = [smalls_shape[n] for n in LAYER_SMALL]
    sum0 = [head[2]] + _unpack(sum_over_devices(small0_all, "sum_small_grads_0"), shapes[1:])
    sum1 = _unpack(sum_over_devices(small1_all, "sum_small_grads_1"), shapes)
    G["c_ctx"], G["final_norm_g"] = head[0], head[1]
    for n, a0, a1 in zip(LAYER_SMALL, sum0, sum1):
        a = jnp.stack([a0, a1])
        G[n] = lax.dynamic_slice_in_dim(a, me * LRU_SHARD, LRU_SHARD, axis=a.ndim - 1) if n in SHARDED_SMALL else a

    rest = [n for n in WEIGHTS if n not in BIG]
    shapes = [W[n].shape for n in rest]
    outs, _ = adamw(*[_pack([src[n] for n in rest], row_multiple=ADAMW_SMALL_ROWS) for src in (W, G, M, V)], "adamw_small")
    for dst, packed in zip((delta, new_m, new_v), outs):
        for n, a in zip(rest, _unpack(packed, shapes)):
            dst[n] = a

    loss = head[3][0, 0]
    grad_x = dx[None]
    return (loss, grad_x, *[G[n] for n in WEIGHTS], *[delta[n] for n in WEIGHTS], *[new_m[n] for n in WEIGHTS],
            *[new_v[n] for n in WEIGHTS])
```

```python
import jax
import jax.numpy as jnp
from jax import lax
from jax.experimental import pallas as pl
from jax.experimental.pallas import tpu as pltpu

F32 = jnp.float32
BF16 = jnp.bfloat16

D = 1024
CTX = 256
DEPTH = 2
EPS = 1e-6
D_FF = 2816
LRU_W = 512
HEADS = 8
HEAD_DIM = 64
CONV_W = 4
RG_C = 8.0
GROUPS = 4
GROUP_DIM = 128
CHUNK = 128
MLP_W = 512
IN_PROJ = 2048
N_MOD = 9
N_DEV = 8

LR = 0.001
B1 = 0.9
B2 = 0.999
ADAM_EPS = 1e-08
WD = 0.01
STEP = 10

HT = 256
WT = 512
N_MIX_SHARD = IN_PROJ // N_DEV
ADA_SHARD = N_MOD * D // N_DEV

TM = 256
SUB = 3
TMX = SUB * TM
HALO = 8
VMEM_LIMIT = 60 * 1024 * 1024

MESH = pl.DeviceIdType.MESH
ANY = pl.BlockSpec(memory_space=pl.ANY)


def _cp(n_axes=1):
    return pltpu.CompilerParams(dimension_semantics=("arbitrary",) * n_axes, vmem_limit_bytes=VMEM_LIMIT)


def _position():
    return lax.axis_index("x"), lax.axis_index("y"), lax.axis_index("c")


class GatherRider:
    def __init__(self, shards):
        n = len(shards)
        self.n = n
        self.ins = list(shards)
        self.out_shape = [jax.ShapeDtypeStruct((N_DEV,) + s.shape, s.dtype) for s in shards]
        self.sems = [pltpu.SemaphoreType.DMA((n, 7)), pltpu.SemaphoreType.DMA((n, 7)), pltpu.SemaphoreType.DMA((n,))]

    def _ctx(self, outs, sems):
        x, y, c = _position()
        chips = [(1 - x, y), (x, 1 - y), (1 - x, 1 - y)]

        def copy(t, k, block, to, src=None):
            dst = outs[t].at[4 * block[0] + 2 * block[1] + block[2]]
            return pltpu.make_async_remote_copy(
                src_ref=dst if src is None else src, dst_ref=dst, send_sem=sems[0].at[t, k],
                recv_sem=sems[1].at[t, k], device_id=to, device_id_type=MESH)

        return (x, y, c), (x, y, 1 - c), chips, copy

    def _local(self, ins, outs, sems, t):
        x, y, c = _position()
        return pltpu.make_async_copy(ins[t], outs[t].at[4 * x + 2 * y + c], sems[2].at[t])

    def _first(self, ins, outs, sems, t):
        me, sibling, chips, copy = self._ctx(outs, sems)
        return [copy(t, 0, me, sibling, src=ins[t])] + [copy(t, 1 + j, me, (*chip, me[2]), src=ins[t])
                                                         for j, chip in enumerate(chips)]

    def start(self, ins, outs, sems):
        for t in range(self.n):
            self._local(ins, outs, sems, t).start()
            for cp in self._first(ins, outs, sems, t):
                cp.start()

    def mid(self, ins, outs, sems):
        me, sibling, chips, copy = self._ctx(outs, sems)
        for j, chip in enumerate(chips):
            for t in range(self.n):
                copy(t, 1 + j, (*chip, me[2]), me).wait_recv()
                copy(t, 4 + j, (*chip, me[2]), sibling).start()

    def finish(self, ins, outs, sems):
        me, sibling, chips, copy = self._ctx(outs, sems)
        for t in range(self.n):
            copy(t, 0, sibling, me).wait_recv()
            for j, chip in enumerate(chips):
                copy(t, 4 + j, (*chip, 1 - me[2]), me).wait_recv()
        for t in range(self.n):
            for cp in self._first(ins, outs, sems, t):
                cp.wait_send()
            for j, chip in enumerate(chips):
                copy(t, 4 + j, (*chip, me[2]), sibling).wait_send()
            self._local(ins, outs, sems, t).wait()


class ExchangeRider:
    def __init__(self, tensors, plan, n_slots):
        n = len(tensors)
        self.n, self.plan = n, plan
        self.ins = list(tensors)
        self.out_shape = [jax.ShapeDtypeStruct((n_slots,) + s.shape[1:], s.dtype) for s in tensors]
        self.sems = [pltpu.SemaphoreType.DMA((n, n_slots)), pltpu.SemaphoreType.DMA((n, n_slots))]

    def _copies(self, ins, outs, sems):
        return [pltpu.make_async_remote_copy(
            src_ref=ins[t].at[block], dst_ref=outs[t].at[k], send_sem=sems[0].at[t, k], recv_sem=sems[1].at[t, k],
            device_id=to, device_id_type=MESH)
            for t in range(self.n) for k, (block, to) in enumerate(self.plan(*_position()))]

    def start(self, ins, outs, sems):
        for cp in self._copies(ins, outs, sems):
            cp.start()

    def mid(self, ins, outs, sems):
        pass

    def finish(self, ins, outs, sems):
        for cp in self._copies(ins, outs, sems):
            cp.wait()


class Riders:
    def __init__(self, riders):
        self.riders = list(riders)
        self.ins = [a for r in self.riders for a in r.ins]
        self.out_shape = [s for r in self.riders for s in r.out_shape]
        self.sems = [s for r in self.riders for s in r.sems]

    def _each(self, ins, outs, sems):
        i = o = s = 0
        for r in self.riders:
            ni, no, ns = len(r.ins), len(r.out_shape), len(r.sems)
            yield r, ins[i:i + ni], outs[o:o + no], sems[s:s + ns]
            i, o, s = i + ni, o + no, s + ns

    def start(self, ins, outs, sems):
        for r, a, b, c in self._each(ins, outs, sems):
            r.start(a, b, c)

    def mid(self, ins, outs, sems):
        for r, a, b, c in self._each(ins, outs, sems):
            r.mid(a, b, c)

    def finish(self, ins, outs, sems):
        for r, a, b, c in self._each(ins, outs, sems):
            r.finish(a, b, c)

    def split(self, outs):
        res, o = [], 0
        for r in self.riders:
            res.append(list(outs[o:o + len(r.out_shape)]))
            o += len(r.out_shape)
        return res


def pair_rider(grads):
    def plan(x, y, c):
        return [(4 * cx + 2 * cy + (1 - c), (x, y, 1 - c)) for cx in range(2) for cy in range(2)]
    return ExchangeRider(grads, plan, 4)


def chips_rider(parts):
    def plan(x, y, c):
        return [(2 * cx + cy, (cx, cy, c)) for cx, cy in [(1 - x, y), (x, 1 - y), (1 - x, 1 - y)]]
    return ExchangeRider(parts, plan, 3)


def run_alone(rider, space, name):
    ni = len(rider.ins)
    no = len(rider.out_shape)

    def body(*refs):
        ins, outs, sems = refs[:ni], refs[ni:ni + no], refs[ni + no:]
        rider.start(ins, outs, sems)
        rider.mid(ins, outs, sems)
        rider.finish(ins, outs, sems)

    spec = pl.BlockSpec(memory_space=space)
    return pl.pallas_call(
        body, name=name, in_specs=[spec] * ni, out_specs=[spec] * no, out_shape=rider.out_shape,
        scratch_shapes=rider.sems, compiler_params=pltpu.CompilerParams(vmem_limit_bytes=VMEM_LIMIT),
    )(*rider.ins)


def _grid_call(body, *, name, nsteps, in_specs, out_specs, out_shape, scratch_shapes, args, rider=None, aliases=None):
    aliases = aliases or {}
    if rider is None:
        outs = pl.pallas_call(body, name=name, grid=(nsteps,), in_specs=in_specs, out_specs=out_specs,
                              out_shape=out_shape, scratch_shapes=scratch_shapes, input_output_aliases=aliases,
                              compiler_params=_cp())(*args)
        return outs, []
    ni, no, ns = len(in_specs), len(out_specs), len(scratch_shapes)
    ri, ro = len(rider.ins), len(rider.out_shape)

    def wrapped(*refs):
        ins, refs = refs[:ni], refs[ni:]
        r_ins, refs = refs[:ri], refs[ri:]
        outs, refs = refs[:no], refs[no:]
        r_outs, refs = refs[:ro], refs[ro:]
        scratch, r_sems = refs[:ns], refs[ns:]
        s = pl.program_id(0)

        @pl.when(s == 0)
        def _():
            rider.start(r_ins, r_outs, r_sems)

        body(*ins, *outs, *scratch)

        @pl.when(s == (3 * nsteps) // 4)
        def _():
            rider.mid(r_ins, r_outs, r_sems)

        @pl.when(s == nsteps - 1)
        def _():
            rider.finish(r_ins, r_outs, r_sems)

    outs = pl.pallas_call(
        wrapped, name=name, grid=(nsteps,), in_specs=list(in_specs) + [ANY] * ri, out_specs=list(out_specs) + [ANY] * ro,
        out_shape=list(out_shape) + rider.out_shape, scratch_shapes=list(scratch_shapes) + rider.sems,
        input_output_aliases=aliases, compiler_params=_cp())(*args, *rider.ins)
    return outs[:no], outs[no:]


def _dot(a, b):
    return jnp.dot(a, b, preferred_element_type=F32)


def _dot_nt(a, b):
    return lax.dot_general(a, b, (((1,), (1,)), ((), ())), preferred_element_type=F32)


def _dot_tn(a, b):
    return lax.dot_general(a, b, (((0,), (0,)), ((), ())), preferred_element_type=F32)


def _sigmoid(x):
    return 1.0 / (1.0 + jnp.exp(-x))


def _kind(i):
    return jnp.where(i < CTX // TM, 0, 1)


def _sel(kind, mod_ref, k):
    return mod_ref[kind, k:k + 1, :]


def _acc2(ref, k, val, kind):
    ref[kind, k:k + 1, :] += jnp.sum(val, axis=0, keepdims=True)


def _norm_mod(h, g, shift, scale):
    r = lax.rsqrt(jnp.mean(h * h, axis=-1, keepdims=True) + EPS)
    n = h * r
    return (n * g) * (1.0 + scale) + shift, n, r


def _norm_mod_bwd(dz, n, r, g, scale):
    dn = dz * (g * (1.0 + scale))
    return r * (dn - n * jnp.mean(dn * n, axis=-1, keepdims=True))


def ffn_fwd(h, mod, g, win, wout, name, rider=None, loss=None):
    split = isinstance(h, tuple)
    nc = CTX // TM
    t = h[0].shape[0] + h[1].shape[0] if split else h.shape[0]

    def body(*refs):
        refs = list(refs)
        win_v, wout_v, a_v = refs[-3:]
        rows = refs[:2] if split else refs[:1]
        mod_ref, g_ref, win_hbm, wout_hbm = refs[len(rows):len(rows) + 4]
        rest = refs[len(rows) + 4:-3]
        if loss is not None:
            fg_ref, tgt_ref, rest = rest[0], rest[1], rest[2:]
        out_ref, gu_ref, acc_ref, rest = rest[0], rest[1], rest[2], rest[3:]
        i = pl.program_id(0)

        @pl.when(i == 0)
        def _():
            pltpu.sync_copy(win_hbm, win_v)
            pltpu.sync_copy(wout_hbm, wout_v)

        if split:
            hh = jnp.where(i < nc, rows[0][...], rows[1][...])
            rest[0][...] = hh
        else:
            hh = rows[0][...]
        ic = _kind(i)
        z, _, _ = _norm_mod(hh, g_ref[...], _sel(ic, mod_ref, 0), _sel(ic, mod_ref, 1))
        zb = z.astype(BF16)
        for j in range(D_FF // HT):
            gb, ub = slice(j * HT, (j + 1) * HT), slice(D_FF + j * HT, D_FF + (j + 1) * HT)
            gg = _dot_nt(zb, win_v[gb, :])
            uu = _dot_nt(zb, win_v[ub, :])
            gu_ref[:, gb] = gg.astype(BF16)
            gu_ref[:, ub] = uu.astype(BF16)
            a_v[:, gb] = ((gg * _sigmoid(gg)) * uu).astype(BF16)
        acc = _dot(a_v[...], wout_v[...])
        acc_ref[...] = acc
        hn = hh + (0.5 * _sel(ic, mod_ref, 2)) * acc
        if loss is None:
            out_ref[...] = hn
        else:
            loss_ref, dgf_ref = rest

            @pl.when(i == 0)
            def _():
                loss_ref[...] = jnp.zeros_like(loss_ref)
                dgf_ref[...] = jnp.zeros_like(dgf_ref)

            @pl.when(i < nc)
            def _():
                out_ref[...] = jnp.zeros_like(out_ref)

            @pl.when(i >= nc)
            def _():
                gain = fg_ref[...]
                r = lax.rsqrt(jnp.mean(hn * hn, axis=-1, keepdims=True) + EPS)
                n = hn * r
                err = n * gain - tgt_ref[...]
                loss_ref[...] += 0.5 * jnp.sum(jnp.mean(err * err, axis=-1, keepdims=True))
                dy = err * (1.0 / D)
                dgf_ref[...] += jnp.sum(dy * n, axis=0, keepdims=True)
                dn = dy * gain
                out_ref[...] = r * (dn - n * jnp.mean(dn * n, axis=-1, keepdims=True))

    row = pl.BlockSpec((TM, D), lambda i: (i, 0))
    vec = pl.BlockSpec((1, D), lambda i: (0, 0))
    rshape = jax.ShapeDtypeStruct((t, D), F32)
    if split:
        rows_in = [pl.BlockSpec((TM, D), lambda i: (jnp.minimum(i, nc - 1), 0)),
                   pl.BlockSpec((TM, D), lambda i: (jnp.maximum(i - nc, 0), 0))]
    else:
        rows_in = [row]
    in_specs = rows_in + [pl.BlockSpec((2, 3, D), lambda i: (0, 0, 0)), vec, ANY, ANY]
    out_specs = [row, pl.BlockSpec((TM, 2 * D_FF), lambda i: (i, 0)), row] + ([row] if split else [])
    out_shape = [rshape, jax.ShapeDtypeStruct((t, 2 * D_FF), BF16), rshape] + ([rshape] if split else [])
    args = (*(h if split else (h,)), mod, g, win, wout)
    if loss is not None:
        in_specs += [vec, pl.BlockSpec((TM, D), lambda i: (jnp.maximum(i - nc, 0), 0))]
        out_specs += [pl.BlockSpec((8, 128), lambda i: (0, 0)), vec]
        out_shape += [jax.ShapeDtypeStruct((8, 128), F32), jax.ShapeDtypeStruct((1, D), F32)]
        args += tuple(loss)
    return _grid_call(
        body, name=name, nsteps=t // TM, in_specs=in_specs, out_specs=out_specs, out_shape=out_shape,
        scratch_shapes=[pltpu.VMEM((2 * D_FF, D), BF16), pltpu.VMEM((D_FF, D), BF16), pltpu.VMEM((TM, D_FF), BF16)],
        args=args, rider=rider)


def ffn_bwd_a(dy, acc, gu, mod, wout, name, rider=None):
    t = dy.shape[0]
    nt = t // TM

    def body(dy_ref, acc_ref, gu_ref, mod_ref, wout_hbm, dp_ref, dwout_hbm, dgate_ref, wout_v, dwout_v):
        i = pl.program_id(0)

        @pl.when(i == 0)
        def _():
            pltpu.sync_copy(wout_hbm, wout_v)
            dwout_v[...] = jnp.zeros_like(dwout_v)
            dgate_ref[...] = jnp.zeros_like(dgate_ref)

        dy_ = dy_ref[...]
        ic = _kind(i)
        _acc2(dgate_ref, 0, 0.5 * dy_ * acc_ref[...], ic)
        daccb = ((0.5 * _sel(ic, mod_ref, 2)) * dy_).astype(BF16)
        for j in range(D_FF // HT):
            blk, ublk = slice(j * HT, (j + 1) * HT), slice(D_FF + j * HT, D_FF + (j + 1) * HT)
            da = _dot_nt(daccb, wout_v[blk, :])
            gg = gu_ref[:, blk].astype(F32)
            uu = gu_ref[:, ublk].astype(F32)
            s = _sigmoid(gg)
            sl = gg * s
            dwout_v[blk, :] += _dot_tn((sl * uu).astype(BF16), daccb)
            dp_ref[:, blk] = (da * uu * (s + sl * (1.0 - s))).astype(BF16)
            dp_ref[:, ublk] = (da * sl).astype(BF16)

        @pl.when(i == nt - 1)
        def _():
            pltpu.sync_copy(dwout_v, dwout_hbm)

    row = pl.BlockSpec((TM, D), lambda i: (i, 0))
    wide = pl.BlockSpec((TM, 2 * D_FF), lambda i: (i, 0))
    return _grid_call(
        body, name=name, nsteps=nt,
        in_specs=[row, row, wide, pl.BlockSpec((2, 3, D), lambda i: (0, 0, 0)), ANY],
        out_specs=[wide, ANY, pl.BlockSpec((2, 1, D), lambda i: (0, 0, 0))],
        out_shape=[jax.ShapeDtypeStruct((t, 2 * D_FF), BF16), jax.ShapeDtypeStruct((D_FF, D), F32),
                   jax.ShapeDtypeStruct((2, 1, D), F32)],
        scratch_shapes=[pltpu.VMEM((D_FF, D), BF16), pltpu.VMEM((D_FF, D), F32)],
        args=(dy, acc, gu, mod, wout), rider=rider)


def ffn_bwd_b(dy, h, dp, mod, g, win, name, rider=None, latent_only=False):
    t = dy.shape[0]
    nt = t // TM
    nc = CTX // TM

    def body(dy_ref, h_ref, dp_ref, mod_ref, g_ref, win_hbm, dh_ref, dwin_hbm, dmod_ref, dg_ref, win_v, dwin_v):
        i = pl.program_id(0)

        @pl.when(i == 0)
        def _():
            pltpu.sync_copy(win_hbm, win_v)
            dwin_v[...] = jnp.zeros_like(dwin_v)
            dmod_ref[...] = jnp.zeros_like(dmod_ref)
            dg_ref[...] = jnp.zeros_like(dg_ref)

        ic = _kind(i)
        gain = g_ref[...]
        scale = _sel(ic, mod_ref, 1)
        z, n, r = _norm_mod(h_ref[...], gain, _sel(ic, mod_ref, 0), scale)
        zb = z.astype(BF16)
        dz = _dot(dp_ref[...], win_v[...])
        for j in range(2 * D_FF // WT):
            blk = slice(j * WT, (j + 1) * WT)
            dwin_v[blk, :] += _dot_tn(dp_ref[:, blk], zb)
        _acc2(dmod_ref, 0, dz, ic)
        _acc2(dmod_ref, 1, dz * (n * gain), ic)
        dg_ref[...] += jnp.sum(dz * (1.0 + scale) * n, axis=0, keepdims=True)
        dh_ref[...] = dy_ref[...] + _norm_mod_bwd(dz, n, r, gain, scale)

        @pl.when(i == nt - 1)
        def _():
            pltpu.sync_copy(dwin_v, dwin_hbm)

    row = pl.BlockSpec((TM, D), lambda i: (i, 0))
    if latent_only:
        dh_spec = pl.BlockSpec((TM, D), lambda i: (jnp.maximum(i - nc, 0), 0))
        dh_shape = jax.ShapeDtypeStruct((t - CTX, D), F32)
    else:
        dh_spec, dh_shape = row, jax.ShapeDtypeStruct((t, D), F32)
    return _grid_call(
        body, name=name, nsteps=nt,
        in_specs=[row, row, pl.BlockSpec((TM, 2 * D_FF), lambda i: (i, 0)),
                  pl.BlockSpec((2, 3, D), lambda i: (0, 0, 0)), pl.BlockSpec((1, D), lambda i: (0, 0)), ANY],
        out_specs=[dh_spec, ANY, pl.BlockSpec((2, 2, D), lambda i: (0, 0, 0)), pl.BlockSpec((1, D), lambda i: (0, 0))],
        out_shape=[dh_shape, jax.ShapeDtypeStruct((2 * D_FF, D), F32),
                   jax.ShapeDtypeStruct((2, 2, D), F32), jax.ShapeDtypeStruct((1, D), F32)],
        scratch_shapes=[pltpu.VMEM((2 * D_FF, D), BF16), pltpu.VMEM((2 * D_FF, D), F32)],
        args=(dy, h, dp, mod, g, win), rider=rider)


def ffn_bwd_dw(h, dp, mod, g, name, rider=None):
    t = h.shape[0]
    nt = t // TM

    def body(h_ref, dp_ref, mod_ref, g_ref, dwin_hbm, dwin_v):
        i = pl.program_id(0)

        @pl.when(i == 0)
        def _():
            dwin_v[...] = jnp.zeros_like(dwin_v)

        zs = []
        for k in range(SUB):
            ic = _kind(SUB * i + k)
            z, _, _ = _norm_mod(h_ref[pl.ds(k * TM, TM), :], g_ref[...], _sel(ic, mod_ref, 0), _sel(ic, mod_ref, 1))
            zs.append(z.astype(BF16))
        zb = jnp.concatenate(zs, axis=0)
        for j in range(2 * D_FF // WT):
            blk = slice(j * WT, (j + 1) * WT)
            dwin_v[blk, :] += _dot_tn(dp_ref[:, blk], zb)

        @pl.when(i == nt // SUB - 1)
        def _():
            pltpu.sync_copy(dwin_v, dwin_hbm)

    return _grid_call(
        body, name=name, nsteps=nt // SUB,
        in_specs=[pl.BlockSpec((TMX, D), lambda i: (i, 0)), pl.BlockSpec((TMX, 2 * D_FF), lambda i: (i, 0)),
                  pl.BlockSpec((2, 3, D), lambda i: (0, 0, 0)), pl.BlockSpec((1, D), lambda i: (0, 0))],
        out_specs=[ANY], out_shape=[jax.ShapeDtypeStruct((2 * D_FF, D), F32)],
        scratch_shapes=[pltpu.VMEM((2 * D_FF, D), F32)],
        args=(h, dp, mod, g), rider=rider)


def ffn_bwd_dh(dy, h, dp, mod, g, win, name, blocks, carry=None, rider=None):
    t = dy.shape[0]
    b0, b1 = blocks

    def body(*refs):
        if carry is None:
            dy_ref, h_ref, dp_ref, mod_ref, g_ref, win_hbm, dh_ref, dmod_ref, dg_ref, win_v = refs
        else:
            dy_ref, h_ref, dp_ref, mod_ref, g_ref, win_hbm, _, dmod0_ref, dg0_ref, dh_ref, dmod_ref, dg_ref, win_v = refs
        i = pl.program_id(0)

        @pl.when(i == 0)
        def _():
            pltpu.sync_copy(win_hbm, win_v)
            dmod_ref[...] = jnp.zeros_like(dmod_ref) if carry is None else dmod0_ref[...]
            dg_ref[...] = jnp.zeros_like(dg_ref) if carry is None else dg0_ref[...]

        gain = g_ref[...]
        dz_all = _dot(dp_ref[...], win_v[...])
        for k in range(SUB):
            ic = _kind(SUB * (i + b0) + k)
            rows = pl.ds(k * TM, TM)
            scale = _sel(ic, mod_ref, 1)
            _, n, r = _norm_mod(h_ref[rows, :], gain, _sel(ic, mod_ref, 0), scale)
            dz = dz_all[k * TM:(k + 1) * TM]
            _acc2(dmod_ref, 0, dz, ic)
            _acc2(dmod_ref, 1, dz * (n * gain), ic)
            dg_ref[...] += jnp.sum(dz * (1.0 + scale) * n, axis=0, keepdims=True)
            dh_ref[rows, :] = dy_ref[rows, :] + _norm_mod_bwd(dz, n, r, gain, scale)

    row = pl.BlockSpec((TMX, D), lambda i: (i + b0, 0))
    small = [pl.BlockSpec((2, 2, D), lambda i: (0, 0, 0)), pl.BlockSpec((1, D), lambda i: (0, 0))]
    in_specs = [row, row, pl.BlockSpec((TMX, 2 * D_FF), lambda i: (i + b0, 0)),
                pl.BlockSpec((2, 3, D), lambda i: (0, 0, 0)), pl.BlockSpec((1, D), lambda i: (0, 0)), ANY]
    args = (dy, h, dp, mod, g, win)
    if carry is not None:
        in_specs += [ANY] + small
        args += tuple(carry)
    return _grid_call(
        body, name=name, nsteps=b1 - b0, in_specs=in_specs, out_specs=[row] + small,
        out_shape=[jax.ShapeDtypeStruct((t, D), F32), jax.ShapeDtypeStruct((2, 2, D), F32),
                   jax.ShapeDtypeStruct((1, D), F32)],
        scratch_shapes=[pltpu.VMEM((2 * D_FF, D), BF16)],
        args=args, rider=rider, aliases=None if carry is None else {6: 0})


def mix_in_fwd(h, mod, g, wmix, name):
    t = h.shape[0]

    def body(h_ref, mod_ref, g_ref, w_hbm, p_ref, w_v):
        i = pl.program_id(0)

        @pl.when(i == 0)
        def _():
            pltpu.sync_copy(w_hbm, w_v)

        zs = []
        for k in range(SUB):
            ic = _kind(SUB * i + k)
            z, _, _ = _norm_mod(h_ref[pl.ds(k * TM, TM), :], g_ref[...], _sel(ic, mod_ref, 0), _sel(ic, mod_ref, 1))
            zs.append(z.astype(BF16))
        zb = jnp.concatenate(zs, axis=0)
        for dd in range(N_DEV):
            p_ref[:, dd * N_MIX_SHARD:(dd + 1) * N_MIX_SHARD] = _dot(zb, w_v[dd])

    return pl.pallas_call(
        body, name=name, grid=(t // TMX,),
        in_specs=[pl.BlockSpec((TMX, D), lambda i: (i, 0)), pl.BlockSpec((2, 3, D), lambda i: (0, 0, 0)),
                  pl.BlockSpec((1, D), lambda i: (0, 0)), ANY],
        out_specs=pl.BlockSpec((TMX, IN_PROJ), lambda i: (i, 0)),
        out_shape=jax.ShapeDtypeStruct((t, IN_PROJ), F32),
        scratch_shapes=[pltpu.VMEM((N_DEV, D, N_MIX_SHARD), BF16)],
        compiler_params=_cp(),
    )(h, mod, g, wmix)


def _halo_specs(nt, tile_of, rows=TM):
    nb = nt * (rows // HALO)
    main = pl.BlockSpec((rows, LRU_W), lambda s: (tile_of(s), 0))
    prev = pl.BlockSpec((HALO, LRU_W), lambda s: (jnp.maximum(tile_of(s) * (rows // HALO) - 1, 0), 0))
    nxt = pl.BlockSpec((HALO, LRU_W), lambda s: (jnp.minimum((tile_of(s) + 1) * (rows // HALO), nb - 1), 0))
    return main, prev, nxt


def _ext(tile, nt, main, prev, nxt):
    has_prev = jnp.logical_and(tile != 0, tile != 1)
    has_next = jnp.logical_and(tile != 0, tile != nt - 1)
    return jnp.concatenate([jnp.where(has_prev, prev, 0.0), main, jnp.where(has_next, nxt, 0.0)], axis=0)


def _shifted(ext, off):
    n = ext.shape[0]
    return pltpu.roll(ext, (-off) % n, 0)[HALO:HALO + TM]


def _conv(ext, cw_ref, cb_ref):
    xc = cb_ref[...] + cw_ref[0:1, :] * _shifted(ext, -2)
    for k in range(1, CONV_W):
        xc = xc + cw_ref[k:k + 1, :] * _shifted(ext, k - 2)
    return xc


def _log1p(y):
    return jnp.where(y < 1e-2, y * (1.0 - y * (0.5 - y * (1.0 / 3.0 - 0.25 * y))), jnp.log(1.0 + y))


def _softplus_neg(lam):
    return jnp.maximum(-lam, 0.0) + _log1p(jnp.exp(-jnp.abs(lam)))


def _one_minus_exp(x, exp_half):
    p = x * (1.0 + x * (1 / 2 + x * (1 / 6 + x * (1 / 24))))
    return jnp.where(x > -0.1, -p, 1.0 - exp_half * exp_half)


def _gates(xc, wr, br, wi, bi, lam):
    xb = xc.astype(BF16)
    r = _sigmoid(_dot(xb, wr) + br)
    ig = _sigmoid(_dot(xb, wi) + bi)
    sp = _softplus_neg(lam)
    log_a = -RG_C * r * sp
    a = jnp.exp(log_a)
    mult = jnp.sqrt(_one_minus_exp(2.0 * log_a, a))
    return r, ig, sp, a, mult


def _scan(a, b, reverse):
    n = a.shape[0]
    row = lax.broadcasted_iota(jnp.int32, a.shape, 0)
    s = 1
    while s < n:
        if s < HALO:
            if reverse:
                keep = row < n - s
                a_s = jnp.where(keep, pltpu.roll(a, n - s, 0), 1.0)
                b_s = jnp.where(keep, pltpu.roll(b, n - s, 0), 0.0)
            else:
                keep = row >= s
                a_s = jnp.where(keep, pltpu.roll(a, s, 0), 1.0)
                b_s = jnp.where(keep, pltpu.roll(b, s, 0), 0.0)
            b = a * b_s + b
            a = a * a_s
        elif reverse:
            b = jnp.concatenate([a[:n - s] * b[s:] + b[:n - s], b[n - s:]], axis=0)
            a = jnp.concatenate([a[:n - s] * a[s:], a[n - s:]], axis=0)
        else:
            b = jnp.concatenate([b[:s], a[s:] * b[:n - s] + b[s:]], axis=0)
            a = jnp.concatenate([a[:s], a[s:] * a[:n - s]], axis=0)
        s *= 2
    return a, b


def _direction_specs(d):
    return (pl.BlockSpec((None, 1, LRU_W), lambda s: (d, 0, 0)), pl.BlockSpec((None, LRU_W, LRU_W), lambda s: (d, 0, 0)))


def lru_fwd(p, conv_w, conv_b, wr, br, wi, bi, lam, reverse, name, rider=None):
    t = p.shape[0]
    nt = t // TM

    def tile_of(s):
        return jnp.where(s == 0, 0, nt - s) if reverse else s

    def body(x_ref, xp_ref, xn_ref, cw_ref, cb_ref, wr_ref, br_ref, wi_ref, bi_ref, lam_ref, h_ref, carry):
        s = pl.program_id(0)
        tile = tile_of(s)

        @pl.when(s == 0)
        def _():
            carry[...] = jnp.zeros_like(carry)

        ext = _ext(tile, nt, x_ref[...], xp_ref[...], xn_ref[...])
        xc = _conv(ext, cw_ref, cb_ref)
        _, ig, _, a, mult = _gates(xc, wr_ref[...], br_ref[...], wi_ref[...], bi_ref[...], lam_ref[...])
        a_cum, hl = _scan(a, mult * (ig * xc), reverse)
        hh = hl + a_cum * carry[...]
        h_ref[...] = hh
        carry[...] = hh[0:1, :] if reverse else hh[TM - 1:TM, :]

    main, prev, nxt = _halo_specs(nt, tile_of)
    vec = pl.BlockSpec((1, LRU_W), lambda s: (0, 0))
    dvec, dmat = _direction_specs(int(reverse))
    outs, got = _grid_call(
        body, name=name, nsteps=nt,
        in_specs=[main, prev, nxt, pl.BlockSpec((CONV_W, LRU_W), lambda s: (0, 0)), vec, dmat, dvec, dmat, dvec, dvec],
        out_specs=[main],
        out_shape=[jax.ShapeDtypeStruct((t, LRU_W), F32)],
        scratch_shapes=[pltpu.VMEM((1, LRU_W), F32)],
        args=(p, p, p, conv_w, conv_b, wr, br, wi, bi, lam), rider=rider)
    return outs[0], got


def lru_bwd(p, hs, dhs, conv_w, conv_b, wr, br, wi, bi, lam, reverse, name):
    t = p.shape[0]
    nt = t // TM
    bpt = TM // HALO

    def tile_of(s):
        return jnp.where(s == nt - 1, 0, s + 1) if reverse else nt - 1 - s

    def hprev_block(s):
        tile = tile_of(s)
        if reverse:
            return (jnp.where(tile == nt - 1, 0, jnp.minimum((tile + 1) * bpt, nt * bpt - 1)), 0)
        return (jnp.maximum(tile * bpt - 1, 0), 0)

    def body(x_ref, xp_ref, xn_ref, h_ref, hp_ref, dh_ref, cw_ref, cb_ref, wr_ref, br_ref, wi_ref, bi_ref, lam_ref,
             dxc_ref, dwr_out, dwi_out, dbr_ref, dbi_ref, dlam_ref, carry, dwr_ref, dwi_ref):
        s = pl.program_id(0)
        tile = tile_of(s)

        @pl.when(s == 0)
        def _():
            carry[...] = jnp.zeros_like(carry)
            for ref in (dwr_ref, dwi_ref, dbr_ref, dbi_ref, dlam_ref):
                ref[...] = jnp.zeros_like(ref)

        ext = _ext(tile, nt, x_ref[...], xp_ref[...], xn_ref[...])
        xc = _conv(ext, cw_ref, cb_ref)
        wr_, wi_ = wr_ref[...], wi_ref[...]
        r, ig, sp, a, mult = _gates(xc, wr_, br_ref[...], wi_, bi_ref[...], lam_ref[...])
        gated = ig * xc
        row = lax.broadcasted_iota(jnp.int32, (TM, LRU_W), 0)
        hh = h_ref[...]
        start = jnp.where(tile != 0, hp_ref[0:1, :] if reverse else hp_ref[HALO - 1:HALO, :], 0.0)
        if reverse:
            edge = row == TM - 1
            hprev = jnp.where(edge, start, pltpu.roll(hh, TM - 1, 0))
            coef = jnp.where(row == 0, 0.0, pltpu.roll(a, 1, 0))
            bb = dh_ref[...] + jnp.where(row == 0, carry[...], 0.0)
        else:
            edge = row == 0
            hprev = jnp.where(edge, start, pltpu.roll(hh, 1, 0))
            coef = jnp.where(row == TM - 1, 0.0, pltpu.roll(a, TM - 1, 0))
            bb = dh_ref[...] + jnp.where(row == TM - 1, carry[...], 0.0)
        _, lmb = _scan(coef, bb, not reverse)
        al = a * lmb
        carry[...] = al[TM - 1:TM, :] if reverse else al[0:1, :]

        dgated = lmb * mult
        dloga = (lmb * hprev) * a - (lmb * gated) * (a * a) / mult
        dpre_r = (dloga * (-RG_C * sp)) * r * (1.0 - r)
        dpre_i = (dgated * xc) * ig * (1.0 - ig)
        drb, dib = dpre_r.astype(BF16), dpre_i.astype(BF16)
        xb = xc.astype(BF16)
        dxc_ref[...] = dgated * ig + _dot_nt(drb, wr_) + _dot_nt(dib, wi_)
        dwr_ref[...] += _dot_tn(xb, drb)
        dwi_ref[...] += _dot_tn(xb, dib)
        dbr_ref[...] += jnp.sum(dpre_r, axis=0, keepdims=True)
        dbi_ref[...] += jnp.sum(dpre_i, axis=0, keepdims=True)
        dlam_ref[...] += jnp.sum(dloga * (-RG_C * r), axis=0, keepdims=True)

        @pl.when(s == nt - 1)
        def _():
            dlam_ref[...] = dlam_ref[...] * (-_sigmoid(-lam_ref[...]))
            for hd in range(HEADS):
                blk = pl.ds(hd * HEAD_DIM, HEAD_DIM)
                dwr_out[hd] = dwr_ref[blk, blk]
                dwi_out[hd] = dwi_ref[blk, blk]

    main, prev, nxt = _halo_specs(nt, tile_of)
    vec = pl.BlockSpec((1, LRU_W), lambda s: (0, 0))
    dvec, dmat = _direction_specs(int(reverse))
    heads =pl.BlockSpec((HEADS, HEAD_DIM, HEAD_DIM), lambda s: (0, 0, 0))
    vshape = jax.ShapeDtypeStruct((1, LRU_W), F32)
    hshape = jax.ShapeDtypeStruct((HEADS, HEAD_DIM, HEAD_DIM), F32)
    return pl.pallas_call(
        body, name=name, grid=(nt,),
        in_specs=[main, prev, nxt, main, pl.BlockSpec((HALO, LRU_W), hprev_block), main,
                  pl.BlockSpec((CONV_W, LRU_W), lambda s: (0, 0)), vec, dmat, dvec, dmat, dvec, dvec],
        out_specs=[main, heads, heads, vec, vec, vec],
        out_shape=[jax.ShapeDtypeStruct((t, LRU_W), F32), hshape, hshape, vshape, vshape, vshape],
        scratch_shapes=[pltpu.VMEM((1, LRU_W), F32), pltpu.VMEM((LRU_W, LRU_W), F32), pltpu.VMEM((LRU_W, LRU_W), F32)],
        compiler_params=_cp(),
    )(p, p, p, hs, hs, dhs, conv_w, conv_b, wr, br, wi, bi, lam)


GELU_C = 0.7978845608028654
GELU_A = 0.044715


def _gelu(x):
    th = jnp.tanh(GELU_C * (x + GELU_A * x * x * x))
    return 0.5 * x * (1.0 + th), th


def _sgu(v, gain, w_ref, bt_ref):
    mu = jnp.mean(v, axis=-1, keepdims=True)
    xc = v - mu
    rs = lax.rsqrt(jnp.mean(xc * xc, axis=-1, keepdims=True) + EPS)
    vhat = xc * rs
    vnb = (vhat * gain).astype(BF16)
    chunks = []
    for ch in range(TM // CHUNK):
        zs = []
        for gi in range(GROUPS):
            vb = vnb[ch * CHUNK:(ch + 1) * CHUNK, gi * GROUP_DIM:(gi + 1) * GROUP_DIM]
            zs.append(_dot(w_ref[gi].astype(BF16), vb) + bt_ref[:, gi:gi + 1])
        chunks.append(jnp.concatenate(zs, axis=1))
    return jnp.concatenate(chunks, axis=0), vhat, rs, vnb


def _pcols(k, rows=TM):
    return pl.BlockSpec((rows, LRU_W), lambda i: (i, k))


def mix_out_fwd(h, p, hf, hb, mod, sgu_g, sgu_w, sgu_bt, womix, name):
    t = h.shape[0]

    def body(h_ref, gl_ref, u_ref, v_ref, hf_ref, hb_ref, mod_ref, sg_ref, sw_ref, sb_ref, w_hbm, out_ref, o_ref, w_v):
        i = pl.program_id(0)

        @pl.when(i == 0)
        def _():
            pltpu.sync_copy(w_hbm, w_v)

        ys = []
        for k in range(SUB):
            rows = pl.ds(k * TM, TM)
            ge, _ = _gelu(gl_ref[rows, :])
            y_lru = (hf_ref[rows, :] + hb_ref[rows, :]) * ge
            z, _, _, _ = _sgu(v_ref[rows, :], sg_ref[...], sw_ref, sb_ref)
            ys.append(jnp.concatenate([y_lru, u_ref[rows, :] * z], axis=1).astype(BF16))
        o = _dot(jnp.concatenate(ys, axis=0), w_v[...])
        o_ref[...] = o
        for k in range(SUB):
            rows = pl.ds(k * TM, TM)
            out_ref[rows, :] = h_ref[rows, :] + _sel(_kind(SUB * i + k), mod_ref, 2) * o[k * TM:(k + 1) * TM]

    row = pl.BlockSpec((TMX, D), lambda i: (i, 0))
    half = pl.BlockSpec((TMX, LRU_W), lambda i: (i, 0))
    return pl.pallas_call(
        body, name=name, grid=(t // TMX,),
        in_specs=[row, _pcols(1, TMX), _pcols(2, TMX), _pcols(3, TMX), half, half, pl.BlockSpec((2, 3, D), lambda i: (0, 0, 0)),
                  pl.BlockSpec((1, MLP_W), lambda i: (0, 0)), pl.BlockSpec((GROUPS, CHUNK, CHUNK), lambda i: (0, 0, 0)),
                  pl.BlockSpec((CHUNK, GROUPS), lambda i: (0, 0)), ANY],
        out_specs=[row, row],
        out_shape=[jax.ShapeDtypeStruct((t, D), F32), jax.ShapeDtypeStruct((t, D), F32)],
        scratch_shapes=[pltpu.VMEM((D, D), BF16)],
        compiler_params=_cp(),
    )(h, p, p, p, hf, hb, mod, sgu_g, sgu_w, sgu_bt, womix)


def mix_out_bwd(dy, p, hf, hb, o, mod, sgu_g, sgu_w, sgu_bt, womix, name, rider=None):
    t = dy.shape[0]

    def body(dy_ref, gl_ref, u_ref, v_ref, hf_ref, hb_ref, o_ref, mod_ref, sg_ref, sw_ref, sb_ref, w_hbm,
             dhs_ref, dp_ref, dw_ref, dgate_ref, dsg_ref, dsw_ref, dsb_ref, w_v):
        i = pl.program_id(0)

        @pl.when(i == 0)
        def _():
            pltpu.sync_copy(w_hbm, w_v)
            for ref in (dw_ref, dgate_ref, dsg_ref, dsw_ref, dsb_ref):
                ref[...] = jnp.zeros_like(ref)

        ys, dobs = [], []
        for k in range(SUB):
            rows = pl.ds(k * TM, TM)
            ic = _kind(SUB * i + k)
            dy_ = dy_ref[rows, :]
            _acc2(dgate_ref, 0, dy_ * o_ref[rows, :], ic)
            dob = (_sel(ic, mod_ref, 2) * dy_).astype(BF16)

            gl = gl_ref[rows, :]
            ge, th = _gelu(gl)
            hsum = hf_ref[rows, :] + hb_ref[rows, :]
            gain = sg_ref[...]
            uu = u_ref[rows, :]
            z, vhat, rs, vnb = _sgu(v_ref[rows, :], gain, sw_ref, sb_ref)
            ys.append(jnp.concatenate([hsum * ge, uu * z], axis=1).astype(BF16))
            dobs.append(dob)
            dyy = _dot_nt(dob, w_v[...])
            dyl, dys = dyy[:, :LRU_W], dyy[:, LRU_W:]

            dhs_ref[rows, :] = dyl * ge
            dge = 0.5 * (1.0 + th) + 0.5 * gl * (1.0 - th * th) * (GELU_C * (1.0 + 3.0 * GELU_A * gl * gl))
            dp_ref[rows, 0:LRU_W] = (dyl * hsum * dge).astype(BF16)
            dp_ref[rows, LRU_W:2 * LRU_W] = (dys * z).astype(BF16)

            dz = dys * uu
            dzb = dz.astype(BF16)
            dvn_chunks, dsb_cols = [], [jnp.zeros((CHUNK, 1), F32)] * GROUPS
            for ch in range(TM // CHUNK):
                cols = []
                for gi in range(GROUPS):
                    rs_, cs_ = slice(ch * CHUNK, (ch + 1) * CHUNK), slice(gi * GROUP_DIM, (gi + 1) * GROUP_DIM)
                    dzg = dzb[rs_, cs_]
                    dsb_cols[gi] = dsb_cols[gi] + jnp.sum(dz[rs_, cs_], axis=1, keepdims=True)
                    dsw_ref[gi] += _dot_nt(dzg, vnb[rs_, cs_])
                    cols.append(_dot_tn(sw_ref[gi].astype(BF16), dzg))
                dvn_chunks.append(jnp.concatenate(cols, axis=1))
            dsb_ref[...] += jnp.concatenate(dsb_cols, axis=1)
            dvn = jnp.concatenate(dvn_chunks, axis=0)
            dsg_ref[...] += jnp.sum(dvn * vhat, axis=0, keepdims=True)
            dvh = dvn * gain
            dv = rs * (dvh - jnp.mean(dvh, axis=-1, keepdims=True) - vhat * jnp.mean(dvh * vhat, axis=-1, keepdims=True))
            dp_ref[rows, 2 * LRU_W:3 * LRU_W] = dv.astype(BF16)
        dw_ref[...] += _dot_tn(jnp.concatenate(ys, axis=0), jnp.concatenate(dobs, axis=0))

    row = pl.BlockSpec((TMX, D), lambda i: (i, 0))
    half = pl.BlockSpec((TMX, LRU_W), lambda i: (i, 0))
    const2 = lambda i: (0, 0)
    const3 = lambda i: (0, 0, 0)
    return _grid_call(
        body, name=name, nsteps=t // TMX,
        in_specs=[row, _pcols(1, TMX), _pcols(2, TMX), _pcols(3, TMX), half, half, row, pl.BlockSpec((2, 3, D), const3),
                  pl.BlockSpec((1, MLP_W), const2), pl.BlockSpec((GROUPS, CHUNK, CHUNK), const3),
                  pl.BlockSpec((CHUNK, GROUPS), const2), ANY],
        out_specs=[half, pl.BlockSpec((TMX, 3 * LRU_W), lambda i: (i, 0)), pl.BlockSpec((D, D), const2),
                   pl.BlockSpec((2, 1, D), const3), pl.BlockSpec((1, MLP_W), const2),
                   pl.BlockSpec((GROUPS, CHUNK, CHUNK), const3), pl.BlockSpec((CHUNK, GROUPS), const2)],
        out_shape=[jax.ShapeDtypeStruct((t, LRU_W), F32), jax.ShapeDtypeStruct((t, 3 * LRU_W), BF16),
                   jax.ShapeDtypeStruct((D, D), F32), jax.ShapeDtypeStruct((2, 1, D), F32),
                   jax.ShapeDtypeStruct((1, MLP_W), F32), jax.ShapeDtypeStruct((GROUPS, CHUNK, CHUNK), F32),
                   jax.ShapeDtypeStruct((CHUNK, GROUPS), F32)],
        scratch_shapes=[pltpu.VMEM((D, D), BF16)],
        args=(dy, p, p, p, hf, hb, o, mod, sgu_g, sgu_w, sgu_bt, womix), rider=rider)


def mix_in_bwd(dy, h, p, dxf, dxb, dprest, mod, g, conv_w, wmix, name, rider=None):
    t = dy.shape[0]
    nt = t // TM

    def body(dy_ref, h_ref, x_ref, xp_ref, xn_ref, f_ref, fp_ref, fn_ref, b_ref, bp_ref, bn_ref, dpr_ref, mod_ref,
             g_ref, cw_ref, w_hbm, dh_ref, dw_hbm, dmod_ref, dg_ref, dcw_ref, dcb_ref, w_v, dw_v):
        i = pl.program_id(0)

        @pl.when(i == 0)
        def _():
            pltpu.sync_copy(w_hbm, w_v)
            dw_v[...] = jnp.zeros_like(dw_v)
            for ref in (dmod_ref, dg_ref, dcw_ref, dcb_ref):
                ref[...] = jnp.zeros_like(ref)

        def around(ref, before, after, k):
            prev = ref[pl.ds(k * TM - HALO, HALO), :] if k > 0 else before[...]
            nxt = ref[pl.ds((k + 1) * TM, HALO), :] if k < SUB - 1 else after[...]
            return ref[pl.ds(k * TM, TM), :], prev, nxt

        gain = g_ref[...]
        zs, dps, rs = [], [], []
        for k in range(SUB):
            f_m, f_p, f_n = around(f_ref, fp_ref, fn_ref, k)
            b_m, b_p, b_n = around(b_ref, bp_ref, bn_ref, k)
            dmain = f_m + b_m
            dext = _ext(SUB * i + k, nt, dmain, f_p + b_p, f_n + b_n)
            xext = _ext(SUB * i + k, nt, *around(x_ref, xp_ref, xn_ref, k))
            dxl = cw_ref[0:1, :] * _shifted(dext, 2)
            for tap in range(1, CONV_W):
                dxl = dxl + cw_ref[tap:tap + 1, :] * _shifted(dext, 2 - tap)
            dcw_ref[...] += jnp.concatenate(
                [jnp.sum(dmain * _shifted(xext, tap - 2), axis=0, keepdims=True) for tap in range(CONV_W)], axis=0)
            dcb_ref[...] += jnp.sum(dmain, axis=0, keepdims=True)

            ic = _kind(SUB * i + k)
            rows = pl.ds(k * TM, TM)
            z, _, r = _norm_mod(h_ref[rows, :], gain, _sel(ic, mod_ref, 0), _sel(ic, mod_ref, 1))
            zs.append(z.astype(BF16))
            rs.append(r)
            dps.append(jnp.concatenate([dxl.astype(BF16), dpr_ref[rows, :]], axis=1))
        zb = jnp.concatenate(zs, axis=0)
        dpb = jnp.concatenate(dps, axis=0)
        dz_all = jnp.zeros((TMX, D), F32)
        for dd in range(N_DEV):
            dpd = dpb[:, dd * N_MIX_SHARD:(dd + 1) * N_MIX_SHARD]
            dz_all = dz_all + _dot_nt(dpd, w_v[dd])
            dw_v[dd] += _dot_tn(zb, dpd)
        for k in range(SUB):
            ic = _kind(SUB * i + k)
            rows = pl.ds(k * TM, TM)
            scale = _sel(ic, mod_ref, 1)
            dz = dz_all[k * TM:(k + 1) * TM]
            n = h_ref[rows, :] * rs[k]
            _acc2(dmod_ref, 0, dz, ic)
            _acc2(dmod_ref, 1, dz * (n * gain), ic)
            dg_ref[...] += jnp.sum(dz * (1.0 + scale) * n, axis=0, keepdims=True)
            dh_ref[rows, :] = dy_ref[rows, :] + _norm_mod_bwd(dz, n, rs[k], gain, scale)

        @pl.when(i == nt // SUB - 1)
        def _():
            pltpu.sync_copy(dw_v, dw_hbm)

    main, prev, nxt = _halo_specs(nt // SUB, lambda s: s, TMX)
    row = pl.BlockSpec((TMX, D), lambda i: (i, 0))
    const2 = lambda i: (0, 0)
    return _grid_call(
        body, name=name, nsteps=nt // SUB,
        in_specs=[row, row, main, prev, nxt, main, prev, nxt, main, prev, nxt,
                  pl.BlockSpec((TMX, 3 * LRU_W), lambda i: (i, 0)), pl.BlockSpec((2, 3, D), lambda i: (0, 0, 0)),
                  pl.BlockSpec((1, D), const2), pl.BlockSpec((CONV_W, LRU_W), const2), ANY],
        out_specs=[row, ANY, pl.BlockSpec((2, 2, D), lambda i: (0, 0, 0)), pl.BlockSpec((1, D), const2),
                   pl.BlockSpec((CONV_W, LRU_W), const2), pl.BlockSpec((1, LRU_W), const2)],
        out_shape=[jax.ShapeDtypeStruct((t, D), F32), jax.ShapeDtypeStruct((N_DEV, D, N_MIX_SHARD), F32),
                   jax.ShapeDtypeStruct((2, 2, D), F32), jax.ShapeDtypeStruct((1, D), F32),
                   jax.ShapeDtypeStruct((CONV_W, LRU_W), F32), jax.ShapeDtypeStruct((1, LRU_W), F32)],
        scratch_shapes=[pltpu.VMEM((N_DEV, D, N_MIX_SHARD), BF16), pltpu.VMEM((N_DEV, D, N_MIX_SHARD), F32)],
        args=(dy, h, p, p, p, dxf, dxf, dxf, dxb, dxb, dxb, dprest, mod, g, conv_w, wmix), rider=rider)


def _block_diag(w):
    eye = jnp.eye(HEADS, dtype=w.dtype)
    return jnp.einsum("dhij,hk->dhikj", w, eye).reshape(2, LRU_W, LRU_W)


def small_layer(g1, gm, g2, conv_w, conv_b, w_r, b_r, w_i, b_i, lam, sgu_g, sgu_w, sgu_b):
    return dict(g1=g1[None, :], gm=gm[None, :], g2=g2[None, :], conv_w=conv_w, conv_b=conv_b[None, :],
                wr=_block_diag(w_r).astype(BF16), br=b_r[:, None, :], wi=_block_diag(w_i).astype(BF16),
                bi=b_i[:, None, :], lam=lam[:, None, :], sgu_g=sgu_g[None, :], sgu_w=sgu_w, sgu_bt=sgu_b.T)


def small_grads(g):
    out = dict(mix_norm_g=g["gm"][0], ffn2_norm_g=g["g2"][0], lru_conv_w=g["conv_w"],
               lru_conv_b=g["conv_b"][0], lru_w_r=g["wr"], lru_b_r=g["br"][:, 0, :],
               lru_w_i=g["wi"], lru_b_i=g["bi"][:, 0, :], lru_lambda=g["lam"][:, 0, :],
               sgu_norm_g=g["sgu_g"][0], sgu_w=g["sgu_w"], sgu_b=g["sgu_bt"].T)
    if "g1" in g:
        out["ffn1_norm_g"] = g["g1"][0]
    return out


def _as_blocks(key, g):
    return g if key == "wmix" else g.reshape(N_DEV, g.shape[0] // N_DEV, D)


def _gathered(key, a):
    return a if key == "wmix" else a.reshape(N_DEV * a.shape[1], D)


class _ReduceScatter:
    def __init__(self, c_idx, where):
        self.c_idx, self.where = c_idx, where
        self.out = {}

    def pair(self, group):
        return pair_rider([g for _, g in group])

    def after_pair(self, group, recv1, tag):
        return chips_rider(pair_sum([g for _, g in group], list(recv1), self.c_idx, f"pair_sum_{tag}"))

    def after_chips(self, group, recv1, recv2, tag):
        for (key, g), r1, r2 in zip(group, recv1, recv2):
            self.out[key] = (g, r1, r2)


FIRST_WEIGHTS = [(0, "win1"), (0, "wout1")]


def fwd_bwd(ctx_rows, x_rows, target, mods, shards, first_weights, smalls, final_g, c_idx, where):
    assert CTX == TM and len(shards) == 2

    def gather(keys_by_layer):
        return GatherRider([shards[l][k] for l, k in keys_by_layer])

    def put(full, keys_by_layer, got):
        for (l, k), a in zip(keys_by_layer, got):
            full[l][k] = _gathered(k, a)

    full = [dict(s) for s in smalls]
    put(full, FIRST_WEIGHTS, first_weights)
    riders = {
        "ffn1_fwd_0": [(0, "wmix"), (0, "womix"), (0, "win2")],
        "lru_fwd_0_0": [(0, "wout2")],
        "ffn2_fwd_0": [(1, "win1"), (1, "wout1")],
        "ffn1_fwd_1": [(1, "wmix"), (1, "womix"), (1, "win2")],
        "lru_fwd_1_0": [(1, "wout2")],
    }

    def ffn(which, l, h):
        name = f"ffn{which}_fwd_{l}"
        w = full[l]
        keys = riders.get(name)
        m = mods[l][:, 0:3] if which == 1 else mods[l][:, 6:9]
        last = (which, l) == (2, 1)
        outs, got = ffn_fwd(h, m, w[f"g{which}"], w[f"win{which}"], w[f"wout{which}"], name,
                            rider=gather(keys) if keys else None, loss=(final_g, target) if last else None)
        if keys:
            put(full, keys, got)
        return outs

    saved = []
    h = (ctx_rows, x_rows)
    for l in range(2):
        mm = mods[l][:, 3:6]
        outs = ffn(1, l, h)
        h1, gu1, acc1 = outs[:3]
        hin = outs[3] if l == 0 else h
        w = full[l]
        p = mix_in_fwd(h1, mm, w["gm"], w["wmix"], f"mix_in_fwd_{l}")
        hs = []
        for d in range(2):
            keys = riders.get(f"lru_fwd_{l}_{d}")
            hd, got = lru_fwd(p, w["conv_w"], w["conv_b"], w["wr"], w["br"], w["wi"], w["bi"], w["lam"],
                              bool(d), f"lru_fwd_{l}_{d}", rider=gather(keys) if keys else None)
            if keys:
                put(full, keys, got)
            hs.append(hd)
        h2, o = mix_out_fwd(h1, p, hs[0], hs[1], mm, w["sgu_g"], w["sgu_w"], w["sgu_bt"], w["womix"], f"mix_out_fwd_{l}")
        outs = ffn(2, l, h2)
        h, gu2, acc2 = outs[:3]
        saved.append((hin, h1, h2, gu1, acc1, p, hs, o, gu2, acc2))
    dh, (loss, dgf) = h, outs[3:]

    rs = _ReduceScatter(c_idx, where)
    grads, dmods, sums = [None, None], [None, None], [None, None]
    pending = None
    for l in (1, 0):
        w = full[l]
        m1, mm, m2 = mods[l][:, 0:3], mods[l][:, 3:6], mods[l][:, 6:9]
        hin, h1, h2, gu1, acc1, p, hs, o, gu2, acc2 = saved[l]
        g = {}
        rs.out = {}
        both = Riders([rs.pair(pending[0]), GatherRider([small_pack])]) if pending else None
        (dp2, g["wout2"], dgate2), got = ffn_bwd_a(dh, acc2, gu2, m2, w["wout2"], f"ffn2_bwd_a_{l}", rider=both)
        if pending:
            r1, (small_all,) = both.split(got)
        chips = rs.after_pair(pending[0], r1, pending[1]) if pending else None
        (dh, g["win2"], dmod2, g["g2"]), r2 = ffn_bwd_b(dh, h2, dp2, m2, w["g2"], w["win2"], f"ffn2_bwd_b_{l}", rider=chips)
        if pending:
            rs.after_chips(pending[0], r1, r2, pending[1])
            sums[l + 1].update(rs.out)
            rs.out = {}

        grp = [(k, _as_blocks(k, g[k])) for k in ("win2", "wout2")]
        (dhs, dprest, g["womix"], dgatem, g["sgu_g"], g["sgu_w"], g["sgu_bt"]), r1 = mix_out_bwd(
            dh, p, hs[0], hs[1], o, mm, w["sgu_g"], w["sgu_w"], w["sgu_bt"], w["womix"], f"mix_out_bwd_{l}",
            rider=rs.pair(grp))
        chips = rs.after_pair(grp, r1, f"a{l}")
        dx, per_dir = [], []
        for d in range(2):
            out = lru_bwd(p, hs[d], dhs, w["conv_w"], w["conv_b"], w["wr"], w["br"], w["wi"], w["bi"], w["lam"],
                          bool(d), f"lru_bwd_{l}_{d}")
            dx.append(out[0])
            per_dir.append(out[1:])
        for k, nm in enumerate(("wr", "wi", "br", "bi", "lam")):
            g[nm] = jnp.stack([per_dir[0][k], per_dir[1][k]])
        (dh, g["wmix"], dmodm, g["gm"], g["conv_w"], g["conv_b"]), r2 = mix_in_bwd(
            dh, h1, p, dx[0], dx[1], dprest, mm, w["gm"], w["conv_w"], w["wmix"], f"mix_in_bwd_{l}", rider=chips)
        rs.after_chips(grp, r1, r2, f"a{l}")
        sums[l] = dict(rs.out)
        rs.out = {}

        if l == 1:
            (dp1, g["wout1"], dgate1), _ = ffn_bwd_a(dh, acc1, gu1, m1, w["wout1"], f"ffn1_bwd_a_{l}")
            (dh, g["win1"], dmod1, g["g1"]), _ = ffn_bwd_b(dh, hin, dp1, m1, w["g1"], w["win1"], f"ffn1_bwd_b_{l}")
            pending = ([(k, _as_blocks(k, g[k])) for k in ("womix", "wmix", "wout1", "win1")], f"b{l}")
            per = small_grads(g)
            small_pack = _pack([per[n] for n in LAYER_SMALL])
        else:
            g_mix = [(k, _as_blocks(k, g[k])) for k in ("womix", "wmix")]
            (dp1, g["wout1"], dgate1), r1_mix = ffn_bwd_a(dh, acc1, gu1, m1, w["wout1"], f"ffn1_bwd_a_{l}",
                                                          rider=rs.pair(g_mix))
            g_out = [("wout1", _as_blocks("wout1", g["wout1"]))]
            per = small_grads(g)
            three = Riders([rs.after_pair(g_mix, r1_mix, f"b{l}"), rs.pair(g_out),
                            GatherRider([_pack([per[n] for n in LAYER_SMALL[1:]])])])
            (g["win1"],), got = ffn_bwd_dw(hin, dp1, m1, w["g1"], f"ffn1_bwd_dw_{l}", rider=three)
            r2_mix, r1_out, (small0_all,) = three.split(got)
            rs.after_chips(g_mix, r1_mix, r2_mix, f"b{l}")
            g_in = [("win1", _as_blocks("win1", g["win1"]))]
            both = Riders([rs.after_pair(g_out, r1_out, f"c{l}"), rs.pair(g_in)])
            nb = dh.shape[0] // TMX
            cut = max((3 * nb) // 8, 1)
            part, got = ffn_bwd_dh(dh, hin, dp1, m1, w["g1"], w["win1"], f"ffn1_bwd_dh0_{l}", (0, cut), rider=both)
            r2_out, r1_in = both.split(got)
            rs.after_chips(g_out, r1_out, r2_out, f"c{l}")
            (dh, dmod1, g["g1"]), r2_in = ffn_bwd_dh(dh, hin, dp1, m1, w["g1"], w["win1"], f"ffn1_bwd_dh1_{l}", (cut, nb),
                                                     carry=part, rider=rs.after_pair(g_in, r1_in, f"d{l}"))
            rs.after_chips(g_in, r1_in, r2_in, f"d{l}")
            dh = dh[CTX:]
            sums[l].update(rs.out)
        dmods[l] = jnp.concatenate([dmod1, dgate1, dmodm, dgatem, dmod2, dgate2], axis=1)
        grads[l] = g
    return loss, dh, jnp.stack(dmods), grads, (small0_all, small_all), sums, dgf


def _row_block(r, c, limit=262144):
    best = 8
    for rb in range(8, r + 1, 8):
        if r % rb == 0 and rb * c <= limit:
            best = rb
    return best


PAIR_SUM_SPLIT = 2


def pair_sum(grads, recv, c_idx, name):
    n = len(grads)

    def body(c_ref, *refs):
        for t in range(n):
            refs[2 * n + t][...] = (refs[t][...] + refs[n + t][...]).astype(BF16)

    mine, theirs, outs = [], [], []
    for g in grads:
        _, r, c = g.shape
        rb = r // PAIR_SUM_SPLIT
        assert rb % 16 == 0
        mine.append(pl.BlockSpec((1, rb, c), lambda j, i, c_ref: (2 * j + c_ref[0], i, 0)))
        theirs.append(pl.BlockSpec((1, rb, c), lambda j, i, c_ref: (j, i, 0)))
        outs.append(jax.ShapeDtypeStruct((4, r, c), BF16))
    return pl.pallas_call(
        body, name=name,
        grid_spec=pltpu.PrefetchScalarGridSpec(num_scalar_prefetch=1, grid=(4, PAIR_SUM_SPLIT), in_specs=mine + theirs,
                                               out_specs=list(theirs)),
        out_shape=outs, compiler_params=_cp(2),
    )(c_idx, *grads, *recv)


ADA_ROWS = 16


def _silu(v):
    return v * _sigmoid(v)


def ada_fwd(cond, w_ada, b_slab, name):
    def body(c_ref, w_ref, b_ref, o_ref):
        s = _silu(c_ref[...]).astype(BF16)
        o_ref[0] = _dot(s, w_ref[0].astype(BF16)) + b_ref[0]

    return pl.pallas_call(
        body, name=name, grid=(DEPTH,),
        in_specs=[pl.BlockSpec((ADA_ROWS, D), lambda l: (0, 0)), pl.BlockSpec((1, D, ADA_SHARD), lambda l: (l, 0, 0)),
                  pl.BlockSpec((1, 1, ADA_SHARD), lambda l: (l, 0, 0))],
        out_specs=pl.BlockSpec((1, ADA_ROWS, ADA_SHARD), lambda l: (l, 0, 0)),
        out_shape=jax.ShapeDtypeStruct((DEPTH, ADA_ROWS, ADA_SHARD), F32),
        compiler_params=_cp(),
    )(cond, w_ada, b_slab)


def ada_bwd(cond, dm_sample, dm_ctx, w_ada, name):
    def body(c_ref, ds_ref, dc_ref, w_ref, gw_ref, dsc_ref):
        @pl.when(pl.program_id(0) == 0)
        def _():
            dsc_ref[...] = jnp.zeros_like(dsc_ref)

        s = _silu(c_ref[...]).astype(BF16)
        dcs = dc_ref[0]
        tot = dcs[0:1]
        for j in range(1, N_DEV):
            tot = tot + dcs[j:j + 1]
        tot8 = jnp.where(lax.broadcasted_iota(jnp.int32, (N_DEV, ADA_SHARD), 0) == 0, tot, 0.0)
        dm = jnp.concatenate([ds_ref[0], tot8], axis=0).astype(BF16)
        gw_ref[0] = _dot_tn(s, dm)
        dsc_ref[...] += _dot_nt(dm, w_ref[0].astype(BF16))[N_DEV:N_DEV + 1]

    slab = pl.BlockSpec((1, N_DEV, ADA_SHARD), lambda l: (l, 0, 0))
    wspec = pl.BlockSpec((1, D, ADA_SHARD), lambda l: (l, 0, 0))
    return pl.pallas_call(
        body, name=name, grid=(DEPTH,),
        in_specs=[pl.BlockSpec((ADA_ROWS, D), lambda l: (0, 0)), slab, slab, wspec],
        out_specs=[wspec, pl.BlockSpec((1, D), lambda l: (0, 0))],
        out_shape=[jax.ShapeDtypeStruct((DEPTH, D, ADA_SHARD), F32), jax.ShapeDtypeStruct((1, D), F32)],
        compiler_params=_cp(),
    )(cond, dm_sample, dm_ctx, w_ada)


def sum_over_devices(parts, name, silu_rows=0, w=None):
    _, r, c = parts.shape

    def body(*refs):
        p_ref, o_ref = refs[0], refs[-1]
        tot = p_ref[0]
        for j in range(1, N_DEV):
            tot = tot + p_ref[j]
        o_ref[...] = tot
        if silu_rows:
            wv = refs[1][...]
            s = _sigmoid(wv)
            o_ref[0:silu_rows, :] = tot[0:silu_rows, :] * (s * (1.0 + wv * (1.0 - s)))

    vm = pl.BlockSpec(memory_space=pltpu.VMEM)
    args = (parts,) if w is None else (parts, w)
    return pl.pallas_call(
        body, name=name, in_specs=[vm] * len(args), out_specs=vm,
        out_shape=jax.ShapeDtypeStruct((r, c), F32),
        compiler_params=pltpu.CompilerParams(vmem_limit_bytes=VMEM_LIMIT),
    )(*args)


def sum_dmods(dm_all, name):
    def body(d_ref, o_ref):
        for l in range(DEPTH):
            tot = d_ref[0, l]
            for j in range(1, N_DEV):
                tot = tot + d_ref[j, l]
            o_ref[l:l + 1, :] = tot[0:1] + tot[1:2]

    vm = pl.BlockSpec(memory_space=pltpu.VMEM)
    return pl.pallas_call(
        body, name=name, in_specs=[vm], out_specs=vm,
        out_shape=jax.ShapeDtypeStruct((DEPTH, N_MOD * D), F32),
    )(dm_all)


ADAMW_BLOCK = 512 * 1024


def adamw(w, g, m, v, name, rider=None):
    r, c = w.shape
    rb = _row_block(r, c, limit=ADAMW_BLOCK)

    def body(w_ref, g_ref, m_ref, v_ref, d_ref, nm_ref, nv_ref):
        g_ = g_ref[...]
        nm = B1 * m_ref[...] + (1.0 - B1) * g_
        nv = B2 * v_ref[...] + (1.0 - B2) * (g_ * g_)
        nm_ref[...] = nm
        nv_ref[...] = nv
        m_hat = nm / (1.0 - B1 ** STEP)
        v_hat = nv / (1.0 - B2 ** STEP)
        d_ref[...] = -LR * (m_hat / (jnp.sqrt(v_hat) + ADAM_EPS) + WD * w_ref[...])

    blk = pl.BlockSpec((rb, c), lambda i: (i, 0))
    shp = jax.ShapeDtypeStruct((r, c), F32)
    return _grid_call(body, name=name, nsteps=r // rb, in_specs=[blk] * 4, out_specs=[blk] * 3, out_shape=[shp] * 3,
                      scratch_shapes=[], args=(w, g, m, v), rider=rider)


def _adamw_math(w, g, m, v):
    nm = B1 * m + (1.0 - B1) * g
    nv = B2 * v + (1.0 - B2) * (g * g)
    m_hat = nm / (1.0 - B1 ** STEP)
    v_hat = nv / (1.0 - B2 ** STEP)
    return -LR * (m_hat / (jnp.sqrt(v_hat) + ADAM_EPS) + WD * w), nm, nv


def adamw_layers(w, parts, m, v, where, name):
    _, r, c = w.shape
    assert len(parts) == DEPTH == 2 and parts[0][0].shape == (N_DEV, r, c)
    rb = _row_block(r, c, limit=ADAMW_BLOCK // 2)
    nb = r // rb

    def body(where_ref, w_ref, a0, b0, c0, a1, b1, c1, m_ref, v_ref, go_ref, d_ref, nm_ref, nv_ref):
        def total(mine, pair, far):
            return (mine[0] + pair[0]) + ((far[0].astype(F32) + far[1].astype(F32)) + far[2].astype(F32))

        g = jnp.where(pl.program_id(0) == 0, total(a0, b0, c0), total(a1, b1, c1))
        go_ref[0], d_ref[0], nm_ref[0], nv_ref[0] = (g,) + _adamw_math(w_ref[0], g, m_ref[0], v_ref[0])

    blk = pl.BlockSpec((1, rb, c), lambda l, i, wr: (l, i, 0))

    def layer_specs(layer):
        row = (lambda l, i: jnp.where(l == 0, i, nb - 1)) if layer == 0 else (lambda l, i: jnp.where(l == 0, 0, i))
        return [pl.BlockSpec((1, rb, c), lambda l, i, wr: (wr[0], row(l, i), 0)),
                pl.BlockSpec((1, rb, c), lambda l, i, wr: (wr[1], row(l, i), 0)),
                pl.BlockSpec((3, rb, c), lambda l, i, wr: (0, row(l, i), 0))]

    shp = jax.ShapeDtypeStruct(w.shape, F32)
    return pl.pallas_call(
        body, name=name,
        grid_spec=pltpu.PrefetchScalarGridSpec(
            num_scalar_prefetch=1, grid=(DEPTH, nb),
            in_specs=[blk] + layer_specs(0) + layer_specs(1) + [blk, blk], out_specs=[blk] * 4),
        out_shape=[shp] * 4, compiler_params=_cp(2),
    )(where, w, *parts[0], *parts[1], m, v)


def _rows(a):
    return a.reshape(-1, a.shape[-1])


def adamw_many(groups, name):
    ws, gs, ms, vs = groups
    n = len(ws)

    def body(*refs):
        for i in range(n):
            d, nm, nv = _adamw_math(refs[i][...], refs[n + i][...], refs[2 * n + i][...], refs[3 * n + i][...])
            refs[4 * n + i][...] = d
            refs[5 * n + i][...] = nm
            refs[6 * n + i][...] = nv

    vm = pl.BlockSpec(memory_space=pltpu.VMEM)
    shapes = [jax.ShapeDtypeStruct(w.shape, F32) for w in ws]
    outs = pl.pallas_call(
        body, name=name, in_specs=[vm] * (4 * n), out_specs=[vm] * (3 * n), out_shape=shapes * 3,
        compiler_params=pltpu.CompilerParams(vmem_limit_bytes=VMEM_LIMIT),
    )(*ws, *gs, *ms, *vs)
    return outs[:n], outs[n:2 * n], outs[2 * n:]


def _adamw_nd(w, g, m, v, name, rider=None):
    shape = w.shape
    flat = lambda a: a.reshape(-1, shape[-1])
    outs, got = adamw(flat(w), flat(g), flat(m), flat(v), name, rider=rider)
    return tuple(o.reshape(shape) for o in outs), got


LANES = 128


PACK_UNIT = 8 * LANES


ADAMW_SMALL_ROWS = 512


def _pack(arrays, row_multiple=8):
    pieces, n = [], 0
    for a in arrays:
        pieces.append(a.reshape(-1).astype(F32))
        pad = (-a.size) % PACK_UNIT
        if pad:
            pieces.append(jnp.zeros((pad,), F32))
        n += a.size + pad
    tail = (-n) % (row_multiple * LANES)
    if tail:
        pieces.append(jnp.zeros((tail,), F32))
    return jnp.concatenate(pieces).reshape(-1, LANES)


def _unpack(packed, shapes):
    out, r0 = [], 0
    lead = packed.shape[:-2]
    for shp in shapes:
        size = 1
        for s in shp:
            size *= s
        nr = 8 * -(-size // PACK_UNIT)
        blk = packed[..., r0:r0 + nr, :].reshape(lead + (nr * LANES,))[..., :size]
        out.append(blk.reshape(lead + tuple(shp)))
        r0 += nr
    return out


WEIGHTS = ["c_ctx", "w_ada", "b_ada", "ffn1_norm_g", "ffn1_w_in", "ffn1_w_out", "mix_norm_g", "w_in_mix", "lru_conv_w",
           "lru_conv_b", "lru_w_r", "lru_b_r", "lru_w_i", "lru_b_i", "lru_lambda", "sgu_norm_g", "sgu_w", "sgu_b",
           "w_out_mix", "ffn2_norm_g", "ffn2_w_in", "ffn2_w_out", "final_norm_g"]
BIG = ["w_ada", "ffn1_w_in", "ffn1_w_out", "w_in_mix", "w_out_mix", "ffn2_w_in", "ffn2_w_out"]
SHARDED_SMALL = ["lru_conv_w", "lru_b_r", "lru_b_i", "lru_lambda"]
LAYER_SMALL = ["ffn1_norm_g", "mix_norm_g", "ffn2_norm_g", "lru_conv_w", "lru_conv_b", "lru_w_r", "lru_b_r", "lru_w_i",
               "lru_b_i", "lru_lambda", "sgu_norm_g", "sgu_w", "sgu_b"]
LRU_SHARD = LRU_W // N_DEV


def _widen(a):
    return jnp.moveaxis(a, 0, -2).reshape(a.shape[1:-1] + (LRU_W,))


def kernel(x, c, ctx, c_ctx, w_ada, b_ada, ffn1_norm_g, ffn1_w_in, ffn1_w_out, mix_norm_g, w_in_mix, lru_conv_w, lru_conv_b, lru_w_r, lru_b_r, lru_w_i, lru_b_i, lru_lambda, sgu_norm_g, sgu_w, sgu_b, w_out_mix, ffn2_norm_g, ffn2_w_in, ffn2_w_out, final_norm_g, loss_target, m_c_ctx, m_w_ada, m_b_ada, m_ffn1_norm_g, m_ffn1_w_in, m_ffn1_w_out, m_mix_norm_g, m_w_in_mix, m_lru_conv_w, m_lru_conv_b, m_lru_w_r, m_lru_b_r, m_lru_w_i, m_lru_b_i, m_lru_lambda, m_sgu_norm_g, m_sgu_w, m_sgu_b, m_w_out_mix, m_ffn2_norm_g, m_ffn2_w_in, m_ffn2_w_out, m_final_norm_g, v_c_ctx, v_w_ada, v_b_ada, v_ffn1_norm_g, v_ffn1_w_in, v_ffn1_w_out, v_mix_norm_g, v_w_in_mix, v_lru_conv_w, v_lru_conv_b, v_lru_w_r, v_lru_b_r, v_lru_w_i, v_lru_b_i, v_lru_lambda, v_sgu_norm_g, v_sgu_w, v_sgu_b, v_w_out_mix, v_ffn2_norm_g, v_ffn2_w_in, v_ffn2_w_out, v_final_norm_g):
    given = dict(locals())
    W = {n: given[n] for n in WEIGHTS}
    M = {n: given["m_" + n] for n in WEIGHTS}
    V = {n: given["v_" + n] for n in WEIGHTS}
    xi, yi, ci = _position()
    me = 4 * xi + 2 * yi + ci
    chip = 2 * xi + yi

    shards = []
    tr = lambda a: jnp.swapaxes(a, 1, 2)
    for l in range(DEPTH):
        sh = dict(win1=tr(ffn1_w_in)[l], wout1=ffn1_w_out[l], wmix=w_in_mix[l], womix=w_out_mix[l], win2=tr(ffn2_w_in)[l],
                  wout2=ffn2_w_out[l])
        shards.append({k: a.astype(BF16) for k, a in sh.items()})

    sharded_shapes = [W[n].shape for n in SHARDED_SMALL]
    both = Riders([GatherRider([_pack([c[0]] + [W[n] for n in SHARDED_SMALL])]),
                   GatherRider([shards[0][k] for _, k in FIRST_WEIGHTS])])
    (got,), first_weights = both.split(run_alone(both, pl.ANY, "gather_first"))
    parts = _unpack(got, [(D,)] + sharded_shapes)
    c_all = parts[0]
    wide = {n: _widen(a) for n, a in zip(SHARDED_SMALL, parts[1:])}
    cond = jnp.concatenate([c_all, c_ctx[None, :], jnp.zeros((ADA_ROWS - N_DEV - 1, D), F32)], axis=0)
    b_slab = lax.dynamic_slice_in_dim(b_ada, me * ADA_SHARD, ADA_SHARD, axis=1)[:, None, :]
    slabs = ada_fwd(cond, w_ada, b_slab, "ada_fwd")
    mall = run_alone(GatherRider([slabs.reshape(DEPTH * ADA_ROWS, ADA_SHARD)]), pltpu.VMEM, "gather_mod")[0]
    mall = mall.reshape(N_DEV, DEPTH, ADA_ROWS, ADA_SHARD)
    m_sample = lax.dynamic_index_in_dim(mall, me, axis=2, keepdims=False)
    m_ctx = mall[:, :, N_DEV, :]
    mods = jnp.stack([jnp.transpose(m, (1, 0, 2)).reshape(DEPTH, N_MOD, D) for m in (m_ctx, m_sample)], axis=1)

    smalls = []
    for l in range(DEPTH):
        smalls.append(small_layer(ffn1_norm_g[l], mix_norm_g[l], ffn2_norm_g[l], wide["lru_conv_w"][l], lru_conv_b[l],
                                  lru_w_r[l], wide["lru_b_r"][l], lru_w_i[l], wide["lru_b_i"][l], wide["lru_lambda"][l],
                                  sgu_norm_g[l], sgu_w[l], sgu_b[l]))

    c_idx = ci.reshape(1).astype(jnp.int32)
    where = jnp.stack([me, chip]).astype(jnp.int32)
    loss_blk, dx, dmods, grads, (small0_all, small1_all), gsum, dgf = fwd_bwd(
        ctx[0], x[0], loss_target[0], mods, shards, first_weights, smalls, final_norm_g[None, :], c_idx, where)
    smalls_shape = {n: (W[n].shape[1:-1] + (LRU_W,)) if n in SHARDED_SMALL else W[n].shape[1:] for n in LAYER_SMALL}
    G, delta, new_m, new_v = {}, {}, {}, {}
    for key, n in (("win1", "ffn1_w_in"), ("wout1", "ffn1_w_out"), ("wmix", "w_in_mix"), ("womix", "w_out_mix"),
                   ("win2", "ffn2_w_in"), ("wout2", "ffn2_w_out")):
        t_in = tr if key in ("win1", "win2") else (lambda a: a)
        outs = adamw_layers(t_in(W[n]), [gsum[l][key] for l in range(DEPTH)], t_in(M[n]), t_in(V[n]), where,
                            f"adamw_{n}")
        G[n], delta[n], new_m[n], new_v[n] = [t_in(o) for o in outs]

    n_rows = DEPTH * 2 * N_MOD
    dm_rows = jnp.concatenate([dmods.reshape(n_rows, D), jnp.zeros((-n_rows % 8, D), F32)], axis=0)
    dm_all = run_alone(GatherRider([dm_rows]), pltpu.VMEM, "gather_dmod")[0][:, :n_rows]
    dm_all = dm_all.reshape(N_DEV, DEPTH, 2, N_MOD * D)
    mine = lax.dynamic_slice_in_dim(dm_all, me * ADA_SHARD, ADA_SHARD, axis=3)
    G["w_ada"], dsc = ada_bwd(cond, jnp.transpose(mine[:, :, 1, :], (1, 0, 2)), jnp.transpose(mine[:, :, 0, :], (1, 0, 2)),
                              w_ada, "ada_bwd")
    G["b_ada"] = sum_dmods(dm_all, "sum_dmods")
    (delta["w_ada"], new_m["w_ada"], new_v["w_ada"]), _ = _adamw_nd(w_ada, G["w_ada"], m_w_ada, v_w_ada, "adamw_w_ada")

    head_all = run_alone(GatherRider([_pack([dsc[0], dgf[0], grads[0]["g1"][0], loss_blk])]), pltpu.VMEM,
                         "gather_head_grads")[0]
    head = _unpack(sum_over_devices(head_all, "sum_head_grads", silu_rows=D // LANES, w=c_ctx.reshape(D // LANES, LANES)),
                   [(D,), (D,), (D,), loss_blk.shape])
    shapes = [smalls_shape[n] for n in LAYER_SMALL]
    sum0 = [head[2]] + _unpack(sum_over_devices(small0_all, "sum_small_grads_0"), shapes[1:])
    sum1 = _unpack(sum_over_devices(small1_all, "sum_small_grads_1"), shapes)
    G["c_ctx"], G["final_norm_g"] = head[0], head[1]
    for n, a0, a1 in zip(LAYER_SMALL, sum0, sum1):
        a = jnp.stack([a0, a1])
        G[n] = lax.dynamic_slice_in_dim(a, me * LRU_SHARD, LRU_SHARD, axis=a.ndim - 1) if n in SHARDED_SMALL else a

    rest = [n for n in WEIGHTS if n not in BIG]
    outs = adamw_many([[_rows(src[n]) for n in rest] for src in (W, G, M, V)], "adamw_small")
    for dst, arrays in zip((delta, new_m, new_v), outs):
        for n, a in zip(rest, arrays):
            dst[n] = a.reshape(W[n].shape)

    loss = head[3][0, 0]
    grad_x = dx[None]
    return (loss, grad_x, *[G[n] for n in WEIGHTS], *[delta[n] for n in WEIGHTS], *[new_m[n] for n in WEIGHTS],
            *[new_v[n] for n in WEIGHTS])
```
